```python
import jax, jax.numpy as jnp
from jax import lax
import numpy as np

D_MODEL = 1024
BATCH = 8
SEQ = 8192
DEPTH = 2

CHUNK = 128
A_GROUPS = 8
A_WIDTH = 512
A_HEAD = A_WIDTH // A_GROUPS
B_WIDTH = 512
CONV_WIDTH = 3
C_WIDTH = 512
POOL_WINDOWS = (2, 4, 8, 16)
C_GROUP = C_WIDTH // len(POOL_WINDOWS)
IN_TOTAL = 3 * A_WIDTH + 4 * B_WIDTH + 2 * C_WIDTH + 3 * D_MODEL
RMS_EPS = 1e-6
LN_EPS = 1e-5

kernel_name = "hybrid_gmlp_shortconv_pool_gated_merge"


def _rmsnorm(x, g):
    xf = x.astype(jnp.float32)
    y = xf * lax.rsqrt(jnp.mean(xf * xf, axis=-1, keepdims=True) + RMS_EPS)
    return (y * g.astype(jnp.float32)).astype(x.dtype)


def _layernorm(x, g, b):
    xf = x.astype(jnp.float32)
    mu = jnp.mean(xf, axis=-1, keepdims=True)
    xc = xf - mu
    var = jnp.mean(xc * xc, axis=-1, keepdims=True)
    y = xc * lax.rsqrt(var + LN_EPS)
    return (y * g.astype(jnp.float32) + b.astype(jnp.float32)).astype(x.dtype)


def _split_points():
    widths = [A_WIDTH] * 3 + [B_WIDTH] * 4 + [C_WIDTH] * 2 + [D_MODEL] * 3
    return [int(s) for s in np.cumsum(widths)[:-1]]


def _gmlp_branch(u, v, ln_g, ln_b, w_s, b_s):
    u = jax.nn.gelu(u)
    v = _layernorm(jax.nn.gelu(v), ln_g, ln_b)
    bsz, s, _ = v.shape
    vc = v.reshape(bsz, s // CHUNK, CHUNK, A_GROUPS, A_HEAD)
    causal = jnp.tril(jnp.ones((CHUNK, CHUNK), dtype=bool))
    w_m = jnp.where(causal, w_s, 0.0)
    sg = jnp.einsum('gts,bnsgc->bntgc', w_m, vc) + b_s.T[:, :, None]
    return u * sg.reshape(bsz, s, A_WIDTH)


def _shortconv_branch(xb, bg, cg, conv_w, conv_b):
    y = cg * xb
    y = lax.conv_general_dilated(
        y, conv_w[:, None, :].astype(y.dtype), window_strides=(1,),
        padding=[(CONV_WIDTH - 1, 0)], dimension_numbers=('NWC', 'WIO', 'NWC'),
        feature_group_count=B_WIDTH) + conv_b
    return bg * y


def _pool_branch(xc, w_pool, pool_scale):
    bsz, s, _ = xc.shape
    xf = xc.astype(jnp.float32).reshape(bsz, s, len(POOL_WINDOWS), C_GROUP)
    cs = jnp.cumsum(xf, axis=1)
    t_count = jnp.arange(1, s + 1, dtype=jnp.float32)
    pooled = []
    for gi, w in enumerate(POOL_WINDOWS):
        c = cs[:, :, gi]
        lag = jnp.pad(c[:, :s - w], ((0, 0), (w, 0), (0, 0)))
        cnt = jnp.minimum(t_count, float(w))[None, :, None]
        pooled.append((c - lag) / cnt)
    pooled = (jnp.stack(pooled, axis=2) - xf).astype(xc.dtype)
    y = jnp.einsum('bsgc,gcd->bsgd', pooled, w_pool).reshape(bsz, s, C_WIDTH)
    return y * pool_scale


def _hybrid_layer(x, norm_g, w_in, ln_g, ln_b, w_s, b_s, conv_w, conv_b,
                  w_pool, pool_scale, w_pa, w_pb, w_pc, w_o):
    h = _rmsnorm(x, norm_g)
    p = h @ w_in
    (u, v, z_a, x_b, b_g, c_g, z_b, x_c, z_c,
     g_a, g_b, g_c) = jnp.split(p, _split_points(), axis=-1)
    y_a = (_gmlp_branch(u, v, ln_g, ln_b, w_s, b_s) * jax.nn.silu(z_a)) @ w_pa
    y_b = (_shortconv_branch(x_b, b_g, c_g, conv_w, conv_b) * jax.nn.silu(z_b)) @ w_pb
    y_c = (_pool_branch(x_c, w_pool, pool_scale) * jax.nn.silu(z_c)) @ w_pc
    merged = (jax.nn.sigmoid(g_a) * y_a + jax.nn.sigmoid(g_b) * y_b
              + jax.nn.sigmoid(g_c) * y_c)
    return x + merged @ w_o


def _fwd_setup_inputs(seed: int = 0) -> dict:
    key = jax.random.key(seed)
    ks = jax.random.split(key, 20)
    f32 = jnp.float32
    n = lambda k, shape: jax.random.normal(k, shape, dtype=f32)
    return {
        "x": n(ks[0], (BATCH, SEQ, D_MODEL)),
        "norm_g": 1.0 + 0.02 * n(ks[1], (DEPTH, D_MODEL)),
        "w_in": n(ks[2], (DEPTH, D_MODEL, IN_TOTAL)) * D_MODEL ** -0.5,
        "ln_g": 1.0 + 0.02 * n(ks[3], (DEPTH, A_WIDTH)),
        "ln_b": 0.02 * n(ks[4], (DEPTH, A_WIDTH)),
        "w_s": n(ks[5], (DEPTH, A_GROUPS, CHUNK, CHUNK)) * CHUNK ** -0.5,
        "b_s": 1.0 + 0.1 * n(ks[6], (DEPTH, A_GROUPS, CHUNK)),
        "conv_w": n(ks[7], (DEPTH, CONV_WIDTH, B_WIDTH)) * CONV_WIDTH ** -0.5,
        "conv_b": 0.02 * n(ks[8], (DEPTH, B_WIDTH)),
        "w_pool": n(ks[9], (DEPTH, len(POOL_WINDOWS), C_GROUP, C_GROUP)) * C_GROUP ** -0.5,
        "pool_scale": 1.0 + 0.02 * n(ks[10], (DEPTH, C_WIDTH)),
        "w_pa": n(ks[11], (DEPTH, A_WIDTH, D_MODEL)) * A_WIDTH ** -0.5,
        "w_pb": n(ks[12], (DEPTH, B_WIDTH, D_MODEL)) * B_WIDTH ** -0.5,
        "w_pc": n(ks[13], (DEPTH, C_WIDTH, D_MODEL)) * C_WIDTH ** -0.5,
        "w_o": n(ks[14], (DEPTH, D_MODEL, D_MODEL)) * D_MODEL ** -0.5,
        "final_g": 1.0 + 0.02 * n(ks[15], (D_MODEL,)),
    }


def _fwd_reference(x, norm_g, w_in, ln_g, ln_b, w_s, b_s, conv_w, conv_b, w_pool,
              pool_scale, w_pa, w_pb, w_pc, w_o, final_g):
    for l in range(DEPTH):
        x = _hybrid_layer(x, norm_g[l], w_in[l], ln_g[l], ln_b[l], w_s[l], b_s[l],
                          conv_w[l], conv_b[l], w_pool[l], pool_scale[l],
                          w_pa[l], w_pb[l], w_pc[l], w_o[l])
    return _rmsnorm(x, final_g)


import jax as _jax
import jax.numpy as _jnp

TWIN_FORMAT = 'train_step'
FWD_PARAMS = ['x', 'norm_g', 'w_in', 'ln_g', 'ln_b', 'w_s', 'b_s', 'conv_w', 'conv_b', 'w_pool', 'pool_scale', 'w_pa', 'w_pb', 'w_pc', 'w_o', 'final_g']
TWIN_WEIGHTS = ['norm_g', 'w_in', 'ln_g', 'ln_b', 'w_s', 'b_s', 'conv_w', 'conv_b', 'w_pool', 'pool_scale', 'w_pa', 'w_pb', 'w_pc', 'w_o', 'final_g']
TWIN_DIFF_INPUT = 'x'
TWIN_INPUTS = ['x', 'norm_g', 'w_in', 'ln_g', 'ln_b', 'w_s', 'b_s', 'conv_w', 'conv_b', 'w_pool', 'pool_scale', 'w_pa', 'w_pb', 'w_pc', 'w_o', 'final_g', 'loss_target', 'm_norm_g', 'm_w_in', 'm_ln_g', 'm_ln_b', 'm_w_s', 'm_b_s', 'm_conv_w', 'm_conv_b', 'm_w_pool', 'm_pool_scale', 'm_w_pa', 'm_w_pb', 'm_w_pc', 'm_w_o', 'm_final_g', 'v_norm_g', 'v_w_in', 'v_ln_g', 'v_ln_b', 'v_w_s', 'v_b_s', 'v_conv_w', 'v_conv_b', 'v_w_pool', 'v_pool_scale', 'v_w_pa', 'v_w_pb', 'v_w_pc', 'v_w_o', 'v_final_g']
TWIN_OUTPUTS = ['loss', 'grad_x', 'grad_norm_g', 'grad_w_in', 'grad_ln_g', 'grad_ln_b', 'grad_w_s', 'grad_b_s', 'grad_conv_w', 'grad_conv_b', 'grad_w_pool', 'grad_pool_scale', 'grad_w_pa', 'grad_w_pb', 'grad_w_pc', 'grad_w_o', 'grad_final_g', 'delta_norm_g', 'delta_w_in', 'delta_ln_g', 'delta_ln_b', 'delta_w_s', 'delta_b_s', 'delta_conv_w', 'delta_conv_b', 'delta_w_pool', 'delta_pool_scale', 'delta_w_pa', 'delta_w_pb', 'delta_w_pc', 'delta_w_o', 'delta_final_g', 'new_m_norm_g', 'new_m_w_in', 'new_m_ln_g', 'new_m_ln_b', 'new_m_w_s', 'new_m_b_s', 'new_m_conv_w', 'new_m_conv_b', 'new_m_w_pool', 'new_m_pool_scale', 'new_m_w_pa', 'new_m_w_pb', 'new_m_w_pc', 'new_m_w_o', 'new_m_final_g', 'new_v_norm_g', 'new_v_w_in', 'new_v_ln_g', 'new_v_ln_b', 'new_v_w_s', 'new_v_b_s', 'new_v_conv_w', 'new_v_conv_b', 'new_v_w_pool', 'new_v_pool_scale', 'new_v_w_pa', 'new_v_w_pb', 'new_v_w_pc', 'new_v_w_o', 'new_v_final_g']
TWIN_LEAF_KINDS = {'loss': 'loss', 'grad_x': 'grad_x', 'grad_norm_g': 'grad_w', 'grad_w_in': 'grad_w', 'grad_ln_g': 'grad_w', 'grad_ln_b': 'grad_w', 'grad_w_s': 'grad_w', 'grad_b_s': 'grad_w', 'grad_conv_w': 'grad_w', 'grad_conv_b': 'grad_w', 'grad_w_pool': 'grad_w', 'grad_pool_scale': 'grad_w', 'grad_w_pa': 'grad_w', 'grad_w_pb': 'grad_w', 'grad_w_pc': 'grad_w', 'grad_w_o': 'grad_w', 'grad_final_g': 'grad_w', 'delta_norm_g': 'delta_w', 'delta_w_in': 'delta_w', 'delta_ln_g': 'delta_w', 'delta_ln_b': 'delta_w', 'delta_w_s': 'delta_w', 'delta_b_s': 'delta_w', 'delta_conv_w': 'delta_w', 'delta_conv_b': 'delta_w', 'delta_w_pool': 'delta_w', 'delta_pool_scale': 'delta_w', 'delta_w_pa': 'delta_w', 'delta_w_pb': 'delta_w', 'delta_w_pc': 'delta_w', 'delta_w_o': 'delta_w', 'delta_final_g': 'delta_w', 'new_m_norm_g': 'new_m', 'new_m_w_in': 'new_m', 'new_m_ln_g': 'new_m', 'new_m_ln_b': 'new_m', 'new_m_w_s': 'new_m', 'new_m_b_s': 'new_m', 'new_m_conv_w': 'new_m', 'new_m_conv_b': 'new_m', 'new_m_w_pool': 'new_m', 'new_m_pool_scale': 'new_m', 'new_m_w_pa': 'new_m', 'new_m_w_pb': 'new_m', 'new_m_w_pc': 'new_m', 'new_m_w_o': 'new_m', 'new_m_final_g': 'new_m', 'new_v_norm_g': 'new_v', 'new_v_w_in': 'new_v', 'new_v_ln_g': 'new_v', 'new_v_ln_b': 'new_v', 'new_v_w_s': 'new_v', 'new_v_b_s': 'new_v', 'new_v_conv_w': 'new_v', 'new_v_conv_b': 'new_v', 'new_v_w_pool': 'new_v', 'new_v_pool_scale': 'new_v', 'new_v_w_pa': 'new_v', 'new_v_w_pb': 'new_v', 'new_v_w_pc': 'new_v', 'new_v_w_o': 'new_v', 'new_v_final_g': 'new_v'}


def _forward(args):
    return _fwd_reference(*[args[k] for k in FWD_PARAMS])


def _output_shape():
    def fwd():
        inp = _fwd_setup_inputs(0)
        return _fwd_reference(*[inp[k] for k in FWD_PARAMS])
    out = _jax.eval_shape(fwd)
    return out.shape, out.dtype

N_MICROBATCH = 1
ADAM_LR = 0.001
ADAM_B1 = 0.9
ADAM_B2 = 0.999
ADAM_EPS = 1e-08
ADAM_WD = 0.01
ADAM_STEP = 10
PER_EXAMPLE_BATCH_AXIS = {'x': 0, 'loss_target': 0}
SHARED_INPUTS = []
_WEIGHT_DTYPES = {'norm_g': _jnp.float32, 'w_in': _jnp.float32, 'ln_g': _jnp.float32, 'ln_b': _jnp.float32, 'w_s': _jnp.float32, 'b_s': _jnp.float32, 'conv_w': _jnp.float32, 'conv_b': _jnp.float32, 'w_pool': _jnp.float32, 'pool_scale': _jnp.float32, 'w_pa': _jnp.float32, 'w_pb': _jnp.float32, 'w_pc': _jnp.float32, 'w_o': _jnp.float32, 'final_g': _jnp.float32}
MOMENT_SCALE = {'norm_g': 2.147848e-01, 'w_in': 7.678576e-02, 'ln_g': 4.935064e-02, 'ln_b': 4.939700e-02, 'w_s': 3.503085e-02, 'b_s': 4.896858e-02, 'conv_w': 1.065629e-01, 'conv_b': 1.082497e-01, 'w_pool': 9.431115e-02, 'pool_scale': 9.508934e-02, 'w_pa': 6.068306e-02, 'w_pb': 7.557470e-02, 'w_pc': 6.674864e-02, 'w_o': 1.176590e-01, 'final_g': 6.399650e+01}


def _to_microbatches(a, axis):
    t = _jnp.moveaxis(a, axis, 0)
    t = t.reshape((N_MICROBATCH, t.shape[0] // N_MICROBATCH) + t.shape[1:])
    return _jnp.moveaxis(t, 1, axis + 1)


def setup_inputs(seed: int = 0) -> dict:
    inp = _fwd_setup_inputs(seed)
    key = _jax.random.fold_in(_jax.random.key(seed), 7919)
    shape, _ = _output_shape()
    out = dict(inp)
    out["loss_target"] = _jax.random.normal(_jax.random.fold_in(key, 0), shape, _jnp.float32)
    for i, name in enumerate(TWIN_WEIGHTS):
        w = inp[name].astype(_jnp.float32)
        if MOMENT_SCALE is None:
            s = _jnp.sqrt(_jnp.mean(_jnp.square(w)) + 1e-30)
        else:
            s = MOMENT_SCALE[name]
        km, kv = _jax.random.split(_jax.random.fold_in(key, i + 1))
        out[name] = w
        out["m_" + name] = s * _jax.random.normal(km, w.shape, _jnp.float32)
        out["v_" + name] = (s * s) * _jax.random.uniform(kv, w.shape, _jnp.float32, 0.5, 1.5)
    if N_MICROBATCH > 1:
        for name, axis in PER_EXAMPLE_BATCH_AXIS.items():
            out[name] = _to_microbatches(out[name], axis)
    return {'x': out['x'], 'norm_g': out['norm_g'], 'w_in': out['w_in'], 'ln_g': out['ln_g'], 'ln_b': out['ln_b'], 'w_s': out['w_s'], 'b_s': out['b_s'], 'conv_w': out['conv_w'], 'conv_b': out['conv_b'], 'w_pool': out['w_pool'], 'pool_scale': out['pool_scale'], 'w_pa': out['w_pa'], 'w_pb': out['w_pb'], 'w_pc': out['w_pc'], 'w_o': out['w_o'], 'final_g': out['final_g'], 'loss_target': out['loss_target'], 'm_norm_g': out['m_norm_g'], 'm_w_in': out['m_w_in'], 'm_ln_g': out['m_ln_g'], 'm_ln_b': out['m_ln_b'], 'm_w_s': out['m_w_s'], 'm_b_s': out['m_b_s'], 'm_conv_w': out['m_conv_w'], 'm_conv_b': out['m_conv_b'], 'm_w_pool': out['m_w_pool'], 'm_pool_scale': out['m_pool_scale'], 'm_w_pa': out['m_w_pa'], 'm_w_pb': out['m_w_pb'], 'm_w_pc': out['m_w_pc'], 'm_w_o': out['m_w_o'], 'm_final_g': out['m_final_g'], 'v_norm_g': out['v_norm_g'], 'v_w_in': out['v_w_in'], 'v_ln_g': out['v_ln_g'], 'v_ln_b': out['v_ln_b'], 'v_w_s': out['v_w_s'], 'v_b_s': out['v_b_s'], 'v_conv_w': out['v_conv_w'], 'v_conv_b': out['v_conv_b'], 'v_w_pool': out['v_w_pool'], 'v_pool_scale': out['v_pool_scale'], 'v_w_pa': out['v_w_pa'], 'v_w_pb': out['v_w_pb'], 'v_w_pc': out['v_w_pc'], 'v_w_o': out['v_w_o'], 'v_final_g': out['v_final_g']}


def _loss(weights, diff, rest, loss_target):
    with _jax.named_scope("forward"):
        args = {**rest, TWIN_DIFF_INPUT: diff, **{k: w.astype(_WEIGHT_DTYPES[k]) for k, w in weights.items()}}
        y = _forward(args)
    with _jax.named_scope("loss_head"):
        err = _jnp.square(y.astype(_jnp.float32) - loss_target)
        return 0.5 * _jnp.sum(_jnp.mean(err, axis=-1)) if err.ndim else 0.5 * err


def _adamw(w, g, m, v):
    m = ADAM_B1 * m + (1.0 - ADAM_B1) * g
    v = ADAM_B2 * v + (1.0 - ADAM_B2) * _jnp.square(g)
    m_hat = m / (1.0 - ADAM_B1 ** ADAM_STEP)
    v_hat = v / (1.0 - ADAM_B2 ** ADAM_STEP)
    delta = -ADAM_LR * (m_hat / (_jnp.sqrt(v_hat) + ADAM_EPS) + ADAM_WD * w)
    return delta, m, v


def reference(x, norm_g, w_in, ln_g, ln_b, w_s, b_s, conv_w, conv_b, w_pool, pool_scale, w_pa, w_pb, w_pc, w_o, final_g, loss_target, m_norm_g, m_w_in, m_ln_g, m_ln_b, m_w_s, m_b_s, m_conv_w, m_conv_b, m_w_pool, m_pool_scale, m_w_pa, m_w_pb, m_w_pc, m_w_o, m_final_g, v_norm_g, v_w_in, v_ln_g, v_ln_b, v_w_s, v_b_s, v_conv_w, v_conv_b, v_w_pool, v_pool_scale, v_w_pa, v_w_pb, v_w_pc, v_w_o, v_final_g):
    given = dict(x=x, norm_g=norm_g, w_in=w_in, ln_g=ln_g, ln_b=ln_b, w_s=w_s, b_s=b_s, conv_w=conv_w, conv_b=conv_b, w_pool=w_pool, pool_scale=pool_scale, w_pa=w_pa, w_pb=w_pb, w_pc=w_pc, w_o=w_o, final_g=final_g, loss_target=loss_target, m_norm_g=m_norm_g, m_w_in=m_w_in, m_ln_g=m_ln_g, m_ln_b=m_ln_b, m_w_s=m_w_s, m_b_s=m_b_s, m_conv_w=m_conv_w, m_conv_b=m_conv_b, m_w_pool=m_w_pool, m_pool_scale=m_pool_scale, m_w_pa=m_w_pa, m_w_pb=m_w_pb, m_w_pc=m_w_pc, m_w_o=m_w_o, m_final_g=m_final_g, v_norm_g=v_norm_g, v_w_in=v_w_in, v_ln_g=v_ln_g, v_ln_b=v_ln_b, v_w_s=v_w_s, v_b_s=v_b_s, v_conv_w=v_conv_w, v_conv_b=v_conv_b, v_w_pool=v_w_pool, v_pool_scale=v_pool_scale, v_w_pa=v_w_pa, v_w_pb=v_w_pb, v_w_pc=v_w_pc, v_w_o=v_w_o, v_final_g=v_final_g)
    weights = {n: given[n] for n in TWIN_WEIGHTS}
    shared = {n: given[n] for n in SHARED_INPUTS}
    per_example = {n: given[n] for n in ['x']}
    grad_fn = _jax.value_and_grad(_loss, argnums=(0, 1))

    def one_microbatch(ex, loss_target):
        ex = dict(ex)
        diff = ex.pop(TWIN_DIFF_INPUT)
        return grad_fn(weights, diff, {**shared, **ex}, loss_target)

    if N_MICROBATCH == 1:
        loss, (grad_w, grad_x) = one_microbatch(per_example, given["loss_target"])
    else:
        def body(carry, xs):
            loss_sum, grad_sum = carry
            l_k, (gw_k, gx_k) = one_microbatch(xs[0], xs[1])
            with _jax.named_scope("update"):
                return (loss_sum + l_k, _jax.tree.map(_jnp.add, grad_sum, gw_k)), gx_k

        init = (_jnp.zeros((), _jnp.float32), _jax.tree.map(_jnp.zeros_like, weights))
        (loss, grad_w), grad_x = _jax.lax.scan(body, init, (per_example, given["loss_target"]))
    with _jax.named_scope("update"):
        delta_w, new_m, new_v = {}, {}, {}
        for n in TWIN_WEIGHTS:
            delta_w[n], new_m[n], new_v[n] = _adamw(weights[n], grad_w[n], given["m_" + n], given["v_" + n])
    return (loss, grad_x, *[grad_w[n] for n in TWIN_WEIGHTS], *[delta_w[n] for n in TWIN_WEIGHTS],
            *[new_m[n] for n in TWIN_WEIGHTS], *[new_v[n] for n in TWIN_WEIGHTS])
```

```python
import functools
import math

import jax
import jax.numpy as jnp
from jax import lax
from jax.experimental import pallas as pl
from jax.experimental.pallas import tpu as pltpu

F32 = jnp.float32
BF16 = jnp.bfloat16
SDS = jax.ShapeDtypeStruct
MESH = pl.DeviceIdType.MESH

SEG = 512
CHUNK = 128
GROUPS = 8
HEAD = SEG // GROUPS
POOL_WINDOWS = (2, 4, 8, 16)
POOL_GROUP = SEG // len(POOL_WINDOWS)
CONV_TAPS = 3
HALO = 16
RMS_EPS = 1e-6
LN_EPS = 1e-5
ADAM_LR, ADAM_B1, ADAM_B2, ADAM_EPS, ADAM_WD, ADAM_STEP = 0.001, 0.9, 0.999, 1e-08, 0.01, 10

O_U, O_V, O_ZA, O_XB, O_BG, O_CG, O_ZB, O_XC, O_ZC, O_G = (SEG * i for i in range(10))

N_DEV = 8
N_CHIP = 4
LANES = 128
VMEM_LIMIT = 48 * 1024 * 1024


def _cparams(sem=None, **kw):
    return pltpu.CompilerParams(dimension_semantics=sem, vmem_limit_bytes=VMEM_LIMIT, **kw)


def _pick(total, target, mult):
    best = None
    for d in range(mult, min(total, target) + 1, mult):
        if total % d == 0:
            best = d
    assert best is not None, (total, target, mult)
    return best


def _dot(a, b):
    return jnp.dot(a, b, preferred_element_type=F32)


def _dot_nt(a, b):
    return lax.dot_general(a, b, (((1,), (1,)), ((), ())), preferred_element_type=F32)


def _dot_tn(a, b):
    return lax.dot_general(a, b, (((0,), (0,)), ((), ())), preferred_element_type=F32)


def _zero(ref):
    ref[...] = jnp.zeros(ref.shape, ref.dtype)


def _sigmoid(x):
    return 1.0 / (1.0 + jnp.exp(-x))


_GELU_C = math.sqrt(2.0 / math.pi)


def _gelu(x):
    t = jnp.tanh(_GELU_C * (x + 0.044715 * x * x * x))
    return 0.5 * x * (1.0 + t), t


def _gelu_grad(x, t):
    return 0.5 * (1.0 + t) + 0.5 * x * (1.0 - t * t) * _GELU_C * (1.0 + 3.0 * 0.044715 * x * x)


def _me():
    return lax.axis_index("x"), lax.axis_index("y"), lax.axis_index("c")


def _inproj(x, norm_g, win_t, name):
    T, D = x.shape
    N = win_t.shape[0]
    bt = _pick(T, 1024, 16)
    bn = _pick(N, 1536, LANES)

    def body(x_ref, g_ref, w_ref, p_ref, h_ref, hs_ref):
        @pl.when(pl.program_id(1) == 0)
        def _():
            xv = x_ref[...]
            rstd = lax.rsqrt(jnp.mean(xv * xv, axis=-1, keepdims=True) + RMS_EPS)
            hb = (xv * rstd * g_ref[...]).astype(BF16)
            hs_ref[...] = hb
            h_ref[...] = hb

        p_ref[...] = _dot_nt(hs_ref[...], w_ref[...]).astype(BF16)

    return pl.pallas_call(
        body, name=name, grid=(T // bt, N // bn),
        in_specs=[pl.BlockSpec((bt, D), lambda i, j: (i, 0)),
                  pl.BlockSpec((1, D), lambda i, j: (0, 0)),
                  pl.BlockSpec((bn, D), lambda i, j: (j, 0))],
        out_specs=[pl.BlockSpec((bt, bn), lambda i, j: (i, j)),
                   pl.BlockSpec((bt, D), lambda i, j: (i, 0))],
        out_shape=[SDS((T, N), BF16), SDS((T, D), BF16)],
        scratch_shapes=[pltpu.VMEM((bt, D), BF16)],
        compiler_params=_cparams(("arbitrary", "arbitrary")),
    )(x, norm_g.reshape(1, D), win_t)


C_LNG, C_LNB, C_CW0, C_CW1, C_CW2, C_CB, C_PS = range(7)
C_ROWS = 8


def _mixers(p_ref, hxb_ref, hcg_ref, hxc_ref, cv, bsb_ref, wcat_ref, wpool_ref, extb, extc,
            first, blk, R, need_grad):
    def seg(lo):
        return p_ref[:, lo:lo + SEG].astype(F32)

    u, v, za = seg(O_U), seg(O_V), seg(O_ZA)
    xb, bg, cg, zb = seg(O_XB), seg(O_BG), seg(O_CG), seg(O_ZB)
    xc, zc = seg(O_XC), seg(O_ZC)
    out = {}

    ug, tu = _gelu(u)
    vg, tv = _gelu(v)
    mu = jnp.mean(vg, axis=-1, keepdims=True)
    vcen = vg - mu
    rs = lax.rsqrt(jnp.mean(vcen * vcen, axis=-1, keepdims=True) + LN_EPS)
    vhat = vcen * rs
    vn = (vhat * cv[C_LNG:C_LNG + 1, :] + cv[C_LNB:C_LNB + 1, :]).astype(BF16)
    lane_group = lax.broadcasted_iota(jnp.int32, (CHUNK, SEG), 1) // HEAD
    zero_b = jnp.zeros((CHUNK, SEG), BF16)
    sgs = []
    for ci in range(R // CHUNK):
        vc = vn[ci * CHUNK:(ci + 1) * CHUNK]
        vst = jnp.concatenate([jnp.where(lane_group == g, vc, zero_b) for g in range(GROUPS)], axis=0)
        sgs.append(_dot(wcat_ref[...], vst) + bsb_ref[...])
    sg = sgs[0] if len(sgs) == 1 else jnp.concatenate(sgs, axis=0)
    a_out = ug * sg
    sa = _sigmoid(za)
    out["a"] = a_out * (za * sa)

    cx = cg * xb
    halo_b = hcg_ref[...].astype(F32) * hxb_ref[...].astype(F32)
    extb[0:HALO, :] = jnp.where(first, 0.0, halo_b)
    extb[HALO:HALO + R, :] = cx
    cx1 = extb[pl.ds(HALO - 1, R), :]
    cx2 = extb[pl.ds(HALO - 2, R), :]
    yconv = (cv[C_CW0:C_CW0 + 1, :] * cx2 + cv[C_CW1:C_CW1 + 1, :] * cx1
             + cv[C_CW2:C_CW2 + 1, :] * cx + cv[C_CB:C_CB + 1, :])
    b_out = bg * yconv
    sb = _sigmoid(zb)
    out["b"] = b_out * (zb * sb)

    extc[0:HALO, :] = jnp.where(first, 0.0, hxc_ref[...].astype(F32))
    extc[HALO:HALO + R, :] = xc
    tpos = blk * R + lax.broadcasted_iota(jnp.int32, (R, POOL_GROUP), 0) + 1
    pooled, invs, pws = [], [], []
    for gi, w in enumerate(POOL_WINDOWS):
        lo = gi * POOL_GROUP
        win = xc[:, lo:lo + POOL_GROUP]
        for j in range(1, w):
            win = win + extc[pl.ds(HALO - j, R), lo:lo + POOL_GROUP]
        inv = 1.0 / jnp.minimum(tpos, w).astype(F32)
        pg = (win * inv - xc[:, lo:lo + POOL_GROUP]).astype(BF16)
        pooled.append(pg)
        invs.append(inv)
        pws.append(_dot(pg, wpool_ref[gi]))
    pw = jnp.concatenate(pws, axis=1)
    c_out = pw * cv[C_PS:C_PS + 1, :]
    sc = _sigmoid(zc)
    out["c"] = c_out * (zc * sc)

    if need_grad:
        out.update(u=u, v=v, tu=tu, tv=tv, ug=ug, sg=sg, a_out=a_out, za=za, sa=sa,
                   rs=rs, vhat=vhat, vn=vn, lane_group=lane_group, zero_b=zero_b,
                   xb=xb, bg=bg, cg=cg, cx=cx, cx1=cx1, cx2=cx2, yconv=yconv, b_out=b_out, zb=zb, sb=sb,
                   pooled=pooled, invs=invs, pw=pw, c_out=c_out, zc=zc, sc=sc)
    return out


def _halo_specs(R, nb, rev):
    step = R // HALO

    def mk(col):
        def imap(i):
            b = (nb - 1 - i) if rev else i
            return (jnp.maximum(b * step - 1, 0), col)
        return pl.BlockSpec((HALO, SEG), imap)

    return [mk(O_XB // SEG), mk(O_CG // SEG), mk(O_XC // SEG)]


def _const_spec(shape):
    nd = len(shape)
    return pl.BlockSpec(shape, lambda i: (0,) * nd, pipeline_mode=pl.Buffered(1))


def _mix_block_rows(T):
    return _pick(T, 128, CHUNK)


def _mix_fwd(p, x, lw, name):
    T, D = x.shape
    N = p.shape[1]
    R = _mix_block_rows(T)
    nb = T // R

    def body(p_ref, hxb, hcg, hxc, x_ref, cv_ref, bsb_ref, wcat_ref, wpool_ref, wpa_ref, wpb_ref, wpc_ref,
             wo_ref, xo_ref, ya_ref, yb_ref, yc_ref, extb, extc):
        i = pl.program_id(0)
        cv = cv_ref[...]
        r = _mixers(p_ref, hxb, hcg, hxc, cv, bsb_ref, wcat_ref, wpool_ref, extb, extc,
                    i == 0, i, R, False)
        ya = _dot(r["a"].astype(BF16), wpa_ref[...])
        yb = _dot(r["b"].astype(BF16), wpb_ref[...])
        yc = _dot(r["c"].astype(BF16), wpc_ref[...])
        ga = p_ref[:, O_G:O_G + D].astype(F32)
        gb = p_ref[:, O_G + D:O_G + 2 * D].astype(F32)
        gc = p_ref[:, O_G + 2 * D:O_G + 3 * D].astype(F32)
        merged = _sigmoid(ga) * ya + _sigmoid(gb) * yb + _sigmoid(gc) * yc
        xo_ref[...] = x_ref[...] + _dot(merged.astype(BF16), wo_ref[...])
        ya_ref[...] = ya.astype(BF16)
        yb_ref[...] = yb.astype(BF16)
        yc_ref[...] = yc.astype(BF16)

    row = lambda w: pl.BlockSpec((R, w), lambda i: (i, 0))
    consts = [lw["cvec"], lw["bsb"], lw["wcat"], lw["wpool"], lw["wpa"], lw["wpb"], lw["wpc"], lw["wo"]]
    return pl.pallas_call(
        body, name=name, grid=(nb,),
        in_specs=[row(N)] + _halo_specs(R, nb, False) + [row(D)] + [_const_spec(c.shape) for c in consts],
        out_specs=[row(D), row(D), row(D), row(D)],
        out_shape=[SDS((T, D), F32), SDS((T, D), BF16), SDS((T, D), BF16), SDS((T, D), BF16)],
        scratch_shapes=[pltpu.VMEM((HALO + R, SEG), F32), pltpu.VMEM((HALO + R, SEG), F32)],
        compiler_params=_cparams(("arbitrary",)),
    )(p, p, p, p, x, *consts)


def _loss_head(x, final_g, target, name):
    T, D = x.shape
    bt = _pick(T, 512, 8)

    def body(x_ref, g_ref, t_ref, dx_ref, loss_ref, dg_ref):
        @pl.when(pl.program_id(0) == 0)
        def _():
            _zero(loss_ref)
            _zero(dg_ref)

        xv = x_ref[...]
        g = g_ref[...]
        rstd = lax.rsqrt(jnp.mean(xv * xv, axis=-1, keepdims=True) + RMS_EPS)
        xhat = xv * rstd
        err = xhat * g - t_ref[...]
        part = 0.5 * jnp.sum(jnp.sum(err * err, axis=-1, keepdims=True), axis=0, keepdims=True) / D
        loss_ref[...] += jnp.broadcast_to(part, loss_ref.shape)
        dy = err * (1.0 / D)
        dg_ref[0:1, :] += jnp.sum(dy * xhat, axis=0, keepdims=True)
        dxn = dy * g
        dx_ref[...] = rstd * (dxn - xhat * jnp.mean(dxn * xhat, axis=-1, keepdims=True))

    return pl.pallas_call(
        body, name=name, grid=(T // bt,),
        in_specs=[pl.BlockSpec((bt, D), lambda i: (i, 0)), _const_spec((1, D)), pl.BlockSpec((bt, D), lambda i: (i, 0))],
        out_specs=[pl.BlockSpec((bt, D), lambda i: (i, 0)), _const_spec((8, LANES)), _const_spec((8, D))],
        out_shape=[SDS((T, D), F32), SDS((8, LANES), F32), SDS((8, D), F32)],
        compiler_params=_cparams(("arbitrary",)),
    )(x, final_g.reshape(1, D), target)


V_LNG, V_LNB, V_CB, V_PS, V_CW0, V_CW1, V_CW2 = range(7)


def _mix_bwd(p, dxo, ya, yb, yc, lw, name):
    T, D = dxo.shape
    N = p.shape[1]
    R = _mix_block_rows(T)
    nb = T // R

    def body(p_ref, hxb, hcg, hxc, dxo_ref, ya_ref, yb_ref, yc_ref, cv_ref, bsb_ref, wcat_ref, wcatt_ref,
             wpool_ref, wpa_ref, wpb_ref, wpc_ref, wo_ref,
             dp_ref, gwo_ref, gwpa_ref, gwpb_ref, gwpc_ref, gwc_ref, gbs_ref, gwpool_ref, gvec_ref,
             extb, extc, extdy, extq, cdy, cq, bsacc):
        i = pl.program_id(0)
        blk = nb - 1 - i

        @pl.when(i == 0)
        def _():
            for ref in (gwo_ref, gwpa_ref, gwpb_ref, gwpc_ref, gwc_ref, gwpool_ref, gvec_ref, cdy, cq, bsacc):
                _zero(ref)

        cv = cv_ref[...]
        r = _mixers(p_ref, hxb, hcg, hxc, cv, bsb_ref, wcat_ref, wpool_ref, extb, extc,
                    blk == 0, blk, R, True)

        dxo_b = dxo_ref[...].astype(BF16)
        dm = _dot_nt(dxo_b, wo_ref[...])
        ys = [ya_ref[...].astype(F32), yb_ref[...].astype(F32), yc_ref[...].astype(F32)]
        sig = [_sigmoid(p_ref[:, O_G + k * D:O_G + (k + 1) * D].astype(F32)) for k in range(3)]
        merged = sig[0] * ys[0] + sig[1] * ys[1] + sig[2] * ys[2]
        gwo_ref[...] += _dot_tn(merged.astype(BF16), dxo_b)
        dacts = []
        for k, (act, w_ref, gw_ref) in enumerate(((r["a"], wpa_ref, gwpa_ref), (r["b"], wpb_ref, gwpb_ref),
                                                  (r["c"], wpc_ref, gwpc_ref))):
            dyk = (dm * sig[k]).astype(BF16)
            dp_ref[:, O_G + k * D:O_G + (k + 1) * D] = (dm * ys[k] * sig[k] * (1.0 - sig[k])).astype(BF16)
            gw_ref[...] += _dot_tn(act.astype(BF16), dyk)
            dacts.append(_dot_nt(dyk, w_ref[...]))
        da, db, dc = dacts

        def silu_bwd(dact, pre, z, s):
            return dact * (z * s), dact * pre * (s * (1.0 + z * (1.0 - s)))

        d_aout, dza = silu_bwd(da, r["a_out"], r["za"], r["sa"])
        dp_ref[:, O_ZA:O_ZA + SEG] = dza.astype(BF16)
        dp_ref[:, O_U:O_U + SEG] = (d_aout * r["sg"] * _gelu_grad(r["u"], r["tu"])).astype(BF16)
        d_sg = d_aout * r["ug"]
        dvns = []
        for ci in range(R // CHUNK):
            dsc = d_sg[ci * CHUNK:(ci + 1) * CHUNK]
            bsacc[...] += dsc
            dsc_b = dsc.astype(BF16)
            dst = jnp.concatenate([jnp.where(r["lane_group"] == g, dsc_b, r["zero_b"]) for g in range(GROUPS)], axis=0)
            dvns.append(_dot(wcatt_ref[...], dst))
            gwc_ref[...] += _dot_nt(dst, r["vn"][ci * CHUNK:(ci + 1) * CHUNK])
        d_vn = dvns[0] if len(dvns) == 1 else jnp.concatenate(dvns, axis=0)
        vhat = r["vhat"]
        gvec_ref[V_LNG:V_LNG + 1, :] += jnp.sum(d_vn * vhat, axis=0, keepdims=True)
        gvec_ref[V_LNB:V_LNB + 1, :] += jnp.sum(d_vn, axis=0, keepdims=True)
        d_vhat = d_vn * cv[C_LNG:C_LNG + 1, :]
        d_vg = r["rs"] * (d_vhat - jnp.mean(d_vhat, axis=-1, keepdims=True)
                          - vhat * jnp.mean(d_vhat * vhat, axis=-1, keepdims=True))
        dp_ref[:, O_V:O_V + SEG] = (d_vg * _gelu_grad(r["v"], r["tv"])).astype(BF16)

        d_bout, dzb = silu_bwd(db, r["b_out"], r["zb"], r["sb"])
        dp_ref[:, O_ZB:O_ZB + SEG] = dzb.astype(BF16)
        dp_ref[:, O_BG:O_BG + SEG] = (d_bout * r["yconv"]).astype(BF16)
        d_y = d_bout * r["bg"]
        gvec_ref[V_CB:V_CB + 1, :] += jnp.sum(d_y, axis=0, keepdims=True)
        gvec_ref[V_CW0:V_CW0 + 1, :] += jnp.sum(d_y * r["cx2"], axis=0, keepdims=True)
        gvec_ref[V_CW1:V_CW1 + 1, :] += jnp.sum(d_y * r["cx1"], axis=0, keepdims=True)
        gvec_ref[V_CW2:V_CW2 + 1, :] += jnp.sum(d_y * r["cx"], axis=0, keepdims=True)
        extdy[0:R, :] = d_y
        extdy[R:R + HALO, :] = cdy[...]
        d_cx = (cv[C_CW2:C_CW2 + 1, :] * d_y + cv[C_CW1:C_CW1 + 1, :] * extdy[pl.ds(1, R), :]
                + cv[C_CW0:C_CW0 + 1, :] * extdy[pl.ds(2, R), :])
        cdy[...] = d_y[0:HALO]
        dp_ref[:, O_CG:O_CG + SEG] = (d_cx * r["xb"]).astype(BF16)
        dp_ref[:, O_XB:O_XB + SEG] = (d_cx * r["cg"]).astype(BF16)

        d_cout, dzc = silu_bwd(dc, r["c_out"], r["zc"], r["sc"])
        dp_ref[:, O_ZC:O_ZC + SEG] = dzc.astype(BF16)
        gvec_ref[V_PS:V_PS + 1, :] += jnp.sum(d_cout * r["pw"], axis=0, keepdims=True)
        d_pw = (d_cout * cv[C_PS:C_PS + 1, :]).astype(BF16)
        dpool = []
        for gi, w in enumerate(POOL_WINDOWS):
            lo = gi * POOL_GROUP
            dpw_g = d_pw[:, lo:lo + POOL_GROUP]
            gwpool_ref[lo:lo + POOL_GROUP, :] += _dot_tn(r["pooled"][gi], dpw_g)
            dpg = _dot_nt(dpw_g, wpool_ref[gi])
            dpool.append(dpg)
            extq[0:R, lo:lo + POOL_GROUP] = dpg * r["invs"][gi]
        extq[R:R + HALO, :] = cq[...]
        for gi, w in enumerate(POOL_WINDOWS):
            lo = gi * POOL_GROUP
            acc = extq[0:R, lo:lo + POOL_GROUP]
            for j in range(1, w):
                acc = acc + extq[pl.ds(j, R), lo:lo + POOL_GROUP]
            dp_ref[:, O_XC + lo:O_XC + lo + POOL_GROUP] = (acc - dpool[gi]).astype(BF16)
        cq[...] = extq[0:HALO, :]

        @pl.when(i == nb - 1)
        def _():
            rr = lax.broadcasted_iota(jnp.int32, gwc_ref.shape, 0) % CHUNK
            cc = lax.broadcasted_iota(jnp.int32, gwc_ref.shape, 1)
            gwc_ref[...] = jnp.where(cc <= rr, gwc_ref[...], 0.0)
            acc = bsacc[...]
            hi = acc.astype(BF16)
            lo_ = (acc - hi.astype(F32)).astype(BF16)
            sel = (lax.broadcasted_iota(jnp.int32, (SEG, LANES), 0) // HEAD
                   == lax.broadcasted_iota(jnp.int32, (SEG, LANES), 1)).astype(BF16)
            gbs_ref[...] = _dot(hi, sel) + _dot(lo_, sel)

    row = lambda w: pl.BlockSpec((R, w), lambda i: (nb - 1 - i, 0))
    consts = [lw["cvec"], lw["bsb"], lw["wcat"], lw["wcatt"], lw["wpool"], lw["wpa"], lw["wpb"], lw["wpc"], lw["wo"]]
    acc_shapes = [(D, D), (SEG, D), (SEG, D), (SEG, D), (GROUPS * CHUNK, CHUNK), (CHUNK, LANES), (SEG, POOL_GROUP), (8, SEG)]
    return pl.pallas_call(
        body, name=name, grid=(nb,),
        in_specs=([row(N)] + _halo_specs(R, nb, True) + [row(D), row(D), row(D), row(D)]
                  + [_const_spec(c.shape) for c in consts]),
        out_specs=[row(N)] + [_const_spec(s) for s in acc_shapes],
        out_shape=[SDS((T, N), BF16)] + [SDS(s, F32) for s in acc_shapes],
        scratch_shapes=[pltpu.VMEM((HALO + R, SEG), F32)] * 4
        + [pltpu.VMEM((HALO, SEG), F32), pltpu.VMEM((HALO, SEG), F32), pltpu.VMEM((CHUNK, SEG), F32)],
        compiler_params=_cparams(("arbitrary",)),
    )(p, p, p, p, dxo, ya, yb, yc, *consts)


def _inproj_bwd_x(dp, win_t, x, norm_g, dxo, name):
    T, D = x.shape
    N = dp.shape[1]
    bt = _pick(T, 1024, 16)
    bk = _pick(N, 1536, LANES)
    nk = N // bk

    def body(dp_ref, w_ref, x_ref, g_ref, dxo_ref, dx_ref, dg_ref, acc_ref):
        i, k = pl.program_id(0), pl.program_id(1)

        @pl.when((i == 0) & (k == 0))
        def _():
            _zero(dg_ref)

        @pl.when(k == 0)
        def _():
            _zero(acc_ref)

        acc_ref[...] += _dot(dp_ref[...], w_ref[...])

        @pl.when(k == nk - 1)
        def _():
            dh = acc_ref[...]
            xv = x_ref[...]
            rstd = lax.rsqrt(jnp.mean(xv * xv, axis=-1, keepdims=True) + RMS_EPS)
            xhat = xv * rstd
            dg_ref[0:1, :] += jnp.sum(dh * xhat, axis=0, keepdims=True)
            dxn = dh * g_ref[...]
            dx_ref[...] = dxo_ref[...] + rstd * (dxn - xhat * jnp.mean(dxn * xhat, axis=-1, keepdims=True))

    return pl.pallas_call(
        body, name=name, grid=(T // bt, nk),
        in_specs=[pl.BlockSpec((bt, bk), lambda i, k: (i, k)),
                  pl.BlockSpec((bk, D), lambda i, k: (k, 0)),
                  pl.BlockSpec((bt, D), lambda i, k: (i, 0)),
                  pl.BlockSpec((1, D), lambda i, k: (0, 0)),
                  pl.BlockSpec((bt, D), lambda i, k: (i, 0))],
        out_specs=[pl.BlockSpec((bt, D), lambda i, k: (i, 0)), pl.BlockSpec((8, D), lambda i, k: (0, 0))],
        out_shape=[SDS((T, D), F32), SDS((8, D), F32)],
        scratch_shapes=[pltpu.VMEM((bt, D), F32)],
        compiler_params=_cparams(("arbitrary", "arbitrary")),
    )(dp, win_t, x, norm_g.reshape(1, D), dxo)


def _inproj_bwd_w(dp, h, name):
    T, N = dp.shape
    D = h.shape[1]
    bn = _pick(N, 1920, LANES)
    bk = _pick(T, 1024, 16)
    nk = T // bk

    def body(dp_ref, h_ref, o_ref):
        @pl.when(pl.program_id(1) == 0)
        def _():
            _zero(o_ref)

        o_ref[...] += _dot_tn(dp_ref[...], h_ref[...])

    return pl.pallas_call(
        body, name=name, grid=(N // bn, nk),
        in_specs=[pl.BlockSpec((bk, bn), lambda j, k: (k, j)), pl.BlockSpec((bk, D), lambda j, k: (k, 0))],
        out_specs=pl.BlockSpec((bn, D), lambda j, k: (j, 0)),
        out_shape=SDS((N, D), F32),
        compiler_params=_cparams(("arbitrary", "arbitrary")),
    )(dp, h)


def _chip_peer(x, y, j):
    px = (1 - x) if (j >> 1) else x
    py = (1 - y) if (j & 1) else y
    return px, py


def _blk(ref, kind, k, n):
    if kind == "rows":
        return ref.at[pl.ds(pl.multiple_of(k * n, 8), n)]
    return ref.at[:, pl.ds(pl.multiple_of(k * n, LANES), n)]


def _all_gather(shards, kinds, name):
    n = len(shards)
    sizes = [s.shape[0] if k == "rows" else s.shape[1] for s, k in zip(shards, kinds)]
    fulls = [SDS((s.shape[0] * N_DEV,) + s.shape[1:], s.dtype) if k == "rows"
             else SDS((s.shape[0], s.shape[1] * N_DEV), s.dtype) for s, k in zip(shards, kinds)]

    def body(*refs):
        ins, outs = refs[:n], refs[n:2 * n]
        send_sems, recv_sems, loc_sems = refs[2 * n:]
        x, y, c = _me()
        sibling = (x, y, 1 - c)

        def dst(a, px, py, pc):
            return _blk(outs[a], kinds[a], 4 * px + 2 * py + pc, sizes[a])

        def copy(a, slot, src, block, to):
            return pltpu.make_async_remote_copy(src_ref=src, dst_ref=dst(a, *block), send_sem=send_sems.at[a, slot],
                                                recv_sem=recv_sems.at[a, slot], device_id=to, device_id_type=MESH)

        local = [pltpu.make_async_copy(ins[a], dst(a, x, y, c), loc_sems.at[a]) for a in range(n)]
        for cp in local:
            cp.start()
        sends = []
        for a in range(n):
            sends.append(copy(a, 0, ins[a], (x, y, c), sibling))
            for j in (1, 2, 3):
                px, py = _chip_peer(x, y, j)
                sends.append(copy(a, j, ins[a], (x, y, c), (px, py, c)))
        for cp in sends:
            cp.start()
        for j in (1, 2, 3):
            px, py = _chip_peer(x, y, j)
            for a in range(n):
                copy(a, j, dst(a, px, py, c), (px, py, c), (x, y, c)).wait_recv()
                fwd = copy(a, 3 + j, dst(a, px, py, c), (px, py, c), sibling)
                fwd.start()
                sends.append(fwd)
        for a in range(n):
            copy(a, 0, ins[a], (x, y, 1 - c), (x, y, c)).wait_recv()
            for j in (1, 2, 3):
                px, py = _chip_peer(x, y, j)
                copy(a, 3 + j, ins[a], (px, py, 1 - c), (x, y, c)).wait_recv()
        for cp in sends:
            cp.wait_send()
        for cp in local:
            cp.wait()

    any_spec = pl.BlockSpec(memory_space=pl.ANY)
    return pl.pallas_call(
        body, name=name,
        in_specs=[any_spec] * n, out_specs=[any_spec] * n, out_shape=fulls,
        scratch_shapes=[pltpu.SemaphoreType.DMA((n, 7)), pltpu.SemaphoreType.DMA((n, 7)), pltpu.SemaphoreType.DMA((n,))],
        compiler_params=pltpu.CompilerParams(has_side_effects=True),
    )(*shards)


def _sibling_exchange(grads, kinds, sizes, name):
    n = len(grads)

    def blk_shape(a):
        g = grads[a]
        return (sizes[a],) + g.shape[1:] if kinds[a] == "rows" else (g.shape[0], sizes[a])

    def body(*refs):
        ins, outs = refs[:n], refs[n:2 * n]
        send_sems, recv_sems = refs[2 * n:]
        x, y, c = _me()
        copies = []
        for a in range(n):
            for q in range(N_CHIP):
                src = _blk(ins[a], kinds[a], 2 * q + (1 - c), sizes[a])
                copies.append(pltpu.make_async_remote_copy(
                    src_ref=src, dst_ref=outs[a].at[q], send_sem=send_sems.at[a, q], recv_sem=recv_sems.at[a, q],
                    device_id=(x, y, 1 - c), device_id_type=MESH))
        for cp in copies:
            cp.start()
        for cp in copies:
            cp.wait()

    any_spec = pl.BlockSpec(memory_space=pl.ANY)
    return pl.pallas_call(
        body, name=name,
        in_specs=[any_spec] * n, out_specs=[any_spec] * n,
        out_shape=[SDS((N_CHIP,) + blk_shape(a), F32) for a in range(n)],
        scratch_shapes=[pltpu.SemaphoreType.DMA((n, N_CHIP)), pltpu.SemaphoreType.DMA((n, N_CHIP))],
        compiler_params=pltpu.CompilerParams(has_side_effects=True),
    )(*grads)


def _chip_partial(g, r1, kind, size, cidx, name):
    if kind == "rows":
        rows, cols = size, g.shape[1]
        g3 = g.reshape(N_DEV, rows, cols)
        rb = _pick(rows, 512, 16)
        g_spec = pl.BlockSpec((1, rb, cols), lambda q, j, c: (2 * q + c[0], j, 0))
        grid = (N_CHIP, rows // rb)
        blk = (1, rb, cols)
        imap = lambda q, j, c: (q, j, 0)
    else:
        rows, cols = g.shape[0], size
        g3 = g
        g_spec = pl.BlockSpec((rows, cols), lambda q, j, c: (0, 2 * q + c[0]))
        grid = (N_CHIP, 1)
        blk = (1, rows, cols)
        imap = lambda q, j, c: (q, 0, 0)

    def body(c_ref, g_ref, r_ref, p_ref, pb_ref):
        s = g_ref[...].reshape(blk) + r_ref[...]
        p_ref[...] = s
        pb_ref[...] = s.astype(BF16)

    return pl.pallas_call(
        body, name=name,
        grid_spec=pltpu.PrefetchScalarGridSpec(
            num_scalar_prefetch=1, grid=grid,
            in_specs=[g_spec, pl.BlockSpec(blk, imap)],
            out_specs=[pl.BlockSpec(blk, imap), pl.BlockSpec(blk, imap)]),
        out_shape=[SDS((N_CHIP, rows, cols), F32), SDS((N_CHIP, rows, cols), BF16)],
        compiler_params=_cparams(("arbitrary", "arbitrary")),
    )(cidx, g3, r1)


def _chip_exchange(parts, name):
    n = len(parts)

    def body(*refs):
        ins, outs = refs[:n], refs[n:2 * n]
        send_sems, recv_sems = refs[2 * n:]
        x, y, c = _me()
        copies = []
        for a in range(n):
            for j in (1, 2, 3):
                px, py = _chip_peer(x, y, j)
                copies.append(pltpu.make_async_remote_copy(
                    src_ref=ins[a].at[2 * px + py], dst_ref=outs[a].at[j - 1], send_sem=send_sems.at[a, j - 1],
                    recv_sem=recv_sems.at[a, j - 1], device_id=(px, py, c), device_id_type=MESH))
        for cp in copies:
            cp.start()
        for cp in copies:
            cp.wait()

    any_spec = pl.BlockSpec(memory_space=pl.ANY)
    return pl.pallas_call(
        body, name=name,
        in_specs=[any_spec] * n, out_specs=[any_spec] * n,
        out_shape=[SDS((N_CHIP - 1,) + p.shape[1:], BF16) for p in parts],
        scratch_shapes=[pltpu.SemaphoreType.DMA((n, N_CHIP - 1)), pltpu.SemaphoreType.DMA((n, N_CHIP - 1))],
        compiler_params=pltpu.CompilerParams(has_side_effects=True),
    )(*parts)


def _grad_total(part, r2, qidx, name):
    _, rows, cols = part.shape
    rb = _pick(rows, 512, 16)

    def body(q_ref, p_ref, r_ref, o_ref):
        s = p_ref[0]
        for j in range(N_CHIP - 1):
            s = s + r_ref[j].astype(F32)
        o_ref[...] = s

    return pl.pallas_call(
        body, name=name,
        grid_spec=pltpu.PrefetchScalarGridSpec(
            num_scalar_prefetch=1, grid=(rows // rb,),
            in_specs=[pl.BlockSpec((1, rb, cols), lambda i, q: (q[0], i, 0)),
                      pl.BlockSpec((N_CHIP - 1, rb, cols), lambda i, q: (0, i, 0))],
            out_specs=pl.BlockSpec((rb, cols), lambda i, q: (i, 0))),
        out_shape=SDS((rows, cols), F32),
        compiler_params=_cparams(("arbitrary",)),
    )(qidx, part, r2)


def _all_reduce_small(pack, name):
    rows = pack.shape[0]
    rs = rows // N_DEV
    assert rs * N_DEV == rows and rs % 8 == 0

    def body(x_ref, o_ref, rbuf, red, send1, recv1, send2, recv2):
        x, y, c = _me()
        me = 4 * x + 2 * y + c

        def peer(d):
            px = (1 - x) if (d >> 2) & 1 else x
            py = (1 - y) if (d >> 1) & 1 else y
            pc = (1 - c) if d & 1 else c
            return px, py, pc

        def sl(ref, k):
            return ref.at[pl.ds(pl.multiple_of(k * rs, 8), rs)]

        phase1 = []
        for d in range(1, N_DEV):
            px, py, pc = peer(d)
            phase1.append(pltpu.make_async_remote_copy(
                src_ref=sl(x_ref, 4 * px + 2 * py + pc), dst_ref=rbuf.at[d], send_sem=send1.at[d], recv_sem=recv1.at[d],
                device_id=(px, py, pc), device_id_type=MESH))
        for cp in phase1:
            cp.start()
        acc = sl(x_ref, me)[...]
        for cp in phase1:
            cp.wait()
        for d in range(1, N_DEV):
            acc = acc + rbuf[d]
        red[...] = acc
        sl(o_ref, me)[...] = acc
        phase2 = []
        for d in range(1, N_DEV):
            px, py, pc = peer(d)
            phase2.append(pltpu.make_async_remote_copy(
                src_ref=red, dst_ref=sl(o_ref, me), send_sem=send2.at[d], recv_sem=recv2.at[d],
                device_id=(px, py, pc), device_id_type=MESH))
        for cp in phase2:
            cp.start()
        for cp in phase2:
            cp.wait()

    vm = pl.BlockSpec(memory_space=pltpu.VMEM)
    return pl.pallas_call(
        body, name=name, in_specs=[vm], out_specs=vm, out_shape=SDS(pack.shape, F32),
        scratch_shapes=[pltpu.VMEM((N_DEV, rs, LANES), F32), pltpu.VMEM((rs, LANES), F32),
                        pltpu.SemaphoreType.DMA((N_DEV,)), pltpu.SemaphoreType.DMA((N_DEV,)),
                        pltpu.SemaphoreType.DMA((N_DEV,)), pltpu.SemaphoreType.DMA((N_DEV,))],
        compiler_params=_cparams(None, has_side_effects=True),
    )(pack)


def _adamw(w, g, m, v, name):
    rows, cols = w.shape
    rb = _pick(rows, 256, 8)
    c1 = 1.0 / (1.0 - ADAM_B1 ** ADAM_STEP)
    c2 = 1.0 / (1.0 - ADAM_B2 ** ADAM_STEP)

    def body(w_ref, g_ref, m_ref, v_ref, d_ref, mo_ref, vo_ref):
        gv = g_ref[...]
        mn = ADAM_B1 * m_ref[...] + (1.0 - ADAM_B1) * gv
        vn = ADAM_B2 * v_ref[...] + (1.0 - ADAM_B2) * (gv * gv)
        mo_ref[...] = mn
        vo_ref[...] = vn
        d_ref[...] = -ADAM_LR * ((mn * c1) / (jnp.sqrt(vn * c2) + ADAM_EPS) + ADAM_WD * w_ref[...])

    spec = pl.BlockSpec((rb, cols), lambda i: (i, 0))
    return pl.pallas_call(
        body, name=name, grid=(rows // rb,),
        in_specs=[spec] * 4, out_specs=[spec] * 3, out_shape=[SDS((rows, cols), F32)] * 3,
        compiler_params=_cparams(("arbitrary",)),
    )(w, g, m, v)


def _pad_rows(a, mult=8):
    r = (-a.shape[0]) % mult
    return a if r == 0 else jnp.pad(a, ((0, r), (0, 0)))


def _as_lanes(a):
    flat = a.reshape(-1)
    pad = (-flat.shape[0]) % (8 * LANES)
    if pad:
        flat = jnp.pad(flat, (0, pad))
    return flat.reshape(-1, LANES)


def kernel(x, norm_g, w_in, ln_g, ln_b, w_s, b_s, conv_w, conv_b, w_pool, pool_scale, w_pa, w_pb, w_pc, w_o, final_g, loss_target, m_norm_g, m_w_in, m_ln_g, m_ln_b, m_w_s, m_b_s, m_conv_w, m_conv_b, m_w_pool, m_pool_scale, m_w_pa, m_w_pb, m_w_pc, m_w_o, m_final_g, v_norm_g, v_w_in, v_ln_g, v_ln_b, v_w_s, v_b_s, v_conv_w, v_conv_b, v_w_pool, v_pool_scale, v_w_pa, v_w_pb, v_w_pc, v_w_o, v_final_g):
    L = w_in.shape[0]
    D = x.shape[-1]
    n_loc = w_in.shape[2]
    pc_loc = w_pa.shape[2]
    x0 = x[0]
    target = loss_target[0]
    xi, yi, ci = _me()
    cidx = jnp.reshape(ci, (1,)).astype(jnp.int32)
    qidx = jnp.reshape(2 * xi + yi, (1,)).astype(jnp.int32)

    shards, kinds = [], []
    for l in range(L):
        shards += [w_in[l].T.astype(BF16), w_pa[l].astype(BF16), w_pb[l].astype(BF16), w_pc[l].astype(BF16),
                   w_o[l].astype(BF16)]
        kinds += ["rows", "cols", "cols", "cols", "rows"]
    cw_loc = _pad_rows(conv_w.reshape(L * CONV_TAPS, -1))
    cw_loc = jnp.pad(cw_loc, ((0, 0), (0, LANES - cw_loc.shape[1])))
    shards.append(cw_loc)
    kinds.append("rows")
    full = _all_gather(shards, kinds, "weights_all_gather")
    cw_all = full[-1].reshape(N_DEV, -1, LANES)[:, :L * CONV_TAPS, :conv_w.shape[2]]
    conv_w_full = jnp.transpose(cw_all, (1, 0, 2)).reshape(L, CONV_TAPS, -1)

    causal = jnp.tril(jnp.ones((CHUNK, CHUNK), dtype=bool))
    layers = []
    for l in range(L):
        win_t, wpa, wpb, wpc, wo = full[5 * l:5 * l + 5]
        wm = jnp.where(causal, w_s[l], 0.0)
        cvec = jnp.concatenate([ln_g[l][None], ln_b[l][None], conv_w_full[l], conv_b[l][None], pool_scale[l][None],
                                jnp.zeros((C_ROWS - 7, SEG), F32)], axis=0)
        layers.append(dict(
            win_t=win_t, wpa=wpa, wpb=wpb, wpc=wpc, wo=wo, cvec=cvec,
            bsb=jnp.repeat(b_s[l].T, HEAD, axis=1),
            wcat=jnp.transpose(wm, (1, 0, 2)).reshape(CHUNK, GROUPS * CHUNK).astype(BF16),
            wcatt=jnp.transpose(wm, (2, 0, 1)).reshape(CHUNK, GROUPS * CHUNK).astype(BF16),
            wpool=w_pool[l].astype(BF16)))

    xs, saved = [x0], []
    for l in range(L):
        lw = layers[l]
        p, h = _inproj(xs[-1], norm_g[l], lw["win_t"], f"inproj_fwd_{l}")
        xn, ya, yb, yc = _mix_fwd(p, xs[-1], lw, f"mix_fwd_{l}")
        saved.append((p, h, ya, yb, yc))
        xs.append(xn)
    dx, loss_acc, dfg_acc = _loss_head(xs[-1], final_g, target, "loss_head")

    rs_kinds = ["rows", "cols", "cols", "cols", "rows"]
    rs_sizes = [n_loc, pc_loc, pc_loc, pc_loc, w_o.shape[1]]
    big_grads = [None] * L
    small = [None] * L
    for l in reversed(range(L)):
        lw = layers[l]
        p, h, ya, yb, yc = saved[l]
        dp, gwo, gwpa, gwpb, gwpc, gwc, gbs, gwpool, gvec = _mix_bwd(p, dx, ya, yb, yc, lw, f"mix_bwd_{l}")
        dx, dng = _inproj_bwd_x(dp, lw["win_t"], xs[l], norm_g[l], dx, f"inproj_bwd_x_{l}")
        gwin_t = _inproj_bwd_w(dp, h, f"inproj_bwd_w_{l}")
        grads = [gwin_t, gwpa, gwpb, gwpc, gwo]
        r1 = _sibling_exchange(grads, rs_kinds, rs_sizes, f"grad_sibling_exchange_{l}")
        parts = [_chip_partial(g, r, k, s, cidx, f"grad_chip_partial_{l}_{a}")
                 for a, (g, r, k, s) in enumerate(zip(grads, r1, rs_kinds, rs_sizes))]
        r2 = _chip_exchange([pb for _, pb in parts], f"grad_chip_exchange_{l}")
        tot = [_grad_total(pf, r, qidx, f"grad_total_{l}_{a}") for a, ((pf, _), r) in enumerate(zip(parts, r2))]
        big_grads[l] = [tot[0].T,
                        tot[1].reshape(SEG, pc_loc), tot[2].reshape(SEG, pc_loc), tot[3].reshape(SEG, pc_loc),
                        tot[4]]
        small[l] = dict(norm_g=dng[0], ln_g=gvec[V_LNG], ln_b=gvec[V_LNB], w_s=gwc, b_s=gbs, conv_w=gvec[V_CW0:V_CW0 + 3],
                        conv_b=gvec[V_CB], w_pool=gwpool, pool_scale=gvec[V_PS])
    grad_x = dx[None]

    names = ["norm_g", "ln_g", "ln_b", "w_s", "b_s", "conv_w", "conv_b", "w_pool", "pool_scale"]
    pieces = [_as_lanes(jnp.stack([small[l][nm] for l in range(L)])) for nm in names]
    pieces += [_as_lanes(dfg_acc[0]), loss_acc]
    sizes = [pc.shape[0] for pc in pieces]
    pack = jnp.concatenate(pieces, axis=0)
    pack = _pad_rows(pack, 8 * N_DEV)
    red = _all_reduce_small(pack, "small_grads_all_reduce")
    offs = [0]
    for s in sizes:
        offs.append(offs[-1] + s)

    def unpack(i, shape):
        n = math.prod(shape)
        return red[offs[i]:offs[i + 1]].reshape(-1)[:n].reshape(shape)

    g_norm_g = unpack(0, (L, D))
    g_ln_g = unpack(1, (L, SEG))
    g_ln_b = unpack(2, (L, SEG))
    g_w_s = unpack(3, (L, GROUPS, CHUNK, CHUNK))
    g_b_s = jnp.transpose(unpack(4, (L, CHUNK, LANES))[:, :, :GROUPS], (0, 2, 1))
    g_conv_w_full = unpack(5, (L, CONV_TAPS, SEG))
    g_conv_b = unpack(6, (L, SEG))
    g_w_pool = unpack(7, (L, len(POOL_WINDOWS), POOL_GROUP, POOL_GROUP))
    g_pool_scale = unpack(8, (L, SEG))
    g_final_g = unpack(9, (D,))
    loss = red[offs[10], 0]
    dev = 4 * xi + 2 * yi + ci
    g_conv_w = lax.dynamic_slice_in_dim(g_conv_w_full, dev * conv_w.shape[2], conv_w.shape[2], axis=2)

    g_w_in = jnp.stack([big_grads[l][0] for l in range(L)])
    g_w_pa = jnp.stack([big_grads[l][1] for l in range(L)])
    g_w_pb = jnp.stack([big_grads[l][2] for l in range(L)])
    g_w_pc = jnp.stack([big_grads[l][3] for l in range(L)])
    g_w_o = jnp.stack([big_grads[l][4] for l in range(L)])

    grads = dict(norm_g=g_norm_g, w_in=g_w_in, ln_g=g_ln_g, ln_b=g_ln_b, w_s=g_w_s, b_s=g_b_s, conv_w=g_conv_w,
                 conv_b=g_conv_b, w_pool=g_w_pool, pool_scale=g_pool_scale, w_pa=g_w_pa, w_pb=g_w_pb, w_pc=g_w_pc,
                 w_o=g_w_o, final_g=g_final_g)
    weights = dict(norm_g=norm_g, w_in=w_in, ln_g=ln_g, ln_b=ln_b, w_s=w_s, b_s=b_s, conv_w=conv_w, conv_b=conv_b,
                   w_pool=w_pool, pool_scale=pool_scale, w_pa=w_pa, w_pb=w_pb, w_pc=w_pc, w_o=w_o, final_g=final_g)
    ms = dict(norm_g=m_norm_g, w_in=m_w_in, ln_g=m_ln_g, ln_b=m_ln_b, w_s=m_w_s, b_s=m_b_s, conv_w=m_conv_w,
              conv_b=m_conv_b, w_pool=m_w_pool, pool_scale=m_pool_scale, w_pa=m_w_pa, w_pb=m_w_pb, w_pc=m_w_pc,
              w_o=m_w_o, final_g=m_final_g)
    vs = dict(norm_g=v_norm_g, w_in=v_w_in, ln_g=v_ln_g, ln_b=v_ln_b, w_s=v_w_s, b_s=v_b_s, conv_w=v_conv_w,
              conv_b=v_conv_b, w_pool=v_w_pool, pool_scale=v_pool_scale, w_pa=v_w_pa, w_pb=v_w_pb, w_pc=v_w_pc,
              w_o=v_w_o, final_g=v_final_g)
    order = ["norm_g", "w_in", "ln_g", "ln_b", "w_s", "b_s", "conv_w", "conv_b", "w_pool", "pool_scale", "w_pa", "w_pb",
             "w_pc", "w_o", "final_g"]

    delta, new_m, new_v = {}, {}, {}
    big = ["w_in", "w_pa", "w_pb", "w_pc", "w_o"]
    for nm in big:
        shp = weights[nm].shape
        two = lambda a: a.reshape(-1, shp[-1])
        d, mn, vn = _adamw(two(weights[nm]), two(grads[nm]), two(ms[nm]), two(vs[nm]), f"adamw_{nm}")
        delta[nm], new_m[nm], new_v[nm] = d.reshape(shp), mn.reshape(shp), vn.reshape(shp)
    rest = [nm for nm in order if nm not in big]
    cat = lambda src: jnp.concatenate([_as_lanes(src[nm]) for nm in rest], axis=0)
    d, mn, vn = _adamw(cat(weights), cat(grads), cat(ms), cat(vs), "adamw_small")
    off = 0
    for nm in rest:
        shp = weights[nm].shape
        n = math.prod(shp)
        rows = _as_lanes(weights[nm]).shape[0]
        cut = lambda a: a[off:off + rows].reshape(-1)[:n].reshape(shp)
        delta[nm], new_m[nm], new_v[nm] = cut(d), cut(mn), cut(vn)
        off += rows

    return (loss, grad_x, *[grads[nm] for nm in order], *[delta[nm] for nm in order],
            *[new_m[nm] for nm in order], *[new_v[nm] for nm in order])
```

```python
import functools
import math

import jax
import jax.numpy as jnp
from jax import lax
from jax.experimental import pallas as pl
from jax.experimental.pallas import tpu as pltpu

F32 = jnp.float32
BF16 = jnp.bfloat16
SDS = jax.ShapeDtypeStruct
MESH = pl.DeviceIdType.MESH

SEG = 512
CHUNK = 128
GROUPS = 8
HEAD = SEG // GROUPS
POOL_WINDOWS = (2, 4, 8, 16)
POOL_GROUP = SEG // len(POOL_WINDOWS)
CONV_TAPS = 3
HALO = 16
RMS_EPS = 1e-6
LN_EPS = 1e-5
ADAM_LR, ADAM_B1, ADAM_B2, ADAM_EPS, ADAM_WD, ADAM_STEP = 0.001, 0.9, 0.999, 1e-08, 0.01, 10

O_U, O_V, O_ZA, O_XB, O_BG, O_CG, O_ZB, O_XC, O_ZC, O_G = (SEG * i for i in range(10))

N_DEV = 8
N_CHIP = 4
LANES = 128
VMEM_LIMIT = 48 * 1024 * 1024
ADAMW_WHOLE_BYTES = 2 * 1024 * 1024


def _cparams(sem=None, **kw):
    return pltpu.CompilerParams(dimension_semantics=sem, vmem_limit_bytes=VMEM_LIMIT, **kw)


def _pick(total, target, mult):
    best = None
    for d in range(mult, min(total, target) + 1, mult):
        if total % d == 0:
            best = d
    assert best is not None, (total, target, mult)
    return best


def _dot(a, b):
    return jnp.dot(a, b, preferred_element_type=F32)


def _dot_nt(a, b):
    return lax.dot_general(a, b, (((1,), (1,)), ((), ())), preferred_element_type=F32)


def _dot_tn(a, b):
    return lax.dot_general(a, b, (((0,), (0,)), ((), ())), preferred_element_type=F32)


def _zero(ref):
    ref[...] = jnp.zeros(ref.shape, ref.dtype)


def _sigmoid(x):
    return 1.0 / (1.0 + jnp.exp(-x))


_GELU_C = math.sqrt(2.0 / math.pi)


def _gelu(x):
    t = jnp.tanh(_GELU_C * (x + 0.044715 * x * x * x))
    return 0.5 * x * (1.0 + t), t


def _gelu_grad(x, t):
    return 0.5 * (1.0 + t) + 0.5 * x * (1.0 - t * t) * _GELU_C * (1.0 + 3.0 * 0.044715 * x * x)


def _me():
    return lax.axis_index("x"), lax.axis_index("y"), lax.axis_index("c")


def _inproj(x, norm_g, win_t, name, riders=()):
    T, D = x.shape
    N = win_t.shape[0]
    bt = _pick(T, 1024, 16)
    bn = _pick(N, 1536, LANES)
    grid = (T // bt, N // bn)

    def compute(x_ref, g_ref, w_ref, p_ref, h_ref, hs_ref):
        @pl.when(pl.program_id(1) == 0)
        def _():
            xv = x_ref[...]
            rstd = lax.rsqrt(jnp.mean(xv * xv, axis=-1, keepdims=True) + RMS_EPS)
            hb = (xv * rstd * g_ref[...]).astype(BF16)
            hs_ref[...] = hb
            h_ref[...] = hb

        p_ref[...] = _dot_nt(hs_ref[...], w_ref[...]).astype(BF16)

    return _host_call(
        compute, name, grid, riders,
        inputs=[x, norm_g.reshape(1, D), win_t],
        in_specs=[pl.BlockSpec((bt, D), lambda i, j: (i, 0)),
                  pl.BlockSpec((1, D), lambda i, j: (0, 0)),
                  pl.BlockSpec((bn, D), lambda i, j: (j, 0))],
        out_specs=[pl.BlockSpec((bt, bn), lambda i, j: (i, j)),
                   pl.BlockSpec((bt, D), lambda i, j: (i, 0))],
        out_shape=[SDS((T, N), BF16), SDS((T, D), BF16)],
        scratch_shapes=[pltpu.VMEM((bt, D), BF16)])


C_LNG, C_LNB, C_CW0, C_CW1, C_CW2, C_CB, C_PS = range(7)
C_ROWS = 8


def _mixers(p_ref, hxb_ref, hcg_ref, hxc_ref, cv, bsb_ref, wcat_ref, wpool_ref, extb, extc,
            first, blk, R, need_grad):
    def seg(lo):
        return p_ref[:, lo:lo + SEG].astype(F32)

    u, v, za = seg(O_U), seg(O_V), seg(O_ZA)
    xb, bg, cg, zb = seg(O_XB), seg(O_BG), seg(O_CG), seg(O_ZB)
    xc, zc = seg(O_XC), seg(O_ZC)
    out = {}

    ug, tu = _gelu(u)
    vg, tv = _gelu(v)
    mu = jnp.mean(vg, axis=-1, keepdims=True)
    vcen = vg - mu
    rs = lax.rsqrt(jnp.mean(vcen * vcen, axis=-1, keepdims=True) + LN_EPS)
    vhat = vcen * rs
    vn = (vhat * cv[C_LNG:C_LNG + 1, :] + cv[C_LNB:C_LNB + 1, :]).astype(BF16)
    lane_group = lax.broadcasted_iota(jnp.int32, (CHUNK, SEG), 1) // HEAD
    zero_b = jnp.zeros((CHUNK, SEG), BF16)
    sgs = []
    for ci in range(R // CHUNK):
        vc = vn[ci * CHUNK:(ci + 1) * CHUNK]
        vst = jnp.concatenate([jnp.where(lane_group == g, vc, zero_b) for g in range(GROUPS)], axis=0)
        sgs.append(_dot(wcat_ref[...], vst) + bsb_ref[...])
    sg = sgs[0] if len(sgs) == 1 else jnp.concatenate(sgs, axis=0)
    a_out = ug * sg
    sa = _sigmoid(za)
    out["a"] = a_out * (za * sa)

    cx = cg * xb
    halo_b = hcg_ref[...].astype(F32) * hxb_ref[...].astype(F32)
    extb[0:HALO, :] = jnp.where(first, 0.0, halo_b)
    extb[HALO:HALO + R, :] = cx
    cx1 = extb[pl.ds(HALO - 1, R), :]
    cx2 = extb[pl.ds(HALO - 2, R), :]
    yconv = (cv[C_CW0:C_CW0 + 1, :] * cx2 + cv[C_CW1:C_CW1 + 1, :] * cx1
             + cv[C_CW2:C_CW2 + 1, :] * cx + cv[C_CB:C_CB + 1, :])
    b_out = bg * yconv
    sb = _sigmoid(zb)
    out["b"] = b_out * (zb * sb)

    extc[0:HALO, :] = jnp.where(first, 0.0, hxc_ref[...].astype(F32))
    extc[HALO:HALO + R, :] = xc
    tpos = blk * R + lax.broadcasted_iota(jnp.int32, (R, POOL_GROUP), 0) + 1
    pooled, invs, pws = [], [], []
    for gi, w in enumerate(POOL_WINDOWS):
        lo = gi * POOL_GROUP
        win = xc[:, lo:lo + POOL_GROUP]
        for j in range(1, w):
            win = win + extc[pl.ds(HALO - j, R), lo:lo + POOL_GROUP]
        inv = 1.0 / jnp.minimum(tpos, w).astype(F32)
        pg = (win * inv - xc[:, lo:lo + POOL_GROUP]).astype(BF16)
        pooled.append(pg)
        invs.append(inv)
        pws.append(_dot(pg, wpool_ref[gi]))
    pw = jnp.concatenate(pws, axis=1)
    c_out = pw * cv[C_PS:C_PS + 1, :]
    sc = _sigmoid(zc)
    out["c"] = c_out * (zc * sc)

    if need_grad:
        out.update(u=u, v=v, tu=tu, tv=tv, ug=ug, sg=sg, a_out=a_out, za=za, sa=sa,
                   rs=rs, vhat=vhat, vn=vn, lane_group=lane_group, zero_b=zero_b,
                   xb=xb, bg=bg, cg=cg, cx=cx, cx1=cx1, cx2=cx2, yconv=yconv, b_out=b_out, zb=zb, sb=sb,
                   pooled=pooled, invs=invs, pw=pw, c_out=c_out, zc=zc, sc=sc)
    return out


def _halo_specs(R, nb, rev):
    step = R // HALO

    def mk(col):
        def imap(i):
            b = (nb - 1 - i) if rev else i
            return (jnp.maximum(b * step - 1, 0), col)
        return pl.BlockSpec((HALO, SEG), imap)

    return [mk(O_XB // SEG), mk(O_CG // SEG), mk(O_XC // SEG)]


def _const_spec(shape):
    nd = len(shape)
    return pl.BlockSpec(shape, lambda i: (0,) * nd, pipeline_mode=pl.Buffered(1))


def _mix_block_rows(T):
    return _pick(T, 128, CHUNK)


def _mix_fwd(p, x, lw, name):
    T, D = x.shape
    N = p.shape[1]
    R = _mix_block_rows(T)
    nb = T // R

    def body(p_ref, hxb, hcg, hxc, x_ref, cv_ref, bsb_ref, wcat_ref, wpool_ref, wpa_ref, wpb_ref, wpc_ref,
             wo_ref, xo_ref, ya_ref, yb_ref, yc_ref, extb, extc):
        i = pl.program_id(0)
        cv = cv_ref[...]
        r = _mixers(p_ref, hxb, hcg, hxc, cv, bsb_ref, wcat_ref, wpool_ref, extb, extc,
                    i == 0, i, R, False)
        ya = _dot(r["a"].astype(BF16), wpa_ref[...])
        yb = _dot(r["b"].astype(BF16), wpb_ref[...])
        yc = _dot(r["c"].astype(BF16), wpc_ref[...])
        ga = p_ref[:, O_G:O_G + D].astype(F32)
        gb = p_ref[:, O_G + D:O_G + 2 * D].astype(F32)
        gc = p_ref[:, O_G + 2 * D:O_G + 3 * D].astype(F32)
        merged = _sigmoid(ga) * ya + _sigmoid(gb) * yb + _sigmoid(gc) * yc
        xo_ref[...] = x_ref[...] + _dot(merged.astype(BF16), wo_ref[...])
        ya_ref[...] = ya.astype(BF16)
        yb_ref[...] = yb.astype(BF16)
        yc_ref[...] = yc.astype(BF16)

    row = lambda w: pl.BlockSpec((R, w), lambda i: (i, 0))
    consts = [lw["cvec"], lw["bsb"], lw["wcat"], lw["wpool"], lw["wpa"], lw["wpb"], lw["wpc"], lw["wo"]]
    return pl.pallas_call(
        body, name=name, grid=(nb,),
        in_specs=[row(N)] + _halo_specs(R, nb, False) + [row(D)] + [_const_spec(c.shape) for c in consts],
        out_specs=[row(D), row(D), row(D), row(D)],
        out_shape=[SDS((T, D), F32), SDS((T, D), BF16), SDS((T, D), BF16), SDS((T, D), BF16)],
        scratch_shapes=[pltpu.VMEM((HALO + R, SEG), F32), pltpu.VMEM((HALO + R, SEG), F32)],
        compiler_params=_cparams(("arbitrary",)),
    )(p, p, p, p, x, *consts)


def _loss_head(x, final_g, target, name):
    T, D = x.shape
    bt = _pick(T, 512, 8)

    def body(x_ref, g_ref, t_ref, dx_ref, loss_ref, dg_ref):
        @pl.when(pl.program_id(0) == 0)
        def _():
            _zero(loss_ref)
            _zero(dg_ref)

        xv = x_ref[...]
        g = g_ref[...]
        rstd = lax.rsqrt(jnp.mean(xv * xv, axis=-1, keepdims=True) + RMS_EPS)
        xhat = xv * rstd
        err = xhat * g - t_ref[...]
        part = 0.5 * jnp.sum(jnp.sum(err * err, axis=-1, keepdims=True), axis=0, keepdims=True) / D
        loss_ref[...] += jnp.broadcast_to(part, loss_ref.shape)
        dy = err * (1.0 / D)
        dg_ref[0:1, :] += jnp.sum(dy * xhat, axis=0, keepdims=True)
        dxn = dy * g
        dx_ref[...] = rstd * (dxn - xhat * jnp.mean(dxn * xhat, axis=-1, keepdims=True))

    return pl.pallas_call(
        body, name=name, grid=(T // bt,),
        in_specs=[pl.BlockSpec((bt, D), lambda i: (i, 0)), _const_spec((1, D)), pl.BlockSpec((bt, D), lambda i: (i, 0))],
        out_specs=[pl.BlockSpec((bt, D), lambda i: (i, 0)), _const_spec((8, LANES)), _const_spec((8, D))],
        out_shape=[SDS((T, D), F32), SDS((8, LANES), F32), SDS((8, D), F32)],
        compiler_params=_cparams(("arbitrary",)),
    )(x, final_g.reshape(1, D), target)


V_LNG, V_LNB, V_CB, V_PS, V_CW0, V_CW1, V_CW2 = range(7)


def _mix_bwd(p, dxo, ya, yb, yc, lw, name):
    T, D = dxo.shape
    N = p.shape[1]
    R = _mix_block_rows(T)
    nb = T // R

    def body(p_ref, hxb, hcg, hxc, dxo_ref, ya_ref, yb_ref, yc_ref, cv_ref, bsb_ref, wcat_ref, wcatt_ref,
             wpool_ref, wpa_ref, wpb_ref, wpc_ref, wo_ref,
             dp_ref, gwo_ref, gwpa_ref, gwpb_ref, gwpc_ref, gwc_ref, gbs_ref, gwpool_ref, gvec_ref,
             extb, extc, extdy, extq, cdy, cq, bsacc):
        i = pl.program_id(0)
        blk = nb - 1 - i

        @pl.when(i == 0)
        def _():
            for ref in (gwo_ref, gwpa_ref, gwpb_ref, gwpc_ref, gwc_ref, gwpool_ref, gvec_ref, cdy, cq, bsacc):
                _zero(ref)

        cv = cv_ref[...]
        r = _mixers(p_ref, hxb, hcg, hxc, cv, bsb_ref, wcat_ref, wpool_ref, extb, extc,
                    blk == 0, blk, R, True)

        dxo_b = dxo_ref[...].astype(BF16)
        dm = _dot_nt(dxo_b, wo_ref[...])
        ys = [ya_ref[...].astype(F32), yb_ref[...].astype(F32), yc_ref[...].astype(F32)]
        sig = [_sigmoid(p_ref[:, O_G + k * D:O_G + (k + 1) * D].astype(F32)) for k in range(3)]
        merged = sig[0] * ys[0] + sig[1] * ys[1] + sig[2] * ys[2]
        gwo_ref[...] += _dot_tn(merged.astype(BF16), dxo_b)
        dacts = []
        for k, (act, w_ref, gw_ref) in enumerate(((r["a"], wpa_ref, gwpa_ref), (r["b"], wpb_ref, gwpb_ref),
                                                  (r["c"], wpc_ref, gwpc_ref))):
            dyk = (dm * sig[k]).astype(BF16)
            dp_ref[:, O_G + k * D:O_G + (k + 1) * D] = (dm * ys[k] * sig[k] * (1.0 - sig[k])).astype(BF16)
            gw_ref[...] += _dot_tn(act.astype(BF16), dyk)
            dacts.append(_dot_nt(dyk, w_ref[...]))
        da, db, dc = dacts

        def silu_bwd(dact, pre, z, s):
            return dact * (z * s), dact * pre * (s * (1.0 + z * (1.0 - s)))

        d_aout, dza = silu_bwd(da, r["a_out"], r["za"], r["sa"])
        dp_ref[:, O_ZA:O_ZA + SEG] = dza.astype(BF16)
        dp_ref[:, O_U:O_U + SEG] = (d_aout * r["sg"] * _gelu_grad(r["u"], r["tu"])).astype(BF16)
        d_sg = d_aout * r["ug"]
        dvns = []
        for ci in range(R // CHUNK):
            dsc = d_sg[ci * CHUNK:(ci + 1) * CHUNK]
            bsacc[...] += dsc
            dsc_b = dsc.astype(BF16)
            dst = jnp.concatenate([jnp.where(r["lane_group"] == g, dsc_b, r["zero_b"]) for g in range(GROUPS)], axis=0)
            dvns.append(_dot(wcatt_ref[...], dst))
            gwc_ref[...] += _dot_nt(dst, r["vn"][ci * CHUNK:(ci + 1) * CHUNK])
        d_vn = dvns[0] if len(dvns) == 1 else jnp.concatenate(dvns, axis=0)
        vhat = r["vhat"]
        gvec_ref[V_LNG:V_LNG + 1, :] += jnp.sum(d_vn * vhat, axis=0, keepdims=True)
        gvec_ref[V_LNB:V_LNB + 1, :] += jnp.sum(d_vn, axis=0, keepdims=True)
        d_vhat = d_vn * cv[C_LNG:C_LNG + 1, :]
        d_vg = r["rs"] * (d_vhat - jnp.mean(d_vhat, axis=-1, keepdims=True)
                          - vhat * jnp.mean(d_vhat * vhat, axis=-1, keepdims=True))
        dp_ref[:, O_V:O_V + SEG] = (d_vg * _gelu_grad(r["v"], r["tv"])).astype(BF16)

        d_bout, dzb = silu_bwd(db, r["b_out"], r["zb"], r["sb"])
        dp_ref[:, O_ZB:O_ZB + SEG] = dzb.astype(BF16)
        dp_ref[:, O_BG:O_BG + SEG] = (d_bout * r["yconv"]).astype(BF16)
        d_y = d_bout * r["bg"]
        gvec_ref[V_CB:V_CB + 1, :] += jnp.sum(d_y, axis=0, keepdims=True)
        gvec_ref[V_CW0:V_CW0 + 1, :] += jnp.sum(d_y * r["cx2"], axis=0, keepdims=True)
        gvec_ref[V_CW1:V_CW1 + 1, :] += jnp.sum(d_y * r["cx1"], axis=0, keepdims=True)
        gvec_ref[V_CW2:V_CW2 + 1, :] += jnp.sum(d_y * r["cx"], axis=0, keepdims=True)
        extdy[0:R, :] = d_y
        extdy[R:R + HALO, :] = cdy[...]
        d_cx = (cv[C_CW2:C_CW2 + 1, :] * d_y + cv[C_CW1:C_CW1 + 1, :] * extdy[pl.ds(1, R), :]
                + cv[C_CW0:C_CW0 + 1, :] * extdy[pl.ds(2, R), :])
        cdy[...] = d_y[0:HALO]
        dp_ref[:, O_CG:O_CG + SEG] = (d_cx * r["xb"]).astype(BF16)
        dp_ref[:, O_XB:O_XB + SEG] = (d_cx * r["cg"]).astype(BF16)

        d_cout, dzc = silu_bwd(dc, r["c_out"], r["zc"], r["sc"])
        dp_ref[:, O_ZC:O_ZC + SEG] = dzc.astype(BF16)
        gvec_ref[V_PS:V_PS + 1, :] += jnp.sum(d_cout * r["pw"], axis=0, keepdims=True)
        d_pw = (d_cout * cv[C_PS:C_PS + 1, :]).astype(BF16)
        dpool = []
        for gi, w in enumerate(POOL_WINDOWS):
            lo = gi * POOL_GROUP
            dpw_g = d_pw[:, lo:lo + POOL_GROUP]
            gwpool_ref[lo:lo + POOL_GROUP, :] += _dot_tn(r["pooled"][gi], dpw_g)
            dpg = _dot_nt(dpw_g, wpool_ref[gi])
            dpool.append(dpg)
            extq[0:R, lo:lo + POOL_GROUP] = dpg * r["invs"][gi]
        extq[R:R + HALO, :] = cq[...]
        for gi, w in enumerate(POOL_WINDOWS):
            lo = gi * POOL_GROUP
            acc = extq[0:R, lo:lo + POOL_GROUP]
            for j in range(1, w):
                acc = acc + extq[pl.ds(j, R), lo:lo + POOL_GROUP]
            dp_ref[:, O_XC + lo:O_XC + lo + POOL_GROUP] = (acc - dpool[gi]).astype(BF16)
        cq[...] = extq[0:HALO, :]

        @pl.when(i == nb - 1)
        def _():
            rr = lax.broadcasted_iota(jnp.int32, gwc_ref.shape, 0) % CHUNK
            cc = lax.broadcasted_iota(jnp.int32, gwc_ref.shape, 1)
            gwc_ref[...] = jnp.where(cc <= rr, gwc_ref[...], 0.0)
            acc = bsacc[...]
            hi = acc.astype(BF16)
            lo_ = (acc - hi.astype(F32)).astype(BF16)
            sel = (lax.broadcasted_iota(jnp.int32, (SEG, LANES), 0) // HEAD
                   == lax.broadcasted_iota(jnp.int32, (SEG, LANES), 1)).astype(BF16)
            gbs_ref[...] = _dot(hi, sel) + _dot(lo_, sel)

    row = lambda w: pl.BlockSpec((R, w), lambda i: (nb - 1 - i, 0))
    consts = [lw["cvec"], lw["bsb"], lw["wcat"], lw["wcatt"], lw["wpool"], lw["wpa"], lw["wpb"], lw["wpc"], lw["wo"]]
    acc_shapes = [(D, D), (SEG, D), (SEG, D), (SEG, D), (GROUPS * CHUNK, CHUNK), (CHUNK, LANES), (SEG, POOL_GROUP), (8, SEG)]
    return pl.pallas_call(
        body, name=name, grid=(nb,),
        in_specs=([row(N)] + _halo_specs(R, nb, True) + [row(D), row(D), row(D), row(D)]
                  + [_const_spec(c.shape) for c in consts]),
        out_specs=[row(N)] + [_const_spec(s) for s in acc_shapes],
        out_shape=[SDS((T, N), BF16)] + [SDS(s, F32) for s in acc_shapes],
        scratch_shapes=[pltpu.VMEM((HALO + R, SEG), F32)] * 4
        + [pltpu.VMEM((HALO, SEG), F32), pltpu.VMEM((HALO, SEG), F32), pltpu.VMEM((CHUNK, SEG), F32)],
        compiler_params=_cparams(("arbitrary",)),
    )(p, p, p, p, dxo, ya, yb, yc, *consts)


def _inproj_bwd_x(dp, win_t, x, norm_g, dxo, name, riders=()):
    T, D = x.shape
    N = dp.shape[1]
    bt = _pick(T, 1024, 16)
    bk = _pick(N, 1536, LANES)
    nk = N // bk

    def compute(dp_ref, w_ref, x_ref, g_ref, dxo_ref, dx_ref, dg_ref, acc_ref):
        i, k = pl.program_id(0), pl.program_id(1)

        @pl.when((i == 0) & (k == 0))
        def _():
            _zero(dg_ref)

        @pl.when(k == 0)
        def _():
            _zero(acc_ref)

        acc_ref[...] += _dot(dp_ref[...], w_ref[...])

        @pl.when(k == nk - 1)
        def _():
            dh = acc_ref[...]
            xv = x_ref[...]
            rstd = lax.rsqrt(jnp.mean(xv * xv, axis=-1, keepdims=True) + RMS_EPS)
            xhat = xv * rstd
            dg_ref[0:1, :] += jnp.sum(dh * xhat, axis=0, keepdims=True)
            dxn = dh * g_ref[...]
            dx_ref[...] = dxo_ref[...] + rstd * (dxn - xhat * jnp.mean(dxn * xhat, axis=-1, keepdims=True))

    return _host_call(
        compute, name, (T // bt, nk), riders,
        inputs=[dp, win_t, x, norm_g.reshape(1, D), dxo],
        in_specs=[pl.BlockSpec((bt, bk), lambda i, k: (i, k)),
                  pl.BlockSpec((bk, D), lambda i, k: (k, 0)),
                  pl.BlockSpec((bt, D), lambda i, k: (i, 0)),
                  pl.BlockSpec((1, D), lambda i, k: (0, 0)),
                  pl.BlockSpec((bt, D), lambda i, k: (i, 0))],
        out_specs=[pl.BlockSpec((bt, D), lambda i, k: (i, 0)), pl.BlockSpec((8, D), lambda i, k: (0, 0))],
        out_shape=[SDS((T, D), F32), SDS((8, D), F32)],
        scratch_shapes=[pltpu.VMEM((bt, D), F32)])


def _inproj_bwd_w(dp, h, name, riders=()):
    T, N = dp.shape
    D = h.shape[1]
    bn = _pick(N, 1920, LANES)
    bk = _pick(T, 1024, 16)
    nk = T // bk

    def compute(dp_ref, h_ref, o_ref):
        @pl.when(pl.program_id(1) == 0)
        def _():
            _zero(o_ref)

        o_ref[...] += _dot_tn(dp_ref[...], h_ref[...])

    return _host_call(
        compute, name, (N // bn, nk), riders,
        inputs=[dp, h],
        in_specs=[pl.BlockSpec((bk, bn), lambda j, k: (k, j)), pl.BlockSpec((bk, D), lambda j, k: (k, 0))],
        out_specs=[pl.BlockSpec((bn, D), lambda j, k: (j, 0))],
        out_shape=[SDS((N, D), F32)],
        scratch_shapes=[])


def _chip_peer(x, y, j):
    px = (1 - x) if (j >> 1) else x
    py = (1 - y) if (j & 1) else y
    return px, py


def _blk(ref, kind, k, n):
    if kind == "rows":
        return ref.at[pl.ds(pl.multiple_of(k * n, 8), n)]
    return ref.at[:, pl.ds(pl.multiple_of(k * n, LANES), n)]


class _Exchange:
    def __init__(self, srcs, out_shapes, n_sems, build, alias=False):
        self.srcs, self.out_shapes, self.n_sems, self.build, self.alias = list(srcs), list(out_shapes), n_sems, build, alias


def _rider_plan(riders):
    inputs = [s for e in riders for s in e.srcs]
    out_shapes = [o for e in riders for o in e.out_shapes]
    sems = [pltpu.SemaphoreType.DMA((e.n_sems,)) for e in riders for _ in range(2)]

    def copies(in_refs, out_refs, sem_refs):
        cps, i, o = [], 0, 0
        for k, e in enumerate(riders):
            ni, no = len(e.srcs), len(e.out_shapes)
            cps += e.build(in_refs[i:i + ni], out_refs[o:o + no], sem_refs[2 * k], sem_refs[2 * k + 1])
            i, o = i + ni, o + no
        return cps

    return inputs, out_shapes, sems, copies


_ANY = pl.BlockSpec(memory_space=pl.ANY)


def _host_call(compute, name, grid, riders, inputs, in_specs, out_specs, out_shape, scratch_shapes):
    r_in, r_out, r_sems, copies = _rider_plan(riders)
    ni, no, ns = len(inputs), len(out_shape), len(scratch_shapes)

    def body(*refs):
        ins, rins = refs[:ni], refs[ni:ni + len(r_in)]
        outs = refs[ni + len(r_in):ni + len(r_in) + no]
        routs = refs[ni + len(r_in) + no:ni + len(r_in) + no + len(r_out)]
        scr = refs[ni + len(r_in) + no + len(r_out):]
        first = functools.reduce(lambda a, b: a & b, [pl.program_id(d) == 0 for d in range(len(grid))])
        last = functools.reduce(lambda a, b: a & b, [pl.program_id(d) == grid[d] - 1 for d in range(len(grid))])
        if riders:
            @pl.when(first)
            def _():
                for cp in copies(rins, routs, scr[ns:]):
                    cp.start()

        compute(*ins, *outs, *scr[:ns])

        if riders:
            @pl.when(last)
            def _():
                for cp in copies(rins, routs, scr[ns:]):
                    cp.wait()

    res = pl.pallas_call(
        body, name=name, grid=grid,
        in_specs=list(in_specs) + [_ANY] * len(r_in),
        out_specs=list(out_specs) + [_ANY] * len(r_out),
        out_shape=list(out_shape) + r_out,
        scratch_shapes=list(scratch_shapes) + r_sems,
        compiler_params=_cparams(("arbitrary",) * len(grid)),
    )(*inputs, *r_in)
    return res[:no], _split_riders(riders, res[no:])


def _split_riders(riders, flat):
    out, o = [], 0
    for e in riders:
        out.append(list(flat[o:o + len(e.out_shapes)]))
        o += len(e.out_shapes)
    return out


def _run_exchange(ex, name):
    n_in, n_out = len(ex.srcs), len(ex.out_shapes)

    def body(*refs):
        cps = ex.build(refs[:n_in], refs[n_in:n_in + n_out], refs[n_in + n_out], refs[n_in + n_out + 1])
        for cp in cps:
            cp.start()
        for cp in cps:
            cp.wait()

    return pl.pallas_call(
        body, name=name,
        in_specs=[_ANY] * n_in, out_specs=[_ANY] * n_out, out_shape=ex.out_shapes,
        input_output_aliases={i: i for i in range(n_in)} if ex.alias else {},
        scratch_shapes=[pltpu.SemaphoreType.DMA((ex.n_sems,)), pltpu.SemaphoreType.DMA((ex.n_sems,))],
        compiler_params=pltpu.CompilerParams(has_side_effects=True),
    )(*ex.srcs)


def _gather_sizes(shards, kinds):
    sizes = [s.shape[0] if k == "rows" else s.shape[1] for s, k in zip(shards, kinds)]
    fulls = [SDS((s.shape[0] * N_DEV,) + s.shape[1:], s.dtype) if k == "rows"
             else SDS((s.shape[0], s.shape[1] * N_DEV), s.dtype) for s, k in zip(shards, kinds)]
    return sizes, fulls


def _gather_direct(shards, kinds):
    n = len(shards)
    sizes, fulls = _gather_sizes(shards, kinds)

    def build(ins, outs, send_sems, recv_sems):
        x, y, c = _me()
        cps = []
        for a in range(n):
            mine = _blk(outs[a], kinds[a], 4 * x + 2 * y + c, sizes[a])
            cps.append(pltpu.make_async_copy(ins[a], mine, send_sems.at[5 * a + 4]))
            for j in range(N_CHIP):
                to = (x, y, 1 - c) if j == 0 else (*_chip_peer(x, y, j), c)
                cps.append(pltpu.make_async_remote_copy(
                    src_ref=ins[a], dst_ref=mine, send_sem=send_sems.at[5 * a + j], recv_sem=recv_sems.at[5 * a + j],
                    device_id=to, device_id_type=MESH))
        return cps

    return _Exchange(shards, fulls, 5 * n, build)


def _gather_forward(fulls, kinds, sizes):
    n = len(fulls)

    def build(ins, outs, send_sems, recv_sems):
        x, y, c = _me()
        cps = []
        for a in range(n):
            for j in (1, 2, 3):
                px, py = _chip_peer(x, y, j)
                k = 4 * px + 2 * py + c
                cps.append(pltpu.make_async_remote_copy(
                    src_ref=_blk(ins[a], kinds[a], k, sizes[a]), dst_ref=_blk(outs[a], kinds[a], k, sizes[a]),
                    send_sem=send_sems.at[3 * a + j - 1], recv_sem=recv_sems.at[3 * a + j - 1],
                    device_id=(x, y, 1 - c), device_id_type=MESH))
        return cps

    return _Exchange(fulls, [SDS(f.shape, f.dtype) for f in fulls], 3 * n, build, alias=True)


def _sibling_exchange(grads, kinds, sizes):
    n = len(grads)

    def blk_shape(a):
        g = grads[a]
        return (sizes[a],) + g.shape[1:] if kinds[a] == "rows" else (g.shape[0], sizes[a])

    def build(ins, outs, send_sems, recv_sems):
        x, y, c = _me()
        cps = []
        for a in range(n):
            for q in range(N_CHIP):
                cps.append(pltpu.make_async_remote_copy(
                    src_ref=_blk(ins[a], kinds[a], 2 * q + (1 - c), sizes[a]), dst_ref=outs[a].at[q],
                    send_sem=send_sems.at[N_CHIP * a + q], recv_sem=recv_sems.at[N_CHIP * a + q],
                    device_id=(x, y, 1 - c), device_id_type=MESH))
        return cps

    return _Exchange(grads, [SDS((N_CHIP,) + blk_shape(a), F32) for a in range(n)], N_CHIP * n, build)


def _chip_partial(g, r1, kind, size, cidx, name):
    if kind == "rows":
        rows, cols = size, g.shape[1]
        g3 = g.reshape(N_DEV, rows, cols)
        rb = _pick(rows, 512, 16)
        g_spec = pl.BlockSpec((1, rb, cols), lambda q, j, c: (2 * q + c[0], j, 0))
        grid = (N_CHIP, rows // rb)
        blk = (1, rb, cols)
        imap = lambda q, j, c: (q, j, 0)
    else:
        rows, cols = g.shape[0], size
        g3 = g
        g_spec = pl.BlockSpec((rows, cols), lambda q, j, c: (0, 2 * q + c[0]))
        grid = (N_CHIP, 1)
        blk = (1, rows, cols)
        imap = lambda q, j, c: (q, 0, 0)

    def body(c_ref, g_ref, r_ref, p_ref, pb_ref):
        s = g_ref[...].reshape(blk) + r_ref[...]
        p_ref[...] = s
        pb_ref[...] = s.astype(BF16)

    return pl.pallas_call(
        body, name=name,
        grid_spec=pltpu.PrefetchScalarGridSpec(
            num_scalar_prefetch=1, grid=grid,
            in_specs=[g_spec, pl.BlockSpec(blk, imap)],
            out_specs=[pl.BlockSpec(blk, imap), pl.BlockSpec(blk, imap)]),
        out_shape=[SDS((N_CHIP, rows, cols), F32), SDS((N_CHIP, rows, cols), BF16)],
        compiler_params=_cparams(("arbitrary", "arbitrary")),
    )(cidx, g3, r1)


def _chip_exchange(parts):
    n = len(parts)
    m = N_CHIP - 1

    def build(ins, outs, send_sems, recv_sems):
        x, y, c = _me()
        cps = []
        for a in range(n):
            for j in (1, 2, 3):
                px, py = _chip_peer(x, y, j)
                cps.append(pltpu.make_async_remote_copy(
                    src_ref=ins[a].at[2 * px + py], dst_ref=outs[a].at[j - 1], send_sem=send_sems.at[m * a + j - 1],
                    recv_sem=recv_sems.at[m * a + j - 1], device_id=(px, py, c), device_id_type=MESH))
        return cps

    return _Exchange(parts, [SDS((m,) + p.shape[1:], BF16) for p in parts], m * n, build)


def _grad_total(part, r2, qidx, name):
    _, rows, cols = part.shape
    rb = _pick(rows, 512, 16)

    def body(q_ref, p_ref, r_ref, o_ref):
        s = p_ref[0]
        for j in range(N_CHIP - 1):
            s = s + r_ref[j].astype(F32)
        o_ref[...] = s

    return pl.pallas_call(
        body, name=name,
        grid_spec=pltpu.PrefetchScalarGridSpec(
            num_scalar_prefetch=1, grid=(rows // rb,),
            in_specs=[pl.BlockSpec((1, rb, cols), lambda i, q: (q[0], i, 0)),
                      pl.BlockSpec((N_CHIP - 1, rb, cols), lambda i, q: (0, i, 0))],
            out_specs=pl.BlockSpec((rb, cols), lambda i, q: (i, 0))),
        out_shape=SDS((rows, cols), F32),
        compiler_params=_cparams(("arbitrary",)),
    )(qidx, part, r2)


def _all_reduce_small(pack, name):
    rows = pack.shape[0]
    rs = rows // N_DEV
    assert rs * N_DEV == rows and rs % 8 == 0

    def body(x_ref, o_ref, rbuf, red, send1, recv1, send2, recv2):
        x, y, c = _me()
        me = 4 * x + 2 * y + c

        def peer(d):
            px = (1 - x) if (d >> 2) & 1 else x
            py = (1 - y) if (d >> 1) & 1 else y
            pc = (1 - c) if d & 1 else c
            return px, py, pc

        def sl(ref, k):
            return ref.at[pl.ds(pl.multiple_of(k * rs, 8), rs)]

        phase1 = []
        for d in range(1, N_DEV):
            px, py, pc = peer(d)
            phase1.append(pltpu.make_async_remote_copy(
                src_ref=sl(x_ref, 4 * px + 2 * py + pc), dst_ref=rbuf.at[d], send_sem=send1.at[d], recv_sem=recv1.at[d],
                device_id=(px, py, pc), device_id_type=MESH))
        for cp in phase1:
            cp.start()
        acc = sl(x_ref, me)[...]
        for cp in phase1:
            cp.wait()
        for d in range(1, N_DEV):
            acc = acc + rbuf[d]
        red[...] = acc
        sl(o_ref, me)[...] = acc
        phase2 = []
        for d in range(1, N_DEV):
            px, py, pc = peer(d)
            phase2.append(pltpu.make_async_remote_copy(
                src_ref=red, dst_ref=sl(o_ref, me), send_sem=send2.at[d], recv_sem=recv2.at[d],
                device_id=(px, py, pc), device_id_type=MESH))
        for cp in phase2:
            cp.start()
        for cp in phase2:
            cp.wait()

    vm = pl.BlockSpec(memory_space=pltpu.VMEM)
    return pl.pallas_call(
        body, name=name, in_specs=[vm], out_specs=vm, out_shape=SDS(pack.shape, F32),
        scratch_shapes=[pltpu.VMEM((N_DEV, rs, LANES), F32), pltpu.VMEM((rs, LANES), F32),
                        pltpu.SemaphoreType.DMA((N_DEV,)), pltpu.SemaphoreType.DMA((N_DEV,)),
                        pltpu.SemaphoreType.DMA((N_DEV,)), pltpu.SemaphoreType.DMA((N_DEV,))],
        compiler_params=_cparams(None, has_side_effects=True),
    )(pack)


def _adamw(w, g, m, v, name):
    rows, cols = w.shape
    rb = rows if rows * cols * 4 <= ADAMW_WHOLE_BYTES else _pick(rows, 256, 8)
    c1 = 1.0 / (1.0 - ADAM_B1 ** ADAM_STEP)
    c2 = 1.0 / (1.0 - ADAM_B2 ** ADAM_STEP)

    def body(w_ref, g_ref, m_ref, v_ref, d_ref, mo_ref, vo_ref):
        gv = g_ref[...]
        mn = ADAM_B1 * m_ref[...] + (1.0 - ADAM_B1) * gv
        vn = ADAM_B2 * v_ref[...] + (1.0 - ADAM_B2) * (gv * gv)
        mo_ref[...] = mn
        vo_ref[...] = vn
        d_ref[...] = -ADAM_LR * ((mn * c1) / (jnp.sqrt(vn * c2) + ADAM_EPS) + ADAM_WD * w_ref[...])

    spec = pl.BlockSpec((rb, cols), lambda i: (i, 0))
    return pl.pallas_call(
        body, name=name, grid=(rows // rb,),
        in_specs=[spec] * 4, out_specs=[spec] * 3, out_shape=[SDS((rows, cols), F32)] * 3,
        compiler_params=_cparams(("arbitrary",)),
    )(w, g, m, v)


def _pad_rows(a, mult=8):
    r = (-a.shape[0]) % mult
    return a if r == 0 else jnp.pad(a, ((0, r), (0, 0)))


def _as_lanes(a):
    flat = a.reshape(-1)
    pad = (-flat.shape[0]) % (8 * LANES)
    if pad:
        flat = jnp.pad(flat, (0, pad))
    return flat.reshape(-1, LANES)


def kernel(x, norm_g, w_in, ln_g, ln_b, w_s, b_s, conv_w, conv_b, w_pool, pool_scale, w_pa, w_pb, w_pc, w_o, final_g, loss_target, m_norm_g, m_w_in, m_ln_g, m_ln_b, m_w_s, m_b_s, m_conv_w, m_conv_b, m_w_pool, m_pool_scale, m_w_pa, m_w_pb, m_w_pc, m_w_o, m_final_g, v_norm_g, v_w_in, v_ln_g, v_ln_b, v_w_s, v_b_s, v_conv_w, v_conv_b, v_w_pool, v_pool_scale, v_w_pa, v_w_pb, v_w_pc, v_w_o, v_final_g):
    L = w_in.shape[0]
    D = x.shape[-1]
    n_loc = w_in.shape[2]
    pc_loc = w_pa.shape[2]
    x0 = x[0]
    target = loss_target[0]
    xi, yi, ci = _me()
    cidx = jnp.reshape(ci, (1,)).astype(jnp.int32)
    qidx = jnp.reshape(2 * xi + yi, (1,)).astype(jnp.int32)

    kinds5 = ["rows", "cols", "cols", "cols", "rows"]

    def layer_shards(l):
        return [w_in[l].T.astype(BF16), w_pa[l].astype(BF16), w_pb[l].astype(BF16), w_pc[l].astype(BF16),
                w_o[l].astype(BF16)]

    def gathered(direct, shards, kinds, l):
        sizes, _ = _gather_sizes(shards, kinds)
        return _run_exchange(_gather_forward(direct, kinds, sizes), f"weights_forward_{l}")

    cw_loc = _pad_rows(conv_w.reshape(L * CONV_TAPS, -1))
    cw_loc = jnp.pad(cw_loc, ((0, 0), (0, LANES - cw_loc.shape[1])))
    sh0, k0 = layer_shards(0) + [cw_loc], kinds5 + ["rows"]
    full0 = gathered(_run_exchange(_gather_direct(sh0, k0), "weights_gather_0"), sh0, k0, 0)
    cw_all = full0[-1].reshape(N_DEV, -1, LANES)[:, :L * CONV_TAPS, :conv_w.shape[2]]
    conv_w_full = jnp.transpose(cw_all, (1, 0, 2)).reshape(L, CONV_TAPS, -1)
    causal = jnp.tril(jnp.ones((CHUNK, CHUNK), dtype=bool))

    def make_layer(l, full5):
        win_t, wpa, wpb, wpc, wo = full5
        wm = jnp.where(causal, w_s[l], 0.0)
        cvec = jnp.concatenate([ln_g[l][None], ln_b[l][None], conv_w_full[l], conv_b[l][None], pool_scale[l][None],
                                jnp.zeros((C_ROWS - 7, SEG), F32)], axis=0)
        return dict(
            win_t=win_t, wpa=wpa, wpb=wpb, wpc=wpc, wo=wo, cvec=cvec,
            bsb=jnp.repeat(b_s[l].T, HEAD, axis=1),
            wcat=jnp.transpose(wm, (1, 0, 2)).reshape(CHUNK, GROUPS * CHUNK).astype(BF16),
            wcatt=jnp.transpose(wm, (2, 0, 1)).reshape(CHUNK, GROUPS * CHUNK).astype(BF16),
            wpool=w_pool[l].astype(BF16))

    layers = [make_layer(0, full0[:5])]
    xs, saved = [x0], []
    for l in range(L):
        lw = layers[l]
        nxt = layer_shards(l + 1) if l + 1 < L else None
        (p, h), delivered = _inproj(xs[-1], norm_g[l], lw["win_t"], f"inproj_fwd_{l}",
                                    [_gather_direct(nxt, kinds5)] if nxt else [])
        if nxt:
            layers.append(make_layer(l + 1, gathered(delivered[0], nxt, kinds5, l + 1)))
        xn, ya, yb, yc = _mix_fwd(p, xs[-1], lw, f"mix_fwd_{l}")
        saved.append((p, h, ya, yb, yc))
        xs.append(xn)
    dx, loss_acc, dfg_acc = _loss_head(xs[-1], final_g, target, "loss_head")

    rs_sizes = [n_loc, pc_loc, pc_loc, pc_loc, w_o.shape[1]]
    await_sibling, await_chips = [], []
    partial_of, from_chips = {}, {}
    serial = [0]

    def riders_now():
        riders, plan = [], []
        for grp in await_chips:
            riders.append(_chip_exchange([partial_of[t][1] for t, _, _, _ in grp]))
            plan.append(("chips", grp))
        for grp in await_sibling:
            riders.append(_sibling_exchange([g for _, g, _, _ in grp], [k for _, _, k, _ in grp], [s for _, _, _, s in grp]))
            plan.append(("sibling", grp))
        del await_chips[:], await_sibling[:]
        return riders, plan

    def absorb(plan, delivered):
        for (what, grp), res in zip(plan, delivered):
            for (t, g, k, s), r in zip(grp, res):
                if what == "chips":
                    from_chips[t] = r
                else:
                    partial_of[t] = _chip_partial(g, r, k, s, cidx, f"grad_chip_partial_{t[0]}_{t[1]}")
            if what == "sibling":
                await_chips.append(grp)

    small = [None] * L
    for l in reversed(range(L)):
        lw = layers[l]
        p, h, ya, yb, yc = saved[l]
        dp, gwo, gwpa, gwpb, gwpc, gwc, gbs, gwpool, gvec = _mix_bwd(p, dx, ya, yb, yc, lw, f"mix_bwd_{l}")
        await_sibling.append([((l, a), g, kinds5[a], rs_sizes[a]) for a, g in ((1, gwpa), (2, gwpb), (3, gwpc), (4, gwo))])

        def bwd_x(dx):
            riders, plan = riders_now()
            (dx, dng), delivered = _inproj_bwd_x(dp, lw["win_t"], xs[l], norm_g[l], dx, f"inproj_bwd_x_{l}", riders)
            absorb(plan, delivered)
            return dx, dng

        def bwd_w():
            riders, plan = riders_now()
            (gwin_t,), delivered = _inproj_bwd_w(dp, h, f"inproj_bwd_w_{l}", riders)
            absorb(plan, delivered)
            await_sibling.append([((l, 0), gwin_t, kinds5[0], rs_sizes[0])])

        if l == L - 1:
            dx, dng = bwd_x(dx)
            bwd_w()
        else:
            bwd_w()
            dx, dng = bwd_x(dx)
        small[l] = dict(norm_g=dng[0], ln_g=gvec[V_LNG], ln_b=gvec[V_LNB], w_s=gwc, b_s=gbs, conv_w=gvec[V_CW0:V_CW0 + 3],
                        conv_b=gvec[V_CB], w_pool=gwpool, pool_scale=gvec[V_PS])
    while await_sibling or await_chips:
        riders, plan = riders_now()
        delivered = []
        for ex in riders:
            delivered.append(_run_exchange(ex, f"grad_exchange_tail_{serial[0]}"))
            serial[0] += 1
        absorb(plan, delivered)
    grad_x = dx[None]
    big_grads = []
    for l in range(L):
        tot = [_grad_total(partial_of[(l, a)][0], from_chips[(l, a)], qidx, f"grad_total_{l}_{a}") for a in range(5)]
        big_grads.append([tot[0].T,
                          tot[1].reshape(SEG, pc_loc), tot[2].reshape(SEG, pc_loc), tot[3].reshape(SEG, pc_loc),
                          tot[4]])

    names = ["norm_g", "ln_g", "ln_b", "w_s", "b_s", "conv_w", "conv_b", "w_pool", "pool_scale"]
    pieces = [_as_lanes(jnp.stack([small[l][nm] for l in range(L)])) for nm in names]
    pieces += [_as_lanes(dfg_acc[0]), loss_acc]
    sizes = [pc.shape[0] for pc in pieces]
    pack = jnp.concatenate(pieces, axis=0)
    pack = _pad_rows(pack, 8 * N_DEV)
    red = _all_reduce_small(pack, "small_grads_all_reduce")
    offs = [0]
    for s in sizes:
        offs.append(offs[-1] + s)

    def unpack(i, shape):
        n = math.prod(shape)
        return red[offs[i]:offs[i + 1]].reshape(-1)[:n].reshape(shape)

    g_norm_g = unpack(0, (L, D))
    g_ln_g = unpack(1, (L, SEG))
    g_ln_b = unpack(2, (L, SEG))
    g_w_s = unpack(3, (L, GROUPS, CHUNK, CHUNK))
    g_b_s = jnp.transpose(unpack(4, (L, CHUNK, LANES))[:, :, :GROUPS], (0, 2, 1))
    g_conv_w_full = unpack(5, (L, CONV_TAPS, SEG))
    g_conv_b = unpack(6, (L, SEG))
    g_w_pool = unpack(7, (L, len(POOL_WINDOWS), POOL_GROUP, POOL_GROUP))
    g_pool_scale = unpack(8, (L, SEG))
    g_final_g = unpack(9, (D,))
    loss = red[offs[10], 0]
    dev = 4 * xi + 2 * yi + ci
    g_conv_w = lax.dynamic_slice_in_dim(g_conv_w_full, dev * conv_w.shape[2], conv_w.shape[2], axis=2)

    g_w_in = jnp.stack([big_grads[l][0] for l in range(L)])
    g_w_pa = jnp.stack([big_grads[l][1] for l in range(L)])
    g_w_pb = jnp.stack([big_grads[l][2] for l in range(L)])
    g_w_pc = jnp.stack([big_grads[l][3] for l in range(L)])
    g_w_o = jnp.stack([big_grads[l][4] for l in range(L)])

    grads = dict(norm_g=g_norm_g, w_in=g_w_in, ln_g=g_ln_g, ln_b=g_ln_b, w_s=g_w_s, b_s=g_b_s, conv_w=g_conv_w,
                 conv_b=g_conv_b, w_pool=g_w_pool, pool_scale=g_pool_scale, w_pa=g_w_pa, w_pb=g_w_pb, w_pc=g_w_pc,
                 w_o=g_w_o, final_g=g_final_g)
    weights = dict(norm_g=norm_g, w_in=w_in, ln_g=ln_g, ln_b=ln_b, w_s=w_s, b_s=b_s, conv_w=conv_w, conv_b=conv_b,
                   w_pool=w_pool, pool_scale=pool_scale, w_pa=w_pa, w_pb=w_pb, w_pc=w_pc, w_o=w_o, final_g=final_g)
    ms = dict(norm_g=m_norm_g, w_in=m_w_in, ln_g=m_ln_g, ln_b=m_ln_b, w_s=m_w_s, b_s=m_b_s, conv_w=m_conv_w,
              conv_b=m_conv_b, w_pool=m_w_pool, pool_scale=m_pool_scale, w_pa=m_w_pa, w_pb=m_w_pb, w_pc=m_w_pc,
              w_o=m_w_o, final_g=m_final_g)
    vs = dict(norm_g=v_norm_g, w_in=v_w_in, ln_g=v_ln_g, ln_b=v_ln_b, w_s=v_w_s, b_s=v_b_s, conv_w=v_conv_w,
              conv_b=v_conv_b, w_pool=v_w_pool, pool_scale=v_pool_scale, w_pa=v_w_pa, w_pb=v_w_pb, w_pc=v_w_pc,
              w_o=v_w_o, final_g=v_final_g)
    order = ["norm_g", "w_in", "ln_g", "ln_b", "w_s", "b_s", "conv_w", "conv_b", "w_pool", "pool_scale", "w_pa", "w_pb",
             "w_pc", "w_o", "final_g"]

    delta, new_m, new_v = {}, {}, {}
    big = ["w_in", "w_pa", "w_pb", "w_pc", "w_o"]
    for nm in big:
        shp = weights[nm].shape
        two = lambda a: a.reshape(-1, shp[-1])
        d, mn, vn = _adamw(two(weights[nm]), two(grads[nm]), two(ms[nm]), two(vs[nm]), f"adamw_{nm}")
        delta[nm], new_m[nm], new_v[nm] = d.reshape(shp), mn.reshape(shp), vn.reshape(shp)
    rest = [nm for nm in order if nm not in big]
    cat = lambda src: jnp.concatenate([_as_lanes(src[nm]) for nm in rest], axis=0)
    d, mn, vn = _adamw(cat(weights), cat(grads), cat(ms), cat(vs), "adamw_small")
    off = 0
    for nm in rest:
        shp = weights[nm].shape
        n = math.prod(shp)
        rows = _as_lanes(weights[nm]).shape[0]
        cut = lambda a: a[off:off + rows].reshape(-1)[:n].reshape(shp)
        delta[nm], new_m[nm], new_v[nm] = cut(d), cut(mn), cut(vn)
        off += rows

    return (loss, grad_x, *[grads[nm] for nm in order], *[delta[nm] for nm in order],
            *[new_m[nm] for nm in order], *[new_v[nm] for nm in order])
```

```python
import functools
import math

import jax
import jax.numpy as jnp
from jax import lax
from jax.experimental import pallas as pl
from jax.experimental.pallas import tpu as pltpu

F32 = jnp.float32
BF16 = jnp.bfloat16
SDS = jax.ShapeDtypeStruct
MESH = pl.DeviceIdType.MESH

SEG = 512
CHUNK = 128
GROUPS = 8
HEAD = SEG // GROUPS
POOL_WINDOWS = (2, 4, 8, 16)
POOL_GROUP = SEG // len(POOL_WINDOWS)
CONV_TAPS = 3
HALO = 16
RMS_EPS = 1e-6
LN_EPS = 1e-5
ADAM_LR, ADAM_B1, ADAM_B2, ADAM_EPS, ADAM_WD, ADAM_STEP = 0.001, 0.9, 0.999, 1e-08, 0.01, 10

O_U, O_V, O_ZA, O_XB, O_BG, O_CG, O_ZB, O_XC, O_ZC, O_G = (SEG * i for i in range(10))

N_DEV = 8
N_CHIP = 4
LANES = 128
VMEM_LIMIT = 48 * 1024 * 1024
ADAMW_WHOLE_BYTES = 2 * 1024 * 1024


def _cparams(sem=None, **kw):
    return pltpu.CompilerParams(dimension_semantics=sem, vmem_limit_bytes=VMEM_LIMIT, **kw)


def _pick(total, target, mult):
    best = None
    for d in range(mult, min(total, target) + 1, mult):
        if total % d == 0:
            best = d
    assert best is not None, (total, target, mult)
    return best


def _dot(a, b):
    return jnp.dot(a, b, preferred_element_type=F32)


def _dot_nt(a, b):
    return lax.dot_general(a, b, (((1,), (1,)), ((), ())), preferred_element_type=F32)


def _dot_tn(a, b):
    return lax.dot_general(a, b, (((0,), (0,)), ((), ())), preferred_element_type=F32)


def _zero(ref):
    ref[...] = jnp.zeros(ref.shape, ref.dtype)


def _sigmoid(x):
    return 1.0 / (1.0 + jnp.exp(-x))


_GELU_C = math.sqrt(2.0 / math.pi)


def _gelu(x):
    t = jnp.tanh(_GELU_C * (x + 0.044715 * x * x * x))
    return 0.5 * x * (1.0 + t), t


def _gelu_grad(x, t):
    return 0.5 * (1.0 + t) + 0.5 * x * (1.0 - t * t) * _GELU_C * (1.0 + 3.0 * 0.044715 * x * x)


def _me():
    return lax.axis_index("x"), lax.axis_index("y"), lax.axis_index("c")


def _inproj(x, norm_g, win_t, name, riders=()):
    T, D = x.shape
    N = win_t.shape[0]
    bt = _pick(T, 1024, 16)
    bn = _pick(N, 1536, LANES)
    grid = (T // bt, N // bn)

    def compute(x_ref, g_ref, w_ref, p_ref, h_ref, hs_ref):
        @pl.when(pl.program_id(1) == 0)
        def _():
            xv = x_ref[...]
            rstd = lax.rsqrt(jnp.mean(xv * xv, axis=-1, keepdims=True) + RMS_EPS)
            hb = (xv * rstd * g_ref[...]).astype(BF16)
            hs_ref[...] = hb
            h_ref[...] = hb

        p_ref[...] = _dot_nt(hs_ref[...], w_ref[...]).astype(BF16)

    return _host_call(
        compute, name, grid, riders,
        inputs=[x, norm_g.reshape(1, D), win_t],
        in_specs=[pl.BlockSpec((bt, D), lambda i, j: (i, 0)),
                  pl.BlockSpec((1, D), lambda i, j: (0, 0)),
                  pl.BlockSpec((bn, D), lambda i, j: (j, 0))],
        out_specs=[pl.BlockSpec((bt, bn), lambda i, j: (i, j)),
                   pl.BlockSpec((bt, D), lambda i, j: (i, 0))],
        out_shape=[SDS((T, N), BF16), SDS((T, D), BF16)],
        scratch_shapes=[pltpu.VMEM((bt, D), BF16)])


C_LNG, C_LNB, C_CW0, C_CW1, C_CW2, C_CB, C_PS = range(7)
C_ROWS = 8


def _mixers(p_ref, hxb_ref, hcg_ref, hxc_ref, cv, bsb_ref, wcat_ref, wpool_ref, extb, extc,
            first, blk, R, need_grad):
    def seg(lo):
        return p_ref[:, lo:lo + SEG].astype(F32)

    u, v, za = seg(O_U), seg(O_V), seg(O_ZA)
    xb, bg, cg, zb = seg(O_XB), seg(O_BG), seg(O_CG), seg(O_ZB)
    xc, zc = seg(O_XC), seg(O_ZC)
    out = {}

    ug, tu = _gelu(u)
    vg, tv = _gelu(v)
    mu = jnp.mean(vg, axis=-1, keepdims=True)
    vcen = vg - mu
    rs = lax.rsqrt(jnp.mean(vcen * vcen, axis=-1, keepdims=True) + LN_EPS)
    vhat = vcen * rs
    vn = (vhat * cv[C_LNG:C_LNG + 1, :] + cv[C_LNB:C_LNB + 1, :]).astype(BF16)
    lane_group = lax.broadcasted_iota(jnp.int32, (CHUNK, SEG), 1) // HEAD
    zero_b = jnp.zeros((CHUNK, SEG), BF16)
    sgs = []
    for ci in range(R // CHUNK):
        vc = vn[ci * CHUNK:(ci + 1) * CHUNK]
        vst = jnp.concatenate([jnp.where(lane_group == g, vc, zero_b) for g in range(GROUPS)], axis=0)
        sgs.append(_dot(wcat_ref[...], vst) + bsb_ref[...])
    sg = sgs[0] if len(sgs) == 1 else jnp.concatenate(sgs, axis=0)
    a_out = ug * sg
    sa = _sigmoid(za)
    out["a"] = a_out * (za * sa)

    cx = cg * xb
    halo_b = hcg_ref[...].astype(F32) * hxb_ref[...].astype(F32)
    extb[0:HALO, :] = jnp.where(first, 0.0, halo_b)
    extb[HALO:HALO + R, :] = cx
    cx1 = extb[pl.ds(HALO - 1, R), :]
    cx2 = extb[pl.ds(HALO - 2, R), :]
    yconv = (cv[C_CW0:C_CW0 + 1, :] * cx2 + cv[C_CW1:C_CW1 + 1, :] * cx1
             + cv[C_CW2:C_CW2 + 1, :] * cx + cv[C_CB:C_CB + 1, :])
    b_out = bg * yconv
    sb = _sigmoid(zb)
    out["b"] = b_out * (zb * sb)

    extc[0:HALO, :] = jnp.where(first, 0.0, hxc_ref[...].astype(F32))
    extc[HALO:HALO + R, :] = xc
    tpos = blk * R + lax.broadcasted_iota(jnp.int32, (R, POOL_GROUP), 0) + 1
    pooled, invs, pws = [], [], []
    for gi, w in enumerate(POOL_WINDOWS):
        lo = gi * POOL_GROUP
        win = xc[:, lo:lo + POOL_GROUP]
        for j in range(1, w):
            win = win + extc[pl.ds(HALO - j, R), lo:lo + POOL_GROUP]
        inv = 1.0 / jnp.minimum(tpos, w).astype(F32)
        pg = (win * inv - xc[:, lo:lo + POOL_GROUP]).astype(BF16)
        pooled.append(pg)
        invs.append(inv)
        pws.append(_dot(pg, wpool_ref[gi]))
    pw = jnp.concatenate(pws, axis=1)
    c_out = pw * cv[C_PS:C_PS + 1, :]
    sc = _sigmoid(zc)
    out["c"] = c_out * (zc * sc)

    if need_grad:
        out.update(u=u, v=v, tu=tu, tv=tv, ug=ug, sg=sg, a_out=a_out, za=za, sa=sa,
                   rs=rs, vhat=vhat, vn=vn, lane_group=lane_group, zero_b=zero_b,
                   xb=xb, bg=bg, cg=cg, cx=cx, cx1=cx1, cx2=cx2, yconv=yconv, b_out=b_out, zb=zb, sb=sb,
                   pooled=pooled, invs=invs, pw=pw, c_out=c_out, zc=zc, sc=sc)
    return out


def _halo_specs(R, nb, rev):
    step = R // HALO

    def mk(col):
        def imap(i):
            b = (nb - 1 - i) if rev else i
            return (jnp.maximum(b * step - 1, 0), col)
        return pl.BlockSpec((HALO, SEG), imap)

    return [mk(O_XB // SEG), mk(O_CG // SEG), mk(O_XC // SEG)]


def _const_spec(shape):
    nd = len(shape)
    return pl.BlockSpec(shape, lambda i: (0,) * nd, pipeline_mode=pl.Buffered(1))


MIX_FWD_ROWS = 512
MIX_BWD_ROWS = 256


def _mix_block_rows(T, target):
    return _pick(T, target, CHUNK)


def _mix_fwd(p, x, lw, name):
    T, D = x.shape
    N = p.shape[1]
    R = _mix_block_rows(T, MIX_FWD_ROWS)
    nb = T // R

    def body(p_ref, hxb, hcg, hxc, x_ref, cv_ref, bsb_ref, wcat_ref, wpool_ref, wpa_ref, wpb_ref, wpc_ref,
             wo_ref, xo_ref, ya_ref, yb_ref, yc_ref, extb, extc):
        i = pl.program_id(0)
        cv = cv_ref[...]
        r = _mixers(p_ref, hxb, hcg, hxc, cv, bsb_ref, wcat_ref, wpool_ref, extb, extc,
                    i == 0, i, R, False)
        ya = _dot(r["a"].astype(BF16), wpa_ref[...])
        yb = _dot(r["b"].astype(BF16), wpb_ref[...])
        yc = _dot(r["c"].astype(BF16), wpc_ref[...])
        ga = p_ref[:, O_G:O_G + D].astype(F32)
        gb = p_ref[:, O_G + D:O_G + 2 * D].astype(F32)
        gc = p_ref[:, O_G + 2 * D:O_G + 3 * D].astype(F32)
        merged = _sigmoid(ga) * ya + _sigmoid(gb) * yb + _sigmoid(gc) * yc
        xo_ref[...] = x_ref[...] + _dot(merged.astype(BF16), wo_ref[...])
        ya_ref[...] = ya.astype(BF16)
        yb_ref[...] = yb.astype(BF16)
        yc_ref[...] = yc.astype(BF16)

    row = lambda w: pl.BlockSpec((R, w), lambda i: (i, 0))
    consts = [lw["cvec"], lw["bsb"], lw["wcat"], lw["wpool"], lw["wpa"], lw["wpb"], lw["wpc"], lw["wo"]]
    return pl.pallas_call(
        body, name=name, grid=(nb,),
        in_specs=[row(N)] + _halo_specs(R, nb, False) + [row(D)] + [_const_spec(c.shape) for c in consts],
        out_specs=[row(D), row(D), row(D), row(D)],
        out_shape=[SDS((T, D), F32), SDS((T, D), BF16), SDS((T, D), BF16), SDS((T, D), BF16)],
        scratch_shapes=[pltpu.VMEM((HALO + R, SEG), F32), pltpu.VMEM((HALO + R, SEG), F32)],
        compiler_params=_cparams(("arbitrary",)),
    )(p, p, p, p, x, *consts)


def _loss_head(x, final_g, target, name):
    T, D = x.shape
    bt = _pick(T, 512, 8)

    def body(x_ref, g_ref, t_ref, dx_ref, loss_ref, dg_ref):
        @pl.when(pl.program_id(0) == 0)
        def _():
            _zero(loss_ref)
            _zero(dg_ref)

        xv = x_ref[...]
        g = g_ref[...]
        rstd = lax.rsqrt(jnp.mean(xv * xv, axis=-1, keepdims=True) + RMS_EPS)
        xhat = xv * rstd
        err = xhat * g - t_ref[...]
        part = 0.5 * jnp.sum(jnp.sum(err * err, axis=-1, keepdims=True), axis=0, keepdims=True) / D
        loss_ref[...] += jnp.broadcast_to(part, loss_ref.shape)
        dy = err * (1.0 / D)
        dg_ref[0:1, :] += jnp.sum(dy * xhat, axis=0, keepdims=True)
        dxn = dy * g
        dx_ref[...] = rstd * (dxn - xhat * jnp.mean(dxn * xhat, axis=-1, keepdims=True))

    return pl.pallas_call(
        body, name=name, grid=(T // bt,),
        in_specs=[pl.BlockSpec((bt, D), lambda i: (i, 0)), _const_spec((1, D)), pl.BlockSpec((bt, D), lambda i: (i, 0))],
        out_specs=[pl.BlockSpec((bt, D), lambda i: (i, 0)), _const_spec((8, LANES)), _const_spec((8, D))],
        out_shape=[SDS((T, D), F32), SDS((8, LANES), F32), SDS((8, D), F32)],
        compiler_params=_cparams(("arbitrary",)),
    )(x, final_g.reshape(1, D), target)


V_LNG, V_LNB, V_CB, V_PS, V_CW0, V_CW1, V_CW2 = range(7)


def _mix_bwd(p, dxo, ya, yb, yc, lw, name):
    T, D = dxo.shape
    N = p.shape[1]
    R = _mix_block_rows(T, MIX_BWD_ROWS)
    nb = T // R

    def body(p_ref, hxb, hcg, hxc, dxo_ref, ya_ref, yb_ref, yc_ref, cv_ref, bsb_ref, wcat_ref, wcatt_ref,
             wpool_ref, wpa_ref, wpb_ref, wpc_ref, wo_ref,
             dp_ref, acts_ref, mrg_ref, dys_ref, gwc_ref, gbs_ref, gwpool_ref, gvec_ref,
             extb, extc, extdy, extq, cdy, cq, bsacc):
        i = pl.program_id(0)
        blk = nb - 1 - i

        @pl.when(i == 0)
        def _():
            for ref in (gwc_ref, gwpool_ref, gvec_ref, cdy, cq, bsacc):
                _zero(ref)

        cv = cv_ref[...]
        r = _mixers(p_ref, hxb, hcg, hxc, cv, bsb_ref, wcat_ref, wpool_ref, extb, extc,
                    blk == 0, blk, R, True)

        dxo_b = dxo_ref[...].astype(BF16)
        dm = _dot_nt(dxo_b, wo_ref[...])
        ys = [ya_ref[...].astype(F32), yb_ref[...].astype(F32), yc_ref[...].astype(F32)]
        sig = [_sigmoid(p_ref[:, O_G + k * D:O_G + (k + 1) * D].astype(F32)) for k in range(3)]
        merged = sig[0] * ys[0] + sig[1] * ys[1] + sig[2] * ys[2]
        mrg_ref[...] = merged.astype(BF16)
        dacts = []
        for k, (act, w_ref) in enumerate(((r["a"], wpa_ref), (r["b"], wpb_ref), (r["c"], wpc_ref))):
            dyk = (dm * sig[k]).astype(BF16)
            dp_ref[:, O_G + k * D:O_G + (k + 1) * D] = (dm * ys[k] * sig[k] * (1.0 - sig[k])).astype(BF16)
            acts_ref[:, k * SEG:(k + 1) * SEG] = act.astype(BF16)
            dys_ref[:, k * D:(k + 1) * D] = dyk
            dacts.append(_dot_nt(dyk, w_ref[...]))
        da, db, dc = dacts

        def silu_bwd(dact, pre, z, s):
            return dact * (z * s), dact * pre * (s * (1.0 + z * (1.0 - s)))

        d_aout, dza = silu_bwd(da, r["a_out"], r["za"], r["sa"])
        dp_ref[:, O_ZA:O_ZA + SEG] = dza.astype(BF16)
        dp_ref[:, O_U:O_U + SEG] = (d_aout * r["sg"] * _gelu_grad(r["u"], r["tu"])).astype(BF16)
        d_sg = d_aout * r["ug"]
        dvns = []
        for ci in range(R // CHUNK):
            dsc = d_sg[ci * CHUNK:(ci + 1) * CHUNK]
            bsacc[...] += dsc
            dsc_b = dsc.astype(BF16)
            dst = jnp.concatenate([jnp.where(r["lane_group"] == g, dsc_b, r["zero_b"]) for g in range(GROUPS)], axis=0)
            dvns.append(_dot(wcatt_ref[...], dst))
            gwc_ref[...] += _dot_nt(dst, r["vn"][ci * CHUNK:(ci + 1) * CHUNK])
        d_vn = dvns[0] if len(dvns) == 1 else jnp.concatenate(dvns, axis=0)
        vhat = r["vhat"]
        gvec_ref[V_LNG:V_LNG + 1, :] += jnp.sum(d_vn * vhat, axis=0, keepdims=True)
        gvec_ref[V_LNB:V_LNB + 1, :] += jnp.sum(d_vn, axis=0, keepdims=True)
        d_vhat = d_vn * cv[C_LNG:C_LNG + 1, :]
        d_vg = r["rs"] * (d_vhat - jnp.mean(d_vhat, axis=-1, keepdims=True)
                          - vhat * jnp.mean(d_vhat * vhat, axis=-1, keepdims=True))
        dp_ref[:, O_V:O_V + SEG] = (d_vg * _gelu_grad(r["v"], r["tv"])).astype(BF16)

        d_bout, dzb = silu_bwd(db, r["b_out"], r["zb"], r["sb"])
        dp_ref[:, O_ZB:O_ZB + SEG] = dzb.astype(BF16)
        dp_ref[:, O_BG:O_BG + SEG] = (d_bout * r["yconv"]).astype(BF16)
        d_y = d_bout * r["bg"]
        gvec_ref[V_CB:V_CB + 1, :] += jnp.sum(d_y, axis=0, keepdims=True)
        gvec_ref[V_CW0:V_CW0 + 1, :] += jnp.sum(d_y * r["cx2"], axis=0, keepdims=True)
        gvec_ref[V_CW1:V_CW1 + 1, :] += jnp.sum(d_y * r["cx1"], axis=0, keepdims=True)
        gvec_ref[V_CW2:V_CW2 + 1, :] += jnp.sum(d_y * r["cx"], axis=0, keepdims=True)
        extdy[0:R, :] = d_y
        extdy[R:R + HALO, :] = cdy[...]
        d_cx = (cv[C_CW2:C_CW2 + 1, :] * d_y + cv[C_CW1:C_CW1 + 1, :] * extdy[pl.ds(1, R), :]
                + cv[C_CW0:C_CW0 + 1, :] * extdy[pl.ds(2, R), :])
        cdy[...] = d_y[0:HALO]
        dp_ref[:, O_CG:O_CG + SEG] = (d_cx * r["xb"]).astype(BF16)
        dp_ref[:, O_XB:O_XB + SEG] = (d_cx * r["cg"]).astype(BF16)

        d_cout, dzc = silu_bwd(dc, r["c_out"], r["zc"], r["sc"])
        dp_ref[:, O_ZC:O_ZC + SEG] = dzc.astype(BF16)
        gvec_ref[V_PS:V_PS + 1, :] += jnp.sum(d_cout * r["pw"], axis=0, keepdims=True)
        d_pw = (d_cout * cv[C_PS:C_PS + 1, :]).astype(BF16)
        dpool = []
        for gi, w in enumerate(POOL_WINDOWS):
            lo = gi * POOL_GROUP
            dpw_g = d_pw[:, lo:lo + POOL_GROUP]
            gwpool_ref[lo:lo + POOL_GROUP, :] += _dot_tn(r["pooled"][gi], dpw_g)
            dpg = _dot_nt(dpw_g, wpool_ref[gi])
            dpool.append(dpg)
            extq[0:R, lo:lo + POOL_GROUP] = dpg * r["invs"][gi]
        extq[R:R + HALO, :] = cq[...]
        for gi, w in enumerate(POOL_WINDOWS):
            lo = gi * POOL_GROUP
            acc = extq[0:R, lo:lo + POOL_GROUP]
            for j in range(1, w):
                acc = acc + extq[pl.ds(j, R), lo:lo + POOL_GROUP]
            dp_ref[:, O_XC + lo:O_XC + lo + POOL_GROUP] = (acc - dpool[gi]).astype(BF16)
        cq[...] = extq[0:HALO, :]

        @pl.when(i == nb - 1)
        def _():
            rr = lax.broadcasted_iota(jnp.int32, gwc_ref.shape, 0) % CHUNK
            cc = lax.broadcasted_iota(jnp.int32, gwc_ref.shape, 1)
            gwc_ref[...] = jnp.where(cc <= rr, gwc_ref[...], 0.0)
            acc = bsacc[...]
            hi = acc.astype(BF16)
            lo_ = (acc - hi.astype(F32)).astype(BF16)
            sel = (lax.broadcasted_iota(jnp.int32, (SEG, LANES), 0) // HEAD
                   == lax.broadcasted_iota(jnp.int32, (SEG, LANES), 1)).astype(BF16)
            gbs_ref[...] = _dot(hi, sel) + _dot(lo_, sel)

    row = lambda w: pl.BlockSpec((R, w), lambda i: (nb - 1 - i, 0))
    consts = [lw["cvec"], lw["bsb"], lw["wcat"], lw["wcatt"], lw["wpool"], lw["wpa"], lw["wpb"], lw["wpc"], lw["wo"]]
    acc_shapes = [(GROUPS * CHUNK, CHUNK), (CHUNK, LANES), (SEG, POOL_GROUP), (8, SEG)]
    row_widths = [N, 3 * SEG, D, 3 * D]
    return pl.pallas_call(
        body, name=name, grid=(nb,),
        in_specs=([row(N)] + _halo_specs(R, nb, True) + [row(D), row(D), row(D), row(D)]
                  + [_const_spec(c.shape) for c in consts]),
        out_specs=[row(w) for w in row_widths] + [_const_spec(s) for s in acc_shapes],
        out_shape=[SDS((T, w), BF16) for w in row_widths] + [SDS(s, F32) for s in acc_shapes],
        scratch_shapes=[pltpu.VMEM((HALO + R, SEG), F32)] * 4
        + [pltpu.VMEM((HALO, SEG), F32), pltpu.VMEM((HALO, SEG), F32), pltpu.VMEM((CHUNK, SEG), F32)],
        compiler_params=_cparams(("arbitrary",)),
    )(p, p, p, p, dxo, ya, yb, yc, *consts)


def _proj_wgrad(acts, merged, dys, dxo, name):
    T, D = dxo.shape
    bk = _pick(T, 512, 16)

    def body(a_ref, m_ref, dy_ref, dxo_ref, gwpa_ref, gwpb_ref, gwpc_ref, gwo_ref):
        @pl.when(pl.program_id(0) == 0)
        def _():
            for ref in (gwpa_ref, gwpb_ref, gwpc_ref, gwo_ref):
                _zero(ref)

        gwo_ref[...] += _dot_tn(m_ref[...], dxo_ref[...].astype(BF16))
        for k, ref in enumerate((gwpa_ref, gwpb_ref, gwpc_ref)):
            ref[...] += _dot_tn(a_ref[:, k * SEG:(k + 1) * SEG], dy_ref[:, k * D:(k + 1) * D])

    row = lambda w: pl.BlockSpec((bk, w), lambda i: (i, 0))
    shapes = [(SEG, D), (SEG, D), (SEG, D), (D, D)]
    return pl.pallas_call(
        body, name=name, grid=(T // bk,),
        in_specs=[row(3 * SEG), row(D), row(3 * D), row(D)],
        out_specs=[_const_spec(s) for s in shapes], out_shape=[SDS(s, F32) for s in shapes],
        compiler_params=_cparams(("arbitrary",)),
    )(acts, merged, dys, dxo)


def _inproj_bwd_x(dp, win_t, x, norm_g, dxo, name, riders=()):
    T, D = x.shape
    N = dp.shape[1]
    bt = _pick(T, 1024, 16)
    bk = _pick(N, 1536, LANES)
    nk = N // bk

    def compute(dp_ref, w_ref, x_ref, g_ref, dxo_ref, dx_ref, dg_ref, acc_ref):
        i, k = pl.program_id(0), pl.program_id(1)

        @pl.when((i == 0) & (k == 0))
        def _():
            _zero(dg_ref)

        @pl.when(k == 0)
        def _():
            _zero(acc_ref)

        acc_ref[...] += _dot(dp_ref[...], w_ref[...])

        @pl.when(k == nk - 1)
        def _():
            dh = acc_ref[...]
            xv = x_ref[...]
            rstd = lax.rsqrt(jnp.mean(xv * xv, axis=-1, keepdims=True) + RMS_EPS)
            xhat = xv * rstd
            dg_ref[0:1, :] += jnp.sum(dh * xhat, axis=0, keepdims=True)
            dxn = dh * g_ref[...]
            dx_ref[...] = dxo_ref[...] + rstd * (dxn - xhat * jnp.mean(dxn * xhat, axis=-1, keepdims=True))

    return _host_call(
        compute, name, (T // bt, nk), riders,
        inputs=[dp, win_t, x, norm_g.reshape(1, D), dxo],
        in_specs=[pl.BlockSpec((bt, bk), lambda i, k: (i, k)),
                  pl.BlockSpec((bk, D), lambda i, k: (k, 0)),
                  pl.BlockSpec((bt, D), lambda i, k: (i, 0)),
                  pl.BlockSpec((1, D), lambda i, k: (0, 0)),
                  pl.BlockSpec((bt, D), lambda i, k: (i, 0))],
        out_specs=[pl.BlockSpec((bt, D), lambda i, k: (i, 0)), pl.BlockSpec((8, D), lambda i, k: (0, 0))],
        out_shape=[SDS((T, D), F32), SDS((8, D), F32)],
        scratch_shapes=[pltpu.VMEM((bt, D), F32)])


def _inproj_bwd_w(dp, h, name, riders=()):
    T, N = dp.shape
    D = h.shape[1]
    bn = _pick(N, 1920, LANES)
    bk = _pick(T, 1024, 16)
    nk = T // bk

    def compute(dp_ref, h_ref, o_ref):
        @pl.when(pl.program_id(1) == 0)
        def _():
            _zero(o_ref)

        o_ref[...] += _dot_tn(dp_ref[...], h_ref[...])

    return _host_call(
        compute, name, (N // bn, nk), riders,
        inputs=[dp, h],
        in_specs=[pl.BlockSpec((bk, bn), lambda j, k: (k, j)), pl.BlockSpec((bk, D), lambda j, k: (k, 0))],
        out_specs=[pl.BlockSpec((bn, D), lambda j, k: (j, 0))],
        out_shape=[SDS((N, D), F32)],
        scratch_shapes=[])


def _chip_peer(x, y, j):
    px = (1 - x) if (j >> 1) else x
    py = (1 - y) if (j & 1) else y
    return px, py


def _blk(ref, kind, k, n):
    if kind == "rows":
        return ref.at[pl.ds(pl.multiple_of(k * n, 8), n)]
    return ref.at[:, pl.ds(pl.multiple_of(k * n, LANES), n)]


class _Exchange:
    def __init__(self, srcs, out_shapes, n_sems, build, alias=False):
        self.srcs, self.out_shapes, self.n_sems, self.build, self.alias = list(srcs), list(out_shapes), n_sems, build, alias


def _rider_plan(riders):
    inputs = [s for e in riders for s in e.srcs]
    out_shapes = [o for e in riders for o in e.out_shapes]
    sems = [pltpu.SemaphoreType.DMA((e.n_sems,)) for e in riders for _ in range(2)]

    def copies(in_refs, out_refs, sem_refs):
        cps, i, o = [], 0, 0
        for k, e in enumerate(riders):
            ni, no = len(e.srcs), len(e.out_shapes)
            cps += e.build(in_refs[i:i + ni], out_refs[o:o + no], sem_refs[2 * k], sem_refs[2 * k + 1])
            i, o = i + ni, o + no
        return cps

    return inputs, out_shapes, sems, copies


_ANY = pl.BlockSpec(memory_space=pl.ANY)


def _host_call(compute, name, grid, riders, inputs, in_specs, out_specs, out_shape, scratch_shapes):
    r_in, r_out, r_sems, copies = _rider_plan(riders)
    ni, no, ns = len(inputs), len(out_shape), len(scratch_shapes)

    def body(*refs):
        ins, rins = refs[:ni], refs[ni:ni + len(r_in)]
        outs = refs[ni + len(r_in):ni + len(r_in) + no]
        routs = refs[ni + len(r_in) + no:ni + len(r_in) + no + len(r_out)]
        scr = refs[ni + len(r_in) + no + len(r_out):]
        first = functools.reduce(lambda a, b: a & b, [pl.program_id(d) == 0 for d in range(len(grid))])
        last = functools.reduce(lambda a, b: a & b, [pl.program_id(d) == grid[d] - 1 for d in range(len(grid))])
        if riders:
            @pl.when(first)
            def _():
                for cp in copies(rins, routs, scr[ns:]):
                    cp.start()

        compute(*ins, *outs, *scr[:ns])

        if riders:
            @pl.when(last)
            def _():
                for cp in copies(rins, routs, scr[ns:]):
                    cp.wait()

    res = pl.pallas_call(
        body, name=name, grid=grid,
        in_specs=list(in_specs) + [_ANY] * len(r_in),
        out_specs=list(out_specs) + [_ANY] * len(r_out),
        out_shape=list(out_shape) + r_out,
        scratch_shapes=list(scratch_shapes) + r_sems,
        compiler_params=_cparams(("arbitrary",) * len(grid)),
    )(*inputs, *r_in)
    return res[:no], _split_riders(riders, res[no:])


def _split_riders(riders, flat):
    out, o = [], 0
    for e in riders:
        out.append(list(flat[o:o + len(e.out_shapes)]))
        o += len(e.out_shapes)
    return out


def _run_exchange(ex, name):
    n_in, n_out = len(ex.srcs), len(ex.out_shapes)

    def body(*refs):
        cps = ex.build(refs[:n_in], refs[n_in:n_in + n_out], refs[n_in + n_out], refs[n_in + n_out + 1])
        for cp in cps:
            cp.start()
        for cp in cps:
            cp.wait()

    return pl.pallas_call(
        body, name=name,
        in_specs=[_ANY] * n_in, out_specs=[_ANY] * n_out, out_shape=ex.out_shapes,
        input_output_aliases={i: i for i in range(n_in)} if ex.alias else {},
        scratch_shapes=[pltpu.SemaphoreType.DMA((ex.n_sems,)), pltpu.SemaphoreType.DMA((ex.n_sems,))],
        compiler_params=pltpu.CompilerParams(has_side_effects=True),
    )(*ex.srcs)


def _gather_sizes(shards, kinds):
    sizes = [s.shape[0] if k == "rows" else s.shape[1] for s, k in zip(shards, kinds)]
    fulls = [SDS((s.shape[0] * N_DEV,) + s.shape[1:], s.dtype) if k == "rows"
             else SDS((s.shape[0], s.shape[1] * N_DEV), s.dtype) for s, k in zip(shards, kinds)]
    return sizes, fulls


def _gather_direct(shards, kinds):
    n = len(shards)
    sizes, fulls = _gather_sizes(shards, kinds)

    def build(ins, outs, send_sems, recv_sems):
        x, y, c = _me()
        cps = []
        for a in range(n):
            mine = _blk(outs[a], kinds[a], 4 * x + 2 * y + c, sizes[a])
            cps.append(pltpu.make_async_copy(ins[a], mine, send_sems.at[5 * a + 4]))
            for j in range(N_CHIP):
                to = (x, y, 1 - c) if j == 0 else (*_chip_peer(x, y, j), c)
                cps.append(pltpu.make_async_remote_copy(
                    src_ref=ins[a], dst_ref=mine, send_sem=send_sems.at[5 * a + j], recv_sem=recv_sems.at[5 * a + j],
                    device_id=to, device_id_type=MESH))
        return cps

    return _Exchange(shards, fulls, 5 * n, build)


def _gather_forward(fulls, kinds, sizes):
    n = len(fulls)

    def build(ins, outs, send_sems, recv_sems):
        x, y, c = _me()
        cps = []
        for a in range(n):
            for j in (1, 2, 3):
                px, py = _chip_peer(x, y, j)
                k = 4 * px + 2 * py + c
                cps.append(pltpu.make_async_remote_copy(
                    src_ref=_blk(ins[a], kinds[a], k, sizes[a]), dst_ref=_blk(outs[a], kinds[a], k, sizes[a]),
                    send_sem=send_sems.at[3 * a + j - 1], recv_sem=recv_sems.at[3 * a + j - 1],
                    device_id=(x, y, 1 - c), device_id_type=MESH))
        return cps

    return _Exchange(fulls, [SDS(f.shape, f.dtype) for f in fulls], 3 * n, build, alias=True)


def _sibling_exchange(grads, kinds, sizes):
    n = len(grads)

    def blk_shape(a):
        g = grads[a]
        return (sizes[a],) + g.shape[1:] if kinds[a] == "rows" else (g.shape[0], sizes[a])

    def build(ins, outs, send_sems, recv_sems):
        x, y, c = _me()
        cps = []
        for a in range(n):
            for q in range(N_CHIP):
                cps.append(pltpu.make_async_remote_copy(
                    src_ref=_blk(ins[a], kinds[a], 2 * q + (1 - c), sizes[a]), dst_ref=outs[a].at[q],
                    send_sem=send_sems.at[N_CHIP * a + q], recv_sem=recv_sems.at[N_CHIP * a + q],
                    device_id=(x, y, 1 - c), device_id_type=MESH))
        return cps

    return _Exchange(grads, [SDS((N_CHIP,) + blk_shape(a), F32) for a in range(n)], N_CHIP * n, build)


def _chip_partial(g, r1, kind, size, cidx, name):
    if kind == "rows":
        rows, cols = size, g.shape[1]
        g3 = g.reshape(N_DEV, rows, cols)
        rb = _pick(rows, 512, 16)
        g_spec = pl.BlockSpec((1, rb, cols), lambda q, j, c: (2 * q + c[0], j, 0))
        grid = (N_CHIP, rows // rb)
        blk = (1, rb, cols)
        imap = lambda q, j, c: (q, j, 0)
    else:
        rows, cols = g.shape[0], size
        g3 = g
        g_spec = pl.BlockSpec((rows, cols), lambda q, j, c: (0, 2 * q + c[0]))
        grid = (N_CHIP, 1)
        blk = (1, rows, cols)
        imap = lambda q, j, c: (q, 0, 0)

    def body(c_ref, g_ref, r_ref, p_ref, pb_ref):
        s = g_ref[...].reshape(blk) + r_ref[...]
        p_ref[...] = s
        pb_ref[...] = s.astype(BF16)

    return pl.pallas_call(
        body, name=name,
        grid_spec=pltpu.PrefetchScalarGridSpec(
            num_scalar_prefetch=1, grid=grid,
            in_specs=[g_spec, pl.BlockSpec(blk, imap)],
            out_specs=[pl.BlockSpec(blk, imap), pl.BlockSpec(blk, imap)]),
        out_shape=[SDS((N_CHIP, rows, cols), F32), SDS((N_CHIP, rows, cols), BF16)],
        compiler_params=_cparams(("arbitrary", "arbitrary")),
    )(cidx, g3, r1)


def _chip_exchange(parts):
    n = len(parts)
    m = N_CHIP - 1

    def build(ins, outs, send_sems, recv_sems):
        x, y, c = _me()
        cps = []
        for a in range(n):
            for j in (1, 2, 3):
                px, py = _chip_peer(x, y, j)
                cps.append(pltpu.make_async_remote_copy(
                    src_ref=ins[a].at[2 * px + py], dst_ref=outs[a].at[j - 1], send_sem=send_sems.at[m * a + j - 1],
                    recv_sem=recv_sems.at[m * a + j - 1], device_id=(px, py, c), device_id_type=MESH))
        return cps

    return _Exchange(parts, [SDS((m,) + p.shape[1:], BF16) for p in parts], m * n, build)


def _grad_total(part, r2, qidx, name):
    _, rows, cols = part.shape
    rb = _pick(rows, 512, 16)

    def body(q_ref, p_ref, r_ref, o_ref):
        s = p_ref[0]
        for j in range(N_CHIP - 1):
            s = s + r_ref[j].astype(F32)
        o_ref[...] = s

    return pl.pallas_call(
        body, name=name,
        grid_spec=pltpu.PrefetchScalarGridSpec(
            num_scalar_prefetch=1, grid=(rows // rb,),
            in_specs=[pl.BlockSpec((1, rb, cols), lambda i, q: (q[0], i, 0)),
                      pl.BlockSpec((N_CHIP - 1, rb, cols), lambda i, q: (0, i, 0))],
            out_specs=pl.BlockSpec((rb, cols), lambda i, q: (i, 0))),
        out_shape=SDS((rows, cols), F32),
        compiler_params=_cparams(("arbitrary",)),
    )(qidx, part, r2)


def _all_reduce_small(pack, name):
    rows = pack.shape[0]
    rs = rows // N_DEV
    assert rs * N_DEV == rows and rs % 8 == 0

    def body(x_ref, o_ref, rbuf, red, send1, recv1, send2, recv2):
        x, y, c = _me()
        me = 4 * x + 2 * y + c

        def peer(d):
            px = (1 - x) if (d >> 2) & 1 else x
            py = (1 - y) if (d >> 1) & 1 else y
            pc = (1 - c) if d & 1 else c
            return px, py, pc

        def sl(ref, k):
            return ref.at[pl.ds(pl.multiple_of(k * rs, 8), rs)]

        phase1 = []
        for d in range(1, N_DEV):
            px, py, pc = peer(d)
            phase1.append(pltpu.make_async_remote_copy(
                src_ref=sl(x_ref, 4 * px + 2 * py + pc), dst_ref=rbuf.at[d], send_sem=send1.at[d], recv_sem=recv1.at[d],
                device_id=(px, py, pc), device_id_type=MESH))
        for cp in phase1:
            cp.start()
        acc = sl(x_ref, me)[...]
        for cp in phase1:
            cp.wait()
        for d in range(1, N_DEV):
            acc = acc + rbuf[d]
        red[...] = acc
        sl(o_ref, me)[...] = acc
        phase2 = []
        for d in range(1, N_DEV):
            px, py, pc = peer(d)
            phase2.append(pltpu.make_async_remote_copy(
                src_ref=red, dst_ref=sl(o_ref, me), send_sem=send2.at[d], recv_sem=recv2.at[d],
                device_id=(px, py, pc), device_id_type=MESH))
        for cp in phase2:
            cp.start()
        for cp in phase2:
            cp.wait()

    vm = pl.BlockSpec(memory_space=pltpu.VMEM)
    return pl.pallas_call(
        body, name=name, in_specs=[vm], out_specs=vm, out_shape=SDS(pack.shape, F32),
        scratch_shapes=[pltpu.VMEM((N_DEV, rs, LANES), F32), pltpu.VMEM((rs, LANES), F32),
                        pltpu.SemaphoreType.DMA((N_DEV,)), pltpu.SemaphoreType.DMA((N_DEV,)),
                        pltpu.SemaphoreType.DMA((N_DEV,)), pltpu.SemaphoreType.DMA((N_DEV,))],
        compiler_params=_cparams(None, has_side_effects=True),
    )(pack)


def _adamw(w, g, m, v, name):
    rows, cols = w.shape
    rb = rows if rows * cols * 4 <= ADAMW_WHOLE_BYTES else _pick(rows, 256, 8)
    c1 = 1.0 / (1.0 - ADAM_B1 ** ADAM_STEP)
    c2 = 1.0 / (1.0 - ADAM_B2 ** ADAM_STEP)

    def body(w_ref, g_ref, m_ref, v_ref, d_ref, mo_ref, vo_ref):
        gv = g_ref[...]
        mn = ADAM_B1 * m_ref[...] + (1.0 - ADAM_B1) * gv
        vn = ADAM_B2 * v_ref[...] + (1.0 - ADAM_B2) * (gv * gv)
        mo_ref[...] = mn
        vo_ref[...] = vn
        d_ref[...] = -ADAM_LR * ((mn * c1) / (jnp.sqrt(vn * c2) + ADAM_EPS) + ADAM_WD * w_ref[...])

    spec = pl.BlockSpec((rb, cols), lambda i: (i, 0))
    return pl.pallas_call(
        body, name=name, grid=(rows // rb,),
        in_specs=[spec] * 4, out_specs=[spec] * 3, out_shape=[SDS((rows, cols), F32)] * 3,
        compiler_params=_cparams(("arbitrary",)),
    )(w, g, m, v)


def _pad_rows(a, mult=8):
    r = (-a.shape[0]) % mult
    return a if r == 0 else jnp.pad(a, ((0, r), (0, 0)))


def _as_lanes(a):
    flat = a.reshape(-1)
    pad = (-flat.shape[0]) % (8 * LANES)
    if pad:
        flat = jnp.pad(flat, (0, pad))
    return flat.reshape(-1, LANES)


def kernel(x, norm_g, w_in, ln_g, ln_b, w_s, b_s, conv_w, conv_b, w_pool, pool_scale, w_pa, w_pb, w_pc, w_o, final_g, loss_target, m_norm_g, m_w_in, m_ln_g, m_ln_b, m_w_s, m_b_s, m_conv_w, m_conv_b, m_w_pool, m_pool_scale, m_w_pa, m_w_pb, m_w_pc, m_w_o, m_final_g, v_norm_g, v_w_in, v_ln_g, v_ln_b, v_w_s, v_b_s, v_conv_w, v_conv_b, v_w_pool, v_pool_scale, v_w_pa, v_w_pb, v_w_pc, v_w_o, v_final_g):
    L = w_in.shape[0]
    D = x.shape[-1]
    n_loc = w_in.shape[2]
    pc_loc = w_pa.shape[2]
    x0 = x[0]
    target = loss_target[0]
    xi, yi, ci = _me()
    cidx = jnp.reshape(ci, (1,)).astype(jnp.int32)
    qidx = jnp.reshape(2 * xi + yi, (1,)).astype(jnp.int32)

    kinds5 = ["rows", "cols", "cols", "cols", "rows"]

    def layer_shards(l):
        return [w_in[l].T.astype(BF16), w_pa[l].astype(BF16), w_pb[l].astype(BF16), w_pc[l].astype(BF16),
                w_o[l].astype(BF16)]

    def gathered(direct, shards, kinds, l):
        sizes, _ = _gather_sizes(shards, kinds)
        return _run_exchange(_gather_forward(direct, kinds, sizes), f"weights_forward_{l}")

    cw_loc = _pad_rows(conv_w.reshape(L * CONV_TAPS, -1))
    cw_loc = jnp.pad(cw_loc, ((0, 0), (0, LANES - cw_loc.shape[1])))
    sh0, k0 = layer_shards(0) + [cw_loc], kinds5 + ["rows"]
    full0 = gathered(_run_exchange(_gather_direct(sh0, k0), "weights_gather_0"), sh0, k0, 0)
    cw_all = full0[-1].reshape(N_DEV, -1, LANES)[:, :L * CONV_TAPS, :conv_w.shape[2]]
    conv_w_full = jnp.transpose(cw_all, (1, 0, 2)).reshape(L, CONV_TAPS, -1)
    causal = jnp.tril(jnp.ones((CHUNK, CHUNK), dtype=bool))

    def make_layer(l, full5):
        win_t, wpa, wpb, wpc, wo = full5
        wm = jnp.where(causal, w_s[l], 0.0)
        cvec = jnp.concatenate([ln_g[l][None], ln_b[l][None], conv_w_full[l], conv_b[l][None], pool_scale[l][None],
                                jnp.zeros((C_ROWS - 7, SEG), F32)], axis=0)
        return dict(
            win_t=win_t, wpa=wpa, wpb=wpb, wpc=wpc, wo=wo, cvec=cvec,
            bsb=jnp.repeat(b_s[l].T, HEAD, axis=1),
            wcat=jnp.transpose(wm, (1, 0, 2)).reshape(CHUNK, GROUPS * CHUNK).astype(BF16),
            wcatt=jnp.transpose(wm, (2, 0, 1)).reshape(CHUNK, GROUPS * CHUNK).astype(BF16),
            wpool=w_pool[l].astype(BF16))

    layers = [make_layer(0, full0[:5])]
    xs, saved = [x0], []
    for l in range(L):
        lw = layers[l]
        nxt = layer_shards(l + 1) if l + 1 < L else None
        (p, h), delivered = _inproj(xs[-1], norm_g[l], lw["win_t"], f"inproj_fwd_{l}",
                                    [_gather_direct(nxt, kinds5)] if nxt else [])
        if nxt:
            layers.append(make_layer(l + 1, gathered(delivered[0], nxt, kinds5, l + 1)))
        xn, ya, yb, yc = _mix_fwd(p, xs[-1], lw, f"mix_fwd_{l}")
        saved.append((p, h, ya, yb, yc))
        xs.append(xn)
    dx, loss_acc, dfg_acc = _loss_head(xs[-1], final_g, target, "loss_head")

    rs_sizes = [n_loc, pc_loc, pc_loc, pc_loc, w_o.shape[1]]
    await_sibling, await_chips = [], []
    partial_of, from_chips = {}, {}
    serial = [0]

    def riders_now():
        riders, plan = [], []
        for grp in await_chips:
            riders.append(_chip_exchange([partial_of[t][1] for t, _, _, _ in grp]))
            plan.append(("chips", grp))
        for grp in await_sibling:
            riders.append(_sibling_exchange([g for _, g, _, _ in grp], [k for _, _, k, _ in grp], [s for _, _, _, s in grp]))
            plan.append(("sibling", grp))
        del await_chips[:], await_sibling[:]
        return riders, plan

    def absorb(plan, delivered):
        for (what, grp), res in zip(plan, delivered):
            for (t, g, k, s), r in zip(grp, res):
                if what == "chips":
                    from_chips[t] = r
                else:
                    partial_of[t] = _chip_partial(g, r, k, s, cidx, f"grad_chip_partial_{t[0]}_{t[1]}")
            if what == "sibling":
                await_chips.append(grp)

    small = [None] * L
    for l in reversed(range(L)):
        lw = layers[l]
        p, h, ya, yb, yc = saved[l]
        dp, acts, merged, dys, gwc, gbs, gwpool, gvec = _mix_bwd(p, dx, ya, yb, yc, lw, f"mix_bwd_{l}")
        gwpa, gwpb, gwpc, gwo = _proj_wgrad(acts, merged, dys, dx, f"proj_wgrad_{l}")
        await_sibling.append([((l, a), g, kinds5[a], rs_sizes[a]) for a, g in ((1, gwpa), (2, gwpb), (3, gwpc), (4, gwo))])

        def bwd_x(dx):
            riders, plan = riders_now()
            (dx, dng), delivered = _inproj_bwd_x(dp, lw["win_t"], xs[l], norm_g[l], dx, f"inproj_bwd_x_{l}", riders)
            absorb(plan, delivered)
            return dx, dng

        def bwd_w():
            riders, plan = riders_now()
            (gwin_t,), delivered = _inproj_bwd_w(dp, h, f"inproj_bwd_w_{l}", riders)
            absorb(plan, delivered)
            await_sibling.append([((l, 0), gwin_t, kinds5[0], rs_sizes[0])])

        if l == L - 1:
            dx, dng = bwd_x(dx)
            bwd_w()
        else:
            bwd_w()
            dx, dng = bwd_x(dx)
        small[l] = dict(norm_g=dng[0], ln_g=gvec[V_LNG], ln_b=gvec[V_LNB], w_s=gwc, b_s=gbs, conv_w=gvec[V_CW0:V_CW0 + 3],
                        conv_b=gvec[V_CB], w_pool=gwpool, pool_scale=gvec[V_PS])
    while await_sibling or await_chips:
        riders, plan = riders_now()
        delivered = []
        for ex in riders:
            delivered.append(_run_exchange(ex, f"grad_exchange_tail_{serial[0]}"))
            serial[0] += 1
        absorb(plan, delivered)
    grad_x = dx[None]
    big_grads = []
    for l in range(L):
        tot = [_grad_total(partial_of[(l, a)][0], from_chips[(l, a)], qidx, f"grad_total_{l}_{a}") for a in range(5)]
        big_grads.append([tot[0].T,
                          tot[1].reshape(SEG, pc_loc), tot[2].reshape(SEG, pc_loc), tot[3].reshape(SEG, pc_loc),
                          tot[4]])

    names = ["norm_g", "ln_g", "ln_b", "w_s", "b_s", "conv_w", "conv_b", "w_pool", "pool_scale"]
    pieces = [_as_lanes(jnp.stack([small[l][nm] for l in range(L)])) for nm in names]
    pieces += [_as_lanes(dfg_acc[0]), loss_acc]
    sizes = [pc.shape[0] for pc in pieces]
    pack = jnp.concatenate(pieces, axis=0)
    pack = _pad_rows(pack, 8 * N_DEV)
    red = _all_reduce_small(pack, "small_grads_all_reduce")
    offs = [0]
    for s in sizes:
        offs.append(offs[-1] + s)

    def unpack(i, shape):
        n = math.prod(shape)
        return red[offs[i]:offs[i + 1]].reshape(-1)[:n].reshape(shape)

    g_norm_g = unpack(0, (L, D))
    g_ln_g = unpack(1, (L, SEG))
    g_ln_b = unpack(2, (L, SEG))
    g_w_s = unpack(3, (L, GROUPS, CHUNK, CHUNK))
    g_b_s = jnp.transpose(unpack(4, (L, CHUNK, LANES))[:, :, :GROUPS], (0, 2, 1))
    g_conv_w_full = unpack(5, (L, CONV_TAPS, SEG))
    g_conv_b = unpack(6, (L, SEG))
    g_w_pool = unpack(7, (L, len(POOL_WINDOWS), POOL_GROUP, POOL_GROUP))
    g_pool_scale = unpack(8, (L, SEG))
    g_final_g = unpack(9, (D,))
    loss = red[offs[10], 0]
    dev = 4 * xi + 2 * yi + ci
    g_conv_w = lax.dynamic_slice_in_dim(g_conv_w_full, dev * conv_w.shape[2], conv_w.shape[2], axis=2)

    g_w_in = jnp.stack([big_grads[l][0] for l in range(L)])
    g_w_pa = jnp.stack([big_grads[l][1] for l in range(L)])
    g_w_pb = jnp.stack([big_grads[l][2] for l in range(L)])
    g_w_pc = jnp.stack([big_grads[l][3] for l in range(L)])
    g_w_o = jnp.stack([big_grads[l][4] for l in range(L)])

    grads = dict(norm_g=g_norm_g, w_in=g_w_in, ln_g=g_ln_g, ln_b=g_ln_b, w_s=g_w_s, b_s=g_b_s, conv_w=g_conv_w,
                 conv_b=g_conv_b, w_pool=g_w_pool, pool_scale=g_pool_scale, w_pa=g_w_pa, w_pb=g_w_pb, w_pc=g_w_pc,
                 w_o=g_w_o, final_g=g_final_g)
    weights = dict(norm_g=norm_g, w_in=w_in, ln_g=ln_g, ln_b=ln_b, w_s=w_s, b_s=b_s, conv_w=conv_w, conv_b=conv_b,
                   w_pool=w_pool, pool_scale=pool_scale, w_pa=w_pa, w_pb=w_pb, w_pc=w_pc, w_o=w_o, final_g=final_g)
    ms = dict(norm_g=m_norm_g, w_in=m_w_in, ln_g=m_ln_g, ln_b=m_ln_b, w_s=m_w_s, b_s=m_b_s, conv_w=m_conv_w,
              conv_b=m_conv_b, w_pool=m_w_pool, pool_scale=m_pool_scale, w_pa=m_w_pa, w_pb=m_w_pb, w_pc=m_w_pc,
              w_o=m_w_o, final_g=m_final_g)
    vs = dict(norm_g=v_norm_g, w_in=v_w_in, ln_g=v_ln_g, ln_b=v_ln_b, w_s=v_w_s, b_s=v_b_s, conv_w=v_conv_w,
              conv_b=v_conv_b, w_pool=v_w_pool, pool_scale=v_pool_scale, w_pa=v_w_pa, w_pb=v_w_pb, w_pc=v_w_pc,
              w_o=v_w_o, final_g=v_final_g)
    order = ["norm_g", "w_in", "ln_g", "ln_b", "w_s", "b_s", "conv_w", "conv_b", "w_pool", "pool_scale", "w_pa", "w_pb",
             "w_pc", "w_o", "final_g"]

    delta, new_m, new_v = {}, {}, {}
    big = ["w_in", "w_pa", "w_pb", "w_pc", "w_o"]
    for nm in big:
        shp = weights[nm].shape
        two = lambda a: a.reshape(-1, shp[-1])
        d, mn, vn = _adamw(two(weights[nm]), two(grads[nm]), two(ms[nm]), two(vs[nm]), f"adamw_{nm}")
        delta[nm], new_m[nm], new_v[nm] = d.reshape(shp), mn.reshape(shp), vn.reshape(shp)
    rest = [nm for nm in order if nm not in big]
    cat = lambda src: jnp.concatenate([_as_lanes(src[nm]) for nm in rest], axis=0)
    d, mn, vn = _adamw(cat(weights), cat(grads), cat(ms), cat(vs), "adamw_small")
    off = 0
    for nm in rest:
        shp = weights[nm].shape
        n = math.prod(shp)
        rows = _as_lanes(weights[nm]).shape[0]
        cut = lambda a: a[off:off + rows].reshape(-1)[:n].reshape(shp)
        delta[nm], new_m[nm], new_v[nm] = cut(d), cut(mn), cut(vn)
        off += rows

    return (loss, grad_x, *[grads[nm] for nm in order], *[delta[nm] for nm in order],
            *[new_m[nm] for nm in order], *[new_v[nm] for nm in order])
```

```python
import functools
import math

import jax
import jax.numpy as jnp
from jax import lax
from jax.experimental import pallas as pl
from jax.experimental.pallas import tpu as pltpu

F32 = jnp.float32
BF16 = jnp.bfloat16
SDS = jax.ShapeDtypeStruct
MESH = pl.DeviceIdType.MESH

SEG = 512
CHUNK = 128
GROUPS = 8
HEAD = SEG // GROUPS
POOL_WINDOWS = (2, 4, 8, 16)
POOL_GROUP = SEG // len(POOL_WINDOWS)
CONV_TAPS = 3
HALO = 16
RMS_EPS = 1e-6
LN_EPS = 1e-5
ADAM_LR, ADAM_B1, ADAM_B2, ADAM_EPS, ADAM_WD, ADAM_STEP = 0.001, 0.9, 0.999, 1e-08, 0.01, 10

O_U, O_V, O_ZA, O_XB, O_BG, O_CG, O_ZB, O_XC, O_ZC, O_G = (SEG * i for i in range(10))

N_DEV = 8
N_CHIP = 4
LANES = 128
VMEM_LIMIT = 48 * 1024 * 1024
ADAMW_WHOLE_BYTES = 2 * 1024 * 1024
INPROJ_ROWS = 1024


def _cparams(sem=None, **kw):
    return pltpu.CompilerParams(dimension_semantics=sem, vmem_limit_bytes=VMEM_LIMIT, **kw)


def _pick(total, target, mult):
    best = None
    for d in range(mult, min(total, target) + 1, mult):
        if total % d == 0:
            best = d
    assert best is not None, (total, target, mult)
    return best


def _dot(a, b):
    return jnp.dot(a, b, preferred_element_type=F32)


def _dot_nt(a, b):
    return lax.dot_general(a, b, (((1,), (1,)), ((), ())), preferred_element_type=F32)


def _dot_tn(a, b):
    return lax.dot_general(a, b, (((0,), (0,)), ((), ())), preferred_element_type=F32)


def _zero(ref):
    ref[...] = jnp.zeros(ref.shape, ref.dtype)


def _sigmoid(x):
    return 1.0 / (1.0 + jnp.exp(-x))


_GELU_C = math.sqrt(2.0 / math.pi)


def _gelu(x):
    t = jnp.tanh(_GELU_C * (x + 0.044715 * x * x * x))
    return 0.5 * x * (1.0 + t), t


def _gelu_grad(x, t):
    return 0.5 * (1.0 + t) + 0.5 * x * (1.0 - t * t) * _GELU_C * (1.0 + 3.0 * 0.044715 * x * x)


def _me():
    return lax.axis_index("x"), lax.axis_index("y"), lax.axis_index("c")


def _inproj(x, norm_g, win_t, name, riders=()):
    T, D = x.shape
    N = win_t.shape[0]
    bt = _pick(T, INPROJ_ROWS, 16)
    bn = _pick(N, 1536, LANES)
    grid = (T // bt, N // bn)

    def compute(x_ref, g_ref, w_ref, p_ref, h_ref, hs_ref):
        @pl.when(pl.program_id(1) == 0)
        def _():
            xv = x_ref[...]
            rstd = lax.rsqrt(jnp.mean(xv * xv, axis=-1, keepdims=True) + RMS_EPS)
            hb = (xv * rstd * g_ref[...]).astype(BF16)
            hs_ref[...] = hb
            h_ref[...] = hb

        p_ref[...] = _dot_nt(hs_ref[...], w_ref[...]).astype(BF16)

    return _host_call(
        compute, name, grid, riders,
        inputs=[x, norm_g.reshape(1, D), win_t],
        in_specs=[pl.BlockSpec((bt, D), lambda i, j: (i, 0)),
                  pl.BlockSpec((1, D), lambda i, j: (0, 0)),
                  pl.BlockSpec((bn, D), lambda i, j: (j, 0))],
        out_specs=[pl.BlockSpec((bt, bn), lambda i, j: (i, j)),
                   pl.BlockSpec((bt, D), lambda i, j: (i, 0))],
        out_shape=[SDS((T, N), BF16), SDS((T, D), BF16)],
        scratch_shapes=[pltpu.VMEM((bt, D), BF16)])


C_LNG, C_LNB, C_CW0, C_CW1, C_CW2, C_CB, C_PS = range(7)
C_ROWS = 8


def _mixers(p_ref, hxb_ref, hcg_ref, hxc_ref, cv, bsb_ref, wcat_ref, wpool_ref, extb, extc,
            first, blk, R, need_grad):
    def seg(lo):
        return p_ref[:, lo:lo + SEG].astype(F32)

    u, v, za = seg(O_U), seg(O_V), seg(O_ZA)
    xb, bg, cg, zb = seg(O_XB), seg(O_BG), seg(O_CG), seg(O_ZB)
    xc, zc = seg(O_XC), seg(O_ZC)
    out = {}

    ug, tu = _gelu(u)
    vg, tv = _gelu(v)
    mu = jnp.mean(vg, axis=-1, keepdims=True)
    vcen = vg - mu
    rs = lax.rsqrt(jnp.mean(vcen * vcen, axis=-1, keepdims=True) + LN_EPS)
    vhat = vcen * rs
    vn = (vhat * cv[C_LNG:C_LNG + 1, :] + cv[C_LNB:C_LNB + 1, :]).astype(BF16)
    lane_group = lax.broadcasted_iota(jnp.int32, (CHUNK, SEG), 1) // HEAD
    zero_b = jnp.zeros((CHUNK, SEG), BF16)
    sgs = []
    for ci in range(R // CHUNK):
        vc = vn[ci * CHUNK:(ci + 1) * CHUNK]
        vst = jnp.concatenate([jnp.where(lane_group == g, vc, zero_b) for g in range(GROUPS)], axis=0)
        sgs.append(_dot(wcat_ref[...], vst) + bsb_ref[...])
    sg = sgs[0] if len(sgs) == 1 else jnp.concatenate(sgs, axis=0)
    a_out = ug * sg
    sa = _sigmoid(za)
    out["a"] = a_out * (za * sa)

    cx = cg * xb
    halo_b = hcg_ref[...].astype(F32) * hxb_ref[...].astype(F32)
    extb[0:HALO, :] = jnp.where(first, 0.0, halo_b)
    extb[HALO:HALO + R, :] = cx
    cx1 = extb[pl.ds(HALO - 1, R), :]
    cx2 = extb[pl.ds(HALO - 2, R), :]
    yconv = (cv[C_CW0:C_CW0 + 1, :] * cx2 + cv[C_CW1:C_CW1 + 1, :] * cx1
             + cv[C_CW2:C_CW2 + 1, :] * cx + cv[C_CB:C_CB + 1, :])
    b_out = bg * yconv
    sb = _sigmoid(zb)
    out["b"] = b_out * (zb * sb)

    extc[0:HALO, :] = jnp.where(first, 0.0, hxc_ref[...].astype(F32))
    extc[HALO:HALO + R, :] = xc
    tpos = blk * R + lax.broadcasted_iota(jnp.int32, (R, POOL_GROUP), 0) + 1
    pooled, invs, pws = [], [], []
    for gi, w in enumerate(POOL_WINDOWS):
        lo = gi * POOL_GROUP
        win = xc[:, lo:lo + POOL_GROUP]
        for j in range(1, w):
            win = win + extc[pl.ds(HALO - j, R), lo:lo + POOL_GROUP]
        inv = 1.0 / jnp.minimum(tpos, w).astype(F32)
        pg = (win * inv - xc[:, lo:lo + POOL_GROUP]).astype(BF16)
        pooled.append(pg)
        invs.append(inv)
        pws.append(_dot(pg, wpool_ref[gi]))
    pw = jnp.concatenate(pws, axis=1)
    c_out = pw * cv[C_PS:C_PS + 1, :]
    sc = _sigmoid(zc)
    out["c"] = c_out * (zc * sc)

    if need_grad:
        out.update(u=u, v=v, tu=tu, tv=tv, ug=ug, sg=sg, a_out=a_out, za=za, sa=sa,
                   rs=rs, vhat=vhat, vn=vn, lane_group=lane_group, zero_b=zero_b,
                   xb=xb, bg=bg, cg=cg, cx=cx, cx1=cx1, cx2=cx2, yconv=yconv, b_out=b_out, zb=zb, sb=sb,
                   pooled=pooled, invs=invs, pw=pw, c_out=c_out, zc=zc, sc=sc)
    return out


def _halo_specs(R, nb, rev):
    step = R // HALO

    def mk(col):
        def imap(i):
            b = (nb - 1 - i) if rev else i
            return (jnp.maximum(b * step - 1, 0), col)
        return pl.BlockSpec((HALO, SEG), imap)

    return [mk(O_XB // SEG), mk(O_CG // SEG), mk(O_XC // SEG)]


def _const_spec(shape):
    nd = len(shape)
    return pl.BlockSpec(shape, lambda i: (0,) * nd, pipeline_mode=pl.Buffered(1))


MIX_FWD_ROWS = 512
MIX_BWD_ROWS = 256


def _mix_block_rows(T, target):
    return _pick(T, target, CHUNK)


def _mix_fwd(p, x, lw, name):
    T, D = x.shape
    N = p.shape[1]
    R = _mix_block_rows(T, MIX_FWD_ROWS)
    nb = T // R

    def body(p_ref, hxb, hcg, hxc, x_ref, cv_ref, bsb_ref, wcat_ref, wpool_ref, wpa_ref, wpb_ref, wpc_ref,
             wo_ref, xo_ref, ya_ref, yb_ref, yc_ref, extb, extc):
        i = pl.program_id(0)
        cv = cv_ref[...]
        r = _mixers(p_ref, hxb, hcg, hxc, cv, bsb_ref, wcat_ref, wpool_ref, extb, extc,
                    i == 0, i, R, False)
        merged = None
        for k, (act, w_ref, y_ref) in enumerate(((r["a"], wpa_ref, ya_ref), (r["b"], wpb_ref, yb_ref),
                                                 (r["c"], wpc_ref, yc_ref))):
            y = _dot(act.astype(BF16), w_ref[...]).astype(BF16)
            y_ref[...] = y
            term = _sigmoid(p_ref[:, O_G + k * D:O_G + (k + 1) * D]) * y
            merged = term if merged is None else merged + term
        xo_ref[...] = x_ref[...] + _dot(merged, wo_ref[...])

    row = lambda w: pl.BlockSpec((R, w), lambda i: (i, 0))
    consts = [lw["cvec"], lw["bsb"], lw["wcat"], lw["wpool"], lw["wpa"], lw["wpb"], lw["wpc"], lw["wo"]]
    return pl.pallas_call(
        body, name=name, grid=(nb,),
        in_specs=[row(N)] + _halo_specs(R, nb, False) + [row(D)] + [_const_spec(c.shape) for c in consts],
        out_specs=[row(D), row(D), row(D), row(D)],
        out_shape=[SDS((T, D), F32), SDS((T, D), BF16), SDS((T, D), BF16), SDS((T, D), BF16)],
        scratch_shapes=[pltpu.VMEM((HALO + R, SEG), F32), pltpu.VMEM((HALO + R, SEG), F32)],
        compiler_params=_cparams(("arbitrary",)),
    )(p, p, p, p, x, *consts)


def _loss_head(x, final_g, target, name):
    T, D = x.shape
    bt = _pick(T, 512, 8)

    def body(x_ref, g_ref, t_ref, dx_ref, loss_ref, dg_ref):
        @pl.when(pl.program_id(0) == 0)
        def _():
            _zero(loss_ref)
            _zero(dg_ref)

        xv = x_ref[...]
        g = g_ref[...]
        rstd = lax.rsqrt(jnp.mean(xv * xv, axis=-1, keepdims=True) + RMS_EPS)
        xhat = xv * rstd
        err = xhat * g - t_ref[...]
        part = 0.5 * jnp.sum(jnp.sum(err * err, axis=-1, keepdims=True), axis=0, keepdims=True) / D
        loss_ref[...] += jnp.broadcast_to(part, loss_ref.shape)
        dy = err * (1.0 / D)
        dg_ref[0:1, :] += jnp.sum(dy * xhat, axis=0, keepdims=True)
        dxn = dy * g
        dx_ref[...] = rstd * (dxn - xhat * jnp.mean(dxn * xhat, axis=-1, keepdims=True))

    return pl.pallas_call(
        body, name=name, grid=(T // bt,),
        in_specs=[pl.BlockSpec((bt, D), lambda i: (i, 0)), _const_spec((1, D)), pl.BlockSpec((bt, D), lambda i: (i, 0))],
        out_specs=[pl.BlockSpec((bt, D), lambda i: (i, 0)), _const_spec((8, LANES)), _const_spec((8, D))],
        out_shape=[SDS((T, D), F32), SDS((8, LANES), F32), SDS((8, D), F32)],
        compiler_params=_cparams(("arbitrary",)),
    )(x, final_g.reshape(1, D), target)


V_LNG, V_LNB, V_CB, V_PS, V_CW0, V_CW1, V_CW2 = range(7)


def _mix_bwd(p, dxo, ya, yb, yc, lw, name):
    T, D = dxo.shape
    N = p.shape[1]
    R = _mix_block_rows(T, MIX_BWD_ROWS)
    nb = T // R

    def body(p_ref, hxb, hcg, hxc, dxo_ref, ya_ref, yb_ref, yc_ref, cv_ref, bsb_ref, wcat_ref, wcatt_ref,
             wpool_ref, wpa_ref, wpb_ref, wpc_ref, wo_ref,
             dp_ref, acts_ref, mrg_ref, dys_ref, gwc_ref, gbs_ref, gwpool_ref, gvec_ref,
             extb, extc, extdy, extq, cdy, cq, bsacc):
        i = pl.program_id(0)
        blk = nb - 1 - i

        @pl.when(i == 0)
        def _():
            for ref in (gwc_ref, gwpool_ref, gvec_ref, cdy, cq, bsacc):
                _zero(ref)

        cv = cv_ref[...]
        r = _mixers(p_ref, hxb, hcg, hxc, cv, bsb_ref, wcat_ref, wpool_ref, extb, extc,
                    blk == 0, blk, R, True)

        dxo_b = dxo_ref[...].astype(BF16)
        dm = _dot_nt(dxo_b, wo_ref[...]).astype(BF16)
        ys = [ya_ref[...], yb_ref[...], yc_ref[...]]
        sig = [_sigmoid(p_ref[:, O_G + k * D:O_G + (k + 1) * D]) for k in range(3)]
        mrg_ref[...] = sig[0] * ys[0] + sig[1] * ys[1] + sig[2] * ys[2]
        dacts = []
        for k, (act, w_ref) in enumerate(((r["a"], wpa_ref), (r["b"], wpb_ref), (r["c"], wpc_ref))):
            dyk = dm * sig[k]
            dp_ref[:, O_G + k * D:O_G + (k + 1) * D] = dyk * ys[k] * (1.0 - sig[k])
            acts_ref[:, k * SEG:(k + 1) * SEG] = act.astype(BF16)
            dys_ref[:, k * D:(k + 1) * D] = dyk
            dacts.append(_dot_nt(dyk, w_ref[...]))
        da, db, dc = dacts

        def silu_bwd(dact, pre, z, s):
            return dact * (z * s), dact * pre * (s * (1.0 + z * (1.0 - s)))

        d_aout, dza = silu_bwd(da, r["a_out"], r["za"], r["sa"])
        dp_ref[:, O_ZA:O_ZA + SEG] = dza.astype(BF16)
        dp_ref[:, O_U:O_U + SEG] = (d_aout * r["sg"] * _gelu_grad(r["u"], r["tu"])).astype(BF16)
        d_sg = d_aout * r["ug"]
        dvns = []
        for ci in range(R // CHUNK):
            dsc = d_sg[ci * CHUNK:(ci + 1) * CHUNK]
            bsacc[...] += dsc
            dsc_b = dsc.astype(BF16)
            dst = jnp.concatenate([jnp.where(r["lane_group"] == g, dsc_b, r["zero_b"]) for g in range(GROUPS)], axis=0)
            dvns.append(_dot(wcatt_ref[...], dst))
            gwc_ref[...] += _dot_nt(dst, r["vn"][ci * CHUNK:(ci + 1) * CHUNK])
        d_vn = dvns[0] if len(dvns) == 1 else jnp.concatenate(dvns, axis=0)
        vhat = r["vhat"]
        gvec_ref[V_LNG:V_LNG + 1, :] += jnp.sum(d_vn * vhat, axis=0, keepdims=True)
        gvec_ref[V_LNB:V_LNB + 1, :] += jnp.sum(d_vn, axis=0, keepdims=True)
        d_vhat = d_vn * cv[C_LNG:C_LNG + 1, :]
        d_vg = r["rs"] * (d_vhat - jnp.mean(d_vhat, axis=-1, keepdims=True)
                          - vhat * jnp.mean(d_vhat * vhat, axis=-1, keepdims=True))
        dp_ref[:, O_V:O_V + SEG] = (d_vg * _gelu_grad(r["v"], r["tv"])).astype(BF16)

        d_bout, dzb = silu_bwd(db, r["b_out"], r["zb"], r["sb"])
        dp_ref[:, O_ZB:O_ZB + SEG] = dzb.astype(BF16)
        dp_ref[:, O_BG:O_BG + SEG] = (d_bout * r["yconv"]).astype(BF16)
        d_y = d_bout * r["bg"]
        gvec_ref[V_CB:V_CB + 1, :] += jnp.sum(d_y, axis=0, keepdims=True)
        gvec_ref[V_CW0:V_CW0 + 1, :] += jnp.sum(d_y * r["cx2"], axis=0, keepdims=True)
        gvec_ref[V_CW1:V_CW1 + 1, :] += jnp.sum(d_y * r["cx1"], axis=0, keepdims=True)
        gvec_ref[V_CW2:V_CW2 + 1, :] += jnp.sum(d_y * r["cx"], axis=0, keepdims=True)
        extdy[0:R, :] = d_y
        extdy[R:R + HALO, :] = cdy[...]
        d_cx = (cv[C_CW2:C_CW2 + 1, :] * d_y + cv[C_CW1:C_CW1 + 1, :] * extdy[pl.ds(1, R), :]
                + cv[C_CW0:C_CW0 + 1, :] * extdy[pl.ds(2, R), :])
        cdy[...] = d_y[0:HALO]
        dp_ref[:, O_CG:O_CG + SEG] = (d_cx * r["xb"]).astype(BF16)
        dp_ref[:, O_XB:O_XB + SEG] = (d_cx * r["cg"]).astype(BF16)

        d_cout, dzc = silu_bwd(dc, r["c_out"], r["zc"], r["sc"])
        dp_ref[:, O_ZC:O_ZC + SEG] = dzc.astype(BF16)
        gvec_ref[V_PS:V_PS + 1, :] += jnp.sum(d_cout * r["pw"], axis=0, keepdims=True)
        d_pw = (d_cout * cv[C_PS:C_PS + 1, :]).astype(BF16)
        dpool = []
        for gi, w in enumerate(POOL_WINDOWS):
            lo = gi * POOL_GROUP
            dpw_g = d_pw[:, lo:lo + POOL_GROUP]
            gwpool_ref[lo:lo + POOL_GROUP, :] += _dot_tn(r["pooled"][gi], dpw_g)
            dpg = _dot_nt(dpw_g, wpool_ref[gi])
            dpool.append(dpg)
            extq[0:R, lo:lo + POOL_GROUP] = dpg * r["invs"][gi]
        extq[R:R + HALO, :] = cq[...]
        for gi, w in enumerate(POOL_WINDOWS):
            lo = gi * POOL_GROUP
            acc = extq[0:R, lo:lo + POOL_GROUP]
            for j in range(1, w):
                acc = acc + extq[pl.ds(j, R), lo:lo + POOL_GROUP]
            dp_ref[:, O_XC + lo:O_XC + lo + POOL_GROUP] = (acc - dpool[gi]).astype(BF16)
        cq[...] = extq[0:HALO, :]

        @pl.when(i == nb - 1)
        def _():
            rr = lax.broadcasted_iota(jnp.int32, gwc_ref.shape, 0) % CHUNK
            cc = lax.broadcasted_iota(jnp.int32, gwc_ref.shape, 1)
            gwc_ref[...] = jnp.where(cc <= rr, gwc_ref[...], 0.0)
            acc = bsacc[...]
            hi = acc.astype(BF16)
            lo_ = (acc - hi.astype(F32)).astype(BF16)
            sel = (lax.broadcasted_iota(jnp.int32, (SEG, LANES), 0) // HEAD
                   == lax.broadcasted_iota(jnp.int32, (SEG, LANES), 1)).astype(BF16)
            gbs_ref[...] = _dot(hi, sel) + _dot(lo_, sel)

    row = lambda w: pl.BlockSpec((R, w), lambda i: (nb - 1 - i, 0))
    consts = [lw["cvec"], lw["bsb"], lw["wcat"], lw["wcatt"], lw["wpool"], lw["wpa"], lw["wpb"], lw["wpc"], lw["wo"]]
    acc_shapes = [(GROUPS * CHUNK, CHUNK), (CHUNK, LANES), (SEG, POOL_GROUP), (8, SEG)]
    row_widths = [N, 3 * SEG, D, 3 * D]
    return pl.pallas_call(
        body, name=name, grid=(nb,),
        in_specs=([row(N)] + _halo_specs(R, nb, True) + [row(D), row(D), row(D), row(D)]
                  + [_const_spec(c.shape) for c in consts]),
        out_specs=[row(w) for w in row_widths] + [_const_spec(s) for s in acc_shapes],
        out_shape=[SDS((T, w), BF16) for w in row_widths] + [SDS(s, F32) for s in acc_shapes],
        scratch_shapes=[pltpu.VMEM((HALO + R, SEG), F32)] * 4
        + [pltpu.VMEM((HALO, SEG), F32), pltpu.VMEM((HALO, SEG), F32), pltpu.VMEM((CHUNK, SEG), F32)],
        compiler_params=_cparams(("arbitrary",)),
    )(p, p, p, p, dxo, ya, yb, yc, *consts)


def _proj_wgrad(acts, merged, dys, dxo, name):
    T, D = dxo.shape
    bk = _pick(T, 512, 16)

    def body(a_ref, m_ref, dy_ref, dxo_ref, gwpa_ref, gwpb_ref, gwpc_ref, gwo_ref):
        @pl.when(pl.program_id(0) == 0)
        def _():
            for ref in (gwpa_ref, gwpb_ref, gwpc_ref, gwo_ref):
                _zero(ref)

        gwo_ref[...] += _dot_tn(m_ref[...], dxo_ref[...].astype(BF16))
        for k, ref in enumerate((gwpa_ref, gwpb_ref, gwpc_ref)):
            ref[...] += _dot_tn(a_ref[:, k * SEG:(k + 1) * SEG], dy_ref[:, k * D:(k + 1) * D])

    row = lambda w: pl.BlockSpec((bk, w), lambda i: (i, 0))
    shapes = [(SEG, D), (SEG, D), (SEG, D), (D, D)]
    return pl.pallas_call(
        body, name=name, grid=(T // bk,),
        in_specs=[row(3 * SEG), row(D), row(3 * D), row(D)],
        out_specs=[_const_spec(s) for s in shapes], out_shape=[SDS(s, F32) for s in shapes],
        compiler_params=_cparams(("arbitrary",)),
    )(acts, merged, dys, dxo)


def _inproj_token_blocks(T):
    return T // _pick(T, INPROJ_ROWS, 16)


def _inproj_bwd_x(dp, win_t, x, norm_g, dxo, name, riders=(), blocks=None, fill=None):
    T, D = x.shape
    N = dp.shape[1]
    bt = _pick(T, INPROJ_ROWS, 16)
    bk = _pick(N, 1536, LANES)
    nk = N // bk
    b0, nblk = blocks if blocks else (0, T // bt)

    def compute(dp_ref, w_ref, x_ref, g_ref, dxo_ref, *rest):
        dx_ref, dg_ref, acc_ref = rest[-3:]
        i, k = pl.program_id(0), pl.program_id(1)

        @pl.when((i == 0) & (k == 0))
        def _():
            _zero(dg_ref)

        @pl.when(k == 0)
        def _():
            _zero(acc_ref)

        acc_ref[...] += _dot(dp_ref[...], w_ref[...])

        @pl.when(k == nk - 1)
        def _():
            dh = acc_ref[...]
            xv = x_ref[...]
            rstd = lax.rsqrt(jnp.mean(xv * xv, axis=-1, keepdims=True) + RMS_EPS)
            xhat = xv * rstd
            dg_ref[0:1, :] += jnp.sum(dh * xhat, axis=0, keepdims=True)
            dxn = dh * g_ref[...]
            dx_ref[...] = dxo_ref[...] + rstd * (dxn - xhat * jnp.mean(dxn * xhat, axis=-1, keepdims=True))

    rows = pl.BlockSpec((bt, D), lambda i, k: (i + b0, 0))
    return _host_call(
        compute, name, (nblk, nk), riders,
        inputs=[dp, win_t, x, norm_g.reshape(1, D), dxo] + ([] if fill is None else [fill]),
        in_specs=[pl.BlockSpec((bt, bk), lambda i, k: (i + b0, k)),
                  pl.BlockSpec((bk, D), lambda i, k: (k, 0)),
                  rows,
                  pl.BlockSpec((1, D), lambda i, k: (0, 0)),
                  rows] + ([] if fill is None else [_ANY]),
        out_specs=[rows, pl.BlockSpec((8, D), lambda i, k: (0, 0))],
        out_shape=[SDS((T, D), F32), SDS((8, D), F32)],
        scratch_shapes=[pltpu.VMEM((bt, D), F32)],
        aliases={} if fill is None else {5: 0})


def _inproj_bwd_w(dp, h, name, riders=()):
    T, N = dp.shape
    D = h.shape[1]
    bn = _pick(N, 1920, LANES)
    bk = _pick(T, 1024, 16)
    nk = T // bk

    def compute(dp_ref, h_ref, o_ref):
        @pl.when(pl.program_id(1) == 0)
        def _():
            _zero(o_ref)

        o_ref[...] += _dot_tn(dp_ref[...], h_ref[...])

    return _host_call(
        compute, name, (N // bn, nk), riders,
        inputs=[dp, h],
        in_specs=[pl.BlockSpec((bk, bn), lambda j, k: (k, j)), pl.BlockSpec((bk, D), lambda j, k: (k, 0))],
        out_specs=[pl.BlockSpec((bn, D), lambda j, k: (j, 0))],
        out_shape=[SDS((N, D), F32)],
        scratch_shapes=[])


def _chip_peer(x, y, j):
    px = (1 - x) if (j >> 1) else x
    py = (1 - y) if (j & 1) else y
    return px, py


def _blk(ref, kind, k, n):
    if kind == "rows":
        return ref.at[pl.ds(pl.multiple_of(k * n, 8), n)]
    return ref.at[:, pl.ds(pl.multiple_of(k * n, LANES), n)]


class _Exchange:
    def __init__(self, srcs, out_shapes, n_sems, build, alias=False):
        self.srcs, self.out_shapes, self.n_sems, self.build, self.alias = list(srcs), list(out_shapes), n_sems, build, alias


def _rider_plan(riders):
    inputs = [s for e in riders for s in e.srcs]
    out_shapes = [o for e in riders for o in e.out_shapes]
    sems = [pltpu.SemaphoreType.DMA((e.n_sems,)) for e in riders for _ in range(2)]

    def copies(in_refs, out_refs, sem_refs):
        cps, i, o = [], 0, 0
        for k, e in enumerate(riders):
            ni, no = len(e.srcs), len(e.out_shapes)
            cps += e.build(in_refs[i:i + ni], out_refs[o:o + no], sem_refs[2 * k], sem_refs[2 * k + 1])
            i, o = i + ni, o + no
        return cps

    return inputs, out_shapes, sems, copies


_ANY = pl.BlockSpec(memory_space=pl.ANY)


def _host_call(compute, name, grid, riders, inputs, in_specs, out_specs, out_shape, scratch_shapes, aliases=None):
    r_in, r_out, r_sems, copies = _rider_plan(riders)
    ni, no, ns = len(inputs), len(out_shape), len(scratch_shapes)

    def body(*refs):
        ins, rins = refs[:ni], refs[ni:ni + len(r_in)]
        outs = refs[ni + len(r_in):ni + len(r_in) + no]
        routs = refs[ni + len(r_in) + no:ni + len(r_in) + no + len(r_out)]
        scr = refs[ni + len(r_in) + no + len(r_out):]
        first = functools.reduce(lambda a, b: a & b, [pl.program_id(d) == 0 for d in range(len(grid))])
        last = functools.reduce(lambda a, b: a & b, [pl.program_id(d) == grid[d] - 1 for d in range(len(grid))])
        if riders:
            @pl.when(first)
            def _():
                for cp in copies(rins, routs, scr[ns:]):
                    cp.start()

        compute(*ins, *outs, *scr[:ns])

        if riders:
            @pl.when(last)
            def _():
                for cp in copies(rins, routs, scr[ns:]):
                    cp.wait()

    res = pl.pallas_call(
        body, name=name, grid=grid,
        in_specs=list(in_specs) + [_ANY] * len(r_in),
        out_specs=list(out_specs) + [_ANY] * len(r_out),
        out_shape=list(out_shape) + r_out,
        scratch_shapes=list(scratch_shapes) + r_sems,
        input_output_aliases=aliases or {},
        compiler_params=_cparams(("arbitrary",) * len(grid)),
    )(*inputs, *r_in)
    return res[:no], _split_riders(riders, res[no:])


def _split_riders(riders, flat):
    out, o = [], 0
    for e in riders:
        out.append(list(flat[o:o + len(e.out_shapes)]))
        o += len(e.out_shapes)
    return out


def _run_exchange(ex, name):
    n_in, n_out = len(ex.srcs), len(ex.out_shapes)

    def body(*refs):
        cps = ex.build(refs[:n_in], refs[n_in:n_in + n_out], refs[n_in + n_out], refs[n_in + n_out + 1])
        for cp in cps:
            cp.start()
        for cp in cps:
            cp.wait()

    return pl.pallas_call(
        body, name=name,
        in_specs=[_ANY] * n_in, out_specs=[_ANY] * n_out, out_shape=ex.out_shapes,
        input_output_aliases={i: i for i in range(n_in)} if ex.alias else {},
        scratch_shapes=[pltpu.SemaphoreType.DMA((ex.n_sems,)), pltpu.SemaphoreType.DMA((ex.n_sems,))],
        compiler_params=pltpu.CompilerParams(has_side_effects=True),
    )(*ex.srcs)


def _gather_sizes(shards, kinds):
    sizes = [s.shape[0] if k == "rows" else s.shape[1] for s, k in zip(shards, kinds)]
    fulls = [SDS((s.shape[0] * N_DEV,) + s.shape[1:], s.dtype) if k == "rows"
             else SDS((s.shape[0], s.shape[1] * N_DEV), s.dtype) for s, k in zip(shards, kinds)]
    return sizes, fulls


def _gather_direct(shards, kinds):
    n = len(shards)
    sizes, fulls = _gather_sizes(shards, kinds)

    def build(ins, outs, send_sems, recv_sems):
        x, y, c = _me()
        cps = []
        for a in range(n):
            mine = _blk(outs[a], kinds[a], 4 * x + 2 * y + c, sizes[a])
            cps.append(pltpu.make_async_copy(ins[a], mine, send_sems.at[5 * a + 4]))
            for j in range(N_CHIP):
                to = (x, y, 1 - c) if j == 0 else (*_chip_peer(x, y, j), c)
                cps.append(pltpu.make_async_remote_copy(
                    src_ref=ins[a], dst_ref=mine, send_sem=send_sems.at[5 * a + j], recv_sem=recv_sems.at[5 * a + j],
                    device_id=to, device_id_type=MESH))
        return cps

    return _Exchange(shards, fulls, 5 * n, build)


def _gather_forward(fulls, kinds, sizes):
    n = len(fulls)

    def build(ins, outs, send_sems, recv_sems):
        x, y, c = _me()
        cps = []
        for a in range(n):
            for j in (1, 2, 3):
                px, py = _chip_peer(x, y, j)
                k = 4 * px + 2 * py + c
                cps.append(pltpu.make_async_remote_copy(
                    src_ref=_blk(ins[a], kinds[a], k, sizes[a]), dst_ref=_blk(outs[a], kinds[a], k, sizes[a]),
                    send_sem=send_sems.at[3 * a + j - 1], recv_sem=recv_sems.at[3 * a + j - 1],
                    device_id=(x, y, 1 - c), device_id_type=MESH))
        return cps

    return _Exchange(fulls, [SDS(f.shape, f.dtype) for f in fulls], 3 * n, build, alias=True)


def _sibling_exchange(grads, kinds, sizes):
    n = len(grads)

    def blk_shape(a):
        g = grads[a]
        return (sizes[a],) + g.shape[1:] if kinds[a] == "rows" else (g.shape[0], sizes[a])

    def build(ins, outs, send_sems, recv_sems):
        x, y, c = _me()
        cps = []
        for a in range(n):
            for q in range(N_CHIP):
                cps.append(pltpu.make_async_remote_copy(
                    src_ref=_blk(ins[a], kinds[a], 2 * q + (1 - c), sizes[a]), dst_ref=outs[a].at[q],
                    send_sem=send_sems.at[N_CHIP * a + q], recv_sem=recv_sems.at[N_CHIP * a + q],
                    device_id=(x, y, 1 - c), device_id_type=MESH))
        return cps

    return _Exchange(grads, [SDS((N_CHIP,) + blk_shape(a), F32) for a in range(n)], N_CHIP * n, build)


def _chip_partial(g, r1, kind, size, cidx, name):
    if kind == "rows":
        rows, cols = size, g.shape[1]
        g3 = g.reshape(N_DEV, rows, cols)
        rb = _pick(rows, 512, 16)
        g_spec = pl.BlockSpec((1, rb, cols), lambda q, j, c: (2 * q + c[0], j, 0))
        grid = (N_CHIP, rows // rb)
        blk = (1, rb, cols)
        imap = lambda q, j, c: (q, j, 0)
    else:
        rows, cols = g.shape[0], size
        g3 = g
        g_spec = pl.BlockSpec((rows, cols), lambda q, j, c: (0, 2 * q + c[0]))
        grid = (N_CHIP, 1)
        blk = (1, rows, cols)
        imap = lambda q, j, c: (q, 0, 0)

    def body(c_ref, g_ref, r_ref, p_ref, pb_ref):
        s = g_ref[...].reshape(blk) + r_ref[...]
        p_ref[...] = s
        pb_ref[...] = s.astype(BF16)

    return pl.pallas_call(
        body, name=name,
        grid_spec=pltpu.PrefetchScalarGridSpec(
            num_scalar_prefetch=1, grid=grid,
            in_specs=[g_spec, pl.BlockSpec(blk, imap)],
            out_specs=[pl.BlockSpec(blk, imap), pl.BlockSpec(blk, imap)]),
        out_shape=[SDS((N_CHIP, rows, cols), F32), SDS((N_CHIP, rows, cols), BF16)],
        compiler_params=_cparams(("arbitrary", "arbitrary")),
    )(cidx, g3, r1)


def _chip_exchange(parts):
    n = len(parts)
    m = N_CHIP - 1

    def build(ins, outs, send_sems, recv_sems):
        x, y, c = _me()
        cps = []
        for a in range(n):
            for j in (1, 2, 3):
                px, py = _chip_peer(x, y, j)
                cps.append(pltpu.make_async_remote_copy(
                    src_ref=ins[a].at[2 * px + py], dst_ref=outs[a].at[j - 1], send_sem=send_sems.at[m * a + j - 1],
                    recv_sem=recv_sems.at[m * a + j - 1], device_id=(px, py, c), device_id_type=MESH))
        return cps

    return _Exchange(parts, [SDS((m,) + p.shape[1:], BF16) for p in parts], m * n, build)


def _grad_total(part, r2, qidx, name):
    _, rows, cols = part.shape
    rb = _pick(rows, 512, 16)

    def body(q_ref, p_ref, r_ref, o_ref):
        s = p_ref[0]
        for j in range(N_CHIP - 1):
            s = s + r_ref[j].astype(F32)
        o_ref[...] = s

    return pl.pallas_call(
        body, name=name,
        grid_spec=pltpu.PrefetchScalarGridSpec(
            num_scalar_prefetch=1, grid=(rows // rb,),
            in_specs=[pl.BlockSpec((1, rb, cols), lambda i, q: (q[0], i, 0)),
                      pl.BlockSpec((N_CHIP - 1, rb, cols), lambda i, q: (0, i, 0))],
            out_specs=pl.BlockSpec((rb, cols), lambda i, q: (i, 0))),
        out_shape=SDS((rows, cols), F32),
        compiler_params=_cparams(("arbitrary",)),
    )(qidx, part, r2)


def _all_reduce_small(pack, name):
    rows = pack.shape[0]
    rs = rows // N_DEV
    assert rs * N_DEV == rows and rs % 8 == 0

    def body(x_ref, o_ref, rbuf, red, send1, recv1, send2, recv2):
        x, y, c = _me()
        me = 4 * x + 2 * y + c

        def peer(d):
            px = (1 - x) if (d >> 2) & 1 else x
            py = (1 - y) if (d >> 1) & 1 else y
            pc = (1 - c) if d & 1 else c
            return px, py, pc

        def sl(ref, k):
            return ref.at[pl.ds(pl.multiple_of(k * rs, 8), rs)]

        phase1 = []
        for d in range(1, N_DEV):
            px, py, pc = peer(d)
            phase1.append(pltpu.make_async_remote_copy(
                src_ref=sl(x_ref, 4 * px + 2 * py + pc), dst_ref=rbuf.at[d], send_sem=send1.at[d], recv_sem=recv1.at[d],
                device_id=(px, py, pc), device_id_type=MESH))
        for cp in phase1:
            cp.start()
        acc = sl(x_ref, me)[...]
        for cp in phase1:
            cp.wait()
        for d in range(1, N_DEV):
            acc = acc + rbuf[d]
        red[...] = acc
        sl(o_ref, me)[...] = acc
        phase2 = []
        for d in range(1, N_DEV):
            px, py, pc = peer(d)
            phase2.append(pltpu.make_async_remote_copy(
                src_ref=red, dst_ref=sl(o_ref, me), send_sem=send2.at[d], recv_sem=recv2.at[d],
                device_id=(px, py, pc), device_id_type=MESH))
        for cp in phase2:
            cp.start()
        for cp in phase2:
            cp.wait()

    vm = pl.BlockSpec(memory_space=pltpu.VMEM)
    return pl.pallas_call(
        body, name=name, in_specs=[vm], out_specs=vm, out_shape=SDS(pack.shape, F32),
        scratch_shapes=[pltpu.VMEM((N_DEV, rs, LANES), F32), pltpu.VMEM((rs, LANES), F32),
                        pltpu.SemaphoreType.DMA((N_DEV,)), pltpu.SemaphoreType.DMA((N_DEV,)),
                        pltpu.SemaphoreType.DMA((N_DEV,)), pltpu.SemaphoreType.DMA((N_DEV,))],
        compiler_params=_cparams(None, has_side_effects=True),
    )(pack)


def _adamw(w, g, m, v, name):
    rows, cols = w.shape
    rb = rows if rows * cols * 4 <= ADAMW_WHOLE_BYTES else _pick(rows, 256, 8)
    c1 = 1.0 / (1.0 - ADAM_B1 ** ADAM_STEP)
    c2 = 1.0 / (1.0 - ADAM_B2 ** ADAM_STEP)

    def body(w_ref, g_ref, m_ref, v_ref, d_ref, mo_ref, vo_ref):
        gv = g_ref[...]
        mn = ADAM_B1 * m_ref[...] + (1.0 - ADAM_B1) * gv
        vn = ADAM_B2 * v_ref[...] + (1.0 - ADAM_B2) * (gv * gv)
        mo_ref[...] = mn
        vo_ref[...] = vn
        d_ref[...] = -ADAM_LR * ((mn * c1) / (jnp.sqrt(vn * c2) + ADAM_EPS) + ADAM_WD * w_ref[...])

    spec = pl.BlockSpec((rb, cols), lambda i: (i, 0))
    return pl.pallas_call(
        body, name=name, grid=(rows // rb,),
        in_specs=[spec] * 4, out_specs=[spec] * 3, out_shape=[SDS((rows, cols), F32)] * 3,
        compiler_params=_cparams(("arbitrary",)),
    )(w, g, m, v)


def _pad_rows(a, mult=8):
    r = (-a.shape[0]) % mult
    return a if r == 0 else jnp.pad(a, ((0, r), (0, 0)))


def _as_lanes(a):
    flat = a.reshape(-1)
    pad = (-flat.shape[0]) % (8 * LANES)
    if pad:
        flat = jnp.pad(flat, (0, pad))
    return flat.reshape(-1, LANES)


def kernel(x, norm_g, w_in, ln_g, ln_b, w_s, b_s, conv_w, conv_b, w_pool, pool_scale, w_pa, w_pb, w_pc, w_o, final_g, loss_target, m_norm_g, m_w_in, m_ln_g, m_ln_b, m_w_s, m_b_s, m_conv_w, m_conv_b, m_w_pool, m_pool_scale, m_w_pa, m_w_pb, m_w_pc, m_w_o, m_final_g, v_norm_g, v_w_in, v_ln_g, v_ln_b, v_w_s, v_b_s, v_conv_w, v_conv_b, v_w_pool, v_pool_scale, v_w_pa, v_w_pb, v_w_pc, v_w_o, v_final_g):
    L = w_in.shape[0]
    D = x.shape[-1]
    n_loc = w_in.shape[2]
    pc_loc = w_pa.shape[2]
    x0 = x[0]
    target = loss_target[0]
    xi, yi, ci = _me()
    cidx = jnp.reshape(ci, (1,)).astype(jnp.int32)
    qidx = jnp.reshape(2 * xi + yi, (1,)).astype(jnp.int32)

    kinds5 = ["rows", "cols", "cols", "cols", "rows"]

    def layer_shards(l):
        return [w_in[l].T.astype(BF16), w_pa[l].astype(BF16), w_pb[l].astype(BF16), w_pc[l].astype(BF16),
                w_o[l].astype(BF16)]

    def gathered(direct, shards, kinds, l):
        sizes, _ = _gather_sizes(shards, kinds)
        return _run_exchange(_gather_forward(direct, kinds, sizes), f"weights_forward_{l}")

    cw_loc = _pad_rows(conv_w.reshape(L * CONV_TAPS, -1))
    cw_loc = jnp.pad(cw_loc, ((0, 0), (0, LANES - cw_loc.shape[1])))
    sh0, k0 = layer_shards(0) + [cw_loc], kinds5 + ["rows"]
    full0 = gathered(_run_exchange(_gather_direct(sh0, k0), "weights_gather_0"), sh0, k0, 0)
    cw_all = full0[-1].reshape(N_DEV, -1, LANES)[:, :L * CONV_TAPS, :conv_w.shape[2]]
    conv_w_full = jnp.transpose(cw_all, (1, 0, 2)).reshape(L, CONV_TAPS, -1)
    causal = jnp.tril(jnp.ones((CHUNK, CHUNK), dtype=bool))

    def make_layer(l, full5):
        win_t, wpa, wpb, wpc, wo = full5
        wm = jnp.where(causal, w_s[l], 0.0)
        cvec = jnp.concatenate([ln_g[l][None], ln_b[l][None], conv_w_full[l], conv_b[l][None], pool_scale[l][None],
                                jnp.zeros((C_ROWS - 7, SEG), F32)], axis=0)
        return dict(
            win_t=win_t, wpa=wpa, wpb=wpb, wpc=wpc, wo=wo, cvec=cvec,
            bsb=jnp.repeat(b_s[l].T, HEAD, axis=1),
            wcat=jnp.transpose(wm, (1, 0, 2)).reshape(CHUNK, GROUPS * CHUNK).astype(BF16),
            wcatt=jnp.transpose(wm, (2, 0, 1)).reshape(CHUNK, GROUPS * CHUNK).astype(BF16),
            wpool=w_pool[l].astype(BF16))

    layers = [make_layer(0, full0[:5])]
    xs, saved = [x0], []
    for l in range(L):
        lw = layers[l]
        nxt = layer_shards(l + 1) if l + 1 < L else None
        (p, h), delivered = _inproj(xs[-1], norm_g[l], lw["win_t"], f"inproj_fwd_{l}",
                                    [_gather_direct(nxt, kinds5)] if nxt else [])
        if nxt:
            layers.append(make_layer(l + 1, gathered(delivered[0], nxt, kinds5, l + 1)))
        xn, ya, yb, yc = _mix_fwd(p, xs[-1], lw, f"mix_fwd_{l}")
        saved.append((p, h, ya, yb, yc))
        xs.append(xn)
    dx, loss_acc, dfg_acc = _loss_head(xs[-1], final_g, target, "loss_head")

    rs_sizes = [n_loc, pc_loc, pc_loc, pc_loc, w_o.shape[1]]
    await_sibling, await_chips = [], []
    partial_of, from_chips = {}, {}
    serial = [0]

    def riders_now():
        riders, plan = [], []
        for grp in await_chips:
            riders.append(_chip_exchange([partial_of[t][1] for t, _, _, _ in grp]))
            plan.append(("chips", grp))
        for grp in await_sibling:
            riders.append(_sibling_exchange([g for _, g, _, _ in grp], [k for _, _, k, _ in grp], [s for _, _, _, s in grp]))
            plan.append(("sibling", grp))
        del await_chips[:], await_sibling[:]
        return riders, plan

    def absorb(plan, delivered):
        for (what, grp), res in zip(plan, delivered):
            for (t, g, k, s), r in zip(grp, res):
                if what == "chips":
                    from_chips[t] = r
                else:
                    partial_of[t] = _chip_partial(g, r, k, s, cidx, f"grad_chip_partial_{t[0]}_{t[1]}")
            if what == "sibling":
                await_chips.append(grp)

    small = [None] * L
    for l in reversed(range(L)):
        lw = layers[l]
        p, h, ya, yb, yc = saved[l]
        dp, acts, merged, dys, gwc, gbs, gwpool, gvec = _mix_bwd(p, dx, ya, yb, yc, lw, f"mix_bwd_{l}")
        gwpa, gwpb, gwpc, gwo = _proj_wgrad(acts, merged, dys, dx, f"proj_wgrad_{l}")
        await_sibling.append([((l, a), g, kinds5[a], rs_sizes[a]) for a, g in ((1, gwpa), (2, gwpb), (3, gwpc), (4, gwo))])

        def bwd_x(dxo, pieces):
            nt = _inproj_token_blocks(dxo.shape[0])
            pieces = min(pieces, nt)
            done, dng, b0 = None, None, 0
            for k in range(pieces):
                cnt = (nt - b0) // (pieces - k)
                riders, plan = riders_now()
                (done, dng_k), delivered = _inproj_bwd_x(dp, lw["win_t"], xs[l], norm_g[l], dxo, f"inproj_bwd_x_{l}_{k}",
                                                         riders, blocks=(b0, cnt), fill=done)
                absorb(plan, delivered)
                dng = dng_k if dng is None else dng + dng_k
                b0 += cnt
            return done, dng

        def bwd_w():
            riders, plan = riders_now()
            (gwin_t,), delivered = _inproj_bwd_w(dp, h, f"inproj_bwd_w_{l}", riders)
            absorb(plan, delivered)
            await_sibling.append([((l, 0), gwin_t, kinds5[0], rs_sizes[0])])

        if l == L - 1:
            dx, dng = bwd_x(dx, 1)
            bwd_w()
        else:
            bwd_w()
            dx, dng = bwd_x(dx, 2 if l == 0 else 1)
        small[l] = dict(norm_g=dng[0], ln_g=gvec[V_LNG], ln_b=gvec[V_LNB], w_s=gwc, b_s=gbs, conv_w=gvec[V_CW0:V_CW0 + 3],
                        conv_b=gvec[V_CB], w_pool=gwpool, pool_scale=gvec[V_PS])
    while await_sibling or await_chips:
        riders, plan = riders_now()
        delivered = []
        for ex in riders:
            delivered.append(_run_exchange(ex, f"grad_exchange_tail_{serial[0]}"))
            serial[0] += 1
        absorb(plan, delivered)
    grad_x = dx[None]
    big_grads = []
    for l in range(L):
        tot = [_grad_total(partial_of[(l, a)][0], from_chips[(l, a)], qidx, f"grad_total_{l}_{a}") for a in range(5)]
        big_grads.append([tot[0].T,
                          tot[1].reshape(SEG, pc_loc), tot[2].reshape(SEG, pc_loc), tot[3].reshape(SEG, pc_loc),
                          tot[4]])

    names = ["norm_g", "ln_g", "ln_b", "w_s", "b_s", "conv_w", "conv_b", "w_pool", "pool_scale"]
    pieces = [_as_lanes(jnp.stack([small[l][nm] for l in range(L)])) for nm in names]
    pieces += [_as_lanes(dfg_acc[0]), loss_acc]
    sizes = [pc.shape[0] for pc in pieces]
    pack = jnp.concatenate(pieces, axis=0)
    pack = _pad_rows(pack, 8 * N_DEV)
    red = _all_reduce_small(pack, "small_grads_all_reduce")
    offs = [0]
    for s in sizes:
        offs.append(offs[-1] + s)

    def unpack(i, shape):
        n = math.prod(shape)
        return red[offs[i]:offs[i + 1]].reshape(-1)[:n].reshape(shape)

    g_norm_g = unpack(0, (L, D))
    g_ln_g = unpack(1, (L, SEG))
    g_ln_b = unpack(2, (L, SEG))
    g_w_s = unpack(3, (L, GROUPS, CHUNK, CHUNK))
    g_b_s = jnp.transpose(unpack(4, (L, CHUNK, LANES))[:, :, :GROUPS], (0, 2, 1))
    g_conv_w_full = unpack(5, (L, CONV_TAPS, SEG))
    g_conv_b = unpack(6, (L, SEG))
    g_w_pool = unpack(7, (L, len(POOL_WINDOWS), POOL_GROUP, POOL_GROUP))
    g_pool_scale = unpack(8, (L, SEG))
    g_final_g = unpack(9, (D,))
    loss = red[offs[10], 0]
    dev = 4 * xi + 2 * yi + ci
    g_conv_w = lax.dynamic_slice_in_dim(g_conv_w_full, dev * conv_w.shape[2], conv_w.shape[2], axis=2)

    g_w_in = jnp.stack([big_grads[l][0] for l in range(L)])
    g_w_pa = jnp.stack([big_grads[l][1] for l in range(L)])
    g_w_pb = jnp.stack([big_grads[l][2] for l in range(L)])
    g_w_pc = jnp.stack([big_grads[l][3] for l in range(L)])
    g_w_o = jnp.stack([big_grads[l][4] for l in range(L)])

    grads = dict(norm_g=g_norm_g, w_in=g_w_in, ln_g=g_ln_g, ln_b=g_ln_b, w_s=g_w_s, b_s=g_b_s, conv_w=g_conv_w,
                 conv_b=g_conv_b, w_pool=g_w_pool, pool_scale=g_pool_scale, w_pa=g_w_pa, w_pb=g_w_pb, w_pc=g_w_pc,
                 w_o=g_w_o, final_g=g_final_g)
    weights = dict(norm_g=norm_g, w_in=w_in, ln_g=ln_g, ln_b=ln_b, w_s=w_s, b_s=b_s, conv_w=conv_w, conv_b=conv_b,
                   w_pool=w_pool, pool_scale=pool_scale, w_pa=w_pa, w_pb=w_pb, w_pc=w_pc, w_o=w_o, final_g=final_g)
    ms = dict(norm_g=m_norm_g, w_in=m_w_in, ln_g=m_ln_g, ln_b=m_ln_b, w_s=m_w_s, b_s=m_b_s, conv_w=m_conv_w,
              conv_b=m_conv_b, w_pool=m_w_pool, pool_scale=m_pool_scale, w_pa=m_w_pa, w_pb=m_w_pb, w_pc=m_w_pc,
              w_o=m_w_o, final_g=m_final_g)
    vs = dict(norm_g=v_norm_g, w_in=v_w_in, ln_g=v_ln_g, ln_b=v_ln_b, w_s=v_w_s, b_s=v_b_s, conv_w=v_conv_w,
              conv_b=v_conv_b, w_pool=v_w_pool, pool_scale=v_pool_scale, w_pa=v_w_pa, w_pb=v_w_pb, w_pc=v_w_pc,
              w_o=v_w_o, final_g=v_final_g)
    order = ["norm_g", "w_in", "ln_g", "ln_b", "w_s", "b_s", "conv_w", "conv_b", "w_pool", "pool_scale", "w_pa", "w_pb",
             "w_pc", "w_o", "final_g"]

    delta, new_m, new_v = {}, {}, {}
    big = ["w_in", "w_pa", "w_pb", "w_pc", "w_o"]
    for nm in big:
        shp = weights[nm].shape
        two = lambda a: a.reshape(-1, shp[-1])
        d, mn, vn = _adamw(two(weights[nm]), two(grads[nm]), two(ms[nm]), two(vs[nm]), f"adamw_{nm}")
        delta[nm], new_m[nm], new_v[nm] = d.reshape(shp), mn.reshape(shp), vn.reshape(shp)
    rest = [nm for nm in order if nm not in big]
    cat = lambda src: jnp.concatenate([_as_lanes(src[nm]) for nm in rest], axis=0)
    d, mn, vn = _adamw(cat(weights), cat(grads), cat(ms), cat(vs), "adamw_small")
    off = 0
    for nm in rest:
        shp = weights[nm].shape
        n = math.prod(shp)
        rows = _as_lanes(weights[nm]).shape[0]
        cut = lambda a: a[off:off + rows].reshape(-1)[:n].reshape(shp)
        delta[nm], new_m[nm], new_v[nm] = cut(d), cut(mn), cut(vn)
        off += rows

    return (loss, grad_x, *[grads[nm] for nm in order], *[delta[nm] for nm in order],
            *[new_m[nm] for nm in order], *[new_v[nm] for nm in order])
```

```python
import functools
import math

import jax
import jax.numpy as jnp
from jax import lax
from jax.experimental import pallas as pl
from jax.experimental.pallas import tpu as pltpu

F32 = jnp.float32
BF16 = jnp.bfloat16
SDS = jax.ShapeDtypeStruct
MESH = pl.DeviceIdType.MESH

SEG = 512
CHUNK = 128
GROUPS = 8
HEAD = SEG // GROUPS
POOL_WINDOWS = (2, 4, 8, 16)
POOL_GROUP = SEG // len(POOL_WINDOWS)
CONV_TAPS = 3
HALO = 16
RMS_EPS = 1e-6
LN_EPS = 1e-5
ADAM_LR, ADAM_B1, ADAM_B2, ADAM_EPS, ADAM_WD, ADAM_STEP = 0.001, 0.9, 0.999, 1e-08, 0.01, 10

O_U, O_V, O_ZA, O_XB, O_BG, O_CG, O_ZB, O_XC, O_ZC, O_G = (SEG * i for i in range(10))

N_DEV = 8
N_CHIP = 4
LANES = 128
VMEM_LIMIT = 48 * 1024 * 1024
ADAMW_WHOLE_BYTES = 2 * 1024 * 1024
INPROJ_ROWS = 1024


def _cparams(sem=None, **kw):
    return pltpu.CompilerParams(dimension_semantics=sem, vmem_limit_bytes=VMEM_LIMIT, **kw)


def _pick(total, target, mult):
    best = None
    for d in range(mult, min(total, target) + 1, mult):
        if total % d == 0:
            best = d
    assert best is not None, (total, target, mult)
    return best


def _dot(a, b):
    return jnp.dot(a, b, preferred_element_type=F32)


def _dot_nt(a, b):
    return lax.dot_general(a, b, (((1,), (1,)), ((), ())), preferred_element_type=F32)


def _dot_tn(a, b):
    return lax.dot_general(a, b, (((0,), (0,)), ((), ())), preferred_element_type=F32)


def _zero(ref):
    ref[...] = jnp.zeros(ref.shape, ref.dtype)


def _sigmoid(x):
    return 1.0 / (1.0 + jnp.exp(-x))


_GELU_C = math.sqrt(2.0 / math.pi)


def _gelu(x):
    t = jnp.tanh(_GELU_C * (x + 0.044715 * x * x * x))
    return 0.5 * x * (1.0 + t), t


def _gelu_grad(x, t):
    return 0.5 * (1.0 + t) + 0.5 * x * (1.0 - t * t) * _GELU_C * (1.0 + 3.0 * 0.044715 * x * x)


def _me():
    return lax.axis_index("x"), lax.axis_index("y"), lax.axis_index("c")


def _inproj(x, norm_g, win_t, name, riders=()):
    T, D = x.shape
    N = win_t.shape[0]
    bt = _pick(T, INPROJ_ROWS, 16)
    bn = _pick(N, 1536, LANES)
    grid = (T // bt, N // bn)

    def compute(x_ref, g_ref, w_ref, p_ref, h_ref, hs_ref):
        @pl.when(pl.program_id(1) == 0)
        def _():
            xv = x_ref[...]
            rstd = lax.rsqrt(jnp.mean(xv * xv, axis=-1, keepdims=True) + RMS_EPS)
            hb = (xv * rstd * g_ref[...]).astype(BF16)
            hs_ref[...] = hb
            h_ref[...] = hb

        p_ref[...] = _dot_nt(hs_ref[...], w_ref[...]).astype(BF16)

    return _host_call(
        compute, name, grid, riders,
        inputs=[x, norm_g.reshape(1, D), win_t],
        in_specs=[pl.BlockSpec((bt, D), lambda i, j: (i, 0)),
                  pl.BlockSpec((1, D), lambda i, j: (0, 0)),
                  pl.BlockSpec((bn, D), lambda i, j: (j, 0))],
        out_specs=[pl.BlockSpec((bt, bn), lambda i, j: (i, j)),
                   pl.BlockSpec((bt, D), lambda i, j: (i, 0))],
        out_shape=[SDS((T, N), BF16), SDS((T, D), BF16)],
        scratch_shapes=[pltpu.VMEM((bt, D), BF16)])


def _inproj_gathering(x, norm_g, w_loc, chip_order, name, riders=()):
    T, D = x.shape
    n = w_loc.shape[0]
    N = n * N_DEV
    cw = 2 * n
    bt = _pick(T, INPROJ_ROWS, 16)
    nt = T // bt
    r_in, r_out, r_sems, copies = _rider_plan(riders)
    n_rin, n_rout = len(r_in), len(r_out)

    def body(q_ref, x_ref, g_ref, wloc_ref, *rest):
        rins = rest[:n_rin]
        p_ref, h_ref, wfull_ref = rest[n_rin:n_rin + 3]
        routs = rest[n_rin + 3:n_rin + 3 + n_rout]
        hs_ref, wbuf, send_sems, recv_sems, loc_sems = rest[n_rin + 3 + n_rout:n_rin + 8 + n_rout]
        rsems = rest[n_rin + 8 + n_rout:]
        j, i = pl.program_id(0), pl.program_id(1)
        cx, cy, cc = _me()
        sibling = (cx, cy, 1 - cc)

        def rows(k):
            return wfull_ref.at[pl.ds(pl.multiple_of(k * n, 8), n)]

        def shard_copy(slot, src, k, to):
            return pltpu.make_async_remote_copy(src_ref=src, dst_ref=rows(k), send_sem=send_sems.at[slot],
                                                recv_sem=recv_sems.at[slot], device_id=to, device_id_type=MESH)

        me = 4 * cx + 2 * cy + cc
        place_mine = pltpu.make_async_copy(wloc_ref, rows(me), loc_sems.at[0])
        sends = [shard_copy(0, wloc_ref, me, sibling)]
        for jj in (1, 2, 3):
            sends.append(shard_copy(jj, wloc_ref, me, (*_chip_peer(cx, cy, jj), cc)))

        def forward(jj):
            px, py = _chip_peer(cx, cy, jj)
            k = 4 * px + 2 * py + cc
            return shard_copy(3 + jj, rows(k), k, sibling)

        def load_chunk(q):
            cp = pltpu.make_async_copy(wfull_ref.at[pl.ds(pl.multiple_of(q * cw, 8), cw)], wbuf, loc_sems.at[1])
            cp.start()
            cp.wait()

        keep_h = pltpu.make_async_copy(hs_ref, h_ref, loc_sems.at[2])

        @pl.when((j == 0) & (i == 0))
        def _():
            place_mine.start()
            for cp in sends:
                cp.start()
            for cp in copies(rins, routs, rsems):
                cp.start()
            place_mine.wait()
            sends[0].wait_recv()
            load_chunk(q_ref[0])

        for jj in (1, 2, 3):
            @pl.when((j == jj) & (i == 0))
            def _(jj=jj):
                sends[jj].wait_recv()
                fwd = forward(jj)
                fwd.start()
                fwd.wait_recv()
                load_chunk(q_ref[jj])

        tok = pl.ds(pl.multiple_of(i * bt, bt), bt)

        @pl.when(j == 0)
        def _():
            xv = x_ref[...]
            rstd = lax.rsqrt(jnp.mean(xv * xv, axis=-1, keepdims=True) + RMS_EPS)
            hs_ref[tok, :] = (xv * rstd * g_ref[...]).astype(BF16)

        @pl.when((j == 0) & (i == nt - 1))
        def _():
            keep_h.start()

        p_ref[...] = _dot_nt(hs_ref[tok, :], wbuf[...]).astype(BF16)

        @pl.when((j == N_CHIP - 1) & (i == nt - 1))
        def _():
            keep_h.wait()
            for cp in sends:
                cp.wait_send()
            for jj in (1, 2, 3):
                forward(jj).wait_send()
            for cp in copies(rins, routs, rsems):
                cp.wait()

    res = pl.pallas_call(
        body, name=name,
        grid_spec=pltpu.PrefetchScalarGridSpec(
            num_scalar_prefetch=1, grid=(N_CHIP, nt),
            in_specs=[pl.BlockSpec((bt, D), lambda j, i, q: (jnp.where(j == 0, i, nt - 1), 0)),
                      pl.BlockSpec((1, D), lambda j, i, q: (0, 0)), _ANY] + [_ANY] * n_rin,
            out_specs=[pl.BlockSpec((bt, cw), lambda j, i, q: (i, q[j])), _ANY, _ANY] + [_ANY] * n_rout,
            scratch_shapes=[pltpu.VMEM((T, D), BF16), pltpu.VMEM((cw, D), BF16), pltpu.SemaphoreType.DMA((7,)),
                            pltpu.SemaphoreType.DMA((7,)), pltpu.SemaphoreType.DMA((3,))] + r_sems),
        out_shape=[SDS((T, N), BF16), SDS((T, D), BF16), SDS((N, D), BF16)] + r_out,
        compiler_params=_cparams(("arbitrary", "arbitrary")),
    )(chip_order, x, norm_g.reshape(1, D), w_loc, *r_in)
    return res[:3], _split_riders(riders, res[3:])


C_LNG, C_LNB, C_CW0, C_CW1, C_CW2, C_CB, C_PS = range(7)
C_ROWS = 8


def _mixers(p_ref, hxb_ref, hcg_ref, hxc_ref, cv, bsb_ref, wcat_ref, wpool_ref, extb, extc,
            first, blk, R, need_grad):
    def seg(lo):
        return p_ref[:, lo:lo + SEG].astype(F32)

    u, v, za = seg(O_U), seg(O_V), seg(O_ZA)
    xb, bg, cg, zb = seg(O_XB), seg(O_BG), seg(O_CG), seg(O_ZB)
    xc, zc = seg(O_XC), seg(O_ZC)
    out = {}

    ug, tu = _gelu(u)
    vg, tv = _gelu(v)
    mu = jnp.mean(vg, axis=-1, keepdims=True)
    vcen = vg - mu
    rs = lax.rsqrt(jnp.mean(vcen * vcen, axis=-1, keepdims=True) + LN_EPS)
    vhat = vcen * rs
    vn = (vhat * cv[C_LNG:C_LNG + 1, :] + cv[C_LNB:C_LNB + 1, :]).astype(BF16)
    lane_group = lax.broadcasted_iota(jnp.int32, (CHUNK, SEG), 1) // HEAD
    zero_b = jnp.zeros((CHUNK, SEG), BF16)
    sgs = []
    for ci in range(R // CHUNK):
        vc = vn[ci * CHUNK:(ci + 1) * CHUNK]
        vst = jnp.concatenate([jnp.where(lane_group == g, vc, zero_b) for g in range(GROUPS)], axis=0)
        sgs.append(_dot(wcat_ref[...], vst) + bsb_ref[...])
    sg = sgs[0] if len(sgs) == 1 else jnp.concatenate(sgs, axis=0)
    a_out = ug * sg
    sa = _sigmoid(za)
    out["a"] = a_out * (za * sa)

    cx = cg * xb
    halo_b = hcg_ref[...].astype(F32) * hxb_ref[...].astype(F32)
    extb[0:HALO, :] = jnp.where(first, 0.0, halo_b)
    extb[HALO:HALO + R, :] = cx
    cx1 = extb[pl.ds(HALO - 1, R), :]
    cx2 = extb[pl.ds(HALO - 2, R), :]
    yconv = (cv[C_CW0:C_CW0 + 1, :] * cx2 + cv[C_CW1:C_CW1 + 1, :] * cx1
             + cv[C_CW2:C_CW2 + 1, :] * cx + cv[C_CB:C_CB + 1, :])
    b_out = bg * yconv
    sb = _sigmoid(zb)
    out["b"] = b_out * (zb * sb)

    extc[0:HALO, :] = jnp.where(first, 0.0, hxc_ref[...].astype(F32))
    extc[HALO:HALO + R, :] = xc
    tpos = blk * R + lax.broadcasted_iota(jnp.int32, (R, POOL_GROUP), 0) + 1
    pooled, invs, pws = [], [], []
    for gi, w in enumerate(POOL_WINDOWS):
        lo = gi * POOL_GROUP
        win = xc[:, lo:lo + POOL_GROUP]
        for j in range(1, w):
            win = win + extc[pl.ds(HALO - j, R), lo:lo + POOL_GROUP]
        inv = 1.0 / jnp.minimum(tpos, w).astype(F32)
        pg = (win * inv - xc[:, lo:lo + POOL_GROUP]).astype(BF16)
        pooled.append(pg)
        invs.append(inv)
        pws.append(_dot(pg, wpool_ref[gi]))
    pw = jnp.concatenate(pws, axis=1)
    c_out = pw * cv[C_PS:C_PS + 1, :]
    sc = _sigmoid(zc)
    out["c"] = c_out * (zc * sc)

    if need_grad:
        out.update(u=u, v=v, tu=tu, tv=tv, ug=ug, sg=sg, a_out=a_out, za=za, sa=sa,
                   rs=rs, vhat=vhat, vn=vn, lane_group=lane_group, zero_b=zero_b,
                   xb=xb, bg=bg, cg=cg, cx=cx, cx1=cx1, cx2=cx2, yconv=yconv, b_out=b_out, zb=zb, sb=sb,
                   pooled=pooled, invs=invs, pw=pw, c_out=c_out, zc=zc, sc=sc)
    return out


def _halo_specs(R, nb, rev):
    step = R // HALO

    def mk(col):
        def imap(i):
            b = (nb - 1 - i) if rev else i
            return (jnp.maximum(b * step - 1, 0), col)
        return pl.BlockSpec((HALO, SEG), imap)

    return [mk(O_XB // SEG), mk(O_CG // SEG), mk(O_XC // SEG)]


def _const_spec(shape):
    nd = len(shape)
    return pl.BlockSpec(shape, lambda i: (0,) * nd, pipeline_mode=pl.Buffered(1))


MIX_FWD_ROWS = 512
MIX_BWD_ROWS = 256


def _mix_block_rows(T, target):
    return _pick(T, target, CHUNK)


def _mix_fwd(p, x, lw, name, riders=()):
    T, D = x.shape
    N = p.shape[1]
    R = _mix_block_rows(T, MIX_FWD_ROWS)
    nb = T // R

    def body(p_ref, hxb, hcg, hxc, x_ref, cv_ref, bsb_ref, wcat_ref, wpool_ref, wpa_ref, wpb_ref, wpc_ref,
             wo_ref, xo_ref, ya_ref, yb_ref, yc_ref, extb, extc):
        i = pl.program_id(0)
        cv = cv_ref[...]
        r = _mixers(p_ref, hxb, hcg, hxc, cv, bsb_ref, wcat_ref, wpool_ref, extb, extc,
                    i == 0, i, R, False)
        merged = None
        for k, (act, w_ref, y_ref) in enumerate(((r["a"], wpa_ref, ya_ref), (r["b"], wpb_ref, yb_ref),
                                                 (r["c"], wpc_ref, yc_ref))):
            y = _dot(act.astype(BF16), w_ref[...]).astype(BF16)
            y_ref[...] = y
            term = _sigmoid(p_ref[:, O_G + k * D:O_G + (k + 1) * D]) * y
            merged = term if merged is None else merged + term
        xo_ref[...] = x_ref[...] + _dot(merged, wo_ref[...])

    row = lambda w: pl.BlockSpec((R, w), lambda i: (i, 0))
    consts = [lw["cvec"], lw["bsb"], lw["wcat"], lw["wpool"], lw["wpa"], lw["wpb"], lw["wpc"], lw["wo"]]
    return _host_call(
        body, name, (nb,), riders,
        inputs=[p, p, p, p, x, *consts],
        in_specs=[row(N)] + _halo_specs(R, nb, False) + [row(D)] + [_const_spec(c.shape) for c in consts],
        out_specs=[row(D), row(D), row(D), row(D)],
        out_shape=[SDS((T, D), F32), SDS((T, D), BF16), SDS((T, D), BF16), SDS((T, D), BF16)],
        scratch_shapes=[pltpu.VMEM((HALO + R, SEG), F32), pltpu.VMEM((HALO + R, SEG), F32)])


def _loss_head(x, final_g, target, name):
    T, D = x.shape
    bt = _pick(T, 512, 8)

    def body(x_ref, g_ref, t_ref, dx_ref, loss_ref, dg_ref):
        @pl.when(pl.program_id(0) == 0)
        def _():
            _zero(loss_ref)
            _zero(dg_ref)

        xv = x_ref[...]
        g = g_ref[...]
        rstd = lax.rsqrt(jnp.mean(xv * xv, axis=-1, keepdims=True) + RMS_EPS)
        xhat = xv * rstd
        err = xhat * g - t_ref[...]
        part = 0.5 * jnp.sum(jnp.sum(err * err, axis=-1, keepdims=True), axis=0, keepdims=True) / D
        loss_ref[...] += jnp.broadcast_to(part, loss_ref.shape)
        dy = err * (1.0 / D)
        dg_ref[0:1, :] += jnp.sum(dy * xhat, axis=0, keepdims=True)
        dxn = dy * g
        dx_ref[...] = rstd * (dxn - xhat * jnp.mean(dxn * xhat, axis=-1, keepdims=True))

    return pl.pallas_call(
        body, name=name, grid=(T // bt,),
        in_specs=[pl.BlockSpec((bt, D), lambda i: (i, 0)), _const_spec((1, D)), pl.BlockSpec((bt, D), lambda i: (i, 0))],
        out_specs=[pl.BlockSpec((bt, D), lambda i: (i, 0)), _const_spec((8, LANES)), _const_spec((8, D))],
        out_shape=[SDS((T, D), F32), SDS((8, LANES), F32), SDS((8, D), F32)],
        compiler_params=_cparams(("arbitrary",)),
    )(x, final_g.reshape(1, D), target)


V_LNG, V_LNB, V_CB, V_PS, V_CW0, V_CW1, V_CW2 = range(7)


def _mix_bwd(p, dxo, ya, yb, yc, lw, name):
    T, D = dxo.shape
    N = p.shape[1]
    R = _mix_block_rows(T, MIX_BWD_ROWS)
    nb = T // R

    def body(p_ref, hxb, hcg, hxc, dxo_ref, ya_ref, yb_ref, yc_ref, cv_ref, bsb_ref, wcat_ref, wcatt_ref,
             wpool_ref, wpa_ref, wpb_ref, wpc_ref, wo_ref,
             dp_ref, acts_ref, mrg_ref, dys_ref, gwc_ref, gbs_ref, gwpool_ref, gvec_ref,
             extb, extc, extdy, extq, cdy, cq, bsacc):
        i = pl.program_id(0)
        blk = nb - 1 - i

        @pl.when(i == 0)
        def _():
            for ref in (gwc_ref, gwpool_ref, gvec_ref, cdy, cq, bsacc):
                _zero(ref)

        cv = cv_ref[...]
        r = _mixers(p_ref, hxb, hcg, hxc, cv, bsb_ref, wcat_ref, wpool_ref, extb, extc,
                    blk == 0, blk, R, True)

        dxo_b = dxo_ref[...].astype(BF16)
        dm = _dot_nt(dxo_b, wo_ref[...]).astype(BF16)
        ys = [ya_ref[...], yb_ref[...], yc_ref[...]]
        sig = [_sigmoid(p_ref[:, O_G + k * D:O_G + (k + 1) * D]) for k in range(3)]
        mrg_ref[...] = sig[0] * ys[0] + sig[1] * ys[1] + sig[2] * ys[2]
        dacts = []
        for k, (act, w_ref) in enumerate(((r["a"], wpa_ref), (r["b"], wpb_ref), (r["c"], wpc_ref))):
            dyk = dm * sig[k]
            dp_ref[:, O_G + k * D:O_G + (k + 1) * D] = dyk * ys[k] * (1.0 - sig[k])
            acts_ref[:, k * SEG:(k + 1) * SEG] = act.astype(BF16)
            dys_ref[:, k * D:(k + 1) * D] = dyk
            dacts.append(_dot_nt(dyk, w_ref[...]))
        da, db, dc = dacts

        def silu_bwd(dact, pre, z, s):
            return dact * (z * s), dact * pre * (s * (1.0 + z * (1.0 - s)))

        d_aout, dza = silu_bwd(da, r["a_out"], r["za"], r["sa"])
        dp_ref[:, O_ZA:O_ZA + SEG] = dza.astype(BF16)
        dp_ref[:, O_U:O_U + SEG] = (d_aout * r["sg"] * _gelu_grad(r["u"], r["tu"])).astype(BF16)
        d_sg = d_aout * r["ug"]
        dvns = []
        for ci in range(R // CHUNK):
            dsc = d_sg[ci * CHUNK:(ci + 1) * CHUNK]
            bsacc[...] += dsc
            dsc_b = dsc.astype(BF16)
            dst = jnp.concatenate([jnp.where(r["lane_group"] == g, dsc_b, r["zero_b"]) for g in range(GROUPS)], axis=0)
            dvns.append(_dot(wcatt_ref[...], dst))
            gwc_ref[...] += _dot_nt(dst, r["vn"][ci * CHUNK:(ci + 1) * CHUNK])
        d_vn = dvns[0] if len(dvns) == 1 else jnp.concatenate(dvns, axis=0)
        vhat = r["vhat"]
        gvec_ref[V_LNG:V_LNG + 1, :] += jnp.sum(d_vn * vhat, axis=0, keepdims=True)
        gvec_ref[V_LNB:V_LNB + 1, :] += jnp.sum(d_vn, axis=0, keepdims=True)
        d_vhat = d_vn * cv[C_LNG:C_LNG + 1, :]
        d_vg = r["rs"] * (d_vhat - jnp.mean(d_vhat, axis=-1, keepdims=True)
                          - vhat * jnp.mean(d_vhat * vhat, axis=-1, keepdims=True))
        dp_ref[:, O_V:O_V + SEG] = (d_vg * _gelu_grad(r["v"], r["tv"])).astype(BF16)

        d_bout, dzb = silu_bwd(db, r["b_out"], r["zb"], r["sb"])
        dp_ref[:, O_ZB:O_ZB + SEG] = dzb.astype(BF16)
        dp_ref[:, O_BG:O_BG + SEG] = (d_bout * r["yconv"]).astype(BF16)
        d_y = d_bout * r["bg"]
        gvec_ref[V_CB:V_CB + 1, :] += jnp.sum(d_y, axis=0, keepdims=True)
        gvec_ref[V_CW0:V_CW0 + 1, :] += jnp.sum(d_y * r["cx2"], axis=0, keepdims=True)
        gvec_ref[V_CW1:V_CW1 + 1, :] += jnp.sum(d_y * r["cx1"], axis=0, keepdims=True)
        gvec_ref[V_CW2:V_CW2 + 1, :] += jnp.sum(d_y * r["cx"], axis=0, keepdims=True)
        extdy[0:R, :] = d_y
        extdy[R:R + HALO, :] = cdy[...]
        d_cx = (cv[C_CW2:C_CW2 + 1, :] * d_y + cv[C_CW1:C_CW1 + 1, :] * extdy[pl.ds(1, R), :]
                + cv[C_CW0:C_CW0 + 1, :] * extdy[pl.ds(2, R), :])
        cdy[...] = d_y[0:HALO]
        dp_ref[:, O_CG:O_CG + SEG] = (d_cx * r["xb"]).astype(BF16)
        dp_ref[:, O_XB:O_XB + SEG] = (d_cx * r["cg"]).astype(BF16)

        d_cout, dzc = silu_bwd(dc, r["c_out"], r["zc"], r["sc"])
        dp_ref[:, O_ZC:O_ZC + SEG] = dzc.astype(BF16)
        gvec_ref[V_PS:V_PS + 1, :] += jnp.sum(d_cout * r["pw"], axis=0, keepdims=True)
        d_pw = (d_cout * cv[C_PS:C_PS + 1, :]).astype(BF16)
        dpool = []
        for gi, w in enumerate(POOL_WINDOWS):
            lo = gi * POOL_GROUP
            dpw_g = d_pw[:, lo:lo + POOL_GROUP]
            gwpool_ref[lo:lo + POOL_GROUP, :] += _dot_tn(r["pooled"][gi], dpw_g)
            dpg = _dot_nt(dpw_g, wpool_ref[gi])
            dpool.append(dpg)
            extq[0:R, lo:lo + POOL_GROUP] = dpg * r["invs"][gi]
        extq[R:R + HALO, :] = cq[...]
        for gi, w in enumerate(POOL_WINDOWS):
            lo = gi * POOL_GROUP
            acc = extq[0:R, lo:lo + POOL_GROUP]
            for j in range(1, w):
                acc = acc + extq[pl.ds(j, R), lo:lo + POOL_GROUP]
            dp_ref[:, O_XC + lo:O_XC + lo + POOL_GROUP] = (acc - dpool[gi]).astype(BF16)
        cq[...] = extq[0:HALO, :]

        @pl.when(i == nb - 1)
        def _():
            rr = lax.broadcasted_iota(jnp.int32, gwc_ref.shape, 0) % CHUNK
            cc = lax.broadcasted_iota(jnp.int32, gwc_ref.shape, 1)
            gwc_ref[...] = jnp.where(cc <= rr, gwc_ref[...], 0.0)
            acc = bsacc[...]
            hi = acc.astype(BF16)
            lo_ = (acc - hi.astype(F32)).astype(BF16)
            sel = (lax.broadcasted_iota(jnp.int32, (SEG, LANES), 0) // HEAD
                   == lax.broadcasted_iota(jnp.int32, (SEG, LANES), 1)).astype(BF16)
            gbs_ref[...] = _dot(hi, sel) + _dot(lo_, sel)

    row = lambda w: pl.BlockSpec((R, w), lambda i: (nb - 1 - i, 0))
    consts = [lw["cvec"], lw["bsb"], lw["wcat"], lw["wcatt"], lw["wpool"], lw["wpa"], lw["wpb"], lw["wpc"], lw["wo"]]
    acc_shapes = [(GROUPS * CHUNK, CHUNK), (CHUNK, LANES), (SEG, POOL_GROUP), (8, SEG)]
    row_widths = [N, 3 * SEG, D, 3 * D]
    return pl.pallas_call(
        body, name=name, grid=(nb,),
        in_specs=([row(N)] + _halo_specs(R, nb, True) + [row(D), row(D), row(D), row(D)]
                  + [_const_spec(c.shape) for c in consts]),
        out_specs=[row(w) for w in row_widths] + [_const_spec(s) for s in acc_shapes],
        out_shape=[SDS((T, w), BF16) for w in row_widths] + [SDS(s, F32) for s in acc_shapes],
        scratch_shapes=[pltpu.VMEM((HALO + R, SEG), F32)] * 4
        + [pltpu.VMEM((HALO, SEG), F32), pltpu.VMEM((HALO, SEG), F32), pltpu.VMEM((CHUNK, SEG), F32)],
        compiler_params=_cparams(("arbitrary",)),
    )(p, p, p, p, dxo, ya, yb, yc, *consts)


def _proj_wgrad(acts, merged, dys, dxo, name):
    T, D = dxo.shape
    bk = _pick(T, 512, 16)

    def body(a_ref, m_ref, dy_ref, dxo_ref, gwpa_ref, gwpb_ref, gwpc_ref, gwo_ref):
        @pl.when(pl.program_id(0) == 0)
        def _():
            for ref in (gwpa_ref, gwpb_ref, gwpc_ref, gwo_ref):
                _zero(ref)

        gwo_ref[...] += _dot_tn(m_ref[...], dxo_ref[...].astype(BF16))
        for k, ref in enumerate((gwpa_ref, gwpb_ref, gwpc_ref)):
            ref[...] += _dot_tn(a_ref[:, k * SEG:(k + 1) * SEG], dy_ref[:, k * D:(k + 1) * D])

    row = lambda w: pl.BlockSpec((bk, w), lambda i: (i, 0))
    shapes = [(SEG, D), (SEG, D), (SEG, D), (D, D)]
    return pl.pallas_call(
        body, name=name, grid=(T // bk,),
        in_specs=[row(3 * SEG), row(D), row(3 * D), row(D)],
        out_specs=[_const_spec(s) for s in shapes], out_shape=[SDS(s, F32) for s in shapes],
        compiler_params=_cparams(("arbitrary",)),
    )(acts, merged, dys, dxo)


def _inproj_token_blocks(T):
    return T // _pick(T, INPROJ_ROWS, 16)


def _inproj_bwd_x(dp, win_t, x, norm_g, dxo, name, riders=(), blocks=None, fill=None):
    T, D = x.shape
    N = dp.shape[1]
    bt = _pick(T, INPROJ_ROWS, 16)
    bk = _pick(N, 1536, LANES)
    nk = N // bk
    b0, nblk = blocks if blocks else (0, T // bt)

    def compute(dp_ref, w_ref, x_ref, g_ref, dxo_ref, *rest):
        dx_ref, dg_ref, acc_ref = rest[-3:]
        i, k = pl.program_id(0), pl.program_id(1)

        @pl.when((i == 0) & (k == 0))
        def _():
            _zero(dg_ref)

        @pl.when(k == 0)
        def _():
            _zero(acc_ref)

        acc_ref[...] += _dot(dp_ref[...], w_ref[...])

        @pl.when(k == nk - 1)
        def _():
            dh = acc_ref[...]
            xv = x_ref[...]
            rstd = lax.rsqrt(jnp.mean(xv * xv, axis=-1, keepdims=True) + RMS_EPS)
            xhat = xv * rstd
            dg_ref[0:1, :] += jnp.sum(dh * xhat, axis=0, keepdims=True)
            dxn = dh * g_ref[...]
            dx_ref[...] = dxo_ref[...] + rstd * (dxn - xhat * jnp.mean(dxn * xhat, axis=-1, keepdims=True))

    rows = pl.BlockSpec((bt, D), lambda i, k: (i + b0, 0))
    return _host_call(
        compute, name, (nblk, nk), riders,
        inputs=[dp, win_t, x, norm_g.reshape(1, D), dxo] + ([] if fill is None else [fill]),
        in_specs=[pl.BlockSpec((bt, bk), lambda i, k: (i + b0, k)),
                  pl.BlockSpec((bk, D), lambda i, k: (k, 0)),
                  rows,
                  pl.BlockSpec((1, D), lambda i, k: (0, 0)),
                  rows] + ([] if fill is None else [_ANY]),
        out_specs=[rows, pl.BlockSpec((8, D), lambda i, k: (0, 0))],
        out_shape=[SDS((T, D), F32), SDS((8, D), F32)],
        scratch_shapes=[pltpu.VMEM((bt, D), F32)],
        aliases={} if fill is None else {5: 0})


def _inproj_bwd_w(dp, h, name, riders=()):
    T, N = dp.shape
    D = h.shape[1]
    bn = _pick(N, 1920, LANES)
    bk = _pick(T, 1024, 16)
    nk = T // bk

    def compute(dp_ref, h_ref, o_ref):
        @pl.when(pl.program_id(1) == 0)
        def _():
            _zero(o_ref)

        o_ref[...] += _dot_tn(dp_ref[...], h_ref[...])

    return _host_call(
        compute, name, (N // bn, nk), riders,
        inputs=[dp, h],
        in_specs=[pl.BlockSpec((bk, bn), lambda j, k: (k, j)), pl.BlockSpec((bk, D), lambda j, k: (k, 0))],
        out_specs=[pl.BlockSpec((bn, D), lambda j, k: (j, 0))],
        out_shape=[SDS((N, D), F32)],
        scratch_shapes=[])


def _chip_peer(x, y, j):
    px = (1 - x) if (j >> 1) else x
    py = (1 - y) if (j & 1) else y
    return px, py


def _blk(ref, kind, k, n):
    if kind == "rows":
        return ref.at[pl.ds(pl.multiple_of(k * n, 8), n)]
    return ref.at[:, pl.ds(pl.multiple_of(k * n, LANES), n)]


class _Exchange:
    def __init__(self, srcs, out_shapes, n_sems, build, alias=False):
        self.srcs, self.out_shapes, self.n_sems, self.build, self.alias = list(srcs), list(out_shapes), n_sems, build, alias


def _rider_plan(riders):
    inputs = [s for e in riders for s in e.srcs]
    out_shapes = [o for e in riders for o in e.out_shapes]
    sems = [pltpu.SemaphoreType.DMA((e.n_sems,)) for e in riders for _ in range(2)]

    def copies(in_refs, out_refs, sem_refs):
        cps, i, o = [], 0, 0
        for k, e in enumerate(riders):
            ni, no = len(e.srcs), len(e.out_shapes)
            cps += e.build(in_refs[i:i + ni], out_refs[o:o + no], sem_refs[2 * k], sem_refs[2 * k + 1])
            i, o = i + ni, o + no
        return cps

    return inputs, out_shapes, sems, copies


_ANY = pl.BlockSpec(memory_space=pl.ANY)


def _host_call(compute, name, grid, riders, inputs, in_specs, out_specs, out_shape, scratch_shapes, aliases=None):
    r_in, r_out, r_sems, copies = _rider_plan(riders)
    ni, no, ns = len(inputs), len(out_shape), len(scratch_shapes)

    def body(*refs):
        ins, rins = refs[:ni], refs[ni:ni + len(r_in)]
        outs = refs[ni + len(r_in):ni + len(r_in) + no]
        routs = refs[ni + len(r_in) + no:ni + len(r_in) + no + len(r_out)]
        scr = refs[ni + len(r_in) + no + len(r_out):]
        first = functools.reduce(lambda a, b: a & b, [pl.program_id(d) == 0 for d in range(len(grid))])
        last = functools.reduce(lambda a, b: a & b, [pl.program_id(d) == grid[d] - 1 for d in range(len(grid))])
        if riders:
            @pl.when(first)
            def _():
                for cp in copies(rins, routs, scr[ns:]):
                    cp.start()

        compute(*ins, *outs, *scr[:ns])

        if riders:
            @pl.when(last)
            def _():
                for cp in copies(rins, routs, scr[ns:]):
                    cp.wait()

    res = pl.pallas_call(
        body, name=name, grid=grid,
        in_specs=list(in_specs) + [_ANY] * len(r_in),
        out_specs=list(out_specs) + [_ANY] * len(r_out),
        out_shape=list(out_shape) + r_out,
        scratch_shapes=list(scratch_shapes) + r_sems,
        input_output_aliases=aliases or {},
        compiler_params=_cparams(("arbitrary",) * len(grid)),
    )(*inputs, *r_in)
    return res[:no], _split_riders(riders, res[no:])


def _split_riders(riders, flat):
    out, o = [], 0
    for e in riders:
        out.append(list(flat[o:o + len(e.out_shapes)]))
        o += len(e.out_shapes)
    return out


def _run_exchange(ex, name):
    n_in, n_out = len(ex.srcs), len(ex.out_shapes)

    def body(*refs):
        cps = ex.build(refs[:n_in], refs[n_in:n_in + n_out], refs[n_in + n_out], refs[n_in + n_out + 1])
        for cp in cps:
            cp.start()
        for cp in cps:
            cp.wait()

    return pl.pallas_call(
        body, name=name,
        in_specs=[_ANY] * n_in, out_specs=[_ANY] * n_out, out_shape=ex.out_shapes,
        input_output_aliases={i: i for i in range(n_in)} if ex.alias else {},
        scratch_shapes=[pltpu.SemaphoreType.DMA((ex.n_sems,)), pltpu.SemaphoreType.DMA((ex.n_sems,))],
        compiler_params=pltpu.CompilerParams(has_side_effects=True),
    )(*ex.srcs)


def _gather_sizes(shards, kinds):
    sizes = [s.shape[0] if k == "rows" else s.shape[1] for s, k in zip(shards, kinds)]
    fulls = [SDS((s.shape[0] * N_DEV,) + s.shape[1:], s.dtype) if k == "rows"
             else SDS((s.shape[0], s.shape[1] * N_DEV), s.dtype) for s, k in zip(shards, kinds)]
    return sizes, fulls


def _gather_direct(shards, kinds):
    n = len(shards)
    sizes, fulls = _gather_sizes(shards, kinds)

    def build(ins, outs, send_sems, recv_sems):
        x, y, c = _me()
        cps = []
        for a in range(n):
            mine = _blk(outs[a], kinds[a], 4 * x + 2 * y + c, sizes[a])
            cps.append(pltpu.make_async_copy(ins[a], mine, send_sems.at[5 * a + 4]))
            for j in range(N_CHIP):
                to = (x, y, 1 - c) if j == 0 else (*_chip_peer(x, y, j), c)
                cps.append(pltpu.make_async_remote_copy(
                    src_ref=ins[a], dst_ref=mine, send_sem=send_sems.at[5 * a + j], recv_sem=recv_sems.at[5 * a + j],
                    device_id=to, device_id_type=MESH))
        return cps

    return _Exchange(shards, fulls, 5 * n, build)


def _gather_forward(fulls, kinds, sizes):
    n = len(fulls)

    def build(ins, outs, send_sems, recv_sems):
        x, y, c = _me()
        cps = []
        for a in range(n):
            for j in (1, 2, 3):
                px, py = _chip_peer(x, y, j)
                k = 4 * px + 2 * py + c
                cps.append(pltpu.make_async_remote_copy(
                    src_ref=_blk(ins[a], kinds[a], k, sizes[a]), dst_ref=_blk(outs[a], kinds[a], k, sizes[a]),
                    send_sem=send_sems.at[3 * a + j - 1], recv_sem=recv_sems.at[3 * a + j - 1],
                    device_id=(x, y, 1 - c), device_id_type=MESH))
        return cps

    return _Exchange(fulls, [SDS(f.shape, f.dtype) for f in fulls], 3 * n, build, alias=True)


def _sibling_exchange(grads, kinds, sizes):
    n = len(grads)

    def blk_shape(a):
        g = grads[a]
        return (sizes[a],) + g.shape[1:] if kinds[a] == "rows" else (g.shape[0], sizes[a])

    def build(ins, outs, send_sems, recv_sems):
        x, y, c = _me()
        cps = []
        for a in range(n):
            for q in range(N_CHIP):
                cps.append(pltpu.make_async_remote_copy(
                    src_ref=_blk(ins[a], kinds[a], 2 * q + (1 - c), sizes[a]), dst_ref=outs[a].at[q],
                    send_sem=send_sems.at[N_CHIP * a + q], recv_sem=recv_sems.at[N_CHIP * a + q],
                    device_id=(x, y, 1 - c), device_id_type=MESH))
        return cps

    return _Exchange(grads, [SDS((N_CHIP,) + blk_shape(a), F32) for a in range(n)], N_CHIP * n, build)


def _chip_partial(g, r1, kind, size, cidx, name):
    if kind == "rows":
        rows, cols = size, g.shape[1]
        g3 = g.reshape(N_DEV, rows, cols)
        rb = _pick(rows, 512, 16)
        g_spec = pl.BlockSpec((1, rb, cols), lambda q, j, c: (2 * q + c[0], j, 0))
        grid = (N_CHIP, rows // rb)
        blk = (1, rb, cols)
        imap = lambda q, j, c: (q, j, 0)
    else:
        rows, cols = g.shape[0], size
        g3 = g
        g_spec = pl.BlockSpec((rows, cols), lambda q, j, c: (0, 2 * q + c[0]))
        grid = (N_CHIP, 1)
        blk = (1, rows, cols)
        imap = lambda q, j, c: (q, 0, 0)

    def body(c_ref, g_ref, r_ref, p_ref, pb_ref):
        s = g_ref[...].reshape(blk) + r_ref[...]
        p_ref[...] = s
        pb_ref[...] = s.astype(BF16)

    return pl.pallas_call(
        body, name=name,
        grid_spec=pltpu.PrefetchScalarGridSpec(
            num_scalar_prefetch=1, grid=grid,
            in_specs=[g_spec, pl.BlockSpec(blk, imap)],
            out_specs=[pl.BlockSpec(blk, imap), pl.BlockSpec(blk, imap)]),
        out_shape=[SDS((N_CHIP, rows, cols), F32), SDS((N_CHIP, rows, cols), BF16)],
        compiler_params=_cparams(("arbitrary", "arbitrary")),
    )(cidx, g3, r1)


def _chip_exchange(parts):
    n = len(parts)
    m = N_CHIP - 1

    def build(ins, outs, send_sems, recv_sems):
        x, y, c = _me()
        cps = []
        for a in range(n):
            for j in (1, 2, 3):
                px, py = _chip_peer(x, y, j)
                cps.append(pltpu.make_async_remote_copy(
                    src_ref=ins[a].at[2 * px + py], dst_ref=outs[a].at[j - 1], send_sem=send_sems.at[m * a + j - 1],
                    recv_sem=recv_sems.at[m * a + j - 1], device_id=(px, py, c), device_id_type=MESH))
        return cps

    return _Exchange(parts, [SDS((m,) + p.shape[1:], BF16) for p in parts], m * n, build)


def _grad_total(part, r2, qidx, name):
    _, rows, cols = part.shape
    rb = _pick(rows, 512, 16)

    def body(q_ref, p_ref, r_ref, o_ref):
        s = p_ref[0]
        for j in range(N_CHIP - 1):
            s = s + r_ref[j].astype(F32)
        o_ref[...] = s

    return pl.pallas_call(
        body, name=name,
        grid_spec=pltpu.PrefetchScalarGridSpec(
            num_scalar_prefetch=1, grid=(rows // rb,),
            in_specs=[pl.BlockSpec((1, rb, cols), lambda i, q: (q[0], i, 0)),
                      pl.BlockSpec((N_CHIP - 1, rb, cols), lambda i, q: (0, i, 0))],
            out_specs=pl.BlockSpec((rb, cols), lambda i, q: (i, 0))),
        out_shape=SDS((rows, cols), F32),
        compiler_params=_cparams(("arbitrary",)),
    )(qidx, part, r2)


def _all_reduce_small(pack, name):
    rows = pack.shape[0]
    rs = rows // N_DEV
    assert rs * N_DEV == rows and rs % 8 == 0

    def body(x_ref, o_ref, rbuf, red, send1, recv1, send2, recv2):
        x, y, c = _me()
        me = 4 * x + 2 * y + c

        def peer(d):
            px = (1 - x) if (d >> 2) & 1 else x
            py = (1 - y) if (d >> 1) & 1 else y
            pc = (1 - c) if d & 1 else c
            return px, py, pc

        def sl(ref, k):
            return ref.at[pl.ds(pl.multiple_of(k * rs, 8), rs)]

        phase1 = []
        for d in range(1, N_DEV):
            px, py, pc = peer(d)
            phase1.append(pltpu.make_async_remote_copy(
                src_ref=sl(x_ref, 4 * px + 2 * py + pc), dst_ref=rbuf.at[d], send_sem=send1.at[d], recv_sem=recv1.at[d],
                device_id=(px, py, pc), device_id_type=MESH))
        for cp in phase1:
            cp.start()
        acc = sl(x_ref, me)[...]
        for cp in phase1:
            cp.wait()
        for d in range(1, N_DEV):
            acc = acc + rbuf[d]
        red[...] = acc
        sl(o_ref, me)[...] = acc
        phase2 = []
        for d in range(1, N_DEV):
            px, py, pc = peer(d)
            phase2.append(pltpu.make_async_remote_copy(
                src_ref=red, dst_ref=sl(o_ref, me), send_sem=send2.at[d], recv_sem=recv2.at[d],
                device_id=(px, py, pc), device_id_type=MESH))
        for cp in phase2:
            cp.start()
        for cp in phase2:
            cp.wait()

    vm = pl.BlockSpec(memory_space=pltpu.VMEM)
    return pl.pallas_call(
        body, name=name, in_specs=[vm], out_specs=vm, out_shape=SDS(pack.shape, F32),
        scratch_shapes=[pltpu.VMEM((N_DEV, rs, LANES), F32), pltpu.VMEM((rs, LANES), F32),
                        pltpu.SemaphoreType.DMA((N_DEV,)), pltpu.SemaphoreType.DMA((N_DEV,)),
                        pltpu.SemaphoreType.DMA((N_DEV,)), pltpu.SemaphoreType.DMA((N_DEV,))],
        compiler_params=_cparams(None, has_side_effects=True),
    )(pack)


def _adamw(w, g, m, v, name):
    rows, cols = w.shape
    rb = rows if rows * cols * 4 <= ADAMW_WHOLE_BYTES else _pick(rows, 256, 8)
    c1 = 1.0 / (1.0 - ADAM_B1 ** ADAM_STEP)
    c2 = 1.0 / (1.0 - ADAM_B2 ** ADAM_STEP)

    def body(w_ref, g_ref, m_ref, v_ref, d_ref, mo_ref, vo_ref):
        gv = g_ref[...]
        mn = ADAM_B1 * m_ref[...] + (1.0 - ADAM_B1) * gv
        vn = ADAM_B2 * v_ref[...] + (1.0 - ADAM_B2) * (gv * gv)
        mo_ref[...] = mn
        vo_ref[...] = vn
        d_ref[...] = -ADAM_LR * ((mn * c1) / (jnp.sqrt(vn * c2) + ADAM_EPS) + ADAM_WD * w_ref[...])

    spec = pl.BlockSpec((rb, cols), lambda i: (i, 0))
    return pl.pallas_call(
        body, name=name, grid=(rows // rb,),
        in_specs=[spec] * 4, out_specs=[spec] * 3, out_shape=[SDS((rows, cols), F32)] * 3,
        compiler_params=_cparams(("arbitrary",)),
    )(w, g, m, v)


def _pad_rows(a, mult=8):
    r = (-a.shape[0]) % mult
    return a if r == 0 else jnp.pad(a, ((0, r), (0, 0)))


def _as_lanes(a):
    flat = a.reshape(-1)
    pad = (-flat.shape[0]) % (8 * LANES)
    if pad:
        flat = jnp.pad(flat, (0, pad))
    return flat.reshape(-1, LANES)


def kernel(x, norm_g, w_in, ln_g, ln_b, w_s, b_s, conv_w, conv_b, w_pool, pool_scale, w_pa, w_pb, w_pc, w_o, final_g, loss_target, m_norm_g, m_w_in, m_ln_g, m_ln_b, m_w_s, m_b_s, m_conv_w, m_conv_b, m_w_pool, m_pool_scale, m_w_pa, m_w_pb, m_w_pc, m_w_o, m_final_g, v_norm_g, v_w_in, v_ln_g, v_ln_b, v_w_s, v_b_s, v_conv_w, v_conv_b, v_w_pool, v_pool_scale, v_w_pa, v_w_pb, v_w_pc, v_w_o, v_final_g):
    L = w_in.shape[0]
    D = x.shape[-1]
    n_loc = w_in.shape[2]
    pc_loc = w_pa.shape[2]
    x0 = x[0]
    target = loss_target[0]
    xi, yi, ci = _me()
    cidx = jnp.reshape(ci, (1,)).astype(jnp.int32)
    qidx = jnp.reshape(2 * xi + yi, (1,)).astype(jnp.int32)

    kinds5 = ["rows", "cols", "cols", "cols", "rows"]

    def layer_shards(l):
        return [w_in[l].T.astype(BF16), w_pa[l].astype(BF16), w_pb[l].astype(BF16), w_pc[l].astype(BF16),
                w_o[l].astype(BF16)]

    def gathered(direct, shards, kinds, l):
        sizes, _ = _gather_sizes(shards, kinds)
        return _run_exchange(_gather_forward(direct, kinds, sizes), f"weights_forward_{l}")

    cw_loc = _pad_rows(conv_w.reshape(L * CONV_TAPS, -1))
    cw_loc = jnp.pad(cw_loc, ((0, 0), (0, LANES - cw_loc.shape[1])))
    causal = jnp.tril(jnp.ones((CHUNK, CHUNK), dtype=bool))

    chip_order = jnp.stack([2 * xi + yi] + [2 * px + py for px, py in (_chip_peer(xi, yi, j) for j in (1, 2, 3))])
    sh0 = layer_shards(0)
    rest0, krest0 = sh0[1:] + [cw_loc], kinds5[1:] + ["rows"]
    (p0, h0, win_t0), delivered = _inproj_gathering(x0, norm_g[0], sh0[0], chip_order.astype(jnp.int32), "inproj_fwd_0",
                                                    [_gather_direct(rest0, krest0)])
    rest0_full = gathered(delivered[0], rest0, krest0, 0)
    cw_all = rest0_full[-1].reshape(N_DEV, -1, LANES)[:, :L * CONV_TAPS, :conv_w.shape[2]]
    conv_w_full = jnp.transpose(cw_all, (1, 0, 2)).reshape(L, CONV_TAPS, -1)

    def make_layer(l, full5):
        win_t, wpa, wpb, wpc, wo = full5
        wm = jnp.where(causal, w_s[l], 0.0)
        cvec = jnp.concatenate([ln_g[l][None], ln_b[l][None], conv_w_full[l], conv_b[l][None], pool_scale[l][None],
                                jnp.zeros((C_ROWS - 7, SEG), F32)], axis=0)
        return dict(
            win_t=win_t, wpa=wpa, wpb=wpb, wpc=wpc, wo=wo, cvec=cvec,
            bsb=jnp.repeat(b_s[l].T, HEAD, axis=1),
            wcat=jnp.transpose(wm, (1, 0, 2)).reshape(CHUNK, GROUPS * CHUNK).astype(BF16),
            wcatt=jnp.transpose(wm, (2, 0, 1)).reshape(CHUNK, GROUPS * CHUNK).astype(BF16),
            wpool=w_pool[l].astype(BF16))

    layers = [make_layer(0, [win_t0] + list(rest0_full[:4]))]
    xs, saved = [x0], []
    for l in range(L):
        lw = layers[l]
        if l == 0:
            p, h = p0, h0
        else:
            (p, h), _ = _inproj(xs[-1], norm_g[l], lw["win_t"], f"inproj_fwd_{l}")
        nxt = layer_shards(l + 1) if l + 1 < L else None
        (xn, ya, yb, yc), delivered = _mix_fwd(p, xs[-1], lw, f"mix_fwd_{l}", [_gather_direct(nxt, kinds5)] if nxt else [])
        if nxt:
            layers.append(make_layer(l + 1, gathered(delivered[0], nxt, kinds5, l + 1)))
        saved.append((p, h, ya, yb, yc))
        xs.append(xn)
    dx, loss_acc, dfg_acc = _loss_head(xs[-1], final_g, target, "loss_head")

    rs_sizes = [n_loc, pc_loc, pc_loc, pc_loc, w_o.shape[1]]
    await_sibling, await_chips = [], []
    partial_of, from_chips = {}, {}
    serial = [0]

    def riders_now():
        riders, plan = [], []
        for grp in await_chips:
            riders.append(_chip_exchange([partial_of[t][1] for t, _, _, _ in grp]))
            plan.append(("chips", grp))
        for grp in await_sibling:
            riders.append(_sibling_exchange([g for _, g, _, _ in grp], [k for _, _, k, _ in grp], [s for _, _, _, s in grp]))
            plan.append(("sibling", grp))
        del await_chips[:], await_sibling[:]
        return riders, plan

    def absorb(plan, delivered):
        for (what, grp), res in zip(plan, delivered):
            for (t, g, k, s), r in zip(grp, res):
                if what == "chips":
                    from_chips[t] = r
                else:
                    partial_of[t] = _chip_partial(g, r, k, s, cidx, f"grad_chip_partial_{t[0]}_{t[1]}")
            if what == "sibling":
                await_chips.append(grp)

    small = [None] * L
    for l in reversed(range(L)):
        lw = layers[l]
        p, h, ya, yb, yc = saved[l]
        dp, acts, merged, dys, gwc, gbs, gwpool, gvec = _mix_bwd(p, dx, ya, yb, yc, lw, f"mix_bwd_{l}")
        gwpa, gwpb, gwpc, gwo = _proj_wgrad(acts, merged, dys, dx, f"proj_wgrad_{l}")
        await_sibling.append([((l, a), g, kinds5[a], rs_sizes[a]) for a, g in ((1, gwpa), (2, gwpb), (3, gwpc), (4, gwo))])

        def bwd_x(dxo, pieces):
            nt = _inproj_token_blocks(dxo.shape[0])
            pieces = min(pieces, nt)
            done, dng, b0 = None, None, 0
            for k in range(pieces):
                cnt = (nt - b0) // (pieces - k)
                riders, plan = riders_now()
                (done, dng_k), delivered = _inproj_bwd_x(dp, lw["win_t"], xs[l], norm_g[l], dxo, f"inproj_bwd_x_{l}_{k}",
                                                         riders, blocks=(b0, cnt), fill=done)
                absorb(plan, delivered)
                dng = dng_k if dng is None else dng + dng_k
                b0 += cnt
            return done, dng

        def bwd_w():
            riders, plan = riders_now()
            (gwin_t,), delivered = _inproj_bwd_w(dp, h, f"inproj_bwd_w_{l}", riders)
            absorb(plan, delivered)
            await_sibling.append([((l, 0), gwin_t, kinds5[0], rs_sizes[0])])

        if l == L - 1:
            dx, dng = bwd_x(dx, 1)
            bwd_w()
        else:
            bwd_w()
            dx, dng = bwd_x(dx, 2 if l == 0 else 1)
        small[l] = dict(norm_g=dng[0], ln_g=gvec[V_LNG], ln_b=gvec[V_LNB], w_s=gwc, b_s=gbs, conv_w=gvec[V_CW0:V_CW0 + 3],
                        conv_b=gvec[V_CB], w_pool=gwpool, pool_scale=gvec[V_PS])
    while await_sibling or await_chips:
        riders, plan = riders_now()
        delivered = []
        for ex in riders:
            delivered.append(_run_exchange(ex, f"grad_exchange_tail_{serial[0]}"))
            serial[0] += 1
        absorb(plan, delivered)
    grad_x = dx[None]
    big_grads = []
    for l in range(L):
        tot = [_grad_total(partial_of[(l, a)][0], from_chips[(l, a)], qidx, f"grad_total_{l}_{a}") for a in range(5)]
        big_grads.append([tot[0].T,
                          tot[1].reshape(SEG, pc_loc), tot[2].reshape(SEG, pc_loc), tot[3].reshape(SEG, pc_loc),
                          tot[4]])

    names = ["norm_g", "ln_g", "ln_b", "w_s", "b_s", "conv_w", "conv_b", "w_pool", "pool_scale"]
    pieces = [_as_lanes(jnp.stack([small[l][nm] for l in range(L)])) for nm in names]
    pieces += [_as_lanes(dfg_acc[0]), loss_acc]
    sizes = [pc.shape[0] for pc in pieces]
    pack = jnp.concatenate(pieces, axis=0)
    pack = _pad_rows(pack, 8 * N_DEV)
    red = _all_reduce_small(pack, "small_grads_all_reduce")
    offs = [0]
    for s in sizes:
        offs.append(offs[-1] + s)

    def unpack(i, shape):
        n = math.prod(shape)
        return red[offs[i]:offs[i + 1]].reshape(-1)[:n].reshape(shape)

    g_norm_g = unpack(0, (L, D))
    g_ln_g = unpack(1, (L, SEG))
    g_ln_b = unpack(2, (L, SEG))
    g_w_s = unpack(3, (L, GROUPS, CHUNK, CHUNK))
    g_b_s = jnp.transpose(unpack(4, (L, CHUNK, LANES))[:, :, :GROUPS], (0, 2, 1))
    g_conv_w_full = unpack(5, (L, CONV_TAPS, SEG))
    g_conv_b = unpack(6, (L, SEG))
    g_w_pool = unpack(7, (L, len(POOL_WINDOWS), POOL_GROUP, POOL_GROUP))
    g_pool_scale = unpack(8, (L, SEG))
    g_final_g = unpack(9, (D,))
    loss = red[offs[10], 0]
    dev = 4 * xi + 2 * yi + ci
    g_conv_w = lax.dynamic_slice_in_dim(g_conv_w_full, dev * conv_w.shape[2], conv_w.shape[2], axis=2)

    g_w_in = jnp.stack([big_grads[l][0] for l in range(L)])
    g_w_pa = jnp.stack([big_grads[l][1] for l in range(L)])
    g_w_pb = jnp.stack([big_grads[l][2] for l in range(L)])
    g_w_pc = jnp.stack([big_grads[l][3] for l in range(L)])
    g_w_o = jnp.stack([big_grads[l][4] for l in range(L)])

    grads = dict(norm_g=g_norm_g, w_in=g_w_in, ln_g=g_ln_g, ln_b=g_ln_b, w_s=g_w_s, b_s=g_b_s, conv_w=g_conv_w,
                 conv_b=g_conv_b, w_pool=g_w_pool, pool_scale=g_pool_scale, w_pa=g_w_pa, w_pb=g_w_pb, w_pc=g_w_pc,
                 w_o=g_w_o, final_g=g_final_g)
    weights = dict(norm_g=norm_g, w_in=w_in, ln_g=ln_g, ln_b=ln_b, w_s=w_s, b_s=b_s, conv_w=conv_w, conv_b=conv_b,
                   w_pool=w_pool, pool_scale=pool_scale, w_pa=w_pa, w_pb=w_pb, w_pc=w_pc, w_o=w_o, final_g=final_g)
    ms = dict(norm_g=m_norm_g, w_in=m_w_in, ln_g=m_ln_g, ln_b=m_ln_b, w_s=m_w_s, b_s=m_b_s, conv_w=m_conv_w,
              conv_b=m_conv_b, w_pool=m_w_pool, pool_scale=m_pool_scale, w_pa=m_w_pa, w_pb=m_w_pb, w_pc=m_w_pc,
              w_o=m_w_o, final_g=m_final_g)
    vs = dict(norm_g=v_norm_g, w_in=v_w_in, ln_g=v_ln_g, ln_b=v_ln_b, w_s=v_w_s, b_s=v_b_s, conv_w=v_conv_w,
              conv_b=v_conv_b, w_pool=v_w_pool, pool_scale=v_pool_scale, w_pa=v_w_pa, w_pb=v_w_pb, w_pc=v_w_pc,
              w_o=v_w_o, final_g=v_final_g)
    order = ["norm_g", "w_in", "ln_g", "ln_b", "w_s", "b_s", "conv_w", "conv_b", "w_pool", "pool_scale", "w_pa", "w_pb",
             "w_pc", "w_o", "final_g"]

    delta, new_m, new_v = {}, {}, {}
    big = ["w_in", "w_pa", "w_pb", "w_pc", "w_o"]
    for nm in big:
        shp = weights[nm].shape
        two = lambda a: a.reshape(-1, shp[-1])
        d, mn, vn = _adamw(two(weights[nm]), two(grads[nm]), two(ms[nm]), two(vs[nm]), f"adamw_{nm}")
        delta[nm], new_m[nm], new_v[nm] = d.reshape(shp), mn.reshape(shp), vn.reshape(shp)
    rest = [nm for nm in order if nm not in big]
    cat = lambda src: jnp.concatenate([_as_lanes(src[nm]) for nm in rest], axis=0)
    d, mn, vn = _adamw(cat(weights), cat(grads), cat(ms), cat(vs), "adamw_small")
    off = 0
    for nm in rest:
        shp = weights[nm].shape
        n = math.prod(shp)
        rows = _as_lanes(weights[nm]).shape[0]
        cut = lambda a: a[off:off + rows].reshape(-1)[:n].reshape(shp)
        delta[nm], new_m[nm], new_v[nm] = cut(d), cut(mn), cut(vn)
        off += rows

    return (loss, grad_x, *[grads[nm] for nm in order], *[delta[nm] for nm in order],
            *[new_m[nm] for nm in order], *[new_v[nm] for nm in order])
```

```python
import functools
import math

import numpy as np
import jax
import jax.numpy as jnp
from jax import lax
from jax.experimental import pallas as pl
from jax.experimental.pallas import tpu as pltpu

F32 = jnp.float32
BF16 = jnp.bfloat16
SDS = jax.ShapeDtypeStruct
MESH = pl.DeviceIdType.MESH

SEG = 512
CHUNK = 128
GROUPS = 8
HEAD = SEG // GROUPS
POOL_WINDOWS = (2, 4, 8, 16)
POOL_GROUP = SEG // len(POOL_WINDOWS)
CONV_TAPS = 3
HALO = 16
RMS_EPS = 1e-6
LN_EPS = 1e-5
ADAM_LR, ADAM_B1, ADAM_B2, ADAM_EPS, ADAM_WD, ADAM_STEP = 0.001, 0.9, 0.999, 1e-08, 0.01, 10

O_U, O_V, O_ZA, O_XB, O_BG, O_CG, O_ZB, O_XC, O_ZC, O_G = (SEG * i for i in range(10))

N_DEV = 8
N_CHIP = 4
LANES = 128
VMEM_LIMIT = 48 * 1024 * 1024
ADAMW_WHOLE_BYTES = 2 * 1024 * 1024
INPROJ_ROWS = 1024


def _cparams(sem=None, **kw):
    return pltpu.CompilerParams(dimension_semantics=sem, vmem_limit_bytes=VMEM_LIMIT, **kw)


def _pick(total, target, mult):
    best = None
    for d in range(mult, min(total, target) + 1, mult):
        if total % d == 0:
            best = d
    assert best is not None, (total, target, mult)
    return best


def _dot(a, b):
    return jnp.dot(a, b, preferred_element_type=F32)


def _dot_nt(a, b):
    return lax.dot_general(a, b, (((1,), (1,)), ((), ())), preferred_element_type=F32)


def _dot_tn(a, b):
    return lax.dot_general(a, b, (((0,), (0,)), ((), ())), preferred_element_type=F32)


def _zero(ref):
    ref[...] = jnp.zeros(ref.shape, ref.dtype)


def _sigmoid(x):
    return 1.0 / (1.0 + jnp.exp(-x))


_GELU_C = math.sqrt(2.0 / math.pi)


def _gelu(x):
    t = jnp.tanh(_GELU_C * (x + 0.044715 * x * x * x))
    return 0.5 * x * (1.0 + t), t


def _gelu_grad(x, t):
    return 0.5 * (1.0 + t) + 0.5 * x * (1.0 - t * t) * _GELU_C * (1.0 + 3.0 * 0.044715 * x * x)


def _me():
    return lax.axis_index("x"), lax.axis_index("y"), lax.axis_index("c")


def _inproj(x, norm_g, win_t, name, riders=()):
    T, D = x.shape
    N = win_t.shape[0]
    bt = _pick(T, INPROJ_ROWS, 16)
    bn = _pick(N, 1536, LANES)
    grid = (T // bt, N // bn)

    def compute(x_ref, g_ref, w_ref, p_ref, h_ref, hs_ref):
        @pl.when(pl.program_id(1) == 0)
        def _():
            xv = x_ref[...]
            rstd = lax.rsqrt(jnp.mean(xv * xv, axis=-1, keepdims=True) + RMS_EPS)
            hb = (xv * rstd * g_ref[...]).astype(BF16)
            hs_ref[...] = hb
            h_ref[...] = hb

        p_ref[...] = _dot_nt(hs_ref[...], w_ref[...]).astype(BF16)

    return _host_call(
        compute, name, grid, riders,
        inputs=[x, norm_g.reshape(1, D), win_t],
        in_specs=[pl.BlockSpec((bt, D), lambda i, j: (i, 0)),
                  pl.BlockSpec((1, D), lambda i, j: (0, 0)),
                  pl.BlockSpec((bn, D), lambda i, j: (j, 0))],
        out_specs=[pl.BlockSpec((bt, bn), lambda i, j: (i, j)),
                   pl.BlockSpec((bt, D), lambda i, j: (i, 0))],
        out_shape=[SDS((T, N), BF16), SDS((T, D), BF16)],
        scratch_shapes=[pltpu.VMEM((bt, D), BF16)])


def _inproj_gathering(x, norm_g, w_loc, chip_order, name, riders=()):
    T, D = x.shape
    n = w_loc.shape[0]
    N = n * N_DEV
    cw = 2 * n
    bt = _pick(T, INPROJ_ROWS, 16)
    nt = T // bt
    r_in, r_out, r_sems, copies = _rider_plan(riders)
    n_rin, n_rout = len(r_in), len(r_out)

    def body(q_ref, x_ref, g_ref, wloc_ref, *rest):
        rins = rest[:n_rin]
        p_ref, h_ref, wfull_ref = rest[n_rin:n_rin + 3]
        routs = rest[n_rin + 3:n_rin + 3 + n_rout]
        hs_ref, wbuf, send_sems, recv_sems, loc_sems = rest[n_rin + 3 + n_rout:n_rin + 8 + n_rout]
        rsems = rest[n_rin + 8 + n_rout:]
        j, i = pl.program_id(0), pl.program_id(1)
        cx, cy, cc = _me()
        sibling = (cx, cy, 1 - cc)

        def rows(k):
            return wfull_ref.at[pl.ds(pl.multiple_of(k * n, 8), n)]

        def shard_copy(slot, src, k, to):
            return pltpu.make_async_remote_copy(src_ref=src, dst_ref=rows(k), send_sem=send_sems.at[slot],
                                                recv_sem=recv_sems.at[slot], device_id=to, device_id_type=MESH)

        me = 4 * cx + 2 * cy + cc
        place_mine = pltpu.make_async_copy(wloc_ref, rows(me), loc_sems.at[0])
        sends = [shard_copy(0, wloc_ref, me, sibling)]
        for jj in (1, 2, 3):
            sends.append(shard_copy(jj, wloc_ref, me, (*_chip_peer(cx, cy, jj), cc)))

        def forward(jj):
            px, py = _chip_peer(cx, cy, jj)
            k = 4 * px + 2 * py + cc
            return shard_copy(3 + jj, rows(k), k, sibling)

        def load_chunk(q):
            cp = pltpu.make_async_copy(wfull_ref.at[pl.ds(pl.multiple_of(q * cw, 8), cw)], wbuf, loc_sems.at[1])
            cp.start()
            cp.wait()

        keep_h = pltpu.make_async_copy(hs_ref, h_ref, loc_sems.at[2])

        @pl.when((j == 0) & (i == 0))
        def _():
            place_mine.start()
            for cp in sends:
                cp.start()
            for cp in copies(rins, routs, rsems):
                cp.start()
            place_mine.wait()
            sends[0].wait_recv()
            load_chunk(q_ref[0])

        for jj in (1, 2, 3):
            @pl.when((j == jj) & (i == 0))
            def _(jj=jj):
                sends[jj].wait_recv()
                fwd = forward(jj)
                fwd.start()
                fwd.wait_recv()
                load_chunk(q_ref[jj])

        tok = pl.ds(pl.multiple_of(i * bt, bt), bt)

        @pl.when(j == 0)
        def _():
            xv = x_ref[...]
            rstd = lax.rsqrt(jnp.mean(xv * xv, axis=-1, keepdims=True) + RMS_EPS)
            hs_ref[tok, :] = (xv * rstd * g_ref[...]).astype(BF16)

        @pl.when((j == 0) & (i == nt - 1))
        def _():
            keep_h.start()

        p_ref[...] = _dot_nt(hs_ref[tok, :], wbuf[...]).astype(BF16)

        @pl.when((j == N_CHIP - 1) & (i == nt - 1))
        def _():
            keep_h.wait()
            for cp in sends:
                cp.wait_send()
            for jj in (1, 2, 3):
                forward(jj).wait_send()
            for cp in copies(rins, routs, rsems):
                cp.wait()

    res = pl.pallas_call(
        body, name=name,
        grid_spec=pltpu.PrefetchScalarGridSpec(
            num_scalar_prefetch=1, grid=(N_CHIP, nt),
            in_specs=[pl.BlockSpec((bt, D), lambda j, i, q: (jnp.where(j == 0, i, nt - 1), 0)),
                      pl.BlockSpec((1, D), lambda j, i, q: (0, 0)), _ANY] + [_ANY] * n_rin,
            out_specs=[pl.BlockSpec((bt, cw), lambda j, i, q: (i, q[j])), _ANY, _ANY] + [_ANY] * n_rout,
            scratch_shapes=[pltpu.VMEM((T, D), BF16), pltpu.VMEM((cw, D), BF16), pltpu.SemaphoreType.DMA((7,)),
                            pltpu.SemaphoreType.DMA((7,)), pltpu.SemaphoreType.DMA((3,))] + r_sems),
        out_shape=[SDS((T, N), BF16), SDS((T, D), BF16), SDS((N, D), BF16)] + r_out,
        compiler_params=_cparams(("arbitrary", "arbitrary")),
    )(chip_order, x, norm_g.reshape(1, D), w_loc, *r_in)
    return res[:3], _split_riders(riders, res[3:])


C_LNG, C_LNB, C_CW0, C_CW1, C_CW2, C_CB, C_PS = range(7)
C_ROWS = 8


def _pool_bands(R, anticausal):
    t = np.arange(R)[:, None]
    s = np.arange(R + CHUNK)[None, :]
    bands = [((s >= t) & (s < t + w)) if anticausal else ((s > t + CHUNK - w) & (s <= t + CHUNK)) for w in POOL_WINDOWS]
    return jnp.asarray(np.stack(bands), dtype=BF16)


def _mixers(p_ref, hxb_ref, hcg_ref, hxc_ref, cv, bsb_ref, wcat_ref, wpool_ref, band_ref, extb,
            first, blk, R, need_grad):
    def seg(lo):
        return p_ref[:, lo:lo + SEG].astype(F32)

    u, v, za = seg(O_U), seg(O_V), seg(O_ZA)
    xb, bg, cg, zb = seg(O_XB), seg(O_BG), seg(O_CG), seg(O_ZB)
    xc, zc = seg(O_XC), seg(O_ZC)
    out = {}

    ug, tu = _gelu(u)
    vg, tv = _gelu(v)
    mu = jnp.mean(vg, axis=-1, keepdims=True)
    vcen = vg - mu
    rs = lax.rsqrt(jnp.mean(vcen * vcen, axis=-1, keepdims=True) + LN_EPS)
    vhat = vcen * rs
    vn = (vhat * cv[C_LNG:C_LNG + 1, :] + cv[C_LNB:C_LNB + 1, :]).astype(BF16)
    lane_group = lax.broadcasted_iota(jnp.int32, (CHUNK, SEG), 1) // HEAD
    zero_b = jnp.zeros((CHUNK, SEG), BF16)
    sgs = []
    for ci in range(R // CHUNK):
        vc = vn[ci * CHUNK:(ci + 1) * CHUNK]
        vst = jnp.concatenate([jnp.where(lane_group == g, vc, zero_b) for g in range(GROUPS)], axis=0)
        sgs.append(_dot(wcat_ref[...], vst) + bsb_ref[...])
    sg = sgs[0] if len(sgs) == 1 else jnp.concatenate(sgs, axis=0)
    a_out = ug * sg
    sa = _sigmoid(za)
    out["a"] = a_out * (za * sa)

    cx = cg * xb
    halo_b = hcg_ref[...].astype(F32) * hxb_ref[...].astype(F32)
    extb[0:HALO, :] = jnp.where(first, 0.0, halo_b)
    extb[HALO:HALO + R, :] = cx
    cx1 = extb[pl.ds(HALO - 1, R), :]
    cx2 = extb[pl.ds(HALO - 2, R), :]
    yconv = (cv[C_CW0:C_CW0 + 1, :] * cx2 + cv[C_CW1:C_CW1 + 1, :] * cx1
             + cv[C_CW2:C_CW2 + 1, :] * cx + cv[C_CB:C_CB + 1, :])
    b_out = bg * yconv
    sb = _sigmoid(zb)
    out["b"] = b_out * (zb * sb)

    halo_c = hxc_ref[...]
    xc_ext = jnp.concatenate([jnp.zeros((CHUNK - HALO, SEG), BF16), jnp.where(first, jnp.zeros_like(halo_c), halo_c),
                              p_ref[:, O_XC:O_XC + SEG]], axis=0)
    tpos = blk * R + lax.broadcasted_iota(jnp.int32, (R, POOL_GROUP), 0) + 1
    pooled, invs, pws = [], [], []
    for gi, w in enumerate(POOL_WINDOWS):
        lo = gi * POOL_GROUP
        win = _dot(band_ref[gi], xc_ext[:, lo:lo + POOL_GROUP])
        inv = 1.0 / jnp.minimum(tpos, w).astype(F32)
        pg = (win * inv - xc[:, lo:lo + POOL_GROUP]).astype(BF16)
        pooled.append(pg)
        invs.append(inv)
        pws.append(_dot(pg, wpool_ref[gi]))
    pw = jnp.concatenate(pws, axis=1)
    c_out = pw * cv[C_PS:C_PS + 1, :]
    sc = _sigmoid(zc)
    out["c"] = c_out * (zc * sc)

    if need_grad:
        out.update(u=u, v=v, tu=tu, tv=tv, ug=ug, sg=sg, a_out=a_out, za=za, sa=sa,
                   rs=rs, vhat=vhat, vn=vn, lane_group=lane_group, zero_b=zero_b,
                   xb=xb, bg=bg, cg=cg, cx=cx, cx1=cx1, cx2=cx2, yconv=yconv, b_out=b_out, zb=zb, sb=sb,
                   pooled=pooled, invs=invs, pw=pw, c_out=c_out, zc=zc, sc=sc)
    return out


def _halo_specs(R, nb, rev):
    step = R // HALO

    def mk(col):
        def imap(i):
            b = (nb - 1 - i) if rev else i
            return (jnp.maximum(b * step - 1, 0), col)
        return pl.BlockSpec((HALO, SEG), imap)

    return [mk(O_XB // SEG), mk(O_CG // SEG), mk(O_XC // SEG)]


def _const_spec(shape):
    nd = len(shape)
    return pl.BlockSpec(shape, lambda i: (0,) * nd, pipeline_mode=pl.Buffered(1))


MIX_FWD_ROWS = 512
MIX_BWD_ROWS = 256


def _mix_block_rows(T, target):
    return _pick(T, target, CHUNK)


def _mix_fwd(p, x, lw, name, riders=()):
    T, D = x.shape
    N = p.shape[1]
    R = _mix_block_rows(T, MIX_FWD_ROWS)
    nb = T // R

    def body(p_ref, hxb, hcg, hxc, x_ref, cv_ref, bsb_ref, wcat_ref, wpool_ref, band_ref, wpa_ref, wpb_ref, wpc_ref,
             wo_ref, xo_ref, ya_ref, yb_ref, yc_ref, extb):
        i = pl.program_id(0)
        cv = cv_ref[...]
        r = _mixers(p_ref, hxb, hcg, hxc, cv, bsb_ref, wcat_ref, wpool_ref, band_ref, extb,
                    i == 0, i, R, False)
        merged = None
        for k, (act, w_ref, y_ref) in enumerate(((r["a"], wpa_ref, ya_ref), (r["b"], wpb_ref, yb_ref),
                                                 (r["c"], wpc_ref, yc_ref))):
            y = _dot(act.astype(BF16), w_ref[...]).astype(BF16)
            y_ref[...] = y
            term = _sigmoid(p_ref[:, O_G + k * D:O_G + (k + 1) * D]) * y
            merged = term if merged is None else merged + term
        xo_ref[...] = x_ref[...] + _dot(merged, wo_ref[...])

    row = lambda w: pl.BlockSpec((R, w), lambda i: (i, 0))
    consts = [lw["cvec"], lw["bsb"], lw["wcat"], lw["wpool"], _pool_bands(R, False), lw["wpa"], lw["wpb"], lw["wpc"],
              lw["wo"]]
    return _host_call(
        body, name, (nb,), riders,
        inputs=[p, p, p, p, x, *consts],
        in_specs=[row(N)] + _halo_specs(R, nb, False) + [row(D)] + [_const_spec(c.shape) for c in consts],
        out_specs=[row(D), row(D), row(D), row(D)],
        out_shape=[SDS((T, D), F32), SDS((T, D), BF16), SDS((T, D), BF16), SDS((T, D), BF16)],
        scratch_shapes=[pltpu.VMEM((HALO + R, SEG), F32)])


def _loss_head(x, final_g, target, name):
    T, D = x.shape
    bt = _pick(T, 512, 8)

    def body(x_ref, g_ref, t_ref, dx_ref, loss_ref, dg_ref):
        @pl.when(pl.program_id(0) == 0)
        def _():
            _zero(loss_ref)
            _zero(dg_ref)

        xv = x_ref[...]
        g = g_ref[...]
        rstd = lax.rsqrt(jnp.mean(xv * xv, axis=-1, keepdims=True) + RMS_EPS)
        xhat = xv * rstd
        err = xhat * g - t_ref[...]
        part = 0.5 * jnp.sum(jnp.sum(err * err, axis=-1, keepdims=True), axis=0, keepdims=True) / D
        loss_ref[...] += jnp.broadcast_to(part, loss_ref.shape)
        dy = err * (1.0 / D)
        dg_ref[0:1, :] += jnp.sum(dy * xhat, axis=0, keepdims=True)
        dxn = dy * g
        dx_ref[...] = rstd * (dxn - xhat * jnp.mean(dxn * xhat, axis=-1, keepdims=True))

    return pl.pallas_call(
        body, name=name, grid=(T // bt,),
        in_specs=[pl.BlockSpec((bt, D), lambda i: (i, 0)), _const_spec((1, D)), pl.BlockSpec((bt, D), lambda i: (i, 0))],
        out_specs=[pl.BlockSpec((bt, D), lambda i: (i, 0)), _const_spec((8, LANES)), _const_spec((8, D))],
        out_shape=[SDS((T, D), F32), SDS((8, LANES), F32), SDS((8, D), F32)],
        compiler_params=_cparams(("arbitrary",)),
    )(x, final_g.reshape(1, D), target)


V_LNG, V_LNB, V_CB, V_PS, V_CW0, V_CW1, V_CW2 = range(7)


def _mix_bwd(p, dxo, ya, yb, yc, lw, name):
    T, D = dxo.shape
    N = p.shape[1]
    R = _mix_block_rows(T, MIX_BWD_ROWS)
    nb = T // R

    def body(p_ref, hxb, hcg, hxc, dxo_ref, ya_ref, yb_ref, yc_ref, cv_ref, bsb_ref, wcat_ref, wcatt_ref,
             wpool_ref, band_ref, bandt_ref, wpa_ref, wpb_ref, wpc_ref, wo_ref,
             dp_ref, acts_ref, mrg_ref, dys_ref, gwc_ref, gbs_ref, gwpool_ref, gvec_ref,
             extb, extdy, cdy, cq, bsacc):
        i = pl.program_id(0)
        blk = nb - 1 - i

        @pl.when(i == 0)
        def _():
            for ref in (gwc_ref, gwpool_ref, gvec_ref, cdy, cq, bsacc):
                _zero(ref)

        cv = cv_ref[...]
        r = _mixers(p_ref, hxb, hcg, hxc, cv, bsb_ref, wcat_ref, wpool_ref, band_ref, extb,
                    blk == 0, blk, R, True)

        dxo_b = dxo_ref[...].astype(BF16)
        dm = _dot_nt(dxo_b, wo_ref[...]).astype(BF16)
        ys = [ya_ref[...], yb_ref[...], yc_ref[...]]
        sig = [_sigmoid(p_ref[:, O_G + k * D:O_G + (k + 1) * D]) for k in range(3)]
        mrg_ref[...] = sig[0] * ys[0] + sig[1] * ys[1] + sig[2] * ys[2]
        dacts = []
        for k, (act, w_ref) in enumerate(((r["a"], wpa_ref), (r["b"], wpb_ref), (r["c"], wpc_ref))):
            dyk = dm * sig[k]
            dp_ref[:, O_G + k * D:O_G + (k + 1) * D] = dyk * ys[k] * (1.0 - sig[k])
            acts_ref[:, k * SEG:(k + 1) * SEG] = act.astype(BF16)
            dys_ref[:, k * D:(k + 1) * D] = dyk
            dacts.append(_dot_nt(dyk, w_ref[...]))
        da, db, dc = dacts

        def silu_bwd(dact, pre, z, s):
            return dact * (z * s), dact * pre * (s * (1.0 + z * (1.0 - s)))

        d_aout, dza = silu_bwd(da, r["a_out"], r["za"], r["sa"])
        dp_ref[:, O_ZA:O_ZA + SEG] = dza.astype(BF16)
        dp_ref[:, O_U:O_U + SEG] = (d_aout * r["sg"] * _gelu_grad(r["u"], r["tu"])).astype(BF16)
        d_sg = d_aout * r["ug"]
        dvns = []
        for ci in range(R // CHUNK):
            dsc = d_sg[ci * CHUNK:(ci + 1) * CHUNK]
            bsacc[...] += dsc
            dsc_b = dsc.astype(BF16)
            dst = jnp.concatenate([jnp.where(r["lane_group"] == g, dsc_b, r["zero_b"]) for g in range(GROUPS)], axis=0)
            dvns.append(_dot(wcatt_ref[...], dst))
            gwc_ref[...] += _dot_nt(dst, r["vn"][ci * CHUNK:(ci + 1) * CHUNK])
        d_vn = dvns[0] if len(dvns) == 1 else jnp.concatenate(dvns, axis=0)
        vhat = r["vhat"]
        gvec_ref[V_LNG:V_LNG + 1, :] += jnp.sum(d_vn * vhat, axis=0, keepdims=True)
        gvec_ref[V_LNB:V_LNB + 1, :] += jnp.sum(d_vn, axis=0, keepdims=True)
        d_vhat = d_vn * cv[C_LNG:C_LNG + 1, :]
        d_vg = r["rs"] * (d_vhat - jnp.mean(d_vhat, axis=-1, keepdims=True)
                          - vhat * jnp.mean(d_vhat * vhat, axis=-1, keepdims=True))
        dp_ref[:, O_V:O_V + SEG] = (d_vg * _gelu_grad(r["v"], r["tv"])).astype(BF16)

        d_bout, dzb = silu_bwd(db, r["b_out"], r["zb"], r["sb"])
        dp_ref[:, O_ZB:O_ZB + SEG] = dzb.astype(BF16)
        dp_ref[:, O_BG:O_BG + SEG] = (d_bout * r["yconv"]).astype(BF16)
        d_y = d_bout * r["bg"]
        gvec_ref[V_CB:V_CB + 1, :] += jnp.sum(d_y, axis=0, keepdims=True)
        gvec_ref[V_CW0:V_CW0 + 1, :] += jnp.sum(d_y * r["cx2"], axis=0, keepdims=True)
        gvec_ref[V_CW1:V_CW1 + 1, :] += jnp.sum(d_y * r["cx1"], axis=0, keepdims=True)
        gvec_ref[V_CW2:V_CW2 + 1, :] += jnp.sum(d_y * r["cx"], axis=0, keepdims=True)
        extdy[0:R, :] = d_y
        extdy[R:R + HALO, :] = cdy[...]
        d_cx = (cv[C_CW2:C_CW2 + 1, :] * d_y + cv[C_CW1:C_CW1 + 1, :] * extdy[pl.ds(1, R), :]
                + cv[C_CW0:C_CW0 + 1, :] * extdy[pl.ds(2, R), :])
        cdy[...] = d_y[0:HALO]
        dp_ref[:, O_CG:O_CG + SEG] = (d_cx * r["xb"]).astype(BF16)
        dp_ref[:, O_XB:O_XB + SEG] = (d_cx * r["cg"]).astype(BF16)

        d_cout, dzc = silu_bwd(dc, r["c_out"], r["zc"], r["sc"])
        dp_ref[:, O_ZC:O_ZC + SEG] = dzc.astype(BF16)
        gvec_ref[V_PS:V_PS + 1, :] += jnp.sum(d_cout * r["pw"], axis=0, keepdims=True)
        d_pw = (d_cout * cv[C_PS:C_PS + 1, :]).astype(BF16)
        dpool, scaled = [], []
        for gi, w in enumerate(POOL_WINDOWS):
            lo = gi * POOL_GROUP
            dpw_g = d_pw[:, lo:lo + POOL_GROUP]
            gwpool_ref[lo:lo + POOL_GROUP, :] += _dot_tn(r["pooled"][gi], dpw_g)
            dpg = _dot_nt(dpw_g, wpool_ref[gi])
            dpool.append(dpg)
            scaled.append((dpg * r["invs"][gi]).astype(BF16))
        q = jnp.concatenate(scaled, axis=1)
        q_ext = jnp.concatenate([q, cq[...], jnp.zeros((CHUNK - HALO, SEG), BF16)], axis=0)
        for gi, w in enumerate(POOL_WINDOWS):
            lo = gi * POOL_GROUP
            acc = _dot(bandt_ref[gi], q_ext[:, lo:lo + POOL_GROUP])
            dp_ref[:, O_XC + lo:O_XC + lo + POOL_GROUP] = (acc - dpool[gi]).astype(BF16)
        cq[...] = q[0:HALO]

        @pl.when(i == nb - 1)
        def _():
            rr = lax.broadcasted_iota(jnp.int32, gwc_ref.shape, 0) % CHUNK
            cc = lax.broadcasted_iota(jnp.int32, gwc_ref.shape, 1)
            gwc_ref[...] = jnp.where(cc <= rr, gwc_ref[...], 0.0)
            acc = bsacc[...]
            hi = acc.astype(BF16)
            lo_ = (acc - hi.astype(F32)).astype(BF16)
            sel = (lax.broadcasted_iota(jnp.int32, (SEG, LANES), 0) // HEAD
                   == lax.broadcasted_iota(jnp.int32, (SEG, LANES), 1)).astype(BF16)
            gbs_ref[...] = _dot(hi, sel) + _dot(lo_, sel)

    row = lambda w: pl.BlockSpec((R, w), lambda i: (nb - 1 - i, 0))
    consts = [lw["cvec"], lw["bsb"], lw["wcat"], lw["wcatt"], lw["wpool"], _pool_bands(R, False), _pool_bands(R, True),
              lw["wpa"], lw["wpb"], lw["wpc"], lw["wo"]]
    acc_shapes = [(GROUPS * CHUNK, CHUNK), (CHUNK, LANES), (SEG, POOL_GROUP), (8, SEG)]
    row_widths = [N, 3 * SEG, D, 3 * D]
    return pl.pallas_call(
        body, name=name, grid=(nb,),
        in_specs=([row(N)] + _halo_specs(R, nb, True) + [row(D), row(D), row(D), row(D)]
                  + [_const_spec(c.shape) for c in consts]),
        out_specs=[row(w) for w in row_widths] + [_const_spec(s) for s in acc_shapes],
        out_shape=[SDS((T, w), BF16) for w in row_widths] + [SDS(s, F32) for s in acc_shapes],
        scratch_shapes=[pltpu.VMEM((HALO + R, SEG), F32)] * 2
        + [pltpu.VMEM((HALO, SEG), F32), pltpu.VMEM((HALO, SEG), BF16), pltpu.VMEM((CHUNK, SEG), F32)],
        compiler_params=_cparams(("arbitrary",)),
    )(p, p, p, p, dxo, ya, yb, yc, *consts)


def _proj_wgrad(acts, merged, dys, dxo, name, riders=()):
    T, D = dxo.shape
    bk = _pick(T, 512, 16)

    def body(a_ref, m_ref, dy_ref, dxo_ref, gwpa_ref, gwpb_ref, gwpc_ref, gwo_ref):
        @pl.when(pl.program_id(0) == 0)
        def _():
            for ref in (gwpa_ref, gwpb_ref, gwpc_ref, gwo_ref):
                _zero(ref)

        gwo_ref[...] += _dot_tn(m_ref[...], dxo_ref[...].astype(BF16))
        for k, ref in enumerate((gwpa_ref, gwpb_ref, gwpc_ref)):
            ref[...] += _dot_tn(a_ref[:, k * SEG:(k + 1) * SEG], dy_ref[:, k * D:(k + 1) * D])

    row = lambda w: pl.BlockSpec((bk, w), lambda i: (i, 0))
    shapes = [(SEG, D), (SEG, D), (SEG, D), (D, D)]
    return _host_call(
        body, name, (T // bk,), riders,
        inputs=[acts, merged, dys, dxo],
        in_specs=[row(3 * SEG), row(D), row(3 * D), row(D)],
        out_specs=[_const_spec(s) for s in shapes], out_shape=[SDS(s, F32) for s in shapes],
        scratch_shapes=[])


def _inproj_token_blocks(T):
    return T // _pick(T, INPROJ_ROWS, 16)


def _inproj_bwd_x(dp, win_t, x, norm_g, dxo, name, riders=(), blocks=None, fill=None):
    T, D = x.shape
    N = dp.shape[1]
    bt = _pick(T, INPROJ_ROWS, 16)
    bk = _pick(N, 1536, LANES)
    nk = N // bk
    b0, nblk = blocks if blocks else (0, T // bt)

    def compute(dp_ref, w_ref, x_ref, g_ref, dxo_ref, *rest):
        dx_ref, dg_ref, acc_ref = rest[-3:]
        i, k = pl.program_id(0), pl.program_id(1)

        @pl.when((i == 0) & (k == 0))
        def _():
            _zero(dg_ref)

        @pl.when(k == 0)
        def _():
            _zero(acc_ref)

        acc_ref[...] += _dot(dp_ref[...], w_ref[...])

        @pl.when(k == nk - 1)
        def _():
            dh = acc_ref[...]
            xv = x_ref[...]
            rstd = lax.rsqrt(jnp.mean(xv * xv, axis=-1, keepdims=True) + RMS_EPS)
            xhat = xv * rstd
            dg_ref[0:1, :] += jnp.sum(dh * xhat, axis=0, keepdims=True)
            dxn = dh * g_ref[...]
            dx_ref[...] = dxo_ref[...] + rstd * (dxn - xhat * jnp.mean(dxn * xhat, axis=-1, keepdims=True))

    rows = pl.BlockSpec((bt, D), lambda i, k: (i + b0, 0))
    return _host_call(
        compute, name, (nblk, nk), riders,
        inputs=[dp, win_t, x, norm_g.reshape(1, D), dxo] + ([] if fill is None else [fill]),
        in_specs=[pl.BlockSpec((bt, bk), lambda i, k: (i + b0, k)),
                  pl.BlockSpec((bk, D), lambda i, k: (k, 0)),
                  rows,
                  pl.BlockSpec((1, D), lambda i, k: (0, 0)),
                  rows] + ([] if fill is None else [_ANY]),
        out_specs=[rows, pl.BlockSpec((8, D), lambda i, k: (0, 0))],
        out_shape=[SDS((T, D), F32), SDS((8, D), F32)],
        scratch_shapes=[pltpu.VMEM((bt, D), F32)],
        aliases={} if fill is None else {5: 0})


def _inproj_bwd_w(dp, h, name, riders=()):
    T, N = dp.shape
    D = h.shape[1]
    bn = _pick(N, 1920, LANES)
    bk = _pick(T, 1024, 16)
    nk = T // bk

    def compute(dp_ref, h_ref, o_ref):
        @pl.when(pl.program_id(1) == 0)
        def _():
            _zero(o_ref)

        o_ref[...] += _dot_tn(dp_ref[...], h_ref[...])

    return _host_call(
        compute, name, (N // bn, nk), riders,
        inputs=[dp, h],
        in_specs=[pl.BlockSpec((bk, bn), lambda j, k: (k, j)), pl.BlockSpec((bk, D), lambda j, k: (k, 0))],
        out_specs=[pl.BlockSpec((bn, D), lambda j, k: (j, 0))],
        out_shape=[SDS((N, D), F32)],
        scratch_shapes=[])


def _chip_peer(x, y, j):
    px = (1 - x) if (j >> 1) else x
    py = (1 - y) if (j & 1) else y
    return px, py


def _blk(ref, kind, k, n):
    if kind == "rows":
        return ref.at[pl.ds(pl.multiple_of(k * n, 8), n)]
    return ref.at[:, pl.ds(pl.multiple_of(k * n, LANES), n)]


class _Exchange:
    def __init__(self, srcs, out_shapes, n_sems, build, alias=False):
        self.srcs, self.out_shapes, self.n_sems, self.build, self.alias = list(srcs), list(out_shapes), n_sems, build, alias


def _rider_plan(riders):
    inputs = [s for e in riders for s in e.srcs]
    out_shapes = [o for e in riders for o in e.out_shapes]
    sems = [pltpu.SemaphoreType.DMA((e.n_sems,)) for e in riders for _ in range(2)]

    def copies(in_refs, out_refs, sem_refs):
        cps, i, o = [], 0, 0
        for k, e in enumerate(riders):
            ni, no = len(e.srcs), len(e.out_shapes)
            cps += e.build(in_refs[i:i + ni], out_refs[o:o + no], sem_refs[2 * k], sem_refs[2 * k + 1])
            i, o = i + ni, o + no
        return cps

    return inputs, out_shapes, sems, copies


_ANY = pl.BlockSpec(memory_space=pl.ANY)


def _host_call(compute, name, grid, riders, inputs, in_specs, out_specs, out_shape, scratch_shapes, aliases=None):
    r_in, r_out, r_sems, copies = _rider_plan(riders)
    ni, no, ns = len(inputs), len(out_shape), len(scratch_shapes)

    def body(*refs):
        ins, rins = refs[:ni], refs[ni:ni + len(r_in)]
        outs = refs[ni + len(r_in):ni + len(r_in) + no]
        routs = refs[ni + len(r_in) + no:ni + len(r_in) + no + len(r_out)]
        scr = refs[ni + len(r_in) + no + len(r_out):]
        first = functools.reduce(lambda a, b: a & b, [pl.program_id(d) == 0 for d in range(len(grid))])
        last = functools.reduce(lambda a, b: a & b, [pl.program_id(d) == grid[d] - 1 for d in range(len(grid))])
        if riders:
            @pl.when(first)
            def _():
                for cp in copies(rins, routs, scr[ns:]):
                    cp.start()

        compute(*ins, *outs, *scr[:ns])

        if riders:
            @pl.when(last)
            def _():
                for cp in copies(rins, routs, scr[ns:]):
                    cp.wait()

    res = pl.pallas_call(
        body, name=name, grid=grid,
        in_specs=list(in_specs) + [_ANY] * len(r_in),
        out_specs=list(out_specs) + [_ANY] * len(r_out),
        out_shape=list(out_shape) + r_out,
        scratch_shapes=list(scratch_shapes) + r_sems,
        input_output_aliases=aliases or {},
        compiler_params=_cparams(("arbitrary",) * len(grid)),
    )(*inputs, *r_in)
    return res[:no], _split_riders(riders, res[no:])


def _split_riders(riders, flat):
    out, o = [], 0
    for e in riders:
        out.append(list(flat[o:o + len(e.out_shapes)]))
        o += len(e.out_shapes)
    return out


def _run_exchange(ex, name):
    n_in, n_out = len(ex.srcs), len(ex.out_shapes)

    def body(*refs):
        cps = ex.build(refs[:n_in], refs[n_in:n_in + n_out], refs[n_in + n_out], refs[n_in + n_out + 1])
        for cp in cps:
            cp.start()
        for cp in cps:
            cp.wait()

    return pl.pallas_call(
        body, name=name,
        in_specs=[_ANY] * n_in, out_specs=[_ANY] * n_out, out_shape=ex.out_shapes,
        input_output_aliases={i: i for i in range(n_in)} if ex.alias else {},
        scratch_shapes=[pltpu.SemaphoreType.DMA((ex.n_sems,)), pltpu.SemaphoreType.DMA((ex.n_sems,))],
        compiler_params=pltpu.CompilerParams(has_side_effects=True),
    )(*ex.srcs)


def _gather_sizes(shards, kinds):
    sizes = [s.shape[0] if k == "rows" else s.shape[1] for s, k in zip(shards, kinds)]
    fulls = [SDS((s.shape[0] * N_DEV,) + s.shape[1:], s.dtype) if k == "rows"
             else SDS((s.shape[0], s.shape[1] * N_DEV), s.dtype) for s, k in zip(shards, kinds)]
    return sizes, fulls


def _gather_direct(shards, kinds):
    n = len(shards)
    sizes, fulls = _gather_sizes(shards, kinds)

    def build(ins, outs, send_sems, recv_sems):
        x, y, c = _me()
        cps = []
        for a in range(n):
            mine = _blk(outs[a], kinds[a], 4 * x + 2 * y + c, sizes[a])
            cps.append(pltpu.make_async_copy(ins[a], mine, send_sems.at[5 * a + 4]))
            for j in range(N_CHIP):
                to = (x, y, 1 - c) if j == 0 else (*_chip_peer(x, y, j), c)
                cps.append(pltpu.make_async_remote_copy(
                    src_ref=ins[a], dst_ref=mine, send_sem=send_sems.at[5 * a + j], recv_sem=recv_sems.at[5 * a + j],
                    device_id=to, device_id_type=MESH))
        return cps

    return _Exchange(shards, fulls, 5 * n, build)


def _gather_everywhere(shards, kinds):
    n = len(shards)
    sizes, fulls = _gather_sizes(shards, kinds)

    def build(ins, outs, send_sems, recv_sems):
        x, y, c = _me()
        cps = []
        for a in range(n):
            mine = _blk(outs[a], kinds[a], 4 * x + 2 * y + c, sizes[a])
            cps.append(pltpu.make_async_copy(ins[a], mine, send_sems.at[N_DEV * a]))
            for d in range(1, N_DEV):
                to = ((1 - x) if d & 4 else x, (1 - y) if d & 2 else y, (1 - c) if d & 1 else c)
                cps.append(pltpu.make_async_remote_copy(
                    src_ref=ins[a], dst_ref=mine, send_sem=send_sems.at[N_DEV * a + d],
                    recv_sem=recv_sems.at[N_DEV * a + d], device_id=to, device_id_type=MESH))
        return cps

    return _Exchange(shards, fulls, N_DEV * n, build)


def _gather_forward(fulls, kinds, sizes):
    n = len(fulls)

    def build(ins, outs, send_sems, recv_sems):
        x, y, c = _me()
        cps = []
        for a in range(n):
            for j in (1, 2, 3):
                px, py = _chip_peer(x, y, j)
                k = 4 * px + 2 * py + c
                cps.append(pltpu.make_async_remote_copy(
                    src_ref=_blk(ins[a], kinds[a], k, sizes[a]), dst_ref=_blk(outs[a], kinds[a], k, sizes[a]),
                    send_sem=send_sems.at[3 * a + j - 1], recv_sem=recv_sems.at[3 * a + j - 1],
                    device_id=(x, y, 1 - c), device_id_type=MESH))
        return cps

    return _Exchange(fulls, [SDS(f.shape, f.dtype) for f in fulls], 3 * n, build, alias=True)


def _sibling_exchange(grads, kinds, sizes):
    n = len(grads)

    def blk_shape(a):
        g = grads[a]
        return (sizes[a],) + g.shape[1:] if kinds[a] == "rows" else (g.shape[0], sizes[a])

    def build(ins, outs, send_sems, recv_sems):
        x, y, c = _me()
        cps = []
        for a in range(n):
            for q in range(N_CHIP):
                cps.append(pltpu.make_async_remote_copy(
                    src_ref=_blk(ins[a], kinds[a], 2 * q + (1 - c), sizes[a]), dst_ref=outs[a].at[q],
                    send_sem=send_sems.at[N_CHIP * a + q], recv_sem=recv_sems.at[N_CHIP * a + q],
                    device_id=(x, y, 1 - c), device_id_type=MESH))
        return cps

    return _Exchange(grads, [SDS((N_CHIP,) + blk_shape(a), F32) for a in range(n)], N_CHIP * n, build)


def _chip_partial(g, r1, kind, size, cidx, name):
    if kind == "rows":
        rows, cols = size, g.shape[1]
        g3 = g.reshape(N_DEV, rows, cols)
        rb = _pick(rows, 512, 16)
        g_spec = pl.BlockSpec((1, rb, cols), lambda q, j, c: (2 * q + c[0], j, 0))
        grid = (N_CHIP, rows // rb)
        blk = (1, rb, cols)
        imap = lambda q, j, c: (q, j, 0)
    else:
        rows, cols = g.shape[0], size
        g3 = g
        g_spec = pl.BlockSpec((rows, cols), lambda q, j, c: (0, 2 * q + c[0]))
        grid = (N_CHIP, 1)
        blk = (1, rows, cols)
        imap = lambda q, j, c: (q, 0, 0)

    def body(c_ref, g_ref, r_ref, p_ref, pb_ref):
        s = g_ref[...].reshape(blk) + r_ref[...]
        p_ref[...] = s
        pb_ref[...] = s.astype(BF16)

    return pl.pallas_call(
        body, name=name,
        grid_spec=pltpu.PrefetchScalarGridSpec(
            num_scalar_prefetch=1, grid=grid,
            in_specs=[g_spec, pl.BlockSpec(blk, imap)],
            out_specs=[pl.BlockSpec(blk, imap), pl.BlockSpec(blk, imap)]),
        out_shape=[SDS((N_CHIP, rows, cols), F32), SDS((N_CHIP, rows, cols), BF16)],
        compiler_params=_cparams(("arbitrary", "arbitrary")),
    )(cidx, g3, r1)


def _chip_exchange(parts):
    n = len(parts)
    m = N_CHIP - 1

    def build(ins, outs, send_sems, recv_sems):
        x, y, c = _me()
        cps = []
        for a in range(n):
            for j in (1, 2, 3):
                px, py = _chip_peer(x, y, j)
                cps.append(pltpu.make_async_remote_copy(
                    src_ref=ins[a].at[2 * px + py], dst_ref=outs[a].at[j - 1], send_sem=send_sems.at[m * a + j - 1],
                    recv_sem=recv_sems.at[m * a + j - 1], device_id=(px, py, c), device_id_type=MESH))
        return cps

    return _Exchange(parts, [SDS((m,) + p.shape[1:], BF16) for p in parts], m * n, build)


def _grad_total(part, r2, qidx, name):
    _, rows, cols = part.shape
    rb = _pick(rows, 512, 16)

    def body(q_ref, p_ref, r_ref, o_ref):
        s = p_ref[0]
        for j in range(N_CHIP - 1):
            s = s + r_ref[j].astype(F32)
        o_ref[...] = s

    return pl.pallas_call(
        body, name=name,
        grid_spec=pltpu.PrefetchScalarGridSpec(
            num_scalar_prefetch=1, grid=(rows // rb,),
            in_specs=[pl.BlockSpec((1, rb, cols), lambda i, q: (q[0], i, 0)),
                      pl.BlockSpec((N_CHIP - 1, rb, cols), lambda i, q: (0, i, 0))],
            out_specs=pl.BlockSpec((rb, cols), lambda i, q: (i, 0))),
        out_shape=SDS((rows, cols), F32),
        compiler_params=_cparams(("arbitrary",)),
    )(qidx, part, r2)


def _all_reduce_small(pack, name):
    rows = pack.shape[0]
    rs = rows // N_DEV
    assert rs * N_DEV == rows and rs % 8 == 0

    def body(x_ref, o_ref, rbuf, red, send1, recv1, send2, recv2):
        x, y, c = _me()
        me = 4 * x + 2 * y + c

        def peer(d):
            px = (1 - x) if (d >> 2) & 1 else x
            py = (1 - y) if (d >> 1) & 1 else y
            pc = (1 - c) if d & 1 else c
            return px, py, pc

        def sl(ref, k):
            return ref.at[pl.ds(pl.multiple_of(k * rs, 8), rs)]

        phase1 = []
        for d in range(1, N_DEV):
            px, py, pc = peer(d)
            phase1.append(pltpu.make_async_remote_copy(
                src_ref=sl(x_ref, 4 * px + 2 * py + pc), dst_ref=rbuf.at[d], send_sem=send1.at[d], recv_sem=recv1.at[d],
                device_id=(px, py, pc), device_id_type=MESH))
        for cp in phase1:
            cp.start()
        acc = sl(x_ref, me)[...]
        for cp in phase1:
            cp.wait()
        for d in range(1, N_DEV):
            acc = acc + rbuf[d]
        red[...] = acc
        sl(o_ref, me)[...] = acc
        phase2 = []
        for d in range(1, N_DEV):
            px, py, pc = peer(d)
            phase2.append(pltpu.make_async_remote_copy(
                src_ref=red, dst_ref=sl(o_ref, me), send_sem=send2.at[d], recv_sem=recv2.at[d],
                device_id=(px, py, pc), device_id_type=MESH))
        for cp in phase2:
            cp.start()
        for cp in phase2:
            cp.wait()

    vm = pl.BlockSpec(memory_space=pltpu.VMEM)
    return pl.pallas_call(
        body, name=name, in_specs=[vm], out_specs=vm, out_shape=SDS(pack.shape, F32),
        scratch_shapes=[pltpu.VMEM((N_DEV, rs, LANES), F32), pltpu.VMEM((rs, LANES), F32),
                        pltpu.SemaphoreType.DMA((N_DEV,)), pltpu.SemaphoreType.DMA((N_DEV,)),
                        pltpu.SemaphoreType.DMA((N_DEV,)), pltpu.SemaphoreType.DMA((N_DEV,))],
        compiler_params=_cparams(None, has_side_effects=True),
    )(pack)


def _adamw(w, g, m, v, name):
    rows, cols = w.shape
    rb = rows if rows * cols * 4 <= ADAMW_WHOLE_BYTES else _pick(rows, 256, 8)
    c1 = 1.0 / (1.0 - ADAM_B1 ** ADAM_STEP)
    c2 = 1.0 / (1.0 - ADAM_B2 ** ADAM_STEP)

    def body(w_ref, g_ref, m_ref, v_ref, d_ref, mo_ref, vo_ref):
        gv = g_ref[...]
        mn = ADAM_B1 * m_ref[...] + (1.0 - ADAM_B1) * gv
        vn = ADAM_B2 * v_ref[...] + (1.0 - ADAM_B2) * (gv * gv)
        mo_ref[...] = mn
        vo_ref[...] = vn
        d_ref[...] = -ADAM_LR * ((mn * c1) / (jnp.sqrt(vn * c2) + ADAM_EPS) + ADAM_WD * w_ref[...])

    spec = pl.BlockSpec((rb, cols), lambda i: (i, 0))
    return pl.pallas_call(
        body, name=name, grid=(rows // rb,),
        in_specs=[spec] * 4, out_specs=[spec] * 3, out_shape=[SDS((rows, cols), F32)] * 3,
        compiler_params=_cparams(("arbitrary",)),
    )(w, g, m, v)


def _pad_rows(a, mult=8):
    r = (-a.shape[0]) % mult
    return a if r == 0 else jnp.pad(a, ((0, r), (0, 0)))


def _as_lanes(a):
    flat = a.reshape(-1)
    pad = (-flat.shape[0]) % (8 * LANES)
    if pad:
        flat = jnp.pad(flat, (0, pad))
    return flat.reshape(-1, LANES)


def kernel(x, norm_g, w_in, ln_g, ln_b, w_s, b_s, conv_w, conv_b, w_pool, pool_scale, w_pa, w_pb, w_pc, w_o, final_g, loss_target, m_norm_g, m_w_in, m_ln_g, m_ln_b, m_w_s, m_b_s, m_conv_w, m_conv_b, m_w_pool, m_pool_scale, m_w_pa, m_w_pb, m_w_pc, m_w_o, m_final_g, v_norm_g, v_w_in, v_ln_g, v_ln_b, v_w_s, v_b_s, v_conv_w, v_conv_b, v_w_pool, v_pool_scale, v_w_pa, v_w_pb, v_w_pc, v_w_o, v_final_g):
    L = w_in.shape[0]
    D = x.shape[-1]
    n_loc = w_in.shape[2]
    pc_loc = w_pa.shape[2]
    x0 = x[0]
    target = loss_target[0]
    xi, yi, ci = _me()
    cidx = jnp.reshape(ci, (1,)).astype(jnp.int32)
    qidx = jnp.reshape(2 * xi + yi, (1,)).astype(jnp.int32)

    kinds5 = ["rows", "cols", "cols", "cols", "rows"]

    def layer_shards(l):
        return [w_in[l].T.astype(BF16), w_pa[l].astype(BF16), w_pb[l].astype(BF16), w_pc[l].astype(BF16),
                w_o[l].astype(BF16)]

    def gathered(direct, shards, kinds, l):
        sizes, _ = _gather_sizes(shards, kinds)
        return _run_exchange(_gather_forward(direct, kinds, sizes), f"weights_forward_{l}")

    cw_loc = _pad_rows(conv_w.reshape(L * CONV_TAPS, -1))
    cw_loc = jnp.pad(cw_loc, ((0, 0), (0, LANES - cw_loc.shape[1])))
    causal = jnp.tril(jnp.ones((CHUNK, CHUNK), dtype=bool))

    chip_order = jnp.stack([2 * xi + yi] + [2 * px + py for px, py in (_chip_peer(xi, yi, j) for j in (1, 2, 3))])
    sh0 = layer_shards(0)
    rest0, krest0 = sh0[1:] + [cw_loc], kinds5[1:] + ["rows"]
    (p0, h0, win_t0), delivered = _inproj_gathering(x0, norm_g[0], sh0[0], chip_order.astype(jnp.int32), "inproj_fwd_0",
                                                    [_gather_everywhere(rest0, krest0)])
    rest0_full = delivered[0]
    cw_all = rest0_full[-1].reshape(N_DEV, -1, LANES)[:, :L * CONV_TAPS, :conv_w.shape[2]]
    conv_w_full = jnp.transpose(cw_all, (1, 0, 2)).reshape(L, CONV_TAPS, -1)

    def make_layer(l, full5):
        win_t, wpa, wpb, wpc, wo = full5
        wm = jnp.where(causal, w_s[l], 0.0)
        cvec = jnp.concatenate([ln_g[l][None], ln_b[l][None], conv_w_full[l], conv_b[l][None], pool_scale[l][None],
                                jnp.zeros((C_ROWS - 7, SEG), F32)], axis=0)
        return dict(
            win_t=win_t, wpa=wpa, wpb=wpb, wpc=wpc, wo=wo, cvec=cvec,
            bsb=jnp.repeat(b_s[l].T, HEAD, axis=1),
            wcat=jnp.transpose(wm, (1, 0, 2)).reshape(CHUNK, GROUPS * CHUNK).astype(BF16),
            wcatt=jnp.transpose(wm, (2, 0, 1)).reshape(CHUNK, GROUPS * CHUNK).astype(BF16),
            wpool=w_pool[l].astype(BF16))

    layers = [make_layer(0, [win_t0] + list(rest0_full[:4]))]
    xs, saved = [x0], []
    for l in range(L):
        lw = layers[l]
        if l == 0:
            p, h = p0, h0
        else:
            (p, h), _ = _inproj(xs[-1], norm_g[l], lw["win_t"], f"inproj_fwd_{l}")
        nxt = layer_shards(l + 1) if l + 1 < L else None
        (xn, ya, yb, yc), delivered = _mix_fwd(p, xs[-1], lw, f"mix_fwd_{l}", [_gather_direct(nxt, kinds5)] if nxt else [])
        if nxt:
            layers.append(make_layer(l + 1, gathered(delivered[0], nxt, kinds5, l + 1)))
        saved.append((p, h, ya, yb, yc))
        xs.append(xn)
    dx, loss_acc, dfg_acc = _loss_head(xs[-1], final_g, target, "loss_head")

    rs_sizes = [n_loc, pc_loc, pc_loc, pc_loc, w_o.shape[1]]
    await_sibling, await_chips = [], []
    partial_of, from_chips = {}, {}
    serial = [0]

    def riders_now():
        riders, plan = [], []
        for grp in await_chips:
            riders.append(_chip_exchange([partial_of[t][1] for t, _, _, _ in grp]))
            plan.append(("chips", grp))
        for grp in await_sibling:
            riders.append(_sibling_exchange([g for _, g, _, _ in grp], [k for _, _, k, _ in grp], [s for _, _, _, s in grp]))
            plan.append(("sibling", grp))
        del await_chips[:], await_sibling[:]
        return riders, plan

    def absorb(plan, delivered):
        for (what, grp), res in zip(plan, delivered):
            for (t, g, k, s), r in zip(grp, res):
                if what == "chips":
                    from_chips[t] = r
                else:
                    partial_of[t] = _chip_partial(g, r, k, s, cidx, f"grad_chip_partial_{t[0]}_{t[1]}")
            if what == "sibling":
                await_chips.append(grp)

    small = [None] * L
    for l in reversed(range(L)):
        lw = layers[l]
        p, h, ya, yb, yc = saved[l]
        dp, acts, merged, dys, gwc, gbs, gwpool, gvec = _mix_bwd(p, dx, ya, yb, yc, lw, f"mix_bwd_{l}")
        riders, plan = riders_now()
        (gwpa, gwpb, gwpc, gwo), delivered = _proj_wgrad(acts, merged, dys, dx, f"proj_wgrad_{l}", riders)
        absorb(plan, delivered)
        await_sibling.append([((l, a), g, kinds5[a], rs_sizes[a]) for a, g in ((1, gwpa), (2, gwpb), (3, gwpc), (4, gwo))])

        def bwd_x(dxo, pieces):
            nt = _inproj_token_blocks(dxo.shape[0])
            pieces = min(pieces, nt)
            done, dng, b0 = None, None, 0
            for k in range(pieces):
                cnt = (nt - b0) // (pieces - k)
                riders, plan = riders_now()
                (done, dng_k), delivered = _inproj_bwd_x(dp, lw["win_t"], xs[l], norm_g[l], dxo, f"inproj_bwd_x_{l}_{k}",
                                                         riders, blocks=(b0, cnt), fill=done)
                absorb(plan, delivered)
                dng = dng_k if dng is None else dng + dng_k
                b0 += cnt
            return done, dng

        def bwd_w():
            riders, plan = riders_now()
            (gwin_t,), delivered = _inproj_bwd_w(dp, h, f"inproj_bwd_w_{l}", riders)
            absorb(plan, delivered)
            await_sibling.append([((l, 0), gwin_t, kinds5[0], rs_sizes[0])])

        if l == L - 1:
            dx, dng = bwd_x(dx, 1)
            bwd_w()
        else:
            bwd_w()
            dx, dng = bwd_x(dx, 2 if l == 0 else 1)
        small[l] = dict(norm_g=dng[0], ln_g=gvec[V_LNG], ln_b=gvec[V_LNB], w_s=gwc, b_s=gbs, conv_w=gvec[V_CW0:V_CW0 + 3],
                        conv_b=gvec[V_CB], w_pool=gwpool, pool_scale=gvec[V_PS])
    while await_sibling or await_chips:
        riders, plan = riders_now()
        delivered = []
        for ex in riders:
            delivered.append(_run_exchange(ex, f"grad_exchange_tail_{serial[0]}"))
            serial[0] += 1
        absorb(plan, delivered)
    grad_x = dx[None]
    big_grads = []
    for l in range(L):
        tot = [_grad_total(partial_of[(l, a)][0], from_chips[(l, a)], qidx, f"grad_total_{l}_{a}") for a in range(5)]
        big_grads.append([tot[0].T,
                          tot[1].reshape(SEG, pc_loc), tot[2].reshape(SEG, pc_loc), tot[3].reshape(SEG, pc_loc),
                          tot[4]])

    names = ["norm_g", "ln_g", "ln_b", "w_s", "b_s", "conv_w", "conv_b", "w_pool", "pool_scale"]
    pieces = [_as_lanes(jnp.stack([small[l][nm] for l in range(L)])) for nm in names]
    pieces += [_as_lanes(dfg_acc[0]), loss_acc]
    sizes = [pc.shape[0] for pc in pieces]
    pack = jnp.concatenate(pieces, axis=0)
    pack = _pad_rows(pack, 8 * N_DEV)
    red = _all_reduce_small(pack, "small_grads_all_reduce")
    offs = [0]
    for s in sizes:
        offs.append(offs[-1] + s)

    def unpack(i, shape):
        n = math.prod(shape)
        return red[offs[i]:offs[i + 1]].reshape(-1)[:n].reshape(shape)

    g_norm_g = unpack(0, (L, D))
    g_ln_g = unpack(1, (L, SEG))
    g_ln_b = unpack(2, (L, SEG))
    g_w_s = unpack(3, (L, GROUPS, CHUNK, CHUNK))
    g_b_s = jnp.transpose(unpack(4, (L, CHUNK, LANES))[:, :, :GROUPS], (0, 2, 1))
    g_conv_w_full = unpack(5, (L, CONV_TAPS, SEG))
    g_conv_b = unpack(6, (L, SEG))
    g_w_pool = unpack(7, (L, len(POOL_WINDOWS), POOL_GROUP, POOL_GROUP))
    g_pool_scale = unpack(8, (L, SEG))
    g_final_g = unpack(9, (D,))
    loss = red[offs[10], 0]
    dev = 4 * xi + 2 * yi + ci
    g_conv_w = lax.dynamic_slice_in_dim(g_conv_w_full, dev * conv_w.shape[2], conv_w.shape[2], axis=2)

    g_w_in = jnp.stack([big_grads[l][0] for l in range(L)])
    g_w_pa = jnp.stack([big_grads[l][1] for l in range(L)])
    g_w_pb = jnp.stack([big_grads[l][2] for l in range(L)])
    g_w_pc = jnp.stack([big_grads[l][3] for l in range(L)])
    g_w_o = jnp.stack([big_grads[l][4] for l in range(L)])

    grads = dict(norm_g=g_norm_g, w_in=g_w_in, ln_g=g_ln_g, ln_b=g_ln_b, w_s=g_w_s, b_s=g_b_s, conv_w=g_conv_w,
                 conv_b=g_conv_b, w_pool=g_w_pool, pool_scale=g_pool_scale, w_pa=g_w_pa, w_pb=g_w_pb, w_pc=g_w_pc,
                 w_o=g_w_o, final_g=g_final_g)
    weights = dict(norm_g=norm_g, w_in=w_in, ln_g=ln_g, ln_b=ln_b, w_s=w_s, b_s=b_s, conv_w=conv_w, conv_b=conv_b,
                   w_pool=w_pool, pool_scale=pool_scale, w_pa=w_pa, w_pb=w_pb, w_pc=w_pc, w_o=w_o, final_g=final_g)
    ms = dict(norm_g=m_norm_g, w_in=m_w_in, ln_g=m_ln_g, ln_b=m_ln_b, w_s=m_w_s, b_s=m_b_s, conv_w=m_conv_w,
              conv_b=m_conv_b, w_pool=m_w_pool, pool_scale=m_pool_scale, w_pa=m_w_pa, w_pb=m_w_pb, w_pc=m_w_pc,
              w_o=m_w_o, final_g=m_final_g)
    vs = dict(norm_g=v_norm_g, w_in=v_w_in, ln_g=v_ln_g, ln_b=v_ln_b, w_s=v_w_s, b_s=v_b_s, conv_w=v_conv_w,
              conv_b=v_conv_b, w_pool=v_w_pool, pool_scale=v_pool_scale, w_pa=v_w_pa, w_pb=v_w_pb, w_pc=v_w_pc,
              w_o=v_w_o, final_g=v_final_g)
    order = ["norm_g", "w_in", "ln_g", "ln_b", "w_s", "b_s", "conv_w", "conv_b", "w_pool", "pool_scale", "w_pa", "w_pb",
             "w_pc", "w_o", "final_g"]

    delta, new_m, new_v = {}, {}, {}
    big = ["w_in", "w_pa", "w_pb", "w_pc", "w_o"]
    for nm in big:
        shp = weights[nm].shape
        two = lambda a: a.reshape(-1, shp[-1])
        d, mn, vn = _adamw(two(weights[nm]), two(grads[nm]), two(ms[nm]), two(vs[nm]), f"adamw_{nm}")
        delta[nm], new_m[nm], new_v[nm] = d.reshape(shp), mn.reshape(shp), vn.reshape(shp)
    rest = [nm for nm in order if nm not in big]
    cat = lambda src: jnp.concatenate([_as_lanes(src[nm]) for nm in rest], axis=0)
    d, mn, vn = _adamw(cat(weights), cat(grads), cat(ms), cat(vs), "adamw_small")
    off = 0
    for nm in rest:
        shp = weights[nm].shape
        n = math.prod(shp)
        rows = _as_lanes(weights[nm]).shape[0]
        cut = lambda a: a[off:off + rows].reshape(-1)[:n].reshape(shp)
        delta[nm], new_m[nm], new_v[nm] = cut(d), cut(mn), cut(vn)
        off += rows

    return (loss, grad_x, *[grads[nm] for nm in order], *[delta[nm] for nm in order],
            *[new_m[nm] for nm in order], *[new_v[nm] for nm in order])
```

```python
import functools
import math

import numpy as np
import jax
import jax.numpy as jnp
from jax import lax
from jax.experimental import pallas as pl
from jax.experimental.pallas import tpu as pltpu

F32 = jnp.float32
BF16 = jnp.bfloat16
SDS = jax.ShapeDtypeStruct
MESH = pl.DeviceIdType.MESH

SEG = 512
CHUNK = 128
GROUPS = 8
HEAD = SEG // GROUPS
POOL_WINDOWS = (2, 4, 8, 16)
POOL_GROUP = SEG // len(POOL_WINDOWS)
CONV_TAPS = 3
HALO = 16
RMS_EPS = 1e-6
LN_EPS = 1e-5
ADAM_LR, ADAM_B1, ADAM_B2, ADAM_EPS, ADAM_WD, ADAM_STEP = 0.001, 0.9, 0.999, 1e-08, 0.01, 10

O_U, O_V, O_ZA, O_XB, O_BG, O_CG, O_ZB, O_XC, O_ZC, O_G = (SEG * i for i in range(10))

N_DEV = 8
N_CHIP = 4
LANES = 128
VMEM_LIMIT = 48 * 1024 * 1024
ADAMW_WHOLE_BYTES = 2 * 1024 * 1024
INPROJ_ROWS = 1024
BWD_X_ROWS = 512
WGRAD_ROWS = 2048


def _cparams(sem=None, **kw):
    return pltpu.CompilerParams(dimension_semantics=sem, vmem_limit_bytes=VMEM_LIMIT, **kw)


def _pick(total, target, mult):
    best = None
    for d in range(mult, min(total, target) + 1, mult):
        if total % d == 0:
            best = d
    assert best is not None, (total, target, mult)
    return best


def _dot(a, b):
    return jnp.dot(a, b, preferred_element_type=F32)


def _dot_nt(a, b):
    return lax.dot_general(a, b, (((1,), (1,)), ((), ())), preferred_element_type=F32)


def _dot_tn(a, b):
    return lax.dot_general(a, b, (((0,), (0,)), ((), ())), preferred_element_type=F32)


def _zero(ref):
    ref[...] = jnp.zeros(ref.shape, ref.dtype)


def _sigmoid(x):
    return 1.0 / (1.0 + jnp.exp(-x))


_GELU_C = math.sqrt(2.0 / math.pi)


def _gelu(x):
    t = jnp.tanh(_GELU_C * (x + 0.044715 * x * x * x))
    return 0.5 * x * (1.0 + t), t


def _gelu_grad(x, t):
    return 0.5 * (1.0 + t) + 0.5 * x * (1.0 - t * t) * _GELU_C * (1.0 + 3.0 * 0.044715 * x * x)


def _me():
    return lax.axis_index("x"), lax.axis_index("y"), lax.axis_index("c")


def _inproj(x, norm_g, win_t, name, riders=()):
    T, D = x.shape
    N = win_t.shape[0]
    bt = _pick(T, INPROJ_ROWS, 16)
    bn = _pick(N, 1536, LANES)
    grid = (T // bt, N // bn)

    def compute(x_ref, g_ref, w_ref, p_ref, h_ref, hs_ref):
        @pl.when(pl.program_id(1) == 0)
        def _():
            xv = x_ref[...]
            rstd = lax.rsqrt(jnp.mean(xv * xv, axis=-1, keepdims=True) + RMS_EPS)
            hb = (xv * rstd * g_ref[...]).astype(BF16)
            hs_ref[...] = hb
            h_ref[...] = hb

        p_ref[...] = _dot_nt(hs_ref[...], w_ref[...]).astype(BF16)

    return _host_call(
        compute, name, grid, riders,
        inputs=[x, norm_g.reshape(1, D), win_t],
        in_specs=[pl.BlockSpec((bt, D), lambda i, j: (i, 0)),
                  pl.BlockSpec((1, D), lambda i, j: (0, 0)),
                  pl.BlockSpec((bn, D), lambda i, j: (j, 0))],
        out_specs=[pl.BlockSpec((bt, bn), lambda i, j: (i, j)),
                   pl.BlockSpec((bt, D), lambda i, j: (i, 0))],
        out_shape=[SDS((T, N), BF16), SDS((T, D), BF16)],
        scratch_shapes=[pltpu.VMEM((bt, D), BF16)])


def _inproj_gathering(x, norm_g, w_loc, chip_order, name, riders=()):
    T, D = x.shape
    n = w_loc.shape[0]
    N = n * N_DEV
    cw = 2 * n
    bt = _pick(T, INPROJ_ROWS, 16)
    nt = T // bt
    r_in, r_out, r_sems, copies = _rider_plan(riders)
    n_rin, n_rout = len(r_in), len(r_out)

    def body(q_ref, x_ref, g_ref, wloc_ref, *rest):
        rins = rest[:n_rin]
        p_ref, h_ref, wfull_ref = rest[n_rin:n_rin + 3]
        routs = rest[n_rin + 3:n_rin + 3 + n_rout]
        hs_ref, wbuf, send_sems, recv_sems, loc_sems = rest[n_rin + 3 + n_rout:n_rin + 8 + n_rout]
        rsems = rest[n_rin + 8 + n_rout:]
        j, i = pl.program_id(0), pl.program_id(1)
        cx, cy, cc = _me()
        sibling = (cx, cy, 1 - cc)

        def rows(k):
            return wfull_ref.at[pl.ds(pl.multiple_of(k * n, 8), n)]

        def shard_copy(slot, src, k, to):
            return pltpu.make_async_remote_copy(src_ref=src, dst_ref=rows(k), send_sem=send_sems.at[slot],
                                                recv_sem=recv_sems.at[slot], device_id=to, device_id_type=MESH)

        me = 4 * cx + 2 * cy + cc
        place_mine = pltpu.make_async_copy(wloc_ref, rows(me), loc_sems.at[0])
        sends = [shard_copy(0, wloc_ref, me, sibling)]
        for jj in (1, 2, 3):
            sends.append(shard_copy(jj, wloc_ref, me, (*_chip_peer(cx, cy, jj), cc)))

        def forward(jj):
            px, py = _chip_peer(cx, cy, jj)
            k = 4 * px + 2 * py + cc
            return shard_copy(3 + jj, rows(k), k, sibling)

        def load_chunk(q):
            cp = pltpu.make_async_copy(wfull_ref.at[pl.ds(pl.multiple_of(q * cw, 8), cw)], wbuf, loc_sems.at[1])
            cp.start()
            cp.wait()

        keep_h = pltpu.make_async_copy(hs_ref, h_ref, loc_sems.at[2])

        @pl.when((j == 0) & (i == 0))
        def _():
            place_mine.start()
            for cp in sends:
                cp.start()
            for cp in copies(rins, routs, rsems):
                cp.start()
            place_mine.wait()
            sends[0].wait_recv()
            load_chunk(q_ref[0])

        for jj in (1, 2, 3):
            @pl.when((j == jj) & (i == 0))
            def _(jj=jj):
                sends[jj].wait_recv()
                fwd = forward(jj)
                fwd.start()
                fwd.wait_recv()
                load_chunk(q_ref[jj])

        tok = pl.ds(pl.multiple_of(i * bt, bt), bt)

        @pl.when(j == 0)
        def _():
            xv = x_ref[...]
            rstd = lax.rsqrt(jnp.mean(xv * xv, axis=-1, keepdims=True) + RMS_EPS)
            hs_ref[tok, :] = (xv * rstd * g_ref[...]).astype(BF16)

        @pl.when((j == 0) & (i == nt - 1))
        def _():
            keep_h.start()

        p_ref[...] = _dot_nt(hs_ref[tok, :], wbuf[...]).astype(BF16)

        @pl.when((j == N_CHIP - 1) & (i == nt - 1))
        def _():
            keep_h.wait()
            for cp in sends:
                cp.wait_send()
            for jj in (1, 2, 3):
                forward(jj).wait_send()
            for cp in copies(rins, routs, rsems):
                cp.wait()

    res = pl.pallas_call(
        body, name=name,
        grid_spec=pltpu.PrefetchScalarGridSpec(
            num_scalar_prefetch=1, grid=(N_CHIP, nt),
            in_specs=[pl.BlockSpec((bt, D), lambda j, i, q: (jnp.where(j == 0, i, nt - 1), 0)),
                      pl.BlockSpec((1, D), lambda j, i, q: (0, 0)), _ANY] + [_ANY] * n_rin,
            out_specs=[pl.BlockSpec((bt, cw), lambda j, i, q: (i, q[j])), _ANY, _ANY] + [_ANY] * n_rout,
            scratch_shapes=[pltpu.VMEM((T, D), BF16), pltpu.VMEM((cw, D), BF16), pltpu.SemaphoreType.DMA((7,)),
                            pltpu.SemaphoreType.DMA((7,)), pltpu.SemaphoreType.DMA((3,))] + r_sems),
        out_shape=[SDS((T, N), BF16), SDS((T, D), BF16), SDS((N, D), BF16)] + r_out,
        compiler_params=_cparams(("arbitrary", "arbitrary")),
    )(chip_order, x, norm_g.reshape(1, D), w_loc, *r_in)
    return res[:3], _split_riders(riders, res[3:])


C_LNG, C_LNB, C_CW0, C_CW1, C_CW2, C_CB, C_PS = range(7)
C_ROWS = 8


def _pool_bands(R, anticausal):
    t = np.arange(R)[:, None]
    s = np.arange(R + CHUNK)[None, :]
    bands = [((s >= t) & (s < t + w)) if anticausal else ((s > t + CHUNK - w) & (s <= t + CHUNK)) for w in POOL_WINDOWS]
    return jnp.asarray(np.stack(bands), dtype=BF16)


def _mixers(p_ref, hxb_ref, hcg_ref, hxc_ref, cv, bsb_ref, wcat_ref, wpool_ref, band_ref, extb,
            first, blk, R, need_grad):
    def seg(lo):
        return p_ref[:, lo:lo + SEG].astype(F32)

    u, v, za = seg(O_U), seg(O_V), seg(O_ZA)
    xb, bg, cg, zb = seg(O_XB), seg(O_BG), seg(O_CG), seg(O_ZB)
    xc, zc = seg(O_XC), seg(O_ZC)
    out = {}

    ug, tu = _gelu(u)
    vg, tv = _gelu(v)
    mu = jnp.mean(vg, axis=-1, keepdims=True)
    vcen = vg - mu
    rs = lax.rsqrt(jnp.mean(vcen * vcen, axis=-1, keepdims=True) + LN_EPS)
    vhat = vcen * rs
    vn = (vhat * cv[C_LNG:C_LNG + 1, :] + cv[C_LNB:C_LNB + 1, :]).astype(BF16)
    lane_group = lax.broadcasted_iota(jnp.int32, (CHUNK, SEG), 1) // HEAD
    zero_b = jnp.zeros((CHUNK, SEG), BF16)
    sgs = []
    for ci in range(R // CHUNK):
        vc = vn[ci * CHUNK:(ci + 1) * CHUNK]
        vst = jnp.concatenate([jnp.where(lane_group == g, vc, zero_b) for g in range(GROUPS)], axis=0)
        sgs.append(_dot(wcat_ref[...], vst) + bsb_ref[...])
    sg = sgs[0] if len(sgs) == 1 else jnp.concatenate(sgs, axis=0)
    a_out = ug * sg
    sa = _sigmoid(za)
    out["a"] = a_out * (za * sa)

    cx = cg * xb
    halo_b = hcg_ref[...].astype(F32) * hxb_ref[...].astype(F32)
    extb[0:HALO, :] = jnp.where(first, 0.0, halo_b)
    extb[HALO:HALO + R, :] = cx
    cx1 = extb[pl.ds(HALO - 1, R), :]
    cx2 = extb[pl.ds(HALO - 2, R), :]
    yconv = (cv[C_CW0:C_CW0 + 1, :] * cx2 + cv[C_CW1:C_CW1 + 1, :] * cx1
             + cv[C_CW2:C_CW2 + 1, :] * cx + cv[C_CB:C_CB + 1, :])
    b_out = bg * yconv
    sb = _sigmoid(zb)
    out["b"] = b_out * (zb * sb)

    halo_c = hxc_ref[...]
    xc_ext = jnp.concatenate([jnp.zeros((CHUNK - HALO, SEG), BF16), jnp.where(first, jnp.zeros_like(halo_c), halo_c),
                              p_ref[:, O_XC:O_XC + SEG]], axis=0)
    tpos = blk * R + lax.broadcasted_iota(jnp.int32, (R, POOL_GROUP), 0) + 1
    pooled, invs, pws = [], [], []
    for gi, w in enumerate(POOL_WINDOWS):
        lo = gi * POOL_GROUP
        win = _dot(band_ref[gi], xc_ext[:, lo:lo + POOL_GROUP])
        inv = 1.0 / jnp.minimum(tpos, w).astype(F32)
        pg = (win * inv - xc[:, lo:lo + POOL_GROUP]).astype(BF16)
        pooled.append(pg)
        invs.append(inv)
        pws.append(_dot(pg, wpool_ref[gi]))
    pw = jnp.concatenate(pws, axis=1)
    c_out = pw * cv[C_PS:C_PS + 1, :]
    sc = _sigmoid(zc)
    out["c"] = c_out * (zc * sc)

    if need_grad:
        out.update(u=u, v=v, tu=tu, tv=tv, ug=ug, sg=sg, a_out=a_out, za=za, sa=sa,
                   rs=rs, vhat=vhat, vn=vn, lane_group=lane_group, zero_b=zero_b,
                   xb=xb, bg=bg, cg=cg, cx=cx, cx1=cx1, cx2=cx2, yconv=yconv, b_out=b_out, zb=zb, sb=sb,
                   pooled=pooled, invs=invs, pw=pw, c_out=c_out, zc=zc, sc=sc)
    return out


def _halo_specs(R, nb, rev):
    step = R // HALO

    def mk(col):
        def imap(i):
            b = (nb - 1 - i) if rev else i
            return (jnp.maximum(b * step - 1, 0), col)
        return pl.BlockSpec((HALO, SEG), imap)

    return [mk(O_XB // SEG), mk(O_CG // SEG), mk(O_XC // SEG)]


def _const_spec(shape):
    nd = len(shape)
    return pl.BlockSpec(shape, lambda i: (0,) * nd, pipeline_mode=pl.Buffered(1))


MIX_FWD_ROWS = 512
MIX_BWD_ROWS = 256


def _mix_block_rows(T, target):
    return _pick(T, target, CHUNK)


def _mix_fwd(p, x, lw, name, riders=()):
    T, D = x.shape
    N = p.shape[1]
    R = _mix_block_rows(T, MIX_FWD_ROWS)
    nb = T // R

    def body(p_ref, hxb, hcg, hxc, x_ref, cv_ref, bsb_ref, wcat_ref, wpool_ref, band_ref, wpa_ref, wpb_ref, wpc_ref,
             wo_ref, xo_ref, ya_ref, yb_ref, yc_ref, extb):
        i = pl.program_id(0)
        cv = cv_ref[...]
        r = _mixers(p_ref, hxb, hcg, hxc, cv, bsb_ref, wcat_ref, wpool_ref, band_ref, extb,
                    i == 0, i, R, False)
        merged = None
        for k, (act, w_ref, y_ref) in enumerate(((r["a"], wpa_ref, ya_ref), (r["b"], wpb_ref, yb_ref),
                                                 (r["c"], wpc_ref, yc_ref))):
            y = _dot(act.astype(BF16), w_ref[...]).astype(BF16)
            y_ref[...] = y
            term = _sigmoid(p_ref[:, O_G + k * D:O_G + (k + 1) * D]) * y
            merged = term if merged is None else merged + term
        xo_ref[...] = x_ref[...] + _dot(merged, wo_ref[...])

    row = lambda w: pl.BlockSpec((R, w), lambda i: (i, 0))
    consts = [lw["cvec"], lw["bsb"], lw["wcat"], lw["wpool"], _pool_bands(R, False), lw["wpa"], lw["wpb"], lw["wpc"],
              lw["wo"]]
    return _host_call(
        body, name, (nb,), riders,
        inputs=[p, p, p, p, x, *consts],
        in_specs=[row(N)] + _halo_specs(R, nb, False) + [row(D)] + [_const_spec(c.shape) for c in consts],
        out_specs=[row(D), row(D), row(D), row(D)],
        out_shape=[SDS((T, D), F32), SDS((T, D), BF16), SDS((T, D), BF16), SDS((T, D), BF16)],
        scratch_shapes=[pltpu.VMEM((HALO + R, SEG), F32)])


def _loss_head(x, final_g, target, name):
    T, D = x.shape
    bt = _pick(T, 512, 8)

    def body(x_ref, g_ref, t_ref, dx_ref, loss_ref, dg_ref):
        @pl.when(pl.program_id(0) == 0)
        def _():
            _zero(loss_ref)
            _zero(dg_ref)

        xv = x_ref[...]
        g = g_ref[...]
        rstd = lax.rsqrt(jnp.mean(xv * xv, axis=-1, keepdims=True) + RMS_EPS)
        xhat = xv * rstd
        err = xhat * g - t_ref[...]
        part = 0.5 * jnp.sum(jnp.sum(err * err, axis=-1, keepdims=True), axis=0, keepdims=True) / D
        loss_ref[...] += jnp.broadcast_to(part, loss_ref.shape)
        dy = err * (1.0 / D)
        dg_ref[0:1, :] += jnp.sum(dy * xhat, axis=0, keepdims=True)
        dxn = dy * g
        dx_ref[...] = rstd * (dxn - xhat * jnp.mean(dxn * xhat, axis=-1, keepdims=True))

    return pl.pallas_call(
        body, name=name, grid=(T // bt,),
        in_specs=[pl.BlockSpec((bt, D), lambda i: (i, 0)), _const_spec((1, D)), pl.BlockSpec((bt, D), lambda i: (i, 0))],
        out_specs=[pl.BlockSpec((bt, D), lambda i: (i, 0)), _const_spec((8, LANES)), _const_spec((8, D))],
        out_shape=[SDS((T, D), F32), SDS((8, LANES), F32), SDS((8, D), F32)],
        compiler_params=_cparams(("arbitrary",)),
    )(x, final_g.reshape(1, D), target)


V_LNG, V_LNB, V_CB, V_PS, V_CW0, V_CW1, V_CW2 = range(7)


def _mix_bwd(p, dxo, ya, yb, yc, lw, name):
    T, D = dxo.shape
    N = p.shape[1]
    R = _mix_block_rows(T, MIX_BWD_ROWS)
    nb = T // R

    def body(p_ref, hxb, hcg, hxc, dxo_ref, ya_ref, yb_ref, yc_ref, cv_ref, bsb_ref, wcat_ref, wcatt_ref,
             wpool_ref, band_ref, bandt_ref, wpa_ref, wpb_ref, wpc_ref, wo_ref,
             dp_ref, acts_ref, mrg_ref, dys_ref, gwc_ref, gbs_ref, gwpool_ref, gvec_ref,
             extb, extdy, cdy, cq, bsacc):
        i = pl.program_id(0)
        blk = nb - 1 - i

        @pl.when(i == 0)
        def _():
            for ref in (gwc_ref, gwpool_ref, gvec_ref, cdy, cq, bsacc):
                _zero(ref)

        cv = cv_ref[...]
        r = _mixers(p_ref, hxb, hcg, hxc, cv, bsb_ref, wcat_ref, wpool_ref, band_ref, extb,
                    blk == 0, blk, R, True)

        dxo_b = dxo_ref[...].astype(BF16)
        dm = _dot_nt(dxo_b, wo_ref[...]).astype(BF16)
        ys = [ya_ref[...], yb_ref[...], yc_ref[...]]
        sig = [_sigmoid(p_ref[:, O_G + k * D:O_G + (k + 1) * D]) for k in range(3)]
        mrg_ref[...] = sig[0] * ys[0] + sig[1] * ys[1] + sig[2] * ys[2]
        dacts = []
        for k, (act, w_ref) in enumerate(((r["a"], wpa_ref), (r["b"], wpb_ref), (r["c"], wpc_ref))):
            dyk = dm * sig[k]
            dp_ref[:, O_G + k * D:O_G + (k + 1) * D] = dyk * ys[k] * (1.0 - sig[k])
            acts_ref[:, k * SEG:(k + 1) * SEG] = act.astype(BF16)
            dys_ref[:, k * D:(k + 1) * D] = dyk
            dacts.append(_dot_nt(dyk, w_ref[...]))
        da, db, dc = dacts

        def silu_bwd(dact, pre, z, s):
            return dact * (z * s), dact * pre * (s * (1.0 + z * (1.0 - s)))

        d_aout, dza = silu_bwd(da, r["a_out"], r["za"], r["sa"])
        dp_ref[:, O_ZA:O_ZA + SEG] = dza.astype(BF16)
        dp_ref[:, O_U:O_U + SEG] = (d_aout * r["sg"] * _gelu_grad(r["u"], r["tu"])).astype(BF16)
        d_sg = d_aout * r["ug"]
        dvns = []
        for ci in range(R // CHUNK):
            dsc = d_sg[ci * CHUNK:(ci + 1) * CHUNK]
            bsacc[...] += dsc
            dsc_b = dsc.astype(BF16)
            dst = jnp.concatenate([jnp.where(r["lane_group"] == g, dsc_b, r["zero_b"]) for g in range(GROUPS)], axis=0)
            dvns.append(_dot(wcatt_ref[...], dst))
            gwc_ref[...] += _dot_nt(dst, r["vn"][ci * CHUNK:(ci + 1) * CHUNK])
        d_vn = dvns[0] if len(dvns) == 1 else jnp.concatenate(dvns, axis=0)
        vhat = r["vhat"]
        gvec_ref[V_LNG:V_LNG + 1, :] += jnp.sum(d_vn * vhat, axis=0, keepdims=True)
        gvec_ref[V_LNB:V_LNB + 1, :] += jnp.sum(d_vn, axis=0, keepdims=True)
        d_vhat = d_vn * cv[C_LNG:C_LNG + 1, :]
        d_vg = r["rs"] * (d_vhat - jnp.mean(d_vhat, axis=-1, keepdims=True)
                          - vhat * jnp.mean(d_vhat * vhat, axis=-1, keepdims=True))
        dp_ref[:, O_V:O_V + SEG] = (d_vg * _gelu_grad(r["v"], r["tv"])).astype(BF16)

        d_bout, dzb = silu_bwd(db, r["b_out"], r["zb"], r["sb"])
        dp_ref[:, O_ZB:O_ZB + SEG] = dzb.astype(BF16)
        dp_ref[:, O_BG:O_BG + SEG] = (d_bout * r["yconv"]).astype(BF16)
        d_y = d_bout * r["bg"]
        gvec_ref[V_CB:V_CB + 1, :] += jnp.sum(d_y, axis=0, keepdims=True)
        gvec_ref[V_CW0:V_CW0 + 1, :] += jnp.sum(d_y * r["cx2"], axis=0, keepdims=True)
        gvec_ref[V_CW1:V_CW1 + 1, :] += jnp.sum(d_y * r["cx1"], axis=0, keepdims=True)
        gvec_ref[V_CW2:V_CW2 + 1, :] += jnp.sum(d_y * r["cx"], axis=0, keepdims=True)
        extdy[0:R, :] = d_y
        extdy[R:R + HALO, :] = cdy[...]
        d_cx = (cv[C_CW2:C_CW2 + 1, :] * d_y + cv[C_CW1:C_CW1 + 1, :] * extdy[pl.ds(1, R), :]
                + cv[C_CW0:C_CW0 + 1, :] * extdy[pl.ds(2, R), :])
        cdy[...] = d_y[0:HALO]
        dp_ref[:, O_CG:O_CG + SEG] = (d_cx * r["xb"]).astype(BF16)
        dp_ref[:, O_XB:O_XB + SEG] = (d_cx * r["cg"]).astype(BF16)

        d_cout, dzc = silu_bwd(dc, r["c_out"], r["zc"], r["sc"])
        dp_ref[:, O_ZC:O_ZC + SEG] = dzc.astype(BF16)
        gvec_ref[V_PS:V_PS + 1, :] += jnp.sum(d_cout * r["pw"], axis=0, keepdims=True)
        d_pw = (d_cout * cv[C_PS:C_PS + 1, :]).astype(BF16)
        dpool, scaled = [], []
        for gi, w in enumerate(POOL_WINDOWS):
            lo = gi * POOL_GROUP
            dpw_g = d_pw[:, lo:lo + POOL_GROUP]
            gwpool_ref[lo:lo + POOL_GROUP, :] += _dot_tn(r["pooled"][gi], dpw_g)
            dpg = _dot_nt(dpw_g, wpool_ref[gi])
            dpool.append(dpg)
            scaled.append((dpg * r["invs"][gi]).astype(BF16))
        q = jnp.concatenate(scaled, axis=1)
        q_ext = jnp.concatenate([q, cq[...], jnp.zeros((CHUNK - HALO, SEG), BF16)], axis=0)
        for gi, w in enumerate(POOL_WINDOWS):
            lo = gi * POOL_GROUP
            acc = _dot(bandt_ref[gi], q_ext[:, lo:lo + POOL_GROUP])
            dp_ref[:, O_XC + lo:O_XC + lo + POOL_GROUP] = (acc - dpool[gi]).astype(BF16)
        cq[...] = q[0:HALO]

        @pl.when(i == nb - 1)
        def _():
            rr = lax.broadcasted_iota(jnp.int32, gwc_ref.shape, 0) % CHUNK
            cc = lax.broadcasted_iota(jnp.int32, gwc_ref.shape, 1)
            gwc_ref[...] = jnp.where(cc <= rr, gwc_ref[...], 0.0)
            acc = bsacc[...]
            hi = acc.astype(BF16)
            lo_ = (acc - hi.astype(F32)).astype(BF16)
            sel = (lax.broadcasted_iota(jnp.int32, (SEG, LANES), 0) // HEAD
                   == lax.broadcasted_iota(jnp.int32, (SEG, LANES), 1)).astype(BF16)
            gbs_ref[...] = _dot(hi, sel) + _dot(lo_, sel)

    row = lambda w: pl.BlockSpec((R, w), lambda i: (nb - 1 - i, 0))
    consts = [lw["cvec"], lw["bsb"], lw["wcat"], lw["wcatt"], lw["wpool"], _pool_bands(R, False), _pool_bands(R, True),
              lw["wpa"], lw["wpb"], lw["wpc"], lw["wo"]]
    acc_shapes = [(GROUPS * CHUNK, CHUNK), (CHUNK, LANES), (SEG, POOL_GROUP), (8, SEG)]
    row_widths = [N, 3 * SEG, D, 3 * D]
    return pl.pallas_call(
        body, name=name, grid=(nb,),
        in_specs=([row(N)] + _halo_specs(R, nb, True) + [row(D), row(D), row(D), row(D)]
                  + [_const_spec(c.shape) for c in consts]),
        out_specs=[row(w) for w in row_widths] + [_const_spec(s) for s in acc_shapes],
        out_shape=[SDS((T, w), BF16) for w in row_widths] + [SDS(s, F32) for s in acc_shapes],
        scratch_shapes=[pltpu.VMEM((HALO + R, SEG), F32)] * 2
        + [pltpu.VMEM((HALO, SEG), F32), pltpu.VMEM((HALO, SEG), BF16), pltpu.VMEM((CHUNK, SEG), F32)],
        compiler_params=_cparams(("arbitrary",)),
    )(p, p, p, p, dxo, ya, yb, yc, *consts)


def _proj_wgrad(acts, merged, dys, dxo, name, riders=()):
    T, D = dxo.shape
    bk = _pick(T, WGRAD_ROWS // 2, 16)

    def body(a_ref, m_ref, dy_ref, dxo_ref, gwpa_ref, gwpb_ref, gwpc_ref, gwo_ref):
        @pl.when(pl.program_id(0) == 0)
        def _():
            for ref in (gwpa_ref, gwpb_ref, gwpc_ref, gwo_ref):
                _zero(ref)

        gwo_ref[...] += _dot_tn(m_ref[...], dxo_ref[...].astype(BF16))
        for k, ref in enumerate((gwpa_ref, gwpb_ref, gwpc_ref)):
            ref[...] += _dot_tn(a_ref[:, k * SEG:(k + 1) * SEG], dy_ref[:, k * D:(k + 1) * D])

    row = lambda w: pl.BlockSpec((bk, w), lambda i: (i, 0))
    shapes = [(SEG, D), (SEG, D), (SEG, D), (D, D)]
    return _host_call(
        body, name, (T // bk,), riders,
        inputs=[acts, merged, dys, dxo],
        in_specs=[row(3 * SEG), row(D), row(3 * D), row(D)],
        out_specs=[_const_spec(s) for s in shapes], out_shape=[SDS(s, F32) for s in shapes],
        scratch_shapes=[])


def _inproj_token_blocks(T):
    return T // _pick(T, BWD_X_ROWS, 16)


def _inproj_bwd_x(dp, win_t, x, norm_g, dxo, name, riders=(), blocks=None, fill=None):
    T, D = x.shape
    N = dp.shape[1]
    bt = _pick(T, BWD_X_ROWS, 16)
    b0, nblk = blocks if blocks else (0, T // bt)

    def compute(dp_ref, w_ref, x_ref, g_ref, dxo_ref, *rest):
        dx_ref, dg_ref = rest[-2:]

        @pl.when(pl.program_id(0) == 0)
        def _():
            _zero(dg_ref)

        dh = _dot(dp_ref[...], w_ref[...])
        xv = x_ref[...]
        rstd = lax.rsqrt(jnp.mean(xv * xv, axis=-1, keepdims=True) + RMS_EPS)
        xhat = xv * rstd
        dg_ref[0:1, :] += jnp.sum(dh * xhat, axis=0, keepdims=True)
        dxn = dh * g_ref[...]
        dx_ref[...] = dxo_ref[...] + rstd * (dxn - xhat * jnp.mean(dxn * xhat, axis=-1, keepdims=True))

    rows = pl.BlockSpec((bt, D), lambda i: (i + b0, 0))
    return _host_call(
        compute, name, (nblk,), riders,
        inputs=[dp, win_t, x, norm_g.reshape(1, D), dxo] + ([] if fill is None else [fill]),
        in_specs=[pl.BlockSpec((bt, N), lambda i: (i + b0, 0)), _const_spec((N, D)), rows, _const_spec((1, D)), rows]
        + ([] if fill is None else [_ANY]),
        out_specs=[rows, _const_spec((8, D))],
        out_shape=[SDS((T, D), F32), SDS((8, D), F32)],
        scratch_shapes=[],
        aliases={} if fill is None else {5: 0})


def _inproj_bwd_w(dp, h, name, riders=()):
    T, N = dp.shape
    D = h.shape[1]
    bn = _pick(N, 1536, LANES)
    bk = _pick(T, WGRAD_ROWS, 16)
    nk = T // bk

    def compute(dp_ref, h_ref, o_ref):
        @pl.when(pl.program_id(1) == 0)
        def _():
            _zero(o_ref)

        o_ref[...] += _dot_tn(dp_ref[...], h_ref[...])

    return _host_call(
        compute, name, (N // bn, nk), riders,
        inputs=[dp, h],
        in_specs=[pl.BlockSpec((bk, bn), lambda j, k: (k, j)), pl.BlockSpec((bk, D), lambda j, k: (k, 0))],
        out_specs=[pl.BlockSpec((bn, D), lambda j, k: (j, 0))],
        out_shape=[SDS((N, D), F32)],
        scratch_shapes=[])


def _chip_peer(x, y, j):
    px = (1 - x) if (j >> 1) else x
    py = (1 - y) if (j & 1) else y
    return px, py


def _blk(ref, kind, k, n):
    if kind == "rows":
        return ref.at[pl.ds(pl.multiple_of(k * n, 8), n)]
    return ref.at[:, pl.ds(pl.multiple_of(k * n, LANES), n)]


class _Exchange:
    def __init__(self, srcs, out_shapes, n_sems, build, alias=False):
        self.srcs, self.out_shapes, self.n_sems, self.build, self.alias = list(srcs), list(out_shapes), n_sems, build, alias


def _rider_plan(riders):
    inputs = [s for e in riders for s in e.srcs]
    out_shapes = [o for e in riders for o in e.out_shapes]
    sems = [pltpu.SemaphoreType.DMA((e.n_sems,)) for e in riders for _ in range(2)]

    def copies(in_refs, out_refs, sem_refs):
        cps, i, o = [], 0, 0
        for k, e in enumerate(riders):
            ni, no = len(e.srcs), len(e.out_shapes)
            cps += e.build(in_refs[i:i + ni], out_refs[o:o + no], sem_refs[2 * k], sem_refs[2 * k + 1])
            i, o = i + ni, o + no
        return cps

    return inputs, out_shapes, sems, copies


_ANY = pl.BlockSpec(memory_space=pl.ANY)


def _host_call(compute, name, grid, riders, inputs, in_specs, out_specs, out_shape, scratch_shapes, aliases=None):
    r_in, r_out, r_sems, copies = _rider_plan(riders)
    ni, no, ns = len(inputs), len(out_shape), len(scratch_shapes)

    def body(*refs):
        ins, rins = refs[:ni], refs[ni:ni + len(r_in)]
        outs = refs[ni + len(r_in):ni + len(r_in) + no]
        routs = refs[ni + len(r_in) + no:ni + len(r_in) + no + len(r_out)]
        scr = refs[ni + len(r_in) + no + len(r_out):]
        first = functools.reduce(lambda a, b: a & b, [pl.program_id(d) == 0 for d in range(len(grid))])
        last = functools.reduce(lambda a, b: a & b, [pl.program_id(d) == grid[d] - 1 for d in range(len(grid))])
        if riders:
            @pl.when(first)
            def _():
                for cp in copies(rins, routs, scr[ns:]):
                    cp.start()

        compute(*ins, *outs, *scr[:ns])

        if riders:
            @pl.when(last)
            def _():
                for cp in copies(rins, routs, scr[ns:]):
                    cp.wait()

    res = pl.pallas_call(
        body, name=name, grid=grid,
        in_specs=list(in_specs) + [_ANY] * len(r_in),
        out_specs=list(out_specs) + [_ANY] * len(r_out),
        out_shape=list(out_shape) + r_out,
        scratch_shapes=list(scratch_shapes) + r_sems,
        input_output_aliases=aliases or {},
        compiler_params=_cparams(("arbitrary",) * len(grid)),
    )(*inputs, *r_in)
    return res[:no], _split_riders(riders, res[no:])


def _split_riders(riders, flat):
    out, o = [], 0
    for e in riders:
        out.append(list(flat[o:o + len(e.out_shapes)]))
        o += len(e.out_shapes)
    return out


def _run_exchange(ex, name):
    n_in, n_out = len(ex.srcs), len(ex.out_shapes)

    def body(*refs):
        cps = ex.build(refs[:n_in], refs[n_in:n_in + n_out], refs[n_in + n_out], refs[n_in + n_out + 1])
        for cp in cps:
            cp.start()
        for cp in cps:
            cp.wait()

    return pl.pallas_call(
        body, name=name,
        in_specs=[_ANY] * n_in, out_specs=[_ANY] * n_out, out_shape=ex.out_shapes,
        input_output_aliases={i: i for i in range(n_in)} if ex.alias else {},
        scratch_shapes=[pltpu.SemaphoreType.DMA((ex.n_sems,)), pltpu.SemaphoreType.DMA((ex.n_sems,))],
        compiler_params=pltpu.CompilerParams(has_side_effects=True),
    )(*ex.srcs)


def _gather_sizes(shards, kinds):
    sizes = [s.shape[0] if k == "rows" else s.shape[1] for s, k in zip(shards, kinds)]
    fulls = [SDS((s.shape[0] * N_DEV,) + s.shape[1:], s.dtype) if k == "rows"
             else SDS((s.shape[0], s.shape[1] * N_DEV), s.dtype) for s, k in zip(shards, kinds)]
    return sizes, fulls


def _gather_direct(shards, kinds):
    n = len(shards)
    sizes, fulls = _gather_sizes(shards, kinds)

    def build(ins, outs, send_sems, recv_sems):
        x, y, c = _me()
        cps = []
        for a in range(n):
            mine = _blk(outs[a], kinds[a], 4 * x + 2 * y + c, sizes[a])
            cps.append(pltpu.make_async_copy(ins[a], mine, send_sems.at[5 * a + 4]))
            for j in range(N_CHIP):
                to = (x, y, 1 - c) if j == 0 else (*_chip_peer(x, y, j), c)
                cps.append(pltpu.make_async_remote_copy(
                    src_ref=ins[a], dst_ref=mine, send_sem=send_sems.at[5 * a + j], recv_sem=recv_sems.at[5 * a + j],
                    device_id=to, device_id_type=MESH))
        return cps

    return _Exchange(shards, fulls, 5 * n, build)


def _gather_everywhere(shards, kinds):
    n = len(shards)
    sizes, fulls = _gather_sizes(shards, kinds)

    def build(ins, outs, send_sems, recv_sems):
        x, y, c = _me()
        cps = []
        for a in range(n):
            mine = _blk(outs[a], kinds[a], 4 * x + 2 * y + c, sizes[a])
            cps.append(pltpu.make_async_copy(ins[a], mine, send_sems.at[N_DEV * a]))
            for d in range(1, N_DEV):
                to = ((1 - x) if d & 4 else x, (1 - y) if d & 2 else y, (1 - c) if d & 1 else c)
                cps.append(pltpu.make_async_remote_copy(
                    src_ref=ins[a], dst_ref=mine, send_sem=send_sems.at[N_DEV * a + d],
                    recv_sem=recv_sems.at[N_DEV * a + d], device_id=to, device_id_type=MESH))
        return cps

    return _Exchange(shards, fulls, N_DEV * n, build)


def _gather_forward(fulls, kinds, sizes):
    n = len(fulls)

    def build(ins, outs, send_sems, recv_sems):
        x, y, c = _me()
        cps = []
        for a in range(n):
            for j in (1, 2, 3):
                px, py = _chip_peer(x, y, j)
                k = 4 * px + 2 * py + c
                cps.append(pltpu.make_async_remote_copy(
                    src_ref=_blk(ins[a], kinds[a], k, sizes[a]), dst_ref=_blk(outs[a], kinds[a], k, sizes[a]),
                    send_sem=send_sems.at[3 * a + j - 1], recv_sem=recv_sems.at[3 * a + j - 1],
                    device_id=(x, y, 1 - c), device_id_type=MESH))
        return cps

    return _Exchange(fulls, [SDS(f.shape, f.dtype) for f in fulls], 3 * n, build, alias=True)


def _sibling_exchange(grads, kinds, sizes):
    n = len(grads)

    def blk_shape(a):
        g = grads[a]
        return (sizes[a],) + g.shape[1:] if kinds[a] == "rows" else (g.shape[0], sizes[a])

    def build(ins, outs, send_sems, recv_sems):
        x, y, c = _me()
        cps = []
        for a in range(n):
            for q in range(N_CHIP):
                cps.append(pltpu.make_async_remote_copy(
                    src_ref=_blk(ins[a], kinds[a], 2 * q + (1 - c), sizes[a]), dst_ref=outs[a].at[q],
                    send_sem=send_sems.at[N_CHIP * a + q], recv_sem=recv_sems.at[N_CHIP * a + q],
                    device_id=(x, y, 1 - c), device_id_type=MESH))
        return cps

    return _Exchange(grads, [SDS((N_CHIP,) + blk_shape(a), F32) for a in range(n)], N_CHIP * n, build)


def _chip_partial(g, r1, kind, size, cidx, name):
    if kind == "rows":
        rows, cols = size, g.shape[1]
        g3 = g.reshape(N_DEV, rows, cols)
        rb = _pick(rows, 512, 16)
        g_spec = pl.BlockSpec((1, rb, cols), lambda q, j, c: (2 * q + c[0], j, 0))
        grid = (N_CHIP, rows // rb)
        blk = (1, rb, cols)
        imap = lambda q, j, c: (q, j, 0)
    else:
        rows, cols = g.shape[0], size
        g3 = g
        g_spec = pl.BlockSpec((rows, cols), lambda q, j, c: (0, 2 * q + c[0]))
        grid = (N_CHIP, 1)
        blk = (1, rows, cols)
        imap = lambda q, j, c: (q, 0, 0)

    def body(c_ref, g_ref, r_ref, p_ref, pb_ref):
        s = g_ref[...].reshape(blk) + r_ref[...]
        p_ref[...] = s
        pb_ref[...] = s.astype(BF16)

    return pl.pallas_call(
        body, name=name,
        grid_spec=pltpu.PrefetchScalarGridSpec(
            num_scalar_prefetch=1, grid=grid,
            in_specs=[g_spec, pl.BlockSpec(blk, imap)],
            out_specs=[pl.BlockSpec(blk, imap), pl.BlockSpec(blk, imap)]),
        out_shape=[SDS((N_CHIP, rows, cols), F32), SDS((N_CHIP, rows, cols), BF16)],
        compiler_params=_cparams(("arbitrary", "arbitrary")),
    )(cidx, g3, r1)


def _chip_exchange(parts):
    n = len(parts)
    m = N_CHIP - 1

    def build(ins, outs, send_sems, recv_sems):
        x, y, c = _me()
        cps = []
        for a in range(n):
            for j in (1, 2, 3):
                px, py = _chip_peer(x, y, j)
                cps.append(pltpu.make_async_remote_copy(
                    src_ref=ins[a].at[2 * px + py], dst_ref=outs[a].at[j - 1], send_sem=send_sems.at[m * a + j - 1],
                    recv_sem=recv_sems.at[m * a + j - 1], device_id=(px, py, c), device_id_type=MESH))
        return cps

    return _Exchange(parts, [SDS((m,) + p.shape[1:], BF16) for p in parts], m * n, build)


def _grad_total(part, r2, qidx, name):
    _, rows, cols = part.shape
    rb = _pick(rows, 512, 16)

    def body(q_ref, p_ref, r_ref, o_ref):
        s = p_ref[0]
        for j in range(N_CHIP - 1):
            s = s + r_ref[j].astype(F32)
        o_ref[...] = s

    return pl.pallas_call(
        body, name=name,
        grid_spec=pltpu.PrefetchScalarGridSpec(
            num_scalar_prefetch=1, grid=(rows // rb,),
            in_specs=[pl.BlockSpec((1, rb, cols), lambda i, q: (q[0], i, 0)),
                      pl.BlockSpec((N_CHIP - 1, rb, cols), lambda i, q: (0, i, 0))],
            out_specs=pl.BlockSpec((rb, cols), lambda i, q: (i, 0))),
        out_shape=SDS((rows, cols), F32),
        compiler_params=_cparams(("arbitrary",)),
    )(qidx, part, r2)


def _all_reduce_small(pack, name):
    rows = pack.shape[0]
    rs = rows // N_DEV
    assert rs * N_DEV == rows and rs % 8 == 0

    def body(x_ref, o_ref, rbuf, red, send1, recv1, send2, recv2):
        x, y, c = _me()
        me = 4 * x + 2 * y + c

        def peer(d):
            px = (1 - x) if (d >> 2) & 1 else x
            py = (1 - y) if (d >> 1) & 1 else y
            pc = (1 - c) if d & 1 else c
            return px, py, pc

        def sl(ref, k):
            return ref.at[pl.ds(pl.multiple_of(k * rs, 8), rs)]

        phase1 = []
        for d in range(1, N_DEV):
            px, py, pc = peer(d)
            phase1.append(pltpu.make_async_remote_copy(
                src_ref=sl(x_ref, 4 * px + 2 * py + pc), dst_ref=rbuf.at[d], send_sem=send1.at[d], recv_sem=recv1.at[d],
                device_id=(px, py, pc), device_id_type=MESH))
        for cp in phase1:
            cp.start()
        acc = sl(x_ref, me)[...]
        for cp in phase1:
            cp.wait()
        for d in range(1, N_DEV):
            acc = acc + rbuf[d]
        red[...] = acc
        sl(o_ref, me)[...] = acc
        phase2 = []
        for d in range(1, N_DEV):
            px, py, pc = peer(d)
            phase2.append(pltpu.make_async_remote_copy(
                src_ref=red, dst_ref=sl(o_ref, me), send_sem=send2.at[d], recv_sem=recv2.at[d],
                device_id=(px, py, pc), device_id_type=MESH))
        for cp in phase2:
            cp.start()
        for cp in phase2:
            cp.wait()

    vm = pl.BlockSpec(memory_space=pltpu.VMEM)
    return pl.pallas_call(
        body, name=name, in_specs=[vm], out_specs=vm, out_shape=SDS(pack.shape, F32),
        scratch_shapes=[pltpu.VMEM((N_DEV, rs, LANES), F32), pltpu.VMEM((rs, LANES), F32),
                        pltpu.SemaphoreType.DMA((N_DEV,)), pltpu.SemaphoreType.DMA((N_DEV,)),
                        pltpu.SemaphoreType.DMA((N_DEV,)), pltpu.SemaphoreType.DMA((N_DEV,))],
        compiler_params=_cparams(None, has_side_effects=True),
    )(pack)


def _adamw(w, g, m, v, name):
    rows, cols = w.shape
    rb = rows if rows * cols * 4 <= ADAMW_WHOLE_BYTES else _pick(rows, 256, 8)
    c1 = 1.0 / (1.0 - ADAM_B1 ** ADAM_STEP)
    c2 = 1.0 / (1.0 - ADAM_B2 ** ADAM_STEP)

    def body(w_ref, g_ref, m_ref, v_ref, d_ref, mo_ref, vo_ref):
        gv = g_ref[...]
        mn = ADAM_B1 * m_ref[...] + (1.0 - ADAM_B1) * gv
        vn = ADAM_B2 * v_ref[...] + (1.0 - ADAM_B2) * (gv * gv)
        mo_ref[...] = mn
        vo_ref[...] = vn
        d_ref[...] = -ADAM_LR * ((mn * c1) / (jnp.sqrt(vn * c2) + ADAM_EPS) + ADAM_WD * w_ref[...])

    spec = pl.BlockSpec((rb, cols), lambda i: (i, 0))
    return pl.pallas_call(
        body, name=name, grid=(rows // rb,),
        in_specs=[spec] * 4, out_specs=[spec] * 3, out_shape=[SDS((rows, cols), F32)] * 3,
        compiler_params=_cparams(("arbitrary",)),
    )(w, g, m, v)


def _pad_rows(a, mult=8):
    r = (-a.shape[0]) % mult
    return a if r == 0 else jnp.pad(a, ((0, r), (0, 0)))


def _as_lanes(a):
    flat = a.reshape(-1)
    pad = (-flat.shape[0]) % (8 * LANES)
    if pad:
        flat = jnp.pad(flat, (0, pad))
    return flat.reshape(-1, LANES)


def kernel(x, norm_g, w_in, ln_g, ln_b, w_s, b_s, conv_w, conv_b, w_pool, pool_scale, w_pa, w_pb, w_pc, w_o, final_g, loss_target, m_norm_g, m_w_in, m_ln_g, m_ln_b, m_w_s, m_b_s, m_conv_w, m_conv_b, m_w_pool, m_pool_scale, m_w_pa, m_w_pb, m_w_pc, m_w_o, m_final_g, v_norm_g, v_w_in, v_ln_g, v_ln_b, v_w_s, v_b_s, v_conv_w, v_conv_b, v_w_pool, v_pool_scale, v_w_pa, v_w_pb, v_w_pc, v_w_o, v_final_g):
    L = w_in.shape[0]
    D = x.shape[-1]
    n_loc = w_in.shape[2]
    pc_loc = w_pa.shape[2]
    x0 = x[0]
    target = loss_target[0]
    xi, yi, ci = _me()
    cidx = jnp.reshape(ci, (1,)).astype(jnp.int32)
    qidx = jnp.reshape(2 * xi + yi, (1,)).astype(jnp.int32)

    kinds5 = ["rows", "cols", "cols", "cols", "rows"]

    def layer_shards(l):
        return [w_in[l].T.astype(BF16), w_pa[l].astype(BF16), w_pb[l].astype(BF16), w_pc[l].astype(BF16),
                w_o[l].astype(BF16)]

    def gathered(direct, shards, kinds, l):
        sizes, _ = _gather_sizes(shards, kinds)
        return _run_exchange(_gather_forward(direct, kinds, sizes), f"weights_forward_{l}")

    cw_loc = _pad_rows(conv_w.reshape(L * CONV_TAPS, -1))
    cw_loc = jnp.pad(cw_loc, ((0, 0), (0, LANES - cw_loc.shape[1])))
    causal = jnp.tril(jnp.ones((CHUNK, CHUNK), dtype=bool))

    chip_order = jnp.stack([2 * xi + yi] + [2 * px + py for px, py in (_chip_peer(xi, yi, j) for j in (1, 2, 3))])
    sh0 = layer_shards(0)
    rest0, krest0 = sh0[1:] + [cw_loc], kinds5[1:] + ["rows"]
    (p0, h0, win_t0), delivered = _inproj_gathering(x0, norm_g[0], sh0[0], chip_order.astype(jnp.int32), "inproj_fwd_0",
                                                    [_gather_everywhere(rest0, krest0)])
    rest0_full = delivered[0]
    cw_all = rest0_full[-1].reshape(N_DEV, -1, LANES)[:, :L * CONV_TAPS, :conv_w.shape[2]]
    conv_w_full = jnp.transpose(cw_all, (1, 0, 2)).reshape(L, CONV_TAPS, -1)

    def make_layer(l, full5):
        win_t, wpa, wpb, wpc, wo = full5
        wm = jnp.where(causal, w_s[l], 0.0)
        cvec = jnp.concatenate([ln_g[l][None], ln_b[l][None], conv_w_full[l], conv_b[l][None], pool_scale[l][None],
                                jnp.zeros((C_ROWS - 7, SEG), F32)], axis=0)
        return dict(
            win_t=win_t, wpa=wpa, wpb=wpb, wpc=wpc, wo=wo, cvec=cvec,
            bsb=jnp.repeat(b_s[l].T, HEAD, axis=1),
            wcat=jnp.transpose(wm, (1, 0, 2)).reshape(CHUNK, GROUPS * CHUNK).astype(BF16),
            wcatt=jnp.transpose(wm, (2, 0, 1)).reshape(CHUNK, GROUPS * CHUNK).astype(BF16),
            wpool=w_pool[l].astype(BF16))

    layers = [make_layer(0, [win_t0] + list(rest0_full[:4]))]
    xs, saved = [x0], []
    for l in range(L):
        lw = layers[l]
        if l == 0:
            p, h = p0, h0
        else:
            (p, h), _ = _inproj(xs[-1], norm_g[l], lw["win_t"], f"inproj_fwd_{l}")
        nxt = layer_shards(l + 1) if l + 1 < L else None
        (xn, ya, yb, yc), delivered = _mix_fwd(p, xs[-1], lw, f"mix_fwd_{l}", [_gather_direct(nxt, kinds5)] if nxt else [])
        if nxt:
            layers.append(make_layer(l + 1, gathered(delivered[0], nxt, kinds5, l + 1)))
        saved.append((p, h, ya, yb, yc))
        xs.append(xn)
    dx, loss_acc, dfg_acc = _loss_head(xs[-1], final_g, target, "loss_head")

    rs_sizes = [n_loc, pc_loc, pc_loc, pc_loc, w_o.shape[1]]
    await_sibling, await_chips = [], []
    partial_of, from_chips = {}, {}
    serial = [0]

    def riders_now():
        riders, plan = [], []
        for grp in await_chips:
            riders.append(_chip_exchange([partial_of[t][1] for t, _, _, _ in grp]))
            plan.append(("chips", grp))
        for grp in await_sibling:
            riders.append(_sibling_exchange([g for _, g, _, _ in grp], [k for _, _, k, _ in grp], [s for _, _, _, s in grp]))
            plan.append(("sibling", grp))
        del await_chips[:], await_sibling[:]
        return riders, plan

    def absorb(plan, delivered):
        for (what, grp), res in zip(plan, delivered):
            for (t, g, k, s), r in zip(grp, res):
                if what == "chips":
                    from_chips[t] = r
                else:
                    partial_of[t] = _chip_partial(g, r, k, s, cidx, f"grad_chip_partial_{t[0]}_{t[1]}")
            if what == "sibling":
                await_chips.append(grp)

    small = [None] * L
    for l in reversed(range(L)):
        lw = layers[l]
        p, h, ya, yb, yc = saved[l]
        dp, acts, merged, dys, gwc, gbs, gwpool, gvec = _mix_bwd(p, dx, ya, yb, yc, lw, f"mix_bwd_{l}")
        riders, plan = riders_now()
        (gwpa, gwpb, gwpc, gwo), delivered = _proj_wgrad(acts, merged, dys, dx, f"proj_wgrad_{l}", riders)
        absorb(plan, delivered)
        await_sibling.append([((l, a), g, kinds5[a], rs_sizes[a]) for a, g in ((1, gwpa), (2, gwpb), (3, gwpc), (4, gwo))])

        def bwd_x(dxo, pieces):
            nt = _inproj_token_blocks(dxo.shape[0])
            pieces = min(pieces, nt)
            done, dng, b0 = None, None, 0
            for k in range(pieces):
                cnt = (nt - b0) // (pieces - k)
                riders, plan = riders_now()
                (done, dng_k), delivered = _inproj_bwd_x(dp, lw["win_t"], xs[l], norm_g[l], dxo, f"inproj_bwd_x_{l}_{k}",
                                                         riders, blocks=(b0, cnt), fill=done)
                absorb(plan, delivered)
                dng = dng_k if dng is None else dng + dng_k
                b0 += cnt
            return done, dng

        def bwd_w():
            riders, plan = riders_now()
            (gwin_t,), delivered = _inproj_bwd_w(dp, h, f"inproj_bwd_w_{l}", riders)
            absorb(plan, delivered)
            await_sibling.append([((l, 0), gwin_t, kinds5[0], rs_sizes[0])])

        if l == L - 1:
            dx, dng = bwd_x(dx, 1)
            bwd_w()
        else:
            bwd_w()
            dx, dng = bwd_x(dx, 2 if l == 0 else 1)
        small[l] = dict(norm_g=dng[0], ln_g=gvec[V_LNG], ln_b=gvec[V_LNB], w_s=gwc, b_s=gbs, conv_w=gvec[V_CW0:V_CW0 + 3],
                        conv_b=gvec[V_CB], w_pool=gwpool, pool_scale=gvec[V_PS])
    while await_sibling or await_chips:
        riders, plan = riders_now()
        delivered = []
        for ex in riders:
            delivered.append(_run_exchange(ex, f"grad_exchange_tail_{serial[0]}"))
            serial[0] += 1
        absorb(plan, delivered)
    grad_x = dx[None]
    big_grads = []
    for l in range(L):
        tot = [_grad_total(partial_of[(l, a)][0], from_chips[(l, a)], qidx, f"grad_total_{l}_{a}") for a in range(5)]
        big_grads.append([tot[0].T,
                          tot[1].reshape(SEG, pc_loc), tot[2].reshape(SEG, pc_loc), tot[3].reshape(SEG, pc_loc),
                          tot[4]])

    names = ["norm_g", "ln_g", "ln_b", "w_s", "b_s", "conv_w", "conv_b", "w_pool", "pool_scale"]
    pieces = [_as_lanes(jnp.stack([small[l][nm] for l in range(L)])) for nm in names]
    pieces += [_as_lanes(dfg_acc[0]), loss_acc]
    sizes = [pc.shape[0] for pc in pieces]
    pack = jnp.concatenate(pieces, axis=0)
    pack = _pad_rows(pack, 8 * N_DEV)
    red = _all_reduce_small(pack, "small_grads_all_reduce")
    offs = [0]
    for s in sizes:
        offs.append(offs[-1] + s)

    def unpack(i, shape):
        n = math.prod(shape)
        return red[offs[i]:offs[i + 1]].reshape(-1)[:n].reshape(shape)

    g_norm_g = unpack(0, (L, D))
    g_ln_g = unpack(1, (L, SEG))
    g_ln_b = unpack(2, (L, SEG))
    g_w_s = unpack(3, (L, GROUPS, CHUNK, CHUNK))
    g_b_s = jnp.transpose(unpack(4, (L, CHUNK, LANES))[:, :, :GROUPS], (0, 2, 1))
    g_conv_w_full = unpack(5, (L, CONV_TAPS, SEG))
    g_conv_b = unpack(6, (L, SEG))
    g_w_pool = unpack(7, (L, len(POOL_WINDOWS), POOL_GROUP, POOL_GROUP))
    g_pool_scale = unpack(8, (L, SEG))
    g_final_g = unpack(9, (D,))
    loss = red[offs[10], 0]
    dev = 4 * xi + 2 * yi + ci
    g_conv_w = lax.dynamic_slice_in_dim(g_conv_w_full, dev * conv_w.shape[2], conv_w.shape[2], axis=2)

    g_w_in = jnp.stack([big_grads[l][0] for l in range(L)])
    g_w_pa = jnp.stack([big_grads[l][1] for l in range(L)])
    g_w_pb = jnp.stack([big_grads[l][2] for l in range(L)])
    g_w_pc = jnp.stack([big_grads[l][3] for l in range(L)])
    g_w_o = jnp.stack([big_grads[l][4] for l in range(L)])

    grads = dict(norm_g=g_norm_g, w_in=g_w_in, ln_g=g_ln_g, ln_b=g_ln_b, w_s=g_w_s, b_s=g_b_s, conv_w=g_conv_w,
                 conv_b=g_conv_b, w_pool=g_w_pool, pool_scale=g_pool_scale, w_pa=g_w_pa, w_pb=g_w_pb, w_pc=g_w_pc,
                 w_o=g_w_o, final_g=g_final_g)
    weights = dict(norm_g=norm_g, w_in=w_in, ln_g=ln_g, ln_b=ln_b, w_s=w_s, b_s=b_s, conv_w=conv_w, conv_b=conv_b,
                   w_pool=w_pool, pool_scale=pool_scale, w_pa=w_pa, w_pb=w_pb, w_pc=w_pc, w_o=w_o, final_g=final_g)
    ms = dict(norm_g=m_norm_g, w_in=m_w_in, ln_g=m_ln_g, ln_b=m_ln_b, w_s=m_w_s, b_s=m_b_s, conv_w=m_conv_w,
              conv_b=m_conv_b, w_pool=m_w_pool, pool_scale=m_pool_scale, w_pa=m_w_pa, w_pb=m_w_pb, w_pc=m_w_pc,
              w_o=m_w_o, final_g=m_final_g)
    vs = dict(norm_g=v_norm_g, w_in=v_w_in, ln_g=v_ln_g, ln_b=v_ln_b, w_s=v_w_s, b_s=v_b_s, conv_w=v_conv_w,
              conv_b=v_conv_b, w_pool=v_w_pool, pool_scale=v_pool_scale, w_pa=v_w_pa, w_pb=v_w_pb, w_pc=v_w_pc,
              w_o=v_w_o, final_g=v_final_g)
    order = ["norm_g", "w_in", "ln_g", "ln_b", "w_s", "b_s", "conv_w", "conv_b", "w_pool", "pool_scale", "w_pa", "w_pb",
             "w_pc", "w_o", "final_g"]

    delta, new_m, new_v = {}, {}, {}
    big = ["w_in", "w_pa", "w_pb", "w_pc", "w_o"]
    for nm in big:
        shp = weights[nm].shape
        two = lambda a: a.reshape(-1, shp[-1])
        d, mn, vn = _adamw(two(weights[nm]), two(grads[nm]), two(ms[nm]), two(vs[nm]), f"adamw_{nm}")
        delta[nm], new_m[nm], new_v[nm] = d.reshape(shp), mn.reshape(shp), vn.reshape(shp)
    rest = [nm for nm in order if nm not in big]
    cat = lambda src: jnp.concatenate([_as_lanes(src[nm]) for nm in rest], axis=0)
    d, mn, vn = _adamw(cat(weights), cat(grads), cat(ms), cat(vs), "adamw_small")
    off = 0
    for nm in rest:
        shp = weights[nm].shape
        n = math.prod(shp)
        rows = _as_lanes(weights[nm]).shape[0]
        cut = lambda a: a[off:off + rows].reshape(-1)[:n].reshape(shp)
        delta[nm], new_m[nm], new_v[nm] = cut(d), cut(mn), cut(vn)
        off += rows

    return (loss, grad_x, *[grads[nm] for nm in order], *[delta[nm] for nm in order],
            *[new_m[nm] for nm in order], *[new_v[nm] for nm in order])
```

```python
import functools
import math

import numpy as np
import jax
import jax.numpy as jnp
from jax import lax
from jax.experimental import pallas as pl
from jax.experimental.pallas import tpu as pltpu

F32 = jnp.float32
BF16 = jnp.bfloat16
SDS = jax.ShapeDtypeStruct
MESH = pl.DeviceIdType.MESH

SEG = 512
CHUNK = 128
GROUPS = 8
HEAD = SEG // GROUPS
POOL_WINDOWS = (2, 4, 8, 16)
POOL_GROUP = SEG // len(POOL_WINDOWS)
CONV_TAPS = 3
HALO = 16
RMS_EPS = 1e-6
LN_EPS = 1e-5
ADAM_LR, ADAM_B1, ADAM_B2, ADAM_EPS, ADAM_WD, ADAM_STEP = 0.001, 0.9, 0.999, 1e-08, 0.01, 10

O_U, O_V, O_ZA, O_XB, O_BG, O_CG, O_ZB, O_XC, O_ZC, O_G = (SEG * i for i in range(10))

N_DEV = 8
N_CHIP = 4
LANES = 128
VMEM_LIMIT = 48 * 1024 * 1024
ADAMW_WHOLE_BYTES = 2 * 1024 * 1024
INPROJ_ROWS = 1024
BWD_X_ROWS = 512
WGRAD_ROWS = 2048
LATE_RIDER_CHUNK = 2


def _cparams(sem=None, **kw):
    return pltpu.CompilerParams(dimension_semantics=sem, vmem_limit_bytes=VMEM_LIMIT, **kw)


def _pick(total, target, mult):
    best = None
    for d in range(mult, min(total, target) + 1, mult):
        if total % d == 0:
            best = d
    assert best is not None, (total, target, mult)
    return best


def _dot(a, b):
    return jnp.dot(a, b, preferred_element_type=F32)


def _dot_nt(a, b):
    return lax.dot_general(a, b, (((1,), (1,)), ((), ())), preferred_element_type=F32)


def _dot_tn(a, b):
    return lax.dot_general(a, b, (((0,), (0,)), ((), ())), preferred_element_type=F32)


def _zero(ref):
    ref[...] = jnp.zeros(ref.shape, ref.dtype)


def _sigmoid(x):
    return 1.0 / (1.0 + jnp.exp(-x))


_GELU_C = math.sqrt(2.0 / math.pi)


def _gelu(x):
    t = jnp.tanh(_GELU_C * (x + 0.044715 * x * x * x))
    return 0.5 * x * (1.0 + t), t


def _gelu_grad(x, t):
    return 0.5 * (1.0 + t) + 0.5 * x * (1.0 - t * t) * _GELU_C * (1.0 + 3.0 * 0.044715 * x * x)


def _me():
    return lax.axis_index("x"), lax.axis_index("y"), lax.axis_index("c")


def _inproj(x, norm_g, win_t, name, riders=()):
    T, D = x.shape
    N = win_t.shape[0]
    bt = _pick(T, INPROJ_ROWS, 16)
    bn = _pick(N, 1536, LANES)
    grid = (T // bt, N // bn)

    def compute(x_ref, g_ref, w_ref, p_ref, h_ref, hs_ref):
        @pl.when(pl.program_id(1) == 0)
        def _():
            xv = x_ref[...]
            rstd = lax.rsqrt(jnp.mean(xv * xv, axis=-1, keepdims=True) + RMS_EPS)
            hb = (xv * rstd * g_ref[...]).astype(BF16)
            hs_ref[...] = hb
            h_ref[...] = hb

        p_ref[...] = _dot_nt(hs_ref[...], w_ref[...]).astype(BF16)

    return _host_call(
        compute, name, grid, riders,
        inputs=[x, norm_g.reshape(1, D), win_t],
        in_specs=[pl.BlockSpec((bt, D), lambda i, j: (i, 0)),
                  pl.BlockSpec((1, D), lambda i, j: (0, 0)),
                  pl.BlockSpec((bn, D), lambda i, j: (j, 0))],
        out_specs=[pl.BlockSpec((bt, bn), lambda i, j: (i, j)),
                   pl.BlockSpec((bt, D), lambda i, j: (i, 0))],
        out_shape=[SDS((T, N), BF16), SDS((T, D), BF16)],
        scratch_shapes=[pltpu.VMEM((bt, D), BF16)])


def _inproj_gathering(x, norm_g, w_loc, chip_order, name, riders=()):
    T, D = x.shape
    n = w_loc.shape[0]
    N = n * N_DEV
    cw = 2 * n
    bt = _pick(T, INPROJ_ROWS, 16)
    nt = T // bt
    r_in, r_out, r_sems, copies = _rider_plan(riders)
    n_rin, n_rout = len(r_in), len(r_out)

    def body(q_ref, x_ref, g_ref, wloc_ref, *rest):
        rins = rest[:n_rin]
        p_ref, h_ref, wfull_ref = rest[n_rin:n_rin + 3]
        routs = rest[n_rin + 3:n_rin + 3 + n_rout]
        hs_ref, wbuf, send_sems, recv_sems, loc_sems = rest[n_rin + 3 + n_rout:n_rin + 8 + n_rout]
        rsems = rest[n_rin + 8 + n_rout:]
        j, i = pl.program_id(0), pl.program_id(1)
        cx, cy, cc = _me()
        sibling = (cx, cy, 1 - cc)

        def rows(k):
            return wfull_ref.at[pl.ds(pl.multiple_of(k * n, 8), n)]

        def shard_copy(slot, src, k, to):
            return pltpu.make_async_remote_copy(src_ref=src, dst_ref=rows(k), send_sem=send_sems.at[slot],
                                                recv_sem=recv_sems.at[slot], device_id=to, device_id_type=MESH)

        me = 4 * cx + 2 * cy + cc
        place_mine = pltpu.make_async_copy(wloc_ref, rows(me), loc_sems.at[0])
        sends = [shard_copy(0, wloc_ref, me, sibling)]
        for jj in (1, 2, 3):
            sends.append(shard_copy(jj, wloc_ref, me, (*_chip_peer(cx, cy, jj), cc)))

        def forward(jj):
            px, py = _chip_peer(cx, cy, jj)
            k = 4 * px + 2 * py + cc
            return shard_copy(3 + jj, rows(k), k, sibling)

        def load_chunk(q):
            cp = pltpu.make_async_copy(wfull_ref.at[pl.ds(pl.multiple_of(q * cw, 8), cw)], wbuf, loc_sems.at[1])
            cp.start()
            cp.wait()

        keep_h = pltpu.make_async_copy(hs_ref, h_ref, loc_sems.at[2])

        @pl.when((j == 0) & (i == 0))
        def _():
            place_mine.start()
            for cp in sends:
                cp.start()
            for cp in copies(rins, routs, rsems):
                cp.start()
            place_mine.wait()
            sends[0].wait_recv()
            load_chunk(q_ref[0])

        for jj in (1, 2, 3):
            @pl.when((j == jj) & (i == 0))
            def _(jj=jj):
                sends[jj].wait_recv()
                fwd = forward(jj)
                fwd.start()
                fwd.wait_recv()
                load_chunk(q_ref[jj])
                if jj == LATE_RIDER_CHUNK:
                    for cp in copies(rins, routs, rsems, late=True):
                        cp.start()

        tok = pl.ds(pl.multiple_of(i * bt, bt), bt)

        @pl.when(j == 0)
        def _():
            xv = x_ref[...]
            rstd = lax.rsqrt(jnp.mean(xv * xv, axis=-1, keepdims=True) + RMS_EPS)
            hs_ref[tok, :] = (xv * rstd * g_ref[...]).astype(BF16)

        @pl.when((j == 0) & (i == nt - 1))
        def _():
            keep_h.start()

        p_ref[...] = _dot_nt(hs_ref[tok, :], wbuf[...]).astype(BF16)

        @pl.when((j == N_CHIP - 1) & (i == nt - 1))
        def _():
            keep_h.wait()
            for cp in sends:
                cp.wait_send()
            for jj in (1, 2, 3):
                forward(jj).wait_send()
            for late in (False, True):
                for cp in copies(rins, routs, rsems, late):
                    cp.wait()

    res = pl.pallas_call(
        body, name=name,
        grid_spec=pltpu.PrefetchScalarGridSpec(
            num_scalar_prefetch=1, grid=(N_CHIP, nt),
            in_specs=[pl.BlockSpec((bt, D), lambda j, i, q: (jnp.where(j == 0, i, nt - 1), 0)),
                      pl.BlockSpec((1, D), lambda j, i, q: (0, 0)), _ANY] + [_ANY] * n_rin,
            out_specs=[pl.BlockSpec((bt, cw), lambda j, i, q: (i, q[j])), _ANY, _ANY] + [_ANY] * n_rout,
            scratch_shapes=[pltpu.VMEM((T, D), BF16), pltpu.VMEM((cw, D), BF16), pltpu.SemaphoreType.DMA((7,)),
                            pltpu.SemaphoreType.DMA((7,)), pltpu.SemaphoreType.DMA((3,))] + r_sems),
        out_shape=[SDS((T, N), BF16), SDS((T, D), BF16), SDS((N, D), BF16)] + r_out,
        compiler_params=_cparams(("arbitrary", "arbitrary")),
    )(chip_order, x, norm_g.reshape(1, D), w_loc, *r_in)
    return res[:3], _split_riders(riders, res[3:])


C_LNG, C_LNB, C_CW0, C_CW1, C_CW2, C_CB, C_PS = range(7)
C_ROWS = 8


def _pool_bands(R, anticausal):
    t = np.arange(R)[:, None]
    s = np.arange(R + CHUNK)[None, :]
    bands = [((s >= t) & (s < t + w)) if anticausal else ((s > t + CHUNK - w) & (s <= t + CHUNK)) for w in POOL_WINDOWS]
    return jnp.asarray(np.stack(bands), dtype=BF16)


def _mixers(p_ref, hxb_ref, hcg_ref, hxc_ref, cv, bsb_ref, wcat_ref, wpool_ref, band_ref, extb,
            first, blk, R, need_grad):
    def seg(lo):
        return p_ref[:, lo:lo + SEG].astype(F32)

    u, v, za = seg(O_U), seg(O_V), seg(O_ZA)
    xb, bg, cg, zb = seg(O_XB), seg(O_BG), seg(O_CG), seg(O_ZB)
    xc, zc = seg(O_XC), seg(O_ZC)
    out = {}

    ug, tu = _gelu(u)
    vg, tv = _gelu(v)
    mu = jnp.mean(vg, axis=-1, keepdims=True)
    vcen = vg - mu
    rs = lax.rsqrt(jnp.mean(vcen * vcen, axis=-1, keepdims=True) + LN_EPS)
    vhat = vcen * rs
    vn = (vhat * cv[C_LNG:C_LNG + 1, :] + cv[C_LNB:C_LNB + 1, :]).astype(BF16)
    lane_group = lax.broadcasted_iota(jnp.int32, (CHUNK, SEG), 1) // HEAD
    zero_b = jnp.zeros((CHUNK, SEG), BF16)
    sgs = []
    for ci in range(R // CHUNK):
        vc = vn[ci * CHUNK:(ci + 1) * CHUNK]
        vst = jnp.concatenate([jnp.where(lane_group == g, vc, zero_b) for g in range(GROUPS)], axis=0)
        sgs.append(_dot(wcat_ref[...], vst) + bsb_ref[...])
    sg = sgs[0] if len(sgs) == 1 else jnp.concatenate(sgs, axis=0)
    a_out = ug * sg
    sa = _sigmoid(za)
    out["a"] = a_out * (za * sa)

    cx = cg * xb
    halo_b = hcg_ref[...].astype(F32) * hxb_ref[...].astype(F32)
    extb[0:HALO, :] = jnp.where(first, 0.0, halo_b)
    extb[HALO:HALO + R, :] = cx
    cx1 = extb[pl.ds(HALO - 1, R), :]
    cx2 = extb[pl.ds(HALO - 2, R), :]
    yconv = (cv[C_CW0:C_CW0 + 1, :] * cx2 + cv[C_CW1:C_CW1 + 1, :] * cx1
             + cv[C_CW2:C_CW2 + 1, :] * cx + cv[C_CB:C_CB + 1, :])
    b_out = bg * yconv
    sb = _sigmoid(zb)
    out["b"] = b_out * (zb * sb)

    halo_c = hxc_ref[...]
    xc_ext = jnp.concatenate([jnp.zeros((CHUNK - HALO, SEG), BF16), jnp.where(first, jnp.zeros_like(halo_c), halo_c),
                              p_ref[:, O_XC:O_XC + SEG]], axis=0)
    tpos = blk * R + lax.broadcasted_iota(jnp.int32, (R, POOL_GROUP), 0) + 1
    pooled, invs, pws = [], [], []
    for gi, w in enumerate(POOL_WINDOWS):
        lo = gi * POOL_GROUP
        win = _dot(band_ref[gi], xc_ext[:, lo:lo + POOL_GROUP])
        inv = 1.0 / jnp.minimum(tpos, w).astype(F32)
        pg = (win * inv - xc[:, lo:lo + POOL_GROUP]).astype(BF16)
        pooled.append(pg)
        invs.append(inv)
        pws.append(_dot(pg, wpool_ref[gi]))
    pw = jnp.concatenate(pws, axis=1)
    c_out = pw * cv[C_PS:C_PS + 1, :]
    sc = _sigmoid(zc)
    out["c"] = c_out * (zc * sc)

    if need_grad:
        out.update(u=u, v=v, tu=tu, tv=tv, ug=ug, sg=sg, a_out=a_out, za=za, sa=sa,
                   rs=rs, vhat=vhat, vn=vn, lane_group=lane_group, zero_b=zero_b,
                   xb=xb, bg=bg, cg=cg, cx=cx, cx1=cx1, cx2=cx2, yconv=yconv, b_out=b_out, zb=zb, sb=sb,
                   pooled=pooled, invs=invs, pw=pw, c_out=c_out, zc=zc, sc=sc)
    return out


def _halo_specs(R, nb, rev):
    step = R // HALO

    def mk(col):
        def imap(i):
            b = (nb - 1 - i) if rev else i
            return (jnp.maximum(b * step - 1, 0), col)
        return pl.BlockSpec((HALO, SEG), imap)

    return [mk(O_XB // SEG), mk(O_CG // SEG), mk(O_XC // SEG)]


def _const_spec(shape):
    nd = len(shape)
    return pl.BlockSpec(shape, lambda i: (0,) * nd, pipeline_mode=pl.Buffered(1))


MIX_FWD_ROWS = 512
MIX_BWD_ROWS = 256


def _mix_block_rows(T, target):
    return _pick(T, target, CHUNK)


def _mix_fwd(p, x, lw, name, riders=()):
    T, D = x.shape
    N = p.shape[1]
    R = _mix_block_rows(T, MIX_FWD_ROWS)
    nb = T // R

    def body(p_ref, hxb, hcg, hxc, x_ref, cv_ref, bsb_ref, wcat_ref, wpool_ref, band_ref, wpa_ref, wpb_ref, wpc_ref,
             wo_ref, xo_ref, ya_ref, yb_ref, yc_ref, extb):
        i = pl.program_id(0)
        cv = cv_ref[...]
        r = _mixers(p_ref, hxb, hcg, hxc, cv, bsb_ref, wcat_ref, wpool_ref, band_ref, extb,
                    i == 0, i, R, False)
        merged = None
        for k, (act, w_ref, y_ref) in enumerate(((r["a"], wpa_ref, ya_ref), (r["b"], wpb_ref, yb_ref),
                                                 (r["c"], wpc_ref, yc_ref))):
            y = _dot(act.astype(BF16), w_ref[...]).astype(BF16)
            y_ref[...] = y
            term = _sigmoid(p_ref[:, O_G + k * D:O_G + (k + 1) * D]) * y
            merged = term if merged is None else merged + term
        xo_ref[...] = x_ref[...] + _dot(merged, wo_ref[...])

    row = lambda w: pl.BlockSpec((R, w), lambda i: (i, 0))
    consts = [lw["cvec"], lw["bsb"], lw["wcat"], lw["wpool"], _pool_bands(R, False), lw["wpa"], lw["wpb"], lw["wpc"],
              lw["wo"]]
    return _host_call(
        body, name, (nb,), riders,
        inputs=[p, p, p, p, x, *consts],
        in_specs=[row(N)] + _halo_specs(R, nb, False) + [row(D)] + [_const_spec(c.shape) for c in consts],
        out_specs=[row(D), row(D), row(D), row(D)],
        out_shape=[SDS((T, D), F32), SDS((T, D), BF16), SDS((T, D), BF16), SDS((T, D), BF16)],
        scratch_shapes=[pltpu.VMEM((HALO + R, SEG), F32)])


def _loss_head(x, final_g, target, name):
    T, D = x.shape
    bt = _pick(T, 512, 8)

    def body(x_ref, g_ref, t_ref, dx_ref, loss_ref, dg_ref):
        @pl.when(pl.program_id(0) == 0)
        def _():
            _zero(loss_ref)
            _zero(dg_ref)

        xv = x_ref[...]
        g = g_ref[...]
        rstd = lax.rsqrt(jnp.mean(xv * xv, axis=-1, keepdims=True) + RMS_EPS)
        xhat = xv * rstd
        err = xhat * g - t_ref[...]
        part = 0.5 * jnp.sum(jnp.sum(err * err, axis=-1, keepdims=True), axis=0, keepdims=True) / D
        loss_ref[...] += jnp.broadcast_to(part, loss_ref.shape)
        dy = err * (1.0 / D)
        dg_ref[0:1, :] += jnp.sum(dy * xhat, axis=0, keepdims=True)
        dxn = dy * g
        dx_ref[...] = rstd * (dxn - xhat * jnp.mean(dxn * xhat, axis=-1, keepdims=True))

    return pl.pallas_call(
        body, name=name, grid=(T // bt,),
        in_specs=[pl.BlockSpec((bt, D), lambda i: (i, 0)), _const_spec((1, D)), pl.BlockSpec((bt, D), lambda i: (i, 0))],
        out_specs=[pl.BlockSpec((bt, D), lambda i: (i, 0)), _const_spec((8, LANES)), _const_spec((8, D))],
        out_shape=[SDS((T, D), F32), SDS((8, LANES), F32), SDS((8, D), F32)],
        compiler_params=_cparams(("arbitrary",)),
    )(x, final_g.reshape(1, D), target)


V_LNG, V_LNB, V_CB, V_PS, V_CW0, V_CW1, V_CW2 = range(7)


def _mix_bwd(p, dxo, ya, yb, yc, lw, name):
    T, D = dxo.shape
    N = p.shape[1]
    R = _mix_block_rows(T, MIX_BWD_ROWS)
    nb = T // R

    def body(p_ref, hxb, hcg, hxc, dxo_ref, ya_ref, yb_ref, yc_ref, cv_ref, bsb_ref, wcat_ref, wcatt_ref,
             wpool_ref, band_ref, bandt_ref, wpa_ref, wpb_ref, wpc_ref, wo_ref,
             dp_ref, acts_ref, mrg_ref, dys_ref, gwc_ref, gbs_ref, gwpool_ref, gvec_ref,
             extb, extdy, cdy, cq, bsacc):
        i = pl.program_id(0)
        blk = nb - 1 - i

        @pl.when(i == 0)
        def _():
            for ref in (gwc_ref, gwpool_ref, gvec_ref, cdy, cq, bsacc):
                _zero(ref)

        cv = cv_ref[...]
        r = _mixers(p_ref, hxb, hcg, hxc, cv, bsb_ref, wcat_ref, wpool_ref, band_ref, extb,
                    blk == 0, blk, R, True)

        dxo_b = dxo_ref[...].astype(BF16)
        dm = _dot_nt(dxo_b, wo_ref[...]).astype(BF16)
        ys = [ya_ref[...], yb_ref[...], yc_ref[...]]
        sig = [_sigmoid(p_ref[:, O_G + k * D:O_G + (k + 1) * D]) for k in range(3)]
        mrg_ref[...] = sig[0] * ys[0] + sig[1] * ys[1] + sig[2] * ys[2]
        dacts = []
        for k, (act, w_ref) in enumerate(((r["a"], wpa_ref), (r["b"], wpb_ref), (r["c"], wpc_ref))):
            dyk = dm * sig[k]
            dp_ref[:, O_G + k * D:O_G + (k + 1) * D] = dyk * ys[k] * (1.0 - sig[k])
            acts_ref[:, k * SEG:(k + 1) * SEG] = act.astype(BF16)
            dys_ref[:, k * D:(k + 1) * D] = dyk
            dacts.append(_dot_nt(dyk, w_ref[...]))
        da, db, dc = dacts

        def silu_bwd(dact, pre, z, s):
            return dact * (z * s), dact * pre * (s * (1.0 + z * (1.0 - s)))

        d_aout, dza = silu_bwd(da, r["a_out"], r["za"], r["sa"])
        dp_ref[:, O_ZA:O_ZA + SEG] = dza.astype(BF16)
        dp_ref[:, O_U:O_U + SEG] = (d_aout * r["sg"] * _gelu_grad(r["u"], r["tu"])).astype(BF16)
        d_sg = d_aout * r["ug"]
        dvns = []
        for ci in range(R // CHUNK):
            dsc = d_sg[ci * CHUNK:(ci + 1) * CHUNK]
            bsacc[...] += dsc
            dsc_b = dsc.astype(BF16)
            dst = jnp.concatenate([jnp.where(r["lane_group"] == g, dsc_b, r["zero_b"]) for g in range(GROUPS)], axis=0)
            dvns.append(_dot(wcatt_ref[...], dst))
            gwc_ref[...] += _dot_nt(dst, r["vn"][ci * CHUNK:(ci + 1) * CHUNK])
        d_vn = dvns[0] if len(dvns) == 1 else jnp.concatenate(dvns, axis=0)
        vhat = r["vhat"]
        gvec_ref[V_LNG:V_LNG + 1, :] += jnp.sum(d_vn * vhat, axis=0, keepdims=True)
        gvec_ref[V_LNB:V_LNB + 1, :] += jnp.sum(d_vn, axis=0, keepdims=True)
        d_vhat = d_vn * cv[C_LNG:C_LNG + 1, :]
        d_vg = r["rs"] * (d_vhat - jnp.mean(d_vhat, axis=-1, keepdims=True)
                          - vhat * jnp.mean(d_vhat * vhat, axis=-1, keepdims=True))
        dp_ref[:, O_V:O_V + SEG] = (d_vg * _gelu_grad(r["v"], r["tv"])).astype(BF16)

        d_bout, dzb = silu_bwd(db, r["b_out"], r["zb"], r["sb"])
        dp_ref[:, O_ZB:O_ZB + SEG] = dzb.astype(BF16)
        dp_ref[:, O_BG:O_BG + SEG] = (d_bout * r["yconv"]).astype(BF16)
        d_y = d_bout * r["bg"]
        gvec_ref[V_CB:V_CB + 1, :] += jnp.sum(d_y, axis=0, keepdims=True)
        gvec_ref[V_CW0:V_CW0 + 1, :] += jnp.sum(d_y * r["cx2"], axis=0, keepdims=True)
        gvec_ref[V_CW1:V_CW1 + 1, :] += jnp.sum(d_y * r["cx1"], axis=0, keepdims=True)
        gvec_ref[V_CW2:V_CW2 + 1, :] += jnp.sum(d_y * r["cx"], axis=0, keepdims=True)
        extdy[0:R, :] = d_y
        extdy[R:R + HALO, :] = cdy[...]
        d_cx = (cv[C_CW2:C_CW2 + 1, :] * d_y + cv[C_CW1:C_CW1 + 1, :] * extdy[pl.ds(1, R), :]
                + cv[C_CW0:C_CW0 + 1, :] * extdy[pl.ds(2, R), :])
        cdy[...] = d_y[0:HALO]
        dp_ref[:, O_CG:O_CG + SEG] = (d_cx * r["xb"]).astype(BF16)
        dp_ref[:, O_XB:O_XB + SEG] = (d_cx * r["cg"]).astype(BF16)

        d_cout, dzc = silu_bwd(dc, r["c_out"], r["zc"], r["sc"])
        dp_ref[:, O_ZC:O_ZC + SEG] = dzc.astype(BF16)
        gvec_ref[V_PS:V_PS + 1, :] += jnp.sum(d_cout * r["pw"], axis=0, keepdims=True)
        d_pw = (d_cout * cv[C_PS:C_PS + 1, :]).astype(BF16)
        dpool, scaled = [], []
        for gi, w in enumerate(POOL_WINDOWS):
            lo = gi * POOL_GROUP
            dpw_g = d_pw[:, lo:lo + POOL_GROUP]
            gwpool_ref[lo:lo + POOL_GROUP, :] += _dot_tn(r["pooled"][gi], dpw_g)
            dpg = _dot_nt(dpw_g, wpool_ref[gi])
            dpool.append(dpg)
            scaled.append((dpg * r["invs"][gi]).astype(BF16))
        q = jnp.concatenate(scaled, axis=1)
        q_ext = jnp.concatenate([q, cq[...], jnp.zeros((CHUNK - HALO, SEG), BF16)], axis=0)
        for gi, w in enumerate(POOL_WINDOWS):
            lo = gi * POOL_GROUP
            acc = _dot(bandt_ref[gi], q_ext[:, lo:lo + POOL_GROUP])
            dp_ref[:, O_XC + lo:O_XC + lo + POOL_GROUP] = (acc - dpool[gi]).astype(BF16)
        cq[...] = q[0:HALO]

        @pl.when(i == nb - 1)
        def _():
            rr = lax.broadcasted_iota(jnp.int32, gwc_ref.shape, 0) % CHUNK
            cc = lax.broadcasted_iota(jnp.int32, gwc_ref.shape, 1)
            gwc_ref[...] = jnp.where(cc <= rr, gwc_ref[...], 0.0)
            acc = bsacc[...]
            hi = acc.astype(BF16)
            lo_ = (acc - hi.astype(F32)).astype(BF16)
            sel = (lax.broadcasted_iota(jnp.int32, (SEG, LANES), 0) // HEAD
                   == lax.broadcasted_iota(jnp.int32, (SEG, LANES), 1)).astype(BF16)
            gbs_ref[...] = _dot(hi, sel) + _dot(lo_, sel)

    row = lambda w: pl.BlockSpec((R, w), lambda i: (nb - 1 - i, 0))
    consts = [lw["cvec"], lw["bsb"], lw["wcat"], lw["wcatt"], lw["wpool"], _pool_bands(R, False), _pool_bands(R, True),
              lw["wpa"], lw["wpb"], lw["wpc"], lw["wo"]]
    acc_shapes = [(GROUPS * CHUNK, CHUNK), (CHUNK, LANES), (SEG, POOL_GROUP), (8, SEG)]
    row_widths = [N, 3 * SEG, D, 3 * D]
    return pl.pallas_call(
        body, name=name, grid=(nb,),
        in_specs=([row(N)] + _halo_specs(R, nb, True) + [row(D), row(D), row(D), row(D)]
                  + [_const_spec(c.shape) for c in consts]),
        out_specs=[row(w) for w in row_widths] + [_const_spec(s) for s in acc_shapes],
        out_shape=[SDS((T, w), BF16) for w in row_widths] + [SDS(s, F32) for s in acc_shapes],
        scratch_shapes=[pltpu.VMEM((HALO + R, SEG), F32)] * 2
        + [pltpu.VMEM((HALO, SEG), F32), pltpu.VMEM((HALO, SEG), BF16), pltpu.VMEM((CHUNK, SEG), F32)],
        compiler_params=_cparams(("arbitrary",)),
    )(p, p, p, p, dxo, ya, yb, yc, *consts)


def _proj_wgrad(acts, merged, dys, dxo, name, riders=()):
    T, D = dxo.shape
    bk = _pick(T, WGRAD_ROWS // 2, 16)

    def body(a_ref, m_ref, dy_ref, dxo_ref, gwpa_ref, gwpb_ref, gwpc_ref, gwo_ref):
        @pl.when(pl.program_id(0) == 0)
        def _():
            for ref in (gwpa_ref, gwpb_ref, gwpc_ref, gwo_ref):
                _zero(ref)

        gwo_ref[...] += _dot_tn(m_ref[...], dxo_ref[...].astype(BF16))
        for k, ref in enumerate((gwpa_ref, gwpb_ref, gwpc_ref)):
            ref[...] += _dot_tn(a_ref[:, k * SEG:(k + 1) * SEG], dy_ref[:, k * D:(k + 1) * D])

    row = lambda w: pl.BlockSpec((bk, w), lambda i: (i, 0))
    shapes = [(SEG, D), (SEG, D), (SEG, D), (D, D)]
    return _host_call(
        body, name, (T // bk,), riders,
        inputs=[acts, merged, dys, dxo],
        in_specs=[row(3 * SEG), row(D), row(3 * D), row(D)],
        out_specs=[_const_spec(s) for s in shapes], out_shape=[SDS(s, F32) for s in shapes],
        scratch_shapes=[])


def _inproj_token_blocks(T):
    return T // _pick(T, BWD_X_ROWS, 16)


def _inproj_bwd_x(dp, win_t, x, norm_g, dxo, name, riders=(), blocks=None, fill=None):
    T, D = x.shape
    N = dp.shape[1]
    bt = _pick(T, BWD_X_ROWS, 16)
    b0, nblk = blocks if blocks else (0, T // bt)

    def compute(dp_ref, w_ref, x_ref, g_ref, dxo_ref, *rest):
        dx_ref, dg_ref = rest[-2:]

        @pl.when(pl.program_id(0) == 0)
        def _():
            _zero(dg_ref)

        dh = _dot(dp_ref[...], w_ref[...])
        xv = x_ref[...]
        rstd = lax.rsqrt(jnp.mean(xv * xv, axis=-1, keepdims=True) + RMS_EPS)
        xhat = xv * rstd
        dg_ref[0:1, :] += jnp.sum(dh * xhat, axis=0, keepdims=True)
        dxn = dh * g_ref[...]
        dx_ref[...] = dxo_ref[...] + rstd * (dxn - xhat * jnp.mean(dxn * xhat, axis=-1, keepdims=True))

    rows = pl.BlockSpec((bt, D), lambda i: (i + b0, 0))
    return _host_call(
        compute, name, (nblk,), riders,
        inputs=[dp, win_t, x, norm_g.reshape(1, D), dxo] + ([] if fill is None else [fill]),
        in_specs=[pl.BlockSpec((bt, N), lambda i: (i + b0, 0)), _const_spec((N, D)), rows, _const_spec((1, D)), rows]
        + ([] if fill is None else [_ANY]),
        out_specs=[rows, _const_spec((8, D))],
        out_shape=[SDS((T, D), F32), SDS((8, D), F32)],
        scratch_shapes=[],
        aliases={} if fill is None else {5: 0})


def _inproj_bwd_w(dp, h, name, riders=()):
    T, N = dp.shape
    D = h.shape[1]
    bn = _pick(N, 1536, LANES)
    bk = _pick(T, WGRAD_ROWS, 16)
    nk = T // bk

    def compute(dp_ref, h_ref, o_ref):
        @pl.when(pl.program_id(1) == 0)
        def _():
            _zero(o_ref)

        o_ref[...] += _dot_tn(dp_ref[...], h_ref[...])

    return _host_call(
        compute, name, (N // bn, nk), riders,
        inputs=[dp, h],
        in_specs=[pl.BlockSpec((bk, bn), lambda j, k: (k, j)), pl.BlockSpec((bk, D), lambda j, k: (k, 0))],
        out_specs=[pl.BlockSpec((bn, D), lambda j, k: (j, 0))],
        out_shape=[SDS((N, D), F32)],
        scratch_shapes=[])


def _chip_peer(x, y, j):
    px = (1 - x) if (j >> 1) else x
    py = (1 - y) if (j & 1) else y
    return px, py


def _blk(ref, kind, k, n):
    if kind == "rows":
        return ref.at[pl.ds(pl.multiple_of(k * n, 8), n)]
    return ref.at[:, pl.ds(pl.multiple_of(k * n, LANES), n)]


class _Exchange:
    def __init__(self, srcs, out_shapes, n_sems, build, alias=False):
        self.srcs, self.out_shapes, self.n_sems, self.build, self.alias = list(srcs), list(out_shapes), n_sems, build, alias
        self.late = False

    def starting_late(self):
        self.late = True
        return self


def _rider_plan(riders):
    inputs = [s for e in riders for s in e.srcs]
    out_shapes = [o for e in riders for o in e.out_shapes]
    sems = [pltpu.SemaphoreType.DMA((e.n_sems,)) for e in riders for _ in range(2)]

    def copies(in_refs, out_refs, sem_refs, late=False):
        cps, i, o = [], 0, 0
        for k, e in enumerate(riders):
            ni, no = len(e.srcs), len(e.out_shapes)
            if e.late == late:
                cps += e.build(in_refs[i:i + ni], out_refs[o:o + no], sem_refs[2 * k], sem_refs[2 * k + 1])
            i, o = i + ni, o + no
        return cps

    return inputs, out_shapes, sems, copies


_ANY = pl.BlockSpec(memory_space=pl.ANY)


def _host_call(compute, name, grid, riders, inputs, in_specs, out_specs, out_shape, scratch_shapes, aliases=None):
    assert not any(e.late for e in riders), "this host starts every rider in its first grid step"
    r_in, r_out, r_sems, copies = _rider_plan(riders)
    ni, no, ns = len(inputs), len(out_shape), len(scratch_shapes)

    def body(*refs):
        ins, rins = refs[:ni], refs[ni:ni + len(r_in)]
        outs = refs[ni + len(r_in):ni + len(r_in) + no]
        routs = refs[ni + len(r_in) + no:ni + len(r_in) + no + len(r_out)]
        scr = refs[ni + len(r_in) + no + len(r_out):]
        first = functools.reduce(lambda a, b: a & b, [pl.program_id(d) == 0 for d in range(len(grid))])
        last = functools.reduce(lambda a, b: a & b, [pl.program_id(d) == grid[d] - 1 for d in range(len(grid))])
        if riders:
            @pl.when(first)
            def _():
                for cp in copies(rins, routs, scr[ns:]):
                    cp.start()

        compute(*ins, *outs, *scr[:ns])

        if riders:
            @pl.when(last)
            def _():
                for cp in copies(rins, routs, scr[ns:]):
                    cp.wait()

    res = pl.pallas_call(
        body, name=name, grid=grid,
        in_specs=list(in_specs) + [_ANY] * len(r_in),
        out_specs=list(out_specs) + [_ANY] * len(r_out),
        out_shape=list(out_shape) + r_out,
        scratch_shapes=list(scratch_shapes) + r_sems,
        input_output_aliases=aliases or {},
        compiler_params=_cparams(("arbitrary",) * len(grid)),
    )(*inputs, *r_in)
    return res[:no], _split_riders(riders, res[no:])


def _split_riders(riders, flat):
    out, o = [], 0
    for e in riders:
        out.append(list(flat[o:o + len(e.out_shapes)]))
        o += len(e.out_shapes)
    return out


def _run_exchange(ex, name):
    n_in, n_out = len(ex.srcs), len(ex.out_shapes)

    def body(*refs):
        cps = ex.build(refs[:n_in], refs[n_in:n_in + n_out], refs[n_in + n_out], refs[n_in + n_out + 1])
        for cp in cps:
            cp.start()
        for cp in cps:
            cp.wait()

    return pl.pallas_call(
        body, name=name,
        in_specs=[_ANY] * n_in, out_specs=[_ANY] * n_out, out_shape=ex.out_shapes,
        input_output_aliases={i: i for i in range(n_in)} if ex.alias else {},
        scratch_shapes=[pltpu.SemaphoreType.DMA((ex.n_sems,)), pltpu.SemaphoreType.DMA((ex.n_sems,))],
        compiler_params=pltpu.CompilerParams(has_side_effects=True),
    )(*ex.srcs)


def _gather_sizes(shards, kinds):
    sizes = [s.shape[0] if k == "rows" else s.shape[1] for s, k in zip(shards, kinds)]
    fulls = [SDS((s.shape[0] * N_DEV,) + s.shape[1:], s.dtype) if k == "rows"
             else SDS((s.shape[0], s.shape[1] * N_DEV), s.dtype) for s, k in zip(shards, kinds)]
    return sizes, fulls


def _gather_direct(shards, kinds):
    n = len(shards)
    sizes, fulls = _gather_sizes(shards, kinds)

    def build(ins, outs, send_sems, recv_sems):
        x, y, c = _me()
        cps = []
        for a in range(n):
            mine = _blk(outs[a], kinds[a], 4 * x + 2 * y + c, sizes[a])
            cps.append(pltpu.make_async_copy(ins[a], mine, send_sems.at[5 * a + 4]))
            for j in range(N_CHIP):
                to = (x, y, 1 - c) if j == 0 else (*_chip_peer(x, y, j), c)
                cps.append(pltpu.make_async_remote_copy(
                    src_ref=ins[a], dst_ref=mine, send_sem=send_sems.at[5 * a + j], recv_sem=recv_sems.at[5 * a + j],
                    device_id=to, device_id_type=MESH))
        return cps

    return _Exchange(shards, fulls, 5 * n, build)


def _gather_everywhere(shards, kinds):
    n = len(shards)
    sizes, fulls = _gather_sizes(shards, kinds)

    def build(ins, outs, send_sems, recv_sems):
        x, y, c = _me()
        cps = []
        for a in range(n):
            mine = _blk(outs[a], kinds[a], 4 * x + 2 * y + c, sizes[a])
            cps.append(pltpu.make_async_copy(ins[a], mine, send_sems.at[N_DEV * a]))
            for d in range(1, N_DEV):
                to = ((1 - x) if d & 4 else x, (1 - y) if d & 2 else y, (1 - c) if d & 1 else c)
                cps.append(pltpu.make_async_remote_copy(
                    src_ref=ins[a], dst_ref=mine, send_sem=send_sems.at[N_DEV * a + d],
                    recv_sem=recv_sems.at[N_DEV * a + d], device_id=to, device_id_type=MESH))
        return cps

    return _Exchange(shards, fulls, N_DEV * n, build)


def _gather_complete(partials, kinds, sizes):
    n = len(partials)

    def build(ins, outs, send_sems, recv_sems):
        x, y, c = _me()
        cps = []
        for a in range(n):
            def blocks(k, a=a):
                return _blk(ins[a], kinds[a], k, sizes[a]), _blk(outs[a], kinds[a], k, sizes[a])

            from_chips = [4 * px + 2 * py + c for px, py in (_chip_peer(x, y, j) for j in (1, 2, 3))]
            for t, k in enumerate([4 * x + 2 * y + c, 4 * x + 2 * y + 1 - c] + from_chips):
                src, dst = blocks(k)
                cps.append(pltpu.make_async_copy(src, dst, send_sems.at[N_DEV * a + 3 + t]))
            for t, k in enumerate(from_chips):
                src, dst = blocks(k)
                cps.append(pltpu.make_async_remote_copy(
                    src_ref=src, dst_ref=dst, send_sem=send_sems.at[N_DEV * a + t], recv_sem=recv_sems.at[N_DEV * a + t],
                    device_id=(x, y, 1 - c), device_id_type=MESH))
        return cps

    return _Exchange(partials, [SDS(f.shape, f.dtype) for f in partials], N_DEV * n, build)


def _gather_forward(fulls, kinds, sizes):
    n = len(fulls)

    def build(ins, outs, send_sems, recv_sems):
        x, y, c = _me()
        cps = []
        for a in range(n):
            for j in (1, 2, 3):
                px, py = _chip_peer(x, y, j)
                k = 4 * px + 2 * py + c
                cps.append(pltpu.make_async_remote_copy(
                    src_ref=_blk(ins[a], kinds[a], k, sizes[a]), dst_ref=_blk(outs[a], kinds[a], k, sizes[a]),
                    send_sem=send_sems.at[3 * a + j - 1], recv_sem=recv_sems.at[3 * a + j - 1],
                    device_id=(x, y, 1 - c), device_id_type=MESH))
        return cps

    return _Exchange(fulls, [SDS(f.shape, f.dtype) for f in fulls], 3 * n, build, alias=True)


def _sibling_exchange(grads, kinds, sizes):
    n = len(grads)

    def blk_shape(a):
        g = grads[a]
        return (sizes[a],) + g.shape[1:] if kinds[a] == "rows" else (g.shape[0], sizes[a])

    def build(ins, outs, send_sems, recv_sems):
        x, y, c = _me()
        cps = []
        for a in range(n):
            for q in range(N_CHIP):
                cps.append(pltpu.make_async_remote_copy(
                    src_ref=_blk(ins[a], kinds[a], 2 * q + (1 - c), sizes[a]), dst_ref=outs[a].at[q],
                    send_sem=send_sems.at[N_CHIP * a + q], recv_sem=recv_sems.at[N_CHIP * a + q],
                    device_id=(x, y, 1 - c), device_id_type=MESH))
        return cps

    return _Exchange(grads, [SDS((N_CHIP,) + blk_shape(a), F32) for a in range(n)], N_CHIP * n, build)


def _chip_partial(g, r1, kind, size, cidx, name):
    if kind == "rows":
        rows, cols = size, g.shape[1]
        g3 = g.reshape(N_DEV, rows, cols)
        rb = _pick(rows, 512, 16)
        g_spec = pl.BlockSpec((1, rb, cols), lambda q, j, c: (2 * q + c[0], j, 0))
        grid = (N_CHIP, rows // rb)
        blk = (1, rb, cols)
        imap = lambda q, j, c: (q, j, 0)
    else:
        rows, cols = g.shape[0], size
        g3 = g
        g_spec = pl.BlockSpec((rows, cols), lambda q, j, c: (0, 2 * q + c[0]))
        grid = (N_CHIP, 1)
        blk = (1, rows, cols)
        imap = lambda q, j, c: (q, 0, 0)

    def body(c_ref, g_ref, r_ref, p_ref, pb_ref):
        s = g_ref[...].reshape(blk) + r_ref[...]
        p_ref[...] = s
        pb_ref[...] = s.astype(BF16)

    return pl.pallas_call(
        body, name=name,
        grid_spec=pltpu.PrefetchScalarGridSpec(
            num_scalar_prefetch=1, grid=grid,
            in_specs=[g_spec, pl.BlockSpec(blk, imap)],
            out_specs=[pl.BlockSpec(blk, imap), pl.BlockSpec(blk, imap)]),
        out_shape=[SDS((N_CHIP, rows, cols), F32), SDS((N_CHIP, rows, cols), BF16)],
        compiler_params=_cparams(("arbitrary", "arbitrary")),
    )(cidx, g3, r1)


def _chip_exchange(parts):
    n = len(parts)
    m = N_CHIP - 1

    def build(ins, outs, send_sems, recv_sems):
        x, y, c = _me()
        cps = []
        for a in range(n):
            for j in (1, 2, 3):
                px, py = _chip_peer(x, y, j)
                cps.append(pltpu.make_async_remote_copy(
                    src_ref=ins[a].at[2 * px + py], dst_ref=outs[a].at[j - 1], send_sem=send_sems.at[m * a + j - 1],
                    recv_sem=recv_sems.at[m * a + j - 1], device_id=(px, py, c), device_id_type=MESH))
        return cps

    return _Exchange(parts, [SDS((m,) + p.shape[1:], BF16) for p in parts], m * n, build)


def _grad_total(part, r2, qidx, name):
    _, rows, cols = part.shape
    rb = _pick(rows, 512, 16)

    def body(q_ref, p_ref, r_ref, o_ref):
        s = p_ref[0]
        for j in range(N_CHIP - 1):
            s = s + r_ref[j].astype(F32)
        o_ref[...] = s

    return pl.pallas_call(
        body, name=name,
        grid_spec=pltpu.PrefetchScalarGridSpec(
            num_scalar_prefetch=1, grid=(rows // rb,),
            in_specs=[pl.BlockSpec((1, rb, cols), lambda i, q: (q[0], i, 0)),
                      pl.BlockSpec((N_CHIP - 1, rb, cols), lambda i, q: (0, i, 0))],
            out_specs=pl.BlockSpec((rb, cols), lambda i, q: (i, 0))),
        out_shape=SDS((rows, cols), F32),
        compiler_params=_cparams(("arbitrary",)),
    )(qidx, part, r2)


def _all_reduce_small(pack, name):
    rows = pack.shape[0]
    rs = rows // N_DEV
    assert rs * N_DEV == rows and rs % 8 == 0

    def body(x_ref, o_ref, rbuf, red, send1, recv1, send2, recv2):
        x, y, c = _me()
        me = 4 * x + 2 * y + c

        def peer(d):
            px = (1 - x) if (d >> 2) & 1 else x
            py = (1 - y) if (d >> 1) & 1 else y
            pc = (1 - c) if d & 1 else c
            return px, py, pc

        def sl(ref, k):
            return ref.at[pl.ds(pl.multiple_of(k * rs, 8), rs)]

        phase1 = []
        for d in range(1, N_DEV):
            px, py, pc = peer(d)
            phase1.append(pltpu.make_async_remote_copy(
                src_ref=sl(x_ref, 4 * px + 2 * py + pc), dst_ref=rbuf.at[d], send_sem=send1.at[d], recv_sem=recv1.at[d],
                device_id=(px, py, pc), device_id_type=MESH))
        for cp in phase1:
            cp.start()
        acc = sl(x_ref, me)[...]
        for cp in phase1:
            cp.wait()
        for d in range(1, N_DEV):
            acc = acc + rbuf[d]
        red[...] = acc
        sl(o_ref, me)[...] = acc
        phase2 = []
        for d in range(1, N_DEV):
            px, py, pc = peer(d)
            phase2.append(pltpu.make_async_remote_copy(
                src_ref=red, dst_ref=sl(o_ref, me), send_sem=send2.at[d], recv_sem=recv2.at[d],
                device_id=(px, py, pc), device_id_type=MESH))
        for cp in phase2:
            cp.start()
        for cp in phase2:
            cp.wait()

    vm = pl.BlockSpec(memory_space=pltpu.VMEM)
    return pl.pallas_call(
        body, name=name, in_specs=[vm], out_specs=vm, out_shape=SDS(pack.shape, F32),
        scratch_shapes=[pltpu.VMEM((N_DEV, rs, LANES), F32), pltpu.VMEM((rs, LANES), F32),
                        pltpu.SemaphoreType.DMA((N_DEV,)), pltpu.SemaphoreType.DMA((N_DEV,)),
                        pltpu.SemaphoreType.DMA((N_DEV,)), pltpu.SemaphoreType.DMA((N_DEV,))],
        compiler_params=_cparams(None, has_side_effects=True),
    )(pack)


def _adamw(w, g, m, v, name):
    rows, cols = w.shape
    rb = rows if rows * cols * 4 <= ADAMW_WHOLE_BYTES else _pick(rows, 256, 8)
    c1 = 1.0 / (1.0 - ADAM_B1 ** ADAM_STEP)
    c2 = 1.0 / (1.0 - ADAM_B2 ** ADAM_STEP)

    def body(w_ref, g_ref, m_ref, v_ref, d_ref, mo_ref, vo_ref):
        gv = g_ref[...]
        mn = ADAM_B1 * m_ref[...] + (1.0 - ADAM_B1) * gv
        vn = ADAM_B2 * v_ref[...] + (1.0 - ADAM_B2) * (gv * gv)
        mo_ref[...] = mn
        vo_ref[...] = vn
        d_ref[...] = -ADAM_LR * ((mn * c1) / (jnp.sqrt(vn * c2) + ADAM_EPS) + ADAM_WD * w_ref[...])

    spec = pl.BlockSpec((rb, cols), lambda i: (i, 0))
    return pl.pallas_call(
        body, name=name, grid=(rows // rb,),
        in_specs=[spec] * 4, out_specs=[spec] * 3, out_shape=[SDS((rows, cols), F32)] * 3,
        compiler_params=_cparams(("arbitrary",)),
    )(w, g, m, v)


def _pad_rows(a, mult=8):
    r = (-a.shape[0]) % mult
    return a if r == 0 else jnp.pad(a, ((0, r), (0, 0)))


def _as_lanes(a):
    flat = a.reshape(-1)
    pad = (-flat.shape[0]) % (8 * LANES)
    if pad:
        flat = jnp.pad(flat, (0, pad))
    return flat.reshape(-1, LANES)


def kernel(x, norm_g, w_in, ln_g, ln_b, w_s, b_s, conv_w, conv_b, w_pool, pool_scale, w_pa, w_pb, w_pc, w_o, final_g, loss_target, m_norm_g, m_w_in, m_ln_g, m_ln_b, m_w_s, m_b_s, m_conv_w, m_conv_b, m_w_pool, m_pool_scale, m_w_pa, m_w_pb, m_w_pc, m_w_o, m_final_g, v_norm_g, v_w_in, v_ln_g, v_ln_b, v_w_s, v_b_s, v_conv_w, v_conv_b, v_w_pool, v_pool_scale, v_w_pa, v_w_pb, v_w_pc, v_w_o, v_final_g):
    L = w_in.shape[0]
    D = x.shape[-1]
    n_loc = w_in.shape[2]
    pc_loc = w_pa.shape[2]
    x0 = x[0]
    target = loss_target[0]
    xi, yi, ci = _me()
    cidx = jnp.reshape(ci, (1,)).astype(jnp.int32)
    qidx = jnp.reshape(2 * xi + yi, (1,)).astype(jnp.int32)

    kinds5 = ["rows", "cols", "cols", "cols", "rows"]

    def layer_shards(l):
        return [w_in[l].T.astype(BF16), w_pa[l].astype(BF16), w_pb[l].astype(BF16), w_pc[l].astype(BF16),
                w_o[l].astype(BF16)]

    def gathered(direct, shards, kinds, l):
        sizes, _ = _gather_sizes(shards, kinds)
        return _run_exchange(_gather_forward(direct, kinds, sizes), f"weights_forward_{l}")

    cw_loc = _pad_rows(conv_w.reshape(L * CONV_TAPS, -1))
    cw_loc = jnp.pad(cw_loc, ((0, 0), (0, LANES - cw_loc.shape[1])))
    causal = jnp.tril(jnp.ones((CHUNK, CHUNK), dtype=bool))

    chip_order = jnp.stack([2 * xi + yi] + [2 * px + py for px, py in (_chip_peer(xi, yi, j) for j in (1, 2, 3))])
    sh0 = layer_shards(0)
    rest0, krest0 = sh0[1:] + [cw_loc], kinds5[1:] + ["rows"]
    sh1 = layer_shards(1) if L > 1 else None
    late = [_gather_direct(sh1[:1], kinds5[:1]).starting_late()] if sh1 else []
    (p0, h0, win_t0), delivered = _inproj_gathering(x0, norm_g[0], sh0[0], chip_order.astype(jnp.int32), "inproj_fwd_0",
                                                    [_gather_everywhere(rest0, krest0)] + late)
    rest0_full = delivered[0]
    win1_partial = delivered[1] if sh1 else None
    cw_all = rest0_full[-1].reshape(N_DEV, -1, LANES)[:, :L * CONV_TAPS, :conv_w.shape[2]]
    conv_w_full = jnp.transpose(cw_all, (1, 0, 2)).reshape(L, CONV_TAPS, -1)

    def make_layer(l, full5):
        win_t, wpa, wpb, wpc, wo = full5
        wm = jnp.where(causal, w_s[l], 0.0)
        cvec = jnp.concatenate([ln_g[l][None], ln_b[l][None], conv_w_full[l], conv_b[l][None], pool_scale[l][None],
                                jnp.zeros((C_ROWS - 7, SEG), F32)], axis=0)
        return dict(
            win_t=win_t, wpa=wpa, wpb=wpb, wpc=wpc, wo=wo, cvec=cvec,
            bsb=jnp.repeat(b_s[l].T, HEAD, axis=1),
            wcat=jnp.transpose(wm, (1, 0, 2)).reshape(CHUNK, GROUPS * CHUNK).astype(BF16),
            wcatt=jnp.transpose(wm, (2, 0, 1)).reshape(CHUNK, GROUPS * CHUNK).astype(BF16),
            wpool=w_pool[l].astype(BF16))

    layers = [make_layer(0, [win_t0] + list(rest0_full[:4]))]
    xs, saved = [x0], []
    for l in range(L):
        lw = layers[l]
        if l == 0:
            p, h = p0, h0
        else:
            (p, h), _ = _inproj(xs[-1], norm_g[l], lw["win_t"], f"inproj_fwd_{l}")
        nxt = layer_shards(l + 1) if l + 1 < L else None
        if nxt and l == 0:
            riders = [_gather_complete(win1_partial, kinds5[:1], [n_loc]), _gather_everywhere(nxt[1:], kinds5[1:])]
        else:
            riders = [_gather_direct(nxt, kinds5)] if nxt else []
        (xn, ya, yb, yc), delivered = _mix_fwd(p, xs[-1], lw, f"mix_fwd_{l}", riders)
        if nxt and l == 0:
            layers.append(make_layer(l + 1, delivered[0] + delivered[1]))
        elif nxt:
            layers.append(make_layer(l + 1, gathered(delivered[0], nxt, kinds5, l + 1)))
        saved.append((p, h, ya, yb, yc))
        xs.append(xn)
    dx, loss_acc, dfg_acc = _loss_head(xs[-1], final_g, target, "loss_head")

    rs_sizes = [n_loc, pc_loc, pc_loc, pc_loc, w_o.shape[1]]
    await_sibling, await_chips = [], []
    partial_of, from_chips = {}, {}
    serial = [0]

    def riders_now():
        riders, plan = [], []
        for grp in await_chips:
            riders.append(_chip_exchange([partial_of[t][1] for t, _, _, _ in grp]))
            plan.append(("chips", grp))
        for grp in await_sibling:
            riders.append(_sibling_exchange([g for _, g, _, _ in grp], [k for _, _, k, _ in grp], [s for _, _, _, s in grp]))
            plan.append(("sibling", grp))
        del await_chips[:], await_sibling[:]
        return riders, plan

    def absorb(plan, delivered):
        for (what, grp), res in zip(plan, delivered):
            for (t, g, k, s), r in zip(grp, res):
                if what == "chips":
                    from_chips[t] = r
                else:
                    partial_of[t] = _chip_partial(g, r, k, s, cidx, f"grad_chip_partial_{t[0]}_{t[1]}")
            if what == "sibling":
                await_chips.append(grp)

    small = [None] * L
    for l in reversed(range(L)):
        lw = layers[l]
        p, h, ya, yb, yc = saved[l]
        dp, acts, merged, dys, gwc, gbs, gwpool, gvec = _mix_bwd(p, dx, ya, yb, yc, lw, f"mix_bwd_{l}")
        riders, plan = riders_now()
        (gwpa, gwpb, gwpc, gwo), delivered = _proj_wgrad(acts, merged, dys, dx, f"proj_wgrad_{l}", riders)
        absorb(plan, delivered)
        await_sibling.append([((l, a), g, kinds5[a], rs_sizes[a]) for a, g in ((1, gwpa), (2, gwpb), (3, gwpc), (4, gwo))])

        def bwd_x(dxo, pieces):
            nt = _inproj_token_blocks(dxo.shape[0])
            pieces = min(pieces, nt)
            done, dng, b0 = None, None, 0
            for k in range(pieces):
                cnt = (nt - b0) // (pieces - k)
                riders, plan = riders_now()
                (done, dng_k), delivered = _inproj_bwd_x(dp, lw["win_t"], xs[l], norm_g[l], dxo, f"inproj_bwd_x_{l}_{k}",
                                                         riders, blocks=(b0, cnt), fill=done)
                absorb(plan, delivered)
                dng = dng_k if dng is None else dng + dng_k
                b0 += cnt
            return done, dng

        def bwd_w():
            riders, plan = riders_now()
            (gwin_t,), delivered = _inproj_bwd_w(dp, h, f"inproj_bwd_w_{l}", riders)
            absorb(plan, delivered)
            await_sibling.append([((l, 0), gwin_t, kinds5[0], rs_sizes[0])])

        if l == L - 1:
            dx, dng = bwd_x(dx, 1)
            bwd_w()
        else:
            bwd_w()
            dx, dng = bwd_x(dx, 2 if l == 0 else 1)
        small[l] = dict(norm_g=dng[0], ln_g=gvec[V_LNG], ln_b=gvec[V_LNB], w_s=gwc, b_s=gbs, conv_w=gvec[V_CW0:V_CW0 + 3],
                        conv_b=gvec[V_CB], w_pool=gwpool, pool_scale=gvec[V_PS])
    while await_sibling or await_chips:
        riders, plan = riders_now()
        delivered = []
        for ex in riders:
            delivered.append(_run_exchange(ex, f"grad_exchange_tail_{serial[0]}"))
            serial[0] += 1
        absorb(plan, delivered)
    grad_x = dx[None]
    big_grads = []
    for l in range(L):
        tot = [_grad_total(partial_of[(l, a)][0], from_chips[(l, a)], qidx, f"grad_total_{l}_{a}") for a in range(5)]
        big_grads.append([tot[0].T,
                          tot[1].reshape(SEG, pc_loc), tot[2].reshape(SEG, pc_loc), tot[3].reshape(SEG, pc_loc),
                          tot[4]])

    names = ["norm_g", "ln_g", "ln_b", "w_s", "b_s", "conv_w", "conv_b", "w_pool", "pool_scale"]
    pieces = [_as_lanes(jnp.stack([small[l][nm] for l in range(L)])) for nm in names]
    pieces += [_as_lanes(dfg_acc[0]), loss_acc]
    sizes = [pc.shape[0] for pc in pieces]
    pack = jnp.concatenate(pieces, axis=0)
    pack = _pad_rows(pack, 8 * N_DEV)
    red = _all_reduce_small(pack, "small_grads_all_reduce")
    offs = [0]
    for s in sizes:
        offs.append(offs[-1] + s)

    def unpack(i, shape):
        n = math.prod(shape)
        return red[offs[i]:offs[i + 1]].reshape(-1)[:n].reshape(shape)

    g_norm_g = unpack(0, (L, D))
    g_ln_g = unpack(1, (L, SEG))
    g_ln_b = unpack(2, (L, SEG))
    g_w_s = unpack(3, (L, GROUPS, CHUNK, CHUNK))
    g_b_s = jnp.transpose(unpack(4, (L, CHUNK, LANES))[:, :, :GROUPS], (0, 2, 1))
    g_conv_w_full = unpack(5, (L, CONV_TAPS, SEG))
    g_conv_b = unpack(6, (L, SEG))
    g_w_pool = unpack(7, (L, len(POOL_WINDOWS), POOL_GROUP, POOL_GROUP))
    g_pool_scale = unpack(8, (L, SEG))
    g_final_g = unpack(9, (D,))
    loss = red[offs[10], 0]
    dev = 4 * xi + 2 * yi + ci
    g_conv_w = lax.dynamic_slice_in_dim(g_conv_w_full, dev * conv_w.shape[2], conv_w.shape[2], axis=2)

    g_w_in = jnp.stack([big_grads[l][0] for l in range(L)])
    g_w_pa = jnp.stack([big_grads[l][1] for l in range(L)])
    g_w_pb = jnp.stack([big_grads[l][2] for l in range(L)])
    g_w_pc = jnp.stack([big_grads[l][3] for l in range(L)])
    g_w_o = jnp.stack([big_grads[l][4] for l in range(L)])

    grads = dict(norm_g=g_norm_g, w_in=g_w_in, ln_g=g_ln_g, ln_b=g_ln_b, w_s=g_w_s, b_s=g_b_s, conv_w=g_conv_w,
                 conv_b=g_conv_b, w_pool=g_w_pool, pool_scale=g_pool_scale, w_pa=g_w_pa, w_pb=g_w_pb, w_pc=g_w_pc,
                 w_o=g_w_o, final_g=g_final_g)
    weights = dict(norm_g=norm_g, w_in=w_in, ln_g=ln_g, ln_b=ln_b, w_s=w_s, b_s=b_s, conv_w=conv_w, conv_b=conv_b,
                   w_pool=w_pool, pool_scale=pool_scale, w_pa=w_pa, w_pb=w_pb, w_pc=w_pc, w_o=w_o, final_g=final_g)
    ms = dict(norm_g=m_norm_g, w_in=m_w_in, ln_g=m_ln_g, ln_b=m_ln_b, w_s=m_w_s, b_s=m_b_s, conv_w=m_conv_w,
              conv_b=m_conv_b, w_pool=m_w_pool, pool_scale=m_pool_scale, w_pa=m_w_pa, w_pb=m_w_pb, w_pc=m_w_pc,
              w_o=m_w_o, final_g=m_final_g)
    vs = dict(norm_g=v_norm_g, w_in=v_w_in, ln_g=v_ln_g, ln_b=v_ln_b, w_s=v_w_s, b_s=v_b_s, conv_w=v_conv_w,
              conv_b=v_conv_b, w_pool=v_w_pool, pool_scale=v_pool_scale, w_pa=v_w_pa, w_pb=v_w_pb, w_pc=v_w_pc,
              w_o=v_w_o, final_g=v_final_g)
    order = ["norm_g", "w_in", "ln_g", "ln_b", "w_s", "b_s", "conv_w", "conv_b", "w_pool", "pool_scale", "w_pa", "w_pb",
             "w_pc", "w_o", "final_g"]

    delta, new_m, new_v = {}, {}, {}
    big = ["w_in", "w_pa", "w_pb", "w_pc", "w_o"]
    for nm in big:
        shp = weights[nm].shape
        two = lambda a: a.reshape(-1, shp[-1])
        d, mn, vn = _adamw(two(weights[nm]), two(grads[nm]), two(ms[nm]), two(vs[nm]), f"adamw_{nm}")
        delta[nm], new_m[nm], new_v[nm] = d.reshape(shp), mn.reshape(shp), vn.reshape(shp)
    rest = [nm for nm in order if nm not in big]
    cat = lambda src: jnp.concatenate([_as_lanes(src[nm]) for nm in rest], axis=0)
    d, mn, vn = _adamw(cat(weights), cat(grads), cat(ms), cat(vs), "adamw_small")
    off = 0
    for nm in rest:
        shp = weights[nm].shape
        n = math.prod(shp)
        rows = _as_lanes(weights[nm]).shape[0]
        cut = lambda a: a[off:off + rows].reshape(-1)[:n].reshape(shp)
        delta[nm], new_m[nm], new_v[nm] = cut(d), cut(mn), cut(vn)
        off += rows

    return (loss, grad_x, *[grads[nm] for nm in order], *[delta[nm] for nm in order],
            *[new_m[nm] for nm in order], *[new_v[nm] for nm in order])
```

```python
import functools
import math

import numpy as np
import jax
import jax.numpy as jnp
from jax import lax
from jax.experimental import pallas as pl
from jax.experimental.pallas import tpu as pltpu

F32 = jnp.float32
BF16 = jnp.bfloat16
SDS = jax.ShapeDtypeStruct
MESH = pl.DeviceIdType.MESH

SEG = 512
CHUNK = 128
GROUPS = 8
HEAD = SEG // GROUPS
POOL_WINDOWS = (2, 4, 8, 16)
POOL_GROUP = SEG // len(POOL_WINDOWS)
CONV_TAPS = 3
HALO = 16
RMS_EPS = 1e-6
LN_EPS = 1e-5
ADAM_LR, ADAM_B1, ADAM_B2, ADAM_EPS, ADAM_WD, ADAM_STEP = 0.001, 0.9, 0.999, 1e-08, 0.01, 10

O_U, O_V, O_ZA, O_XB, O_BG, O_CG, O_ZB, O_XC, O_ZC, O_G = (SEG * i for i in range(10))

N_DEV = 8
N_CHIP = 4
LANES = 128
VMEM_LIMIT = 48 * 1024 * 1024
ADAMW_WHOLE_BYTES = 2 * 1024 * 1024
INPROJ_ROWS = 1024
BWD_X_ROWS = 512
WGRAD_ROWS = 2048


def _cparams(sem=None, **kw):
    return pltpu.CompilerParams(dimension_semantics=sem, vmem_limit_bytes=VMEM_LIMIT, **kw)


def _pick(total, target, mult):
    best = None
    for d in range(mult, min(total, target) + 1, mult):
        if total % d == 0:
            best = d
    assert best is not None, (total, target, mult)
    return best


def _dot(a, b):
    return jnp.dot(a, b, preferred_element_type=F32)


def _dot_nt(a, b):
    return lax.dot_general(a, b, (((1,), (1,)), ((), ())), preferred_element_type=F32)


def _dot_tn(a, b):
    return lax.dot_general(a, b, (((0,), (0,)), ((), ())), preferred_element_type=F32)


def _zero(ref):
    ref[...] = jnp.zeros(ref.shape, ref.dtype)


def _sigmoid(x):
    return 1.0 / (1.0 + jnp.exp(-x))


_GELU_C = math.sqrt(2.0 / math.pi)


def _gelu(x):
    t = jnp.tanh(_GELU_C * (x + 0.044715 * x * x * x))
    return 0.5 * x * (1.0 + t), t


def _gelu_grad(x, t):
    return 0.5 * (1.0 + t) + 0.5 * x * (1.0 - t * t) * _GELU_C * (1.0 + 3.0 * 0.044715 * x * x)


def _me():
    return lax.axis_index("x"), lax.axis_index("y"), lax.axis_index("c")


def _inproj(x, norm_g, win_t, name, riders=()):
    T, D = x.shape
    N = win_t.shape[0]
    bt = _pick(T, INPROJ_ROWS, 16)
    bn = _pick(N, 1536, LANES)
    grid = (T // bt, N // bn)

    def compute(x_ref, g_ref, w_ref, p_ref, h_ref, hs_ref):
        @pl.when(pl.program_id(1) == 0)
        def _():
            xv = x_ref[...]
            rstd = lax.rsqrt(jnp.mean(xv * xv, axis=-1, keepdims=True) + RMS_EPS)
            hb = (xv * rstd * g_ref[...]).astype(BF16)
            hs_ref[...] = hb
            h_ref[...] = hb

        p_ref[...] = _dot_nt(hs_ref[...], w_ref[...]).astype(BF16)

    return _host_call(
        compute, name, grid, riders,
        inputs=[x, norm_g.reshape(1, D), win_t],
        in_specs=[pl.BlockSpec((bt, D), lambda i, j: (i, 0)),
                  pl.BlockSpec((1, D), lambda i, j: (0, 0)),
                  pl.BlockSpec((bn, D), lambda i, j: (j, 0))],
        out_specs=[pl.BlockSpec((bt, bn), lambda i, j: (i, j)),
                   pl.BlockSpec((bt, D), lambda i, j: (i, 0))],
        out_shape=[SDS((T, N), BF16), SDS((T, D), BF16)],
        scratch_shapes=[pltpu.VMEM((bt, D), BF16)])


def _inproj_gathering(x, norm_g, w_loc, chip_order, dev, name, riders=()):
    T, D = x.shape
    n = w_loc.shape[0]
    N = n * N_DEV
    cw = 2 * n
    bt = _pick(T, INPROJ_ROWS, 16)
    nt = T // bt
    r_in, r_out, r_sems, copies = _rider_plan(riders)
    n_rin, n_rout = len(r_in), len(r_out)

    def body(q_ref, x_ref, g_ref, wloc_ref, placed_ref, *rest):
        rins = rest[:n_rin]
        p_ref, h_ref, wfull_ref = rest[n_rin:n_rin + 3]
        routs = rest[n_rin + 3:n_rin + 3 + n_rout]
        hs_ref, wbuf, send_sems, recv_sems, loc_sems = rest[n_rin + 3 + n_rout:n_rin + 8 + n_rout]
        rsems = rest[n_rin + 8 + n_rout:]
        j, i = pl.program_id(0), pl.program_id(1)
        cx, cy, cc = _me()
        sibling = (cx, cy, 1 - cc)

        def rows(k):
            return wfull_ref.at[pl.ds(pl.multiple_of(k * n, 8), n)]

        def shard_copy(slot, src, k, to):
            return pltpu.make_async_remote_copy(src_ref=src, dst_ref=rows(k), send_sem=send_sems.at[slot],
                                                recv_sem=recv_sems.at[slot], device_id=to, device_id_type=MESH)

        me = 4 * cx + 2 * cy + cc
        sends = [shard_copy(0, wloc_ref, me, sibling)]
        for jj in (1, 2, 3):
            sends.append(shard_copy(jj, wloc_ref, me, (*_chip_peer(cx, cy, jj), cc)))

        def forward(jj):
            px, py = _chip_peer(cx, cy, jj)
            k = 4 * px + 2 * py + cc
            return shard_copy(3 + jj, rows(k), k, sibling)

        def load_chunk(q):
            cp = pltpu.make_async_copy(wfull_ref.at[pl.ds(pl.multiple_of(q * cw, 8), cw)], wbuf, loc_sems.at[0])
            cp.start()
            cp.wait()

        keep_h = pltpu.make_async_copy(hs_ref, h_ref, loc_sems.at[1])

        @pl.when((j == 0) & (i == 0))
        def _():
            for cp in sends:
                cp.start()
            for cp in copies(rins, routs, rsems):
                cp.start()
            sends[0].wait_recv()
            load_chunk(q_ref[0])

        for jj in (1, 2, 3):
            @pl.when((j == jj) & (i == 0))
            def _(jj=jj):
                sends[jj].wait_recv()
                fwd = forward(jj)
                fwd.start()
                fwd.wait_recv()
                load_chunk(q_ref[jj])

        tok = pl.ds(pl.multiple_of(i * bt, bt), bt)

        @pl.when(j == 0)
        def _():
            xv = x_ref[...]
            rstd = lax.rsqrt(jnp.mean(xv * xv, axis=-1, keepdims=True) + RMS_EPS)
            hs_ref[tok, :] = (xv * rstd * g_ref[...]).astype(BF16)

        @pl.when((j == 0) & (i == nt - 1))
        def _():
            keep_h.start()

        p_ref[...] = _dot_nt(hs_ref[tok, :], wbuf[...]).astype(BF16)

        @pl.when((j == N_CHIP - 1) & (i == nt - 1))
        def _():
            keep_h.wait()
            for cp in sends:
                cp.wait_send()
            for jj in (1, 2, 3):
                forward(jj).wait_send()
            for cp in copies(rins, routs, rsems):
                cp.wait()

    res = pl.pallas_call(
        body, name=name,
        grid_spec=pltpu.PrefetchScalarGridSpec(
            num_scalar_prefetch=1, grid=(N_CHIP, nt),
            in_specs=[pl.BlockSpec((bt, D), lambda j, i, q: (jnp.where(j == 0, i, nt - 1), 0)),
                      pl.BlockSpec((1, D), lambda j, i, q: (0, 0)), _ANY, _ANY] + [_ANY] * n_rin,
            out_specs=[pl.BlockSpec((bt, cw), lambda j, i, q: (i, q[j])), _ANY, _ANY] + [_ANY] * n_rout,
            scratch_shapes=[pltpu.VMEM((T, D), BF16), pltpu.VMEM((cw, D), BF16), pltpu.SemaphoreType.DMA((7,)),
                            pltpu.SemaphoreType.DMA((7,)), pltpu.SemaphoreType.DMA((2,))] + r_sems),
        out_shape=[SDS((T, N), BF16), SDS((T, D), BF16), SDS((N, D), BF16)] + r_out,
        input_output_aliases={4: 2, **_rider_aliases(riders, 5, 3)},
        compiler_params=_cparams(("arbitrary", "arbitrary")),
    )(chip_order, x, norm_g.reshape(1, D), w_loc, _own_block_placed(w_loc, "rows", dev), *r_in)
    return res[:3], _split_riders(riders, res[3:])


C_LNG, C_LNB, C_CW0, C_CW1, C_CW2, C_CB, C_PS = range(7)
C_ROWS = 8


def _pool_bands(R, anticausal):
    t = np.arange(R)[:, None]
    s = np.arange(R + CHUNK)[None, :]
    bands = [((s >= t) & (s < t + w)) if anticausal else ((s > t + CHUNK - w) & (s <= t + CHUNK)) for w in POOL_WINDOWS]
    return jnp.asarray(np.stack(bands), dtype=BF16)


def _mixers(p_ref, hxb_ref, hcg_ref, hxc_ref, cv, bsb_ref, wcat_ref, wpool_ref, band_ref, extb,
            first, blk, R, need_grad):
    def seg(lo):
        return p_ref[:, lo:lo + SEG].astype(F32)

    u, v, za = seg(O_U), seg(O_V), seg(O_ZA)
    xb, bg, cg, zb = seg(O_XB), seg(O_BG), seg(O_CG), seg(O_ZB)
    xc, zc = seg(O_XC), seg(O_ZC)
    out = {}

    ug, tu = _gelu(u)
    vg, tv = _gelu(v)
    mu = jnp.mean(vg, axis=-1, keepdims=True)
    vcen = vg - mu
    rs = lax.rsqrt(jnp.mean(vcen * vcen, axis=-1, keepdims=True) + LN_EPS)
    vhat = vcen * rs
    vn = (vhat * cv[C_LNG:C_LNG + 1, :] + cv[C_LNB:C_LNB + 1, :]).astype(BF16)
    lane_group = lax.broadcasted_iota(jnp.int32, (CHUNK, SEG), 1) // HEAD
    zero_b = jnp.zeros((CHUNK, SEG), BF16)
    sgs = []
    for ci in range(R // CHUNK):
        vc = vn[ci * CHUNK:(ci + 1) * CHUNK]
        vst = jnp.concatenate([jnp.where(lane_group == g, vc, zero_b) for g in range(GROUPS)], axis=0)
        sgs.append(_dot(wcat_ref[...], vst) + bsb_ref[...])
    sg = sgs[0] if len(sgs) == 1 else jnp.concatenate(sgs, axis=0)
    a_out = ug * sg
    sa = _sigmoid(za)
    out["a"] = a_out * (za * sa)

    cx = cg * xb
    halo_b = hcg_ref[...].astype(F32) * hxb_ref[...].astype(F32)
    extb[0:HALO, :] = jnp.where(first, 0.0, halo_b)
    extb[HALO:HALO + R, :] = cx
    cx1 = extb[pl.ds(HALO - 1, R), :]
    cx2 = extb[pl.ds(HALO - 2, R), :]
    yconv = (cv[C_CW0:C_CW0 + 1, :] * cx2 + cv[C_CW1:C_CW1 + 1, :] * cx1
             + cv[C_CW2:C_CW2 + 1, :] * cx + cv[C_CB:C_CB + 1, :])
    b_out = bg * yconv
    sb = _sigmoid(zb)
    out["b"] = b_out * (zb * sb)

    halo_c = hxc_ref[...]
    xc_ext = jnp.concatenate([jnp.zeros((CHUNK - HALO, SEG), BF16), jnp.where(first, jnp.zeros_like(halo_c), halo_c),
                              p_ref[:, O_XC:O_XC + SEG]], axis=0)
    tpos = blk * R + lax.broadcasted_iota(jnp.int32, (R, POOL_GROUP), 0) + 1
    pooled, invs, pws = [], [], []
    for gi, w in enumerate(POOL_WINDOWS):
        lo = gi * POOL_GROUP
        win = _dot(band_ref[gi], xc_ext[:, lo:lo + POOL_GROUP])
        inv = 1.0 / jnp.minimum(tpos, w).astype(F32)
        pg = (win * inv - xc[:, lo:lo + POOL_GROUP]).astype(BF16)
        pooled.append(pg)
        invs.append(inv)
        pws.append(_dot(pg, wpool_ref[gi]))
    pw = jnp.concatenate(pws, axis=1)
    c_out = pw * cv[C_PS:C_PS + 1, :]
    sc = _sigmoid(zc)
    out["c"] = c_out * (zc * sc)

    if need_grad:
        out.update(u=u, v=v, tu=tu, tv=tv, ug=ug, sg=sg, a_out=a_out, za=za, sa=sa,
                   rs=rs, vhat=vhat, vn=vn, lane_group=lane_group, zero_b=zero_b,
                   xb=xb, bg=bg, cg=cg, cx=cx, cx1=cx1, cx2=cx2, yconv=yconv, b_out=b_out, zb=zb, sb=sb,
                   pooled=pooled, invs=invs, pw=pw, c_out=c_out, zc=zc, sc=sc)
    return out


def _halo_specs(R, nb, rev):
    step = R // HALO

    def mk(col):
        def imap(i):
            b = (nb - 1 - i) if rev else i
            return (jnp.maximum(b * step - 1, 0), col)
        return pl.BlockSpec((HALO, SEG), imap)

    return [mk(O_XB // SEG), mk(O_CG // SEG), mk(O_XC // SEG)]


def _const_spec(shape):
    nd = len(shape)
    return pl.BlockSpec(shape, lambda i: (0,) * nd, pipeline_mode=pl.Buffered(1))


MIX_FWD_ROWS = 512
MIX_BWD_ROWS = 256


def _mix_block_rows(T, target):
    return _pick(T, target, CHUNK)


def _mix_fwd(p, x, lw, name, riders=()):
    T, D = x.shape
    N = p.shape[1]
    R = _mix_block_rows(T, MIX_FWD_ROWS)
    nb = T // R

    def body(p_ref, hxb, hcg, hxc, x_ref, cv_ref, bsb_ref, wcat_ref, wpool_ref, band_ref, wpa_ref, wpb_ref, wpc_ref,
             wo_ref, xo_ref, ya_ref, yb_ref, yc_ref, extb):
        i = pl.program_id(0)
        cv = cv_ref[...]
        r = _mixers(p_ref, hxb, hcg, hxc, cv, bsb_ref, wcat_ref, wpool_ref, band_ref, extb,
                    i == 0, i, R, False)
        merged = None
        for k, (act, w_ref, y_ref) in enumerate(((r["a"], wpa_ref, ya_ref), (r["b"], wpb_ref, yb_ref),
                                                 (r["c"], wpc_ref, yc_ref))):
            y = _dot(act.astype(BF16), w_ref[...]).astype(BF16)
            y_ref[...] = y
            term = _sigmoid(p_ref[:, O_G + k * D:O_G + (k + 1) * D]) * y
            merged = term if merged is None else merged + term
        xo_ref[...] = x_ref[...] + _dot(merged, wo_ref[...])

    row = lambda w: pl.BlockSpec((R, w), lambda i: (i, 0))
    consts = [lw["cvec"], lw["bsb"], lw["wcat"], lw["wpool"], _pool_bands(R, False), lw["wpa"], lw["wpb"], lw["wpc"],
              lw["wo"]]
    return _host_call(
        body, name, (nb,), riders,
        inputs=[p, p, p, p, x, *consts],
        in_specs=[row(N)] + _halo_specs(R, nb, False) + [row(D)] + [_const_spec(c.shape) for c in consts],
        out_specs=[row(D), row(D), row(D), row(D)],
        out_shape=[SDS((T, D), F32), SDS((T, D), BF16), SDS((T, D), BF16), SDS((T, D), BF16)],
        scratch_shapes=[pltpu.VMEM((HALO + R, SEG), F32)])


def _loss_head(x, final_g, target, name):
    T, D = x.shape
    bt = _pick(T, 512, 8)

    def body(x_ref, g_ref, t_ref, dx_ref, loss_ref, dg_ref):
        @pl.when(pl.program_id(0) == 0)
        def _():
            _zero(loss_ref)
            _zero(dg_ref)

        xv = x_ref[...]
        g = g_ref[...]
        rstd = lax.rsqrt(jnp.mean(xv * xv, axis=-1, keepdims=True) + RMS_EPS)
        xhat = xv * rstd
        err = xhat * g - t_ref[...]
        part = 0.5 * jnp.sum(jnp.sum(err * err, axis=-1, keepdims=True), axis=0, keepdims=True) / D
        loss_ref[...] += jnp.broadcast_to(part, loss_ref.shape)
        dy = err * (1.0 / D)
        dg_ref[0:1, :] += jnp.sum(dy * xhat, axis=0, keepdims=True)
        dxn = dy * g
        dx_ref[...] = rstd * (dxn - xhat * jnp.mean(dxn * xhat, axis=-1, keepdims=True))

    return pl.pallas_call(
        body, name=name, grid=(T // bt,),
        in_specs=[pl.BlockSpec((bt, D), lambda i: (i, 0)), _const_spec((1, D)), pl.BlockSpec((bt, D), lambda i: (i, 0))],
        out_specs=[pl.BlockSpec((bt, D), lambda i: (i, 0)), _const_spec((8, LANES)), _const_spec((8, D))],
        out_shape=[SDS((T, D), F32), SDS((8, LANES), F32), SDS((8, D), F32)],
        compiler_params=_cparams(("arbitrary",)),
    )(x, final_g.reshape(1, D), target)


V_LNG, V_LNB, V_CB, V_PS, V_CW0, V_CW1, V_CW2 = range(7)


def _mix_bwd(p, dxo, ya, yb, yc, lw, name):
    T, D = dxo.shape
    N = p.shape[1]
    R = _mix_block_rows(T, MIX_BWD_ROWS)
    nb = T // R

    def body(p_ref, hxb, hcg, hxc, dxo_ref, ya_ref, yb_ref, yc_ref, cv_ref, bsb_ref, wcat_ref, wcatt_ref,
             wpool_ref, band_ref, bandt_ref, wpa_ref, wpb_ref, wpc_ref, wo_ref,
             dp_ref, acts_ref, mrg_ref, dys_ref, gwc_ref, gbs_ref, gwpool_ref, gvec_ref,
             extb, extdy, cdy, cq, bsacc):
        i = pl.program_id(0)
        blk = nb - 1 - i

        @pl.when(i == 0)
        def _():
            for ref in (gwc_ref, gwpool_ref, gvec_ref, cdy, cq, bsacc):
                _zero(ref)

        cv = cv_ref[...]
        r = _mixers(p_ref, hxb, hcg, hxc, cv, bsb_ref, wcat_ref, wpool_ref, band_ref, extb,
                    blk == 0, blk, R, True)

        dxo_b = dxo_ref[...].astype(BF16)
        dm = _dot_nt(dxo_b, wo_ref[...]).astype(BF16)
        ys = [ya_ref[...], yb_ref[...], yc_ref[...]]
        sig = [_sigmoid(p_ref[:, O_G + k * D:O_G + (k + 1) * D]) for k in range(3)]
        mrg_ref[...] = sig[0] * ys[0] + sig[1] * ys[1] + sig[2] * ys[2]
        dacts = []
        for k, (act, w_ref) in enumerate(((r["a"], wpa_ref), (r["b"], wpb_ref), (r["c"], wpc_ref))):
            dyk = dm * sig[k]
            dp_ref[:, O_G + k * D:O_G + (k + 1) * D] = dyk * ys[k] * (1.0 - sig[k])
            acts_ref[:, k * SEG:(k + 1) * SEG] = act.astype(BF16)
            dys_ref[:, k * D:(k + 1) * D] = dyk
            dacts.append(_dot_nt(dyk, w_ref[...]))
        da, db, dc = dacts

        def silu_bwd(dact, pre, z, s):
            return dact * (z * s), dact * pre * (s * (1.0 + z * (1.0 - s)))

        d_aout, dza = silu_bwd(da, r["a_out"], r["za"], r["sa"])
        dp_ref[:, O_ZA:O_ZA + SEG] = dza.astype(BF16)
        dp_ref[:, O_U:O_U + SEG] = (d_aout * r["sg"] * _gelu_grad(r["u"], r["tu"])).astype(BF16)
        d_sg = d_aout * r["ug"]
        dvns = []
        for ci in range(R // CHUNK):
            dsc = d_sg[ci * CHUNK:(ci + 1) * CHUNK]
            bsacc[...] += dsc
            dsc_b = dsc.astype(BF16)
            dst = jnp.concatenate([jnp.where(r["lane_group"] == g, dsc_b, r["zero_b"]) for g in range(GROUPS)], axis=0)
            dvns.append(_dot(wcatt_ref[...], dst))
            gwc_ref[...] += _dot_nt(dst, r["vn"][ci * CHUNK:(ci + 1) * CHUNK])
        d_vn = dvns[0] if len(dvns) == 1 else jnp.concatenate(dvns, axis=0)
        vhat = r["vhat"]
        gvec_ref[V_LNG:V_LNG + 1, :] += jnp.sum(d_vn * vhat, axis=0, keepdims=True)
        gvec_ref[V_LNB:V_LNB + 1, :] += jnp.sum(d_vn, axis=0, keepdims=True)
        d_vhat = d_vn * cv[C_LNG:C_LNG + 1, :]
        d_vg = r["rs"] * (d_vhat - jnp.mean(d_vhat, axis=-1, keepdims=True)
                          - vhat * jnp.mean(d_vhat * vhat, axis=-1, keepdims=True))
        dp_ref[:, O_V:O_V + SEG] = (d_vg * _gelu_grad(r["v"], r["tv"])).astype(BF16)

        d_bout, dzb = silu_bwd(db, r["b_out"], r["zb"], r["sb"])
        dp_ref[:, O_ZB:O_ZB + SEG] = dzb.astype(BF16)
        dp_ref[:, O_BG:O_BG + SEG] = (d_bout * r["yconv"]).astype(BF16)
        d_y = d_bout * r["bg"]
        gvec_ref[V_CB:V_CB + 1, :] += jnp.sum(d_y, axis=0, keepdims=True)
        gvec_ref[V_CW0:V_CW0 + 1, :] += jnp.sum(d_y * r["cx2"], axis=0, keepdims=True)
        gvec_ref[V_CW1:V_CW1 + 1, :] += jnp.sum(d_y * r["cx1"], axis=0, keepdims=True)
        gvec_ref[V_CW2:V_CW2 + 1, :] += jnp.sum(d_y * r["cx"], axis=0, keepdims=True)
        extdy[0:R, :] = d_y
        extdy[R:R + HALO, :] = cdy[...]
        d_cx = (cv[C_CW2:C_CW2 + 1, :] * d_y + cv[C_CW1:C_CW1 + 1, :] * extdy[pl.ds(1, R), :]
                + cv[C_CW0:C_CW0 + 1, :] * extdy[pl.ds(2, R), :])
        cdy[...] = d_y[0:HALO]
        dp_ref[:, O_CG:O_CG + SEG] = (d_cx * r["xb"]).astype(BF16)
        dp_ref[:, O_XB:O_XB + SEG] = (d_cx * r["cg"]).astype(BF16)

        d_cout, dzc = silu_bwd(dc, r["c_out"], r["zc"], r["sc"])
        dp_ref[:, O_ZC:O_ZC + SEG] = dzc.astype(BF16)
        gvec_ref[V_PS:V_PS + 1, :] += jnp.sum(d_cout * r["pw"], axis=0, keepdims=True)
        d_pw = (d_cout * cv[C_PS:C_PS + 1, :]).astype(BF16)
        dpool, scaled = [], []
        for gi, w in enumerate(POOL_WINDOWS):
            lo = gi * POOL_GROUP
            dpw_g = d_pw[:, lo:lo + POOL_GROUP]
            gwpool_ref[lo:lo + POOL_GROUP, :] += _dot_tn(r["pooled"][gi], dpw_g)
            dpg = _dot_nt(dpw_g, wpool_ref[gi])
            dpool.append(dpg)
            scaled.append((dpg * r["invs"][gi]).astype(BF16))
        q = jnp.concatenate(scaled, axis=1)
        q_ext = jnp.concatenate([q, cq[...], jnp.zeros((CHUNK - HALO, SEG), BF16)], axis=0)
        for gi, w in enumerate(POOL_WINDOWS):
            lo = gi * POOL_GROUP
            acc = _dot(bandt_ref[gi], q_ext[:, lo:lo + POOL_GROUP])
            dp_ref[:, O_XC + lo:O_XC + lo + POOL_GROUP] = (acc - dpool[gi]).astype(BF16)
        cq[...] = q[0:HALO]

        @pl.when(i == nb - 1)
        def _():
            rr = lax.broadcasted_iota(jnp.int32, gwc_ref.shape, 0) % CHUNK
            cc = lax.broadcasted_iota(jnp.int32, gwc_ref.shape, 1)
            gwc_ref[...] = jnp.where(cc <= rr, gwc_ref[...], 0.0)
            acc = bsacc[...]
            hi = acc.astype(BF16)
            lo_ = (acc - hi.astype(F32)).astype(BF16)
            sel = (lax.broadcasted_iota(jnp.int32, (SEG, LANES), 0) // HEAD
                   == lax.broadcasted_iota(jnp.int32, (SEG, LANES), 1)).astype(BF16)
            gbs_ref[...] = _dot(hi, sel) + _dot(lo_, sel)

    row = lambda w: pl.BlockSpec((R, w), lambda i: (nb - 1 - i, 0))
    consts = [lw["cvec"], lw["bsb"], lw["wcat"], lw["wcatt"], lw["wpool"], _pool_bands(R, False), _pool_bands(R, True),
              lw["wpa"], lw["wpb"], lw["wpc"], lw["wo"]]
    acc_shapes = [(GROUPS * CHUNK, CHUNK), (CHUNK, LANES), (SEG, POOL_GROUP), (8, SEG)]
    row_widths = [N, 3 * SEG, D, 3 * D]
    return pl.pallas_call(
        body, name=name, grid=(nb,),
        in_specs=([row(N)] + _halo_specs(R, nb, True) + [row(D), row(D), row(D), row(D)]
                  + [_const_spec(c.shape) for c in consts]),
        out_specs=[row(w) for w in row_widths] + [_const_spec(s) for s in acc_shapes],
        out_shape=[SDS((T, w), BF16) for w in row_widths] + [SDS(s, F32) for s in acc_shapes],
        scratch_shapes=[pltpu.VMEM((HALO + R, SEG), F32)] * 2
        + [pltpu.VMEM((HALO, SEG), F32), pltpu.VMEM((HALO, SEG), BF16), pltpu.VMEM((CHUNK, SEG), F32)],
        compiler_params=_cparams(("arbitrary",)),
    )(p, p, p, p, dxo, ya, yb, yc, *consts)


def _proj_wgrad(acts, merged, dys, dxo, name, riders=()):
    T, D = dxo.shape
    bk = _pick(T, WGRAD_ROWS // 2, 16)

    def body(a_ref, m_ref, dy_ref, dxo_ref, gwpa_ref, gwpb_ref, gwpc_ref, gwo_ref):
        @pl.when(pl.program_id(0) == 0)
        def _():
            for ref in (gwpa_ref, gwpb_ref, gwpc_ref, gwo_ref):
                _zero(ref)

        gwo_ref[...] += _dot_tn(m_ref[...], dxo_ref[...].astype(BF16))
        for k, ref in enumerate((gwpa_ref, gwpb_ref, gwpc_ref)):
            ref[...] += _dot_tn(a_ref[:, k * SEG:(k + 1) * SEG], dy_ref[:, k * D:(k + 1) * D])

    row = lambda w: pl.BlockSpec((bk, w), lambda i: (i, 0))
    shapes = [(SEG, D), (SEG, D), (SEG, D), (D, D)]
    return _host_call(
        body, name, (T // bk,), riders,
        inputs=[acts, merged, dys, dxo],
        in_specs=[row(3 * SEG), row(D), row(3 * D), row(D)],
        out_specs=[_const_spec(s) for s in shapes], out_shape=[SDS(s, F32) for s in shapes],
        scratch_shapes=[])


def _inproj_token_blocks(T):
    return T // _pick(T, BWD_X_ROWS, 16)


def _inproj_bwd_x(dp, win_t, x, norm_g, dxo, name, riders=(), blocks=None, fill=None):
    T, D = x.shape
    N = dp.shape[1]
    bt = _pick(T, BWD_X_ROWS, 16)
    b0, nblk = blocks if blocks else (0, T // bt)

    def compute(dp_ref, w_ref, x_ref, g_ref, dxo_ref, *rest):
        dx_ref, dg_ref = rest[-2:]

        @pl.when(pl.program_id(0) == 0)
        def _():
            _zero(dg_ref)

        dh = _dot(dp_ref[...], w_ref[...])
        xv = x_ref[...]
        rstd = lax.rsqrt(jnp.mean(xv * xv, axis=-1, keepdims=True) + RMS_EPS)
        xhat = xv * rstd
        dg_ref[0:1, :] += jnp.sum(dh * xhat, axis=0, keepdims=True)
        dxn = dh * g_ref[...]
        dx_ref[...] = dxo_ref[...] + rstd * (dxn - xhat * jnp.mean(dxn * xhat, axis=-1, keepdims=True))

    rows = pl.BlockSpec((bt, D), lambda i: (i + b0, 0))
    return _host_call(
        compute, name, (nblk,), riders,
        inputs=[dp, win_t, x, norm_g.reshape(1, D), dxo] + ([] if fill is None else [fill]),
        in_specs=[pl.BlockSpec((bt, N), lambda i: (i + b0, 0)), _const_spec((N, D)), rows, _const_spec((1, D)), rows]
        + ([] if fill is None else [_ANY]),
        out_specs=[rows, _const_spec((8, D))],
        out_shape=[SDS((T, D), F32), SDS((8, D), F32)],
        scratch_shapes=[],
        aliases={} if fill is None else {5: 0})


def _inproj_bwd_w(dp, h, name, riders=()):
    T, N = dp.shape
    D = h.shape[1]
    bn = _pick(N, 1536, LANES)
    bk = _pick(T, WGRAD_ROWS, 16)
    nk = T // bk

    def compute(dp_ref, h_ref, o_ref):
        @pl.when(pl.program_id(1) == 0)
        def _():
            _zero(o_ref)

        o_ref[...] += _dot_tn(dp_ref[...], h_ref[...])

    return _host_call(
        compute, name, (N // bn, nk), riders,
        inputs=[dp, h],
        in_specs=[pl.BlockSpec((bk, bn), lambda j, k: (k, j)), pl.BlockSpec((bk, D), lambda j, k: (k, 0))],
        out_specs=[pl.BlockSpec((bn, D), lambda j, k: (j, 0))],
        out_shape=[SDS((N, D), F32)],
        scratch_shapes=[])


def _chip_peer(x, y, j):
    px = (1 - x) if (j >> 1) else x
    py = (1 - y) if (j & 1) else y
    return px, py


def _blk(ref, kind, k, n):
    if kind == "rows":
        return ref.at[pl.ds(pl.multiple_of(k * n, 8), n)]
    return ref.at[:, pl.ds(pl.multiple_of(k * n, LANES), n)]


class _Exchange:
    def __init__(self, srcs, out_shapes, n_sems, build, alias=None):
        self.srcs, self.out_shapes, self.n_sems, self.build = list(srcs), list(out_shapes), n_sems, build
        self.alias = dict(alias or {})


def _rider_aliases(riders, first_in, first_out):
    out, i, o = {}, first_in, first_out
    for e in riders:
        out.update({i + s: o + d for s, d in e.alias.items()})
        i, o = i + len(e.srcs), o + len(e.out_shapes)
    return out


def _rider_plan(riders):
    inputs = [s for e in riders for s in e.srcs]
    out_shapes = [o for e in riders for o in e.out_shapes]
    sems = [pltpu.SemaphoreType.DMA((e.n_sems,)) for e in riders for _ in range(2)]

    def copies(in_refs, out_refs, sem_refs):
        cps, i, o = [], 0, 0
        for k, e in enumerate(riders):
            ni, no = len(e.srcs), len(e.out_shapes)
            cps += e.build(in_refs[i:i + ni], out_refs[o:o + no], sem_refs[2 * k], sem_refs[2 * k + 1])
            i, o = i + ni, o + no
        return cps

    return inputs, out_shapes, sems, copies


_ANY = pl.BlockSpec(memory_space=pl.ANY)


def _host_call(compute, name, grid, riders, inputs, in_specs, out_specs, out_shape, scratch_shapes, aliases=None):
    r_in, r_out, r_sems, copies = _rider_plan(riders)
    ni, no, ns = len(inputs), len(out_shape), len(scratch_shapes)

    def body(*refs):
        ins, rins = refs[:ni], refs[ni:ni + len(r_in)]
        outs = refs[ni + len(r_in):ni + len(r_in) + no]
        routs = refs[ni + len(r_in) + no:ni + len(r_in) + no + len(r_out)]
        scr = refs[ni + len(r_in) + no + len(r_out):]
        first = functools.reduce(lambda a, b: a & b, [pl.program_id(d) == 0 for d in range(len(grid))])
        last = functools.reduce(lambda a, b: a & b, [pl.program_id(d) == grid[d] - 1 for d in range(len(grid))])
        if riders:
            @pl.when(first)
            def _():
                for cp in copies(rins, routs, scr[ns:]):
                    cp.start()

        compute(*ins, *outs, *scr[:ns])

        if riders:
            @pl.when(last)
            def _():
                for cp in copies(rins, routs, scr[ns:]):
                    cp.wait()

    res = pl.pallas_call(
        body, name=name, grid=grid,
        in_specs=list(in_specs) + [_ANY] * len(r_in),
        out_specs=list(out_specs) + [_ANY] * len(r_out),
        out_shape=list(out_shape) + r_out,
        scratch_shapes=list(scratch_shapes) + r_sems,
        input_output_aliases={**(aliases or {}), **_rider_aliases(riders, ni, no)},
        compiler_params=_cparams(("arbitrary",) * len(grid)),
    )(*inputs, *r_in)
    return res[:no], _split_riders(riders, res[no:])


def _split_riders(riders, flat):
    out, o = [], 0
    for e in riders:
        out.append(list(flat[o:o + len(e.out_shapes)]))
        o += len(e.out_shapes)
    return out


def _run_exchange(ex, name):
    n_in, n_out = len(ex.srcs), len(ex.out_shapes)

    def body(*refs):
        cps = ex.build(refs[:n_in], refs[n_in:n_in + n_out], refs[n_in + n_out], refs[n_in + n_out + 1])
        for cp in cps:
            cp.start()
        for cp in cps:
            cp.wait()

    return pl.pallas_call(
        body, name=name,
        in_specs=[_ANY] * n_in, out_specs=[_ANY] * n_out, out_shape=ex.out_shapes,
        input_output_aliases=ex.alias,
        scratch_shapes=[pltpu.SemaphoreType.DMA((ex.n_sems,)), pltpu.SemaphoreType.DMA((ex.n_sems,))],
        compiler_params=pltpu.CompilerParams(has_side_effects=True),
    )(*ex.srcs)


def _gather_sizes(shards, kinds):
    sizes = [s.shape[0] if k == "rows" else s.shape[1] for s, k in zip(shards, kinds)]
    fulls = [SDS((s.shape[0] * N_DEV,) + s.shape[1:], s.dtype) if k == "rows"
             else SDS((s.shape[0], s.shape[1] * N_DEV), s.dtype) for s, k in zip(shards, kinds)]
    return sizes, fulls


def _own_block_placed(shard, kind, dev):
    n = shard.shape[0] if kind == "rows" else shard.shape[1]
    shape = (n * N_DEV,) + shard.shape[1:] if kind == "rows" else (shard.shape[0], n * N_DEV)
    start = (dev * n, 0) if kind == "rows" else (0, dev * n)
    return lax.dynamic_update_slice(lax.empty(shape, shard.dtype), shard, start)


def _gather_direct(shards, kinds, dev):
    n = len(shards)
    sizes, fulls = _gather_sizes(shards, kinds)
    placed = [_own_block_placed(s, k, dev) for s, k in zip(shards, kinds)]

    def build(ins, outs, send_sems, recv_sems):
        x, y, c = _me()
        cps = []
        for a in range(n):
            mine = _blk(outs[a], kinds[a], 4 * x + 2 * y + c, sizes[a])
            for j in range(N_CHIP):
                to = (x, y, 1 - c) if j == 0 else (*_chip_peer(x, y, j), c)
                cps.append(pltpu.make_async_remote_copy(
                    src_ref=ins[a], dst_ref=mine, send_sem=send_sems.at[N_CHIP * a + j],
                    recv_sem=recv_sems.at[N_CHIP * a + j], device_id=to, device_id_type=MESH))
        return cps

    return _Exchange(list(shards) + placed, fulls, N_CHIP * n, build, alias={n + a: a for a in range(n)})


def _gather_everywhere(shards, kinds, dev):
    n = len(shards)
    sizes, fulls = _gather_sizes(shards, kinds)
    placed = [_own_block_placed(s, k, dev) for s, k in zip(shards, kinds)]

    def build(ins, outs, send_sems, recv_sems):
        x, y, c = _me()
        cps = []
        for a in range(n):
            mine = _blk(outs[a], kinds[a], 4 * x + 2 * y + c, sizes[a])
            for d in range(1, N_DEV):
                to = ((1 - x) if d & 4 else x, (1 - y) if d & 2 else y, (1 - c) if d & 1 else c)
                cps.append(pltpu.make_async_remote_copy(
                    src_ref=ins[a], dst_ref=mine, send_sem=send_sems.at[N_DEV * a + d],
                    recv_sem=recv_sems.at[N_DEV * a + d], device_id=to, device_id_type=MESH))
        return cps

    return _Exchange(list(shards) + placed, fulls, N_DEV * n, build, alias={n + a: a for a in range(n)})


def _gather_forward(fulls, kinds, sizes):
    n = len(fulls)

    def build(ins, outs, send_sems, recv_sems):
        x, y, c = _me()
        cps = []
        for a in range(n):
            for j in (1, 2, 3):
                px, py = _chip_peer(x, y, j)
                k = 4 * px + 2 * py + c
                cps.append(pltpu.make_async_remote_copy(
                    src_ref=_blk(ins[a], kinds[a], k, sizes[a]), dst_ref=_blk(outs[a], kinds[a], k, sizes[a]),
                    send_sem=send_sems.at[3 * a + j - 1], recv_sem=recv_sems.at[3 * a + j - 1],
                    device_id=(x, y, 1 - c), device_id_type=MESH))
        return cps

    return _Exchange(fulls, [SDS(f.shape, f.dtype) for f in fulls], 3 * n, build, alias={a: a for a in range(n)})


def _sibling_exchange(grads, kinds, sizes):
    n = len(grads)

    def blk_shape(a):
        g = grads[a]
        return (sizes[a],) + g.shape[1:] if kinds[a] == "rows" else (g.shape[0], sizes[a])

    def build(ins, outs, send_sems, recv_sems):
        x, y, c = _me()
        cps = []
        for a in range(n):
            for q in range(N_CHIP):
                cps.append(pltpu.make_async_remote_copy(
                    src_ref=_blk(ins[a], kinds[a], 2 * q + (1 - c), sizes[a]), dst_ref=outs[a].at[q],
                    send_sem=send_sems.at[N_CHIP * a + q], recv_sem=recv_sems.at[N_CHIP * a + q],
                    device_id=(x, y, 1 - c), device_id_type=MESH))
        return cps

    return _Exchange(grads, [SDS((N_CHIP,) + blk_shape(a), F32) for a in range(n)], N_CHIP * n, build)


def _chip_partial(g, r1, kind, size, cidx, name):
    if kind == "rows":
        rows, cols = size, g.shape[1]
        g3 = g.reshape(N_DEV, rows, cols)
        rb = _pick(rows, 512, 16)
        g_spec = pl.BlockSpec((1, rb, cols), lambda q, j, c: (2 * q + c[0], j, 0))
        grid = (N_CHIP, rows // rb)
        blk = (1, rb, cols)
        imap = lambda q, j, c: (q, j, 0)
    else:
        rows, cols = g.shape[0], size
        g3 = g
        g_spec = pl.BlockSpec((rows, cols), lambda q, j, c: (0, 2 * q + c[0]))
        grid = (N_CHIP, 1)
        blk = (1, rows, cols)
        imap = lambda q, j, c: (q, 0, 0)

    def body(c_ref, g_ref, r_ref, p_ref, pb_ref):
        s = g_ref[...].reshape(blk) + r_ref[...]
        p_ref[...] = s
        pb_ref[...] = s.astype(BF16)

    return pl.pallas_call(
        body, name=name,
        grid_spec=pltpu.PrefetchScalarGridSpec(
            num_scalar_prefetch=1, grid=grid,
            in_specs=[g_spec, pl.BlockSpec(blk, imap)],
            out_specs=[pl.BlockSpec(blk, imap), pl.BlockSpec(blk, imap)]),
        out_shape=[SDS((N_CHIP, rows, cols), F32), SDS((N_CHIP, rows, cols), BF16)],
        compiler_params=_cparams(("arbitrary", "arbitrary")),
    )(cidx, g3, r1)


def _chip_exchange(parts):
    n = len(parts)
    m = N_CHIP - 1

    def build(ins, outs, send_sems, recv_sems):
        x, y, c = _me()
        cps = []
        for a in range(n):
            for j in (1, 2, 3):
                px, py = _chip_peer(x, y, j)
                cps.append(pltpu.make_async_remote_copy(
                    src_ref=ins[a].at[2 * px + py], dst_ref=outs[a].at[j - 1], send_sem=send_sems.at[m * a + j - 1],
                    recv_sem=recv_sems.at[m * a + j - 1], device_id=(px, py, c), device_id_type=MESH))
        return cps

    return _Exchange(parts, [SDS((m,) + p.shape[1:], BF16) for p in parts], m * n, build)


def _grad_total(part, r2, qidx, name):
    _, rows, cols = part.shape
    rb = _pick(rows, 512, 16)

    def body(q_ref, p_ref, r_ref, o_ref):
        s = p_ref[0]
        for j in range(N_CHIP - 1):
            s = s + r_ref[j].astype(F32)
        o_ref[...] = s

    return pl.pallas_call(
        body, name=name,
        grid_spec=pltpu.PrefetchScalarGridSpec(
            num_scalar_prefetch=1, grid=(rows // rb,),
            in_specs=[pl.BlockSpec((1, rb, cols), lambda i, q: (q[0], i, 0)),
                      pl.BlockSpec((N_CHIP - 1, rb, cols), lambda i, q: (0, i, 0))],
            out_specs=pl.BlockSpec((rb, cols), lambda i, q: (i, 0))),
        out_shape=SDS((rows, cols), F32),
        compiler_params=_cparams(("arbitrary",)),
    )(qidx, part, r2)


def _all_reduce_small(pack, name):
    rows = pack.shape[0]
    rs = rows // N_DEV
    assert rs * N_DEV == rows and rs % 8 == 0

    def body(x_ref, o_ref, rbuf, red, send1, recv1, send2, recv2):
        x, y, c = _me()
        me = 4 * x + 2 * y + c

        def peer(d):
            px = (1 - x) if (d >> 2) & 1 else x
            py = (1 - y) if (d >> 1) & 1 else y
            pc = (1 - c) if d & 1 else c
            return px, py, pc

        def sl(ref, k):
            return ref.at[pl.ds(pl.multiple_of(k * rs, 8), rs)]

        phase1 = []
        for d in range(1, N_DEV):
            px, py, pc = peer(d)
            phase1.append(pltpu.make_async_remote_copy(
                src_ref=sl(x_ref, 4 * px + 2 * py + pc), dst_ref=rbuf.at[d], send_sem=send1.at[d], recv_sem=recv1.at[d],
                device_id=(px, py, pc), device_id_type=MESH))
        for cp in phase1:
            cp.start()
        acc = sl(x_ref, me)[...]
        for cp in phase1:
            cp.wait()
        for d in range(1, N_DEV):
            acc = acc + rbuf[d]
        red[...] = acc
        sl(o_ref, me)[...] = acc
        phase2 = []
        for d in range(1, N_DEV):
            px, py, pc = peer(d)
            phase2.append(pltpu.make_async_remote_copy(
                src_ref=red, dst_ref=sl(o_ref, me), send_sem=send2.at[d], recv_sem=recv2.at[d],
                device_id=(px, py, pc), device_id_type=MESH))
        for cp in phase2:
            cp.start()
        for cp in phase2:
            cp.wait()

    vm = pl.BlockSpec(memory_space=pltpu.VMEM)
    return pl.pallas_call(
        body, name=name, in_specs=[vm], out_specs=vm, out_shape=SDS(pack.shape, F32),
        scratch_shapes=[pltpu.VMEM((N_DEV, rs, LANES), F32), pltpu.VMEM((rs, LANES), F32),
                        pltpu.SemaphoreType.DMA((N_DEV,)), pltpu.SemaphoreType.DMA((N_DEV,)),
                        pltpu.SemaphoreType.DMA((N_DEV,)), pltpu.SemaphoreType.DMA((N_DEV,))],
        compiler_params=_cparams(None, has_side_effects=True),
    )(pack)


def _adamw(w, g, m, v, name):
    rows, cols = w.shape
    rb = rows if rows * cols * 4 <= ADAMW_WHOLE_BYTES else _pick(rows, 256, 8)
    c1 = 1.0 / (1.0 - ADAM_B1 ** ADAM_STEP)
    c2 = 1.0 / (1.0 - ADAM_B2 ** ADAM_STEP)

    def body(w_ref, g_ref, m_ref, v_ref, d_ref, mo_ref, vo_ref):
        gv = g_ref[...]
        mn = ADAM_B1 * m_ref[...] + (1.0 - ADAM_B1) * gv
        vn = ADAM_B2 * v_ref[...] + (1.0 - ADAM_B2) * (gv * gv)
        mo_ref[...] = mn
        vo_ref[...] = vn
        d_ref[...] = -ADAM_LR * ((mn * c1) / (jnp.sqrt(vn * c2) + ADAM_EPS) + ADAM_WD * w_ref[...])

    spec = pl.BlockSpec((rb, cols), lambda i: (i, 0))
    return pl.pallas_call(
        body, name=name, grid=(rows // rb,),
        in_specs=[spec] * 4, out_specs=[spec] * 3, out_shape=[SDS((rows, cols), F32)] * 3,
        compiler_params=_cparams(("arbitrary",)),
    )(w, g, m, v)


def _pad_rows(a, mult=8):
    r = (-a.shape[0]) % mult
    return a if r == 0 else jnp.pad(a, ((0, r), (0, 0)))


def _as_lanes(a):
    flat = a.reshape(-1)
    pad = (-flat.shape[0]) % (8 * LANES)
    if pad:
        flat = jnp.pad(flat, (0, pad))
    return flat.reshape(-1, LANES)


def kernel(x, norm_g, w_in, ln_g, ln_b, w_s, b_s, conv_w, conv_b, w_pool, pool_scale, w_pa, w_pb, w_pc, w_o, final_g, loss_target, m_norm_g, m_w_in, m_ln_g, m_ln_b, m_w_s, m_b_s, m_conv_w, m_conv_b, m_w_pool, m_pool_scale, m_w_pa, m_w_pb, m_w_pc, m_w_o, m_final_g, v_norm_g, v_w_in, v_ln_g, v_ln_b, v_w_s, v_b_s, v_conv_w, v_conv_b, v_w_pool, v_pool_scale, v_w_pa, v_w_pb, v_w_pc, v_w_o, v_final_g):
    L = w_in.shape[0]
    D = x.shape[-1]
    n_loc = w_in.shape[2]
    pc_loc = w_pa.shape[2]
    x0 = x[0]
    target = loss_target[0]
    xi, yi, ci = _me()
    cidx = jnp.reshape(ci, (1,)).astype(jnp.int32)
    qidx = jnp.reshape(2 * xi + yi, (1,)).astype(jnp.int32)

    kinds5 = ["rows", "cols", "cols", "cols", "rows"]

    def layer_shards(l):
        return [w_in[l].T.astype(BF16), w_pa[l].astype(BF16), w_pb[l].astype(BF16), w_pc[l].astype(BF16),
                w_o[l].astype(BF16)]

    def gathered(direct, shards, kinds, l):
        sizes, _ = _gather_sizes(shards, kinds)
        return _run_exchange(_gather_forward(direct, kinds, sizes), f"weights_forward_{l}")

    cw_loc = _pad_rows(conv_w.reshape(L * CONV_TAPS, -1))
    cw_loc = jnp.pad(cw_loc, ((0, 0), (0, LANES - cw_loc.shape[1])))
    causal = jnp.tril(jnp.ones((CHUNK, CHUNK), dtype=bool))

    chip_order = jnp.stack([2 * xi + yi] + [2 * px + py for px, py in (_chip_peer(xi, yi, j) for j in (1, 2, 3))])
    sh0 = layer_shards(0)
    rest0, krest0 = sh0[1:] + [cw_loc], kinds5[1:] + ["rows"]
    dev = 4 * xi + 2 * yi + ci
    (p0, h0, win_t0), delivered = _inproj_gathering(x0, norm_g[0], sh0[0], chip_order.astype(jnp.int32), dev,
                                                    "inproj_fwd_0", [_gather_everywhere(rest0, krest0, dev)])
    rest0_full = delivered[0]
    cw_all = rest0_full[-1].reshape(N_DEV, -1, LANES)[:, :L * CONV_TAPS, :conv_w.shape[2]]
    conv_w_full = jnp.transpose(cw_all, (1, 0, 2)).reshape(L, CONV_TAPS, -1)

    def make_layer(l, full5):
        win_t, wpa, wpb, wpc, wo = full5
        wm = jnp.where(causal, w_s[l], 0.0)
        cvec = jnp.concatenate([ln_g[l][None], ln_b[l][None], conv_w_full[l], conv_b[l][None], pool_scale[l][None],
                                jnp.zeros((C_ROWS - 7, SEG), F32)], axis=0)
        return dict(
            win_t=win_t, wpa=wpa, wpb=wpb, wpc=wpc, wo=wo, cvec=cvec,
            bsb=jnp.repeat(b_s[l].T, HEAD, axis=1),
            wcat=jnp.transpose(wm, (1, 0, 2)).reshape(CHUNK, GROUPS * CHUNK).astype(BF16),
            wcatt=jnp.transpose(wm, (2, 0, 1)).reshape(CHUNK, GROUPS * CHUNK).astype(BF16),
            wpool=w_pool[l].astype(BF16))

    layers = [make_layer(0, [win_t0] + list(rest0_full[:4]))]
    xs, saved = [x0], []
    for l in range(L):
        lw = layers[l]
        if l == 0:
            p, h = p0, h0
        else:
            (p, h), _ = _inproj(xs[-1], norm_g[l], lw["win_t"], f"inproj_fwd_{l}")
        nxt = layer_shards(l + 1) if l + 1 < L else None
        (xn, ya, yb, yc), delivered = _mix_fwd(p, xs[-1], lw, f"mix_fwd_{l}",
                                               [_gather_direct(nxt, kinds5, dev)] if nxt else [])
        if nxt:
            layers.append(make_layer(l + 1, gathered(delivered[0], nxt, kinds5, l + 1)))
        saved.append((p, h, ya, yb, yc))
        xs.append(xn)
    dx, loss_acc, dfg_acc = _loss_head(xs[-1], final_g, target, "loss_head")

    rs_sizes = [n_loc, pc_loc, pc_loc, pc_loc, w_o.shape[1]]
    await_sibling, await_chips = [], []
    partial_of, from_chips = {}, {}
    serial = [0]

    def riders_now():
        riders, plan = [], []
        for grp in await_chips:
            riders.append(_chip_exchange([partial_of[t][1] for t, _, _, _ in grp]))
            plan.append(("chips", grp))
        for grp in await_sibling:
            riders.append(_sibling_exchange([g for _, g, _, _ in grp], [k for _, _, k, _ in grp], [s for _, _, _, s in grp]))
            plan.append(("sibling", grp))
        del await_chips[:], await_sibling[:]
        return riders, plan

    def absorb(plan, delivered):
        for (what, grp), res in zip(plan, delivered):
            for (t, g, k, s), r in zip(grp, res):
                if what == "chips":
                    from_chips[t] = r
                else:
                    partial_of[t] = _chip_partial(g, r, k, s, cidx, f"grad_chip_partial_{t[0]}_{t[1]}")
            if what == "sibling":
                await_chips.append(grp)

    small = [None] * L
    for l in reversed(range(L)):
        lw = layers[l]
        p, h, ya, yb, yc = saved[l]
        dp, acts, merged, dys, gwc, gbs, gwpool, gvec = _mix_bwd(p, dx, ya, yb, yc, lw, f"mix_bwd_{l}")
        riders, plan = riders_now()
        (gwpa, gwpb, gwpc, gwo), delivered = _proj_wgrad(acts, merged, dys, dx, f"proj_wgrad_{l}", riders)
        absorb(plan, delivered)
        await_sibling.append([((l, a), g, kinds5[a], rs_sizes[a]) for a, g in ((1, gwpa), (2, gwpb), (3, gwpc), (4, gwo))])

        def bwd_x(dxo, pieces):
            nt = _inproj_token_blocks(dxo.shape[0])
            pieces = min(pieces, nt)
            done, dng, b0 = None, None, 0
            for k in range(pieces):
                cnt = (nt - b0) // (pieces - k)
                riders, plan = riders_now()
                (done, dng_k), delivered = _inproj_bwd_x(dp, lw["win_t"], xs[l], norm_g[l], dxo, f"inproj_bwd_x_{l}_{k}",
                                                         riders, blocks=(b0, cnt), fill=done)
                absorb(plan, delivered)
                dng = dng_k if dng is None else dng + dng_k
                b0 += cnt
            return done, dng

        def bwd_w():
            riders, plan = riders_now()
            (gwin_t,), delivered = _inproj_bwd_w(dp, h, f"inproj_bwd_w_{l}", riders)
            absorb(plan, delivered)
            await_sibling.append([((l, 0), gwin_t, kinds5[0], rs_sizes[0])])

        if l == L - 1:
            dx, dng = bwd_x(dx, 1)
            bwd_w()
        else:
            bwd_w()
            dx, dng = bwd_x(dx, 2 if l == 0 else 1)
        small[l] = dict(norm_g=dng[0], ln_g=gvec[V_LNG], ln_b=gvec[V_LNB], w_s=gwc, b_s=gbs, conv_w=gvec[V_CW0:V_CW0 + 3],
                        conv_b=gvec[V_CB], w_pool=gwpool, pool_scale=gvec[V_PS])
    while await_sibling or await_chips:
        riders, plan = riders_now()
        delivered = []
        for ex in riders:
            delivered.append(_run_exchange(ex, f"grad_exchange_tail_{serial[0]}"))
            serial[0] += 1
        absorb(plan, delivered)
    grad_x = dx[None]
    big_grads = []
    for l in range(L):
        tot = [_grad_total(partial_of[(l, a)][0], from_chips[(l, a)], qidx, f"grad_total_{l}_{a}") for a in range(5)]
        big_grads.append([tot[0].T,
                          tot[1].reshape(SEG, pc_loc), tot[2].reshape(SEG, pc_loc), tot[3].reshape(SEG, pc_loc),
                          tot[4]])

    names = ["norm_g", "ln_g", "ln_b", "w_s", "b_s", "conv_w", "conv_b", "w_pool", "pool_scale"]
    pieces = [_as_lanes(jnp.stack([small[l][nm] for l in range(L)])) for nm in names]
    pieces += [_as_lanes(dfg_acc[0]), loss_acc]
    sizes = [pc.shape[0] for pc in pieces]
    pack = jnp.concatenate(pieces, axis=0)
    pack = _pad_rows(pack, 8 * N_DEV)
    red = _all_reduce_small(pack, "small_grads_all_reduce")
    offs = [0]
    for s in sizes:
        offs.append(offs[-1] + s)

    def unpack(i, shape):
        n = math.prod(shape)
        return red[offs[i]:offs[i + 1]].reshape(-1)[:n].reshape(shape)

    g_norm_g = unpack(0, (L, D))
    g_ln_g = unpack(1, (L, SEG))
    g_ln_b = unpack(2, (L, SEG))
    g_w_s = unpack(3, (L, GROUPS, CHUNK, CHUNK))
    g_b_s = jnp.transpose(unpack(4, (L, CHUNK, LANES))[:, :, :GROUPS], (0, 2, 1))
    g_conv_w_full = unpack(5, (L, CONV_TAPS, SEG))
    g_conv_b = unpack(6, (L, SEG))
    g_w_pool = unpack(7, (L, len(POOL_WINDOWS), POOL_GROUP, POOL_GROUP))
    g_pool_scale = unpack(8, (L, SEG))
    g_final_g = unpack(9, (D,))
    loss = red[offs[10], 0]
    g_conv_w = lax.dynamic_slice_in_dim(g_conv_w_full, dev * conv_w.shape[2], conv_w.shape[2], axis=2)

    g_w_in = jnp.stack([big_grads[l][0] for l in range(L)])
    g_w_pa = jnp.stack([big_grads[l][1] for l in range(L)])
    g_w_pb = jnp.stack([big_grads[l][2] for l in range(L)])
    g_w_pc = jnp.stack([big_grads[l][3] for l in range(L)])
    g_w_o = jnp.stack([big_grads[l][4] for l in range(L)])

    grads = dict(norm_g=g_norm_g, w_in=g_w_in, ln_g=g_ln_g, ln_b=g_ln_b, w_s=g_w_s, b_s=g_b_s, conv_w=g_conv_w,
                 conv_b=g_conv_b, w_pool=g_w_pool, pool_scale=g_pool_scale, w_pa=g_w_pa, w_pb=g_w_pb, w_pc=g_w_pc,
                 w_o=g_w_o, final_g=g_final_g)
    weights = dict(norm_g=norm_g, w_in=w_in, ln_g=ln_g, ln_b=ln_b, w_s=w_s, b_s=b_s, conv_w=conv_w, conv_b=conv_b,
                   w_pool=w_pool, pool_scale=pool_scale, w_pa=w_pa, w_pb=w_pb, w_pc=w_pc, w_o=w_o, final_g=final_g)
    ms = dict(norm_g=m_norm_g, w_in=m_w_in, ln_g=m_ln_g, ln_b=m_ln_b, w_s=m_w_s, b_s=m_b_s, conv_w=m_conv_w,
              conv_b=m_conv_b, w_pool=m_w_pool, pool_scale=m_pool_scale, w_pa=m_w_pa, w_pb=m_w_pb, w_pc=m_w_pc,
              w_o=m_w_o, final_g=m_final_g)
    vs = dict(norm_g=v_norm_g, w_in=v_w_in, ln_g=v_ln_g, ln_b=v_ln_b, w_s=v_w_s, b_s=v_b_s, conv_w=v_conv_w,
              conv_b=v_conv_b, w_pool=v_w_pool, pool_scale=v_pool_scale, w_pa=v_w_pa, w_pb=v_w_pb, w_pc=v_w_pc,
              w_o=v_w_o, final_g=v_final_g)
    order = ["norm_g", "w_in", "ln_g", "ln_b", "w_s", "b_s", "conv_w", "conv_b", "w_pool", "pool_scale", "w_pa", "w_pb",
             "w_pc", "w_o", "final_g"]

    delta, new_m, new_v = {}, {}, {}
    big = ["w_in", "w_pa", "w_pb", "w_pc", "w_o"]
    for nm in big:
        shp = weights[nm].shape
        two = lambda a: a.reshape(-1, shp[-1])
        d, mn, vn = _adamw(two(weights[nm]), two(grads[nm]), two(ms[nm]), two(vs[nm]), f"adamw_{nm}")
        delta[nm], new_m[nm], new_v[nm] = d.reshape(shp), mn.reshape(shp), vn.reshape(shp)
    rest = [nm for nm in order if nm not in big]
    cat = lambda src: jnp.concatenate([_as_lanes(src[nm]) for nm in rest], axis=0)
    d, mn, vn = _adamw(cat(weights), cat(grads), cat(ms), cat(vs), "adamw_small")
    off = 0
    for nm in rest:
        shp = weights[nm].shape
        n = math.prod(shp)
        rows = _as_lanes(weights[nm]).shape[0]
        cut = lambda a: a[off:off + rows].reshape(-1)[:n].reshape(shp)
        delta[nm], new_m[nm], new_v[nm] = cut(d), cut(mn), cut(vn)
        off += rows

    return (loss, grad_x, *[grads[nm] for nm in order], *[delta[nm] for nm in order],
            *[new_m[nm] for nm in order], *[new_v[nm] for nm in order])
```

```python
import functools
import math

import numpy as np
import jax
import jax.numpy as jnp
from jax import lax
from jax.experimental import pallas as pl
from jax.experimental.pallas import tpu as pltpu

F32 = jnp.float32
BF16 = jnp.bfloat16
SDS = jax.ShapeDtypeStruct
MESH = pl.DeviceIdType.MESH

SEG = 512
CHUNK = 128
GROUPS = 8
HEAD = SEG // GROUPS
POOL_WINDOWS = (2, 4, 8, 16)
POOL_GROUP = SEG // len(POOL_WINDOWS)
CONV_TAPS = 3
HALO = 16
RMS_EPS = 1e-6
LN_EPS = 1e-5
ADAM_LR, ADAM_B1, ADAM_B2, ADAM_EPS, ADAM_WD, ADAM_STEP = 0.001, 0.9, 0.999, 1e-08, 0.01, 10

O_U, O_V, O_ZA, O_XB, O_BG, O_CG, O_ZB, O_XC, O_ZC, O_G = (SEG * i for i in range(10))

N_DEV = 8
N_CHIP = 4
LANES = 128
VMEM_LIMIT = 48 * 1024 * 1024
ADAMW_WHOLE_BYTES = 2 * 1024 * 1024
INPROJ_ROWS = 1024
BWD_X_ROWS = 512
WGRAD_ROWS = 2048


def _cparams(sem=None, **kw):
    return pltpu.CompilerParams(dimension_semantics=sem, vmem_limit_bytes=VMEM_LIMIT, **kw)


def _pick(total, target, mult):
    best = None
    for d in range(mult, min(total, target) + 1, mult):
        if total % d == 0:
            best = d
    assert best is not None, (total, target, mult)
    return best


def _dot(a, b):
    return jnp.dot(a, b, preferred_element_type=F32)


def _dot_nt(a, b):
    return lax.dot_general(a, b, (((1,), (1,)), ((), ())), preferred_element_type=F32)


def _dot_tn(a, b):
    return lax.dot_general(a, b, (((0,), (0,)), ((), ())), preferred_element_type=F32)


def _zero(ref):
    ref[...] = jnp.zeros(ref.shape, ref.dtype)


def _sigmoid(x):
    return 1.0 / (1.0 + jnp.exp(-x))


_GELU_C = math.sqrt(2.0 / math.pi)


def _gelu(x):
    t = jnp.tanh(_GELU_C * (x + 0.044715 * x * x * x))
    return 0.5 * x * (1.0 + t), t


def _gelu_grad(x, t):
    return 0.5 * (1.0 + t) + 0.5 * x * (1.0 - t * t) * _GELU_C * (1.0 + 3.0 * 0.044715 * x * x)


def _me():
    return lax.axis_index("x"), lax.axis_index("y"), lax.axis_index("c")


def _inproj(x, norm_g, win_t, name, riders=()):
    T, D = x.shape
    N = win_t.shape[0]
    bt = _pick(T, INPROJ_ROWS, 16)
    bn = _pick(N, 1536, LANES)
    grid = (T // bt, N // bn)

    def compute(x_ref, g_ref, w_ref, p_ref, h_ref, hs_ref):
        @pl.when(pl.program_id(1) == 0)
        def _():
            xv = x_ref[...]
            rstd = lax.rsqrt(jnp.mean(xv * xv, axis=-1, keepdims=True) + RMS_EPS)
            hb = (xv * rstd * g_ref[...]).astype(BF16)
            hs_ref[...] = hb
            h_ref[...] = hb

        p_ref[...] = _dot_nt(hs_ref[...], w_ref[...]).astype(BF16)

    return _host_call(
        compute, name, grid, riders,
        inputs=[x, norm_g.reshape(1, D), win_t],
        in_specs=[pl.BlockSpec((bt, D), lambda i, j: (i, 0)),
                  pl.BlockSpec((1, D), lambda i, j: (0, 0)),
                  pl.BlockSpec((bn, D), lambda i, j: (j, 0))],
        out_specs=[pl.BlockSpec((bt, bn), lambda i, j: (i, j)),
                   pl.BlockSpec((bt, D), lambda i, j: (i, 0))],
        out_shape=[SDS((T, N), BF16), SDS((T, D), BF16)],
        scratch_shapes=[pltpu.VMEM((bt, D), BF16)])


def _inproj_gathering(x, norm_g, w_loc, chip_order, name, riders=()):
    T, D = x.shape
    n = w_loc.shape[0]
    N = n * N_DEV
    cw = 2 * n
    bt = _pick(T, INPROJ_ROWS, 16)
    nt = T // bt
    r_in, r_out, r_sems, copies = _rider_plan(riders)
    n_rin, n_rout = len(r_in), len(r_out)

    def body(q_ref, x_ref, g_ref, wloc_ref, *rest):
        rins = rest[:n_rin]
        p_ref, h_ref, wfull_ref = rest[n_rin:n_rin + 3]
        routs = rest[n_rin + 3:n_rin + 3 + n_rout]
        hs_ref, wbuf, send_sems, recv_sems, loc_sems = rest[n_rin + 3 + n_rout:n_rin + 8 + n_rout]
        rsems = rest[n_rin + 8 + n_rout:]
        j, i = pl.program_id(0), pl.program_id(1)
        cx, cy, cc = _me()
        sibling = (cx, cy, 1 - cc)

        def rows(k):
            return wfull_ref.at[pl.ds(pl.multiple_of(k * n, 8), n)]

        def shard_copy(slot, src, k, to):
            return pltpu.make_async_remote_copy(src_ref=src, dst_ref=rows(k), send_sem=send_sems.at[slot],
                                                recv_sem=recv_sems.at[slot], device_id=to, device_id_type=MESH)

        me = 4 * cx + 2 * cy + cc
        place_mine = pltpu.make_async_copy(wloc_ref, rows(me), loc_sems.at[0])
        sends = [shard_copy(0, wloc_ref, me, sibling)]
        for jj in (1, 2, 3):
            sends.append(shard_copy(jj, wloc_ref, me, (*_chip_peer(cx, cy, jj), cc)))

        def forward(jj):
            px, py = _chip_peer(cx, cy, jj)
            k = 4 * px + 2 * py + cc
            return shard_copy(3 + jj, rows(k), k, sibling)

        def load_chunk(q):
            cp = pltpu.make_async_copy(wfull_ref.at[pl.ds(pl.multiple_of(q * cw, 8), cw)], wbuf, loc_sems.at[1])
            cp.start()
            cp.wait()

        keep_h = pltpu.make_async_copy(hs_ref, h_ref, loc_sems.at[2])

        @pl.when((j == 0) & (i == 0))
        def _():
            place_mine.start()
            for cp in sends:
                cp.start()
            for cp in copies(rins, routs, rsems):
                cp.start()
            place_mine.wait()
            sends[0].wait_recv()
            load_chunk(q_ref[0])

        for jj in (1, 2, 3):
            @pl.when((j == jj) & (i == 0))
            def _(jj=jj):
                sends[jj].wait_recv()
                fwd = forward(jj)
                fwd.start()
                fwd.wait_recv()
                load_chunk(q_ref[jj])

        tok = pl.ds(pl.multiple_of(i * bt, bt), bt)

        @pl.when(j == 0)
        def _():
            xv = x_ref[...]
            rstd = lax.rsqrt(jnp.mean(xv * xv, axis=-1, keepdims=True) + RMS_EPS)
            hs_ref[tok, :] = (xv * rstd * g_ref[...]).astype(BF16)

        @pl.when((j == 0) & (i == nt - 1))
        def _():
            keep_h.start()

        p_ref[...] = _dot_nt(hs_ref[tok, :], wbuf[...]).astype(BF16)

        @pl.when((j == N_CHIP - 1) & (i == nt - 1))
        def _():
            keep_h.wait()
            for cp in sends:
                cp.wait_send()
            for jj in (1, 2, 3):
                forward(jj).wait_send()
            for cp in copies(rins, routs, rsems):
                cp.wait()

    res = pl.pallas_call(
        body, name=name,
        grid_spec=pltpu.PrefetchScalarGridSpec(
            num_scalar_prefetch=1, grid=(N_CHIP, nt),
            in_specs=[pl.BlockSpec((bt, D), lambda j, i, q: (jnp.where(j == 0, i, nt - 1), 0)),
                      pl.BlockSpec((1, D), lambda j, i, q: (0, 0)), _ANY] + [_ANY] * n_rin,
            out_specs=[pl.BlockSpec((bt, cw), lambda j, i, q: (i, q[j])), _ANY, _ANY] + [_ANY] * n_rout,
            scratch_shapes=[pltpu.VMEM((T, D), BF16), pltpu.VMEM((cw, D), BF16), pltpu.SemaphoreType.DMA((7,)),
                            pltpu.SemaphoreType.DMA((7,)), pltpu.SemaphoreType.DMA((3,))] + r_sems),
        out_shape=[SDS((T, N), BF16), SDS((T, D), BF16), SDS((N, D), BF16)] + r_out,
        input_output_aliases=_rider_aliases(riders, 4, 3),
        compiler_params=_cparams(("arbitrary", "arbitrary")),
    )(chip_order, x, norm_g.reshape(1, D), w_loc, *r_in)
    return res[:3], _split_riders(riders, res[3:])


C_LNG, C_LNB, C_CW0, C_CW1, C_CW2, C_CB, C_PS = range(7)
C_ROWS = 8


def _pool_bands(R, anticausal):
    t = np.arange(R)[:, None]
    s = np.arange(R + CHUNK)[None, :]
    bands = [((s >= t) & (s < t + w)) if anticausal else ((s > t + CHUNK - w) & (s <= t + CHUNK)) for w in POOL_WINDOWS]
    return jnp.asarray(np.stack(bands), dtype=BF16)


def _mixers(p_ref, hxb_ref, hcg_ref, hxc_ref, cv, bsb_ref, wcat_ref, wpool_ref, band_ref, extb,
            first, blk, R, need_grad):
    def seg(lo):
        return p_ref[:, lo:lo + SEG].astype(F32)

    u, v, za = seg(O_U), seg(O_V), seg(O_ZA)
    xb, bg, cg, zb = seg(O_XB), seg(O_BG), seg(O_CG), seg(O_ZB)
    xc, zc = seg(O_XC), seg(O_ZC)
    out = {}

    ug, tu = _gelu(u)
    vg, tv = _gelu(v)
    mu = jnp.mean(vg, axis=-1, keepdims=True)
    vcen = vg - mu
    rs = lax.rsqrt(jnp.mean(vcen * vcen, axis=-1, keepdims=True) + LN_EPS)
    vhat = vcen * rs
    vn = (vhat * cv[C_LNG:C_LNG + 1, :] + cv[C_LNB:C_LNB + 1, :]).astype(BF16)
    lane_group = lax.broadcasted_iota(jnp.int32, (CHUNK, SEG), 1) // HEAD
    zero_b = jnp.zeros((CHUNK, SEG), BF16)
    sgs = []
    for ci in range(R // CHUNK):
        vc = vn[ci * CHUNK:(ci + 1) * CHUNK]
        vst = jnp.concatenate([jnp.where(lane_group == g, vc, zero_b) for g in range(GROUPS)], axis=0)
        sgs.append(_dot(wcat_ref[...], vst) + bsb_ref[...])
    sg = sgs[0] if len(sgs) == 1 else jnp.concatenate(sgs, axis=0)
    a_out = ug * sg
    sa = _sigmoid(za)
    out["a"] = a_out * (za * sa)

    cx = cg * xb
    halo_b = hcg_ref[...].astype(F32) * hxb_ref[...].astype(F32)
    extb[0:HALO, :] = jnp.where(first, 0.0, halo_b)
    extb[HALO:HALO + R, :] = cx
    cx1 = extb[pl.ds(HALO - 1, R), :]
    cx2 = extb[pl.ds(HALO - 2, R), :]
    yconv = (cv[C_CW0:C_CW0 + 1, :] * cx2 + cv[C_CW1:C_CW1 + 1, :] * cx1
             + cv[C_CW2:C_CW2 + 1, :] * cx + cv[C_CB:C_CB + 1, :])
    b_out = bg * yconv
    sb = _sigmoid(zb)
    out["b"] = b_out * (zb * sb)

    halo_c = hxc_ref[...]
    xc_ext = jnp.concatenate([jnp.zeros((CHUNK - HALO, SEG), BF16), jnp.where(first, jnp.zeros_like(halo_c), halo_c),
                              p_ref[:, O_XC:O_XC + SEG]], axis=0)
    tpos = blk * R + lax.broadcasted_iota(jnp.int32, (R, POOL_GROUP), 0) + 1
    pooled, invs, pws = [], [], []
    for gi, w in enumerate(POOL_WINDOWS):
        lo = gi * POOL_GROUP
        win = _dot(band_ref[gi], xc_ext[:, lo:lo + POOL_GROUP])
        inv = 1.0 / jnp.minimum(tpos, w).astype(F32)
        pg = (win * inv - xc[:, lo:lo + POOL_GROUP]).astype(BF16)
        pooled.append(pg)
        invs.append(inv)
        pws.append(_dot(pg, wpool_ref[gi]))
    pw = jnp.concatenate(pws, axis=1)
    c_out = pw * cv[C_PS:C_PS + 1, :]
    sc = _sigmoid(zc)
    out["c"] = c_out * (zc * sc)

    if need_grad:
        out.update(u=u, v=v, tu=tu, tv=tv, ug=ug, sg=sg, a_out=a_out, za=za, sa=sa,
                   rs=rs, vhat=vhat, vn=vn, lane_group=lane_group, zero_b=zero_b,
                   xb=xb, bg=bg, cg=cg, cx=cx, cx1=cx1, cx2=cx2, yconv=yconv, b_out=b_out, zb=zb, sb=sb,
                   pooled=pooled, invs=invs, pw=pw, c_out=c_out, zc=zc, sc=sc)
    return out


def _halo_specs(R, nb, rev):
    step = R // HALO

    def mk(col):
        def imap(i):
            b = (nb - 1 - i) if rev else i
            return (jnp.maximum(b * step - 1, 0), col)
        return pl.BlockSpec((HALO, SEG), imap)

    return [mk(O_XB // SEG), mk(O_CG // SEG), mk(O_XC // SEG)]


def _const_spec(shape):
    nd = len(shape)
    return pl.BlockSpec(shape, lambda i: (0,) * nd, pipeline_mode=pl.Buffered(1))


MIX_FWD_ROWS = 512
MIX_BWD_ROWS = 256


def _mix_block_rows(T, target):
    return _pick(T, target, CHUNK)


def _mix_fwd(p, x, lw, name, riders=()):
    T, D = x.shape
    N = p.shape[1]
    R = _mix_block_rows(T, MIX_FWD_ROWS)
    nb = T // R

    def body(p_ref, hxb, hcg, hxc, x_ref, cv_ref, bsb_ref, wcat_ref, wpool_ref, band_ref, wpa_ref, wpb_ref, wpc_ref,
             wo_ref, xo_ref, ya_ref, yb_ref, yc_ref, extb):
        i = pl.program_id(0)
        cv = cv_ref[...]
        r = _mixers(p_ref, hxb, hcg, hxc, cv, bsb_ref, wcat_ref, wpool_ref, band_ref, extb,
                    i == 0, i, R, False)
        merged = None
        for k, (act, w_ref, y_ref) in enumerate(((r["a"], wpa_ref, ya_ref), (r["b"], wpb_ref, yb_ref),
                                                 (r["c"], wpc_ref, yc_ref))):
            y = _dot(act.astype(BF16), w_ref[...]).astype(BF16)
            y_ref[...] = y
            term = _sigmoid(p_ref[:, O_G + k * D:O_G + (k + 1) * D]) * y
            merged = term if merged is None else merged + term
        xo_ref[...] = x_ref[...] + _dot(merged, wo_ref[...])

    row = lambda w: pl.BlockSpec((R, w), lambda i: (i, 0))
    consts = [lw["cvec"], lw["bsb"], lw["wcat"], lw["wpool"], _pool_bands(R, False), lw["wpa"], lw["wpb"], lw["wpc"],
              lw["wo"]]
    return _host_call(
        body, name, (nb,), riders,
        inputs=[p, p, p, p, x, *consts],
        in_specs=[row(N)] + _halo_specs(R, nb, False) + [row(D)] + [_const_spec(c.shape) for c in consts],
        out_specs=[row(D), row(D), row(D), row(D)],
        out_shape=[SDS((T, D), F32), SDS((T, D), BF16), SDS((T, D), BF16), SDS((T, D), BF16)],
        scratch_shapes=[pltpu.VMEM((HALO + R, SEG), F32)])


def _loss_head(x, final_g, target, name):
    T, D = x.shape
    bt = _pick(T, 512, 8)

    def body(x_ref, g_ref, t_ref, dx_ref, loss_ref, dg_ref):
        @pl.when(pl.program_id(0) == 0)
        def _():
            _zero(loss_ref)
            _zero(dg_ref)

        xv = x_ref[...]
        g = g_ref[...]
        rstd = lax.rsqrt(jnp.mean(xv * xv, axis=-1, keepdims=True) + RMS_EPS)
        xhat = xv * rstd
        err = xhat * g - t_ref[...]
        part = 0.5 * jnp.sum(jnp.sum(err * err, axis=-1, keepdims=True), axis=0, keepdims=True) / D
        loss_ref[...] += jnp.broadcast_to(part, loss_ref.shape)
        dy = err * (1.0 / D)
        dg_ref[0:1, :] += jnp.sum(dy * xhat, axis=0, keepdims=True)
        dxn = dy * g
        dx_ref[...] = rstd * (dxn - xhat * jnp.mean(dxn * xhat, axis=-1, keepdims=True))

    return pl.pallas_call(
        body, name=name, grid=(T // bt,),
        in_specs=[pl.BlockSpec((bt, D), lambda i: (i, 0)), _const_spec((1, D)), pl.BlockSpec((bt, D), lambda i: (i, 0))],
        out_specs=[pl.BlockSpec((bt, D), lambda i: (i, 0)), _const_spec((8, LANES)), _const_spec((8, D))],
        out_shape=[SDS((T, D), F32), SDS((8, LANES), F32), SDS((8, D), F32)],
        compiler_params=_cparams(("arbitrary",)),
    )(x, final_g.reshape(1, D), target)


V_LNG, V_LNB, V_CB, V_PS, V_CW0, V_CW1, V_CW2 = range(7)


def _mix_bwd(p, dxo, ya, yb, yc, lw, name):
    T, D = dxo.shape
    N = p.shape[1]
    R = _mix_block_rows(T, MIX_BWD_ROWS)
    nb = T // R

    def body(p_ref, hxb, hcg, hxc, dxo_ref, ya_ref, yb_ref, yc_ref, cv_ref, bsb_ref, wcat_ref, wcatt_ref,
             wpool_ref, band_ref, bandt_ref, wpa_ref, wpb_ref, wpc_ref, wo_ref,
             dp_ref, acts_ref, mrg_ref, dys_ref, gwc_ref, gbs_ref, gwpool_ref, gvec_ref,
             extb, extdy, cdy, cq, bsacc):
        i = pl.program_id(0)
        blk = nb - 1 - i

        @pl.when(i == 0)
        def _():
            for ref in (gwc_ref, gwpool_ref, gvec_ref, cdy, cq, bsacc):
                _zero(ref)

        cv = cv_ref[...]
        r = _mixers(p_ref, hxb, hcg, hxc, cv, bsb_ref, wcat_ref, wpool_ref, band_ref, extb,
                    blk == 0, blk, R, True)

        dxo_b = dxo_ref[...].astype(BF16)
        dm = _dot_nt(dxo_b, wo_ref[...]).astype(BF16)
        ys = [ya_ref[...], yb_ref[...], yc_ref[...]]
        sig = [_sigmoid(p_ref[:, O_G + k * D:O_G + (k + 1) * D]) for k in range(3)]
        mrg_ref[...] = sig[0] * ys[0] + sig[1] * ys[1] + sig[2] * ys[2]
        dacts = []
        for k, (act, w_ref) in enumerate(((r["a"], wpa_ref), (r["b"], wpb_ref), (r["c"], wpc_ref))):
            dyk = dm * sig[k]
            dp_ref[:, O_G + k * D:O_G + (k + 1) * D] = dyk * ys[k] * (1.0 - sig[k])
            acts_ref[:, k * SEG:(k + 1) * SEG] = act.astype(BF16)
            dys_ref[:, k * D:(k + 1) * D] = dyk
            dacts.append(_dot_nt(dyk, w_ref[...]))
        da, db, dc = dacts

        def silu_bwd(dact, pre, z, s):
            return dact * (z * s), dact * pre * (s * (1.0 + z * (1.0 - s)))

        d_aout, dza = silu_bwd(da, r["a_out"], r["za"], r["sa"])
        dp_ref[:, O_ZA:O_ZA + SEG] = dza.astype(BF16)
        dp_ref[:, O_U:O_U + SEG] = (d_aout * r["sg"] * _gelu_grad(r["u"], r["tu"])).astype(BF16)
        d_sg = d_aout * r["ug"]
        dvns = []
        for ci in range(R // CHUNK):
            dsc = d_sg[ci * CHUNK:(ci + 1) * CHUNK]
            bsacc[...] += dsc
            dsc_b = dsc.astype(BF16)
            dst = jnp.concatenate([jnp.where(r["lane_group"] == g, dsc_b, r["zero_b"]) for g in range(GROUPS)], axis=0)
            dvns.append(_dot(wcatt_ref[...], dst))
            gwc_ref[...] += _dot_nt(dst, r["vn"][ci * CHUNK:(ci + 1) * CHUNK])
        d_vn = dvns[0] if len(dvns) == 1 else jnp.concatenate(dvns, axis=0)
        vhat = r["vhat"]
        gvec_ref[V_LNG:V_LNG + 1, :] += jnp.sum(d_vn * vhat, axis=0, keepdims=True)
        gvec_ref[V_LNB:V_LNB + 1, :] += jnp.sum(d_vn, axis=0, keepdims=True)
        d_vhat = d_vn * cv[C_LNG:C_LNG + 1, :]
        d_vg = r["rs"] * (d_vhat - jnp.mean(d_vhat, axis=-1, keepdims=True)
                          - vhat * jnp.mean(d_vhat * vhat, axis=-1, keepdims=True))
        dp_ref[:, O_V:O_V + SEG] = (d_vg * _gelu_grad(r["v"], r["tv"])).astype(BF16)

        d_bout, dzb = silu_bwd(db, r["b_out"], r["zb"], r["sb"])
        dp_ref[:, O_ZB:O_ZB + SEG] = dzb.astype(BF16)
        dp_ref[:, O_BG:O_BG + SEG] = (d_bout * r["yconv"]).astype(BF16)
        d_y = d_bout * r["bg"]
        gvec_ref[V_CB:V_CB + 1, :] += jnp.sum(d_y, axis=0, keepdims=True)
        gvec_ref[V_CW0:V_CW0 + 1, :] += jnp.sum(d_y * r["cx2"], axis=0, keepdims=True)
        gvec_ref[V_CW1:V_CW1 + 1, :] += jnp.sum(d_y * r["cx1"], axis=0, keepdims=True)
        gvec_ref[V_CW2:V_CW2 + 1, :] += jnp.sum(d_y * r["cx"], axis=0, keepdims=True)
        extdy[0:R, :] = d_y
        extdy[R:R + HALO, :] = cdy[...]
        d_cx = (cv[C_CW2:C_CW2 + 1, :] * d_y + cv[C_CW1:C_CW1 + 1, :] * extdy[pl.ds(1, R), :]
                + cv[C_CW0:C_CW0 + 1, :] * extdy[pl.ds(2, R), :])
        cdy[...] = d_y[0:HALO]
        dp_ref[:, O_CG:O_CG + SEG] = (d_cx * r["xb"]).astype(BF16)
        dp_ref[:, O_XB:O_XB + SEG] = (d_cx * r["cg"]).astype(BF16)

        d_cout, dzc = silu_bwd(dc, r["c_out"], r["zc"], r["sc"])
        dp_ref[:, O_ZC:O_ZC + SEG] = dzc.astype(BF16)
        gvec_ref[V_PS:V_PS + 1, :] += jnp.sum(d_cout * r["pw"], axis=0, keepdims=True)
        d_pw = (d_cout * cv[C_PS:C_PS + 1, :]).astype(BF16)
        dpool, scaled = [], []
        for gi, w in enumerate(POOL_WINDOWS):
            lo = gi * POOL_GROUP
            dpw_g = d_pw[:, lo:lo + POOL_GROUP]
            gwpool_ref[lo:lo + POOL_GROUP, :] += _dot_tn(r["pooled"][gi], dpw_g)
            dpg = _dot_nt(dpw_g, wpool_ref[gi])
            dpool.append(dpg)
            scaled.append((dpg * r["invs"][gi]).astype(BF16))
        q = jnp.concatenate(scaled, axis=1)
        q_ext = jnp.concatenate([q, cq[...], jnp.zeros((CHUNK - HALO, SEG), BF16)], axis=0)
        for gi, w in enumerate(POOL_WINDOWS):
            lo = gi * POOL_GROUP
            acc = _dot(bandt_ref[gi], q_ext[:, lo:lo + POOL_GROUP])
            dp_ref[:, O_XC + lo:O_XC + lo + POOL_GROUP] = (acc - dpool[gi]).astype(BF16)
        cq[...] = q[0:HALO]

        @pl.when(i == nb - 1)
        def _():
            rr = lax.broadcasted_iota(jnp.int32, gwc_ref.shape, 0) % CHUNK
            cc = lax.broadcasted_iota(jnp.int32, gwc_ref.shape, 1)
            gwc_ref[...] = jnp.where(cc <= rr, gwc_ref[...], 0.0)
            acc = bsacc[...]
            hi = acc.astype(BF16)
            lo_ = (acc - hi.astype(F32)).astype(BF16)
            sel = (lax.broadcasted_iota(jnp.int32, (SEG, LANES), 0) // HEAD
                   == lax.broadcasted_iota(jnp.int32, (SEG, LANES), 1)).astype(BF16)
            gbs_ref[...] = _dot(hi, sel) + _dot(lo_, sel)

    row = lambda w: pl.BlockSpec((R, w), lambda i: (nb - 1 - i, 0))
    consts = [lw["cvec"], lw["bsb"], lw["wcat"], lw["wcatt"], lw["wpool"], _pool_bands(R, False), _pool_bands(R, True),
              lw["wpa"], lw["wpb"], lw["wpc"], lw["wo"]]
    acc_shapes = [(GROUPS * CHUNK, CHUNK), (CHUNK, LANES), (SEG, POOL_GROUP), (8, SEG)]
    row_widths = [N, 3 * SEG, D, 3 * D]
    return pl.pallas_call(
        body, name=name, grid=(nb,),
        in_specs=([row(N)] + _halo_specs(R, nb, True) + [row(D), row(D), row(D), row(D)]
                  + [_const_spec(c.shape) for c in consts]),
        out_specs=[row(w) for w in row_widths] + [_const_spec(s) for s in acc_shapes],
        out_shape=[SDS((T, w), BF16) for w in row_widths] + [SDS(s, F32) for s in acc_shapes],
        scratch_shapes=[pltpu.VMEM((HALO + R, SEG), F32)] * 2
        + [pltpu.VMEM((HALO, SEG), F32), pltpu.VMEM((HALO, SEG), BF16), pltpu.VMEM((CHUNK, SEG), F32)],
        compiler_params=_cparams(("arbitrary",)),
    )(p, p, p, p, dxo, ya, yb, yc, *consts)


def _proj_wgrad(acts, merged, dys, dxo, name, riders=()):
    T, D = dxo.shape
    bk = _pick(T, WGRAD_ROWS // 2, 16)

    def body(a_ref, m_ref, dy_ref, dxo_ref, gwpa_ref, gwpb_ref, gwpc_ref, gwo_ref):
        @pl.when(pl.program_id(0) == 0)
        def _():
            for ref in (gwpa_ref, gwpb_ref, gwpc_ref, gwo_ref):
                _zero(ref)

        gwo_ref[...] += _dot_tn(m_ref[...], dxo_ref[...].astype(BF16))
        for k, ref in enumerate((gwpa_ref, gwpb_ref, gwpc_ref)):
            ref[...] += _dot_tn(a_ref[:, k * SEG:(k + 1) * SEG], dy_ref[:, k * D:(k + 1) * D])

    row = lambda w: pl.BlockSpec((bk, w), lambda i: (i, 0))
    shapes = [(SEG, D), (SEG, D), (SEG, D), (D, D)]
    return _host_call(
        body, name, (T // bk,), riders,
        inputs=[acts, merged, dys, dxo],
        in_specs=[row(3 * SEG), row(D), row(3 * D), row(D)],
        out_specs=[_const_spec(s) for s in shapes], out_shape=[SDS(s, F32) for s in shapes],
        scratch_shapes=[])


def _inproj_token_blocks(T):
    return T // _pick(T, BWD_X_ROWS, 16)


def _inproj_bwd_x(dp, win_t, x, norm_g, dxo, name, riders=(), blocks=None, fill=None):
    T, D = x.shape
    N = dp.shape[1]
    bt = _pick(T, BWD_X_ROWS, 16)
    b0, nblk = blocks if blocks else (0, T // bt)

    def compute(dp_ref, w_ref, x_ref, g_ref, dxo_ref, *rest):
        dx_ref, dg_ref = rest[-2:]

        @pl.when(pl.program_id(0) == 0)
        def _():
            _zero(dg_ref)

        dh = _dot(dp_ref[...], w_ref[...])
        xv = x_ref[...]
        rstd = lax.rsqrt(jnp.mean(xv * xv, axis=-1, keepdims=True) + RMS_EPS)
        xhat = xv * rstd
        dg_ref[0:1, :] += jnp.sum(dh * xhat, axis=0, keepdims=True)
        dxn = dh * g_ref[...]
        dx_ref[...] = dxo_ref[...] + rstd * (dxn - xhat * jnp.mean(dxn * xhat, axis=-1, keepdims=True))

    rows = pl.BlockSpec((bt, D), lambda i: (i + b0, 0))
    return _host_call(
        compute, name, (nblk,), riders,
        inputs=[dp, win_t, x, norm_g.reshape(1, D), dxo] + ([] if fill is None else [fill]),
        in_specs=[pl.BlockSpec((bt, N), lambda i: (i + b0, 0)), _const_spec((N, D)), rows, _const_spec((1, D)), rows]
        + ([] if fill is None else [_ANY]),
        out_specs=[rows, _const_spec((8, D))],
        out_shape=[SDS((T, D), F32), SDS((8, D), F32)],
        scratch_shapes=[],
        aliases={} if fill is None else {5: 0})


def _inproj_bwd_w(dp, h, name, riders=()):
    T, N = dp.shape
    D = h.shape[1]
    bn = _pick(N, 1536, LANES)
    bk = _pick(T, WGRAD_ROWS, 16)
    nk = T // bk

    def compute(dp_ref, h_ref, o_ref):
        @pl.when(pl.program_id(1) == 0)
        def _():
            _zero(o_ref)

        o_ref[...] += _dot_tn(dp_ref[...], h_ref[...])

    return _host_call(
        compute, name, (N // bn, nk), riders,
        inputs=[dp, h],
        in_specs=[pl.BlockSpec((bk, bn), lambda j, k: (k, j)), pl.BlockSpec((bk, D), lambda j, k: (k, 0))],
        out_specs=[pl.BlockSpec((bn, D), lambda j, k: (j, 0))],
        out_shape=[SDS((N, D), F32)],
        scratch_shapes=[])


def _chip_peer(x, y, j):
    px = (1 - x) if (j >> 1) else x
    py = (1 - y) if (j & 1) else y
    return px, py


def _blk(ref, kind, k, n):
    if kind == "rows":
        return ref.at[pl.ds(pl.multiple_of(k * n, 8), n)]
    return ref.at[:, pl.ds(pl.multiple_of(k * n, LANES), n)]


class _Exchange:
    def __init__(self, srcs, out_shapes, n_sems, build, alias=None):
        self.srcs, self.out_shapes, self.n_sems, self.build = list(srcs), list(out_shapes), n_sems, build
        self.alias = dict(alias or {})


def _rider_aliases(riders, first_in, first_out):
    out, i, o = {}, first_in, first_out
    for e in riders:
        out.update({i + s: o + d for s, d in e.alias.items()})
        i, o = i + len(e.srcs), o + len(e.out_shapes)
    return out


def _rider_plan(riders):
    inputs = [s for e in riders for s in e.srcs]
    out_shapes = [o for e in riders for o in e.out_shapes]
    sems = [pltpu.SemaphoreType.DMA((e.n_sems,)) for e in riders for _ in range(2)]

    def copies(in_refs, out_refs, sem_refs):
        cps, i, o = [], 0, 0
        for k, e in enumerate(riders):
            ni, no = len(e.srcs), len(e.out_shapes)
            cps += e.build(in_refs[i:i + ni], out_refs[o:o + no], sem_refs[2 * k], sem_refs[2 * k + 1])
            i, o = i + ni, o + no
        return cps

    return inputs, out_shapes, sems, copies


_ANY = pl.BlockSpec(memory_space=pl.ANY)


def _host_call(compute, name, grid, riders, inputs, in_specs, out_specs, out_shape, scratch_shapes, aliases=None):
    r_in, r_out, r_sems, copies = _rider_plan(riders)
    ni, no, ns = len(inputs), len(out_shape), len(scratch_shapes)

    def body(*refs):
        ins, rins = refs[:ni], refs[ni:ni + len(r_in)]
        outs = refs[ni + len(r_in):ni + len(r_in) + no]
        routs = refs[ni + len(r_in) + no:ni + len(r_in) + no + len(r_out)]
        scr = refs[ni + len(r_in) + no + len(r_out):]
        first = functools.reduce(lambda a, b: a & b, [pl.program_id(d) == 0 for d in range(len(grid))])
        last = functools.reduce(lambda a, b: a & b, [pl.program_id(d) == grid[d] - 1 for d in range(len(grid))])
        if riders:
            @pl.when(first)
            def _():
                for cp in copies(rins, routs, scr[ns:]):
                    cp.start()

        compute(*ins, *outs, *scr[:ns])

        if riders:
            @pl.when(last)
            def _():
                for cp in copies(rins, routs, scr[ns:]):
                    cp.wait()

    res = pl.pallas_call(
        body, name=name, grid=grid,
        in_specs=list(in_specs) + [_ANY] * len(r_in),
        out_specs=list(out_specs) + [_ANY] * len(r_out),
        out_shape=list(out_shape) + r_out,
        scratch_shapes=list(scratch_shapes) + r_sems,
        input_output_aliases={**(aliases or {}), **_rider_aliases(riders, ni, no)},
        compiler_params=_cparams(("arbitrary",) * len(grid)),
    )(*inputs, *r_in)
    return res[:no], _split_riders(riders, res[no:])


def _split_riders(riders, flat):
    out, o = [], 0
    for e in riders:
        out.append(list(flat[o:o + len(e.out_shapes)]))
        o += len(e.out_shapes)
    return out


def _run_exchange(ex, name):
    n_in, n_out = len(ex.srcs), len(ex.out_shapes)

    def body(*refs):
        cps = ex.build(refs[:n_in], refs[n_in:n_in + n_out], refs[n_in + n_out], refs[n_in + n_out + 1])
        for cp in cps:
            cp.start()
        for cp in cps:
            cp.wait()

    return pl.pallas_call(
        body, name=name,
        in_specs=[_ANY] * n_in, out_specs=[_ANY] * n_out, out_shape=ex.out_shapes,
        input_output_aliases=ex.alias,
        scratch_shapes=[pltpu.SemaphoreType.DMA((ex.n_sems,)), pltpu.SemaphoreType.DMA((ex.n_sems,))],
        compiler_params=pltpu.CompilerParams(has_side_effects=True),
    )(*ex.srcs)


def _gather_sizes(shards, kinds):
    sizes = [s.shape[0] if k == "rows" else s.shape[1] for s, k in zip(shards, kinds)]
    fulls = [SDS((s.shape[0] * N_DEV,) + s.shape[1:], s.dtype) if k == "rows"
             else SDS((s.shape[0], s.shape[1] * N_DEV), s.dtype) for s, k in zip(shards, kinds)]
    return sizes, fulls


def _gather_direct(shards, kinds):
    n = len(shards)
    sizes, fulls = _gather_sizes(shards, kinds)

    def build(ins, outs, send_sems, recv_sems):
        x, y, c = _me()
        cps = []
        for a in range(n):
            mine = _blk(outs[a], kinds[a], 4 * x + 2 * y + c, sizes[a])
            cps.append(pltpu.make_async_copy(ins[a], mine, send_sems.at[5 * a + 4]))
            for j in range(N_CHIP):
                to = (x, y, 1 - c) if j == 0 else (*_chip_peer(x, y, j), c)
                cps.append(pltpu.make_async_remote_copy(
                    src_ref=ins[a], dst_ref=mine, send_sem=send_sems.at[5 * a + j], recv_sem=recv_sems.at[5 * a + j],
                    device_id=to, device_id_type=MESH))
        return cps

    return _Exchange(shards, fulls, 5 * n, build)


def _gather_everywhere(shards, kinds):
    n = len(shards)
    sizes, fulls = _gather_sizes(shards, kinds)

    def build(ins, outs, send_sems, recv_sems):
        x, y, c = _me()
        cps = []
        for a in range(n):
            mine = _blk(outs[a], kinds[a], 4 * x + 2 * y + c, sizes[a])
            cps.append(pltpu.make_async_copy(ins[a], mine, send_sems.at[N_DEV * a]))
            for d in range(1, N_DEV):
                to = ((1 - x) if d & 4 else x, (1 - y) if d & 2 else y, (1 - c) if d & 1 else c)
                cps.append(pltpu.make_async_remote_copy(
                    src_ref=ins[a], dst_ref=mine, send_sem=send_sems.at[N_DEV * a + d],
                    recv_sem=recv_sems.at[N_DEV * a + d], device_id=to, device_id_type=MESH))
        return cps

    return _Exchange(shards, fulls, N_DEV * n, build)


def _gather_forward(fulls, kinds, sizes):
    n = len(fulls)

    def build(ins, outs, send_sems, recv_sems):
        x, y, c = _me()
        cps = []
        for a in range(n):
            for j in (1, 2, 3):
                px, py = _chip_peer(x, y, j)
                k = 4 * px + 2 * py + c
                cps.append(pltpu.make_async_remote_copy(
                    src_ref=_blk(ins[a], kinds[a], k, sizes[a]), dst_ref=_blk(outs[a], kinds[a], k, sizes[a]),
                    send_sem=send_sems.at[3 * a + j - 1], recv_sem=recv_sems.at[3 * a + j - 1],
                    device_id=(x, y, 1 - c), device_id_type=MESH))
        return cps

    return _Exchange(fulls, [SDS(f.shape, f.dtype) for f in fulls], 3 * n, build, alias={a: a for a in range(n)})


def _sibling_exchange(grads, kinds, sizes):
    n = len(grads)

    def blk_shape(a):
        g = grads[a]
        return (sizes[a],) + g.shape[1:] if kinds[a] == "rows" else (g.shape[0], sizes[a])

    def build(ins, outs, send_sems, recv_sems):
        x, y, c = _me()
        cps = []
        for a in range(n):
            for q in range(N_CHIP):
                cps.append(pltpu.make_async_remote_copy(
                    src_ref=_blk(ins[a], kinds[a], 2 * q + (1 - c), sizes[a]), dst_ref=outs[a].at[q],
                    send_sem=send_sems.at[N_CHIP * a + q], recv_sem=recv_sems.at[N_CHIP * a + q],
                    device_id=(x, y, 1 - c), device_id_type=MESH))
        return cps

    return _Exchange(grads, [SDS((N_CHIP,) + blk_shape(a), F32) for a in range(n)], N_CHIP * n, build)


def _chip_partial(g, r1, kind, size, cidx, name):
    if kind == "rows":
        rows, cols = size, g.shape[1]
        g3 = g.reshape(N_DEV, rows, cols)
        rb = _pick(rows, 512, 16)
        g_spec = pl.BlockSpec((1, rb, cols), lambda q, j, c: (2 * q + c[0], j, 0))
        grid = (N_CHIP, rows // rb)
        blk = (1, rb, cols)
        imap = lambda q, j, c: (q, j, 0)
    else:
        rows, cols = g.shape[0], size
        g3 = g
        g_spec = pl.BlockSpec((rows, cols), lambda q, j, c: (0, 2 * q + c[0]))
        grid = (N_CHIP, 1)
        blk = (1, rows, cols)
        imap = lambda q, j, c: (q, 0, 0)

    def body(c_ref, g_ref, r_ref, p_ref, pb_ref):
        s = g_ref[...].reshape(blk) + r_ref[...]
        p_ref[...] = s
        pb_ref[...] = s.astype(BF16)

    return pl.pallas_call(
        body, name=name,
        grid_spec=pltpu.PrefetchScalarGridSpec(
            num_scalar_prefetch=1, grid=grid,
            in_specs=[g_spec, pl.BlockSpec(blk, imap)],
            out_specs=[pl.BlockSpec(blk, imap), pl.BlockSpec(blk, imap)]),
        out_shape=[SDS((N_CHIP, rows, cols), F32), SDS((N_CHIP, rows, cols), BF16)],
        compiler_params=_cparams(("arbitrary", "arbitrary")),
    )(cidx, g3, r1)


def _chip_exchange(parts):
    n = len(parts)
    m = N_CHIP - 1

    def build(ins, outs, send_sems, recv_sems):
        x, y, c = _me()
        cps = []
        for a in range(n):
            for j in (1, 2, 3):
                px, py = _chip_peer(x, y, j)
                cps.append(pltpu.make_async_remote_copy(
                    src_ref=ins[a].at[2 * px + py], dst_ref=outs[a].at[j - 1], send_sem=send_sems.at[m * a + j - 1],
                    recv_sem=recv_sems.at[m * a + j - 1], device_id=(px, py, c), device_id_type=MESH))
        return cps

    return _Exchange(parts, [SDS((m,) + p.shape[1:], BF16) for p in parts], m * n, build)


def _grad_total(part, r2, qidx, name):
    _, rows, cols = part.shape
    rb = _pick(rows, 512, 16)

    def body(q_ref, p_ref, r_ref, o_ref):
        s = p_ref[0]
        for j in range(N_CHIP - 1):
            s = s + r_ref[j].astype(F32)
        o_ref[...] = s

    return pl.pallas_call(
        body, name=name,
        grid_spec=pltpu.PrefetchScalarGridSpec(
            num_scalar_prefetch=1, grid=(rows // rb,),
            in_specs=[pl.BlockSpec((1, rb, cols), lambda i, q: (q[0], i, 0)),
                      pl.BlockSpec((N_CHIP - 1, rb, cols), lambda i, q: (0, i, 0))],
            out_specs=pl.BlockSpec((rb, cols), lambda i, q: (i, 0))),
        out_shape=SDS((rows, cols), F32),
        compiler_params=_cparams(("arbitrary",)),
    )(qidx, part, r2)


def _all_reduce_small(pack, name):
    rows = pack.shape[0]
    rs = rows // N_DEV
    assert rs * N_DEV == rows and rs % 8 == 0

    def body(x_ref, o_ref, rbuf, red, send1, recv1, send2, recv2):
        x, y, c = _me()
        me = 4 * x + 2 * y + c

        def peer(d):
            px = (1 - x) if (d >> 2) & 1 else x
            py = (1 - y) if (d >> 1) & 1 else y
            pc = (1 - c) if d & 1 else c
            return px, py, pc

        def sl(ref, k):
            return ref.at[pl.ds(pl.multiple_of(k * rs, 8), rs)]

        phase1 = []
        for d in range(1, N_DEV):
            px, py, pc = peer(d)
            phase1.append(pltpu.make_async_remote_copy(
                src_ref=sl(x_ref, 4 * px + 2 * py + pc), dst_ref=rbuf.at[d], send_sem=send1.at[d], recv_sem=recv1.at[d],
                device_id=(px, py, pc), device_id_type=MESH))
        for cp in phase1:
            cp.start()
        acc = sl(x_ref, me)[...]
        for cp in phase1:
            cp.wait()
        for d in range(1, N_DEV):
            acc = acc + rbuf[d]
        red[...] = acc
        sl(o_ref, me)[...] = acc
        phase2 = []
        for d in range(1, N_DEV):
            px, py, pc = peer(d)
            phase2.append(pltpu.make_async_remote_copy(
                src_ref=red, dst_ref=sl(o_ref, me), send_sem=send2.at[d], recv_sem=recv2.at[d],
                device_id=(px, py, pc), device_id_type=MESH))
        for cp in phase2:
            cp.start()
        for cp in phase2:
            cp.wait()

    vm = pl.BlockSpec(memory_space=pltpu.VMEM)
    return pl.pallas_call(
        body, name=name, in_specs=[vm], out_specs=vm, out_shape=SDS(pack.shape, F32),
        scratch_shapes=[pltpu.VMEM((N_DEV, rs, LANES), F32), pltpu.VMEM((rs, LANES), F32),
                        pltpu.SemaphoreType.DMA((N_DEV,)), pltpu.SemaphoreType.DMA((N_DEV,)),
                        pltpu.SemaphoreType.DMA((N_DEV,)), pltpu.SemaphoreType.DMA((N_DEV,))],
        compiler_params=_cparams(None, has_side_effects=True),
    )(pack)


def _adamw(w, g, m, v, name):
    rows, cols = w.shape
    rb = rows if rows * cols * 4 <= ADAMW_WHOLE_BYTES else _pick(rows, 256, 8)
    c1 = 1.0 / (1.0 - ADAM_B1 ** ADAM_STEP)
    c2 = 1.0 / (1.0 - ADAM_B2 ** ADAM_STEP)

    def body(w_ref, g_ref, m_ref, v_ref, d_ref, mo_ref, vo_ref):
        gv = g_ref[...]
        mn = ADAM_B1 * m_ref[...] + (1.0 - ADAM_B1) * gv
        vn = ADAM_B2 * v_ref[...] + (1.0 - ADAM_B2) * (gv * gv)
        mo_ref[...] = mn
        vo_ref[...] = vn
        d_ref[...] = -ADAM_LR * ((mn * c1) / (jnp.sqrt(vn * c2) + ADAM_EPS) + ADAM_WD * w_ref[...])

    spec = pl.BlockSpec((rb, cols), lambda i: (i, 0))
    return pl.pallas_call(
        body, name=name, grid=(rows // rb,),
        in_specs=[spec] * 4, out_specs=[spec] * 3, out_shape=[SDS((rows, cols), F32)] * 3,
        compiler_params=_cparams(("arbitrary",)),
    )(w, g, m, v)


def _pad_rows(a, mult=8):
    r = (-a.shape[0]) % mult
    return a if r == 0 else jnp.pad(a, ((0, r), (0, 0)))


def _as_lanes(a):
    flat = a.reshape(-1)
    pad = (-flat.shape[0]) % (8 * LANES)
    if pad:
        flat = jnp.pad(flat, (0, pad))
    return flat.reshape(-1, LANES)


def kernel(x, norm_g, w_in, ln_g, ln_b, w_s, b_s, conv_w, conv_b, w_pool, pool_scale, w_pa, w_pb, w_pc, w_o, final_g, loss_target, m_norm_g, m_w_in, m_ln_g, m_ln_b, m_w_s, m_b_s, m_conv_w, m_conv_b, m_w_pool, m_pool_scale, m_w_pa, m_w_pb, m_w_pc, m_w_o, m_final_g, v_norm_g, v_w_in, v_ln_g, v_ln_b, v_w_s, v_b_s, v_conv_w, v_conv_b, v_w_pool, v_pool_scale, v_w_pa, v_w_pb, v_w_pc, v_w_o, v_final_g):
    L = w_in.shape[0]
    D = x.shape[-1]
    n_loc = w_in.shape[2]
    pc_loc = w_pa.shape[2]
    x0 = x[0]
    target = loss_target[0]
    xi, yi, ci = _me()
    cidx = jnp.reshape(ci, (1,)).astype(jnp.int32)
    qidx = jnp.reshape(2 * xi + yi, (1,)).astype(jnp.int32)

    kinds5 = ["rows", "cols", "cols", "cols", "rows"]

    def layer_shards(l):
        return [w_in[l].T.astype(BF16), w_pa[l].astype(BF16), w_pb[l].astype(BF16), w_pc[l].astype(BF16),
                w_o[l].astype(BF16)]

    def gathered(direct, shards, kinds, l):
        sizes, _ = _gather_sizes(shards, kinds)
        return _run_exchange(_gather_forward(direct, kinds, sizes), f"weights_forward_{l}")

    cw_loc = _pad_rows(conv_w.reshape(L * CONV_TAPS, -1))
    cw_loc = jnp.pad(cw_loc, ((0, 0), (0, LANES - cw_loc.shape[1])))
    causal = jnp.tril(jnp.ones((CHUNK, CHUNK), dtype=bool))

    chip_order = jnp.stack([2 * xi + yi] + [2 * px + py for px, py in (_chip_peer(xi, yi, j) for j in (1, 2, 3))])
    sh0 = layer_shards(0)
    rest0, krest0 = sh0[1:] + [cw_loc], kinds5[1:] + ["rows"]
    (p0, h0, win_t0), delivered = _inproj_gathering(x0, norm_g[0], sh0[0], chip_order.astype(jnp.int32), "inproj_fwd_0",
                                                    [_gather_everywhere(rest0, krest0)])
    rest0_full = delivered[0]
    cw_all = rest0_full[-1].reshape(N_DEV, -1, LANES)[:, :L * CONV_TAPS, :conv_w.shape[2]]
    conv_w_full = jnp.transpose(cw_all, (1, 0, 2)).reshape(L, CONV_TAPS, -1)

    def make_layer(l, full5):
        win_t, wpa, wpb, wpc, wo = full5
        wm = jnp.where(causal, w_s[l], 0.0)
        cvec = jnp.concatenate([ln_g[l][None], ln_b[l][None], conv_w_full[l], conv_b[l][None], pool_scale[l][None],
                                jnp.zeros((C_ROWS - 7, SEG), F32)], axis=0)
        return dict(
            win_t=win_t, wpa=wpa, wpb=wpb, wpc=wpc, wo=wo, cvec=cvec,
            bsb=jnp.repeat(b_s[l].T, HEAD, axis=1),
            wcat=jnp.transpose(wm, (1, 0, 2)).reshape(CHUNK, GROUPS * CHUNK).astype(BF16),
            wcatt=jnp.transpose(wm, (2, 0, 1)).reshape(CHUNK, GROUPS * CHUNK).astype(BF16),
            wpool=w_pool[l].astype(BF16))

    layers, xs, saved, win_next = [], [x0], [], None
    for l in range(L):
        if l == 0:
            p, h = p0, h0
            lw = make_layer(0, [win_t0] + list(rest0_full[:4]))
        else:
            (p, h), delivered = _inproj(xs[-1], norm_g[l], win_next, f"inproj_fwd_{l}",
                                        [_gather_everywhere(layer_shards(l)[1:], kinds5[1:])])
            lw = make_layer(l, [win_next] + delivered[0])
        layers.append(lw)
        nxt = layer_shards(l + 1)[:1] if l + 1 < L else None
        (xn, ya, yb, yc), delivered = _mix_fwd(p, xs[-1], lw, f"mix_fwd_{l}", [_gather_direct(nxt, kinds5[:1])] if nxt else [])
        if nxt:
            win_next = gathered(delivered[0], nxt, kinds5[:1], l + 1)[0]
        saved.append((p, h, ya, yb, yc))
        xs.append(xn)
    dx, loss_acc, dfg_acc = _loss_head(xs[-1], final_g, target, "loss_head")

    rs_sizes = [n_loc, pc_loc, pc_loc, pc_loc, w_o.shape[1]]
    await_sibling, await_chips = [], []
    partial_of, from_chips = {}, {}
    serial = [0]

    def riders_now():
        riders, plan = [], []
        for grp in await_chips:
            riders.append(_chip_exchange([partial_of[t][1] for t, _, _, _ in grp]))
            plan.append(("chips", grp))
        for grp in await_sibling:
            riders.append(_sibling_exchange([g for _, g, _, _ in grp], [k for _, _, k, _ in grp], [s for _, _, _, s in grp]))
            plan.append(("sibling", grp))
        del await_chips[:], await_sibling[:]
        return riders, plan

    def absorb(plan, delivered):
        for (what, grp), res in zip(plan, delivered):
            for (t, g, k, s), r in zip(grp, res):
                if what == "chips":
                    from_chips[t] = r
                else:
                    partial_of[t] = _chip_partial(g, r, k, s, cidx, f"grad_chip_partial_{t[0]}_{t[1]}")
            if what == "sibling":
                await_chips.append(grp)

    small = [None] * L
    for l in reversed(range(L)):
        lw = layers[l]
        p, h, ya, yb, yc = saved[l]
        dp, acts, merged, dys, gwc, gbs, gwpool, gvec = _mix_bwd(p, dx, ya, yb, yc, lw, f"mix_bwd_{l}")
        riders, plan = riders_now()
        (gwpa, gwpb, gwpc, gwo), delivered = _proj_wgrad(acts, merged, dys, dx, f"proj_wgrad_{l}", riders)
        absorb(plan, delivered)
        await_sibling.append([((l, a), g, kinds5[a], rs_sizes[a]) for a, g in ((1, gwpa), (2, gwpb), (3, gwpc), (4, gwo))])

        def bwd_x(dxo, pieces):
            nt = _inproj_token_blocks(dxo.shape[0])
            pieces = min(pieces, nt)
            done, dng, b0 = None, None, 0
            for k in range(pieces):
                cnt = (nt - b0) // (pieces - k)
                riders, plan = riders_now()
                (done, dng_k), delivered = _inproj_bwd_x(dp, lw["win_t"], xs[l], norm_g[l], dxo, f"inproj_bwd_x_{l}_{k}",
                                                         riders, blocks=(b0, cnt), fill=done)
                absorb(plan, delivered)
                dng = dng_k if dng is None else dng + dng_k
                b0 += cnt
            return done, dng

        def bwd_w():
            riders, plan = riders_now()
            (gwin_t,), delivered = _inproj_bwd_w(dp, h, f"inproj_bwd_w_{l}", riders)
            absorb(plan, delivered)
            await_sibling.append([((l, 0), gwin_t, kinds5[0], rs_sizes[0])])

        if l == L - 1:
            dx, dng = bwd_x(dx, 1)
            bwd_w()
        else:
            bwd_w()
            dx, dng = bwd_x(dx, 2 if l == 0 else 1)
        small[l] = dict(norm_g=dng[0], ln_g=gvec[V_LNG], ln_b=gvec[V_LNB], w_s=gwc, b_s=gbs, conv_w=gvec[V_CW0:V_CW0 + 3],
                        conv_b=gvec[V_CB], w_pool=gwpool, pool_scale=gvec[V_PS])
    while await_sibling or await_chips:
        riders, plan = riders_now()
        delivered = []
        for ex in riders:
            delivered.append(_run_exchange(ex, f"grad_exchange_tail_{serial[0]}"))
            serial[0] += 1
        absorb(plan, delivered)
    grad_x = dx[None]
    big_grads = []
    for l in range(L):
        tot = [_grad_total(partial_of[(l, a)][0], from_chips[(l, a)], qidx, f"grad_total_{l}_{a}") for a in range(5)]
        big_grads.append([tot[0].T,
                          tot[1].reshape(SEG, pc_loc), tot[2].reshape(SEG, pc_loc), tot[3].reshape(SEG, pc_loc),
                          tot[4]])

    names = ["norm_g", "ln_g", "ln_b", "w_s", "b_s", "conv_w", "conv_b", "w_pool", "pool_scale"]
    pieces = [_as_lanes(jnp.stack([small[l][nm] for l in range(L)])) for nm in names]
    pieces += [_as_lanes(dfg_acc[0]), loss_acc]
    sizes = [pc.shape[0] for pc in pieces]
    pack = jnp.concatenate(pieces, axis=0)
    pack = _pad_rows(pack, 8 * N_DEV)
    red = _all_reduce_small(pack, "small_grads_all_reduce")
    offs = [0]
    for s in sizes:
        offs.append(offs[-1] + s)

    def unpack(i, shape):
        n = math.prod(shape)
        return red[offs[i]:offs[i + 1]].reshape(-1)[:n].reshape(shape)

    g_norm_g = unpack(0, (L, D))
    g_ln_g = unpack(1, (L, SEG))
    g_ln_b = unpack(2, (L, SEG))
    g_w_s = unpack(3, (L, GROUPS, CHUNK, CHUNK))
    g_b_s = jnp.transpose(unpack(4, (L, CHUNK, LANES))[:, :, :GROUPS], (0, 2, 1))
    g_conv_w_full = unpack(5, (L, CONV_TAPS, SEG))
    g_conv_b = unpack(6, (L, SEG))
    g_w_pool = unpack(7, (L, len(POOL_WINDOWS), POOL_GROUP, POOL_GROUP))
    g_pool_scale = unpack(8, (L, SEG))
    g_final_g = unpack(9, (D,))
    loss = red[offs[10], 0]
    dev = 4 * xi + 2 * yi + ci
    g_conv_w = lax.dynamic_slice_in_dim(g_conv_w_full, dev * conv_w.shape[2], conv_w.shape[2], axis=2)

    g_w_in = jnp.stack([big_grads[l][0] for l in range(L)])
    g_w_pa = jnp.stack([big_grads[l][1] for l in range(L)])
    g_w_pb = jnp.stack([big_grads[l][2] for l in range(L)])
    g_w_pc = jnp.stack([big_grads[l][3] for l in range(L)])
    g_w_o = jnp.stack([big_grads[l][4] for l in range(L)])

    grads = dict(norm_g=g_norm_g, w_in=g_w_in, ln_g=g_ln_g, ln_b=g_ln_b, w_s=g_w_s, b_s=g_b_s, conv_w=g_conv_w,
                 conv_b=g_conv_b, w_pool=g_w_pool, pool_scale=g_pool_scale, w_pa=g_w_pa, w_pb=g_w_pb, w_pc=g_w_pc,
                 w_o=g_w_o, final_g=g_final_g)
    weights = dict(norm_g=norm_g, w_in=w_in, ln_g=ln_g, ln_b=ln_b, w_s=w_s, b_s=b_s, conv_w=conv_w, conv_b=conv_b,
                   w_pool=w_pool, pool_scale=pool_scale, w_pa=w_pa, w_pb=w_pb, w_pc=w_pc, w_o=w_o, final_g=final_g)
    ms = dict(norm_g=m_norm_g, w_in=m_w_in, ln_g=m_ln_g, ln_b=m_ln_b, w_s=m_w_s, b_s=m_b_s, conv_w=m_conv_w,
              conv_b=m_conv_b, w_pool=m_w_pool, pool_scale=m_pool_scale, w_pa=m_w_pa, w_pb=m_w_pb, w_pc=m_w_pc,
              w_o=m_w_o, final_g=m_final_g)
    vs = dict(norm_g=v_norm_g, w_in=v_w_in, ln_g=v_ln_g, ln_b=v_ln_b, w_s=v_w_s, b_s=v_b_s, conv_w=v_conv_w,
              conv_b=v_conv_b, w_pool=v_w_pool, pool_scale=v_pool_scale, w_pa=v_w_pa, w_pb=v_w_pb, w_pc=v_w_pc,
              w_o=v_w_o, final_g=v_final_g)
    order = ["norm_g", "w_in", "ln_g", "ln_b", "w_s", "b_s", "conv_w", "conv_b", "w_pool", "pool_scale", "w_pa", "w_pb",
             "w_pc", "w_o", "final_g"]

    delta, new_m, new_v = {}, {}, {}
    big = ["w_in", "w_pa", "w_pb", "w_pc", "w_o"]
    for nm in big:
        shp = weights[nm].shape
        two = lambda a: a.reshape(-1, shp[-1])
        d, mn, vn = _adamw(two(weights[nm]), two(grads[nm]), two(ms[nm]), two(vs[nm]), f"adamw_{nm}")
        delta[nm], new_m[nm], new_v[nm] = d.reshape(shp), mn.reshape(shp), vn.reshape(shp)
    rest = [nm for nm in order if nm not in big]
    cat = lambda src: jnp.concatenate([_as_lanes(src[nm]) for nm in rest], axis=0)
    d, mn, vn = _adamw(cat(weights), cat(grads), cat(ms), cat(vs), "adamw_small")
    off = 0
    for nm in rest:
        shp = weights[nm].shape
        n = math.prod(shp)
        rows = _as_lanes(weights[nm]).shape[0]
        cut = lambda a: a[off:off + rows].reshape(-1)[:n].reshape(shp)
        delta[nm], new_m[nm], new_v[nm] = cut(d), cut(mn), cut(vn)
        off += rows

    return (loss, grad_x, *[grads[nm] for nm in order], *[delta[nm] for nm in order],
            *[new_m[nm] for nm in order], *[new_v[nm] for nm in order])
```

```python
import functools
import math

import numpy as np
import jax
import jax.numpy as jnp
from jax import lax
from jax.experimental import pallas as pl
from jax.experimental.pallas import tpu as pltpu

F32 = jnp.float32
BF16 = jnp.bfloat16
SDS = jax.ShapeDtypeStruct
MESH = pl.DeviceIdType.MESH

SEG = 512
CHUNK = 128
GROUPS = 8
HEAD = SEG // GROUPS
POOL_WINDOWS = (2, 4, 8, 16)
POOL_GROUP = SEG // len(POOL_WINDOWS)
CONV_TAPS = 3
HALO = 16
RMS_EPS = 1e-6
LN_EPS = 1e-5
ADAM_LR, ADAM_B1, ADAM_B2, ADAM_EPS, ADAM_WD, ADAM_STEP = 0.001, 0.9, 0.999, 1e-08, 0.01, 10

O_U, O_V, O_ZA, O_XB, O_BG, O_CG, O_ZB, O_XC, O_ZC, O_G = (SEG * i for i in range(10))

N_DEV = 8
N_CHIP = 4
LANES = 128
VMEM_LIMIT = 48 * 1024 * 1024
ADAMW_WHOLE_BYTES = 2 * 1024 * 1024
INPROJ_ROWS = 1024
BWD_X_ROWS = 512
WGRAD_ROWS = 2048


def _cparams(sem=None, **kw):
    return pltpu.CompilerParams(dimension_semantics=sem, vmem_limit_bytes=VMEM_LIMIT, **kw)


def _pick(total, target, mult):
    best = None
    for d in range(mult, min(total, target) + 1, mult):
        if total % d == 0:
            best = d
    assert best is not None, (total, target, mult)
    return best


def _dot(a, b):
    return jnp.dot(a, b, preferred_element_type=F32)


def _dot_nt(a, b):
    return lax.dot_general(a, b, (((1,), (1,)), ((), ())), preferred_element_type=F32)


def _dot_tn(a, b):
    return lax.dot_general(a, b, (((0,), (0,)), ((), ())), preferred_element_type=F32)


def _zero(ref):
    ref[...] = jnp.zeros(ref.shape, ref.dtype)


def _sigmoid(x):
    return 1.0 / (1.0 + jnp.exp(-x))


_GELU_C = math.sqrt(2.0 / math.pi)


def _gelu(x):
    t = jnp.tanh(_GELU_C * (x + 0.044715 * x * x * x))
    return 0.5 * x * (1.0 + t), t


def _gelu_grad(x, t):
    return 0.5 * (1.0 + t) + 0.5 * x * (1.0 - t * t) * _GELU_C * (1.0 + 3.0 * 0.044715 * x * x)


def _me():
    return lax.axis_index("x"), lax.axis_index("y"), lax.axis_index("c")


def _inproj(x, norm_g, win_t, name, riders=()):
    T, D = x.shape
    N = win_t.shape[0]
    bt = _pick(T, INPROJ_ROWS, 16)
    bn = _pick(N, 1536, LANES)
    grid = (T // bt, N // bn)

    def compute(x_ref, g_ref, w_ref, p_ref, h_ref, hs_ref):
        @pl.when(pl.program_id(1) == 0)
        def _():
            xv = x_ref[...]
            rstd = lax.rsqrt(jnp.mean(xv * xv, axis=-1, keepdims=True) + RMS_EPS)
            hb = (xv * rstd * g_ref[...]).astype(BF16)
            hs_ref[...] = hb
            h_ref[...] = hb

        p_ref[...] = _dot_nt(hs_ref[...], w_ref[...]).astype(BF16)

    return _host_call(
        compute, name, grid, riders,
        inputs=[x, norm_g.reshape(1, D), win_t],
        in_specs=[pl.BlockSpec((bt, D), lambda i, j: (i, 0)),
                  pl.BlockSpec((1, D), lambda i, j: (0, 0)),
                  pl.BlockSpec((bn, D), lambda i, j: (j, 0))],
        out_specs=[pl.BlockSpec((bt, bn), lambda i, j: (i, j)),
                   pl.BlockSpec((bt, D), lambda i, j: (i, 0))],
        out_shape=[SDS((T, N), BF16), SDS((T, D), BF16)],
        scratch_shapes=[pltpu.VMEM((bt, D), BF16)])


def _inproj_gathering(x, norm_g, w_loc, chip_order, name, riders=()):
    T, D = x.shape
    n = w_loc.shape[0]
    N = n * N_DEV
    cw = 2 * n
    bt = _pick(T, INPROJ_ROWS, 16)
    nt = T // bt
    r_in, r_out, r_sems, copies = _rider_plan(riders)
    n_rin, n_rout = len(r_in), len(r_out)

    def body(q_ref, x_ref, g_ref, wloc_ref, *rest):
        rins = rest[:n_rin]
        p_ref, h_ref, wfull_ref = rest[n_rin:n_rin + 3]
        routs = rest[n_rin + 3:n_rin + 3 + n_rout]
        hs_ref, wbuf, send_sems, recv_sems, loc_sems = rest[n_rin + 3 + n_rout:n_rin + 8 + n_rout]
        rsems = rest[n_rin + 8 + n_rout:]
        j, i = pl.program_id(0), pl.program_id(1)
        cx, cy, cc = _me()
        sibling = (cx, cy, 1 - cc)

        def rows(k):
            return wfull_ref.at[pl.ds(pl.multiple_of(k * n, 8), n)]

        def shard_copy(slot, src, k, to):
            return pltpu.make_async_remote_copy(src_ref=src, dst_ref=rows(k), send_sem=send_sems.at[slot],
                                                recv_sem=recv_sems.at[slot], device_id=to, device_id_type=MESH)

        me = 4 * cx + 2 * cy + cc
        place_mine = pltpu.make_async_copy(wloc_ref, rows(me), loc_sems.at[0])
        sends = [shard_copy(0, wloc_ref, me, sibling)]
        for jj in (1, 2, 3):
            sends.append(shard_copy(jj, wloc_ref, me, (*_chip_peer(cx, cy, jj), cc)))

        def forward(jj):
            px, py = _chip_peer(cx, cy, jj)
            k = 4 * px + 2 * py + cc
            return shard_copy(3 + jj, rows(k), k, sibling)

        def load_chunk(q):
            cp = pltpu.make_async_copy(wfull_ref.at[pl.ds(pl.multiple_of(q * cw, 8), cw)], wbuf, loc_sems.at[1])
            cp.start()
            cp.wait()

        keep_h = pltpu.make_async_copy(hs_ref, h_ref, loc_sems.at[2])

        @pl.when((j == 0) & (i == 0))
        def _():
            place_mine.start()
            for cp in sends:
                cp.start()
            for cp in copies(rins, routs, rsems):
                cp.start()
            place_mine.wait()
            sends[0].wait_recv()
            load_chunk(q_ref[0])

        for jj in (1, 2, 3):
            @pl.when((j == jj) & (i == 0))
            def _(jj=jj):
                sends[jj].wait_recv()
                fwd = forward(jj)
                fwd.start()
                fwd.wait_recv()
                load_chunk(q_ref[jj])

        tok = pl.ds(pl.multiple_of(i * bt, bt), bt)

        @pl.when(j == 0)
        def _():
            xv = x_ref[...]
            rstd = lax.rsqrt(jnp.mean(xv * xv, axis=-1, keepdims=True) + RMS_EPS)
            hs_ref[tok, :] = (xv * rstd * g_ref[...]).astype(BF16)

        @pl.when((j == 0) & (i == nt - 1))
        def _():
            keep_h.start()

        p_ref[...] = _dot_nt(hs_ref[tok, :], wbuf[...]).astype(BF16)

        @pl.when((j == N_CHIP - 1) & (i == nt - 1))
        def _():
            keep_h.wait()
            for cp in sends:
                cp.wait_send()
            for jj in (1, 2, 3):
                forward(jj).wait_send()
            for cp in copies(rins, routs, rsems):
                cp.wait()

    res = pl.pallas_call(
        body, name=name,
        grid_spec=pltpu.PrefetchScalarGridSpec(
            num_scalar_prefetch=1, grid=(N_CHIP, nt),
            in_specs=[pl.BlockSpec((bt, D), lambda j, i, q: (jnp.where(j == 0, i, nt - 1), 0)),
                      pl.BlockSpec((1, D), lambda j, i, q: (0, 0)), _ANY] + [_ANY] * n_rin,
            out_specs=[pl.BlockSpec((bt, cw), lambda j, i, q: (i, q[j])), _ANY, _ANY] + [_ANY] * n_rout,
            scratch_shapes=[pltpu.VMEM((T, D), BF16), pltpu.VMEM((cw, D), BF16), pltpu.SemaphoreType.DMA((7,)),
                            pltpu.SemaphoreType.DMA((7,)), pltpu.SemaphoreType.DMA((3,))] + r_sems),
        out_shape=[SDS((T, N), BF16), SDS((T, D), BF16), SDS((N, D), BF16)] + r_out,
        input_output_aliases=_rider_aliases(riders, 4, 3),
        compiler_params=_cparams(("arbitrary", "arbitrary")),
    )(chip_order, x, norm_g.reshape(1, D), w_loc, *r_in)
    return res[:3], _split_riders(riders, res[3:])


C_LNG, C_LNB, C_CW0, C_CW1, C_CW2, C_CB, C_PS = range(7)
C_ROWS = 8


def _pool_bands(R, anticausal):
    t = np.arange(R)[:, None]
    s = np.arange(R + CHUNK)[None, :]
    bands = [((s >= t) & (s < t + w)) if anticausal else ((s > t + CHUNK - w) & (s <= t + CHUNK)) for w in POOL_WINDOWS]
    return jnp.asarray(np.stack(bands), dtype=BF16)


def _mixers(p_ref, hxb_ref, hcg_ref, hxc_ref, cv, bsb_ref, wcat_ref, wpool_ref, band_ref, extb,
            first, blk, R, need_grad):
    def seg(lo):
        return p_ref[:, lo:lo + SEG].astype(F32)

    u, v, za = seg(O_U), seg(O_V), seg(O_ZA)
    xb, bg, cg, zb = seg(O_XB), seg(O_BG), seg(O_CG), seg(O_ZB)
    xc, zc = seg(O_XC), seg(O_ZC)
    out = {}

    ug, tu = _gelu(u)
    vg, tv = _gelu(v)
    mu = jnp.mean(vg, axis=-1, keepdims=True)
    vcen = vg - mu
    rs = lax.rsqrt(jnp.mean(vcen * vcen, axis=-1, keepdims=True) + LN_EPS)
    vhat = vcen * rs
    vn = (vhat * cv[C_LNG:C_LNG + 1, :] + cv[C_LNB:C_LNB + 1, :]).astype(BF16)
    lane_group = lax.broadcasted_iota(jnp.int32, (CHUNK, SEG), 1) // HEAD
    zero_b = jnp.zeros((CHUNK, SEG), BF16)
    sgs = []
    for ci in range(R // CHUNK):
        vc = vn[ci * CHUNK:(ci + 1) * CHUNK]
        vst = jnp.concatenate([jnp.where(lane_group == g, vc, zero_b) for g in range(GROUPS)], axis=0)
        sgs.append(_dot(wcat_ref[...], vst) + bsb_ref[...])
    sg = sgs[0] if len(sgs) == 1 else jnp.concatenate(sgs, axis=0)
    a_out = ug * sg
    sa = _sigmoid(za)
    out["a"] = a_out * (za * sa)

    cx = cg * xb
    halo_b = hcg_ref[...].astype(F32) * hxb_ref[...].astype(F32)
    extb[0:HALO, :] = jnp.where(first, 0.0, halo_b)
    extb[HALO:HALO + R, :] = cx
    cx1 = extb[pl.ds(HALO - 1, R), :]
    cx2 = extb[pl.ds(HALO - 2, R), :]
    yconv = (cv[C_CW0:C_CW0 + 1, :] * cx2 + cv[C_CW1:C_CW1 + 1, :] * cx1
             + cv[C_CW2:C_CW2 + 1, :] * cx + cv[C_CB:C_CB + 1, :])
    b_out = bg * yconv
    sb = _sigmoid(zb)
    out["b"] = b_out * (zb * sb)

    halo_c = hxc_ref[...]
    xc_ext = jnp.concatenate([jnp.zeros((CHUNK - HALO, SEG), BF16), jnp.where(first, jnp.zeros_like(halo_c), halo_c),
                              p_ref[:, O_XC:O_XC + SEG]], axis=0)
    tpos = blk * R + lax.broadcasted_iota(jnp.int32, (R, POOL_GROUP), 0) + 1
    pooled, invs, pws = [], [], []
    for gi, w in enumerate(POOL_WINDOWS):
        lo = gi * POOL_GROUP
        win = _dot(band_ref[gi], xc_ext[:, lo:lo + POOL_GROUP])
        inv = 1.0 / jnp.minimum(tpos, w).astype(F32)
        pg = (win * inv - xc[:, lo:lo + POOL_GROUP]).astype(BF16)
        pooled.append(pg)
        invs.append(inv)
        pws.append(_dot(pg, wpool_ref[gi]))
    pw = jnp.concatenate(pws, axis=1)
    c_out = pw * cv[C_PS:C_PS + 1, :]
    sc = _sigmoid(zc)
    out["c"] = c_out * (zc * sc)

    if need_grad:
        out.update(u=u, v=v, tu=tu, tv=tv, ug=ug, sg=sg, a_out=a_out, za=za, sa=sa,
                   rs=rs, vhat=vhat, vn=vn, lane_group=lane_group, zero_b=zero_b,
                   xb=xb, bg=bg, cg=cg, cx=cx, cx1=cx1, cx2=cx2, yconv=yconv, b_out=b_out, zb=zb, sb=sb,
                   pooled=pooled, invs=invs, pw=pw, c_out=c_out, zc=zc, sc=sc)
    return out


def _halo_specs(R, nb, rev):
    step = R // HALO

    def mk(col):
        def imap(i):
            b = (nb - 1 - i) if rev else i
            return (jnp.maximum(b * step - 1, 0), col)
        return pl.BlockSpec((HALO, SEG), imap)

    return [mk(O_XB // SEG), mk(O_CG // SEG), mk(O_XC // SEG)]


def _const_spec(shape):
    nd = len(shape)
    return pl.BlockSpec(shape, lambda i: (0,) * nd, pipeline_mode=pl.Buffered(1))


MIX_FWD_ROWS = 512
MIX_BWD_ROWS = 256


def _mix_block_rows(T, target):
    return _pick(T, target, CHUNK)


def _mix_fwd(p, x, lw, name, riders=()):
    T, D = x.shape
    N = p.shape[1]
    R = _mix_block_rows(T, MIX_FWD_ROWS)
    nb = T // R

    def body(p_ref, hxb, hcg, hxc, x_ref, cv_ref, bsb_ref, wcat_ref, wpool_ref, band_ref, wpa_ref, wpb_ref, wpc_ref,
             wo_ref, xo_ref, ya_ref, yb_ref, yc_ref, extb):
        i = pl.program_id(0)
        cv = cv_ref[...]
        r = _mixers(p_ref, hxb, hcg, hxc, cv, bsb_ref, wcat_ref, wpool_ref, band_ref, extb,
                    i == 0, i, R, False)
        merged = None
        for k, (act, w_ref, y_ref) in enumerate(((r["a"], wpa_ref, ya_ref), (r["b"], wpb_ref, yb_ref),
                                                 (r["c"], wpc_ref, yc_ref))):
            y = _dot(act.astype(BF16), w_ref[...]).astype(BF16)
            y_ref[...] = y
            term = _sigmoid(p_ref[:, O_G + k * D:O_G + (k + 1) * D]) * y
            merged = term if merged is None else merged + term
        xo_ref[...] = x_ref[...] + _dot(merged, wo_ref[...])

    row = lambda w: pl.BlockSpec((R, w), lambda i: (i, 0))
    consts = [lw["cvec"], lw["bsb"], lw["wcat"], lw["wpool"], _pool_bands(R, False), lw["wpa"], lw["wpb"], lw["wpc"],
              lw["wo"]]
    return _host_call(
        body, name, (nb,), riders,
        inputs=[p, p, p, p, x, *consts],
        in_specs=[row(N)] + _halo_specs(R, nb, False) + [row(D)] + [_const_spec(c.shape) for c in consts],
        out_specs=[row(D), row(D), row(D), row(D)],
        out_shape=[SDS((T, D), F32), SDS((T, D), BF16), SDS((T, D), BF16), SDS((T, D), BF16)],
        scratch_shapes=[pltpu.VMEM((HALO + R, SEG), F32)])


def _loss_head(x, final_g, target, name):
    T, D = x.shape
    bt = _pick(T, 512, 8)

    def body(x_ref, g_ref, t_ref, dx_ref, loss_ref, dg_ref):
        @pl.when(pl.program_id(0) == 0)
        def _():
            _zero(loss_ref)
            _zero(dg_ref)

        xv = x_ref[...]
        g = g_ref[...]
        rstd = lax.rsqrt(jnp.mean(xv * xv, axis=-1, keepdims=True) + RMS_EPS)
        xhat = xv * rstd
        err = xhat * g - t_ref[...]
        part = 0.5 * jnp.sum(jnp.sum(err * err, axis=-1, keepdims=True), axis=0, keepdims=True) / D
        loss_ref[...] += jnp.broadcast_to(part, loss_ref.shape)
        dy = err * (1.0 / D)
        dg_ref[0:1, :] += jnp.sum(dy * xhat, axis=0, keepdims=True)
        dxn = dy * g
        dx_ref[...] = rstd * (dxn - xhat * jnp.mean(dxn * xhat, axis=-1, keepdims=True))

    return pl.pallas_call(
        body, name=name, grid=(T // bt,),
        in_specs=[pl.BlockSpec((bt, D), lambda i: (i, 0)), _const_spec((1, D)), pl.BlockSpec((bt, D), lambda i: (i, 0))],
        out_specs=[pl.BlockSpec((bt, D), lambda i: (i, 0)), _const_spec((8, LANES)), _const_spec((8, D))],
        out_shape=[SDS((T, D), F32), SDS((8, LANES), F32), SDS((8, D), F32)],
        compiler_params=_cparams(("arbitrary",)),
    )(x, final_g.reshape(1, D), target)


V_LNG, V_LNB, V_CB, V_PS, V_CW0, V_CW1, V_CW2 = range(7)


def _mix_bwd(p, dxo, ya, yb, yc, lw, name):
    T, D = dxo.shape
    N = p.shape[1]
    R = _mix_block_rows(T, MIX_BWD_ROWS)
    nb = T // R

    def body(p_ref, hxb, hcg, hxc, dxo_ref, ya_ref, yb_ref, yc_ref, cv_ref, bsb_ref, wcat_ref, wcatt_ref,
             wpool_ref, band_ref, bandt_ref, wpa_ref, wpb_ref, wpc_ref, wo_ref,
             dp_ref, acts_ref, mrg_ref, dys_ref, gwc_ref, gbs_ref, gwpool_ref, gvec_ref,
             extb, extdy, cdy, cq, bsacc):
        i = pl.program_id(0)
        blk = nb - 1 - i

        @pl.when(i == 0)
        def _():
            for ref in (gwc_ref, gwpool_ref, gvec_ref, cdy, cq, bsacc):
                _zero(ref)

        cv = cv_ref[...]
        r = _mixers(p_ref, hxb, hcg, hxc, cv, bsb_ref, wcat_ref, wpool_ref, band_ref, extb,
                    blk == 0, blk, R, True)

        dxo_b = dxo_ref[...].astype(BF16)
        dm = _dot_nt(dxo_b, wo_ref[...]).astype(BF16)
        ys = [ya_ref[...], yb_ref[...], yc_ref[...]]
        sig = [_sigmoid(p_ref[:, O_G + k * D:O_G + (k + 1) * D]) for k in range(3)]
        mrg_ref[...] = sig[0] * ys[0] + sig[1] * ys[1] + sig[2] * ys[2]
        dacts = []
        for k, (act, w_ref) in enumerate(((r["a"], wpa_ref), (r["b"], wpb_ref), (r["c"], wpc_ref))):
            dyk = dm * sig[k]
            dp_ref[:, O_G + k * D:O_G + (k + 1) * D] = dyk * ys[k] * (1.0 - sig[k])
            acts_ref[:, k * SEG:(k + 1) * SEG] = act.astype(BF16)
            dys_ref[:, k * D:(k + 1) * D] = dyk
            dacts.append(_dot_nt(dyk, w_ref[...]))
        da, db, dc = dacts

        def silu_bwd(dact, pre, z, s):
            return dact * (z * s), dact * pre * (s * (1.0 + z * (1.0 - s)))

        d_aout, dza = silu_bwd(da, r["a_out"], r["za"], r["sa"])
        dp_ref[:, O_ZA:O_ZA + SEG] = dza.astype(BF16)
        dp_ref[:, O_U:O_U + SEG] = (d_aout * r["sg"] * _gelu_grad(r["u"], r["tu"])).astype(BF16)
        d_sg = d_aout * r["ug"]
        dvns = []
        for ci in range(R // CHUNK):
            dsc = d_sg[ci * CHUNK:(ci + 1) * CHUNK]
            bsacc[...] += dsc
            dsc_b = dsc.astype(BF16)
            dst = jnp.concatenate([jnp.where(r["lane_group"] == g, dsc_b, r["zero_b"]) for g in range(GROUPS)], axis=0)
            dvns.append(_dot(wcatt_ref[...], dst))
            gwc_ref[...] += _dot_nt(dst, r["vn"][ci * CHUNK:(ci + 1) * CHUNK])
        d_vn = dvns[0] if len(dvns) == 1 else jnp.concatenate(dvns, axis=0)
        vhat = r["vhat"]
        gvec_ref[V_LNG:V_LNG + 1, :] += jnp.sum(d_vn * vhat, axis=0, keepdims=True)
        gvec_ref[V_LNB:V_LNB + 1, :] += jnp.sum(d_vn, axis=0, keepdims=True)
        d_vhat = d_vn * cv[C_LNG:C_LNG + 1, :]
        d_vg = r["rs"] * (d_vhat - jnp.mean(d_vhat, axis=-1, keepdims=True)
                          - vhat * jnp.mean(d_vhat * vhat, axis=-1, keepdims=True))
        dp_ref[:, O_V:O_V + SEG] = (d_vg * _gelu_grad(r["v"], r["tv"])).astype(BF16)

        d_bout, dzb = silu_bwd(db, r["b_out"], r["zb"], r["sb"])
        dp_ref[:, O_ZB:O_ZB + SEG] = dzb.astype(BF16)
        dp_ref[:, O_BG:O_BG + SEG] = (d_bout * r["yconv"]).astype(BF16)
        d_y = d_bout * r["bg"]
        gvec_ref[V_CB:V_CB + 1, :] += jnp.sum(d_y, axis=0, keepdims=True)
        gvec_ref[V_CW0:V_CW0 + 1, :] += jnp.sum(d_y * r["cx2"], axis=0, keepdims=True)
        gvec_ref[V_CW1:V_CW1 + 1, :] += jnp.sum(d_y * r["cx1"], axis=0, keepdims=True)
        gvec_ref[V_CW2:V_CW2 + 1, :] += jnp.sum(d_y * r["cx"], axis=0, keepdims=True)
        extdy[0:R, :] = d_y
        extdy[R:R + HALO, :] = cdy[...]
        d_cx = (cv[C_CW2:C_CW2 + 1, :] * d_y + cv[C_CW1:C_CW1 + 1, :] * extdy[pl.ds(1, R), :]
                + cv[C_CW0:C_CW0 + 1, :] * extdy[pl.ds(2, R), :])
        cdy[...] = d_y[0:HALO]
        dp_ref[:, O_CG:O_CG + SEG] = (d_cx * r["xb"]).astype(BF16)
        dp_ref[:, O_XB:O_XB + SEG] = (d_cx * r["cg"]).astype(BF16)

        d_cout, dzc = silu_bwd(dc, r["c_out"], r["zc"], r["sc"])
        dp_ref[:, O_ZC:O_ZC + SEG] = dzc.astype(BF16)
        gvec_ref[V_PS:V_PS + 1, :] += jnp.sum(d_cout * r["pw"], axis=0, keepdims=True)
        d_pw = (d_cout * cv[C_PS:C_PS + 1, :]).astype(BF16)
        dpool, scaled = [], []
        for gi, w in enumerate(POOL_WINDOWS):
            lo = gi * POOL_GROUP
            dpw_g = d_pw[:, lo:lo + POOL_GROUP]
            gwpool_ref[lo:lo + POOL_GROUP, :] += _dot_tn(r["pooled"][gi], dpw_g)
            dpg = _dot_nt(dpw_g, wpool_ref[gi])
            dpool.append(dpg)
            scaled.append((dpg * r["invs"][gi]).astype(BF16))
        q = jnp.concatenate(scaled, axis=1)
        q_ext = jnp.concatenate([q, cq[...], jnp.zeros((CHUNK - HALO, SEG), BF16)], axis=0)
        for gi, w in enumerate(POOL_WINDOWS):
            lo = gi * POOL_GROUP
            acc = _dot(bandt_ref[gi], q_ext[:, lo:lo + POOL_GROUP])
            dp_ref[:, O_XC + lo:O_XC + lo + POOL_GROUP] = (acc - dpool[gi]).astype(BF16)
        cq[...] = q[0:HALO]

        @pl.when(i == nb - 1)
        def _():
            rr = lax.broadcasted_iota(jnp.int32, gwc_ref.shape, 0) % CHUNK
            cc = lax.broadcasted_iota(jnp.int32, gwc_ref.shape, 1)
            gwc_ref[...] = jnp.where(cc <= rr, gwc_ref[...], 0.0)
            acc = bsacc[...]
            hi = acc.astype(BF16)
            lo_ = (acc - hi.astype(F32)).astype(BF16)
            sel = (lax.broadcasted_iota(jnp.int32, (SEG, LANES), 0) // HEAD
                   == lax.broadcasted_iota(jnp.int32, (SEG, LANES), 1)).astype(BF16)
            gbs_ref[...] = _dot(hi, sel) + _dot(lo_, sel)

    row = lambda w: pl.BlockSpec((R, w), lambda i: (nb - 1 - i, 0))
    consts = [lw["cvec"], lw["bsb"], lw["wcat"], lw["wcatt"], lw["wpool"], _pool_bands(R, False), _pool_bands(R, True),
              lw["wpa"], lw["wpb"], lw["wpc"], lw["wo"]]
    acc_shapes = [(GROUPS * CHUNK, CHUNK), (CHUNK, LANES), (SEG, POOL_GROUP), (8, SEG)]
    row_widths = [N, 3 * SEG, D, 3 * D]
    return pl.pallas_call(
        body, name=name, grid=(nb,),
        in_specs=([row(N)] + _halo_specs(R, nb, True) + [row(D), row(D), row(D), row(D)]
                  + [_const_spec(c.shape) for c in consts]),
        out_specs=[row(w) for w in row_widths] + [_const_spec(s) for s in acc_shapes],
        out_shape=[SDS((T, w), BF16) for w in row_widths] + [SDS(s, F32) for s in acc_shapes],
        scratch_shapes=[pltpu.VMEM((HALO + R, SEG), F32)] * 2
        + [pltpu.VMEM((HALO, SEG), F32), pltpu.VMEM((HALO, SEG), BF16), pltpu.VMEM((CHUNK, SEG), F32)],
        compiler_params=_cparams(("arbitrary",)),
    )(p, p, p, p, dxo, ya, yb, yc, *consts)


def _proj_wgrad(acts, merged, dys, dxo, name, riders=()):
    T, D = dxo.shape
    bk = _pick(T, WGRAD_ROWS // 2, 16)

    def body(a_ref, m_ref, dy_ref, dxo_ref, gwpa_ref, gwpb_ref, gwpc_ref, gwo_ref):
        @pl.when(pl.program_id(0) == 0)
        def _():
            for ref in (gwpa_ref, gwpb_ref, gwpc_ref, gwo_ref):
                _zero(ref)

        gwo_ref[...] += _dot_tn(m_ref[...], dxo_ref[...].astype(BF16))
        for k, ref in enumerate((gwpa_ref, gwpb_ref, gwpc_ref)):
            ref[...] += _dot_tn(a_ref[:, k * SEG:(k + 1) * SEG], dy_ref[:, k * D:(k + 1) * D])

    row = lambda w: pl.BlockSpec((bk, w), lambda i: (i, 0))
    shapes = [(SEG, D), (SEG, D), (SEG, D), (D, D)]
    return _host_call(
        body, name, (T // bk,), riders,
        inputs=[acts, merged, dys, dxo],
        in_specs=[row(3 * SEG), row(D), row(3 * D), row(D)],
        out_specs=[_const_spec(s) for s in shapes], out_shape=[SDS(s, F32) for s in shapes],
        scratch_shapes=[])


def _inproj_token_blocks(T):
    return T // _pick(T, BWD_X_ROWS, 16)


def _inproj_bwd_x(dp, win_t, x, norm_g, dxo, name, riders=(), blocks=None, fill=None):
    T, D = x.shape
    N = dp.shape[1]
    bt = _pick(T, BWD_X_ROWS, 16)
    b0, nblk = blocks if blocks else (0, T // bt)

    def compute(dp_ref, w_ref, x_ref, g_ref, dxo_ref, *rest):
        dx_ref, dg_ref = rest[-2:]

        @pl.when(pl.program_id(0) == 0)
        def _():
            _zero(dg_ref)

        dh = _dot(dp_ref[...], w_ref[...])
        xv = x_ref[...]
        rstd = lax.rsqrt(jnp.mean(xv * xv, axis=-1, keepdims=True) + RMS_EPS)
        xhat = xv * rstd
        dg_ref[0:1, :] += jnp.sum(dh * xhat, axis=0, keepdims=True)
        dxn = dh * g_ref[...]
        dx_ref[...] = dxo_ref[...] + rstd * (dxn - xhat * jnp.mean(dxn * xhat, axis=-1, keepdims=True))

    rows = pl.BlockSpec((bt, D), lambda i: (i + b0, 0))
    return _host_call(
        compute, name, (nblk,), riders,
        inputs=[dp, win_t, x, norm_g.reshape(1, D), dxo] + ([] if fill is None else [fill]),
        in_specs=[pl.BlockSpec((bt, N), lambda i: (i + b0, 0)), _const_spec((N, D)), rows, _const_spec((1, D)), rows]
        + ([] if fill is None else [_ANY]),
        out_specs=[rows, _const_spec((8, D))],
        out_shape=[SDS((T, D), F32), SDS((8, D), F32)],
        scratch_shapes=[],
        aliases={} if fill is None else {5: 0})


def _inproj_bwd_w(dp, h, name, riders=()):
    T, N = dp.shape
    D = h.shape[1]
    bn = _pick(N, 1536, LANES)
    bk = _pick(T, WGRAD_ROWS, 16)
    nk = T // bk

    def compute(dp_ref, h_ref, o_ref):
        @pl.when(pl.program_id(1) == 0)
        def _():
            _zero(o_ref)

        o_ref[...] += _dot_tn(dp_ref[...], h_ref[...])

    return _host_call(
        compute, name, (N // bn, nk), riders,
        inputs=[dp, h],
        in_specs=[pl.BlockSpec((bk, bn), lambda j, k: (k, j)), pl.BlockSpec((bk, D), lambda j, k: (k, 0))],
        out_specs=[pl.BlockSpec((bn, D), lambda j, k: (j, 0))],
        out_shape=[SDS((N, D), F32)],
        scratch_shapes=[])


def _chip_peer(x, y, j):
    px = (1 - x) if (j >> 1) else x
    py = (1 - y) if (j & 1) else y
    return px, py


def _blk(ref, kind, k, n):
    if kind == "rows":
        return ref.at[pl.ds(pl.multiple_of(k * n, 8), n)]
    return ref.at[:, pl.ds(pl.multiple_of(k * n, LANES), n)]


class _Exchange:
    def __init__(self, srcs, out_shapes, n_sems, build, alias=None):
        self.srcs, self.out_shapes, self.n_sems, self.build = list(srcs), list(out_shapes), n_sems, build
        self.alias = dict(alias or {})


def _rider_aliases(riders, first_in, first_out):
    out, i, o = {}, first_in, first_out
    for e in riders:
        out.update({i + s: o + d for s, d in e.alias.items()})
        i, o = i + len(e.srcs), o + len(e.out_shapes)
    return out


def _rider_plan(riders):
    inputs = [s for e in riders for s in e.srcs]
    out_shapes = [o for e in riders for o in e.out_shapes]
    sems = [pltpu.SemaphoreType.DMA((e.n_sems,)) for e in riders for _ in range(2)]

    def copies(in_refs, out_refs, sem_refs):
        cps, i, o = [], 0, 0
        for k, e in enumerate(riders):
            ni, no = len(e.srcs), len(e.out_shapes)
            cps += e.build(in_refs[i:i + ni], out_refs[o:o + no], sem_refs[2 * k], sem_refs[2 * k + 1])
            i, o = i + ni, o + no
        return cps

    return inputs, out_shapes, sems, copies


_ANY = pl.BlockSpec(memory_space=pl.ANY)


def _host_call(compute, name, grid, riders, inputs, in_specs, out_specs, out_shape, scratch_shapes, aliases=None):
    r_in, r_out, r_sems, copies = _rider_plan(riders)
    ni, no, ns = len(inputs), len(out_shape), len(scratch_shapes)

    def body(*refs):
        ins, rins = refs[:ni], refs[ni:ni + len(r_in)]
        outs = refs[ni + len(r_in):ni + len(r_in) + no]
        routs = refs[ni + len(r_in) + no:ni + len(r_in) + no + len(r_out)]
        scr = refs[ni + len(r_in) + no + len(r_out):]
        first = functools.reduce(lambda a, b: a & b, [pl.program_id(d) == 0 for d in range(len(grid))])
        last = functools.reduce(lambda a, b: a & b, [pl.program_id(d) == grid[d] - 1 for d in range(len(grid))])
        if riders:
            @pl.when(first)
            def _():
                for cp in copies(rins, routs, scr[ns:]):
                    cp.start()

        compute(*ins, *outs, *scr[:ns])

        if riders:
            @pl.when(last)
            def _():
                for cp in copies(rins, routs, scr[ns:]):
                    cp.wait()

    res = pl.pallas_call(
        body, name=name, grid=grid,
        in_specs=list(in_specs) + [_ANY] * len(r_in),
        out_specs=list(out_specs) + [_ANY] * len(r_out),
        out_shape=list(out_shape) + r_out,
        scratch_shapes=list(scratch_shapes) + r_sems,
        input_output_aliases={**(aliases or {}), **_rider_aliases(riders, ni, no)},
        compiler_params=_cparams(("arbitrary",) * len(grid)),
    )(*inputs, *r_in)
    return res[:no], _split_riders(riders, res[no:])


def _split_riders(riders, flat):
    out, o = [], 0
    for e in riders:
        out.append(list(flat[o:o + len(e.out_shapes)]))
        o += len(e.out_shapes)
    return out


def _run_exchange(ex, name):
    n_in, n_out = len(ex.srcs), len(ex.out_shapes)

    def body(*refs):
        cps = ex.build(refs[:n_in], refs[n_in:n_in + n_out], refs[n_in + n_out], refs[n_in + n_out + 1])
        for cp in cps:
            cp.start()
        for cp in cps:
            cp.wait()

    return pl.pallas_call(
        body, name=name,
        in_specs=[_ANY] * n_in, out_specs=[_ANY] * n_out, out_shape=ex.out_shapes,
        input_output_aliases=ex.alias,
        scratch_shapes=[pltpu.SemaphoreType.DMA((ex.n_sems,)), pltpu.SemaphoreType.DMA((ex.n_sems,))],
        compiler_params=pltpu.CompilerParams(has_side_effects=True),
    )(*ex.srcs)


def _gather_sizes(shards, kinds):
    sizes = [s.shape[0] if k == "rows" else s.shape[1] for s, k in zip(shards, kinds)]
    fulls = [SDS((s.shape[0] * N_DEV,) + s.shape[1:], s.dtype) if k == "rows"
             else SDS((s.shape[0], s.shape[1] * N_DEV), s.dtype) for s, k in zip(shards, kinds)]
    return sizes, fulls


def _gather_direct(shards, kinds):
    n = len(shards)
    sizes, fulls = _gather_sizes(shards, kinds)

    def build(ins, outs, send_sems, recv_sems):
        x, y, c = _me()
        cps = []
        for a in range(n):
            mine = _blk(outs[a], kinds[a], 4 * x + 2 * y + c, sizes[a])
            cps.append(pltpu.make_async_copy(ins[a], mine, send_sems.at[5 * a + 4]))
            for j in range(N_CHIP):
                to = (x, y, 1 - c) if j == 0 else (*_chip_peer(x, y, j), c)
                cps.append(pltpu.make_async_remote_copy(
                    src_ref=ins[a], dst_ref=mine, send_sem=send_sems.at[5 * a + j], recv_sem=recv_sems.at[5 * a + j],
                    device_id=to, device_id_type=MESH))
        return cps

    return _Exchange(shards, fulls, 5 * n, build)


def _gather_everywhere(shards, kinds):
    n = len(shards)
    sizes, fulls = _gather_sizes(shards, kinds)

    def build(ins, outs, send_sems, recv_sems):
        x, y, c = _me()
        cps = []
        for a in range(n):
            mine = _blk(outs[a], kinds[a], 4 * x + 2 * y + c, sizes[a])
            cps.append(pltpu.make_async_copy(ins[a], mine, send_sems.at[N_DEV * a]))
            for d in range(1, N_DEV):
                to = ((1 - x) if d & 4 else x, (1 - y) if d & 2 else y, (1 - c) if d & 1 else c)
                cps.append(pltpu.make_async_remote_copy(
                    src_ref=ins[a], dst_ref=mine, send_sem=send_sems.at[N_DEV * a + d],
                    recv_sem=recv_sems.at[N_DEV * a + d], device_id=to, device_id_type=MESH))
        return cps

    return _Exchange(shards, fulls, N_DEV * n, build)


def _gather_forward(fulls, kinds, sizes):
    n = len(fulls)

    def build(ins, outs, send_sems, recv_sems):
        x, y, c = _me()
        cps = []
        for a in range(n):
            for j in (1, 2, 3):
                px, py = _chip_peer(x, y, j)
                k = 4 * px + 2 * py + c
                cps.append(pltpu.make_async_remote_copy(
                    src_ref=_blk(ins[a], kinds[a], k, sizes[a]), dst_ref=_blk(outs[a], kinds[a], k, sizes[a]),
                    send_sem=send_sems.at[3 * a + j - 1], recv_sem=recv_sems.at[3 * a + j - 1],
                    device_id=(x, y, 1 - c), device_id_type=MESH))
        return cps

    return _Exchange(fulls, [SDS(f.shape, f.dtype) for f in fulls], 3 * n, build, alias={a: a for a in range(n)})


def _sibling_exchange(grads, kinds, sizes):
    n = len(grads)

    def blk_shape(a):
        g = grads[a]
        return (sizes[a],) + g.shape[1:] if kinds[a] == "rows" else (g.shape[0], sizes[a])

    def build(ins, outs, send_sems, recv_sems):
        x, y, c = _me()
        cps = []
        for a in range(n):
            for q in range(N_CHIP):
                cps.append(pltpu.make_async_remote_copy(
                    src_ref=_blk(ins[a], kinds[a], 2 * q + (1 - c), sizes[a]), dst_ref=outs[a].at[q],
                    send_sem=send_sems.at[N_CHIP * a + q], recv_sem=recv_sems.at[N_CHIP * a + q],
                    device_id=(x, y, 1 - c), device_id_type=MESH))
        return cps

    return _Exchange(grads, [SDS((N_CHIP,) + blk_shape(a), F32) for a in range(n)], N_CHIP * n, build)


def _chip_partial(g, r1, kind, size, cidx, name):
    if kind == "rows":
        rows, cols = size, g.shape[1]
        g3 = g.reshape(N_DEV, rows, cols)
        rb = _pick(rows, 512, 16)
        g_spec = pl.BlockSpec((1, rb, cols), lambda q, j, c: (2 * q + c[0], j, 0))
        grid = (N_CHIP, rows // rb)
        blk = (1, rb, cols)
        imap = lambda q, j, c: (q, j, 0)
    else:
        rows, cols = g.shape[0], size
        g3 = g
        g_spec = pl.BlockSpec((rows, cols), lambda q, j, c: (0, 2 * q + c[0]))
        grid = (N_CHIP, 1)
        blk = (1, rows, cols)
        imap = lambda q, j, c: (q, 0, 0)

    def body(c_ref, g_ref, r_ref, p_ref, pb_ref):
        s = g_ref[...].reshape(blk) + r_ref[...]
        p_ref[...] = s
        pb_ref[...] = s.astype(BF16)

    return pl.pallas_call(
        body, name=name,
        grid_spec=pltpu.PrefetchScalarGridSpec(
            num_scalar_prefetch=1, grid=grid,
            in_specs=[g_spec, pl.BlockSpec(blk, imap)],
            out_specs=[pl.BlockSpec(blk, imap), pl.BlockSpec(blk, imap)]),
        out_shape=[SDS((N_CHIP, rows, cols), F32), SDS((N_CHIP, rows, cols), BF16)],
        compiler_params=_cparams(("arbitrary", "arbitrary")),
    )(cidx, g3, r1)


def _chip_exchange(parts):
    n = len(parts)
    m = N_CHIP - 1

    def build(ins, outs, send_sems, recv_sems):
        x, y, c = _me()
        cps = []
        for a in range(n):
            for j in (1, 2, 3):
                px, py = _chip_peer(x, y, j)
                cps.append(pltpu.make_async_remote_copy(
                    src_ref=ins[a].at[2 * px + py], dst_ref=outs[a].at[j - 1], send_sem=send_sems.at[m * a + j - 1],
                    recv_sem=recv_sems.at[m * a + j - 1], device_id=(px, py, c), device_id_type=MESH))
        return cps

    return _Exchange(parts, [SDS((m,) + p.shape[1:], BF16) for p in parts], m * n, build)


def _grad_total(part, r2, qidx, name):
    _, rows, cols = part.shape
    rb = _pick(rows, 512, 16)

    def body(q_ref, p_ref, r_ref, o_ref):
        s = p_ref[0]
        for j in range(N_CHIP - 1):
            s = s + r_ref[j].astype(F32)
        o_ref[...] = s

    return pl.pallas_call(
        body, name=name,
        grid_spec=pltpu.PrefetchScalarGridSpec(
            num_scalar_prefetch=1, grid=(rows // rb,),
            in_specs=[pl.BlockSpec((1, rb, cols), lambda i, q: (q[0], i, 0)),
                      pl.BlockSpec((N_CHIP - 1, rb, cols), lambda i, q: (0, i, 0))],
            out_specs=pl.BlockSpec((rb, cols), lambda i, q: (i, 0))),
        out_shape=SDS((rows, cols), F32),
        compiler_params=_cparams(("arbitrary",)),
    )(qidx, part, r2)


def _all_reduce_small(pack, name):
    rows = pack.shape[0]
    rs = rows // N_DEV
    assert rs * N_DEV == rows and rs % 8 == 0

    def body(x_ref, o_ref, rbuf, red, send1, recv1, send2, recv2):
        x, y, c = _me()
        me = 4 * x + 2 * y + c

        def peer(d):
            px = (1 - x) if (d >> 2) & 1 else x
            py = (1 - y) if (d >> 1) & 1 else y
            pc = (1 - c) if d & 1 else c
            return px, py, pc

        def sl(ref, k):
            return ref.at[pl.ds(pl.multiple_of(k * rs, 8), rs)]

        phase1 = []
        for d in range(1, N_DEV):
            px, py, pc = peer(d)
            phase1.append(pltpu.make_async_remote_copy(
                src_ref=sl(x_ref, 4 * px + 2 * py + pc), dst_ref=rbuf.at[d], send_sem=send1.at[d], recv_sem=recv1.at[d],
                device_id=(px, py, pc), device_id_type=MESH))
        for cp in phase1:
            cp.start()
        acc = sl(x_ref, me)[...]
        for cp in phase1:
            cp.wait()
        for d in range(1, N_DEV):
            acc = acc + rbuf[d]
        red[...] = acc
        sl(o_ref, me)[...] = acc
        phase2 = []
        for d in range(1, N_DEV):
            px, py, pc = peer(d)
            phase2.append(pltpu.make_async_remote_copy(
                src_ref=red, dst_ref=sl(o_ref, me), send_sem=send2.at[d], recv_sem=recv2.at[d],
                device_id=(px, py, pc), device_id_type=MESH))
        for cp in phase2:
            cp.start()
        for cp in phase2:
            cp.wait()

    vm = pl.BlockSpec(memory_space=pltpu.VMEM)
    return pl.pallas_call(
        body, name=name, in_specs=[vm], out_specs=vm, out_shape=SDS(pack.shape, F32),
        scratch_shapes=[pltpu.VMEM((N_DEV, rs, LANES), F32), pltpu.VMEM((rs, LANES), F32),
                        pltpu.SemaphoreType.DMA((N_DEV,)), pltpu.SemaphoreType.DMA((N_DEV,)),
                        pltpu.SemaphoreType.DMA((N_DEV,)), pltpu.SemaphoreType.DMA((N_DEV,))],
        compiler_params=_cparams(None, has_side_effects=True),
    )(pack)


def _adamw(w, g, m, v, name):
    rows, cols = w.shape
    rb = rows if rows * cols * 4 <= ADAMW_WHOLE_BYTES else _pick(rows, 256, 8)
    c1 = 1.0 / (1.0 - ADAM_B1 ** ADAM_STEP)
    c2 = 1.0 / (1.0 - ADAM_B2 ** ADAM_STEP)

    def body(w_ref, g_ref, m_ref, v_ref, d_ref, mo_ref, vo_ref):
        gv = g_ref[...]
        mn = ADAM_B1 * m_ref[...] + (1.0 - ADAM_B1) * gv
        vn = ADAM_B2 * v_ref[...] + (1.0 - ADAM_B2) * (gv * gv)
        mo_ref[...] = mn
        vo_ref[...] = vn
        d_ref[...] = -ADAM_LR * ((mn * c1) / (jnp.sqrt(vn * c2) + ADAM_EPS) + ADAM_WD * w_ref[...])

    spec = pl.BlockSpec((rb, cols), lambda i: (i, 0))
    return pl.pallas_call(
        body, name=name, grid=(rows // rb,),
        in_specs=[spec] * 4, out_specs=[spec] * 3, out_shape=[SDS((rows, cols), F32)] * 3,
        compiler_params=_cparams(("arbitrary",)),
    )(w, g, m, v)


def _pad_rows(a, mult=8):
    r = (-a.shape[0]) % mult
    return a if r == 0 else jnp.pad(a, ((0, r), (0, 0)))


def _as_lanes(a):
    flat = a.reshape(-1)
    pad = (-flat.shape[0]) % (8 * LANES)
    if pad:
        flat = jnp.pad(flat, (0, pad))
    return flat.reshape(-1, LANES)


def kernel(x, norm_g, w_in, ln_g, ln_b, w_s, b_s, conv_w, conv_b, w_pool, pool_scale, w_pa, w_pb, w_pc, w_o, final_g, loss_target, m_norm_g, m_w_in, m_ln_g, m_ln_b, m_w_s, m_b_s, m_conv_w, m_conv_b, m_w_pool, m_pool_scale, m_w_pa, m_w_pb, m_w_pc, m_w_o, m_final_g, v_norm_g, v_w_in, v_ln_g, v_ln_b, v_w_s, v_b_s, v_conv_w, v_conv_b, v_w_pool, v_pool_scale, v_w_pa, v_w_pb, v_w_pc, v_w_o, v_final_g):
    L = w_in.shape[0]
    D = x.shape[-1]
    n_loc = w_in.shape[2]
    pc_loc = w_pa.shape[2]
    x0 = x[0]
    target = loss_target[0]
    xi, yi, ci = _me()
    cidx = jnp.reshape(ci, (1,)).astype(jnp.int32)
    qidx = jnp.reshape(2 * xi + yi, (1,)).astype(jnp.int32)

    kinds5 = ["rows", "cols", "cols", "cols", "rows"]

    def layer_shards(l):
        return [w_in[l].T.astype(BF16), w_pa[l].astype(BF16), w_pb[l].astype(BF16), w_pc[l].astype(BF16),
                w_o[l].astype(BF16)]

    def gathered(direct, shards, kinds, l):
        sizes, _ = _gather_sizes(shards, kinds)
        return _run_exchange(_gather_forward(direct, kinds, sizes), f"weights_forward_{l}")

    cw_loc = _pad_rows(conv_w.reshape(L * CONV_TAPS, -1))
    cw_loc = jnp.pad(cw_loc, ((0, 0), (0, LANES - cw_loc.shape[1])))
    causal = jnp.tril(jnp.ones((CHUNK, CHUNK), dtype=bool))

    chip_order = jnp.stack([2 * xi + yi] + [2 * px + py for px, py in (_chip_peer(xi, yi, j) for j in (1, 2, 3))])
    sh0 = layer_shards(0)
    rest0, krest0 = sh0[1:] + [cw_loc], kinds5[1:] + ["rows"]
    (p0, h0, win_t0), delivered = _inproj_gathering(x0, norm_g[0], sh0[0], chip_order.astype(jnp.int32), "inproj_fwd_0",
                                                    [_gather_everywhere(rest0, krest0)])
    rest0_full = delivered[0]
    cw_all = rest0_full[-1].reshape(N_DEV, -1, LANES)[:, :L * CONV_TAPS, :conv_w.shape[2]]
    conv_w_full = jnp.transpose(cw_all, (1, 0, 2)).reshape(L, CONV_TAPS, -1)

    def make_layer(l, full5):
        win_t, wpa, wpb, wpc, wo = full5
        wm = jnp.where(causal, w_s[l], 0.0)
        cvec = jnp.concatenate([ln_g[l][None], ln_b[l][None], conv_w_full[l], conv_b[l][None], pool_scale[l][None],
                                jnp.zeros((C_ROWS - 7, SEG), F32)], axis=0)
        return dict(
            win_t=win_t, wpa=wpa, wpb=wpb, wpc=wpc, wo=wo, cvec=cvec,
            bsb=jnp.repeat(b_s[l].T, HEAD, axis=1),
            wcat=jnp.transpose(wm, (1, 0, 2)).reshape(CHUNK, GROUPS * CHUNK).astype(BF16),
            wcatt=jnp.transpose(wm, (2, 0, 1)).reshape(CHUNK, GROUPS * CHUNK).astype(BF16),
            wpool=w_pool[l].astype(BF16))

    layers, xs, saved, win_next = [], [x0], [], None
    for l in range(L):
        if l == 0:
            p, h = p0, h0
            lw = make_layer(0, [win_t0] + list(rest0_full[:4]))
        else:
            (p, h), delivered = _inproj(xs[-1], norm_g[l], win_next, f"inproj_fwd_{l}",
                                        [_gather_everywhere(layer_shards(l)[1:], kinds5[1:])])
            lw = make_layer(l, [win_next] + delivered[0])
        layers.append(lw)
        nxt = layer_shards(l + 1)[:1] if l + 1 < L else None
        (xn, ya, yb, yc), delivered = _mix_fwd(p, xs[-1], lw, f"mix_fwd_{l}", [_gather_direct(nxt, kinds5[:1])] if nxt else [])
        if nxt:
            win_next = gathered(delivered[0], nxt, kinds5[:1], l + 1)[0]
        saved.append((p, h, ya, yb, yc))
        xs.append(xn)
    dx, loss_acc, dfg_acc = _loss_head(xs[-1], final_g, target, "loss_head")

    rs_sizes = [n_loc, pc_loc, pc_loc, pc_loc, w_o.shape[1]]
    await_sibling, await_chips = [], []
    partial_of, from_chips = {}, {}
    serial = [0]

    def riders_now():
        riders, plan = [], []
        for grp in await_chips:
            riders.append(_chip_exchange([partial_of[t][1] for t, _, _, _ in grp]))
            plan.append(("chips", grp))
        for grp in await_sibling:
            riders.append(_sibling_exchange([g for _, g, _, _ in grp], [k for _, _, k, _ in grp], [s for _, _, _, s in grp]))
            plan.append(("sibling", grp))
        del await_chips[:], await_sibling[:]
        return riders, plan

    def absorb(plan, delivered):
        for (what, grp), res in zip(plan, delivered):
            for (t, g, k, s), r in zip(grp, res):
                if what == "chips":
                    from_chips[t] = r
                else:
                    partial_of[t] = _chip_partial(g, r, k, s, cidx, f"grad_chip_partial_{t[0]}_{t[1]}")
            if what == "sibling":
                await_chips.append(grp)

    small = [None] * L
    for l in reversed(range(L)):
        lw = layers[l]
        p, h, ya, yb, yc = saved[l]
        dp, acts, merged, dys, gwc, gbs, gwpool, gvec = _mix_bwd(p, dx, ya, yb, yc, lw, f"mix_bwd_{l}")
        riders, plan = riders_now()
        (gwpa, gwpb, gwpc, gwo), delivered = _proj_wgrad(acts, merged, dys, dx, f"proj_wgrad_{l}", riders)
        absorb(plan, delivered)
        await_sibling.append([((l, a), g, kinds5[a], rs_sizes[a]) for a, g in ((1, gwpa), (2, gwpb), (3, gwpc), (4, gwo))])

        def bwd_x(dxo, pieces):
            nt = _inproj_token_blocks(dxo.shape[0])
            pieces = min(pieces, nt)
            done, dng, b0 = None, None, 0
            for k in range(pieces):
                cnt = (nt - b0) // (pieces - k)
                riders, plan = riders_now()
                (done, dng_k), delivered = _inproj_bwd_x(dp, lw["win_t"], xs[l], norm_g[l], dxo, f"inproj_bwd_x_{l}_{k}",
                                                         riders, blocks=(b0, cnt), fill=done)
                absorb(plan, delivered)
                dng = dng_k if dng is None else dng + dng_k
                b0 += cnt
            return done, dng

        def bwd_w():
            riders, plan = riders_now()
            (gwin_t,), delivered = _inproj_bwd_w(dp, h, f"inproj_bwd_w_{l}", riders)
            absorb(plan, delivered)
            await_sibling.append([((l, 0), gwin_t, kinds5[0], rs_sizes[0])])

        if l == L - 1:
            dx, dng = bwd_x(dx, 1)
            bwd_w()
        else:
            bwd_w()
            dx, dng = bwd_x(dx, 2 if l == 0 else 1)
        small[l] = dict(norm_g=dng[0], ln_g=gvec[V_LNG], ln_b=gvec[V_LNB], w_s=gwc, b_s=gbs, conv_w=gvec[V_CW0:V_CW0 + 3],
                        conv_b=gvec[V_CB], w_pool=gwpool, pool_scale=gvec[V_PS])
    while await_sibling or await_chips:
        riders, plan = riders_now()
        delivered = []
        for ex in riders:
            delivered.append(_run_exchange(ex, f"grad_exchange_tail_{serial[0]}"))
            serial[0] += 1
        absorb(plan, delivered)
    grad_x = dx[None]
    big_grads = []
    for l in range(L):
        tot = [_grad_total(partial_of[(l, a)][0], from_chips[(l, a)], qidx, f"grad_total_{l}_{a}") for a in range(5)]
        big_grads.append([tot[0],
                          tot[1].reshape(SEG, pc_loc), tot[2].reshape(SEG, pc_loc), tot[3].reshape(SEG, pc_loc),
                          tot[4]])

    names = ["norm_g", "ln_g", "ln_b", "w_s", "b_s", "conv_w", "conv_b", "w_pool", "pool_scale"]
    pieces = [_as_lanes(jnp.stack([small[l][nm] for l in range(L)])) for nm in names]
    pieces += [_as_lanes(dfg_acc[0]), loss_acc]
    sizes = [pc.shape[0] for pc in pieces]
    pack = jnp.concatenate(pieces, axis=0)
    pack = _pad_rows(pack, 8 * N_DEV)
    red = _all_reduce_small(pack, "small_grads_all_reduce")
    offs = [0]
    for s in sizes:
        offs.append(offs[-1] + s)

    def unpack(i, shape):
        n = math.prod(shape)
        return red[offs[i]:offs[i + 1]].reshape(-1)[:n].reshape(shape)

    g_norm_g = unpack(0, (L, D))
    g_ln_g = unpack(1, (L, SEG))
    g_ln_b = unpack(2, (L, SEG))
    g_w_s = unpack(3, (L, GROUPS, CHUNK, CHUNK))
    g_b_s = jnp.transpose(unpack(4, (L, CHUNK, LANES))[:, :, :GROUPS], (0, 2, 1))
    g_conv_w_full = unpack(5, (L, CONV_TAPS, SEG))
    g_conv_b = unpack(6, (L, SEG))
    g_w_pool = unpack(7, (L, len(POOL_WINDOWS), POOL_GROUP, POOL_GROUP))
    g_pool_scale = unpack(8, (L, SEG))
    g_final_g = unpack(9, (D,))
    loss = red[offs[10], 0]
    dev = 4 * xi + 2 * yi + ci
    g_conv_w = lax.dynamic_slice_in_dim(g_conv_w_full, dev * conv_w.shape[2], conv_w.shape[2], axis=2)

    g_w_in_t = jnp.stack([big_grads[l][0] for l in range(L)])
    g_w_in = jnp.swapaxes(g_w_in_t, 1, 2)
    g_w_pa = jnp.stack([big_grads[l][1] for l in range(L)])
    g_w_pb = jnp.stack([big_grads[l][2] for l in range(L)])
    g_w_pc = jnp.stack([big_grads[l][3] for l in range(L)])
    g_w_o = jnp.stack([big_grads[l][4] for l in range(L)])

    grads = dict(norm_g=g_norm_g, w_in=g_w_in, ln_g=g_ln_g, ln_b=g_ln_b, w_s=g_w_s, b_s=g_b_s, conv_w=g_conv_w,
                 conv_b=g_conv_b, w_pool=g_w_pool, pool_scale=g_pool_scale, w_pa=g_w_pa, w_pb=g_w_pb, w_pc=g_w_pc,
                 w_o=g_w_o, final_g=g_final_g)
    weights = dict(norm_g=norm_g, w_in=w_in, ln_g=ln_g, ln_b=ln_b, w_s=w_s, b_s=b_s, conv_w=conv_w, conv_b=conv_b,
                   w_pool=w_pool, pool_scale=pool_scale, w_pa=w_pa, w_pb=w_pb, w_pc=w_pc, w_o=w_o, final_g=final_g)
    ms = dict(norm_g=m_norm_g, w_in=m_w_in, ln_g=m_ln_g, ln_b=m_ln_b, w_s=m_w_s, b_s=m_b_s, conv_w=m_conv_w,
              conv_b=m_conv_b, w_pool=m_w_pool, pool_scale=m_pool_scale, w_pa=m_w_pa, w_pb=m_w_pb, w_pc=m_w_pc,
              w_o=m_w_o, final_g=m_final_g)
    vs = dict(norm_g=v_norm_g, w_in=v_w_in, ln_g=v_ln_g, ln_b=v_ln_b, w_s=v_w_s, b_s=v_b_s, conv_w=v_conv_w,
              conv_b=v_conv_b, w_pool=v_w_pool, pool_scale=v_pool_scale, w_pa=v_w_pa, w_pb=v_w_pb, w_pc=v_w_pc,
              w_o=v_w_o, final_g=v_final_g)
    order = ["norm_g", "w_in", "ln_g", "ln_b", "w_s", "b_s", "conv_w", "conv_b", "w_pool", "pool_scale", "w_pa", "w_pb",
             "w_pc", "w_o", "final_g"]

    delta, new_m, new_v = {}, {}, {}
    big = ["w_in", "w_pa", "w_pb", "w_pc", "w_o"]
    tr = lambda a: jnp.swapaxes(a, 1, 2)
    for nm in big:
        into = tr if nm == "w_in" else (lambda a: a)
        shp = into(weights[nm]).shape
        two = lambda a: a.reshape(-1, shp[-1])
        g2 = two(g_w_in_t) if nm == "w_in" else two(grads[nm])
        d, mn, vn = _adamw(two(into(weights[nm])), g2, two(into(ms[nm])), two(into(vs[nm])), f"adamw_{nm}")
        delta[nm], new_m[nm], new_v[nm] = (into(a.reshape(shp)) for a in (d, mn, vn))
    rest = [nm for nm in order if nm not in big]
    cat = lambda src: jnp.concatenate([_as_lanes(src[nm]) for nm in rest], axis=0)
    d, mn, vn = _adamw(cat(weights), cat(grads), cat(ms), cat(vs), "adamw_small")
    off = 0
    for nm in rest:
        shp = weights[nm].shape
        n = math.prod(shp)
        rows = _as_lanes(weights[nm]).shape[0]
        cut = lambda a: a[off:off + rows].reshape(-1)[:n].reshape(shp)
        delta[nm], new_m[nm], new_v[nm] = cut(d), cut(mn), cut(vn)
        off += rows

    return (loss, grad_x, *[grads[nm] for nm in order], *[delta[nm] for nm in order],
            *[new_m[nm] for nm in order], *[new_v[nm] for nm in order])
```

```python
import functools
import math

import numpy as np
import jax
import jax.numpy as jnp
from jax import lax
from jax.experimental import pallas as pl
from jax.experimental.pallas import tpu as pltpu

F32 = jnp.float32
BF16 = jnp.bfloat16
SDS = jax.ShapeDtypeStruct
MESH = pl.DeviceIdType.MESH

SEG = 512
CHUNK = 128
GROUPS = 8
HEAD = SEG // GROUPS
POOL_WINDOWS = (2, 4, 8, 16)
POOL_GROUP = SEG // len(POOL_WINDOWS)
CONV_TAPS = 3
HALO = 16
RMS_EPS = 1e-6
LN_EPS = 1e-5
ADAM_LR, ADAM_B1, ADAM_B2, ADAM_EPS, ADAM_WD, ADAM_STEP = 0.001, 0.9, 0.999, 1e-08, 0.01, 10

O_U, O_V, O_ZA, O_XB, O_BG, O_CG, O_ZB, O_XC, O_ZC, O_G = (SEG * i for i in range(10))

N_DEV = 8
N_CHIP = 4
LANES = 128
VMEM_LIMIT = 48 * 1024 * 1024
ADAMW_WHOLE_BYTES = 2 * 1024 * 1024
INPROJ_ROWS = 1024
BWD_X_ROWS = 512
WGRAD_ROWS = 2048
RIDERS_FROM_CHUNK = 2


def _cparams(sem=None, **kw):
    return pltpu.CompilerParams(dimension_semantics=sem, vmem_limit_bytes=VMEM_LIMIT, **kw)


def _pick(total, target, mult):
    best = None
    for d in range(mult, min(total, target) + 1, mult):
        if total % d == 0:
            best = d
    assert best is not None, (total, target, mult)
    return best


def _dot(a, b):
    return jnp.dot(a, b, preferred_element_type=F32)


def _dot_nt(a, b):
    return lax.dot_general(a, b, (((1,), (1,)), ((), ())), preferred_element_type=F32)


def _dot_tn(a, b):
    return lax.dot_general(a, b, (((0,), (0,)), ((), ())), preferred_element_type=F32)


def _zero(ref):
    ref[...] = jnp.zeros(ref.shape, ref.dtype)


def _sigmoid(x):
    return 1.0 / (1.0 + jnp.exp(-x))


_GELU_C = math.sqrt(2.0 / math.pi)


def _gelu(x):
    t = jnp.tanh(_GELU_C * (x + 0.044715 * x * x * x))
    return 0.5 * x * (1.0 + t), t


def _gelu_grad(x, t):
    return 0.5 * (1.0 + t) + 0.5 * x * (1.0 - t * t) * _GELU_C * (1.0 + 3.0 * 0.044715 * x * x)


def _me():
    return lax.axis_index("x"), lax.axis_index("y"), lax.axis_index("c")


def _inproj(x, norm_g, win_t, name, riders=()):
    T, D = x.shape
    N = win_t.shape[0]
    bt = _pick(T, INPROJ_ROWS, 16)
    bn = _pick(N, 1536, LANES)
    grid = (T // bt, N // bn)

    def compute(x_ref, g_ref, w_ref, p_ref, h_ref, hs_ref):
        @pl.when(pl.program_id(1) == 0)
        def _():
            xv = x_ref[...]
            rstd = lax.rsqrt(jnp.mean(xv * xv, axis=-1, keepdims=True) + RMS_EPS)
            hb = (xv * rstd * g_ref[...]).astype(BF16)
            hs_ref[...] = hb
            h_ref[...] = hb

        p_ref[...] = _dot_nt(hs_ref[...], w_ref[...]).astype(BF16)

    return _host_call(
        compute, name, grid, riders,
        inputs=[x, norm_g.reshape(1, D), win_t],
        in_specs=[pl.BlockSpec((bt, D), lambda i, j: (i, 0)),
                  pl.BlockSpec((1, D), lambda i, j: (0, 0)),
                  pl.BlockSpec((bn, D), lambda i, j: (j, 0))],
        out_specs=[pl.BlockSpec((bt, bn), lambda i, j: (i, j)),
                   pl.BlockSpec((bt, D), lambda i, j: (i, 0))],
        out_shape=[SDS((T, N), BF16), SDS((T, D), BF16)],
        scratch_shapes=[pltpu.VMEM((bt, D), BF16)])


def _inproj_gathering(x, norm_g, w_loc, chip_order, name, riders=()):
    T, D = x.shape
    n = w_loc.shape[0]
    N = n * N_DEV
    cw = 2 * n
    bt = _pick(T, INPROJ_ROWS, 16)
    nt = T // bt
    r_in, r_out, r_sems, copies = _rider_plan(riders)
    n_rin, n_rout = len(r_in), len(r_out)

    def body(q_ref, x_ref, g_ref, wloc_ref, *rest):
        rins = rest[:n_rin]
        p_ref, h_ref, wfull_ref = rest[n_rin:n_rin + 3]
        routs = rest[n_rin + 3:n_rin + 3 + n_rout]
        hs_ref, wbuf, send_sems, recv_sems, loc_sems = rest[n_rin + 3 + n_rout:n_rin + 8 + n_rout]
        rsems = rest[n_rin + 8 + n_rout:]
        j, i = pl.program_id(0), pl.program_id(1)
        cx, cy, cc = _me()
        sibling = (cx, cy, 1 - cc)

        def rows(k):
            return wfull_ref.at[pl.ds(pl.multiple_of(k * n, 8), n)]

        def shard_copy(slot, src, k, to):
            return pltpu.make_async_remote_copy(src_ref=src, dst_ref=rows(k), send_sem=send_sems.at[slot],
                                                recv_sem=recv_sems.at[slot], device_id=to, device_id_type=MESH)

        me = 4 * cx + 2 * cy + cc
        place_mine = pltpu.make_async_copy(wloc_ref, rows(me), loc_sems.at[0])
        sends = [shard_copy(0, wloc_ref, me, sibling)]
        for jj in (1, 2, 3):
            sends.append(shard_copy(jj, wloc_ref, me, (*_chip_peer(cx, cy, jj), cc)))

        def forward(jj):
            px, py = _chip_peer(cx, cy, jj)
            k = 4 * px + 2 * py + cc
            return shard_copy(3 + jj, rows(k), k, sibling)

        def load_chunk(q):
            cp = pltpu.make_async_copy(wfull_ref.at[pl.ds(pl.multiple_of(q * cw, 8), cw)], wbuf, loc_sems.at[1])
            cp.start()
            cp.wait()

        keep_h = pltpu.make_async_copy(hs_ref, h_ref, loc_sems.at[2])

        @pl.when((j == 0) & (i == 0))
        def _():
            place_mine.start()
            for cp in sends:
                cp.start()
            place_mine.wait()
            sends[0].wait_recv()
            load_chunk(q_ref[0])

        for jj in (1, 2, 3):
            @pl.when((j == jj) & (i == 0))
            def _(jj=jj):
                sends[jj].wait_recv()
                fwd = forward(jj)
                fwd.start()
                fwd.wait_recv()
                load_chunk(q_ref[jj])
                if jj == RIDERS_FROM_CHUNK:
                    for cp in copies(rins, routs, rsems):
                        cp.start()

        tok = pl.ds(pl.multiple_of(i * bt, bt), bt)

        @pl.when(j == 0)
        def _():
            xv = x_ref[...]
            rstd = lax.rsqrt(jnp.mean(xv * xv, axis=-1, keepdims=True) + RMS_EPS)
            hs_ref[tok, :] = (xv * rstd * g_ref[...]).astype(BF16)

        @pl.when((j == 0) & (i == nt - 1))
        def _():
            keep_h.start()

        p_ref[...] = _dot_nt(hs_ref[tok, :], wbuf[...]).astype(BF16)

        @pl.when((j == N_CHIP - 1) & (i == nt - 1))
        def _():
            keep_h.wait()
            for cp in sends:
                cp.wait_send()
            for jj in (1, 2, 3):
                forward(jj).wait_send()
            for cp in copies(rins, routs, rsems):
                cp.wait()

    res = pl.pallas_call(
        body, name=name,
        grid_spec=pltpu.PrefetchScalarGridSpec(
            num_scalar_prefetch=1, grid=(N_CHIP, nt),
            in_specs=[pl.BlockSpec((bt, D), lambda j, i, q: (jnp.where(j == 0, i, nt - 1), 0)),
                      pl.BlockSpec((1, D), lambda j, i, q: (0, 0)), _ANY] + [_ANY] * n_rin,
            out_specs=[pl.BlockSpec((bt, cw), lambda j, i, q: (i, q[j])), _ANY, _ANY] + [_ANY] * n_rout,
            scratch_shapes=[pltpu.VMEM((T, D), BF16), pltpu.VMEM((cw, D), BF16), pltpu.SemaphoreType.DMA((7,)),
                            pltpu.SemaphoreType.DMA((7,)), pltpu.SemaphoreType.DMA((3,))] + r_sems),
        out_shape=[SDS((T, N), BF16), SDS((T, D), BF16), SDS((N, D), BF16)] + r_out,
        input_output_aliases=_rider_aliases(riders, 4, 3),
        compiler_params=_cparams(("arbitrary", "arbitrary")),
    )(chip_order, x, norm_g.reshape(1, D), w_loc, *r_in)
    return res[:3], _split_riders(riders, res[3:])


C_LNG, C_LNB, C_CW0, C_CW1, C_CW2, C_CB, C_PS = range(7)
C_ROWS = 8


def _pool_bands(R, anticausal):
    t = np.arange(R)[:, None]
    s = np.arange(R + CHUNK)[None, :]
    bands = [((s >= t) & (s < t + w)) if anticausal else ((s > t + CHUNK - w) & (s <= t + CHUNK)) for w in POOL_WINDOWS]
    return jnp.asarray(np.stack(bands), dtype=BF16)


def _mixers(p_ref, hxb_ref, hcg_ref, hxc_ref, cv, bsb_ref, wcat_ref, wpool_ref, band_ref, extb,
            first, blk, R, need_grad):
    def seg(lo):
        return p_ref[:, lo:lo + SEG].astype(F32)

    u, v, za = seg(O_U), seg(O_V), seg(O_ZA)
    xb, bg, cg, zb = seg(O_XB), seg(O_BG), seg(O_CG), seg(O_ZB)
    xc, zc = seg(O_XC), seg(O_ZC)
    out = {}

    ug, tu = _gelu(u)
    vg, tv = _gelu(v)
    mu = jnp.mean(vg, axis=-1, keepdims=True)
    vcen = vg - mu
    rs = lax.rsqrt(jnp.mean(vcen * vcen, axis=-1, keepdims=True) + LN_EPS)
    vhat = vcen * rs
    vn = (vhat * cv[C_LNG:C_LNG + 1, :] + cv[C_LNB:C_LNB + 1, :]).astype(BF16)
    lane_group = lax.broadcasted_iota(jnp.int32, (CHUNK, SEG), 1) // HEAD
    zero_b = jnp.zeros((CHUNK, SEG), BF16)
    sgs = []
    for ci in range(R // CHUNK):
        vc = vn[ci * CHUNK:(ci + 1) * CHUNK]
        vst = jnp.concatenate([jnp.where(lane_group == g, vc, zero_b) for g in range(GROUPS)], axis=0)
        sgs.append(_dot(wcat_ref[...], vst) + bsb_ref[...])
    sg = sgs[0] if len(sgs) == 1 else jnp.concatenate(sgs, axis=0)
    a_out = ug * sg
    sa = _sigmoid(za)
    out["a"] = a_out * (za * sa)

    cx = cg * xb
    halo_b = hcg_ref[...].astype(F32) * hxb_ref[...].astype(F32)
    extb[0:HALO, :] = jnp.where(first, 0.0, halo_b)
    extb[HALO:HALO + R, :] = cx
    cx1 = extb[pl.ds(HALO - 1, R), :]
    cx2 = extb[pl.ds(HALO - 2, R), :]
    yconv = (cv[C_CW0:C_CW0 + 1, :] * cx2 + cv[C_CW1:C_CW1 + 1, :] * cx1
             + cv[C_CW2:C_CW2 + 1, :] * cx + cv[C_CB:C_CB + 1, :])
    b_out = bg * yconv
    sb = _sigmoid(zb)
    out["b"] = b_out * (zb * sb)

    halo_c = hxc_ref[...]
    xc_ext = jnp.concatenate([jnp.zeros((CHUNK - HALO, SEG), BF16), jnp.where(first, jnp.zeros_like(halo_c), halo_c),
                              p_ref[:, O_XC:O_XC + SEG]], axis=0)
    tpos = blk * R + lax.broadcasted_iota(jnp.int32, (R, POOL_GROUP), 0) + 1
    pooled, invs, pws = [], [], []
    for gi, w in enumerate(POOL_WINDOWS):
        lo = gi * POOL_GROUP
        win = _dot(band_ref[gi], xc_ext[:, lo:lo + POOL_GROUP])
        inv = 1.0 / jnp.minimum(tpos, w).astype(F32)
        pg = (win * inv - xc[:, lo:lo + POOL_GROUP]).astype(BF16)
        pooled.append(pg)
        invs.append(inv)
        pws.append(_dot(pg, wpool_ref[gi]))
    pw = jnp.concatenate(pws, axis=1)
    c_out = pw * cv[C_PS:C_PS + 1, :]
    sc = _sigmoid(zc)
    out["c"] = c_out * (zc * sc)

    if need_grad:
        out.update(u=u, v=v, tu=tu, tv=tv, ug=ug, sg=sg, a_out=a_out, za=za, sa=sa,
                   rs=rs, vhat=vhat, vn=vn, lane_group=lane_group, zero_b=zero_b,
                   xb=xb, bg=bg, cg=cg, cx=cx, cx1=cx1, cx2=cx2, yconv=yconv, b_out=b_out, zb=zb, sb=sb,
                   pooled=pooled, invs=invs, pw=pw, c_out=c_out, zc=zc, sc=sc)
    return out


def _halo_specs(R, nb, rev):
    step = R // HALO

    def mk(col):
        def imap(i):
            b = (nb - 1 - i) if rev else i
            return (jnp.maximum(b * step - 1, 0), col)
        return pl.BlockSpec((HALO, SEG), imap)

    return [mk(O_XB // SEG), mk(O_CG // SEG), mk(O_XC // SEG)]


def _const_spec(shape):
    nd = len(shape)
    return pl.BlockSpec(shape, lambda i: (0,) * nd, pipeline_mode=pl.Buffered(1))


MIX_FWD_ROWS = 512
MIX_BWD_ROWS = 256


def _mix_block_rows(T, target):
    return _pick(T, target, CHUNK)


def _mix_fwd(p, x, lw, name, riders=()):
    T, D = x.shape
    N = p.shape[1]
    R = _mix_block_rows(T, MIX_FWD_ROWS)
    nb = T // R

    def body(p_ref, hxb, hcg, hxc, x_ref, cv_ref, bsb_ref, wcat_ref, wpool_ref, band_ref, wpa_ref, wpb_ref, wpc_ref,
             wo_ref, xo_ref, ya_ref, yb_ref, yc_ref, extb):
        i = pl.program_id(0)
        cv = cv_ref[...]
        r = _mixers(p_ref, hxb, hcg, hxc, cv, bsb_ref, wcat_ref, wpool_ref, band_ref, extb,
                    i == 0, i, R, False)
        merged = None
        for k, (act, w_ref, y_ref) in enumerate(((r["a"], wpa_ref, ya_ref), (r["b"], wpb_ref, yb_ref),
                                                 (r["c"], wpc_ref, yc_ref))):
            y = _dot(act.astype(BF16), w_ref[...]).astype(BF16)
            y_ref[...] = y
            term = _sigmoid(p_ref[:, O_G + k * D:O_G + (k + 1) * D]) * y
            merged = term if merged is None else merged + term
        xo_ref[...] = x_ref[...] + _dot(merged, wo_ref[...])

    row = lambda w: pl.BlockSpec((R, w), lambda i: (i, 0))
    consts = [lw["cvec"], lw["bsb"], lw["wcat"], lw["wpool"], _pool_bands(R, False), lw["wpa"], lw["wpb"], lw["wpc"],
              lw["wo"]]
    return _host_call(
        body, name, (nb,), riders,
        inputs=[p, p, p, p, x, *consts],
        in_specs=[row(N)] + _halo_specs(R, nb, False) + [row(D)] + [_const_spec(c.shape) for c in consts],
        out_specs=[row(D), row(D), row(D), row(D)],
        out_shape=[SDS((T, D), F32), SDS((T, D), BF16), SDS((T, D), BF16), SDS((T, D), BF16)],
        scratch_shapes=[pltpu.VMEM((HALO + R, SEG), F32)])


def _loss_head(x, final_g, target, name):
    T, D = x.shape
    bt = _pick(T, 512, 8)

    def body(x_ref, g_ref, t_ref, dx_ref, loss_ref, dg_ref):
        @pl.when(pl.program_id(0) == 0)
        def _():
            _zero(loss_ref)
            _zero(dg_ref)

        xv = x_ref[...]
        g = g_ref[...]
        rstd = lax.rsqrt(jnp.mean(xv * xv, axis=-1, keepdims=True) + RMS_EPS)
        xhat = xv * rstd
        err = xhat * g - t_ref[...]
        part = 0.5 * jnp.sum(jnp.sum(err * err, axis=-1, keepdims=True), axis=0, keepdims=True) / D
        loss_ref[...] += jnp.broadcast_to(part, loss_ref.shape)
        dy = err * (1.0 / D)
        dg_ref[0:1, :] += jnp.sum(dy * xhat, axis=0, keepdims=True)
        dxn = dy * g
        dx_ref[...] = rstd * (dxn - xhat * jnp.mean(dxn * xhat, axis=-1, keepdims=True))

    return pl.pallas_call(
        body, name=name, grid=(T // bt,),
        in_specs=[pl.BlockSpec((bt, D), lambda i: (i, 0)), _const_spec((1, D)), pl.BlockSpec((bt, D), lambda i: (i, 0))],
        out_specs=[pl.BlockSpec((bt, D), lambda i: (i, 0)), _const_spec((8, LANES)), _const_spec((8, D))],
        out_shape=[SDS((T, D), F32), SDS((8, LANES), F32), SDS((8, D), F32)],
        compiler_params=_cparams(("arbitrary",)),
    )(x, final_g.reshape(1, D), target)


V_LNG, V_LNB, V_CB, V_PS, V_CW0, V_CW1, V_CW2 = range(7)


def _mix_bwd(p, dxo, ya, yb, yc, lw, name):
    T, D = dxo.shape
    N = p.shape[1]
    R = _mix_block_rows(T, MIX_BWD_ROWS)
    nb = T // R

    def body(p_ref, hxb, hcg, hxc, dxo_ref, ya_ref, yb_ref, yc_ref, cv_ref, bsb_ref, wcat_ref, wcatt_ref,
             wpool_ref, band_ref, bandt_ref, wpa_ref, wpb_ref, wpc_ref, wo_ref,
             dp_ref, acts_ref, mrg_ref, dys_ref, gwc_ref, gbs_ref, gwpool_ref, gvec_ref,
             extb, extdy, cdy, cq, bsacc):
        i = pl.program_id(0)
        blk = nb - 1 - i

        @pl.when(i == 0)
        def _():
            for ref in (gwc_ref, gwpool_ref, gvec_ref, cdy, cq, bsacc):
                _zero(ref)

        cv = cv_ref[...]
        r = _mixers(p_ref, hxb, hcg, hxc, cv, bsb_ref, wcat_ref, wpool_ref, band_ref, extb,
                    blk == 0, blk, R, True)

        dxo_b = dxo_ref[...].astype(BF16)
        dm = _dot_nt(dxo_b, wo_ref[...]).astype(BF16)
        ys = [ya_ref[...], yb_ref[...], yc_ref[...]]
        sig = [_sigmoid(p_ref[:, O_G + k * D:O_G + (k + 1) * D]) for k in range(3)]
        mrg_ref[...] = sig[0] * ys[0] + sig[1] * ys[1] + sig[2] * ys[2]
        dacts = []
        for k, (act, w_ref) in enumerate(((r["a"], wpa_ref), (r["b"], wpb_ref), (r["c"], wpc_ref))):
            dyk = dm * sig[k]
            dp_ref[:, O_G + k * D:O_G + (k + 1) * D] = dyk * ys[k] * (1.0 - sig[k])
            acts_ref[:, k * SEG:(k + 1) * SEG] = act.astype(BF16)
            dys_ref[:, k * D:(k + 1) * D] = dyk
            dacts.append(_dot_nt(dyk, w_ref[...]))
        da, db, dc = dacts

        def silu_bwd(dact, pre, z, s):
            return dact * (z * s), dact * pre * (s * (1.0 + z * (1.0 - s)))

        d_aout, dza = silu_bwd(da, r["a_out"], r["za"], r["sa"])
        dp_ref[:, O_ZA:O_ZA + SEG] = dza.astype(BF16)
        dp_ref[:, O_U:O_U + SEG] = (d_aout * r["sg"] * _gelu_grad(r["u"], r["tu"])).astype(BF16)
        d_sg = d_aout * r["ug"]
        dvns = []
        for ci in range(R // CHUNK):
            dsc = d_sg[ci * CHUNK:(ci + 1) * CHUNK]
            bsacc[...] += dsc
            dsc_b = dsc.astype(BF16)
            dst = jnp.concatenate([jnp.where(r["lane_group"] == g, dsc_b, r["zero_b"]) for g in range(GROUPS)], axis=0)
            dvns.append(_dot(wcatt_ref[...], dst))
            gwc_ref[...] += _dot_nt(dst, r["vn"][ci * CHUNK:(ci + 1) * CHUNK])
        d_vn = dvns[0] if len(dvns) == 1 else jnp.concatenate(dvns, axis=0)
        vhat = r["vhat"]
        gvec_ref[V_LNG:V_LNG + 1, :] += jnp.sum(d_vn * vhat, axis=0, keepdims=True)
        gvec_ref[V_LNB:V_LNB + 1, :] += jnp.sum(d_vn, axis=0, keepdims=True)
        d_vhat = d_vn * cv[C_LNG:C_LNG + 1, :]
        d_vg = r["rs"] * (d_vhat - jnp.mean(d_vhat, axis=-1, keepdims=True)
                          - vhat * jnp.mean(d_vhat * vhat, axis=-1, keepdims=True))
        dp_ref[:, O_V:O_V + SEG] = (d_vg * _gelu_grad(r["v"], r["tv"])).astype(BF16)

        d_bout, dzb = silu_bwd(db, r["b_out"], r["zb"], r["sb"])
        dp_ref[:, O_ZB:O_ZB + SEG] = dzb.astype(BF16)
        dp_ref[:, O_BG:O_BG + SEG] = (d_bout * r["yconv"]).astype(BF16)
        d_y = d_bout * r["bg"]
        gvec_ref[V_CB:V_CB + 1, :] += jnp.sum(d_y, axis=0, keepdims=True)
        gvec_ref[V_CW0:V_CW0 + 1, :] += jnp.sum(d_y * r["cx2"], axis=0, keepdims=True)
        gvec_ref[V_CW1:V_CW1 + 1, :] += jnp.sum(d_y * r["cx1"], axis=0, keepdims=True)
        gvec_ref[V_CW2:V_CW2 + 1, :] += jnp.sum(d_y * r["cx"], axis=0, keepdims=True)
        extdy[0:R, :] = d_y
        extdy[R:R + HALO, :] = cdy[...]
        d_cx = (cv[C_CW2:C_CW2 + 1, :] * d_y + cv[C_CW1:C_CW1 + 1, :] * extdy[pl.ds(1, R), :]
                + cv[C_CW0:C_CW0 + 1, :] * extdy[pl.ds(2, R), :])
        cdy[...] = d_y[0:HALO]
        dp_ref[:, O_CG:O_CG + SEG] = (d_cx * r["xb"]).astype(BF16)
        dp_ref[:, O_XB:O_XB + SEG] = (d_cx * r["cg"]).astype(BF16)

        d_cout, dzc = silu_bwd(dc, r["c_out"], r["zc"], r["sc"])
        dp_ref[:, O_ZC:O_ZC + SEG] = dzc.astype(BF16)
        gvec_ref[V_PS:V_PS + 1, :] += jnp.sum(d_cout * r["pw"], axis=0, keepdims=True)
        d_pw = (d_cout * cv[C_PS:C_PS + 1, :]).astype(BF16)
        dpool, scaled = [], []
        for gi, w in enumerate(POOL_WINDOWS):
            lo = gi * POOL_GROUP
            dpw_g = d_pw[:, lo:lo + POOL_GROUP]
            gwpool_ref[lo:lo + POOL_GROUP, :] += _dot_tn(r["pooled"][gi], dpw_g)
            dpg = _dot_nt(dpw_g, wpool_ref[gi])
            dpool.append(dpg)
            scaled.append((dpg * r["invs"][gi]).astype(BF16))
        q = jnp.concatenate(scaled, axis=1)
        q_ext = jnp.concatenate([q, cq[...], jnp.zeros((CHUNK - HALO, SEG), BF16)], axis=0)
        for gi, w in enumerate(POOL_WINDOWS):
            lo = gi * POOL_GROUP
            acc = _dot(bandt_ref[gi], q_ext[:, lo:lo + POOL_GROUP])
            dp_ref[:, O_XC + lo:O_XC + lo + POOL_GROUP] = (acc - dpool[gi]).astype(BF16)
        cq[...] = q[0:HALO]

        @pl.when(i == nb - 1)
        def _():
            rr = lax.broadcasted_iota(jnp.int32, gwc_ref.shape, 0) % CHUNK
            cc = lax.broadcasted_iota(jnp.int32, gwc_ref.shape, 1)
            gwc_ref[...] = jnp.where(cc <= rr, gwc_ref[...], 0.0)
            acc = bsacc[...]
            hi = acc.astype(BF16)
            lo_ = (acc - hi.astype(F32)).astype(BF16)
            sel = (lax.broadcasted_iota(jnp.int32, (SEG, LANES), 0) // HEAD
                   == lax.broadcasted_iota(jnp.int32, (SEG, LANES), 1)).astype(BF16)
            gbs_ref[...] = _dot(hi, sel) + _dot(lo_, sel)

    row = lambda w: pl.BlockSpec((R, w), lambda i: (nb - 1 - i, 0))
    consts = [lw["cvec"], lw["bsb"], lw["wcat"], lw["wcatt"], lw["wpool"], _pool_bands(R, False), _pool_bands(R, True),
              lw["wpa"], lw["wpb"], lw["wpc"], lw["wo"]]
    acc_shapes = [(GROUPS * CHUNK, CHUNK), (CHUNK, LANES), (SEG, POOL_GROUP), (8, SEG)]
    row_widths = [N, 3 * SEG, D, 3 * D]
    return pl.pallas_call(
        body, name=name, grid=(nb,),
        in_specs=([row(N)] + _halo_specs(R, nb, True) + [row(D), row(D), row(D), row(D)]
                  + [_const_spec(c.shape) for c in consts]),
        out_specs=[row(w) for w in row_widths] + [_const_spec(s) for s in acc_shapes],
        out_shape=[SDS((T, w), BF16) for w in row_widths] + [SDS(s, F32) for s in acc_shapes],
        scratch_shapes=[pltpu.VMEM((HALO + R, SEG), F32)] * 2
        + [pltpu.VMEM((HALO, SEG), F32), pltpu.VMEM((HALO, SEG), BF16), pltpu.VMEM((CHUNK, SEG), F32)],
        compiler_params=_cparams(("arbitrary",)),
    )(p, p, p, p, dxo, ya, yb, yc, *consts)


def _proj_wgrad(acts, merged, dys, dxo, name, riders=()):
    T, D = dxo.shape
    bk = _pick(T, WGRAD_ROWS // 2, 16)

    def body(a_ref, m_ref, dy_ref, dxo_ref, gwpa_ref, gwpb_ref, gwpc_ref, gwo_ref):
        @pl.when(pl.program_id(0) == 0)
        def _():
            for ref in (gwpa_ref, gwpb_ref, gwpc_ref, gwo_ref):
                _zero(ref)

        gwo_ref[...] += _dot_tn(m_ref[...], dxo_ref[...].astype(BF16))
        for k, ref in enumerate((gwpa_ref, gwpb_ref, gwpc_ref)):
            ref[...] += _dot_tn(a_ref[:, k * SEG:(k + 1) * SEG], dy_ref[:, k * D:(k + 1) * D])

    row = lambda w: pl.BlockSpec((bk, w), lambda i: (i, 0))
    shapes = [(SEG, D), (SEG, D), (SEG, D), (D, D)]
    return _host_call(
        body, name, (T // bk,), riders,
        inputs=[acts, merged, dys, dxo],
        in_specs=[row(3 * SEG), row(D), row(3 * D), row(D)],
        out_specs=[_const_spec(s) for s in shapes], out_shape=[SDS(s, F32) for s in shapes],
        scratch_shapes=[])


def _inproj_token_blocks(T):
    return T // _pick(T, BWD_X_ROWS, 16)


def _inproj_bwd_x(dp, win_t, x, norm_g, dxo, name, riders=(), blocks=None, fill=None):
    T, D = x.shape
    N = dp.shape[1]
    bt = _pick(T, BWD_X_ROWS, 16)
    b0, nblk = blocks if blocks else (0, T // bt)

    def compute(dp_ref, w_ref, x_ref, g_ref, dxo_ref, *rest):
        dx_ref, dg_ref = rest[-2:]

        @pl.when(pl.program_id(0) == 0)
        def _():
            _zero(dg_ref)

        dh = _dot(dp_ref[...], w_ref[...])
        xv = x_ref[...]
        rstd = lax.rsqrt(jnp.mean(xv * xv, axis=-1, keepdims=True) + RMS_EPS)
        xhat = xv * rstd
        dg_ref[0:1, :] += jnp.sum(dh * xhat, axis=0, keepdims=True)
        dxn = dh * g_ref[...]
        dx_ref[...] = dxo_ref[...] + rstd * (dxn - xhat * jnp.mean(dxn * xhat, axis=-1, keepdims=True))

    rows = pl.BlockSpec((bt, D), lambda i: (i + b0, 0))
    return _host_call(
        compute, name, (nblk,), riders,
        inputs=[dp, win_t, x, norm_g.reshape(1, D), dxo] + ([] if fill is None else [fill]),
        in_specs=[pl.BlockSpec((bt, N), lambda i: (i + b0, 0)), _const_spec((N, D)), rows, _const_spec((1, D)), rows]
        + ([] if fill is None else [_ANY]),
        out_specs=[rows, _const_spec((8, D))],
        out_shape=[SDS((T, D), F32), SDS((8, D), F32)],
        scratch_shapes=[],
        aliases={} if fill is None else {5: 0})


def _inproj_bwd_w(dp, h, name, riders=()):
    T, N = dp.shape
    D = h.shape[1]
    bn = _pick(N, 1536, LANES)
    bk = _pick(T, WGRAD_ROWS, 16)
    nk = T // bk

    def compute(dp_ref, h_ref, o_ref):
        @pl.when(pl.program_id(1) == 0)
        def _():
            _zero(o_ref)

        o_ref[...] += _dot_tn(dp_ref[...], h_ref[...])

    return _host_call(
        compute, name, (N // bn, nk), riders,
        inputs=[dp, h],
        in_specs=[pl.BlockSpec((bk, bn), lambda j, k: (k, j)), pl.BlockSpec((bk, D), lambda j, k: (k, 0))],
        out_specs=[pl.BlockSpec((bn, D), lambda j, k: (j, 0))],
        out_shape=[SDS((N, D), F32)],
        scratch_shapes=[])


def _chip_peer(x, y, j):
    px = (1 - x) if (j >> 1) else x
    py = (1 - y) if (j & 1) else y
    return px, py


def _blk(ref, kind, k, n):
    if kind == "rows":
        return ref.at[pl.ds(pl.multiple_of(k * n, 8), n)]
    return ref.at[:, pl.ds(pl.multiple_of(k * n, LANES), n)]


class _Exchange:
    def __init__(self, srcs, out_shapes, n_sems, build, alias=None):
        self.srcs, self.out_shapes, self.n_sems, self.build = list(srcs), list(out_shapes), n_sems, build
        self.alias = dict(alias or {})


def _rider_aliases(riders, first_in, first_out):
    out, i, o = {}, first_in, first_out
    for e in riders:
        out.update({i + s: o + d for s, d in e.alias.items()})
        i, o = i + len(e.srcs), o + len(e.out_shapes)
    return out


def _rider_plan(riders):
    inputs = [s for e in riders for s in e.srcs]
    out_shapes = [o for e in riders for o in e.out_shapes]
    sems = [pltpu.SemaphoreType.DMA((e.n_sems,)) for e in riders for _ in range(2)]

    def copies(in_refs, out_refs, sem_refs):
        cps, i, o = [], 0, 0
        for k, e in enumerate(riders):
            ni, no = len(e.srcs), len(e.out_shapes)
            cps += e.build(in_refs[i:i + ni], out_refs[o:o + no], sem_refs[2 * k], sem_refs[2 * k + 1])
            i, o = i + ni, o + no
        return cps

    return inputs, out_shapes, sems, copies


_ANY = pl.BlockSpec(memory_space=pl.ANY)


def _host_call(compute, name, grid, riders, inputs, in_specs, out_specs, out_shape, scratch_shapes, aliases=None):
    r_in, r_out, r_sems, copies = _rider_plan(riders)
    ni, no, ns = len(inputs), len(out_shape), len(scratch_shapes)

    def body(*refs):
        ins, rins = refs[:ni], refs[ni:ni + len(r_in)]
        outs = refs[ni + len(r_in):ni + len(r_in) + no]
        routs = refs[ni + len(r_in) + no:ni + len(r_in) + no + len(r_out)]
        scr = refs[ni + len(r_in) + no + len(r_out):]
        first = functools.reduce(lambda a, b: a & b, [pl.program_id(d) == 0 for d in range(len(grid))])
        last = functools.reduce(lambda a, b: a & b, [pl.program_id(d) == grid[d] - 1 for d in range(len(grid))])
        if riders:
            @pl.when(first)
            def _():
                for cp in copies(rins, routs, scr[ns:]):
                    cp.start()

        compute(*ins, *outs, *scr[:ns])

        if riders:
            @pl.when(last)
            def _():
                for cp in copies(rins, routs, scr[ns:]):
                    cp.wait()

    res = pl.pallas_call(
        body, name=name, grid=grid,
        in_specs=list(in_specs) + [_ANY] * len(r_in),
        out_specs=list(out_specs) + [_ANY] * len(r_out),
        out_shape=list(out_shape) + r_out,
        scratch_shapes=list(scratch_shapes) + r_sems,
        input_output_aliases={**(aliases or {}), **_rider_aliases(riders, ni, no)},
        compiler_params=_cparams(("arbitrary",) * len(grid)),
    )(*inputs, *r_in)
    return res[:no], _split_riders(riders, res[no:])


def _split_riders(riders, flat):
    out, o = [], 0
    for e in riders:
        out.append(list(flat[o:o + len(e.out_shapes)]))
        o += len(e.out_shapes)
    return out


def _run_exchange(ex, name):
    n_in, n_out = len(ex.srcs), len(ex.out_shapes)

    def body(*refs):
        cps = ex.build(refs[:n_in], refs[n_in:n_in + n_out], refs[n_in + n_out], refs[n_in + n_out + 1])
        for cp in cps:
            cp.start()
        for cp in cps:
            cp.wait()

    return pl.pallas_call(
        body, name=name,
        in_specs=[_ANY] * n_in, out_specs=[_ANY] * n_out, out_shape=ex.out_shapes,
        input_output_aliases=ex.alias,
        scratch_shapes=[pltpu.SemaphoreType.DMA((ex.n_sems,)), pltpu.SemaphoreType.DMA((ex.n_sems,))],
        compiler_params=pltpu.CompilerParams(has_side_effects=True),
    )(*ex.srcs)


def _gather_sizes(shards, kinds):
    sizes = [s.shape[0] if k == "rows" else s.shape[1] for s, k in zip(shards, kinds)]
    fulls = [SDS((s.shape[0] * N_DEV,) + s.shape[1:], s.dtype) if k == "rows"
             else SDS((s.shape[0], s.shape[1] * N_DEV), s.dtype) for s, k in zip(shards, kinds)]
    return sizes, fulls


def _gather_direct(shards, kinds):
    n = len(shards)
    sizes, fulls = _gather_sizes(shards, kinds)

    def build(ins, outs, send_sems, recv_sems):
        x, y, c = _me()
        cps = []
        for a in range(n):
            mine = _blk(outs[a], kinds[a], 4 * x + 2 * y + c, sizes[a])
            cps.append(pltpu.make_async_copy(ins[a], mine, send_sems.at[5 * a + 4]))
            for j in range(N_CHIP):
                to = (x, y, 1 - c) if j == 0 else (*_chip_peer(x, y, j), c)
                cps.append(pltpu.make_async_remote_copy(
                    src_ref=ins[a], dst_ref=mine, send_sem=send_sems.at[5 * a + j], recv_sem=recv_sems.at[5 * a + j],
                    device_id=to, device_id_type=MESH))
        return cps

    return _Exchange(shards, fulls, 5 * n, build)


def _gather_everywhere(shards, kinds):
    n = len(shards)
    sizes, fulls = _gather_sizes(shards, kinds)

    def build(ins, outs, send_sems, recv_sems):
        x, y, c = _me()
        cps = []
        for a in range(n):
            mine = _blk(outs[a], kinds[a], 4 * x + 2 * y + c, sizes[a])
            cps.append(pltpu.make_async_copy(ins[a], mine, send_sems.at[N_DEV * a]))
            for d in range(1, N_DEV):
                to = ((1 - x) if d & 4 else x, (1 - y) if d & 2 else y, (1 - c) if d & 1 else c)
                cps.append(pltpu.make_async_remote_copy(
                    src_ref=ins[a], dst_ref=mine, send_sem=send_sems.at[N_DEV * a + d],
                    recv_sem=recv_sems.at[N_DEV * a + d], device_id=to, device_id_type=MESH))
        return cps

    return _Exchange(shards, fulls, N_DEV * n, build)


def _gather_forward(fulls, kinds, sizes):
    n = len(fulls)

    def build(ins, outs, send_sems, recv_sems):
        x, y, c = _me()
        cps = []
        for a in range(n):
            for j in (1, 2, 3):
                px, py = _chip_peer(x, y, j)
                k = 4 * px + 2 * py + c
                cps.append(pltpu.make_async_remote_copy(
                    src_ref=_blk(ins[a], kinds[a], k, sizes[a]), dst_ref=_blk(outs[a], kinds[a], k, sizes[a]),
                    send_sem=send_sems.at[3 * a + j - 1], recv_sem=recv_sems.at[3 * a + j - 1],
                    device_id=(x, y, 1 - c), device_id_type=MESH))
        return cps

    return _Exchange(fulls, [SDS(f.shape, f.dtype) for f in fulls], 3 * n, build, alias={a: a for a in range(n)})


def _sibling_exchange(grads, kinds, sizes):
    n = len(grads)

    def blk_shape(a):
        g = grads[a]
        return (sizes[a],) + g.shape[1:] if kinds[a] == "rows" else (g.shape[0], sizes[a])

    def build(ins, outs, send_sems, recv_sems):
        x, y, c = _me()
        cps = []
        for a in range(n):
            for q in range(N_CHIP):
                cps.append(pltpu.make_async_remote_copy(
                    src_ref=_blk(ins[a], kinds[a], 2 * q + (1 - c), sizes[a]), dst_ref=outs[a].at[q],
                    send_sem=send_sems.at[N_CHIP * a + q], recv_sem=recv_sems.at[N_CHIP * a + q],
                    device_id=(x, y, 1 - c), device_id_type=MESH))
        return cps

    return _Exchange(grads, [SDS((N_CHIP,) + blk_shape(a), F32) for a in range(n)], N_CHIP * n, build)


def _chip_partial(g, r1, kind, size, cidx, name):
    if kind == "rows":
        rows, cols = size, g.shape[1]
        g3 = g.reshape(N_DEV, rows, cols)
        rb = _pick(rows, 512, 16)
        g_spec = pl.BlockSpec((1, rb, cols), lambda q, j, c: (2 * q + c[0], j, 0))
        grid = (N_CHIP, rows // rb)
        blk = (1, rb, cols)
        imap = lambda q, j, c: (q, j, 0)
    else:
        rows, cols = g.shape[0], size
        g3 = g
        g_spec = pl.BlockSpec((rows, cols), lambda q, j, c: (0, 2 * q + c[0]))
        grid = (N_CHIP, 1)
        blk = (1, rows, cols)
        imap = lambda q, j, c: (q, 0, 0)

    def body(c_ref, g_ref, r_ref, p_ref, pb_ref):
        s = g_ref[...].reshape(blk) + r_ref[...]
        p_ref[...] = s
        pb_ref[...] = s.astype(BF16)

    return pl.pallas_call(
        body, name=name,
        grid_spec=pltpu.PrefetchScalarGridSpec(
            num_scalar_prefetch=1, grid=grid,
            in_specs=[g_spec, pl.BlockSpec(blk, imap)],
            out_specs=[pl.BlockSpec(blk, imap), pl.BlockSpec(blk, imap)]),
        out_shape=[SDS((N_CHIP, rows, cols), F32), SDS((N_CHIP, rows, cols), BF16)],
        compiler_params=_cparams(("arbitrary", "arbitrary")),
    )(cidx, g3, r1)


def _chip_exchange(parts):
    n = len(parts)
    m = N_CHIP - 1

    def build(ins, outs, send_sems, recv_sems):
        x, y, c = _me()
        cps = []
        for a in range(n):
            for j in (1, 2, 3):
                px, py = _chip_peer(x, y, j)
                cps.append(pltpu.make_async_remote_copy(
                    src_ref=ins[a].at[2 * px + py], dst_ref=outs[a].at[j - 1], send_sem=send_sems.at[m * a + j - 1],
                    recv_sem=recv_sems.at[m * a + j - 1], device_id=(px, py, c), device_id_type=MESH))
        return cps

    return _Exchange(parts, [SDS((m,) + p.shape[1:], BF16) for p in parts], m * n, build)


def _grad_total(part, r2, qidx, name):
    _, rows, cols = part.shape
    rb = _pick(rows, 512, 16)

    def body(q_ref, p_ref, r_ref, o_ref):
        s = p_ref[0]
        for j in range(N_CHIP - 1):
            s = s + r_ref[j].astype(F32)
        o_ref[...] = s

    return pl.pallas_call(
        body, name=name,
        grid_spec=pltpu.PrefetchScalarGridSpec(
            num_scalar_prefetch=1, grid=(rows // rb,),
            in_specs=[pl.BlockSpec((1, rb, cols), lambda i, q: (q[0], i, 0)),
                      pl.BlockSpec((N_CHIP - 1, rb, cols), lambda i, q: (0, i, 0))],
            out_specs=pl.BlockSpec((rb, cols), lambda i, q: (i, 0))),
        out_shape=SDS((rows, cols), F32),
        compiler_params=_cparams(("arbitrary",)),
    )(qidx, part, r2)


def _all_reduce_small(pack, name):
    rows = pack.shape[0]
    rs = rows // N_DEV
    assert rs * N_DEV == rows and rs % 8 == 0

    def body(x_ref, o_ref, rbuf, red, send1, recv1, send2, recv2):
        x, y, c = _me()
        me = 4 * x + 2 * y + c

        def peer(d):
            px = (1 - x) if (d >> 2) & 1 else x
            py = (1 - y) if (d >> 1) & 1 else y
            pc = (1 - c) if d & 1 else c
            return px, py, pc

        def sl(ref, k):
            return ref.at[pl.ds(pl.multiple_of(k * rs, 8), rs)]

        phase1 = []
        for d in range(1, N_DEV):
            px, py, pc = peer(d)
            phase1.append(pltpu.make_async_remote_copy(
                src_ref=sl(x_ref, 4 * px + 2 * py + pc), dst_ref=rbuf.at[d], send_sem=send1.at[d], recv_sem=recv1.at[d],
                device_id=(px, py, pc), device_id_type=MESH))
        for cp in phase1:
            cp.start()
        acc = sl(x_ref, me)[...]
        for cp in phase1:
            cp.wait()
        for d in range(1, N_DEV):
            acc = acc + rbuf[d]
        red[...] = acc
        sl(o_ref, me)[...] = acc
        phase2 = []
        for d in range(1, N_DEV):
            px, py, pc = peer(d)
            phase2.append(pltpu.make_async_remote_copy(
                src_ref=red, dst_ref=sl(o_ref, me), send_sem=send2.at[d], recv_sem=recv2.at[d],
                device_id=(px, py, pc), device_id_type=MESH))
        for cp in phase2:
            cp.start()
        for cp in phase2:
            cp.wait()

    vm = pl.BlockSpec(memory_space=pltpu.VMEM)
    return pl.pallas_call(
        body, name=name, in_specs=[vm], out_specs=vm, out_shape=SDS(pack.shape, F32),
        scratch_shapes=[pltpu.VMEM((N_DEV, rs, LANES), F32), pltpu.VMEM((rs, LANES), F32),
                        pltpu.SemaphoreType.DMA((N_DEV,)), pltpu.SemaphoreType.DMA((N_DEV,)),
                        pltpu.SemaphoreType.DMA((N_DEV,)), pltpu.SemaphoreType.DMA((N_DEV,))],
        compiler_params=_cparams(None, has_side_effects=True),
    )(pack)


def _adamw(w, g, m, v, name):
    rows, cols = w.shape
    rb = rows if rows * cols * 4 <= ADAMW_WHOLE_BYTES else _pick(rows, 256, 8)
    c1 = 1.0 / (1.0 - ADAM_B1 ** ADAM_STEP)
    c2 = 1.0 / (1.0 - ADAM_B2 ** ADAM_STEP)

    def body(w_ref, g_ref, m_ref, v_ref, d_ref, mo_ref, vo_ref):
        gv = g_ref[...]
        mn = ADAM_B1 * m_ref[...] + (1.0 - ADAM_B1) * gv
        vn = ADAM_B2 * v_ref[...] + (1.0 - ADAM_B2) * (gv * gv)
        mo_ref[...] = mn
        vo_ref[...] = vn
        d_ref[...] = -ADAM_LR * ((mn * c1) / (jnp.sqrt(vn * c2) + ADAM_EPS) + ADAM_WD * w_ref[...])

    spec = pl.BlockSpec((rb, cols), lambda i: (i, 0))
    return pl.pallas_call(
        body, name=name, grid=(rows // rb,),
        in_specs=[spec] * 4, out_specs=[spec] * 3, out_shape=[SDS((rows, cols), F32)] * 3,
        compiler_params=_cparams(("arbitrary",)),
    )(w, g, m, v)


def _pad_rows(a, mult=8):
    r = (-a.shape[0]) % mult
    return a if r == 0 else jnp.pad(a, ((0, r), (0, 0)))


def _as_lanes(a):
    flat = a.reshape(-1)
    pad = (-flat.shape[0]) % (8 * LANES)
    if pad:
        flat = jnp.pad(flat, (0, pad))
    return flat.reshape(-1, LANES)


def kernel(x, norm_g, w_in, ln_g, ln_b, w_s, b_s, conv_w, conv_b, w_pool, pool_scale, w_pa, w_pb, w_pc, w_o, final_g, loss_target, m_norm_g, m_w_in, m_ln_g, m_ln_b, m_w_s, m_b_s, m_conv_w, m_conv_b, m_w_pool, m_pool_scale, m_w_pa, m_w_pb, m_w_pc, m_w_o, m_final_g, v_norm_g, v_w_in, v_ln_g, v_ln_b, v_w_s, v_b_s, v_conv_w, v_conv_b, v_w_pool, v_pool_scale, v_w_pa, v_w_pb, v_w_pc, v_w_o, v_final_g):
    L = w_in.shape[0]
    D = x.shape[-1]
    n_loc = w_in.shape[2]
    pc_loc = w_pa.shape[2]
    x0 = x[0]
    target = loss_target[0]
    xi, yi, ci = _me()
    cidx = jnp.reshape(ci, (1,)).astype(jnp.int32)
    qidx = jnp.reshape(2 * xi + yi, (1,)).astype(jnp.int32)

    kinds5 = ["rows", "cols", "cols", "cols", "rows"]

    def layer_shards(l):
        return [w_in[l].T.astype(BF16), w_pa[l].astype(BF16), w_pb[l].astype(BF16), w_pc[l].astype(BF16),
                w_o[l].astype(BF16)]

    def gathered(direct, shards, kinds, l):
        sizes, _ = _gather_sizes(shards, kinds)
        return _run_exchange(_gather_forward(direct, kinds, sizes), f"weights_forward_{l}")

    cw_loc = _pad_rows(conv_w.reshape(L * CONV_TAPS, -1))
    cw_loc = jnp.pad(cw_loc, ((0, 0), (0, LANES - cw_loc.shape[1])))
    causal = jnp.tril(jnp.ones((CHUNK, CHUNK), dtype=bool))

    chip_order = jnp.stack([2 * xi + yi] + [2 * px + py for px, py in (_chip_peer(xi, yi, j) for j in (1, 2, 3))])
    sh0 = layer_shards(0)
    rest0, krest0 = sh0[1:] + [cw_loc], kinds5[1:] + ["rows"]
    (p0, h0, win_t0), delivered = _inproj_gathering(x0, norm_g[0], sh0[0], chip_order.astype(jnp.int32), "inproj_fwd_0",
                                                    [_gather_everywhere(rest0, krest0)])
    rest0_full = delivered[0]
    cw_all = rest0_full[-1].reshape(N_DEV, -1, LANES)[:, :L * CONV_TAPS, :conv_w.shape[2]]
    conv_w_full = jnp.transpose(cw_all, (1, 0, 2)).reshape(L, CONV_TAPS, -1)

    def make_layer(l, full5):
        win_t, wpa, wpb, wpc, wo = full5
        wm = jnp.where(causal, w_s[l], 0.0)
        cvec = jnp.concatenate([ln_g[l][None], ln_b[l][None], conv_w_full[l], conv_b[l][None], pool_scale[l][None],
                                jnp.zeros((C_ROWS - 7, SEG), F32)], axis=0)
        return dict(
            win_t=win_t, wpa=wpa, wpb=wpb, wpc=wpc, wo=wo, cvec=cvec,
            bsb=jnp.repeat(b_s[l].T, HEAD, axis=1),
            wcat=jnp.transpose(wm, (1, 0, 2)).reshape(CHUNK, GROUPS * CHUNK).astype(BF16),
            wcatt=jnp.transpose(wm, (2, 0, 1)).reshape(CHUNK, GROUPS * CHUNK).astype(BF16),
            wpool=w_pool[l].astype(BF16))

    layers, xs, saved, win_next = [], [x0], [], None
    for l in range(L):
        if l == 0:
            p, h = p0, h0
            lw = make_layer(0, [win_t0] + list(rest0_full[:4]))
        else:
            (p, h), delivered = _inproj(xs[-1], norm_g[l], win_next, f"inproj_fwd_{l}",
                                        [_gather_everywhere(layer_shards(l)[1:], kinds5[1:])])
            lw = make_layer(l, [win_next] + delivered[0])
        layers.append(lw)
        nxt = layer_shards(l + 1)[:1] if l + 1 < L else None
        (xn, ya, yb, yc), delivered = _mix_fwd(p, xs[-1], lw, f"mix_fwd_{l}", [_gather_direct(nxt, kinds5[:1])] if nxt else [])
        if nxt:
            win_next = gathered(delivered[0], nxt, kinds5[:1], l + 1)[0]
        saved.append((p, h, ya, yb, yc))
        xs.append(xn)
    dx, loss_acc, dfg_acc = _loss_head(xs[-1], final_g, target, "loss_head")

    rs_sizes = [n_loc, pc_loc, pc_loc, pc_loc, w_o.shape[1]]
    await_sibling, await_chips = [], []
    partial_of, from_chips = {}, {}
    serial = [0]

    def riders_now():
        riders, plan = [], []
        for grp in await_chips:
            riders.append(_chip_exchange([partial_of[t][1] for t, _, _, _ in grp]))
            plan.append(("chips", grp))
        for grp in await_sibling:
            riders.append(_sibling_exchange([g for _, g, _, _ in grp], [k for _, _, k, _ in grp], [s for _, _, _, s in grp]))
            plan.append(("sibling", grp))
        del await_chips[:], await_sibling[:]
        return riders, plan

    def absorb(plan, delivered):
        for (what, grp), res in zip(plan, delivered):
            for (t, g, k, s), r in zip(grp, res):
                if what == "chips":
                    from_chips[t] = r
                else:
                    partial_of[t] = _chip_partial(g, r, k, s, cidx, f"grad_chip_partial_{t[0]}_{t[1]}")
            if what == "sibling":
                await_chips.append(grp)

    small = [None] * L
    for l in reversed(range(L)):
        lw = layers[l]
        p, h, ya, yb, yc = saved[l]
        dp, acts, merged, dys, gwc, gbs, gwpool, gvec = _mix_bwd(p, dx, ya, yb, yc, lw, f"mix_bwd_{l}")
        riders, plan = riders_now()
        (gwpa, gwpb, gwpc, gwo), delivered = _proj_wgrad(acts, merged, dys, dx, f"proj_wgrad_{l}", riders)
        absorb(plan, delivered)
        await_sibling.append([((l, a), g, kinds5[a], rs_sizes[a]) for a, g in ((1, gwpa), (2, gwpb), (3, gwpc), (4, gwo))])

        def bwd_x(dxo, pieces):
            nt = _inproj_token_blocks(dxo.shape[0])
            pieces = min(pieces, nt)
            done, dng, b0 = None, None, 0
            for k in range(pieces):
                cnt = (nt - b0) // (pieces - k)
                riders, plan = riders_now()
                (done, dng_k), delivered = _inproj_bwd_x(dp, lw["win_t"], xs[l], norm_g[l], dxo, f"inproj_bwd_x_{l}_{k}",
                                                         riders, blocks=(b0, cnt), fill=done)
                absorb(plan, delivered)
                dng = dng_k if dng is None else dng + dng_k
                b0 += cnt
            return done, dng

        def bwd_w():
            riders, plan = riders_now()
            (gwin_t,), delivered = _inproj_bwd_w(dp, h, f"inproj_bwd_w_{l}", riders)
            absorb(plan, delivered)
            await_sibling.append([((l, 0), gwin_t, kinds5[0], rs_sizes[0])])

        if l == L - 1:
            dx, dng = bwd_x(dx, 1)
            bwd_w()
        else:
            bwd_w()
            dx, dng = bwd_x(dx, 2 if l == 0 else 1)
        small[l] = dict(norm_g=dng[0], ln_g=gvec[V_LNG], ln_b=gvec[V_LNB], w_s=gwc, b_s=gbs, conv_w=gvec[V_CW0:V_CW0 + 3],
                        conv_b=gvec[V_CB], w_pool=gwpool, pool_scale=gvec[V_PS])
    while await_sibling or await_chips:
        riders, plan = riders_now()
        delivered = []
        for ex in riders:
            delivered.append(_run_exchange(ex, f"grad_exchange_tail_{serial[0]}"))
            serial[0] += 1
        absorb(plan, delivered)
    grad_x = dx[None]
    big_grads = []
    for l in range(L):
        tot = [_grad_total(partial_of[(l, a)][0], from_chips[(l, a)], qidx, f"grad_total_{l}_{a}") for a in range(5)]
        big_grads.append([tot[0],
                          tot[1].reshape(SEG, pc_loc), tot[2].reshape(SEG, pc_loc), tot[3].reshape(SEG, pc_loc),
                          tot[4]])

    names = ["norm_g", "ln_g", "ln_b", "w_s", "b_s", "conv_w", "conv_b", "w_pool", "pool_scale"]
    pieces = [_as_lanes(jnp.stack([small[l][nm] for l in range(L)])) for nm in names]
    pieces += [_as_lanes(dfg_acc[0]), loss_acc]
    sizes = [pc.shape[0] for pc in pieces]
    pack = jnp.concatenate(pieces, axis=0)
    pack = _pad_rows(pack, 8 * N_DEV)
    red = _all_reduce_small(pack, "small_grads_all_reduce")
    offs = [0]
    for s in sizes:
        offs.append(offs[-1] + s)

    def unpack(i, shape):
        n = math.prod(shape)
        return red[offs[i]:offs[i + 1]].reshape(-1)[:n].reshape(shape)

    g_norm_g = unpack(0, (L, D))
    g_ln_g = unpack(1, (L, SEG))
    g_ln_b = unpack(2, (L, SEG))
    g_w_s = unpack(3, (L, GROUPS, CHUNK, CHUNK))
    g_b_s = jnp.transpose(unpack(4, (L, CHUNK, LANES))[:, :, :GROUPS], (0, 2, 1))
    g_conv_w_full = unpack(5, (L, CONV_TAPS, SEG))
    g_conv_b = unpack(6, (L, SEG))
    g_w_pool = unpack(7, (L, len(POOL_WINDOWS), POOL_GROUP, POOL_GROUP))
    g_pool_scale = unpack(8, (L, SEG))
    g_final_g = unpack(9, (D,))
    loss = red[offs[10], 0]
    dev = 4 * xi + 2 * yi + ci
    g_conv_w = lax.dynamic_slice_in_dim(g_conv_w_full, dev * conv_w.shape[2], conv_w.shape[2], axis=2)

    g_w_in_t = jnp.stack([big_grads[l][0] for l in range(L)])
    g_w_in = jnp.swapaxes(g_w_in_t, 1, 2)
    g_w_pa = jnp.stack([big_grads[l][1] for l in range(L)])
    g_w_pb = jnp.stack([big_grads[l][2] for l in range(L)])
    g_w_pc = jnp.stack([big_grads[l][3] for l in range(L)])
    g_w_o = jnp.stack([big_grads[l][4] for l in range(L)])

    grads = dict(norm_g=g_norm_g, w_in=g_w_in, ln_g=g_ln_g, ln_b=g_ln_b, w_s=g_w_s, b_s=g_b_s, conv_w=g_conv_w,
                 conv_b=g_conv_b, w_pool=g_w_pool, pool_scale=g_pool_scale, w_pa=g_w_pa, w_pb=g_w_pb, w_pc=g_w_pc,
                 w_o=g_w_o, final_g=g_final_g)
    weights = dict(norm_g=norm_g, w_in=w_in, ln_g=ln_g, ln_b=ln_b, w_s=w_s, b_s=b_s, conv_w=conv_w, conv_b=conv_b,
                   w_pool=w_pool, pool_scale=pool_scale, w_pa=w_pa, w_pb=w_pb, w_pc=w_pc, w_o=w_o, final_g=final_g)
    ms = dict(norm_g=m_norm_g, w_in=m_w_in, ln_g=m_ln_g, ln_b=m_ln_b, w_s=m_w_s, b_s=m_b_s, conv_w=m_conv_w,
              conv_b=m_conv_b, w_pool=m_w_pool, pool_scale=m_pool_scale, w_pa=m_w_pa, w_pb=m_w_pb, w_pc=m_w_pc,
              w_o=m_w_o, final_g=m_final_g)
    vs = dict(norm_g=v_norm_g, w_in=v_w_in, ln_g=v_ln_g, ln_b=v_ln_b, w_s=v_w_s, b_s=v_b_s, conv_w=v_conv_w,
              conv_b=v_conv_b, w_pool=v_w_pool, pool_scale=v_pool_scale, w_pa=v_w_pa, w_pb=v_w_pb, w_pc=v_w_pc,
              w_o=v_w_o, final_g=v_final_g)
    order = ["norm_g", "w_in", "ln_g", "ln_b", "w_s", "b_s", "conv_w", "conv_b", "w_pool", "pool_scale", "w_pa", "w_pb",
             "w_pc", "w_o", "final_g"]

    delta, new_m, new_v = {}, {}, {}
    big = ["w_in", "w_pa", "w_pb", "w_pc", "w_o"]
    tr = lambda a: jnp.swapaxes(a, 1, 2)
    for nm in big:
        into = tr if nm == "w_in" else (lambda a: a)
        shp = into(weights[nm]).shape
        two = lambda a: a.reshape(-1, shp[-1])
        g2 = two(g_w_in_t) if nm == "w_in" else two(grads[nm])
        d, mn, vn = _adamw(two(into(weights[nm])), g2, two(into(ms[nm])), two(into(vs[nm])), f"adamw_{nm}")
        delta[nm], new_m[nm], new_v[nm] = (into(a.reshape(shp)) for a in (d, mn, vn))
    rest = [nm for nm in order if nm not in big]
    cat = lambda src: jnp.concatenate([_as_lanes(src[nm]) for nm in rest], axis=0)
    d, mn, vn = _adamw(cat(weights), cat(grads), cat(ms), cat(vs), "adamw_small")
    off = 0
    for nm in rest:
        shp = weights[nm].shape
        n = math.prod(shp)
        rows = _as_lanes(weights[nm]).shape[0]
        cut = lambda a: a[off:off + rows].reshape(-1)[:n].reshape(shp)
        delta[nm], new_m[nm], new_v[nm] = cut(d), cut(mn), cut(vn)
        off += rows

    return (loss, grad_x, *[grads[nm] for nm in order], *[delta[nm] for nm in order],
            *[new_m[nm] for nm in order], *[new_v[nm] for nm in order])
```

```python
import functools
import math

import numpy as np
import jax
import jax.numpy as jnp
from jax import lax
from jax.experimental import pallas as pl
from jax.experimental.pallas import tpu as pltpu

F32 = jnp.float32
BF16 = jnp.bfloat16
SDS = jax.ShapeDtypeStruct
MESH = pl.DeviceIdType.MESH

SEG = 512
CHUNK = 128
GROUPS = 8
HEAD = SEG // GROUPS
POOL_WINDOWS = (2, 4, 8, 16)
POOL_GROUP = SEG // len(POOL_WINDOWS)
CONV_TAPS = 3
HALO = 16
RMS_EPS = 1e-6
LN_EPS = 1e-5
ADAM_LR, ADAM_B1, ADAM_B2, ADAM_EPS, ADAM_WD, ADAM_STEP = 0.001, 0.9, 0.999, 1e-08, 0.01, 10

O_U, O_V, O_ZA, O_XB, O_BG, O_CG, O_ZB, O_XC, O_ZC, O_G = (SEG * i for i in range(10))

N_DEV = 8
N_CHIP = 4
LANES = 128
VMEM_LIMIT = 48 * 1024 * 1024
ADAMW_WHOLE_BYTES = 2 * 1024 * 1024
INPROJ_ROWS = 1024
BWD_X_ROWS = 512
WGRAD_ROWS = 2048
RIDERS_FROM_CHUNK = 2
LAST_PIECE_SHARE = (3, 8)


def _cparams(sem=None, **kw):
    return pltpu.CompilerParams(dimension_semantics=sem, vmem_limit_bytes=VMEM_LIMIT, **kw)


def _pick(total, target, mult):
    best = None
    for d in range(mult, min(total, target) + 1, mult):
        if total % d == 0:
            best = d
    assert best is not None, (total, target, mult)
    return best


def _dot(a, b):
    return jnp.dot(a, b, preferred_element_type=F32)


def _dot_nt(a, b):
    return lax.dot_general(a, b, (((1,), (1,)), ((), ())), preferred_element_type=F32)


def _dot_tn(a, b):
    return lax.dot_general(a, b, (((0,), (0,)), ((), ())), preferred_element_type=F32)


def _zero(ref):
    ref[...] = jnp.zeros(ref.shape, ref.dtype)


def _sigmoid(x):
    return 1.0 / (1.0 + jnp.exp(-x))


_GELU_C = math.sqrt(2.0 / math.pi)


def _gelu(x):
    t = jnp.tanh(_GELU_C * (x + 0.044715 * x * x * x))
    return 0.5 * x * (1.0 + t), t


def _gelu_grad(x, t):
    return 0.5 * (1.0 + t) + 0.5 * x * (1.0 - t * t) * _GELU_C * (1.0 + 3.0 * 0.044715 * x * x)


def _me():
    return lax.axis_index("x"), lax.axis_index("y"), lax.axis_index("c")


def _inproj(x, norm_g, win_t, name, riders=()):
    T, D = x.shape
    N = win_t.shape[0]
    bt = _pick(T, INPROJ_ROWS, 16)
    bn = _pick(N, 1536, LANES)
    grid = (T // bt, N // bn)

    def compute(x_ref, g_ref, w_ref, p_ref, h_ref, hs_ref):
        @pl.when(pl.program_id(1) == 0)
        def _():
            xv = x_ref[...]
            rstd = lax.rsqrt(jnp.mean(xv * xv, axis=-1, keepdims=True) + RMS_EPS)
            hb = (xv * rstd * g_ref[...]).astype(BF16)
            hs_ref[...] = hb
            h_ref[...] = hb

        p_ref[...] = _dot_nt(hs_ref[...], w_ref[...]).astype(BF16)

    return _host_call(
        compute, name, grid, riders,
        inputs=[x, norm_g.reshape(1, D), win_t],
        in_specs=[pl.BlockSpec((bt, D), lambda i, j: (i, 0)),
                  pl.BlockSpec((1, D), lambda i, j: (0, 0)),
                  pl.BlockSpec((bn, D), lambda i, j: (j, 0))],
        out_specs=[pl.BlockSpec((bt, bn), lambda i, j: (i, j)),
                   pl.BlockSpec((bt, D), lambda i, j: (i, 0))],
        out_shape=[SDS((T, N), BF16), SDS((T, D), BF16)],
        scratch_shapes=[pltpu.VMEM((bt, D), BF16)])


def _inproj_gathering(x, norm_g, w_loc, chip_order, name, riders=()):
    T, D = x.shape
    n = w_loc.shape[0]
    N = n * N_DEV
    cw = 2 * n
    bt = _pick(T, INPROJ_ROWS, 16)
    nt = T // bt
    r_in, r_out, r_sems, copies = _rider_plan(riders)
    n_rin, n_rout = len(r_in), len(r_out)

    def body(q_ref, x_ref, g_ref, wloc_ref, *rest):
        rins = rest[:n_rin]
        p_ref, h_ref, wfull_ref = rest[n_rin:n_rin + 3]
        routs = rest[n_rin + 3:n_rin + 3 + n_rout]
        hs_ref, wbuf, send_sems, recv_sems, loc_sems = rest[n_rin + 3 + n_rout:n_rin + 8 + n_rout]
        rsems = rest[n_rin + 8 + n_rout:]
        j, i = pl.program_id(0), pl.program_id(1)
        cx, cy, cc = _me()
        sibling = (cx, cy, 1 - cc)

        def rows(k):
            return wfull_ref.at[pl.ds(pl.multiple_of(k * n, 8), n)]

        def shard_copy(slot, src, k, to):
            return pltpu.make_async_remote_copy(src_ref=src, dst_ref=rows(k), send_sem=send_sems.at[slot],
                                                recv_sem=recv_sems.at[slot], device_id=to, device_id_type=MESH)

        me = 4 * cx + 2 * cy + cc
        place_mine = pltpu.make_async_copy(wloc_ref, rows(me), loc_sems.at[0])
        sends = [shard_copy(0, wloc_ref, me, sibling)]
        for jj in (1, 2, 3):
            sends.append(shard_copy(jj, wloc_ref, me, (*_chip_peer(cx, cy, jj), cc)))

        def forward(jj):
            px, py = _chip_peer(cx, cy, jj)
            k = 4 * px + 2 * py + cc
            return shard_copy(3 + jj, rows(k), k, sibling)

        def load_chunk(q):
            cp = pltpu.make_async_copy(wfull_ref.at[pl.ds(pl.multiple_of(q * cw, 8), cw)], wbuf, loc_sems.at[1])
            cp.start()
            cp.wait()

        keep_h = pltpu.make_async_copy(hs_ref, h_ref, loc_sems.at[2])

        @pl.when((j == 0) & (i == 0))
        def _():
            place_mine.start()
            for cp in sends:
                cp.start()
            place_mine.wait()
            sends[0].wait_recv()
            load_chunk(q_ref[0])

        for jj in (1, 2, 3):
            @pl.when((j == jj) & (i == 0))
            def _(jj=jj):
                sends[jj].wait_recv()
                fwd = forward(jj)
                fwd.start()
                fwd.wait_recv()
                load_chunk(q_ref[jj])
                if jj == RIDERS_FROM_CHUNK:
                    for cp in copies(rins, routs, rsems):
                        cp.start()

        tok = pl.ds(pl.multiple_of(i * bt, bt), bt)

        @pl.when(j == 0)
        def _():
            xv = x_ref[...]
            rstd = lax.rsqrt(jnp.mean(xv * xv, axis=-1, keepdims=True) + RMS_EPS)
            hs_ref[tok, :] = (xv * rstd * g_ref[...]).astype(BF16)

        @pl.when((j == 0) & (i == nt - 1))
        def _():
            keep_h.start()

        p_ref[...] = _dot_nt(hs_ref[tok, :], wbuf[...]).astype(BF16)

        @pl.when((j == N_CHIP - 1) & (i == nt - 1))
        def _():
            keep_h.wait()
            for cp in sends:
                cp.wait_send()
            for jj in (1, 2, 3):
                forward(jj).wait_send()
            for cp in copies(rins, routs, rsems):
                cp.wait()

    res = pl.pallas_call(
        body, name=name,
        grid_spec=pltpu.PrefetchScalarGridSpec(
            num_scalar_prefetch=1, grid=(N_CHIP, nt),
            in_specs=[pl.BlockSpec((bt, D), lambda j, i, q: (jnp.where(j == 0, i, nt - 1), 0)),
                      pl.BlockSpec((1, D), lambda j, i, q: (0, 0)), _ANY] + [_ANY] * n_rin,
            out_specs=[pl.BlockSpec((bt, cw), lambda j, i, q: (i, q[j])), _ANY, _ANY] + [_ANY] * n_rout,
            scratch_shapes=[pltpu.VMEM((T, D), BF16), pltpu.VMEM((cw, D), BF16), pltpu.SemaphoreType.DMA((7,)),
                            pltpu.SemaphoreType.DMA((7,)), pltpu.SemaphoreType.DMA((3,))] + r_sems),
        out_shape=[SDS((T, N), BF16), SDS((T, D), BF16), SDS((N, D), BF16)] + r_out,
        input_output_aliases=_rider_aliases(riders, 4, 3),
        compiler_params=_cparams(("arbitrary", "arbitrary")),
    )(chip_order, x, norm_g.reshape(1, D), w_loc, *r_in)
    return res[:3], _split_riders(riders, res[3:])


C_LNG, C_LNB, C_CW0, C_CW1, C_CW2, C_CB, C_PS = range(7)
C_ROWS = 8


def _pool_bands(R, anticausal):
    t = np.arange(R)[:, None]
    s = np.arange(R + CHUNK)[None, :]
    bands = [((s >= t) & (s < t + w)) if anticausal else ((s > t + CHUNK - w) & (s <= t + CHUNK)) for w in POOL_WINDOWS]
    return jnp.asarray(np.stack(bands), dtype=BF16)


def _mixers(p_ref, hxb_ref, hcg_ref, hxc_ref, cv, bsb_ref, wcat_ref, wpool_ref, band_ref, extb,
            first, blk, R, need_grad):
    def seg(lo):
        return p_ref[:, lo:lo + SEG].astype(F32)

    u, v, za = seg(O_U), seg(O_V), seg(O_ZA)
    xb, bg, cg, zb = seg(O_XB), seg(O_BG), seg(O_CG), seg(O_ZB)
    xc, zc = seg(O_XC), seg(O_ZC)
    out = {}

    ug, tu = _gelu(u)
    vg, tv = _gelu(v)
    mu = jnp.mean(vg, axis=-1, keepdims=True)
    vcen = vg - mu
    rs = lax.rsqrt(jnp.mean(vcen * vcen, axis=-1, keepdims=True) + LN_EPS)
    vhat = vcen * rs
    vn = (vhat * cv[C_LNG:C_LNG + 1, :] + cv[C_LNB:C_LNB + 1, :]).astype(BF16)
    lane_group = lax.broadcasted_iota(jnp.int32, (CHUNK, SEG), 1) // HEAD
    zero_b = jnp.zeros((CHUNK, SEG), BF16)
    sgs = []
    for ci in range(R // CHUNK):
        vc = vn[ci * CHUNK:(ci + 1) * CHUNK]
        vst = jnp.concatenate([jnp.where(lane_group == g, vc, zero_b) for g in range(GROUPS)], axis=0)
        sgs.append(_dot(wcat_ref[...], vst) + bsb_ref[...])
    sg = sgs[0] if len(sgs) == 1 else jnp.concatenate(sgs, axis=0)
    a_out = ug * sg
    sa = _sigmoid(za)
    out["a"] = a_out * (za * sa)

    cx = cg * xb
    halo_b = hcg_ref[...].astype(F32) * hxb_ref[...].astype(F32)
    extb[0:HALO, :] = jnp.where(first, 0.0, halo_b)
    extb[HALO:HALO + R, :] = cx
    cx1 = extb[pl.ds(HALO - 1, R), :]
    cx2 = extb[pl.ds(HALO - 2, R), :]
    yconv = (cv[C_CW0:C_CW0 + 1, :] * cx2 + cv[C_CW1:C_CW1 + 1, :] * cx1
             + cv[C_CW2:C_CW2 + 1, :] * cx + cv[C_CB:C_CB + 1, :])
    b_out = bg * yconv
    sb = _sigmoid(zb)
    out["b"] = b_out * (zb * sb)

    halo_c = hxc_ref[...]
    xc_ext = jnp.concatenate([jnp.zeros((CHUNK - HALO, SEG), BF16), jnp.where(first, jnp.zeros_like(halo_c), halo_c),
                              p_ref[:, O_XC:O_XC + SEG]], axis=0)
    tpos = blk * R + lax.broadcasted_iota(jnp.int32, (R, POOL_GROUP), 0) + 1
    pooled, invs, pws = [], [], []
    for gi, w in enumerate(POOL_WINDOWS):
        lo = gi * POOL_GROUP
        win = _dot(band_ref[gi], xc_ext[:, lo:lo + POOL_GROUP])
        inv = 1.0 / jnp.minimum(tpos, w).astype(F32)
        pg = (win * inv - xc[:, lo:lo + POOL_GROUP]).astype(BF16)
        pooled.append(pg)
        invs.append(inv)
        pws.append(_dot(pg, wpool_ref[gi]))
    pw = jnp.concatenate(pws, axis=1)
    c_out = pw * cv[C_PS:C_PS + 1, :]
    sc = _sigmoid(zc)
    out["c"] = c_out * (zc * sc)

    if need_grad:
        out.update(u=u, v=v, tu=tu, tv=tv, ug=ug, sg=sg, a_out=a_out, za=za, sa=sa,
                   rs=rs, vhat=vhat, vn=vn, lane_group=lane_group, zero_b=zero_b,
                   xb=xb, bg=bg, cg=cg, cx=cx, cx1=cx1, cx2=cx2, yconv=yconv, b_out=b_out, zb=zb, sb=sb,
                   pooled=pooled, invs=invs, pw=pw, c_out=c_out, zc=zc, sc=sc)
    return out


def _halo_specs(R, nb, rev):
    step = R // HALO

    def mk(col):
        def imap(i):
            b = (nb - 1 - i) if rev else i
            return (jnp.maximum(b * step - 1, 0), col)
        return pl.BlockSpec((HALO, SEG), imap)

    return [mk(O_XB // SEG), mk(O_CG // SEG), mk(O_XC // SEG)]


def _const_spec(shape):
    nd = len(shape)
    return pl.BlockSpec(shape, lambda i: (0,) * nd, pipeline_mode=pl.Buffered(1))


MIX_FWD_ROWS = 512
MIX_BWD_ROWS = 256


def _mix_block_rows(T, target):
    return _pick(T, target, CHUNK)


def _mix_fwd(p, x, lw, name, riders=()):
    T, D = x.shape
    N = p.shape[1]
    R = _mix_block_rows(T, MIX_FWD_ROWS)
    nb = T // R

    def body(p_ref, hxb, hcg, hxc, x_ref, cv_ref, bsb_ref, wcat_ref, wpool_ref, band_ref, wpa_ref, wpb_ref, wpc_ref,
             wo_ref, xo_ref, ya_ref, yb_ref, yc_ref, extb):
        i = pl.program_id(0)
        cv = cv_ref[...]
        r = _mixers(p_ref, hxb, hcg, hxc, cv, bsb_ref, wcat_ref, wpool_ref, band_ref, extb,
                    i == 0, i, R, False)
        merged = None
        for k, (act, w_ref, y_ref) in enumerate(((r["a"], wpa_ref, ya_ref), (r["b"], wpb_ref, yb_ref),
                                                 (r["c"], wpc_ref, yc_ref))):
            y = _dot(act.astype(BF16), w_ref[...]).astype(BF16)
            y_ref[...] = y
            term = _sigmoid(p_ref[:, O_G + k * D:O_G + (k + 1) * D]) * y
            merged = term if merged is None else merged + term
        xo_ref[...] = x_ref[...] + _dot(merged, wo_ref[...])

    row = lambda w: pl.BlockSpec((R, w), lambda i: (i, 0))
    consts = [lw["cvec"], lw["bsb"], lw["wcat"], lw["wpool"], _pool_bands(R, False), lw["wpa"], lw["wpb"], lw["wpc"],
              lw["wo"]]
    return _host_call(
        body, name, (nb,), riders,
        inputs=[p, p, p, p, x, *consts],
        in_specs=[row(N)] + _halo_specs(R, nb, False) + [row(D)] + [_const_spec(c.shape) for c in consts],
        out_specs=[row(D), row(D), row(D), row(D)],
        out_shape=[SDS((T, D), F32), SDS((T, D), BF16), SDS((T, D), BF16), SDS((T, D), BF16)],
        scratch_shapes=[pltpu.VMEM((HALO + R, SEG), F32)])


def _loss_head(x, final_g, target, name):
    T, D = x.shape
    bt = _pick(T, 512, 8)

    def body(x_ref, g_ref, t_ref, dx_ref, loss_ref, dg_ref):
        @pl.when(pl.program_id(0) == 0)
        def _():
            _zero(loss_ref)
            _zero(dg_ref)

        xv = x_ref[...]
        g = g_ref[...]
        rstd = lax.rsqrt(jnp.mean(xv * xv, axis=-1, keepdims=True) + RMS_EPS)
        xhat = xv * rstd
        err = xhat * g - t_ref[...]
        part = 0.5 * jnp.sum(jnp.sum(err * err, axis=-1, keepdims=True), axis=0, keepdims=True) / D
        loss_ref[...] += jnp.broadcast_to(part, loss_ref.shape)
        dy = err * (1.0 / D)
        dg_ref[0:1, :] += jnp.sum(dy * xhat, axis=0, keepdims=True)
        dxn = dy * g
        dx_ref[...] = rstd * (dxn - xhat * jnp.mean(dxn * xhat, axis=-1, keepdims=True))

    return pl.pallas_call(
        body, name=name, grid=(T // bt,),
        in_specs=[pl.BlockSpec((bt, D), lambda i: (i, 0)), _const_spec((1, D)), pl.BlockSpec((bt, D), lambda i: (i, 0))],
        out_specs=[pl.BlockSpec((bt, D), lambda i: (i, 0)), _const_spec((8, LANES)), _const_spec((8, D))],
        out_shape=[SDS((T, D), F32), SDS((8, LANES), F32), SDS((8, D), F32)],
        compiler_params=_cparams(("arbitrary",)),
    )(x, final_g.reshape(1, D), target)


V_LNG, V_LNB, V_CB, V_PS, V_CW0, V_CW1, V_CW2 = range(7)


def _mix_bwd(p, dxo, ya, yb, yc, lw, name):
    T, D = dxo.shape
    N = p.shape[1]
    R = _mix_block_rows(T, MIX_BWD_ROWS)
    nb = T // R

    def body(p_ref, hxb, hcg, hxc, dxo_ref, ya_ref, yb_ref, yc_ref, cv_ref, bsb_ref, wcat_ref, wcatt_ref,
             wpool_ref, band_ref, bandt_ref, wpa_ref, wpb_ref, wpc_ref, wo_ref,
             dp_ref, acts_ref, mrg_ref, dys_ref, gwc_ref, gbs_ref, gwpool_ref, gvec_ref,
             extb, extdy, cdy, cq, bsacc):
        i = pl.program_id(0)
        blk = nb - 1 - i

        @pl.when(i == 0)
        def _():
            for ref in (gwc_ref, gwpool_ref, gvec_ref, cdy, cq, bsacc):
                _zero(ref)

        cv = cv_ref[...]
        r = _mixers(p_ref, hxb, hcg, hxc, cv, bsb_ref, wcat_ref, wpool_ref, band_ref, extb,
                    blk == 0, blk, R, True)

        dxo_b = dxo_ref[...].astype(BF16)
        dm = _dot_nt(dxo_b, wo_ref[...]).astype(BF16)
        ys = [ya_ref[...], yb_ref[...], yc_ref[...]]
        sig = [_sigmoid(p_ref[:, O_G + k * D:O_G + (k + 1) * D]) for k in range(3)]
        mrg_ref[...] = sig[0] * ys[0] + sig[1] * ys[1] + sig[2] * ys[2]
        dacts = []
        for k, (act, w_ref) in enumerate(((r["a"], wpa_ref), (r["b"], wpb_ref), (r["c"], wpc_ref))):
            dyk = dm * sig[k]
            dp_ref[:, O_G + k * D:O_G + (k + 1) * D] = dyk * ys[k] * (1.0 - sig[k])
            acts_ref[:, k * SEG:(k + 1) * SEG] = act.astype(BF16)
            dys_ref[:, k * D:(k + 1) * D] = dyk
            dacts.append(_dot_nt(dyk, w_ref[...]))
        da, db, dc = dacts

        def silu_bwd(dact, pre, z, s):
            return dact * (z * s), dact * pre * (s * (1.0 + z * (1.0 - s)))

        d_aout, dza = silu_bwd(da, r["a_out"], r["za"], r["sa"])
        dp_ref[:, O_ZA:O_ZA + SEG] = dza.astype(BF16)
        dp_ref[:, O_U:O_U + SEG] = (d_aout * r["sg"] * _gelu_grad(r["u"], r["tu"])).astype(BF16)
        d_sg = d_aout * r["ug"]
        dvns = []
        for ci in range(R // CHUNK):
            dsc = d_sg[ci * CHUNK:(ci + 1) * CHUNK]
            bsacc[...] += dsc
            dsc_b = dsc.astype(BF16)
            dst = jnp.concatenate([jnp.where(r["lane_group"] == g, dsc_b, r["zero_b"]) for g in range(GROUPS)], axis=0)
            dvns.append(_dot(wcatt_ref[...], dst))
            gwc_ref[...] += _dot_nt(dst, r["vn"][ci * CHUNK:(ci + 1) * CHUNK])
        d_vn = dvns[0] if len(dvns) == 1 else jnp.concatenate(dvns, axis=0)
        vhat = r["vhat"]
        gvec_ref[V_LNG:V_LNG + 1, :] += jnp.sum(d_vn * vhat, axis=0, keepdims=True)
        gvec_ref[V_LNB:V_LNB + 1, :] += jnp.sum(d_vn, axis=0, keepdims=True)
        d_vhat = d_vn * cv[C_LNG:C_LNG + 1, :]
        d_vg = r["rs"] * (d_vhat - jnp.mean(d_vhat, axis=-1, keepdims=True)
                          - vhat * jnp.mean(d_vhat * vhat, axis=-1, keepdims=True))
        dp_ref[:, O_V:O_V + SEG] = (d_vg * _gelu_grad(r["v"], r["tv"])).astype(BF16)

        d_bout, dzb = silu_bwd(db, r["b_out"], r["zb"], r["sb"])
        dp_ref[:, O_ZB:O_ZB + SEG] = dzb.astype(BF16)
        dp_ref[:, O_BG:O_BG + SEG] = (d_bout * r["yconv"]).astype(BF16)
        d_y = d_bout * r["bg"]
        gvec_ref[V_CB:V_CB + 1, :] += jnp.sum(d_y, axis=0, keepdims=True)
        gvec_ref[V_CW0:V_CW0 + 1, :] += jnp.sum(d_y * r["cx2"], axis=0, keepdims=True)
        gvec_ref[V_CW1:V_CW1 + 1, :] += jnp.sum(d_y * r["cx1"], axis=0, keepdims=True)
        gvec_ref[V_CW2:V_CW2 + 1, :] += jnp.sum(d_y * r["cx"], axis=0, keepdims=True)
        extdy[0:R, :] = d_y
        extdy[R:R + HALO, :] = cdy[...]
        d_cx = (cv[C_CW2:C_CW2 + 1, :] * d_y + cv[C_CW1:C_CW1 + 1, :] * extdy[pl.ds(1, R), :]
                + cv[C_CW0:C_CW0 + 1, :] * extdy[pl.ds(2, R), :])
        cdy[...] = d_y[0:HALO]
        dp_ref[:, O_CG:O_CG + SEG] = (d_cx * r["xb"]).astype(BF16)
        dp_ref[:, O_XB:O_XB + SEG] = (d_cx * r["cg"]).astype(BF16)

        d_cout, dzc = silu_bwd(dc, r["c_out"], r["zc"], r["sc"])
        dp_ref[:, O_ZC:O_ZC + SEG] = dzc.astype(BF16)
        gvec_ref[V_PS:V_PS + 1, :] += jnp.sum(d_cout * r["pw"], axis=0, keepdims=True)
        d_pw = (d_cout * cv[C_PS:C_PS + 1, :]).astype(BF16)
        dpool, scaled = [], []
        for gi, w in enumerate(POOL_WINDOWS):
            lo = gi * POOL_GROUP
            dpw_g = d_pw[:, lo:lo + POOL_GROUP]
            gwpool_ref[lo:lo + POOL_GROUP, :] += _dot_tn(r["pooled"][gi], dpw_g)
            dpg = _dot_nt(dpw_g, wpool_ref[gi])
            dpool.append(dpg)
            scaled.append((dpg * r["invs"][gi]).astype(BF16))
        q = jnp.concatenate(scaled, axis=1)
        q_ext = jnp.concatenate([q, cq[...], jnp.zeros((CHUNK - HALO, SEG), BF16)], axis=0)
        for gi, w in enumerate(POOL_WINDOWS):
            lo = gi * POOL_GROUP
            acc = _dot(bandt_ref[gi], q_ext[:, lo:lo + POOL_GROUP])
            dp_ref[:, O_XC + lo:O_XC + lo + POOL_GROUP] = (acc - dpool[gi]).astype(BF16)
        cq[...] = q[0:HALO]

        @pl.when(i == nb - 1)
        def _():
            rr = lax.broadcasted_iota(jnp.int32, gwc_ref.shape, 0) % CHUNK
            cc = lax.broadcasted_iota(jnp.int32, gwc_ref.shape, 1)
            gwc_ref[...] = jnp.where(cc <= rr, gwc_ref[...], 0.0)
            acc = bsacc[...]
            hi = acc.astype(BF16)
            lo_ = (acc - hi.astype(F32)).astype(BF16)
            sel = (lax.broadcasted_iota(jnp.int32, (SEG, LANES), 0) // HEAD
                   == lax.broadcasted_iota(jnp.int32, (SEG, LANES), 1)).astype(BF16)
            gbs_ref[...] = _dot(hi, sel) + _dot(lo_, sel)

    row = lambda w: pl.BlockSpec((R, w), lambda i: (nb - 1 - i, 0))
    consts = [lw["cvec"], lw["bsb"], lw["wcat"], lw["wcatt"], lw["wpool"], _pool_bands(R, False), _pool_bands(R, True),
              lw["wpa"], lw["wpb"], lw["wpc"], lw["wo"]]
    acc_shapes = [(GROUPS * CHUNK, CHUNK), (CHUNK, LANES), (SEG, POOL_GROUP), (8, SEG)]
    row_widths = [N, 3 * SEG, D, 3 * D]
    return pl.pallas_call(
        body, name=name, grid=(nb,),
        in_specs=([row(N)] + _halo_specs(R, nb, True) + [row(D), row(D), row(D), row(D)]
                  + [_const_spec(c.shape) for c in consts]),
        out_specs=[row(w) for w in row_widths] + [_const_spec(s) for s in acc_shapes],
        out_shape=[SDS((T, w), BF16) for w in row_widths] + [SDS(s, F32) for s in acc_shapes],
        scratch_shapes=[pltpu.VMEM((HALO + R, SEG), F32)] * 2
        + [pltpu.VMEM((HALO, SEG), F32), pltpu.VMEM((HALO, SEG), BF16), pltpu.VMEM((CHUNK, SEG), F32)],
        compiler_params=_cparams(("arbitrary",)),
    )(p, p, p, p, dxo, ya, yb, yc, *consts)


def _proj_wgrad(acts, merged, dys, dxo, name, riders=()):
    T, D = dxo.shape
    bk = _pick(T, WGRAD_ROWS // 2, 16)

    def body(a_ref, m_ref, dy_ref, dxo_ref, gwpa_ref, gwpb_ref, gwpc_ref, gwo_ref):
        @pl.when(pl.program_id(0) == 0)
        def _():
            for ref in (gwpa_ref, gwpb_ref, gwpc_ref, gwo_ref):
                _zero(ref)

        gwo_ref[...] += _dot_tn(m_ref[...], dxo_ref[...].astype(BF16))
        for k, ref in enumerate((gwpa_ref, gwpb_ref, gwpc_ref)):
            ref[...] += _dot_tn(a_ref[:, k * SEG:(k + 1) * SEG], dy_ref[:, k * D:(k + 1) * D])

    row = lambda w: pl.BlockSpec((bk, w), lambda i: (i, 0))
    shapes = [(SEG, D), (SEG, D), (SEG, D), (D, D)]
    return _host_call(
        body, name, (T // bk,), riders,
        inputs=[acts, merged, dys, dxo],
        in_specs=[row(3 * SEG), row(D), row(3 * D), row(D)],
        out_specs=[_const_spec(s) for s in shapes], out_shape=[SDS(s, F32) for s in shapes],
        scratch_shapes=[])


def _inproj_token_blocks(T):
    return T // _pick(T, BWD_X_ROWS, 16)


def _inproj_bwd_x(dp, win_t, x, norm_g, dxo, name, riders=(), blocks=None, fill=None):
    T, D = x.shape
    N = dp.shape[1]
    bt = _pick(T, BWD_X_ROWS, 16)
    b0, nblk = blocks if blocks else (0, T // bt)

    def compute(dp_ref, w_ref, x_ref, g_ref, dxo_ref, *rest):
        dx_ref, dg_ref = rest[-2:]

        @pl.when(pl.program_id(0) == 0)
        def _():
            _zero(dg_ref)

        dh = _dot(dp_ref[...], w_ref[...])
        xv = x_ref[...]
        rstd = lax.rsqrt(jnp.mean(xv * xv, axis=-1, keepdims=True) + RMS_EPS)
        xhat = xv * rstd
        dg_ref[0:1, :] += jnp.sum(dh * xhat, axis=0, keepdims=True)
        dxn = dh * g_ref[...]
        dx_ref[...] = dxo_ref[...] + rstd * (dxn - xhat * jnp.mean(dxn * xhat, axis=-1, keepdims=True))

    rows = pl.BlockSpec((bt, D), lambda i: (i + b0, 0))
    return _host_call(
        compute, name, (nblk,), riders,
        inputs=[dp, win_t, x, norm_g.reshape(1, D), dxo] + ([] if fill is None else [fill]),
        in_specs=[pl.BlockSpec((bt, N), lambda i: (i + b0, 0)), _const_spec((N, D)), rows, _const_spec((1, D)), rows]
        + ([] if fill is None else [_ANY]),
        out_specs=[rows, _const_spec((8, D))],
        out_shape=[SDS((T, D), F32), SDS((8, D), F32)],
        scratch_shapes=[],
        aliases={} if fill is None else {5: 0})


def _inproj_bwd_w(dp, h, name, riders=()):
    T, N = dp.shape
    D = h.shape[1]
    bn = _pick(N, 1536, LANES)
    bk = _pick(T, WGRAD_ROWS, 16)
    nk = T // bk

    def compute(dp_ref, h_ref, o_ref):
        @pl.when(pl.program_id(1) == 0)
        def _():
            _zero(o_ref)

        o_ref[...] += _dot_tn(dp_ref[...], h_ref[...])

    return _host_call(
        compute, name, (N // bn, nk), riders,
        inputs=[dp, h],
        in_specs=[pl.BlockSpec((bk, bn), lambda j, k: (k, j)), pl.BlockSpec((bk, D), lambda j, k: (k, 0))],
        out_specs=[pl.BlockSpec((bn, D), lambda j, k: (j, 0))],
        out_shape=[SDS((N, D), F32)],
        scratch_shapes=[])


def _chip_peer(x, y, j):
    px = (1 - x) if (j >> 1) else x
    py = (1 - y) if (j & 1) else y
    return px, py


def _blk(ref, kind, k, n):
    if kind == "rows":
        return ref.at[pl.ds(pl.multiple_of(k * n, 8), n)]
    return ref.at[:, pl.ds(pl.multiple_of(k * n, LANES), n)]


class _Exchange:
    def __init__(self, srcs, out_shapes, n_sems, build, alias=None):
        self.srcs, self.out_shapes, self.n_sems, self.build = list(srcs), list(out_shapes), n_sems, build
        self.alias = dict(alias or {})


def _rider_aliases(riders, first_in, first_out):
    out, i, o = {}, first_in, first_out
    for e in riders:
        out.update({i + s: o + d for s, d in e.alias.items()})
        i, o = i + len(e.srcs), o + len(e.out_shapes)
    return out


def _rider_plan(riders):
    inputs = [s for e in riders for s in e.srcs]
    out_shapes = [o for e in riders for o in e.out_shapes]
    sems = [pltpu.SemaphoreType.DMA((e.n_sems,)) for e in riders for _ in range(2)]

    def copies(in_refs, out_refs, sem_refs):
        cps, i, o = [], 0, 0
        for k, e in enumerate(riders):
            ni, no = len(e.srcs), len(e.out_shapes)
            cps += e.build(in_refs[i:i + ni], out_refs[o:o + no], sem_refs[2 * k], sem_refs[2 * k + 1])
            i, o = i + ni, o + no
        return cps

    return inputs, out_shapes, sems, copies


_ANY = pl.BlockSpec(memory_space=pl.ANY)


def _host_call(compute, name, grid, riders, inputs, in_specs, out_specs, out_shape, scratch_shapes, aliases=None):
    r_in, r_out, r_sems, copies = _rider_plan(riders)
    ni, no, ns = len(inputs), len(out_shape), len(scratch_shapes)

    def body(*refs):
        ins, rins = refs[:ni], refs[ni:ni + len(r_in)]
        outs = refs[ni + len(r_in):ni + len(r_in) + no]
        routs = refs[ni + len(r_in) + no:ni + len(r_in) + no + len(r_out)]
        scr = refs[ni + len(r_in) + no + len(r_out):]
        first = functools.reduce(lambda a, b: a & b, [pl.program_id(d) == 0 for d in range(len(grid))])
        last = functools.reduce(lambda a, b: a & b, [pl.program_id(d) == grid[d] - 1 for d in range(len(grid))])
        if riders:
            @pl.when(first)
            def _():
                for cp in copies(rins, routs, scr[ns:]):
                    cp.start()

        compute(*ins, *outs, *scr[:ns])

        if riders:
            @pl.when(last)
            def _():
                for cp in copies(rins, routs, scr[ns:]):
                    cp.wait()

    res = pl.pallas_call(
        body, name=name, grid=grid,
        in_specs=list(in_specs) + [_ANY] * len(r_in),
        out_specs=list(out_specs) + [_ANY] * len(r_out),
        out_shape=list(out_shape) + r_out,
        scratch_shapes=list(scratch_shapes) + r_sems,
        input_output_aliases={**(aliases or {}), **_rider_aliases(riders, ni, no)},
        compiler_params=_cparams(("arbitrary",) * len(grid)),
    )(*inputs, *r_in)
    return res[:no], _split_riders(riders, res[no:])


def _split_riders(riders, flat):
    out, o = [], 0
    for e in riders:
        out.append(list(flat[o:o + len(e.out_shapes)]))
        o += len(e.out_shapes)
    return out


def _run_exchange(ex, name):
    n_in, n_out = len(ex.srcs), len(ex.out_shapes)

    def body(*refs):
        cps = ex.build(refs[:n_in], refs[n_in:n_in + n_out], refs[n_in + n_out], refs[n_in + n_out + 1])
        for cp in cps:
            cp.start()
        for cp in cps:
            cp.wait()

    return pl.pallas_call(
        body, name=name,
        in_specs=[_ANY] * n_in, out_specs=[_ANY] * n_out, out_shape=ex.out_shapes,
        input_output_aliases=ex.alias,
        scratch_shapes=[pltpu.SemaphoreType.DMA((ex.n_sems,)), pltpu.SemaphoreType.DMA((ex.n_sems,))],
        compiler_params=pltpu.CompilerParams(has_side_effects=True),
    )(*ex.srcs)


def _gather_sizes(shards, kinds):
    sizes = [s.shape[0] if k == "rows" else s.shape[1] for s, k in zip(shards, kinds)]
    fulls = [SDS((s.shape[0] * N_DEV,) + s.shape[1:], s.dtype) if k == "rows"
             else SDS((s.shape[0], s.shape[1] * N_DEV), s.dtype) for s, k in zip(shards, kinds)]
    return sizes, fulls


def _gather_direct(shards, kinds):
    n = len(shards)
    sizes, fulls = _gather_sizes(shards, kinds)

    def build(ins, outs, send_sems, recv_sems):
        x, y, c = _me()
        cps = []
        for a in range(n):
            mine = _blk(outs[a], kinds[a], 4 * x + 2 * y + c, sizes[a])
            cps.append(pltpu.make_async_copy(ins[a], mine, send_sems.at[5 * a + 4]))
            for j in range(N_CHIP):
                to = (x, y, 1 - c) if j == 0 else (*_chip_peer(x, y, j), c)
                cps.append(pltpu.make_async_remote_copy(
                    src_ref=ins[a], dst_ref=mine, send_sem=send_sems.at[5 * a + j], recv_sem=recv_sems.at[5 * a + j],
                    device_id=to, device_id_type=MESH))
        return cps

    return _Exchange(shards, fulls, 5 * n, build)


def _gather_everywhere(shards, kinds):
    n = len(shards)
    sizes, fulls = _gather_sizes(shards, kinds)

    def build(ins, outs, send_sems, recv_sems):
        x, y, c = _me()
        cps = []
        for a in range(n):
            mine = _blk(outs[a], kinds[a], 4 * x + 2 * y + c, sizes[a])
            cps.append(pltpu.make_async_copy(ins[a], mine, send_sems.at[N_DEV * a]))
            for d in range(1, N_DEV):
                to = ((1 - x) if d & 4 else x, (1 - y) if d & 2 else y, (1 - c) if d & 1 else c)
                cps.append(pltpu.make_async_remote_copy(
                    src_ref=ins[a], dst_ref=mine, send_sem=send_sems.at[N_DEV * a + d],
                    recv_sem=recv_sems.at[N_DEV * a + d], device_id=to, device_id_type=MESH))
        return cps

    return _Exchange(shards, fulls, N_DEV * n, build)


def _gather_forward(fulls, kinds, sizes):
    n = len(fulls)

    def build(ins, outs, send_sems, recv_sems):
        x, y, c = _me()
        cps = []
        for a in range(n):
            for j in (1, 2, 3):
                px, py = _chip_peer(x, y, j)
                k = 4 * px + 2 * py + c
                cps.append(pltpu.make_async_remote_copy(
                    src_ref=_blk(ins[a], kinds[a], k, sizes[a]), dst_ref=_blk(outs[a], kinds[a], k, sizes[a]),
                    send_sem=send_sems.at[3 * a + j - 1], recv_sem=recv_sems.at[3 * a + j - 1],
                    device_id=(x, y, 1 - c), device_id_type=MESH))
        return cps

    return _Exchange(fulls, [SDS(f.shape, f.dtype) for f in fulls], 3 * n, build, alias={a: a for a in range(n)})


def _sibling_exchange(grads, kinds, sizes):
    n = len(grads)

    def blk_shape(a):
        g = grads[a]
        return (sizes[a],) + g.shape[1:] if kinds[a] == "rows" else (g.shape[0], sizes[a])

    def build(ins, outs, send_sems, recv_sems):
        x, y, c = _me()
        cps = []
        for a in range(n):
            for q in range(N_CHIP):
                cps.append(pltpu.make_async_remote_copy(
                    src_ref=_blk(ins[a], kinds[a], 2 * q + (1 - c), sizes[a]), dst_ref=outs[a].at[q],
                    send_sem=send_sems.at[N_CHIP * a + q], recv_sem=recv_sems.at[N_CHIP * a + q],
                    device_id=(x, y, 1 - c), device_id_type=MESH))
        return cps

    return _Exchange(grads, [SDS((N_CHIP,) + blk_shape(a), F32) for a in range(n)], N_CHIP * n, build)


def _chip_partial(g, r1, kind, size, cidx, name):
    if kind == "rows":
        rows, cols = size, g.shape[1]
        g3 = g.reshape(N_DEV, rows, cols)
        rb = _pick(rows, 512, 16)
        g_spec = pl.BlockSpec((1, rb, cols), lambda q, j, c: (2 * q + c[0], j, 0))
        grid = (N_CHIP, rows // rb)
        blk = (1, rb, cols)
        imap = lambda q, j, c: (q, j, 0)
    else:
        rows, cols = g.shape[0], size
        g3 = g
        g_spec = pl.BlockSpec((rows, cols), lambda q, j, c: (0, 2 * q + c[0]))
        grid = (N_CHIP, 1)
        blk = (1, rows, cols)
        imap = lambda q, j, c: (q, 0, 0)

    def body(c_ref, g_ref, r_ref, p_ref, pb_ref):
        s = g_ref[...].reshape(blk) + r_ref[...]
        p_ref[...] = s
        pb_ref[...] = s.astype(BF16)

    return pl.pallas_call(
        body, name=name,
        grid_spec=pltpu.PrefetchScalarGridSpec(
            num_scalar_prefetch=1, grid=grid,
            in_specs=[g_spec, pl.BlockSpec(blk, imap)],
            out_specs=[pl.BlockSpec(blk, imap), pl.BlockSpec(blk, imap)]),
        out_shape=[SDS((N_CHIP, rows, cols), F32), SDS((N_CHIP, rows, cols), BF16)],
        compiler_params=_cparams(("arbitrary", "arbitrary")),
    )(cidx, g3, r1)


def _chip_exchange(parts):
    n = len(parts)
    m = N_CHIP - 1

    def build(ins, outs, send_sems, recv_sems):
        x, y, c = _me()
        cps = []
        for a in range(n):
            for j in (1, 2, 3):
                px, py = _chip_peer(x, y, j)
                cps.append(pltpu.make_async_remote_copy(
                    src_ref=ins[a].at[2 * px + py], dst_ref=outs[a].at[j - 1], send_sem=send_sems.at[m * a + j - 1],
                    recv_sem=recv_sems.at[m * a + j - 1], device_id=(px, py, c), device_id_type=MESH))
        return cps

    return _Exchange(parts, [SDS((m,) + p.shape[1:], BF16) for p in parts], m * n, build)


def _grad_total(part, r2, qidx, name):
    _, rows, cols = part.shape
    rb = _pick(rows, 512, 16)

    def body(q_ref, p_ref, r_ref, o_ref):
        s = p_ref[0]
        for j in range(N_CHIP - 1):
            s = s + r_ref[j].astype(F32)
        o_ref[...] = s

    return pl.pallas_call(
        body, name=name,
        grid_spec=pltpu.PrefetchScalarGridSpec(
            num_scalar_prefetch=1, grid=(rows // rb,),
            in_specs=[pl.BlockSpec((1, rb, cols), lambda i, q: (q[0], i, 0)),
                      pl.BlockSpec((N_CHIP - 1, rb, cols), lambda i, q: (0, i, 0))],
            out_specs=pl.BlockSpec((rb, cols), lambda i, q: (i, 0))),
        out_shape=SDS((rows, cols), F32),
        compiler_params=_cparams(("arbitrary",)),
    )(qidx, part, r2)


def _all_reduce_small(pack, name):
    rows = pack.shape[0]
    rs = rows // N_DEV
    assert rs * N_DEV == rows and rs % 8 == 0

    def body(x_ref, o_ref, rbuf, red, send1, recv1, send2, recv2):
        x, y, c = _me()
        me = 4 * x + 2 * y + c

        def peer(d):
            px = (1 - x) if (d >> 2) & 1 else x
            py = (1 - y) if (d >> 1) & 1 else y
            pc = (1 - c) if d & 1 else c
            return px, py, pc

        def sl(ref, k):
            return ref.at[pl.ds(pl.multiple_of(k * rs, 8), rs)]

        phase1 = []
        for d in range(1, N_DEV):
            px, py, pc = peer(d)
            phase1.append(pltpu.make_async_remote_copy(
                src_ref=sl(x_ref, 4 * px + 2 * py + pc), dst_ref=rbuf.at[d], send_sem=send1.at[d], recv_sem=recv1.at[d],
                device_id=(px, py, pc), device_id_type=MESH))
        for cp in phase1:
            cp.start()
        acc = sl(x_ref, me)[...]
        for cp in phase1:
            cp.wait()
        for d in range(1, N_DEV):
            acc = acc + rbuf[d]
        red[...] = acc
        sl(o_ref, me)[...] = acc
        phase2 = []
        for d in range(1, N_DEV):
            px, py, pc = peer(d)
            phase2.append(pltpu.make_async_remote_copy(
                src_ref=red, dst_ref=sl(o_ref, me), send_sem=send2.at[d], recv_sem=recv2.at[d],
                device_id=(px, py, pc), device_id_type=MESH))
        for cp in phase2:
            cp.start()
        for cp in phase2:
            cp.wait()

    vm = pl.BlockSpec(memory_space=pltpu.VMEM)
    return pl.pallas_call(
        body, name=name, in_specs=[vm], out_specs=vm, out_shape=SDS(pack.shape, F32),
        scratch_shapes=[pltpu.VMEM((N_DEV, rs, LANES), F32), pltpu.VMEM((rs, LANES), F32),
                        pltpu.SemaphoreType.DMA((N_DEV,)), pltpu.SemaphoreType.DMA((N_DEV,)),
                        pltpu.SemaphoreType.DMA((N_DEV,)), pltpu.SemaphoreType.DMA((N_DEV,))],
        compiler_params=_cparams(None, has_side_effects=True),
    )(pack)


def _adamw(w, g, m, v, name):
    rows, cols = w.shape
    rb = rows if rows * cols * 4 <= ADAMW_WHOLE_BYTES else _pick(rows, 256, 8)
    c1 = 1.0 / (1.0 - ADAM_B1 ** ADAM_STEP)
    c2 = 1.0 / (1.0 - ADAM_B2 ** ADAM_STEP)

    def body(w_ref, g_ref, m_ref, v_ref, d_ref, mo_ref, vo_ref):
        gv = g_ref[...]
        mn = ADAM_B1 * m_ref[...] + (1.0 - ADAM_B1) * gv
        vn = ADAM_B2 * v_ref[...] + (1.0 - ADAM_B2) * (gv * gv)
        mo_ref[...] = mn
        vo_ref[...] = vn
        d_ref[...] = -ADAM_LR * ((mn * c1) / (jnp.sqrt(vn * c2) + ADAM_EPS) + ADAM_WD * w_ref[...])

    spec = pl.BlockSpec((rb, cols), lambda i: (i, 0))
    return pl.pallas_call(
        body, name=name, grid=(rows // rb,),
        in_specs=[spec] * 4, out_specs=[spec] * 3, out_shape=[SDS((rows, cols), F32)] * 3,
        compiler_params=_cparams(("arbitrary",)),
    )(w, g, m, v)


def _pad_rows(a, mult=8):
    r = (-a.shape[0]) % mult
    return a if r == 0 else jnp.pad(a, ((0, r), (0, 0)))


def _as_lanes(a):
    flat = a.reshape(-1)
    pad = (-flat.shape[0]) % (8 * LANES)
    if pad:
        flat = jnp.pad(flat, (0, pad))
    return flat.reshape(-1, LANES)


def kernel(x, norm_g, w_in, ln_g, ln_b, w_s, b_s, conv_w, conv_b, w_pool, pool_scale, w_pa, w_pb, w_pc, w_o, final_g, loss_target, m_norm_g, m_w_in, m_ln_g, m_ln_b, m_w_s, m_b_s, m_conv_w, m_conv_b, m_w_pool, m_pool_scale, m_w_pa, m_w_pb, m_w_pc, m_w_o, m_final_g, v_norm_g, v_w_in, v_ln_g, v_ln_b, v_w_s, v_b_s, v_conv_w, v_conv_b, v_w_pool, v_pool_scale, v_w_pa, v_w_pb, v_w_pc, v_w_o, v_final_g):
    L = w_in.shape[0]
    D = x.shape[-1]
    n_loc = w_in.shape[2]
    pc_loc = w_pa.shape[2]
    x0 = x[0]
    target = loss_target[0]
    xi, yi, ci = _me()
    cidx = jnp.reshape(ci, (1,)).astype(jnp.int32)
    qidx = jnp.reshape(2 * xi + yi, (1,)).astype(jnp.int32)

    kinds5 = ["rows", "cols", "cols", "cols", "rows"]

    def layer_shards(l):
        return [w_in[l].T.astype(BF16), w_pa[l].astype(BF16), w_pb[l].astype(BF16), w_pc[l].astype(BF16),
                w_o[l].astype(BF16)]

    def gathered(direct, shards, kinds, l):
        sizes, _ = _gather_sizes(shards, kinds)
        return _run_exchange(_gather_forward(direct, kinds, sizes), f"weights_forward_{l}")

    cw_loc = _pad_rows(conv_w.reshape(L * CONV_TAPS, -1))
    cw_loc = jnp.pad(cw_loc, ((0, 0), (0, LANES - cw_loc.shape[1])))
    causal = jnp.tril(jnp.ones((CHUNK, CHUNK), dtype=bool))

    chip_order = jnp.stack([2 * xi + yi] + [2 * px + py for px, py in (_chip_peer(xi, yi, j) for j in (1, 2, 3))])
    sh0 = layer_shards(0)
    rest0, krest0 = sh0[1:] + [cw_loc], kinds5[1:] + ["rows"]
    (p0, h0, win_t0), delivered = _inproj_gathering(x0, norm_g[0], sh0[0], chip_order.astype(jnp.int32), "inproj_fwd_0",
                                                    [_gather_everywhere(rest0, krest0)])
    rest0_full = delivered[0]
    cw_all = rest0_full[-1].reshape(N_DEV, -1, LANES)[:, :L * CONV_TAPS, :conv_w.shape[2]]
    conv_w_full = jnp.transpose(cw_all, (1, 0, 2)).reshape(L, CONV_TAPS, -1)

    def make_layer(l, full5):
        win_t, wpa, wpb, wpc, wo = full5
        wm = jnp.where(causal, w_s[l], 0.0)
        cvec = jnp.concatenate([ln_g[l][None], ln_b[l][None], conv_w_full[l], conv_b[l][None], pool_scale[l][None],
                                jnp.zeros((C_ROWS - 7, SEG), F32)], axis=0)
        return dict(
            win_t=win_t, wpa=wpa, wpb=wpb, wpc=wpc, wo=wo, cvec=cvec,
            bsb=jnp.repeat(b_s[l].T, HEAD, axis=1),
            wcat=jnp.transpose(wm, (1, 0, 2)).reshape(CHUNK, GROUPS * CHUNK).astype(BF16),
            wcatt=jnp.transpose(wm, (2, 0, 1)).reshape(CHUNK, GROUPS * CHUNK).astype(BF16),
            wpool=w_pool[l].astype(BF16))

    layers, xs, saved, win_next = [], [x0], [], None
    for l in range(L):
        if l == 0:
            p, h = p0, h0
            lw = make_layer(0, [win_t0] + list(rest0_full[:4]))
        else:
            (p, h), delivered = _inproj(xs[-1], norm_g[l], win_next, f"inproj_fwd_{l}",
                                        [_gather_everywhere(layer_shards(l)[1:], kinds5[1:])])
            lw = make_layer(l, [win_next] + delivered[0])
        layers.append(lw)
        nxt = layer_shards(l + 1)[:1] if l + 1 < L else None
        (xn, ya, yb, yc), delivered = _mix_fwd(p, xs[-1], lw, f"mix_fwd_{l}", [_gather_direct(nxt, kinds5[:1])] if nxt else [])
        if nxt:
            win_next = gathered(delivered[0], nxt, kinds5[:1], l + 1)[0]
        saved.append((p, h, ya, yb, yc))
        xs.append(xn)
    dx, loss_acc, dfg_acc = _loss_head(xs[-1], final_g, target, "loss_head")

    rs_sizes = [n_loc, pc_loc, pc_loc, pc_loc, w_o.shape[1]]
    await_sibling, await_chips = [], []
    partial_of, from_chips = {}, {}
    serial = [0]

    def riders_now():
        riders, plan = [], []
        for grp in await_chips:
            riders.append(_chip_exchange([partial_of[t][1] for t, _, _, _ in grp]))
            plan.append(("chips", grp))
        for grp in await_sibling:
            riders.append(_sibling_exchange([g for _, g, _, _ in grp], [k for _, _, k, _ in grp], [s for _, _, _, s in grp]))
            plan.append(("sibling", grp))
        del await_chips[:], await_sibling[:]
        return riders, plan

    def absorb(plan, delivered):
        for (what, grp), res in zip(plan, delivered):
            for (t, g, k, s), r in zip(grp, res):
                if what == "chips":
                    from_chips[t] = r
                else:
                    partial_of[t] = _chip_partial(g, r, k, s, cidx, f"grad_chip_partial_{t[0]}_{t[1]}")
            if what == "sibling":
                await_chips.append(grp)

    small = [None] * L
    for l in reversed(range(L)):
        lw = layers[l]
        p, h, ya, yb, yc = saved[l]
        dp, acts, merged, dys, gwc, gbs, gwpool, gvec = _mix_bwd(p, dx, ya, yb, yc, lw, f"mix_bwd_{l}")
        riders, plan = riders_now()
        (gwpa, gwpb, gwpc, gwo), delivered = _proj_wgrad(acts, merged, dys, dx, f"proj_wgrad_{l}", riders)
        absorb(plan, delivered)
        await_sibling.append([((l, a), g, kinds5[a], rs_sizes[a]) for a, g in ((1, gwpa), (2, gwpb), (3, gwpc), (4, gwo))])

        def bwd_x(dxo, pieces):
            nt = _inproj_token_blocks(dxo.shape[0])
            pieces = min(pieces, nt)
            done, dng, b0 = None, None, 0
            for k in range(pieces):
                cnt = (nt - b0) if k == pieces - 1 else max(1, (nt * LAST_PIECE_SHARE[0]) // (LAST_PIECE_SHARE[1] * (pieces - 1)))
                riders, plan = riders_now()
                (done, dng_k), delivered = _inproj_bwd_x(dp, lw["win_t"], xs[l], norm_g[l], dxo, f"inproj_bwd_x_{l}_{k}",
                                                         riders, blocks=(b0, cnt), fill=done)
                absorb(plan, delivered)
                dng = dng_k if dng is None else dng + dng_k
                b0 += cnt
            return done, dng

        def bwd_w():
            riders, plan = riders_now()
            (gwin_t,), delivered = _inproj_bwd_w(dp, h, f"inproj_bwd_w_{l}", riders)
            absorb(plan, delivered)
            await_sibling.append([((l, 0), gwin_t, kinds5[0], rs_sizes[0])])

        if l == L - 1:
            dx, dng = bwd_x(dx, 1)
            bwd_w()
        else:
            bwd_w()
            dx, dng = bwd_x(dx, 2 if l == 0 else 1)
        small[l] = dict(norm_g=dng[0], ln_g=gvec[V_LNG], ln_b=gvec[V_LNB], w_s=gwc, b_s=gbs, conv_w=gvec[V_CW0:V_CW0 + 3],
                        conv_b=gvec[V_CB], w_pool=gwpool, pool_scale=gvec[V_PS])
    while await_sibling or await_chips:
        riders, plan = riders_now()
        delivered = []
        for ex in riders:
            delivered.append(_run_exchange(ex, f"grad_exchange_tail_{serial[0]}"))
            serial[0] += 1
        absorb(plan, delivered)
    grad_x = dx[None]
    big_grads = []
    for l in range(L):
        tot = [_grad_total(partial_of[(l, a)][0], from_chips[(l, a)], qidx, f"grad_total_{l}_{a}") for a in range(5)]
        big_grads.append([tot[0],
                          tot[1].reshape(SEG, pc_loc), tot[2].reshape(SEG, pc_loc), tot[3].reshape(SEG, pc_loc),
                          tot[4]])

    names = ["norm_g", "ln_g", "ln_b", "w_s", "b_s", "conv_w", "conv_b", "w_pool", "pool_scale"]
    pieces = [_as_lanes(jnp.stack([small[l][nm] for l in range(L)])) for nm in names]
    pieces += [_as_lanes(dfg_acc[0]), loss_acc]
    sizes = [pc.shape[0] for pc in pieces]
    pack = jnp.concatenate(pieces, axis=0)
    pack = _pad_rows(pack, 8 * N_DEV)
    red = _all_reduce_small(pack, "small_grads_all_reduce")
    offs = [0]
    for s in sizes:
        offs.append(offs[-1] + s)

    def unpack(i, shape):
        n = math.prod(shape)
        return red[offs[i]:offs[i + 1]].reshape(-1)[:n].reshape(shape)

    g_norm_g = unpack(0, (L, D))
    g_ln_g = unpack(1, (L, SEG))
    g_ln_b = unpack(2, (L, SEG))
    g_w_s = unpack(3, (L, GROUPS, CHUNK, CHUNK))
    g_b_s = jnp.transpose(unpack(4, (L, CHUNK, LANES))[:, :, :GROUPS], (0, 2, 1))
    g_conv_w_full = unpack(5, (L, CONV_TAPS, SEG))
    g_conv_b = unpack(6, (L, SEG))
    g_w_pool = unpack(7, (L, len(POOL_WINDOWS), POOL_GROUP, POOL_GROUP))
    g_pool_scale = unpack(8, (L, SEG))
    g_final_g = unpack(9, (D,))
    loss = red[offs[10], 0]
    dev = 4 * xi + 2 * yi + ci
    g_conv_w = lax.dynamic_slice_in_dim(g_conv_w_full, dev * conv_w.shape[2], conv_w.shape[2], axis=2)

    g_w_in_t = jnp.stack([big_grads[l][0] for l in range(L)])
    g_w_in = jnp.swapaxes(g_w_in_t, 1, 2)
    g_w_pa = jnp.stack([big_grads[l][1] for l in range(L)])
    g_w_pb = jnp.stack([big_grads[l][2] for l in range(L)])
    g_w_pc = jnp.stack([big_grads[l][3] for l in range(L)])
    g_w_o = jnp.stack([big_grads[l][4] for l in range(L)])

    grads = dict(norm_g=g_norm_g, w_in=g_w_in, ln_g=g_ln_g, ln_b=g_ln_b, w_s=g_w_s, b_s=g_b_s, conv_w=g_conv_w,
                 conv_b=g_conv_b, w_pool=g_w_pool, pool_scale=g_pool_scale, w_pa=g_w_pa, w_pb=g_w_pb, w_pc=g_w_pc,
                 w_o=g_w_o, final_g=g_final_g)
    weights = dict(norm_g=norm_g, w_in=w_in, ln_g=ln_g, ln_b=ln_b, w_s=w_s, b_s=b_s, conv_w=conv_w, conv_b=conv_b,
                   w_pool=w_pool, pool_scale=pool_scale, w_pa=w_pa, w_pb=w_pb, w_pc=w_pc, w_o=w_o, final_g=final_g)
    ms = dict(norm_g=m_norm_g, w_in=m_w_in, ln_g=m_ln_g, ln_b=m_ln_b, w_s=m_w_s, b_s=m_b_s, conv_w=m_conv_w,
              conv_b=m_conv_b, w_pool=m_w_pool, pool_scale=m_pool_scale, w_pa=m_w_pa, w_pb=m_w_pb, w_pc=m_w_pc,
              w_o=m_w_o, final_g=m_final_g)
    vs = dict(norm_g=v_norm_g, w_in=v_w_in, ln_g=v_ln_g, ln_b=v_ln_b, w_s=v_w_s, b_s=v_b_s, conv_w=v_conv_w,
              conv_b=v_conv_b, w_pool=v_w_pool, pool_scale=v_pool_scale, w_pa=v_w_pa, w_pb=v_w_pb, w_pc=v_w_pc,
              w_o=v_w_o, final_g=v_final_g)
    order = ["norm_g", "w_in", "ln_g", "ln_b", "w_s", "b_s", "conv_w", "conv_b", "w_pool", "pool_scale", "w_pa", "w_pb",
             "w_pc", "w_o", "final_g"]

    delta, new_m, new_v = {}, {}, {}
    big = ["w_in", "w_pa", "w_pb", "w_pc", "w_o"]
    tr = lambda a: jnp.swapaxes(a, 1, 2)
    for nm in big:
        into = tr if nm == "w_in" else (lambda a: a)
        shp = into(weights[nm]).shape
        two = lambda a: a.reshape(-1, shp[-1])
        g2 = two(g_w_in_t) if nm == "w_in" else two(grads[nm])
        d, mn, vn = _adamw(two(into(weights[nm])), g2, two(into(ms[nm])), two(into(vs[nm])), f"adamw_{nm}")
        delta[nm], new_m[nm], new_v[nm] = (into(a.reshape(shp)) for a in (d, mn, vn))
    rest = [nm for nm in order if nm not in big]
    cat = lambda src: jnp.concatenate([_as_lanes(src[nm]) for nm in rest], axis=0)
    d, mn, vn = _adamw(cat(weights), cat(grads), cat(ms), cat(vs), "adamw_small")
    off = 0
    for nm in rest:
        shp = weights[nm].shape
        n = math.prod(shp)
        rows = _as_lanes(weights[nm]).shape[0]
        cut = lambda a: a[off:off + rows].reshape(-1)[:n].reshape(shp)
        delta[nm], new_m[nm], new_v[nm] = cut(d), cut(mn), cut(vn)
        off += rows

    return (loss, grad_x, *[grads[nm] for nm in order], *[delta[nm] for nm in order],
            *[new_m[nm] for nm in order], *[new_v[nm] for nm in order])
```

```python
import functools
import math

import numpy as np
import jax
import jax.numpy as jnp
from jax import lax
from jax.experimental import pallas as pl
from jax.experimental.pallas import tpu as pltpu

F32 = jnp.float32
BF16 = jnp.bfloat16
SDS = jax.ShapeDtypeStruct
MESH = pl.DeviceIdType.MESH

SEG = 512
CHUNK = 128
GROUPS = 8
HEAD = SEG // GROUPS
POOL_WINDOWS = (2, 4, 8, 16)
POOL_GROUP = SEG // len(POOL_WINDOWS)
CONV_TAPS = 3
HALO = 16
RMS_EPS = 1e-6
LN_EPS = 1e-5
ADAM_LR, ADAM_B1, ADAM_B2, ADAM_EPS, ADAM_WD, ADAM_STEP = 0.001, 0.9, 0.999, 1e-08, 0.01, 10

O_U, O_V, O_ZA, O_XB, O_BG, O_CG, O_ZB, O_XC, O_ZC, O_G = (SEG * i for i in range(10))

N_DEV = 8
N_CHIP = 4
LANES = 128
VMEM_LIMIT = 48 * 1024 * 1024
ADAMW_WHOLE_BYTES = 2 * 1024 * 1024
INPROJ_ROWS = 1024
BWD_X_ROWS = 512
WGRAD_ROWS = 2048
RIDERS_FROM_CHUNK = 2
LAST_PIECE_SHARE = (3, 8)


def _cparams(sem=None, **kw):
    return pltpu.CompilerParams(dimension_semantics=sem, vmem_limit_bytes=VMEM_LIMIT, **kw)


def _pick(total, target, mult):
    best = None
    for d in range(mult, min(total, target) + 1, mult):
        if total % d == 0:
            best = d
    assert best is not None, (total, target, mult)
    return best


def _dot(a, b):
    return jnp.dot(a, b, preferred_element_type=F32)


def _dot_nt(a, b):
    return lax.dot_general(a, b, (((1,), (1,)), ((), ())), preferred_element_type=F32)


def _dot_tn(a, b):
    return lax.dot_general(a, b, (((0,), (0,)), ((), ())), preferred_element_type=F32)


def _zero(ref):
    ref[...] = jnp.zeros(ref.shape, ref.dtype)


def _sigmoid(x):
    return 1.0 / (1.0 + jnp.exp(-x))


_GELU_C = math.sqrt(2.0 / math.pi)


def _gelu(x):
    t = jnp.tanh(_GELU_C * (x + 0.044715 * x * x * x))
    return 0.5 * x * (1.0 + t), t


def _gelu_grad(x, t):
    return 0.5 * (1.0 + t) + 0.5 * x * (1.0 - t * t) * _GELU_C * (1.0 + 3.0 * 0.044715 * x * x)


def _me():
    return lax.axis_index("x"), lax.axis_index("y"), lax.axis_index("c")


def _inproj(x, norm_g, win_t, name, riders=()):
    T, D = x.shape
    N = win_t.shape[0]
    bt = _pick(T, INPROJ_ROWS, 16)
    bn = _pick(N, 1536, LANES)
    grid = (T // bt, N // bn)

    def compute(x_ref, g_ref, w_ref, p_ref, h_ref, hs_ref):
        @pl.when(pl.program_id(1) == 0)
        def _():
            xv = x_ref[...]
            rstd = lax.rsqrt(jnp.mean(xv * xv, axis=-1, keepdims=True) + RMS_EPS)
            hb = (xv * rstd * g_ref[...]).astype(BF16)
            hs_ref[...] = hb
            h_ref[...] = hb

        p_ref[...] = _dot_nt(hs_ref[...], w_ref[...]).astype(BF16)

    return _host_call(
        compute, name, grid, riders,
        inputs=[x, norm_g.reshape(1, D), win_t],
        in_specs=[pl.BlockSpec((bt, D), lambda i, j: (i, 0)),
                  pl.BlockSpec((1, D), lambda i, j: (0, 0)),
                  pl.BlockSpec((bn, D), lambda i, j: (j, 0))],
        out_specs=[pl.BlockSpec((bt, bn), lambda i, j: (i, j)),
                   pl.BlockSpec((bt, D), lambda i, j: (i, 0))],
        out_shape=[SDS((T, N), BF16), SDS((T, D), BF16)],
        scratch_shapes=[pltpu.VMEM((bt, D), BF16)])


def _inproj_gathering(x, norm_g, w_loc, chip_order, name, riders=()):
    T, D = x.shape
    n = w_loc.shape[0]
    N = n * N_DEV
    cw = 2 * n
    bt = _pick(T, INPROJ_ROWS, 16)
    nt = T // bt
    r_in, r_out, r_sems, copies = _rider_plan(riders)
    n_rin, n_rout = len(r_in), len(r_out)

    def body(q_ref, x_ref, g_ref, wloc_ref, *rest):
        rins = rest[:n_rin]
        p_ref, h_ref, wfull_ref = rest[n_rin:n_rin + 3]
        routs = rest[n_rin + 3:n_rin + 3 + n_rout]
        hs_ref, wbuf, send_sems, recv_sems, loc_sems = rest[n_rin + 3 + n_rout:n_rin + 8 + n_rout]
        rsems = rest[n_rin + 8 + n_rout:]
        j, i = pl.program_id(0), pl.program_id(1)
        cx, cy, cc = _me()
        sibling = (cx, cy, 1 - cc)

        def rows(k):
            return wfull_ref.at[pl.ds(pl.multiple_of(k * n, 8), n)]

        def shard_copy(slot, src, k, to):
            return pltpu.make_async_remote_copy(src_ref=src, dst_ref=rows(k), send_sem=send_sems.at[slot],
                                                recv_sem=recv_sems.at[slot], device_id=to, device_id_type=MESH)

        me = 4 * cx + 2 * cy + cc
        place_mine = pltpu.make_async_copy(wloc_ref, rows(me), loc_sems.at[0])
        sends = [shard_copy(0, wloc_ref, me, sibling)]
        for jj in (1, 2, 3):
            sends.append(shard_copy(jj, wloc_ref, me, (*_chip_peer(cx, cy, jj), cc)))

        def forward(jj):
            px, py = _chip_peer(cx, cy, jj)
            k = 4 * px + 2 * py + cc
            return shard_copy(3 + jj, rows(k), k, sibling)

        def load_chunk(q):
            cp = pltpu.make_async_copy(wfull_ref.at[pl.ds(pl.multiple_of(q * cw, 8), cw)], wbuf, loc_sems.at[1])
            cp.start()
            cp.wait()

        keep_h = pltpu.make_async_copy(hs_ref, h_ref, loc_sems.at[2])

        @pl.when((j == 0) & (i == 0))
        def _():
            place_mine.start()
            for cp in sends:
                cp.start()
            place_mine.wait()
            sends[0].wait_recv()
            load_chunk(q_ref[0])

        for jj in (1, 2, 3):
            @pl.when((j == jj) & (i == 0))
            def _(jj=jj):
                sends[jj].wait_recv()
                fwd = forward(jj)
                fwd.start()
                fwd.wait_recv()
                load_chunk(q_ref[jj])
                if jj == RIDERS_FROM_CHUNK:
                    for cp in copies(rins, routs, rsems):
                        cp.start()

        tok = pl.ds(pl.multiple_of(i * bt, bt), bt)

        @pl.when(j == 0)
        def _():
            xv = x_ref[...]
            rstd = lax.rsqrt(jnp.mean(xv * xv, axis=-1, keepdims=True) + RMS_EPS)
            hs_ref[tok, :] = (xv * rstd * g_ref[...]).astype(BF16)

        @pl.when((j == 0) & (i == nt - 1))
        def _():
            keep_h.start()

        p_ref[...] = _dot_nt(hs_ref[tok, :], wbuf[...]).astype(BF16)

        @pl.when((j == N_CHIP - 1) & (i == nt - 1))
        def _():
            keep_h.wait()
            for cp in sends:
                cp.wait_send()
            for jj in (1, 2, 3):
                forward(jj).wait_send()
            for cp in copies(rins, routs, rsems):
                cp.wait()

    res = pl.pallas_call(
        body, name=name,
        grid_spec=pltpu.PrefetchScalarGridSpec(
            num_scalar_prefetch=1, grid=(N_CHIP, nt),
            in_specs=[pl.BlockSpec((bt, D), lambda j, i, q: (jnp.where(j == 0, i, nt - 1), 0)),
                      pl.BlockSpec((1, D), lambda j, i, q: (0, 0)), _ANY] + [_ANY] * n_rin,
            out_specs=[pl.BlockSpec((bt, cw), lambda j, i, q: (i, q[j])), _ANY, _ANY] + [_ANY] * n_rout,
            scratch_shapes=[pltpu.VMEM((T, D), BF16), pltpu.VMEM((cw, D), BF16), pltpu.SemaphoreType.DMA((7,)),
                            pltpu.SemaphoreType.DMA((7,)), pltpu.SemaphoreType.DMA((3,))] + r_sems),
        out_shape=[SDS((T, N), BF16), SDS((T, D), BF16), SDS((N, D), BF16)] + r_out,
        input_output_aliases=_rider_aliases(riders, 4, 3),
        compiler_params=_cparams(("arbitrary", "arbitrary")),
    )(chip_order, x, norm_g.reshape(1, D), w_loc, *r_in)
    return res[:3], _split_riders(riders, res[3:])


C_LNG, C_LNB, C_CW0, C_CW1, C_CW2, C_CB, C_PS = range(7)
C_ROWS = 8


def _pool_bands(R, anticausal):
    t = np.arange(R)[:, None]
    s = np.arange(R + CHUNK)[None, :]
    bands = [((s >= t) & (s < t + w)) if anticausal else ((s > t + CHUNK - w) & (s <= t + CHUNK)) for w in POOL_WINDOWS]
    return jnp.asarray(np.stack(bands), dtype=BF16)


def _mixers(p_ref, hxb_ref, hcg_ref, hxc_ref, cv, bsb_ref, wcat_ref, wpool_ref, band_ref, extb,
            first, blk, R, need_grad):
    def seg(lo):
        return p_ref[:, lo:lo + SEG].astype(F32)

    u, v, za = seg(O_U), seg(O_V), seg(O_ZA)
    xb, bg, cg, zb = seg(O_XB), seg(O_BG), seg(O_CG), seg(O_ZB)
    xc, zc = seg(O_XC), seg(O_ZC)
    out = {}

    ug, tu = _gelu(u)
    vg, tv = _gelu(v)
    mu = jnp.mean(vg, axis=-1, keepdims=True)
    vcen = vg - mu
    rs = lax.rsqrt(jnp.mean(vcen * vcen, axis=-1, keepdims=True) + LN_EPS)
    vhat = vcen * rs
    vn = (vhat * cv[C_LNG:C_LNG + 1, :] + cv[C_LNB:C_LNB + 1, :]).astype(BF16)
    lane_group = lax.broadcasted_iota(jnp.int32, (CHUNK, SEG), 1) // HEAD
    zero_b = jnp.zeros((CHUNK, SEG), BF16)
    sgs = []
    for ci in range(R // CHUNK):
        vc = vn[ci * CHUNK:(ci + 1) * CHUNK]
        vst = jnp.concatenate([jnp.where(lane_group == g, vc, zero_b) for g in range(GROUPS)], axis=0)
        sgs.append(_dot(wcat_ref[...], vst) + bsb_ref[...])
    sg = sgs[0] if len(sgs) == 1 else jnp.concatenate(sgs, axis=0)
    a_out = ug * sg
    sa = _sigmoid(za)
    out["a"] = a_out * (za * sa)

    cx = cg * xb
    halo_b = hcg_ref[...].astype(F32) * hxb_ref[...].astype(F32)
    extb[0:HALO, :] = jnp.where(first, 0.0, halo_b)
    extb[HALO:HALO + R, :] = cx
    cx1 = extb[pl.ds(HALO - 1, R), :]
    cx2 = extb[pl.ds(HALO - 2, R), :]
    yconv = (cv[C_CW0:C_CW0 + 1, :] * cx2 + cv[C_CW1:C_CW1 + 1, :] * cx1
             + cv[C_CW2:C_CW2 + 1, :] * cx + cv[C_CB:C_CB + 1, :])
    b_out = bg * yconv
    sb = _sigmoid(zb)
    out["b"] = b_out * (zb * sb)

    halo_c = hxc_ref[...]
    xc_ext = jnp.concatenate([jnp.zeros((CHUNK - HALO, SEG), BF16), jnp.where(first, jnp.zeros_like(halo_c), halo_c),
                              p_ref[:, O_XC:O_XC + SEG]], axis=0)
    tpos = blk * R + lax.broadcasted_iota(jnp.int32, (R, POOL_GROUP), 0) + 1
    pooled, invs, pws = [], [], []
    for gi, w in enumerate(POOL_WINDOWS):
        lo = gi * POOL_GROUP
        win = _dot(band_ref[gi], xc_ext[:, lo:lo + POOL_GROUP])
        inv = 1.0 / jnp.minimum(tpos, w).astype(F32)
        pg = (win * inv - xc[:, lo:lo + POOL_GROUP]).astype(BF16)
        pooled.append(pg)
        invs.append(inv)
        pws.append(_dot(pg, wpool_ref[gi]))
    pw = jnp.concatenate(pws, axis=1)
    c_out = pw * cv[C_PS:C_PS + 1, :]
    sc = _sigmoid(zc)
    out["c"] = c_out * (zc * sc)

    if need_grad:
        out.update(u=u, v=v, tu=tu, tv=tv, ug=ug, sg=sg, a_out=a_out, za=za, sa=sa,
                   rs=rs, vhat=vhat, vn=vn, lane_group=lane_group, zero_b=zero_b,
                   xb=xb, bg=bg, cg=cg, cx=cx, cx1=cx1, cx2=cx2, yconv=yconv, b_out=b_out, zb=zb, sb=sb,
                   pooled=pooled, invs=invs, pw=pw, c_out=c_out, zc=zc, sc=sc)
    return out


def _halo_specs(R, nb, rev):
    step = R // HALO

    def mk(col):
        def imap(i):
            b = (nb - 1 - i) if rev else i
            return (jnp.maximum(b * step - 1, 0), col)
        return pl.BlockSpec((HALO, SEG), imap)

    return [mk(O_XB // SEG), mk(O_CG // SEG), mk(O_XC // SEG)]


def _const_spec(shape):
    nd = len(shape)
    return pl.BlockSpec(shape, lambda i: (0,) * nd, pipeline_mode=pl.Buffered(1))


MIX_FWD_ROWS = 512
MIX_BWD_ROWS = 256


def _mix_block_rows(T, target):
    return _pick(T, target, CHUNK)


def _mix_fwd(p, x, lw, name, riders=()):
    T, D = x.shape
    N = p.shape[1]
    R = _mix_block_rows(T, MIX_FWD_ROWS)
    nb = T // R

    def body(p_ref, hxb, hcg, hxc, x_ref, cv_ref, bsb_ref, wcat_ref, wpool_ref, band_ref, wpa_ref, wpb_ref, wpc_ref,
             wo_ref, xo_ref, ya_ref, yb_ref, yc_ref, extb):
        i = pl.program_id(0)
        cv = cv_ref[...]
        r = _mixers(p_ref, hxb, hcg, hxc, cv, bsb_ref, wcat_ref, wpool_ref, band_ref, extb,
                    i == 0, i, R, False)
        merged = None
        for k, (act, w_ref, y_ref) in enumerate(((r["a"], wpa_ref, ya_ref), (r["b"], wpb_ref, yb_ref),
                                                 (r["c"], wpc_ref, yc_ref))):
            y = _dot(act.astype(BF16), w_ref[...]).astype(BF16)
            y_ref[...] = y
            term = _sigmoid(p_ref[:, O_G + k * D:O_G + (k + 1) * D]) * y
            merged = term if merged is None else merged + term
        xo_ref[...] = x_ref[...] + _dot(merged, wo_ref[...])

    row = lambda w: pl.BlockSpec((R, w), lambda i: (i, 0))
    consts = [lw["cvec"], lw["bsb"], lw["wcat"], lw["wpool"], _pool_bands(R, False), lw["wpa"], lw["wpb"], lw["wpc"],
              lw["wo"]]
    return _host_call(
        body, name, (nb,), riders,
        inputs=[p, p, p, p, x, *consts],
        in_specs=[row(N)] + _halo_specs(R, nb, False) + [row(D)] + [_const_spec(c.shape) for c in consts],
        out_specs=[row(D), row(D), row(D), row(D)],
        out_shape=[SDS((T, D), F32), SDS((T, D), BF16), SDS((T, D), BF16), SDS((T, D), BF16)],
        scratch_shapes=[pltpu.VMEM((HALO + R, SEG), F32)])


def _loss_head(x, final_g, target, name):
    T, D = x.shape
    bt = _pick(T, 512, 8)

    def body(x_ref, g_ref, t_ref, dx_ref, loss_ref, dg_ref):
        @pl.when(pl.program_id(0) == 0)
        def _():
            _zero(loss_ref)
            _zero(dg_ref)

        xv = x_ref[...]
        g = g_ref[...]
        rstd = lax.rsqrt(jnp.mean(xv * xv, axis=-1, keepdims=True) + RMS_EPS)
        xhat = xv * rstd
        err = xhat * g - t_ref[...]
        part = 0.5 * jnp.sum(jnp.sum(err * err, axis=-1, keepdims=True), axis=0, keepdims=True) / D
        loss_ref[...] += jnp.broadcast_to(part, loss_ref.shape)
        dy = err * (1.0 / D)
        dg_ref[0:1, :] += jnp.sum(dy * xhat, axis=0, keepdims=True)
        dxn = dy * g
        dx_ref[...] = rstd * (dxn - xhat * jnp.mean(dxn * xhat, axis=-1, keepdims=True))

    return pl.pallas_call(
        body, name=name, grid=(T // bt,),
        in_specs=[pl.BlockSpec((bt, D), lambda i: (i, 0)), _const_spec((1, D)), pl.BlockSpec((bt, D), lambda i: (i, 0))],
        out_specs=[pl.BlockSpec((bt, D), lambda i: (i, 0)), _const_spec((8, LANES)), _const_spec((8, D))],
        out_shape=[SDS((T, D), F32), SDS((8, LANES), F32), SDS((8, D), F32)],
        compiler_params=_cparams(("arbitrary",)),
    )(x, final_g.reshape(1, D), target)


V_LNG, V_LNB, V_CB, V_PS, V_CW0, V_CW1, V_CW2 = range(7)


def _mix_bwd(p, dxo, ya, yb, yc, lw, name):
    T, D = dxo.shape
    N = p.shape[1]
    R = _mix_block_rows(T, MIX_BWD_ROWS)
    nb = T // R

    def body(p_ref, hxb, hcg, hxc, dxo_ref, ya_ref, yb_ref, yc_ref, cv_ref, bsb_ref, wcat_ref, wcatt_ref,
             wpool_ref, band_ref, bandt_ref, wpa_ref, wpb_ref, wpc_ref, wo_ref,
             dp_ref, acts_ref, mrg_ref, dys_ref, gwc_ref, gbs_ref, gwpool_ref, gvec_ref,
             extb, extdy, cdy, cq, bsacc):
        i = pl.program_id(0)
        blk = nb - 1 - i

        @pl.when(i == 0)
        def _():
            for ref in (gwc_ref, gwpool_ref, gvec_ref, cdy, cq, bsacc):
                _zero(ref)

        cv = cv_ref[...]
        r = _mixers(p_ref, hxb, hcg, hxc, cv, bsb_ref, wcat_ref, wpool_ref, band_ref, extb,
                    blk == 0, blk, R, True)

        dxo_b = dxo_ref[...].astype(BF16)
        dm = _dot_nt(dxo_b, wo_ref[...]).astype(BF16)
        ys = [ya_ref[...], yb_ref[...], yc_ref[...]]
        sig = [_sigmoid(p_ref[:, O_G + k * D:O_G + (k + 1) * D]) for k in range(3)]
        mrg_ref[...] = sig[0] * ys[0] + sig[1] * ys[1] + sig[2] * ys[2]
        dacts = []
        for k, (act, w_ref) in enumerate(((r["a"], wpa_ref), (r["b"], wpb_ref), (r["c"], wpc_ref))):
            dyk = dm * sig[k]
            dp_ref[:, O_G + k * D:O_G + (k + 1) * D] = dyk * ys[k] * (1.0 - sig[k])
            acts_ref[:, k * SEG:(k + 1) * SEG] = act.astype(BF16)
            dys_ref[:, k * D:(k + 1) * D] = dyk
            dacts.append(_dot_nt(dyk, w_ref[...]))
        da, db, dc = dacts

        def silu_bwd(dact, pre, z, s):
            return dact * (z * s), dact * pre * (s * (1.0 + z * (1.0 - s)))

        d_aout, dza = silu_bwd(da, r["a_out"], r["za"], r["sa"])
        dp_ref[:, O_ZA:O_ZA + SEG] = dza.astype(BF16)
        dp_ref[:, O_U:O_U + SEG] = (d_aout * r["sg"] * _gelu_grad(r["u"], r["tu"])).astype(BF16)
        d_sg = d_aout * r["ug"]
        dvns = []
        for ci in range(R // CHUNK):
            dsc = d_sg[ci * CHUNK:(ci + 1) * CHUNK]
            bsacc[...] += dsc
            dsc_b = dsc.astype(BF16)
            dst = jnp.concatenate([jnp.where(r["lane_group"] == g, dsc_b, r["zero_b"]) for g in range(GROUPS)], axis=0)
            dvns.append(_dot(wcatt_ref[...], dst))
            gwc_ref[...] += _dot_nt(dst, r["vn"][ci * CHUNK:(ci + 1) * CHUNK])
        d_vn = dvns[0] if len(dvns) == 1 else jnp.concatenate(dvns, axis=0)
        vhat = r["vhat"]
        gvec_ref[V_LNG:V_LNG + 1, :] += jnp.sum(d_vn * vhat, axis=0, keepdims=True)
        gvec_ref[V_LNB:V_LNB + 1, :] += jnp.sum(d_vn, axis=0, keepdims=True)
        d_vhat = d_vn * cv[C_LNG:C_LNG + 1, :]
        d_vg = r["rs"] * (d_vhat - jnp.mean(d_vhat, axis=-1, keepdims=True)
                          - vhat * jnp.mean(d_vhat * vhat, axis=-1, keepdims=True))
        dp_ref[:, O_V:O_V + SEG] = (d_vg * _gelu_grad(r["v"], r["tv"])).astype(BF16)

        d_bout, dzb = silu_bwd(db, r["b_out"], r["zb"], r["sb"])
        dp_ref[:, O_ZB:O_ZB + SEG] = dzb.astype(BF16)
        dp_ref[:, O_BG:O_BG + SEG] = (d_bout * r["yconv"]).astype(BF16)
        d_y = d_bout * r["bg"]
        gvec_ref[V_CB:V_CB + 1, :] += jnp.sum(d_y, axis=0, keepdims=True)
        gvec_ref[V_CW0:V_CW0 + 1, :] += jnp.sum(d_y * r["cx2"], axis=0, keepdims=True)
        gvec_ref[V_CW1:V_CW1 + 1, :] += jnp.sum(d_y * r["cx1"], axis=0, keepdims=True)
        gvec_ref[V_CW2:V_CW2 + 1, :] += jnp.sum(d_y * r["cx"], axis=0, keepdims=True)
        extdy[0:R, :] = d_y
        extdy[R:R + HALO, :] = cdy[...]
        d_cx = (cv[C_CW2:C_CW2 + 1, :] * d_y + cv[C_CW1:C_CW1 + 1, :] * extdy[pl.ds(1, R), :]
                + cv[C_CW0:C_CW0 + 1, :] * extdy[pl.ds(2, R), :])
        cdy[...] = d_y[0:HALO]
        dp_ref[:, O_CG:O_CG + SEG] = (d_cx * r["xb"]).astype(BF16)
        dp_ref[:, O_XB:O_XB + SEG] = (d_cx * r["cg"]).astype(BF16)

        d_cout, dzc = silu_bwd(dc, r["c_out"], r["zc"], r["sc"])
        dp_ref[:, O_ZC:O_ZC + SEG] = dzc.astype(BF16)
        gvec_ref[V_PS:V_PS + 1, :] += jnp.sum(d_cout * r["pw"], axis=0, keepdims=True)
        d_pw = (d_cout * cv[C_PS:C_PS + 1, :]).astype(BF16)
        dpool, scaled = [], []
        for gi, w in enumerate(POOL_WINDOWS):
            lo = gi * POOL_GROUP
            dpw_g = d_pw[:, lo:lo + POOL_GROUP]
            gwpool_ref[lo:lo + POOL_GROUP, :] += _dot_tn(r["pooled"][gi], dpw_g)
            dpg = _dot_nt(dpw_g, wpool_ref[gi])
            dpool.append(dpg)
            scaled.append((dpg * r["invs"][gi]).astype(BF16))
        q = jnp.concatenate(scaled, axis=1)
        q_ext = jnp.concatenate([q, cq[...], jnp.zeros((CHUNK - HALO, SEG), BF16)], axis=0)
        for gi, w in enumerate(POOL_WINDOWS):
            lo = gi * POOL_GROUP
            acc = _dot(bandt_ref[gi], q_ext[:, lo:lo + POOL_GROUP])
            dp_ref[:, O_XC + lo:O_XC + lo + POOL_GROUP] = (acc - dpool[gi]).astype(BF16)
        cq[...] = q[0:HALO]

        @pl.when(i == nb - 1)
        def _():
            rr = lax.broadcasted_iota(jnp.int32, gwc_ref.shape, 0) % CHUNK
            cc = lax.broadcasted_iota(jnp.int32, gwc_ref.shape, 1)
            gwc_ref[...] = jnp.where(cc <= rr, gwc_ref[...], 0.0)
            acc = bsacc[...]
            hi = acc.astype(BF16)
            lo_ = (acc - hi.astype(F32)).astype(BF16)
            sel = (lax.broadcasted_iota(jnp.int32, (SEG, LANES), 0) // HEAD
                   == lax.broadcasted_iota(jnp.int32, (SEG, LANES), 1)).astype(BF16)
            gbs_ref[...] = _dot(hi, sel) + _dot(lo_, sel)

    row = lambda w: pl.BlockSpec((R, w), lambda i: (nb - 1 - i, 0))
    consts = [lw["cvec"], lw["bsb"], lw["wcat"], lw["wcatt"], lw["wpool"], _pool_bands(R, False), _pool_bands(R, True),
              lw["wpa"], lw["wpb"], lw["wpc"], lw["wo"]]
    acc_shapes = [(GROUPS * CHUNK, CHUNK), (CHUNK, LANES), (SEG, POOL_GROUP), (8, SEG)]
    row_widths = [N, 3 * SEG, D, 3 * D]
    return pl.pallas_call(
        body, name=name, grid=(nb,),
        in_specs=([row(N)] + _halo_specs(R, nb, True) + [row(D), row(D), row(D), row(D)]
                  + [_const_spec(c.shape) for c in consts]),
        out_specs=[row(w) for w in row_widths] + [_const_spec(s) for s in acc_shapes],
        out_shape=[SDS((T, w), BF16) for w in row_widths] + [SDS(s, F32) for s in acc_shapes],
        scratch_shapes=[pltpu.VMEM((HALO + R, SEG), F32)] * 2
        + [pltpu.VMEM((HALO, SEG), F32), pltpu.VMEM((HALO, SEG), BF16), pltpu.VMEM((CHUNK, SEG), F32)],
        compiler_params=_cparams(("arbitrary",)),
    )(p, p, p, p, dxo, ya, yb, yc, *consts)


def _proj_wgrad(acts, merged, dys, dxo, name, riders=()):
    T, D = dxo.shape
    bk = _pick(T, WGRAD_ROWS // 2, 16)

    def body(a_ref, m_ref, dy_ref, dxo_ref, gwpa_ref, gwpb_ref, gwpc_ref, gwo_ref):
        @pl.when(pl.program_id(0) == 0)
        def _():
            for ref in (gwpa_ref, gwpb_ref, gwpc_ref, gwo_ref):
                _zero(ref)

        gwo_ref[...] += _dot_tn(m_ref[...], dxo_ref[...].astype(BF16))
        for k, ref in enumerate((gwpa_ref, gwpb_ref, gwpc_ref)):
            ref[...] += _dot_tn(a_ref[:, k * SEG:(k + 1) * SEG], dy_ref[:, k * D:(k + 1) * D])

    row = lambda w: pl.BlockSpec((bk, w), lambda i: (i, 0))
    shapes = [(SEG, D), (SEG, D), (SEG, D), (D, D)]
    return _host_call(
        body, name, (T // bk,), riders,
        inputs=[acts, merged, dys, dxo],
        in_specs=[row(3 * SEG), row(D), row(3 * D), row(D)],
        out_specs=[_const_spec(s) for s in shapes], out_shape=[SDS(s, F32) for s in shapes],
        scratch_shapes=[])


def _inproj_token_blocks(T):
    return T // _pick(T, BWD_X_ROWS, 16)


def _inproj_bwd_x(dp, win_t, x, norm_g, dxo, name, riders=(), blocks=None, fill=None):
    T, D = x.shape
    N = dp.shape[1]
    bt = _pick(T, BWD_X_ROWS, 16)
    b0, nblk = blocks if blocks else (0, T // bt)

    def compute(dp_ref, w_ref, x_ref, g_ref, dxo_ref, *rest):
        dx_ref, dg_ref = rest[-2:]

        @pl.when(pl.program_id(0) == 0)
        def _():
            _zero(dg_ref)

        dh = _dot(dp_ref[...], w_ref[...])
        xv = x_ref[...]
        rstd = lax.rsqrt(jnp.mean(xv * xv, axis=-1, keepdims=True) + RMS_EPS)
        xhat = xv * rstd
        dg_ref[0:1, :] += jnp.sum(dh * xhat, axis=0, keepdims=True)
        dxn = dh * g_ref[...]
        dx_ref[...] = dxo_ref[...] + rstd * (dxn - xhat * jnp.mean(dxn * xhat, axis=-1, keepdims=True))

    rows = pl.BlockSpec((bt, D), lambda i: (i + b0, 0))
    return _host_call(
        compute, name, (nblk,), riders,
        inputs=[dp, win_t, x, norm_g.reshape(1, D), dxo] + ([] if fill is None else [fill]),
        in_specs=[pl.BlockSpec((bt, N), lambda i: (i + b0, 0)), _const_spec((N, D)), rows, _const_spec((1, D)), rows]
        + ([] if fill is None else [_ANY]),
        out_specs=[rows, _const_spec((8, D))],
        out_shape=[SDS((T, D), F32), SDS((8, D), F32)],
        scratch_shapes=[],
        aliases={} if fill is None else {5: 0})


def _inproj_bwd_w(dp, h, name, riders=()):
    T, N = dp.shape
    D = h.shape[1]
    bn = _pick(N, 1536, LANES)
    bk = _pick(T, WGRAD_ROWS, 16)
    nk = T // bk

    def compute(dp_ref, h_ref, o_ref):
        @pl.when(pl.program_id(1) == 0)
        def _():
            _zero(o_ref)

        o_ref[...] += _dot_tn(dp_ref[...], h_ref[...])

    return _host_call(
        compute, name, (N // bn, nk), riders,
        inputs=[dp, h],
        in_specs=[pl.BlockSpec((bk, bn), lambda j, k: (k, j)), pl.BlockSpec((bk, D), lambda j, k: (k, 0))],
        out_specs=[pl.BlockSpec((bn, D), lambda j, k: (j, 0))],
        out_shape=[SDS((N, D), F32)],
        scratch_shapes=[])


def _chip_peer(x, y, j):
    px = (1 - x) if (j >> 1) else x
    py = (1 - y) if (j & 1) else y
    return px, py


def _blk(ref, kind, k, n):
    if kind == "rows":
        return ref.at[pl.ds(pl.multiple_of(k * n, 8), n)]
    return ref.at[:, pl.ds(pl.multiple_of(k * n, LANES), n)]


class _Exchange:
    def __init__(self, srcs, out_shapes, n_sems, build, alias=None):
        self.srcs, self.out_shapes, self.n_sems, self.build = list(srcs), list(out_shapes), n_sems, build
        self.alias = dict(alias or {})


def _rider_aliases(riders, first_in, first_out):
    out, i, o = {}, first_in, first_out
    for e in riders:
        out.update({i + s: o + d for s, d in e.alias.items()})
        i, o = i + len(e.srcs), o + len(e.out_shapes)
    return out


def _rider_plan(riders):
    inputs = [s for e in riders for s in e.srcs]
    out_shapes = [o for e in riders for o in e.out_shapes]
    sems = [pltpu.SemaphoreType.DMA((e.n_sems,)) for e in riders for _ in range(2)]

    def copies(in_refs, out_refs, sem_refs):
        cps, i, o = [], 0, 0
        for k, e in enumerate(riders):
            ni, no = len(e.srcs), len(e.out_shapes)
            cps += e.build(in_refs[i:i + ni], out_refs[o:o + no], sem_refs[2 * k], sem_refs[2 * k + 1])
            i, o = i + ni, o + no
        return cps

    return inputs, out_shapes, sems, copies


_ANY = pl.BlockSpec(memory_space=pl.ANY)


def _host_call(compute, name, grid, riders, inputs, in_specs, out_specs, out_shape, scratch_shapes, aliases=None):
    r_in, r_out, r_sems, copies = _rider_plan(riders)
    ni, no, ns = len(inputs), len(out_shape), len(scratch_shapes)

    def body(*refs):
        ins, rins = refs[:ni], refs[ni:ni + len(r_in)]
        outs = refs[ni + len(r_in):ni + len(r_in) + no]
        routs = refs[ni + len(r_in) + no:ni + len(r_in) + no + len(r_out)]
        scr = refs[ni + len(r_in) + no + len(r_out):]
        first = functools.reduce(lambda a, b: a & b, [pl.program_id(d) == 0 for d in range(len(grid))])
        last = functools.reduce(lambda a, b: a & b, [pl.program_id(d) == grid[d] - 1 for d in range(len(grid))])
        if riders:
            @pl.when(first)
            def _():
                for cp in copies(rins, routs, scr[ns:]):
                    cp.start()

        compute(*ins, *outs, *scr[:ns])

        if riders:
            @pl.when(last)
            def _():
                for cp in copies(rins, routs, scr[ns:]):
                    cp.wait()

    res = pl.pallas_call(
        body, name=name, grid=grid,
        in_specs=list(in_specs) + [_ANY] * len(r_in),
        out_specs=list(out_specs) + [_ANY] * len(r_out),
        out_shape=list(out_shape) + r_out,
        scratch_shapes=list(scratch_shapes) + r_sems,
        input_output_aliases={**(aliases or {}), **_rider_aliases(riders, ni, no)},
        compiler_params=_cparams(("arbitrary",) * len(grid)),
    )(*inputs, *r_in)
    return res[:no], _split_riders(riders, res[no:])


def _split_riders(riders, flat):
    out, o = [], 0
    for e in riders:
        out.append(list(flat[o:o + len(e.out_shapes)]))
        o += len(e.out_shapes)
    return out


def _run_exchange(ex, name):
    n_in, n_out = len(ex.srcs), len(ex.out_shapes)

    def body(*refs):
        cps = ex.build(refs[:n_in], refs[n_in:n_in + n_out], refs[n_in + n_out], refs[n_in + n_out + 1])
        for cp in cps:
            cp.start()
        for cp in cps:
            cp.wait()

    return pl.pallas_call(
        body, name=name,
        in_specs=[_ANY] * n_in, out_specs=[_ANY] * n_out, out_shape=ex.out_shapes,
        input_output_aliases=ex.alias,
        scratch_shapes=[pltpu.SemaphoreType.DMA((ex.n_sems,)), pltpu.SemaphoreType.DMA((ex.n_sems,))],
        compiler_params=pltpu.CompilerParams(has_side_effects=True),
    )(*ex.srcs)


def _gather_sizes(shards, kinds):
    sizes = [s.shape[0] if k == "rows" else s.shape[1] for s, k in zip(shards, kinds)]
    fulls = [SDS((s.shape[0] * N_DEV,) + s.shape[1:], s.dtype) if k == "rows"
             else SDS((s.shape[0], s.shape[1] * N_DEV), s.dtype) for s, k in zip(shards, kinds)]
    return sizes, fulls


def _gather_direct(shards, kinds):
    n = len(shards)
    sizes, fulls = _gather_sizes(shards, kinds)

    def build(ins, outs, send_sems, recv_sems):
        x, y, c = _me()
        cps = []
        for a in range(n):
            mine = _blk(outs[a], kinds[a], 4 * x + 2 * y + c, sizes[a])
            cps.append(pltpu.make_async_copy(ins[a], mine, send_sems.at[5 * a + 4]))
            for j in range(N_CHIP):
                to = (x, y, 1 - c) if j == 0 else (*_chip_peer(x, y, j), c)
                cps.append(pltpu.make_async_remote_copy(
                    src_ref=ins[a], dst_ref=mine, send_sem=send_sems.at[5 * a + j], recv_sem=recv_sems.at[5 * a + j],
                    device_id=to, device_id_type=MESH))
        return cps

    return _Exchange(shards, fulls, 5 * n, build)


def _gather_everywhere(shards, kinds):
    n = len(shards)
    sizes, fulls = _gather_sizes(shards, kinds)

    def build(ins, outs, send_sems, recv_sems):
        x, y, c = _me()
        cps = []
        for a in range(n):
            mine = _blk(outs[a], kinds[a], 4 * x + 2 * y + c, sizes[a])
            cps.append(pltpu.make_async_copy(ins[a], mine, send_sems.at[N_DEV * a]))
            for d in range(1, N_DEV):
                to = ((1 - x) if d & 4 else x, (1 - y) if d & 2 else y, (1 - c) if d & 1 else c)
                cps.append(pltpu.make_async_remote_copy(
                    src_ref=ins[a], dst_ref=mine, send_sem=send_sems.at[N_DEV * a + d],
                    recv_sem=recv_sems.at[N_DEV * a + d], device_id=to, device_id_type=MESH))
        return cps

    return _Exchange(shards, fulls, N_DEV * n, build)


def _gather_forward(fulls, kinds, sizes):
    n = len(fulls)

    def build(ins, outs, send_sems, recv_sems):
        x, y, c = _me()
        cps = []
        for a in range(n):
            for j in (1, 2, 3):
                px, py = _chip_peer(x, y, j)
                k = 4 * px + 2 * py + c
                cps.append(pltpu.make_async_remote_copy(
                    src_ref=_blk(ins[a], kinds[a], k, sizes[a]), dst_ref=_blk(outs[a], kinds[a], k, sizes[a]),
                    send_sem=send_sems.at[3 * a + j - 1], recv_sem=recv_sems.at[3 * a + j - 1],
                    device_id=(x, y, 1 - c), device_id_type=MESH))
        return cps

    return _Exchange(fulls, [SDS(f.shape, f.dtype) for f in fulls], 3 * n, build, alias={a: a for a in range(n)})


def _sibling_exchange(grads, kinds, sizes):
    n = len(grads)

    def blk_shape(a):
        g = grads[a]
        return (sizes[a],) + g.shape[1:] if kinds[a] == "rows" else (g.shape[0], sizes[a])

    def build(ins, outs, send_sems, recv_sems):
        x, y, c = _me()
        cps = []
        for a in range(n):
            for q in range(N_CHIP):
                cps.append(pltpu.make_async_remote_copy(
                    src_ref=_blk(ins[a], kinds[a], 2 * q + (1 - c), sizes[a]), dst_ref=outs[a].at[q],
                    send_sem=send_sems.at[N_CHIP * a + q], recv_sem=recv_sems.at[N_CHIP * a + q],
                    device_id=(x, y, 1 - c), device_id_type=MESH))
        return cps

    return _Exchange(grads, [SDS((N_CHIP,) + blk_shape(a), F32) for a in range(n)], N_CHIP * n, build)


def _chip_partial(g, r1, kind, size, cidx, name):
    if kind == "rows":
        rows, cols = size, g.shape[1]
        g3 = g.reshape(N_DEV, rows, cols)
        rb = _pick(rows, 512, 16)
        g_spec = pl.BlockSpec((1, rb, cols), lambda q, j, c: (2 * q + c[0], j, 0))
        grid = (N_CHIP, rows // rb)
        blk = (1, rb, cols)
        imap = lambda q, j, c: (q, j, 0)
    else:
        rows, cols = g.shape[0], size
        g3 = g
        g_spec = pl.BlockSpec((rows, cols), lambda q, j, c: (0, 2 * q + c[0]))
        grid = (N_CHIP, 1)
        blk = (1, rows, cols)
        imap = lambda q, j, c: (q, 0, 0)

    def body(c_ref, g_ref, r_ref, p_ref, pb_ref):
        s = g_ref[...].reshape(blk) + r_ref[...]
        p_ref[...] = s
        pb_ref[...] = s.astype(BF16)

    return pl.pallas_call(
        body, name=name,
        grid_spec=pltpu.PrefetchScalarGridSpec(
            num_scalar_prefetch=1, grid=grid,
            in_specs=[g_spec, pl.BlockSpec(blk, imap)],
            out_specs=[pl.BlockSpec(blk, imap), pl.BlockSpec(blk, imap)]),
        out_shape=[SDS((N_CHIP, rows, cols), F32), SDS((N_CHIP, rows, cols), BF16)],
        compiler_params=_cparams(("arbitrary", "arbitrary")),
    )(cidx, g3, r1)


def _chip_exchange(parts):
    n = len(parts)
    m = N_CHIP - 1

    def build(ins, outs, send_sems, recv_sems):
        x, y, c = _me()
        cps = []
        for a in range(n):
            for j in (1, 2, 3):
                px, py = _chip_peer(x, y, j)
                cps.append(pltpu.make_async_remote_copy(
                    src_ref=ins[a].at[2 * px + py], dst_ref=outs[a].at[j - 1], send_sem=send_sems.at[m * a + j - 1],
                    recv_sem=recv_sems.at[m * a + j - 1], device_id=(px, py, c), device_id_type=MESH))
        return cps

    return _Exchange(parts, [SDS((m,) + p.shape[1:], BF16) for p in parts], m * n, build)


def _grad_total(part, r2, qidx, name):
    _, rows, cols = part.shape
    rb = _pick(rows, 512, 16)

    def body(q_ref, p_ref, r_ref, o_ref):
        s = p_ref[0]
        for j in range(N_CHIP - 1):
            s = s + r_ref[j].astype(F32)
        o_ref[...] = s

    return pl.pallas_call(
        body, name=name,
        grid_spec=pltpu.PrefetchScalarGridSpec(
            num_scalar_prefetch=1, grid=(rows // rb,),
            in_specs=[pl.BlockSpec((1, rb, cols), lambda i, q: (q[0], i, 0)),
                      pl.BlockSpec((N_CHIP - 1, rb, cols), lambda i, q: (0, i, 0))],
            out_specs=pl.BlockSpec((rb, cols), lambda i, q: (i, 0))),
        out_shape=SDS((rows, cols), F32),
        compiler_params=_cparams(("arbitrary",)),
    )(qidx, part, r2)


def _all_reduce_small(pack, name):
    rows = pack.shape[0]
    rs = rows // N_DEV
    assert rs * N_DEV == rows and rs % 8 == 0

    def body(x_ref, o_ref, rbuf, red, send1, recv1, send2, recv2):
        x, y, c = _me()
        me = 4 * x + 2 * y + c

        def peer(d):
            px = (1 - x) if (d >> 2) & 1 else x
            py = (1 - y) if (d >> 1) & 1 else y
            pc = (1 - c) if d & 1 else c
            return px, py, pc

        def sl(ref, k):
            return ref.at[pl.ds(pl.multiple_of(k * rs, 8), rs)]

        phase1 = []
        for d in range(1, N_DEV):
            px, py, pc = peer(d)
            phase1.append(pltpu.make_async_remote_copy(
                src_ref=sl(x_ref, 4 * px + 2 * py + pc), dst_ref=rbuf.at[d], send_sem=send1.at[d], recv_sem=recv1.at[d],
                device_id=(px, py, pc), device_id_type=MESH))
        for cp in phase1:
            cp.start()
        acc = sl(x_ref, me)[...]
        for cp in phase1:
            cp.wait()
        for d in range(1, N_DEV):
            acc = acc + rbuf[d]
        red[...] = acc
        sl(o_ref, me)[...] = acc
        phase2 = []
        for d in range(1, N_DEV):
            px, py, pc = peer(d)
            phase2.append(pltpu.make_async_remote_copy(
                src_ref=red, dst_ref=sl(o_ref, me), send_sem=send2.at[d], recv_sem=recv2.at[d],
                device_id=(px, py, pc), device_id_type=MESH))
        for cp in phase2:
            cp.start()
        for cp in phase2:
            cp.wait()

    vm = pl.BlockSpec(memory_space=pltpu.VMEM)
    return pl.pallas_call(
        body, name=name, in_specs=[vm], out_specs=vm, out_shape=SDS(pack.shape, F32),
        scratch_shapes=[pltpu.VMEM((N_DEV, rs, LANES), F32), pltpu.VMEM((rs, LANES), F32),
                        pltpu.SemaphoreType.DMA((N_DEV,)), pltpu.SemaphoreType.DMA((N_DEV,)),
                        pltpu.SemaphoreType.DMA((N_DEV,)), pltpu.SemaphoreType.DMA((N_DEV,))],
        compiler_params=_cparams(None, has_side_effects=True),
    )(pack)


def _adamw(w, g, m, v, name):
    rows, cols = w.shape
    rb = rows if rows * cols * 4 <= ADAMW_WHOLE_BYTES else _pick(rows, 256, 8)
    c1 = 1.0 / (1.0 - ADAM_B1 ** ADAM_STEP)
    c2 = 1.0 / (1.0 - ADAM_B2 ** ADAM_STEP)

    def body(w_ref, g_ref, m_ref, v_ref, d_ref, mo_ref, vo_ref):
        gv = g_ref[...]
        mn = ADAM_B1 * m_ref[...] + (1.0 - ADAM_B1) * gv
        vn = ADAM_B2 * v_ref[...] + (1.0 - ADAM_B2) * (gv * gv)
        mo_ref[...] = mn
        vo_ref[...] = vn
        d_ref[...] = -ADAM_LR * ((mn * c1) / (jnp.sqrt(vn * c2) + ADAM_EPS) + ADAM_WD * w_ref[...])

    spec = pl.BlockSpec((rb, cols), lambda i: (i, 0))
    return pl.pallas_call(
        body, name=name, grid=(rows // rb,),
        in_specs=[spec] * 4, out_specs=[spec] * 3, out_shape=[SDS((rows, cols), F32)] * 3,
        compiler_params=_cparams(("arbitrary",)),
    )(w, g, m, v)


def _pad_rows(a, mult=8):
    r = (-a.shape[0]) % mult
    return a if r == 0 else jnp.pad(a, ((0, r), (0, 0)))


def _as_lanes(a):
    flat = a.reshape(-1)
    pad = (-flat.shape[0]) % (8 * LANES)
    if pad:
        flat = jnp.pad(flat, (0, pad))
    return flat.reshape(-1, LANES)


def kernel(x, norm_g, w_in, ln_g, ln_b, w_s, b_s, conv_w, conv_b, w_pool, pool_scale, w_pa, w_pb, w_pc, w_o, final_g, loss_target, m_norm_g, m_w_in, m_ln_g, m_ln_b, m_w_s, m_b_s, m_conv_w, m_conv_b, m_w_pool, m_pool_scale, m_w_pa, m_w_pb, m_w_pc, m_w_o, m_final_g, v_norm_g, v_w_in, v_ln_g, v_ln_b, v_w_s, v_b_s, v_conv_w, v_conv_b, v_w_pool, v_pool_scale, v_w_pa, v_w_pb, v_w_pc, v_w_o, v_final_g):
    L = w_in.shape[0]
    D = x.shape[-1]
    n_loc = w_in.shape[2]
    pc_loc = w_pa.shape[2]
    x0 = x[0]
    target = loss_target[0]
    xi, yi, ci = _me()
    cidx = jnp.reshape(ci, (1,)).astype(jnp.int32)
    qidx = jnp.reshape(2 * xi + yi, (1,)).astype(jnp.int32)

    kinds5 = ["rows", "cols", "cols", "cols", "rows"]

    def layer_shards(l):
        return [w_in[l].T.astype(BF16), w_pa[l].astype(BF16), w_pb[l].astype(BF16), w_pc[l].astype(BF16),
                w_o[l].astype(BF16)]

    def gathered(direct, shards, kinds, l):
        sizes, _ = _gather_sizes(shards, kinds)
        return _run_exchange(_gather_forward(direct, kinds, sizes), f"weights_forward_{l}")

    cw_loc = _pad_rows(conv_w.reshape(L * CONV_TAPS, -1))
    cw_loc = jnp.pad(cw_loc, ((0, 0), (0, LANES - cw_loc.shape[1])))
    causal = jnp.tril(jnp.ones((CHUNK, CHUNK), dtype=bool))

    chip_order = jnp.stack([2 * xi + yi] + [2 * px + py for px, py in (_chip_peer(xi, yi, j) for j in (1, 2, 3))])
    sh0 = layer_shards(0)
    rest0, krest0 = sh0[1:] + [cw_loc], kinds5[1:] + ["rows"]
    (p0, h0, win_t0), delivered = _inproj_gathering(x0, norm_g[0], sh0[0], chip_order.astype(jnp.int32), "inproj_fwd_0",
                                                    [_gather_everywhere(rest0, krest0)])
    rest0_full = delivered[0]
    cw_all = rest0_full[-1].reshape(N_DEV, -1, LANES)[:, :L * CONV_TAPS, :conv_w.shape[2]]
    conv_w_full = jnp.transpose(cw_all, (1, 0, 2)).reshape(L, CONV_TAPS, -1)

    def make_layer(l, full5):
        win_t, wpa, wpb, wpc, wo = full5
        wm = jnp.where(causal, w_s[l], 0.0)
        cvec = jnp.concatenate([ln_g[l][None], ln_b[l][None], conv_w_full[l], conv_b[l][None], pool_scale[l][None],
                                jnp.zeros((C_ROWS - 7, SEG), F32)], axis=0)
        return dict(
            win_t=win_t, wpa=wpa, wpb=wpb, wpc=wpc, wo=wo, cvec=cvec,
            bsb=jnp.repeat(b_s[l].T, HEAD, axis=1),
            wcat=jnp.transpose(wm, (1, 0, 2)).reshape(CHUNK, GROUPS * CHUNK).astype(BF16),
            wcatt=jnp.transpose(wm, (2, 0, 1)).reshape(CHUNK, GROUPS * CHUNK).astype(BF16),
            wpool=w_pool[l].astype(BF16))

    layers, xs, saved, win_next = [], [x0], [], None
    for l in range(L):
        if l == 0:
            p, h = p0, h0
            lw = make_layer(0, [win_t0] + list(rest0_full[:4]))
        else:
            (p, h), delivered = _inproj(xs[-1], norm_g[l], win_next, f"inproj_fwd_{l}",
                                        [_gather_everywhere(layer_shards(l)[1:], kinds5[1:])])
            lw = make_layer(l, [win_next] + delivered[0])
        layers.append(lw)
        nxt = layer_shards(l + 1)[:1] if l + 1 < L else None
        (xn, ya, yb, yc), delivered = _mix_fwd(p, xs[-1], lw, f"mix_fwd_{l}", [_gather_direct(nxt, kinds5[:1])] if nxt else [])
        if nxt:
            win_next = gathered(delivered[0], nxt, kinds5[:1], l + 1)[0]
        saved.append((p, h, ya, yb, yc))
        xs.append(xn)
    dx, loss_acc, dfg_acc = _loss_head(xs[-1], final_g, target, "loss_head")

    rs_sizes = [n_loc, pc_loc, pc_loc, pc_loc, w_o.shape[1]]
    await_sibling, await_chips = [], []
    partial_of, from_chips = {}, {}
    serial = [0]

    def riders_now():
        riders, plan = [], []
        for grp in await_chips:
            riders.append(_chip_exchange([partial_of[t][1] for t, _, _, _ in grp]))
            plan.append(("chips", grp))
        for grp in await_sibling:
            riders.append(_sibling_exchange([g for _, g, _, _ in grp], [k for _, _, k, _ in grp], [s for _, _, _, s in grp]))
            plan.append(("sibling", grp))
        del await_chips[:], await_sibling[:]
        return riders, plan

    def absorb(plan, delivered):
        for (what, grp), res in zip(plan, delivered):
            for (t, g, k, s), r in zip(grp, res):
                if what == "chips":
                    from_chips[t] = r
                else:
                    partial_of[t] = _chip_partial(g, r, k, s, cidx, f"grad_chip_partial_{t[0]}_{t[1]}")
            if what == "sibling":
                await_chips.append(grp)

    small = [None] * L
    for l in reversed(range(L)):
        lw = layers[l]
        p, h, ya, yb, yc = saved[l]
        dp, acts, merged, dys, gwc, gbs, gwpool, gvec = _mix_bwd(p, dx, ya, yb, yc, lw, f"mix_bwd_{l}")
        riders, plan = riders_now()
        (gwpa, gwpb, gwpc, gwo), delivered = _proj_wgrad(acts, merged, dys, dx, f"proj_wgrad_{l}", riders)
        absorb(plan, delivered)
        await_sibling.append([((l, a), g, kinds5[a], rs_sizes[a]) for a, g in ((1, gwpa), (2, gwpb), (3, gwpc), (4, gwo))])

        def bwd_x(dxo, pieces):
            nt = _inproj_token_blocks(dxo.shape[0])
            pieces = min(pieces, nt)
            done, dng, b0 = None, None, 0
            for k in range(pieces):
                cnt = (nt - b0) if k == pieces - 1 else max(1, (nt * LAST_PIECE_SHARE[0]) // (LAST_PIECE_SHARE[1] * (pieces - 1)))
                riders, plan = riders_now()
                (done, dng_k), delivered = _inproj_bwd_x(dp, lw["win_t"], xs[l], norm_g[l], dxo, f"inproj_bwd_x_{l}_{k}",
                                                         riders, blocks=(b0, cnt), fill=done)
                absorb(plan, delivered)
                dng = dng_k if dng is None else dng + dng_k
                b0 += cnt
            return done, dng

        def bwd_w():
            riders, plan = riders_now()
            (gwin_t,), delivered = _inproj_bwd_w(dp, h, f"inproj_bwd_w_{l}", riders)
            absorb(plan, delivered)
            await_sibling.append([((l, 0), gwin_t, kinds5[0], rs_sizes[0])])

        if l == L - 1:
            dx, dng = bwd_x(dx, 1)
            bwd_w()
        else:
            bwd_w()
            dx, dng = bwd_x(dx, 2 if l == 0 else 1)
        small[l] = dict(norm_g=dng[0], ln_g=gvec[V_LNG], ln_b=gvec[V_LNB], w_s=gwc, b_s=gbs, conv_w=gvec[V_CW0:V_CW0 + 3],
                        conv_b=gvec[V_CB], w_pool=gwpool, pool_scale=gvec[V_PS])
    while await_sibling or await_chips:
        riders, plan = riders_now()
        delivered = []
        for ex in riders:
            delivered.append(_run_exchange(ex, f"grad_exchange_tail_{serial[0]}"))
            serial[0] += 1
        absorb(plan, delivered)
    grad_x = dx[None]
    big_grads = []
    for l in range(L):
        tot = [_grad_total(partial_of[(l, a)][0], from_chips[(l, a)], qidx, f"grad_total_{l}_{a}") for a in range(5)]
        big_grads.append([tot[0],
                          tot[1].reshape(SEG, pc_loc), tot[2].reshape(SEG, pc_loc), tot[3].reshape(SEG, pc_loc),
                          tot[4]])

    names = ["norm_g", "ln_g", "ln_b", "w_s", "b_s", "conv_w", "conv_b", "w_pool", "pool_scale"]
    pieces = [jnp.stack([small[l][nm] for l in range(L)]).reshape(-1) for nm in names]
    pieces += [dfg_acc[0], loss_acc.reshape(-1)]
    offs = [0]
    for pc in pieces:
        offs.append(offs[-1] + pc.shape[0])
    pack = _pad_rows(_as_lanes(jnp.concatenate(pieces)), 8 * N_DEV)
    red = _all_reduce_small(pack, "small_grads_all_reduce").reshape(-1)

    def unpack(i, shape):
        return red[offs[i]:offs[i] + math.prod(shape)].reshape(shape)

    g_norm_g = unpack(0, (L, D))
    g_ln_g = unpack(1, (L, SEG))
    g_ln_b = unpack(2, (L, SEG))
    g_w_s = unpack(3, (L, GROUPS, CHUNK, CHUNK))
    g_b_s = jnp.transpose(unpack(4, (L, CHUNK, LANES))[:, :, :GROUPS], (0, 2, 1))
    g_conv_w_full = unpack(5, (L, CONV_TAPS, SEG))
    g_conv_b = unpack(6, (L, SEG))
    g_w_pool = unpack(7, (L, len(POOL_WINDOWS), POOL_GROUP, POOL_GROUP))
    g_pool_scale = unpack(8, (L, SEG))
    g_final_g = unpack(9, (D,))
    loss = red[offs[10]]
    dev = 4 * xi + 2 * yi + ci
    g_conv_w = lax.dynamic_slice_in_dim(g_conv_w_full, dev * conv_w.shape[2], conv_w.shape[2], axis=2)

    g_w_in_t = jnp.stack([big_grads[l][0] for l in range(L)])
    g_w_in = jnp.swapaxes(g_w_in_t, 1, 2)
    g_w_pa = jnp.stack([big_grads[l][1] for l in range(L)])
    g_w_pb = jnp.stack([big_grads[l][2] for l in range(L)])
    g_w_pc = jnp.stack([big_grads[l][3] for l in range(L)])
    g_w_o = jnp.stack([big_grads[l][4] for l in range(L)])

    grads = dict(norm_g=g_norm_g, w_in=g_w_in, ln_g=g_ln_g, ln_b=g_ln_b, w_s=g_w_s, b_s=g_b_s, conv_w=g_conv_w,
                 conv_b=g_conv_b, w_pool=g_w_pool, pool_scale=g_pool_scale, w_pa=g_w_pa, w_pb=g_w_pb, w_pc=g_w_pc,
                 w_o=g_w_o, final_g=g_final_g)
    weights = dict(norm_g=norm_g, w_in=w_in, ln_g=ln_g, ln_b=ln_b, w_s=w_s, b_s=b_s, conv_w=conv_w, conv_b=conv_b,
                   w_pool=w_pool, pool_scale=pool_scale, w_pa=w_pa, w_pb=w_pb, w_pc=w_pc, w_o=w_o, final_g=final_g)
    ms = dict(norm_g=m_norm_g, w_in=m_w_in, ln_g=m_ln_g, ln_b=m_ln_b, w_s=m_w_s, b_s=m_b_s, conv_w=m_conv_w,
              conv_b=m_conv_b, w_pool=m_w_pool, pool_scale=m_pool_scale, w_pa=m_w_pa, w_pb=m_w_pb, w_pc=m_w_pc,
              w_o=m_w_o, final_g=m_final_g)
    vs = dict(norm_g=v_norm_g, w_in=v_w_in, ln_g=v_ln_g, ln_b=v_ln_b, w_s=v_w_s, b_s=v_b_s, conv_w=v_conv_w,
              conv_b=v_conv_b, w_pool=v_w_pool, pool_scale=v_pool_scale, w_pa=v_w_pa, w_pb=v_w_pb, w_pc=v_w_pc,
              w_o=v_w_o, final_g=v_final_g)
    order = ["norm_g", "w_in", "ln_g", "ln_b", "w_s", "b_s", "conv_w", "conv_b", "w_pool", "pool_scale", "w_pa", "w_pb",
             "w_pc", "w_o", "final_g"]

    delta, new_m, new_v = {}, {}, {}
    big = ["w_in", "w_pa", "w_pb", "w_pc", "w_o"]
    tr = lambda a: jnp.swapaxes(a, 1, 2)
    for nm in big:
        into = tr if nm == "w_in" else (lambda a: a)
        shp = into(weights[nm]).shape
        two = lambda a: a.reshape(-1, shp[-1])
        g2 = two(g_w_in_t) if nm == "w_in" else two(grads[nm])
        d, mn, vn = _adamw(two(into(weights[nm])), g2, two(into(ms[nm])), two(into(vs[nm])), f"adamw_{nm}")
        delta[nm], new_m[nm], new_v[nm] = (into(a.reshape(shp)) for a in (d, mn, vn))
    rest = [nm for nm in order if nm not in big and nm != "conv_w"] + ["conv_w"]
    cat = lambda src: _as_lanes(jnp.concatenate([src[nm].reshape(-1) for nm in rest]))
    flat = [a.reshape(-1) for a in _adamw(cat(weights), cat(grads), cat(ms), cat(vs), "adamw_small")]
    off = 0
    for nm in rest:
        shp = weights[nm].shape
        n = math.prod(shp)
        delta[nm], new_m[nm], new_v[nm] = (a[off:off + n].reshape(shp) for a in flat)
        off += n

    return (loss, grad_x, *[grads[nm] for nm in order], *[delta[nm] for nm in order],
            *[new_m[nm] for nm in order], *[new_v[nm] for nm in order])
```

```python
import functools
import math

import numpy as np
import jax
import jax.numpy as jnp
from jax import lax
from jax.experimental import pallas as pl
from jax.experimental.pallas import tpu as pltpu

F32 = jnp.float32
BF16 = jnp.bfloat16
SDS = jax.ShapeDtypeStruct
MESH = pl.DeviceIdType.MESH

SEG = 512
CHUNK = 128
GROUPS = 8
HEAD = SEG // GROUPS
POOL_WINDOWS = (2, 4, 8, 16)
POOL_GROUP = SEG // len(POOL_WINDOWS)
CONV_TAPS = 3
HALO = 16
RMS_EPS = 1e-6
LN_EPS = 1e-5
ADAM_LR, ADAM_B1, ADAM_B2, ADAM_EPS, ADAM_WD, ADAM_STEP = 0.001, 0.9, 0.999, 1e-08, 0.01, 10

O_U, O_V, O_ZA, O_XB, O_BG, O_CG, O_ZB, O_XC, O_ZC, O_G = (SEG * i for i in range(10))

N_DEV = 8
N_CHIP = 4
LANES = 128
VMEM_LIMIT = 48 * 1024 * 1024
ADAMW_WHOLE_BYTES = 2 * 1024 * 1024
INPROJ_ROWS = 1024
BWD_X_ROWS = 512
WGRAD_ROWS = 2048
RIDERS_FROM_CHUNK = 2
LAST_PIECE_SHARE = (5, 16)


def _cparams(sem=None, **kw):
    return pltpu.CompilerParams(dimension_semantics=sem, vmem_limit_bytes=VMEM_LIMIT, **kw)


def _pick(total, target, mult):
    best = None
    for d in range(mult, min(total, target) + 1, mult):
        if total % d == 0:
            best = d
    assert best is not None, (total, target, mult)
    return best


def _dot(a, b):
    return jnp.dot(a, b, preferred_element_type=F32)


def _dot_nt(a, b):
    return lax.dot_general(a, b, (((1,), (1,)), ((), ())), preferred_element_type=F32)


def _dot_tn(a, b):
    return lax.dot_general(a, b, (((0,), (0,)), ((), ())), preferred_element_type=F32)


def _zero(ref):
    ref[...] = jnp.zeros(ref.shape, ref.dtype)


def _sigmoid(x):
    return 1.0 / (1.0 + jnp.exp(-x))


_GELU_C = math.sqrt(2.0 / math.pi)


def _gelu(x):
    t = jnp.tanh(_GELU_C * (x + 0.044715 * x * x * x))
    return 0.5 * x * (1.0 + t), t


def _gelu_grad(x, t):
    return 0.5 * (1.0 + t) + 0.5 * x * (1.0 - t * t) * _GELU_C * (1.0 + 3.0 * 0.044715 * x * x)


def _me():
    return lax.axis_index("x"), lax.axis_index("y"), lax.axis_index("c")


def _inproj(x, norm_g, win_t, name, riders=()):
    T, D = x.shape
    N = win_t.shape[0]
    bt = _pick(T, INPROJ_ROWS, 16)
    bn = _pick(N, 1536, LANES)
    grid = (T // bt, N // bn)

    def compute(x_ref, g_ref, w_ref, p_ref, h_ref, hs_ref):
        @pl.when(pl.program_id(1) == 0)
        def _():
            xv = x_ref[...]
            rstd = lax.rsqrt(jnp.mean(xv * xv, axis=-1, keepdims=True) + RMS_EPS)
            hb = (xv * rstd * g_ref[...]).astype(BF16)
            hs_ref[...] = hb
            h_ref[...] = hb

        p_ref[...] = _dot_nt(hs_ref[...], w_ref[...]).astype(BF16)

    return _host_call(
        compute, name, grid, riders,
        inputs=[x, norm_g.reshape(1, D), win_t],
        in_specs=[pl.BlockSpec((bt, D), lambda i, j: (i, 0)),
                  pl.BlockSpec((1, D), lambda i, j: (0, 0)),
                  pl.BlockSpec((bn, D), lambda i, j: (j, 0))],
        out_specs=[pl.BlockSpec((bt, bn), lambda i, j: (i, j)),
                   pl.BlockSpec((bt, D), lambda i, j: (i, 0))],
        out_shape=[SDS((T, N), BF16), SDS((T, D), BF16)],
        scratch_shapes=[pltpu.VMEM((bt, D), BF16)])


def _inproj_gathering(x, norm_g, w_loc, chip_order, name, riders=()):
    T, D = x.shape
    n = w_loc.shape[0]
    N = n * N_DEV
    cw = 2 * n
    bt = _pick(T, INPROJ_ROWS, 16)
    nt = T // bt
    r_in, r_out, r_sems, copies = _rider_plan(riders)
    n_rin, n_rout = len(r_in), len(r_out)

    def body(q_ref, x_ref, g_ref, wloc_ref, *rest):
        rins = rest[:n_rin]
        p_ref, h_ref, wfull_ref = rest[n_rin:n_rin + 3]
        routs = rest[n_rin + 3:n_rin + 3 + n_rout]
        hs_ref, wbuf, send_sems, recv_sems, loc_sems = rest[n_rin + 3 + n_rout:n_rin + 8 + n_rout]
        rsems = rest[n_rin + 8 + n_rout:]
        j, i = pl.program_id(0), pl.program_id(1)
        cx, cy, cc = _me()
        sibling = (cx, cy, 1 - cc)

        def rows(k):
            return wfull_ref.at[pl.ds(pl.multiple_of(k * n, 8), n)]

        def shard_copy(slot, src, k, to):
            return pltpu.make_async_remote_copy(src_ref=src, dst_ref=rows(k), send_sem=send_sems.at[slot],
                                                recv_sem=recv_sems.at[slot], device_id=to, device_id_type=MESH)

        me = 4 * cx + 2 * cy + cc
        place_mine = pltpu.make_async_copy(wloc_ref, rows(me), loc_sems.at[0])
        sends = [shard_copy(0, wloc_ref, me, sibling)]
        for jj in (1, 2, 3):
            sends.append(shard_copy(jj, wloc_ref, me, (*_chip_peer(cx, cy, jj), cc)))

        def forward(jj):
            px, py = _chip_peer(cx, cy, jj)
            k = 4 * px + 2 * py + cc
            return shard_copy(3 + jj, rows(k), k, sibling)

        def load_chunk(q):
            cp = pltpu.make_async_copy(wfull_ref.at[pl.ds(pl.multiple_of(q * cw, 8), cw)], wbuf, loc_sems.at[1])
            cp.start()
            cp.wait()

        keep_h = pltpu.make_async_copy(hs_ref, h_ref, loc_sems.at[2])

        @pl.when((j == 0) & (i == 0))
        def _():
            place_mine.start()
            for cp in sends:
                cp.start()
            place_mine.wait()
            sends[0].wait_recv()
            load_chunk(q_ref[0])

        for jj in (1, 2, 3):
            @pl.when((j == jj) & (i == 0))
            def _(jj=jj):
                sends[jj].wait_recv()
                fwd = forward(jj)
                fwd.start()
                fwd.wait_recv()
                load_chunk(q_ref[jj])
                if jj == RIDERS_FROM_CHUNK:
                    for cp in copies(rins, routs, rsems):
                        cp.start()

        tok = pl.ds(pl.multiple_of(i * bt, bt), bt)

        @pl.when(j == 0)
        def _():
            xv = x_ref[...]
            rstd = lax.rsqrt(jnp.mean(xv * xv, axis=-1, keepdims=True) + RMS_EPS)
            hs_ref[tok, :] = (xv * rstd * g_ref[...]).astype(BF16)

        @pl.when((j == 0) & (i == nt - 1))
        def _():
            keep_h.start()

        p_ref[...] = _dot_nt(hs_ref[tok, :], wbuf[...]).astype(BF16)

        @pl.when((j == N_CHIP - 1) & (i == nt - 1))
        def _():
            keep_h.wait()
            for cp in sends:
                cp.wait_send()
            for jj in (1, 2, 3):
                forward(jj).wait_send()
            for cp in copies(rins, routs, rsems):
                cp.wait()

    res = pl.pallas_call(
        body, name=name,
        grid_spec=pltpu.PrefetchScalarGridSpec(
            num_scalar_prefetch=1, grid=(N_CHIP, nt),
            in_specs=[pl.BlockSpec((bt, D), lambda j, i, q: (jnp.where(j == 0, i, nt - 1), 0)),
                      pl.BlockSpec((1, D), lambda j, i, q: (0, 0)), _ANY] + [_ANY] * n_rin,
            out_specs=[pl.BlockSpec((bt, cw), lambda j, i, q: (i, q[j])), _ANY, _ANY] + [_ANY] * n_rout,
            scratch_shapes=[pltpu.VMEM((T, D), BF16), pltpu.VMEM((cw, D), BF16), pltpu.SemaphoreType.DMA((7,)),
                            pltpu.SemaphoreType.DMA((7,)), pltpu.SemaphoreType.DMA((3,))] + r_sems),
        out_shape=[SDS((T, N), BF16), SDS((T, D), BF16), SDS((N, D), BF16)] + r_out,
        input_output_aliases=_rider_aliases(riders, 4, 3),
        compiler_params=_cparams(("arbitrary", "arbitrary")),
    )(chip_order, x, norm_g.reshape(1, D), w_loc, *r_in)
    return res[:3], _split_riders(riders, res[3:])


C_LNG, C_LNB, C_CW0, C_CW1, C_CW2, C_CB, C_PS = range(7)
C_ROWS = 8


def _pool_bands(R, anticausal):
    t = np.arange(R)[:, None]
    s = np.arange(R + CHUNK)[None, :]
    bands = [((s >= t) & (s < t + w)) if anticausal else ((s > t + CHUNK - w) & (s <= t + CHUNK)) for w in POOL_WINDOWS]
    return jnp.asarray(np.stack(bands), dtype=BF16)


def _mixers(p_ref, hxb_ref, hcg_ref, hxc_ref, cv, bsb_ref, wcat_ref, wpool_ref, band_ref, extb,
            first, blk, R, need_grad):
    def seg(lo):
        return p_ref[:, lo:lo + SEG].astype(F32)

    u, v, za = seg(O_U), seg(O_V), seg(O_ZA)
    xb, bg, cg, zb = seg(O_XB), seg(O_BG), seg(O_CG), seg(O_ZB)
    xc, zc = seg(O_XC), seg(O_ZC)
    out = {}

    ug, tu = _gelu(u)
    vg, tv = _gelu(v)
    mu = jnp.mean(vg, axis=-1, keepdims=True)
    vcen = vg - mu
    rs = lax.rsqrt(jnp.mean(vcen * vcen, axis=-1, keepdims=True) + LN_EPS)
    vhat = vcen * rs
    vn = (vhat * cv[C_LNG:C_LNG + 1, :] + cv[C_LNB:C_LNB + 1, :]).astype(BF16)
    lane_group = lax.broadcasted_iota(jnp.int32, (CHUNK, SEG), 1) // HEAD
    zero_b = jnp.zeros((CHUNK, SEG), BF16)
    sgs = []
    for ci in range(R // CHUNK):
        vc = vn[ci * CHUNK:(ci + 1) * CHUNK]
        vst = jnp.concatenate([jnp.where(lane_group == g, vc, zero_b) for g in range(GROUPS)], axis=0)
        sgs.append(_dot(wcat_ref[...], vst) + bsb_ref[...])
    sg = sgs[0] if len(sgs) == 1 else jnp.concatenate(sgs, axis=0)
    a_out = ug * sg
    sa = _sigmoid(za)
    out["a"] = a_out * (za * sa)

    cx = cg * xb
    halo_b = hcg_ref[...].astype(F32) * hxb_ref[...].astype(F32)
    extb[0:HALO, :] = jnp.where(first, 0.0, halo_b)
    extb[HALO:HALO + R, :] = cx
    cx1 = extb[pl.ds(HALO - 1, R), :]
    cx2 = extb[pl.ds(HALO - 2, R), :]
    yconv = (cv[C_CW0:C_CW0 + 1, :] * cx2 + cv[C_CW1:C_CW1 + 1, :] * cx1
             + cv[C_CW2:C_CW2 + 1, :] * cx + cv[C_CB:C_CB + 1, :])
    b_out = bg * yconv
    sb = _sigmoid(zb)
    out["b"] = b_out * (zb * sb)

    halo_c = hxc_ref[...]
    xc_ext = jnp.concatenate([jnp.zeros((CHUNK - HALO, SEG), BF16), jnp.where(first, jnp.zeros_like(halo_c), halo_c),
                              p_ref[:, O_XC:O_XC + SEG]], axis=0)
    tpos = blk * R + lax.broadcasted_iota(jnp.int32, (R, POOL_GROUP), 0) + 1
    pooled, invs, pws = [], [], []
    for gi, w in enumerate(POOL_WINDOWS):
        lo = gi * POOL_GROUP
        win = _dot(band_ref[gi], xc_ext[:, lo:lo + POOL_GROUP])
        inv = 1.0 / jnp.minimum(tpos, w).astype(F32)
        pg = (win * inv - xc[:, lo:lo + POOL_GROUP]).astype(BF16)
        pooled.append(pg)
        invs.append(inv)
        pws.append(_dot(pg, wpool_ref[gi]))
    pw = jnp.concatenate(pws, axis=1)
    c_out = pw * cv[C_PS:C_PS + 1, :]
    sc = _sigmoid(zc)
    out["c"] = c_out * (zc * sc)

    if need_grad:
        out.update(u=u, v=v, tu=tu, tv=tv, ug=ug, sg=sg, a_out=a_out, za=za, sa=sa,
                   rs=rs, vhat=vhat, vn=vn, lane_group=lane_group, zero_b=zero_b,
                   xb=xb, bg=bg, cg=cg, cx=cx, cx1=cx1, cx2=cx2, yconv=yconv, b_out=b_out, zb=zb, sb=sb,
                   pooled=pooled, invs=invs, pw=pw, c_out=c_out, zc=zc, sc=sc)
    return out


def _halo_specs(R, nb, rev):
    step = R // HALO

    def mk(col):
        def imap(i):
            b = (nb - 1 - i) if rev else i
            return (jnp.maximum(b * step - 1, 0), col)
        return pl.BlockSpec((HALO, SEG), imap)

    return [mk(O_XB // SEG), mk(O_CG // SEG), mk(O_XC // SEG)]


def _const_spec(shape):
    nd = len(shape)
    return pl.BlockSpec(shape, lambda i: (0,) * nd, pipeline_mode=pl.Buffered(1))


MIX_FWD_ROWS = 512
MIX_BWD_ROWS = 256


def _mix_block_rows(T, target):
    return _pick(T, target, CHUNK)


def _mix_fwd(p, x, lw, name, riders=()):
    T, D = x.shape
    N = p.shape[1]
    R = _mix_block_rows(T, MIX_FWD_ROWS)
    nb = T // R

    def body(p_ref, hxb, hcg, hxc, x_ref, cv_ref, bsb_ref, wcat_ref, wpool_ref, band_ref, wpa_ref, wpb_ref, wpc_ref,
             wo_ref, xo_ref, ya_ref, yb_ref, yc_ref, extb):
        i = pl.program_id(0)
        cv = cv_ref[...]
        r = _mixers(p_ref, hxb, hcg, hxc, cv, bsb_ref, wcat_ref, wpool_ref, band_ref, extb,
                    i == 0, i, R, False)
        merged = None
        for k, (act, w_ref, y_ref) in enumerate(((r["a"], wpa_ref, ya_ref), (r["b"], wpb_ref, yb_ref),
                                                 (r["c"], wpc_ref, yc_ref))):
            y = _dot(act.astype(BF16), w_ref[...]).astype(BF16)
            y_ref[...] = y
            term = _sigmoid(p_ref[:, O_G + k * D:O_G + (k + 1) * D]) * y
            merged = term if merged is None else merged + term
        xo_ref[...] = x_ref[...] + _dot(merged, wo_ref[...])

    row = lambda w: pl.BlockSpec((R, w), lambda i: (i, 0))
    consts = [lw["cvec"], lw["bsb"], lw["wcat"], lw["wpool"], _pool_bands(R, False), lw["wpa"], lw["wpb"], lw["wpc"],
              lw["wo"]]
    return _host_call(
        body, name, (nb,), riders,
        inputs=[p, p, p, p, x, *consts],
        in_specs=[row(N)] + _halo_specs(R, nb, False) + [row(D)] + [_const_spec(c.shape) for c in consts],
        out_specs=[row(D), row(D), row(D), row(D)],
        out_shape=[SDS((T, D), F32), SDS((T, D), BF16), SDS((T, D), BF16), SDS((T, D), BF16)],
        scratch_shapes=[pltpu.VMEM((HALO + R, SEG), F32)])


def _loss_head(x, final_g, target, name):
    T, D = x.shape
    bt = _pick(T, INPROJ_ROWS, 8)

    def body(x_ref, g_ref, t_ref, dx_ref, loss_ref, dg_ref):
        @pl.when(pl.program_id(0) == 0)
        def _():
            _zero(loss_ref)
            _zero(dg_ref)

        xv = x_ref[...]
        g = g_ref[...]
        rstd = lax.rsqrt(jnp.mean(xv * xv, axis=-1, keepdims=True) + RMS_EPS)
        xhat = xv * rstd
        err = xhat * g - t_ref[...]
        part = 0.5 * jnp.sum(jnp.sum(err * err, axis=-1, keepdims=True), axis=0, keepdims=True) / D
        loss_ref[...] += jnp.broadcast_to(part, loss_ref.shape)
        dy = err * (1.0 / D)
        dg_ref[0:1, :] += jnp.sum(dy * xhat, axis=0, keepdims=True)
        dxn = dy * g
        dx_ref[...] = rstd * (dxn - xhat * jnp.mean(dxn * xhat, axis=-1, keepdims=True))

    return pl.pallas_call(
        body, name=name, grid=(T // bt,),
        in_specs=[pl.BlockSpec((bt, D), lambda i: (i, 0)), _const_spec((1, D)), pl.BlockSpec((bt, D), lambda i: (i, 0))],
        out_specs=[pl.BlockSpec((bt, D), lambda i: (i, 0)), _const_spec((8, LANES)), _const_spec((8, D))],
        out_shape=[SDS((T, D), F32), SDS((8, LANES), F32), SDS((8, D), F32)],
        compiler_params=_cparams(("arbitrary",)),
    )(x, final_g.reshape(1, D), target)


V_LNG, V_LNB, V_CB, V_PS, V_CW0, V_CW1, V_CW2 = range(7)


def _mix_bwd(p, dxo, ya, yb, yc, lw, name):
    T, D = dxo.shape
    N = p.shape[1]
    R = _mix_block_rows(T, MIX_BWD_ROWS)
    nb = T // R

    def body(p_ref, hxb, hcg, hxc, dxo_ref, ya_ref, yb_ref, yc_ref, cv_ref, bsb_ref, wcat_ref, wcatt_ref,
             wpool_ref, band_ref, bandt_ref, wpa_ref, wpb_ref, wpc_ref, wo_ref,
             dp_ref, acts_ref, mrg_ref, dys_ref, gwc_ref, gbs_ref, gwpool_ref, gvec_ref,
             extb, extdy, cdy, cq, bsacc):
        i = pl.program_id(0)
        blk = nb - 1 - i

        @pl.when(i == 0)
        def _():
            for ref in (gwc_ref, gwpool_ref, gvec_ref, cdy, cq, bsacc):
                _zero(ref)

        cv = cv_ref[...]
        r = _mixers(p_ref, hxb, hcg, hxc, cv, bsb_ref, wcat_ref, wpool_ref, band_ref, extb,
                    blk == 0, blk, R, True)

        dxo_b = dxo_ref[...].astype(BF16)
        dm = _dot_nt(dxo_b, wo_ref[...]).astype(BF16)
        ys = [ya_ref[...], yb_ref[...], yc_ref[...]]
        sig = [_sigmoid(p_ref[:, O_G + k * D:O_G + (k + 1) * D]) for k in range(3)]
        mrg_ref[...] = sig[0] * ys[0] + sig[1] * ys[1] + sig[2] * ys[2]
        dacts = []
        for k, (act, w_ref) in enumerate(((r["a"], wpa_ref), (r["b"], wpb_ref), (r["c"], wpc_ref))):
            dyk = dm * sig[k]
            dp_ref[:, O_G + k * D:O_G + (k + 1) * D] = dyk * ys[k] * (1.0 - sig[k])
            acts_ref[:, k * SEG:(k + 1) * SEG] = act.astype(BF16)
            dys_ref[:, k * D:(k + 1) * D] = dyk
            dacts.append(_dot_nt(dyk, w_ref[...]))
        da, db, dc = dacts

        def silu_bwd(dact, pre, z, s):
            return dact * (z * s), dact * pre * (s * (1.0 + z * (1.0 - s)))

        d_aout, dza = silu_bwd(da, r["a_out"], r["za"], r["sa"])
        dp_ref[:, O_ZA:O_ZA + SEG] = dza.astype(BF16)
        dp_ref[:, O_U:O_U + SEG] = (d_aout * r["sg"] * _gelu_grad(r["u"], r["tu"])).astype(BF16)
        d_sg = d_aout * r["ug"]
        dvns = []
        for ci in range(R // CHUNK):
            dsc = d_sg[ci * CHUNK:(ci + 1) * CHUNK]
            bsacc[...] += dsc
            dsc_b = dsc.astype(BF16)
            dst = jnp.concatenate([jnp.where(r["lane_group"] == g, dsc_b, r["zero_b"]) for g in range(GROUPS)], axis=0)
            dvns.append(_dot(wcatt_ref[...], dst))
            gwc_ref[...] += _dot_nt(dst, r["vn"][ci * CHUNK:(ci + 1) * CHUNK])
        d_vn = dvns[0] if len(dvns) == 1 else jnp.concatenate(dvns, axis=0)
        vhat = r["vhat"]
        gvec_ref[V_LNG:V_LNG + 1, :] += jnp.sum(d_vn * vhat, axis=0, keepdims=True)
        gvec_ref[V_LNB:V_LNB + 1, :] += jnp.sum(d_vn, axis=0, keepdims=True)
        d_vhat = d_vn * cv[C_LNG:C_LNG + 1, :]
        d_vg = r["rs"] * (d_vhat - jnp.mean(d_vhat, axis=-1, keepdims=True)
                          - vhat * jnp.mean(d_vhat * vhat, axis=-1, keepdims=True))
        dp_ref[:, O_V:O_V + SEG] = (d_vg * _gelu_grad(r["v"], r["tv"])).astype(BF16)

        d_bout, dzb = silu_bwd(db, r["b_out"], r["zb"], r["sb"])
        dp_ref[:, O_ZB:O_ZB + SEG] = dzb.astype(BF16)
        dp_ref[:, O_BG:O_BG + SEG] = (d_bout * r["yconv"]).astype(BF16)
        d_y = d_bout * r["bg"]
        gvec_ref[V_CB:V_CB + 1, :] += jnp.sum(d_y, axis=0, keepdims=True)
        gvec_ref[V_CW0:V_CW0 + 1, :] += jnp.sum(d_y * r["cx2"], axis=0, keepdims=True)
        gvec_ref[V_CW1:V_CW1 + 1, :] += jnp.sum(d_y * r["cx1"], axis=0, keepdims=True)
        gvec_ref[V_CW2:V_CW2 + 1, :] += jnp.sum(d_y * r["cx"], axis=0, keepdims=True)
        extdy[0:R, :] = d_y
        extdy[R:R + HALO, :] = cdy[...]
        d_cx = (cv[C_CW2:C_CW2 + 1, :] * d_y + cv[C_CW1:C_CW1 + 1, :] * extdy[pl.ds(1, R), :]
                + cv[C_CW0:C_CW0 + 1, :] * extdy[pl.ds(2, R), :])
        cdy[...] = d_y[0:HALO]
        dp_ref[:, O_CG:O_CG + SEG] = (d_cx * r["xb"]).astype(BF16)
        dp_ref[:, O_XB:O_XB + SEG] = (d_cx * r["cg"]).astype(BF16)

        d_cout, dzc = silu_bwd(dc, r["c_out"], r["zc"], r["sc"])
        dp_ref[:, O_ZC:O_ZC + SEG] = dzc.astype(BF16)
        gvec_ref[V_PS:V_PS + 1, :] += jnp.sum(d_cout * r["pw"], axis=0, keepdims=True)
        d_pw = (d_cout * cv[C_PS:C_PS + 1, :]).astype(BF16)
        dpool, scaled = [], []
        for gi, w in enumerate(POOL_WINDOWS):
            lo = gi * POOL_GROUP
            dpw_g = d_pw[:, lo:lo + POOL_GROUP]
            gwpool_ref[lo:lo + POOL_GROUP, :] += _dot_tn(r["pooled"][gi], dpw_g)
            dpg = _dot_nt(dpw_g, wpool_ref[gi])
            dpool.append(dpg)
            scaled.append((dpg * r["invs"][gi]).astype(BF16))
        q = jnp.concatenate(scaled, axis=1)
        q_ext = jnp.concatenate([q, cq[...], jnp.zeros((CHUNK - HALO, SEG), BF16)], axis=0)
        for gi, w in enumerate(POOL_WINDOWS):
            lo = gi * POOL_GROUP
            acc = _dot(bandt_ref[gi], q_ext[:, lo:lo + POOL_GROUP])
            dp_ref[:, O_XC + lo:O_XC + lo + POOL_GROUP] = (acc - dpool[gi]).astype(BF16)
        cq[...] = q[0:HALO]

        @pl.when(i == nb - 1)
        def _():
            rr = lax.broadcasted_iota(jnp.int32, gwc_ref.shape, 0) % CHUNK
            cc = lax.broadcasted_iota(jnp.int32, gwc_ref.shape, 1)
            gwc_ref[...] = jnp.where(cc <= rr, gwc_ref[...], 0.0)
            acc = bsacc[...]
            hi = acc.astype(BF16)
            lo_ = (acc - hi.astype(F32)).astype(BF16)
            sel = (lax.broadcasted_iota(jnp.int32, (SEG, LANES), 0) // HEAD
                   == lax.broadcasted_iota(jnp.int32, (SEG, LANES), 1)).astype(BF16)
            gbs_ref[...] = _dot(hi, sel) + _dot(lo_, sel)

    row = lambda w: pl.BlockSpec((R, w), lambda i: (nb - 1 - i, 0))
    consts = [lw["cvec"], lw["bsb"], lw["wcat"], lw["wcatt"], lw["wpool"], _pool_bands(R, False), _pool_bands(R, True),
              lw["wpa"], lw["wpb"], lw["wpc"], lw["wo"]]
    acc_shapes = [(GROUPS * CHUNK, CHUNK), (CHUNK, LANES), (SEG, POOL_GROUP), (8, SEG)]
    row_widths = [N, 3 * SEG, D, 3 * D]
    return pl.pallas_call(
        body, name=name, grid=(nb,),
        in_specs=([row(N)] + _halo_specs(R, nb, True) + [row(D), row(D), row(D), row(D)]
                  + [_const_spec(c.shape) for c in consts]),
        out_specs=[row(w) for w in row_widths] + [_const_spec(s) for s in acc_shapes],
        out_shape=[SDS((T, w), BF16) for w in row_widths] + [SDS(s, F32) for s in acc_shapes],
        scratch_shapes=[pltpu.VMEM((HALO + R, SEG), F32)] * 2
        + [pltpu.VMEM((HALO, SEG), F32), pltpu.VMEM((HALO, SEG), BF16), pltpu.VMEM((CHUNK, SEG), F32)],
        compiler_params=_cparams(("arbitrary",)),
    )(p, p, p, p, dxo, ya, yb, yc, *consts)


def _proj_wgrad(acts, merged, dys, dxo, name, riders=()):
    T, D = dxo.shape
    bk = _pick(T, WGRAD_ROWS // 2, 16)

    def body(a_ref, m_ref, dy_ref, dxo_ref, gwpa_ref, gwpb_ref, gwpc_ref, gwo_ref):
        @pl.when(pl.program_id(0) == 0)
        def _():
            for ref in (gwpa_ref, gwpb_ref, gwpc_ref, gwo_ref):
                _zero(ref)

        gwo_ref[...] += _dot_tn(m_ref[...], dxo_ref[...].astype(BF16))
        for k, ref in enumerate((gwpa_ref, gwpb_ref, gwpc_ref)):
            ref[...] += _dot_tn(a_ref[:, k * SEG:(k + 1) * SEG], dy_ref[:, k * D:(k + 1) * D])

    row = lambda w: pl.BlockSpec((bk, w), lambda i: (i, 0))
    shapes = [(SEG, D), (SEG, D), (SEG, D), (D, D)]
    return _host_call(
        body, name, (T // bk,), riders,
        inputs=[acts, merged, dys, dxo],
        in_specs=[row(3 * SEG), row(D), row(3 * D), row(D)],
        out_specs=[_const_spec(s) for s in shapes], out_shape=[SDS(s, F32) for s in shapes],
        scratch_shapes=[])


def _inproj_token_blocks(T):
    return T // _pick(T, BWD_X_ROWS, 16)


def _inproj_bwd_x(dp, win_t, x, norm_g, dxo, name, riders=(), blocks=None, fill=None):
    T, D = x.shape
    N = dp.shape[1]
    bt = _pick(T, BWD_X_ROWS, 16)
    b0, nblk = blocks if blocks else (0, T // bt)

    def compute(dp_ref, w_ref, x_ref, g_ref, dxo_ref, *rest):
        dx_ref, dg_ref = rest[-2:]

        @pl.when(pl.program_id(0) == 0)
        def _():
            _zero(dg_ref)

        dh = _dot(dp_ref[...], w_ref[...])
        xv = x_ref[...]
        rstd = lax.rsqrt(jnp.mean(xv * xv, axis=-1, keepdims=True) + RMS_EPS)
        xhat = xv * rstd
        dg_ref[0:1, :] += jnp.sum(dh * xhat, axis=0, keepdims=True)
        dxn = dh * g_ref[...]
        dx_ref[...] = dxo_ref[...] + rstd * (dxn - xhat * jnp.mean(dxn * xhat, axis=-1, keepdims=True))

    rows = pl.BlockSpec((bt, D), lambda i: (i + b0, 0))
    return _host_call(
        compute, name, (nblk,), riders,
        inputs=[dp, win_t, x, norm_g.reshape(1, D), dxo] + ([] if fill is None else [fill]),
        in_specs=[pl.BlockSpec((bt, N), lambda i: (i + b0, 0)), _const_spec((N, D)), rows, _const_spec((1, D)), rows]
        + ([] if fill is None else [_ANY]),
        out_specs=[rows, _const_spec((8, D))],
        out_shape=[SDS((T, D), F32), SDS((8, D), F32)],
        scratch_shapes=[],
        aliases={} if fill is None else {5: 0})


def _inproj_bwd_w(dp, h, name, riders=()):
    T, N = dp.shape
    D = h.shape[1]
    bn = _pick(N, 1536, LANES)
    bk = _pick(T, WGRAD_ROWS, 16)
    nk = T // bk

    def compute(dp_ref, h_ref, o_ref):
        @pl.when(pl.program_id(1) == 0)
        def _():
            _zero(o_ref)

        o_ref[...] += _dot_tn(dp_ref[...], h_ref[...])

    return _host_call(
        compute, name, (N // bn, nk), riders,
        inputs=[dp, h],
        in_specs=[pl.BlockSpec((bk, bn), lambda j, k: (k, j)), pl.BlockSpec((bk, D), lambda j, k: (k, 0))],
        out_specs=[pl.BlockSpec((bn, D), lambda j, k: (j, 0))],
        out_shape=[SDS((N, D), F32)],
        scratch_shapes=[])


def _chip_peer(x, y, j):
    px = (1 - x) if (j >> 1) else x
    py = (1 - y) if (j & 1) else y
    return px, py


def _blk(ref, kind, k, n):
    if kind == "rows":
        return ref.at[pl.ds(pl.multiple_of(k * n, 8), n)]
    return ref.at[:, pl.ds(pl.multiple_of(k * n, LANES), n)]


class _Exchange:
    def __init__(self, srcs, out_shapes, n_sems, build, alias=None):
        self.srcs, self.out_shapes, self.n_sems, self.build = list(srcs), list(out_shapes), n_sems, build
        self.alias = dict(alias or {})


def _rider_aliases(riders, first_in, first_out):
    out, i, o = {}, first_in, first_out
    for e in riders:
        out.update({i + s: o + d for s, d in e.alias.items()})
        i, o = i + len(e.srcs), o + len(e.out_shapes)
    return out


def _rider_plan(riders):
    inputs = [s for e in riders for s in e.srcs]
    out_shapes = [o for e in riders for o in e.out_shapes]
    sems = [pltpu.SemaphoreType.DMA((e.n_sems,)) for e in riders for _ in range(2)]

    def copies(in_refs, out_refs, sem_refs):
        cps, i, o = [], 0, 0
        for k, e in enumerate(riders):
            ni, no = len(e.srcs), len(e.out_shapes)
            cps += e.build(in_refs[i:i + ni], out_refs[o:o + no], sem_refs[2 * k], sem_refs[2 * k + 1])
            i, o = i + ni, o + no
        return cps

    return inputs, out_shapes, sems, copies


_ANY = pl.BlockSpec(memory_space=pl.ANY)


def _host_call(compute, name, grid, riders, inputs, in_specs, out_specs, out_shape, scratch_shapes, aliases=None):
    r_in, r_out, r_sems, copies = _rider_plan(riders)
    ni, no, ns = len(inputs), len(out_shape), len(scratch_shapes)

    def body(*refs):
        ins, rins = refs[:ni], refs[ni:ni + len(r_in)]
        outs = refs[ni + len(r_in):ni + len(r_in) + no]
        routs = refs[ni + len(r_in) + no:ni + len(r_in) + no + len(r_out)]
        scr = refs[ni + len(r_in) + no + len(r_out):]
        first = functools.reduce(lambda a, b: a & b, [pl.program_id(d) == 0 for d in range(len(grid))])
        last = functools.reduce(lambda a, b: a & b, [pl.program_id(d) == grid[d] - 1 for d in range(len(grid))])
        if riders:
            @pl.when(first)
            def _():
                for cp in copies(rins, routs, scr[ns:]):
                    cp.start()

        compute(*ins, *outs, *scr[:ns])

        if riders:
            @pl.when(last)
            def _():
                for cp in copies(rins, routs, scr[ns:]):
                    cp.wait()

    res = pl.pallas_call(
        body, name=name, grid=grid,
        in_specs=list(in_specs) + [_ANY] * len(r_in),
        out_specs=list(out_specs) + [_ANY] * len(r_out),
        out_shape=list(out_shape) + r_out,
        scratch_shapes=list(scratch_shapes) + r_sems,
        input_output_aliases={**(aliases or {}), **_rider_aliases(riders, ni, no)},
        compiler_params=_cparams(("arbitrary",) * len(grid)),
    )(*inputs, *r_in)
    return res[:no], _split_riders(riders, res[no:])


def _split_riders(riders, flat):
    out, o = [], 0
    for e in riders:
        out.append(list(flat[o:o + len(e.out_shapes)]))
        o += len(e.out_shapes)
    return out


def _run_exchange(ex, name):
    n_in, n_out = len(ex.srcs), len(ex.out_shapes)

    def body(*refs):
        cps = ex.build(refs[:n_in], refs[n_in:n_in + n_out], refs[n_in + n_out], refs[n_in + n_out + 1])
        for cp in cps:
            cp.start()
        for cp in cps:
            cp.wait()

    return pl.pallas_call(
        body, name=name,
        in_specs=[_ANY] * n_in, out_specs=[_ANY] * n_out, out_shape=ex.out_shapes,
        input_output_aliases=ex.alias,
        scratch_shapes=[pltpu.SemaphoreType.DMA((ex.n_sems,)), pltpu.SemaphoreType.DMA((ex.n_sems,))],
        compiler_params=pltpu.CompilerParams(has_side_effects=True),
    )(*ex.srcs)


def _gather_sizes(shards, kinds):
    sizes = [s.shape[0] if k == "rows" else s.shape[1] for s, k in zip(shards, kinds)]
    fulls = [SDS((s.shape[0] * N_DEV,) + s.shape[1:], s.dtype) if k == "rows"
             else SDS((s.shape[0], s.shape[1] * N_DEV), s.dtype) for s, k in zip(shards, kinds)]
    return sizes, fulls


def _gather_direct(shards, kinds):
    n = len(shards)
    sizes, fulls = _gather_sizes(shards, kinds)

    def build(ins, outs, send_sems, recv_sems):
        x, y, c = _me()
        cps = []
        for a in range(n):
            mine = _blk(outs[a], kinds[a], 4 * x + 2 * y + c, sizes[a])
            cps.append(pltpu.make_async_copy(ins[a], mine, send_sems.at[5 * a + 4]))
            for j in range(N_CHIP):
                to = (x, y, 1 - c) if j == 0 else (*_chip_peer(x, y, j), c)
                cps.append(pltpu.make_async_remote_copy(
                    src_ref=ins[a], dst_ref=mine, send_sem=send_sems.at[5 * a + j], recv_sem=recv_sems.at[5 * a + j],
                    device_id=to, device_id_type=MESH))
        return cps

    return _Exchange(shards, fulls, 5 * n, build)


def _gather_everywhere(shards, kinds):
    n = len(shards)
    sizes, fulls = _gather_sizes(shards, kinds)

    def build(ins, outs, send_sems, recv_sems):
        x, y, c = _me()
        cps = []
        for a in range(n):
            mine = _blk(outs[a], kinds[a], 4 * x + 2 * y + c, sizes[a])
            cps.append(pltpu.make_async_copy(ins[a], mine, send_sems.at[N_DEV * a]))
            for d in range(1, N_DEV):
                to = ((1 - x) if d & 4 else x, (1 - y) if d & 2 else y, (1 - c) if d & 1 else c)
                cps.append(pltpu.make_async_remote_copy(
                    src_ref=ins[a], dst_ref=mine, send_sem=send_sems.at[N_DEV * a + d],
                    recv_sem=recv_sems.at[N_DEV * a + d], device_id=to, device_id_type=MESH))
        return cps

    return _Exchange(shards, fulls, N_DEV * n, build)


def _gather_forward(fulls, kinds, sizes):
    n = len(fulls)

    def build(ins, outs, send_sems, recv_sems):
        x, y, c = _me()
        cps = []
        for a in range(n):
            for j in (1, 2, 3):
                px, py = _chip_peer(x, y, j)
                k = 4 * px + 2 * py + c
                cps.append(pltpu.make_async_remote_copy(
                    src_ref=_blk(ins[a], kinds[a], k, sizes[a]), dst_ref=_blk(outs[a], kinds[a], k, sizes[a]),
                    send_sem=send_sems.at[3 * a + j - 1], recv_sem=recv_sems.at[3 * a + j - 1],
                    device_id=(x, y, 1 - c), device_id_type=MESH))
        return cps

    return _Exchange(fulls, [SDS(f.shape, f.dtype) for f in fulls], 3 * n, build, alias={a: a for a in range(n)})


def _sibling_exchange(grads, kinds, sizes):
    n = len(grads)

    def blk_shape(a):
        g = grads[a]
        return (sizes[a],) + g.shape[1:] if kinds[a] == "rows" else (g.shape[0], sizes[a])

    def build(ins, outs, send_sems, recv_sems):
        x, y, c = _me()
        cps = []
        for a in range(n):
            for q in range(N_CHIP):
                cps.append(pltpu.make_async_remote_copy(
                    src_ref=_blk(ins[a], kinds[a], 2 * q + (1 - c), sizes[a]), dst_ref=outs[a].at[q],
                    send_sem=send_sems.at[N_CHIP * a + q], recv_sem=recv_sems.at[N_CHIP * a + q],
                    device_id=(x, y, 1 - c), device_id_type=MESH))
        return cps

    return _Exchange(grads, [SDS((N_CHIP,) + blk_shape(a), F32) for a in range(n)], N_CHIP * n, build)


def _chip_partial(g, r1, kind, size, cidx, name):
    if kind == "rows":
        rows, cols = size, g.shape[1]
        g3 = g.reshape(N_DEV, rows, cols)
        rb = _pick(rows, 512, 16)
        g_spec = pl.BlockSpec((1, rb, cols), lambda q, j, c: (2 * q + c[0], j, 0))
        grid = (N_CHIP, rows // rb)
        blk = (1, rb, cols)
        imap = lambda q, j, c: (q, j, 0)
    else:
        rows, cols = g.shape[0], size
        g3 = g
        g_spec = pl.BlockSpec((rows, cols), lambda q, j, c: (0, 2 * q + c[0]))
        grid = (N_CHIP, 1)
        blk = (1, rows, cols)
        imap = lambda q, j, c: (q, 0, 0)

    def body(c_ref, g_ref, r_ref, p_ref, pb_ref):
        s = g_ref[...].reshape(blk) + r_ref[...]
        p_ref[...] = s
        pb_ref[...] = s.astype(BF16)

    return pl.pallas_call(
        body, name=name,
        grid_spec=pltpu.PrefetchScalarGridSpec(
            num_scalar_prefetch=1, grid=grid,
            in_specs=[g_spec, pl.BlockSpec(blk, imap)],
            out_specs=[pl.BlockSpec(blk, imap), pl.BlockSpec(blk, imap)]),
        out_shape=[SDS((N_CHIP, rows, cols), F32), SDS((N_CHIP, rows, cols), BF16)],
        compiler_params=_cparams(("arbitrary", "arbitrary")),
    )(cidx, g3, r1)


def _chip_exchange(parts):
    n = len(parts)
    m = N_CHIP - 1

    def build(ins, outs, send_sems, recv_sems):
        x, y, c = _me()
        cps = []
        for a in range(n):
            for j in (1, 2, 3):
                px, py = _chip_peer(x, y, j)
                cps.append(pltpu.make_async_remote_copy(
                    src_ref=ins[a].at[2 * px + py], dst_ref=outs[a].at[j - 1], send_sem=send_sems.at[m * a + j - 1],
                    recv_sem=recv_sems.at[m * a + j - 1], device_id=(px, py, c), device_id_type=MESH))
        return cps

    return _Exchange(parts, [SDS((m,) + p.shape[1:], BF16) for p in parts], m * n, build)


def _grad_total(part, r2, qidx, name):
    _, rows, cols = part.shape
    rb = _pick(rows, 512, 16)

    def body(q_ref, p_ref, r_ref, o_ref):
        s = p_ref[0]
        for j in range(N_CHIP - 1):
            s = s + r_ref[j].astype(F32)
        o_ref[...] = s

    return pl.pallas_call(
        body, name=name,
        grid_spec=pltpu.PrefetchScalarGridSpec(
            num_scalar_prefetch=1, grid=(rows // rb,),
            in_specs=[pl.BlockSpec((1, rb, cols), lambda i, q: (q[0], i, 0)),
                      pl.BlockSpec((N_CHIP - 1, rb, cols), lambda i, q: (0, i, 0))],
            out_specs=pl.BlockSpec((rb, cols), lambda i, q: (i, 0))),
        out_shape=SDS((rows, cols), F32),
        compiler_params=_cparams(("arbitrary",)),
    )(qidx, part, r2)


def _all_reduce_small(pack, name):
    rows = pack.shape[0]
    rs = rows // N_DEV
    assert rs * N_DEV == rows and rs % 8 == 0

    def body(x_ref, o_ref, rbuf, red, send1, recv1, send2, recv2):
        x, y, c = _me()
        me = 4 * x + 2 * y + c

        def peer(d):
            px = (1 - x) if (d >> 2) & 1 else x
            py = (1 - y) if (d >> 1) & 1 else y
            pc = (1 - c) if d & 1 else c
            return px, py, pc

        def sl(ref, k):
            return ref.at[pl.ds(pl.multiple_of(k * rs, 8), rs)]

        phase1 = []
        for d in range(1, N_DEV):
            px, py, pc = peer(d)
            phase1.append(pltpu.make_async_remote_copy(
                src_ref=sl(x_ref, 4 * px + 2 * py + pc), dst_ref=rbuf.at[d], send_sem=send1.at[d], recv_sem=recv1.at[d],
                device_id=(px, py, pc), device_id_type=MESH))
        for cp in phase1:
            cp.start()
        acc = sl(x_ref, me)[...]
        for cp in phase1:
            cp.wait()
        for d in range(1, N_DEV):
            acc = acc + rbuf[d]
        red[...] = acc
        sl(o_ref, me)[...] = acc
        phase2 = []
        for d in range(1, N_DEV):
            px, py, pc = peer(d)
            phase2.append(pltpu.make_async_remote_copy(
                src_ref=red, dst_ref=sl(o_ref, me), send_sem=send2.at[d], recv_sem=recv2.at[d],
                device_id=(px, py, pc), device_id_type=MESH))
        for cp in phase2:
            cp.start()
        for cp in phase2:
            cp.wait()

    vm = pl.BlockSpec(memory_space=pltpu.VMEM)
    return pl.pallas_call(
        body, name=name, in_specs=[vm], out_specs=vm, out_shape=SDS(pack.shape, F32),
        scratch_shapes=[pltpu.VMEM((N_DEV, rs, LANES), F32), pltpu.VMEM((rs, LANES), F32),
                        pltpu.SemaphoreType.DMA((N_DEV,)), pltpu.SemaphoreType.DMA((N_DEV,)),
                        pltpu.SemaphoreType.DMA((N_DEV,)), pltpu.SemaphoreType.DMA((N_DEV,))],
        compiler_params=_cparams(None, has_side_effects=True),
    )(pack)


def _adamw(w, g, m, v, name):
    rows, cols = w.shape
    rb = rows if rows * cols * 4 <= ADAMW_WHOLE_BYTES else _pick(rows, 256, 8)
    c1 = 1.0 / (1.0 - ADAM_B1 ** ADAM_STEP)
    c2 = 1.0 / (1.0 - ADAM_B2 ** ADAM_STEP)

    def body(w_ref, g_ref, m_ref, v_ref, d_ref, mo_ref, vo_ref):
        gv = g_ref[...]
        mn = ADAM_B1 * m_ref[...] + (1.0 - ADAM_B1) * gv
        vn = ADAM_B2 * v_ref[...] + (1.0 - ADAM_B2) * (gv * gv)
        mo_ref[...] = mn
        vo_ref[...] = vn
        d_ref[...] = -ADAM_LR * ((mn * c1) / (jnp.sqrt(vn * c2) + ADAM_EPS) + ADAM_WD * w_ref[...])

    spec = pl.BlockSpec((rb, cols), lambda i: (i, 0))
    return pl.pallas_call(
        body, name=name, grid=(rows // rb,),
        in_specs=[spec] * 4, out_specs=[spec] * 3, out_shape=[SDS((rows, cols), F32)] * 3,
        compiler_params=_cparams(("arbitrary",)),
    )(w, g, m, v)


def _pad_rows(a, mult=8):
    r = (-a.shape[0]) % mult
    return a if r == 0 else jnp.pad(a, ((0, r), (0, 0)))


def _as_lanes(a):
    flat = a.reshape(-1)
    pad = (-flat.shape[0]) % (8 * LANES)
    if pad:
        flat = jnp.pad(flat, (0, pad))
    return flat.reshape(-1, LANES)


def kernel(x, norm_g, w_in, ln_g, ln_b, w_s, b_s, conv_w, conv_b, w_pool, pool_scale, w_pa, w_pb, w_pc, w_o, final_g, loss_target, m_norm_g, m_w_in, m_ln_g, m_ln_b, m_w_s, m_b_s, m_conv_w, m_conv_b, m_w_pool, m_pool_scale, m_w_pa, m_w_pb, m_w_pc, m_w_o, m_final_g, v_norm_g, v_w_in, v_ln_g, v_ln_b, v_w_s, v_b_s, v_conv_w, v_conv_b, v_w_pool, v_pool_scale, v_w_pa, v_w_pb, v_w_pc, v_w_o, v_final_g):
    L = w_in.shape[0]
    D = x.shape[-1]
    n_loc = w_in.shape[2]
    pc_loc = w_pa.shape[2]
    x0 = x[0]
    target = loss_target[0]
    xi, yi, ci = _me()
    cidx = jnp.reshape(ci, (1,)).astype(jnp.int32)
    qidx = jnp.reshape(2 * xi + yi, (1,)).astype(jnp.int32)

    kinds5 = ["rows", "cols", "cols", "cols", "rows"]

    def layer_shards(l):
        return [w_in[l].T.astype(BF16), w_pa[l].astype(BF16), w_pb[l].astype(BF16), w_pc[l].astype(BF16),
                w_o[l].astype(BF16)]

    def gathered(direct, shards, kinds, l):
        sizes, _ = _gather_sizes(shards, kinds)
        return _run_exchange(_gather_forward(direct, kinds, sizes), f"weights_forward_{l}")

    cw_loc = _pad_rows(conv_w.reshape(L * CONV_TAPS, -1))
    cw_loc = jnp.pad(cw_loc, ((0, 0), (0, LANES - cw_loc.shape[1])))
    causal = jnp.tril(jnp.ones((CHUNK, CHUNK), dtype=bool))

    chip_order = jnp.stack([2 * xi + yi] + [2 * px + py for px, py in (_chip_peer(xi, yi, j) for j in (1, 2, 3))])
    sh0 = layer_shards(0)
    rest0, krest0 = sh0[1:] + [cw_loc], kinds5[1:] + ["rows"]
    (p0, h0, win_t0), delivered = _inproj_gathering(x0, norm_g[0], sh0[0], chip_order.astype(jnp.int32), "inproj_fwd_0",
                                                    [_gather_everywhere(rest0, krest0)])
    rest0_full = delivered[0]
    cw_all = rest0_full[-1].reshape(N_DEV, -1, LANES)[:, :L * CONV_TAPS, :conv_w.shape[2]]
    conv_w_full = jnp.transpose(cw_all, (1, 0, 2)).reshape(L, CONV_TAPS, -1)

    def make_layer(l, full5):
        win_t, wpa, wpb, wpc, wo = full5
        wm = jnp.where(causal, w_s[l], 0.0)
        cvec = jnp.concatenate([ln_g[l][None], ln_b[l][None], conv_w_full[l], conv_b[l][None], pool_scale[l][None],
                                jnp.zeros((C_ROWS - 7, SEG), F32)], axis=0)
        return dict(
            win_t=win_t, wpa=wpa, wpb=wpb, wpc=wpc, wo=wo, cvec=cvec,
            bsb=jnp.repeat(b_s[l].T, HEAD, axis=1),
            wcat=jnp.transpose(wm, (1, 0, 2)).reshape(CHUNK, GROUPS * CHUNK).astype(BF16),
            wcatt=jnp.transpose(wm, (2, 0, 1)).reshape(CHUNK, GROUPS * CHUNK).astype(BF16),
            wpool=w_pool[l].astype(BF16))

    layers, xs, saved, win_next = [], [x0], [], None
    for l in range(L):
        if l == 0:
            p, h = p0, h0
            lw = make_layer(0, [win_t0] + list(rest0_full[:4]))
        else:
            (p, h), delivered = _inproj(xs[-1], norm_g[l], win_next, f"inproj_fwd_{l}",
                                        [_gather_everywhere(layer_shards(l)[1:], kinds5[1:])])
            lw = make_layer(l, [win_next] + delivered[0])
        layers.append(lw)
        nxt = layer_shards(l + 1)[:1] if l + 1 < L else None
        (xn, ya, yb, yc), delivered = _mix_fwd(p, xs[-1], lw, f"mix_fwd_{l}", [_gather_direct(nxt, kinds5[:1])] if nxt else [])
        if nxt:
            win_next = gathered(delivered[0], nxt, kinds5[:1], l + 1)[0]
        saved.append((p, h, ya, yb, yc))
        xs.append(xn)
    dx, loss_acc, dfg_acc = _loss_head(xs[-1], final_g, target, "loss_head")

    rs_sizes = [n_loc, pc_loc, pc_loc, pc_loc, w_o.shape[1]]
    await_sibling, await_chips = [], []
    partial_of, from_chips = {}, {}
    serial = [0]

    def riders_now():
        riders, plan = [], []
        for grp in await_chips:
            riders.append(_chip_exchange([partial_of[t][1] for t, _, _, _ in grp]))
            plan.append(("chips", grp))
        for grp in await_sibling:
            riders.append(_sibling_exchange([g for _, g, _, _ in grp], [k for _, _, k, _ in grp], [s for _, _, _, s in grp]))
            plan.append(("sibling", grp))
        del await_chips[:], await_sibling[:]
        return riders, plan

    def absorb(plan, delivered):
        for (what, grp), res in zip(plan, delivered):
            for (t, g, k, s), r in zip(grp, res):
                if what == "chips":
                    from_chips[t] = r
                else:
                    partial_of[t] = _chip_partial(g, r, k, s, cidx, f"grad_chip_partial_{t[0]}_{t[1]}")
            if what == "sibling":
                await_chips.append(grp)

    small = [None] * L
    for l in reversed(range(L)):
        lw = layers[l]
        p, h, ya, yb, yc = saved[l]
        dp, acts, merged, dys, gwc, gbs, gwpool, gvec = _mix_bwd(p, dx, ya, yb, yc, lw, f"mix_bwd_{l}")
        riders, plan = riders_now()
        (gwpa, gwpb, gwpc, gwo), delivered = _proj_wgrad(acts, merged, dys, dx, f"proj_wgrad_{l}", riders)
        absorb(plan, delivered)
        await_sibling.append([((l, a), g, kinds5[a], rs_sizes[a]) for a, g in ((1, gwpa), (2, gwpb), (3, gwpc), (4, gwo))])

        def bwd_x(dxo, pieces):
            nt = _inproj_token_blocks(dxo.shape[0])
            pieces = min(pieces, nt)
            done, dng, b0 = None, None, 0
            for k in range(pieces):
                cnt = (nt - b0) if k == pieces - 1 else max(1, (nt * LAST_PIECE_SHARE[0]) // (LAST_PIECE_SHARE[1] * (pieces - 1)))
                riders, plan = riders_now()
                (done, dng_k), delivered = _inproj_bwd_x(dp, lw["win_t"], xs[l], norm_g[l], dxo, f"inproj_bwd_x_{l}_{k}",
                                                         riders, blocks=(b0, cnt), fill=done)
                absorb(plan, delivered)
                dng = dng_k if dng is None else dng + dng_k
                b0 += cnt
            return done, dng

        def bwd_w():
            riders, plan = riders_now()
            (gwin_t,), delivered = _inproj_bwd_w(dp, h, f"inproj_bwd_w_{l}", riders)
            absorb(plan, delivered)
            await_sibling.append([((l, 0), gwin_t, kinds5[0], rs_sizes[0])])

        if l == L - 1:
            dx, dng = bwd_x(dx, 1)
            bwd_w()
        else:
            bwd_w()
            dx, dng = bwd_x(dx, 2 if l == 0 else 1)
        small[l] = dict(norm_g=dng[0], ln_g=gvec[V_LNG], ln_b=gvec[V_LNB], w_s=gwc, b_s=gbs, conv_w=gvec[V_CW0:V_CW0 + 3],
                        conv_b=gvec[V_CB], w_pool=gwpool, pool_scale=gvec[V_PS])
    while await_sibling or await_chips:
        riders, plan = riders_now()
        delivered = []
        for ex in riders:
            delivered.append(_run_exchange(ex, f"grad_exchange_tail_{serial[0]}"))
            serial[0] += 1
        absorb(plan, delivered)
    grad_x = dx[None]
    big_grads = []
    for l in range(L):
        tot = [_grad_total(partial_of[(l, a)][0], from_chips[(l, a)], qidx, f"grad_total_{l}_{a}") for a in range(5)]
        big_grads.append([tot[0],
                          tot[1].reshape(SEG, pc_loc), tot[2].reshape(SEG, pc_loc), tot[3].reshape(SEG, pc_loc),
                          tot[4]])

    names = ["norm_g", "ln_g", "ln_b", "w_s", "b_s", "conv_w", "conv_b", "w_pool", "pool_scale"]
    pieces = [jnp.stack([small[l][nm] for l in range(L)]).reshape(-1) for nm in names]
    pieces += [dfg_acc[0], loss_acc.reshape(-1)]
    offs = [0]
    for pc in pieces:
        offs.append(offs[-1] + pc.shape[0])
    pack = _pad_rows(_as_lanes(jnp.concatenate(pieces)), 8 * N_DEV)
    red = _all_reduce_small(pack, "small_grads_all_reduce").reshape(-1)

    def unpack(i, shape):
        return red[offs[i]:offs[i] + math.prod(shape)].reshape(shape)

    g_norm_g = unpack(0, (L, D))
    g_ln_g = unpack(1, (L, SEG))
    g_ln_b = unpack(2, (L, SEG))
    g_w_s = unpack(3, (L, GROUPS, CHUNK, CHUNK))
    g_b_s = jnp.transpose(unpack(4, (L, CHUNK, LANES))[:, :, :GROUPS], (0, 2, 1))
    g_conv_w_full = unpack(5, (L, CONV_TAPS, SEG))
    g_conv_b = unpack(6, (L, SEG))
    g_w_pool = unpack(7, (L, len(POOL_WINDOWS), POOL_GROUP, POOL_GROUP))
    g_pool_scale = unpack(8, (L, SEG))
    g_final_g = unpack(9, (D,))
    loss = red[offs[10]]
    dev = 4 * xi + 2 * yi + ci
    g_conv_w = lax.dynamic_slice_in_dim(g_conv_w_full, dev * conv_w.shape[2], conv_w.shape[2], axis=2)

    g_w_in_t = jnp.stack([big_grads[l][0] for l in range(L)])
    g_w_in = jnp.swapaxes(g_w_in_t, 1, 2)
    g_w_pa = jnp.stack([big_grads[l][1] for l in range(L)])
    g_w_pb = jnp.stack([big_grads[l][2] for l in range(L)])
    g_w_pc = jnp.stack([big_grads[l][3] for l in range(L)])
    g_w_o = jnp.stack([big_grads[l][4] for l in range(L)])

    grads = dict(norm_g=g_norm_g, w_in=g_w_in, ln_g=g_ln_g, ln_b=g_ln_b, w_s=g_w_s, b_s=g_b_s, conv_w=g_conv_w,
                 conv_b=g_conv_b, w_pool=g_w_pool, pool_scale=g_pool_scale, w_pa=g_w_pa, w_pb=g_w_pb, w_pc=g_w_pc,
                 w_o=g_w_o, final_g=g_final_g)
    weights = dict(norm_g=norm_g, w_in=w_in, ln_g=ln_g, ln_b=ln_b, w_s=w_s, b_s=b_s, conv_w=conv_w, conv_b=conv_b,
                   w_pool=w_pool, pool_scale=pool_scale, w_pa=w_pa, w_pb=w_pb, w_pc=w_pc, w_o=w_o, final_g=final_g)
    ms = dict(norm_g=m_norm_g, w_in=m_w_in, ln_g=m_ln_g, ln_b=m_ln_b, w_s=m_w_s, b_s=m_b_s, conv_w=m_conv_w,
              conv_b=m_conv_b, w_pool=m_w_pool, pool_scale=m_pool_scale, w_pa=m_w_pa, w_pb=m_w_pb, w_pc=m_w_pc,
              w_o=m_w_o, final_g=m_final_g)
    vs = dict(norm_g=v_norm_g, w_in=v_w_in, ln_g=v_ln_g, ln_b=v_ln_b, w_s=v_w_s, b_s=v_b_s, conv_w=v_conv_w,
              conv_b=v_conv_b, w_pool=v_w_pool, pool_scale=v_pool_scale, w_pa=v_w_pa, w_pb=v_w_pb, w_pc=v_w_pc,
              w_o=v_w_o, final_g=v_final_g)
    order = ["norm_g", "w_in", "ln_g", "ln_b", "w_s", "b_s", "conv_w", "conv_b", "w_pool", "pool_scale", "w_pa", "w_pb",
             "w_pc", "w_o", "final_g"]

    delta, new_m, new_v = {}, {}, {}
    big = ["w_in", "w_pa", "w_pb", "w_pc", "w_o"]
    tr = lambda a: jnp.swapaxes(a, 1, 2)
    for nm in big:
        into = tr if nm == "w_in" else (lambda a: a)
        shp = into(weights[nm]).shape
        two = lambda a: a.reshape(-1, shp[-1])
        g2 = two(g_w_in_t) if nm == "w_in" else two(grads[nm])
        d, mn, vn = _adamw(two(into(weights[nm])), g2, two(into(ms[nm])), two(into(vs[nm])), f"adamw_{nm}")
        delta[nm], new_m[nm], new_v[nm] = (into(a.reshape(shp)) for a in (d, mn, vn))
    rest = [nm for nm in order if nm not in big and nm != "conv_w"] + ["conv_w"]
    cat = lambda src: _as_lanes(jnp.concatenate([src[nm].reshape(-1) for nm in rest]))
    flat = [a.reshape(-1) for a in _adamw(cat(weights), cat(grads), cat(ms), cat(vs), "adamw_small")]
    off = 0
    for nm in rest:
        shp = weights[nm].shape
        n = math.prod(shp)
        delta[nm], new_m[nm], new_v[nm] = (a[off:off + n].reshape(shp) for a in flat)
        off += n

    return (loss, grad_x, *[grads[nm] for nm in order], *[delta[nm] for nm in order],
            *[new_m[nm] for nm in order], *[new_v[nm] for nm in order])
```

```python
import functools
import math

import numpy as np
import jax
import jax.numpy as jnp
from jax import lax
from jax.experimental import pallas as pl
from jax.experimental.pallas import tpu as pltpu

F32 = jnp.float32
BF16 = jnp.bfloat16
SDS = jax.ShapeDtypeStruct
MESH = pl.DeviceIdType.MESH

SEG = 512
CHUNK = 128
GROUPS = 8
HEAD = SEG // GROUPS
POOL_WINDOWS = (2, 4, 8, 16)
POOL_GROUP = SEG // len(POOL_WINDOWS)
CONV_TAPS = 3
HALO = 16
RMS_EPS = 1e-6
LN_EPS = 1e-5
ADAM_LR, ADAM_B1, ADAM_B2, ADAM_EPS, ADAM_WD, ADAM_STEP = 0.001, 0.9, 0.999, 1e-08, 0.01, 10

O_U, O_V, O_ZA, O_XB, O_BG, O_CG, O_ZB, O_XC, O_ZC, O_G = (SEG * i for i in range(10))

N_DEV = 8
N_CHIP = 4
LANES = 128
VMEM_LIMIT = 48 * 1024 * 1024
ADAMW_WHOLE_BYTES = 2 * 1024 * 1024
INPROJ_ROWS = 1024
BWD_X_ROWS = 512
WGRAD_ROWS = 2048
RIDERS_FROM_CHUNK = 2
LAST_PIECE_SHARE = (5, 16)


def _cparams(sem=None, **kw):
    return pltpu.CompilerParams(dimension_semantics=sem, vmem_limit_bytes=VMEM_LIMIT, **kw)


def _pick(total, target, mult):
    best = None
    for d in range(mult, min(total, target) + 1, mult):
        if total % d == 0:
            best = d
    assert best is not None, (total, target, mult)
    return best


def _dot(a, b):
    return jnp.dot(a, b, preferred_element_type=F32)


def _dot_nt(a, b):
    return lax.dot_general(a, b, (((1,), (1,)), ((), ())), preferred_element_type=F32)


def _dot_tn(a, b):
    return lax.dot_general(a, b, (((0,), (0,)), ((), ())), preferred_element_type=F32)


def _zero(ref):
    ref[...] = jnp.zeros(ref.shape, ref.dtype)


def _sigmoid(x):
    return 1.0 / (1.0 + jnp.exp(-x))


_GELU_C = math.sqrt(2.0 / math.pi)


def _gelu(x):
    t = jnp.tanh(_GELU_C * (x + 0.044715 * x * x * x))
    return 0.5 * x * (1.0 + t), t


def _gelu_grad(x, t):
    return 0.5 * (1.0 + t) + 0.5 * x * (1.0 - t * t) * _GELU_C * (1.0 + 3.0 * 0.044715 * x * x)


def _me():
    return lax.axis_index("x"), lax.axis_index("y"), lax.axis_index("c")


def _inproj(x, norm_g, win_t, name, riders=()):
    T, D = x.shape
    N = win_t.shape[0]
    bt = _pick(T, INPROJ_ROWS, 16)
    bn = _pick(N, 1536, LANES)
    grid = (T // bt, N // bn)

    def compute(x_ref, g_ref, w_ref, p_ref, h_ref, hs_ref):
        @pl.when(pl.program_id(1) == 0)
        def _():
            xv = x_ref[...]
            rstd = lax.rsqrt(jnp.mean(xv * xv, axis=-1, keepdims=True) + RMS_EPS)
            hb = (xv * rstd * g_ref[...]).astype(BF16)
            hs_ref[...] = hb
            h_ref[...] = hb

        p_ref[...] = _dot_nt(hs_ref[...], w_ref[...]).astype(BF16)

    return _host_call(
        compute, name, grid, riders,
        inputs=[x, norm_g.reshape(1, D), win_t],
        in_specs=[pl.BlockSpec((bt, D), lambda i, j: (i, 0)),
                  pl.BlockSpec((1, D), lambda i, j: (0, 0)),
                  pl.BlockSpec((bn, D), lambda i, j: (j, 0))],
        out_specs=[pl.BlockSpec((bt, bn), lambda i, j: (i, j)),
                   pl.BlockSpec((bt, D), lambda i, j: (i, 0))],
        out_shape=[SDS((T, N), BF16), SDS((T, D), BF16)],
        scratch_shapes=[pltpu.VMEM((bt, D), BF16)])


def _inproj_gathering(x, norm_g, w_loc, chip_order, name, riders=()):
    T, D = x.shape
    n = w_loc.shape[0]
    N = n * N_DEV
    cw = 2 * n
    bt = _pick(T, INPROJ_ROWS, 16)
    nt = T // bt
    r_in, r_out, r_sems, copies = _rider_plan(riders)
    n_rin, n_rout = len(r_in), len(r_out)

    def body(q_ref, x_ref, g_ref, wloc_ref, *rest):
        rins = rest[:n_rin]
        p_ref, h_ref, wfull_ref = rest[n_rin:n_rin + 3]
        routs = rest[n_rin + 3:n_rin + 3 + n_rout]
        hs_ref, wbuf, send_sems, recv_sems, loc_sems = rest[n_rin + 3 + n_rout:n_rin + 8 + n_rout]
        rsems = rest[n_rin + 8 + n_rout:]
        j, i = pl.program_id(0), pl.program_id(1)
        cx, cy, cc = _me()
        sibling = (cx, cy, 1 - cc)

        def rows(k):
            return wfull_ref.at[pl.ds(pl.multiple_of(k * n, 8), n)]

        def shard_copy(slot, src, k, to):
            return pltpu.make_async_remote_copy(src_ref=src, dst_ref=rows(k), send_sem=send_sems.at[slot],
                                                recv_sem=recv_sems.at[slot], device_id=to, device_id_type=MESH)

        me = 4 * cx + 2 * cy + cc
        place_mine = pltpu.make_async_copy(wloc_ref, rows(me), loc_sems.at[0])
        sends = [shard_copy(0, wloc_ref, me, sibling)]
        for jj in (1, 2, 3):
            sends.append(shard_copy(jj, wloc_ref, me, (*_chip_peer(cx, cy, jj), cc)))

        def forward(jj):
            px, py = _chip_peer(cx, cy, jj)
            k = 4 * px + 2 * py + cc
            return shard_copy(3 + jj, rows(k), k, sibling)

        def load_chunk(q):
            cp = pltpu.make_async_copy(wfull_ref.at[pl.ds(pl.multiple_of(q * cw, 8), cw)], wbuf, loc_sems.at[1])
            cp.start()
            cp.wait()

        keep_h = pltpu.make_async_copy(hs_ref, h_ref, loc_sems.at[2])

        @pl.when((j == 0) & (i == 0))
        def _():
            place_mine.start()
            for cp in sends:
                cp.start()
            place_mine.wait()
            sends[0].wait_recv()
            load_chunk(q_ref[0])

        for jj in (1, 2, 3):
            @pl.when((j == jj) & (i == 0))
            def _(jj=jj):
                sends[jj].wait_recv()
                fwd = forward(jj)
                fwd.start()
                fwd.wait_recv()
                load_chunk(q_ref[jj])
                if jj == RIDERS_FROM_CHUNK:
                    for cp in copies(rins, routs, rsems):
                        cp.start()

        tok = pl.ds(pl.multiple_of(i * bt, bt), bt)

        @pl.when(j == 0)
        def _():
            xv = x_ref[...]
            rstd = lax.rsqrt(jnp.mean(xv * xv, axis=-1, keepdims=True) + RMS_EPS)
            hs_ref[tok, :] = (xv * rstd * g_ref[...]).astype(BF16)

        @pl.when((j == 0) & (i == nt - 1))
        def _():
            keep_h.start()

        p_ref[...] = _dot_nt(hs_ref[tok, :], wbuf[...]).astype(BF16)

        @pl.when((j == N_CHIP - 1) & (i == nt - 1))
        def _():
            keep_h.wait()
            for cp in sends:
                cp.wait_send()
            for jj in (1, 2, 3):
                forward(jj).wait_send()
            for cp in copies(rins, routs, rsems):
                cp.wait()

    res = pl.pallas_call(
        body, name=name,
        grid_spec=pltpu.PrefetchScalarGridSpec(
            num_scalar_prefetch=1, grid=(N_CHIP, nt),
            in_specs=[pl.BlockSpec((bt, D), lambda j, i, q: (jnp.where(j == 0, i, nt - 1), 0)),
                      pl.BlockSpec((1, D), lambda j, i, q: (0, 0)), _ANY] + [_ANY] * n_rin,
            out_specs=[pl.BlockSpec((bt, cw), lambda j, i, q: (i, q[j])), _ANY, _ANY] + [_ANY] * n_rout,
            scratch_shapes=[pltpu.VMEM((T, D), BF16), pltpu.VMEM((cw, D), BF16), pltpu.SemaphoreType.DMA((7,)),
                            pltpu.SemaphoreType.DMA((7,)), pltpu.SemaphoreType.DMA((3,))] + r_sems),
        out_shape=[SDS((T, N), BF16), SDS((T, D), BF16), SDS((N, D), BF16)] + r_out,
        input_output_aliases=_rider_aliases(riders, 4, 3),
        compiler_params=_cparams(("arbitrary", "arbitrary")),
    )(chip_order, x, norm_g.reshape(1, D), w_loc, *r_in)
    return res[:3], _split_riders(riders, res[3:])


C_LNG, C_LNB, C_CW0, C_CW1, C_CW2, C_CB, C_PS = range(7)
C_ROWS = 8


def _pool_bands(R, anticausal):
    t = np.arange(R)[:, None]
    s = np.arange(R + CHUNK)[None, :]
    bands = [((s >= t) & (s < t + w)) if anticausal else ((s > t + CHUNK - w) & (s <= t + CHUNK)) for w in POOL_WINDOWS]
    return jnp.asarray(np.stack(bands), dtype=BF16)


def _mixers(p_ref, hxb_ref, hcg_ref, hxc_ref, cv, bsb_ref, wcat_ref, wpool_ref, band_ref, extb,
            first, blk, R, need_grad):
    def seg(lo):
        return p_ref[:, lo:lo + SEG].astype(F32)

    u, v, za = seg(O_U), seg(O_V), seg(O_ZA)
    xb, bg, cg, zb = seg(O_XB), seg(O_BG), seg(O_CG), seg(O_ZB)
    xc, zc = seg(O_XC), seg(O_ZC)
    out = {}

    ug, tu = _gelu(u)
    vg, tv = _gelu(v)
    mu = jnp.mean(vg, axis=-1, keepdims=True)
    vcen = vg - mu
    rs = lax.rsqrt(jnp.mean(vcen * vcen, axis=-1, keepdims=True) + LN_EPS)
    vhat = vcen * rs
    vn = (vhat * cv[C_LNG:C_LNG + 1, :] + cv[C_LNB:C_LNB + 1, :]).astype(BF16)
    lane_group = lax.broadcasted_iota(jnp.int32, (CHUNK, SEG), 1) // HEAD
    zero_b = jnp.zeros((CHUNK, SEG), BF16)
    sgs = []
    for ci in range(R // CHUNK):
        vc = vn[ci * CHUNK:(ci + 1) * CHUNK]
        vst = jnp.concatenate([jnp.where(lane_group == g, vc, zero_b) for g in range(GROUPS)], axis=0)
        sgs.append(_dot(wcat_ref[...], vst) + bsb_ref[...])
    sg = sgs[0] if len(sgs) == 1 else jnp.concatenate(sgs, axis=0)
    a_out = ug * sg
    sa = _sigmoid(za)
    out["a"] = a_out * (za * sa)

    cx = cg * xb
    halo_b = hcg_ref[...].astype(F32) * hxb_ref[...].astype(F32)
    extb[0:HALO, :] = jnp.where(first, 0.0, halo_b)
    extb[HALO:HALO + R, :] = cx
    cx1 = extb[pl.ds(HALO - 1, R), :]
    cx2 = extb[pl.ds(HALO - 2, R), :]
    yconv = (cv[C_CW0:C_CW0 + 1, :] * cx2 + cv[C_CW1:C_CW1 + 1, :] * cx1
             + cv[C_CW2:C_CW2 + 1, :] * cx + cv[C_CB:C_CB + 1, :])
    b_out = bg * yconv
    sb = _sigmoid(zb)
    out["b"] = b_out * (zb * sb)

    halo_c = hxc_ref[...]
    xc_ext = jnp.concatenate([jnp.zeros((CHUNK - HALO, SEG), BF16), jnp.where(first, jnp.zeros_like(halo_c), halo_c),
                              p_ref[:, O_XC:O_XC + SEG]], axis=0)
    tpos = blk * R + lax.broadcasted_iota(jnp.int32, (R, POOL_GROUP), 0) + 1
    pooled, invs, pws = [], [], []
    for gi, w in enumerate(POOL_WINDOWS):
        lo = gi * POOL_GROUP
        win = _dot(band_ref[gi], xc_ext[:, lo:lo + POOL_GROUP])
        inv = 1.0 / jnp.minimum(tpos, w).astype(F32)
        pg = (win * inv - xc[:, lo:lo + POOL_GROUP]).astype(BF16)
        pooled.append(pg)
        invs.append(inv)
        pws.append(_dot(pg, wpool_ref[gi]))
    pw = jnp.concatenate(pws, axis=1)
    c_out = pw * cv[C_PS:C_PS + 1, :]
    sc = _sigmoid(zc)
    out["c"] = c_out * (zc * sc)

    if need_grad:
        out.update(u=u, v=v, tu=tu, tv=tv, ug=ug, sg=sg, a_out=a_out, za=za, sa=sa,
                   rs=rs, vhat=vhat, vn=vn, lane_group=lane_group, zero_b=zero_b,
                   xb=xb, bg=bg, cg=cg, cx=cx, cx1=cx1, cx2=cx2, yconv=yconv, b_out=b_out, zb=zb, sb=sb,
                   pooled=pooled, invs=invs, pw=pw, c_out=c_out, zc=zc, sc=sc)
    return out


def _halo_specs(R, nb, rev):
    step = R // HALO

    def mk(col):
        def imap(i):
            b = (nb - 1 - i) if rev else i
            return (jnp.maximum(b * step - 1, 0), col)
        return pl.BlockSpec((HALO, SEG), imap)

    return [mk(O_XB // SEG), mk(O_CG // SEG), mk(O_XC // SEG)]


def _const_spec(shape):
    nd = len(shape)
    return pl.BlockSpec(shape, lambda i: (0,) * nd, pipeline_mode=pl.Buffered(1))


MIX_FWD_ROWS = 512
MIX_BWD_ROWS = 256


def _mix_block_rows(T, target):
    return _pick(T, target, CHUNK)


def _mix_fwd(p, x, lw, name, riders=()):
    T, D = x.shape
    N = p.shape[1]
    R = _mix_block_rows(T, MIX_FWD_ROWS)
    nb = T // R

    def body(p_ref, hxb, hcg, hxc, x_ref, cv_ref, bsb_ref, wcat_ref, wpool_ref, band_ref, wpa_ref, wpb_ref, wpc_ref,
             wo_ref, xo_ref, ya_ref, yb_ref, yc_ref, extb):
        i = pl.program_id(0)
        cv = cv_ref[...]
        r = _mixers(p_ref, hxb, hcg, hxc, cv, bsb_ref, wcat_ref, wpool_ref, band_ref, extb,
                    i == 0, i, R, False)
        merged = None
        for k, (act, w_ref, y_ref) in enumerate(((r["a"], wpa_ref, ya_ref), (r["b"], wpb_ref, yb_ref),
                                                 (r["c"], wpc_ref, yc_ref))):
            y = _dot(act.astype(BF16), w_ref[...]).astype(BF16)
            y_ref[...] = y
            term = _sigmoid(p_ref[:, O_G + k * D:O_G + (k + 1) * D]) * y
            merged = term if merged is None else merged + term
        xo_ref[...] = x_ref[...] + _dot(merged, wo_ref[...])

    row = lambda w: pl.BlockSpec((R, w), lambda i: (i, 0))
    consts = [lw["cvec"], lw["bsb"], lw["wcat"], lw["wpool"], _pool_bands(R, False), lw["wpa"], lw["wpb"], lw["wpc"],
              lw["wo"]]
    return _host_call(
        body, name, (nb,), riders,
        inputs=[p, p, p, p, x, *consts],
        in_specs=[row(N)] + _halo_specs(R, nb, False) + [row(D)] + [_const_spec(c.shape) for c in consts],
        out_specs=[row(D), row(D), row(D), row(D)],
        out_shape=[SDS((T, D), F32), SDS((T, D), BF16), SDS((T, D), BF16), SDS((T, D), BF16)],
        scratch_shapes=[pltpu.VMEM((HALO + R, SEG), F32)])


def _loss_head(x, final_g, target, name):
    T, D = x.shape
    bt = _pick(T, INPROJ_ROWS, 8)

    def body(x_ref, g_ref, t_ref, dx_ref, loss_ref, dg_ref):
        @pl.when(pl.program_id(0) == 0)
        def _():
            _zero(loss_ref)
            _zero(dg_ref)

        xv = x_ref[...]
        g = g_ref[...]
        rstd = lax.rsqrt(jnp.mean(xv * xv, axis=-1, keepdims=True) + RMS_EPS)
        xhat = xv * rstd
        err = xhat * g - t_ref[...]
        part = 0.5 * jnp.sum(jnp.sum(err * err, axis=-1, keepdims=True), axis=0, keepdims=True) / D
        loss_ref[...] += jnp.broadcast_to(part, loss_ref.shape)
        dy = err * (1.0 / D)
        dg_ref[0:1, :] += jnp.sum(dy * xhat, axis=0, keepdims=True)
        dxn = dy * g
        dx_ref[...] = rstd * (dxn - xhat * jnp.mean(dxn * xhat, axis=-1, keepdims=True))

    return pl.pallas_call(
        body, name=name, grid=(T // bt,),
        in_specs=[pl.BlockSpec((bt, D), lambda i: (i, 0)), _const_spec((1, D)), pl.BlockSpec((bt, D), lambda i: (i, 0))],
        out_specs=[pl.BlockSpec((bt, D), lambda i: (i, 0)), _const_spec((8, LANES)), _const_spec((8, D))],
        out_shape=[SDS((T, D), F32), SDS((8, LANES), F32), SDS((8, D), F32)],
        compiler_params=_cparams(("arbitrary",)),
    )(x, final_g.reshape(1, D), target)


V_LNG, V_LNB, V_CB, V_PS, V_CW0, V_CW1, V_CW2 = range(7)


def _mix_bwd(p, dxo, ya, yb, yc, lw, name):
    T, D = dxo.shape
    N = p.shape[1]
    R = _mix_block_rows(T, MIX_BWD_ROWS)
    nb = T // R

    def body(p_ref, hxb, hcg, hxc, dxo_ref, ya_ref, yb_ref, yc_ref, cv_ref, bsb_ref, wcat_ref, wcatt_ref,
             wpool_ref, band_ref, bandt_ref, wpa_ref, wpb_ref, wpc_ref, wo_ref,
             dp_ref, acts_ref, mrg_ref, dys_ref, gwc_ref, gbs_ref, gwpool_ref, gvec_ref,
             extb, extdy, cdy, cq, bsacc):
        i = pl.program_id(0)
        blk = nb - 1 - i

        @pl.when(i == 0)
        def _():
            for ref in (gwc_ref, gwpool_ref, gvec_ref, cdy, cq, bsacc):
                _zero(ref)

        cv = cv_ref[...]
        r = _mixers(p_ref, hxb, hcg, hxc, cv, bsb_ref, wcat_ref, wpool_ref, band_ref, extb,
                    blk == 0, blk, R, True)

        dxo_b = dxo_ref[...].astype(BF16)
        dm = _dot_nt(dxo_b, wo_ref[...]).astype(BF16)
        ys = [ya_ref[...], yb_ref[...], yc_ref[...]]
        sig = [_sigmoid(p_ref[:, O_G + k * D:O_G + (k + 1) * D]) for k in range(3)]
        mrg_ref[...] = sig[0] * ys[0] + sig[1] * ys[1] + sig[2] * ys[2]
        dacts = []
        for k, (act, w_ref) in enumerate(((r["a"], wpa_ref), (r["b"], wpb_ref), (r["c"], wpc_ref))):
            dyk = dm * sig[k]
            dp_ref[:, O_G + k * D:O_G + (k + 1) * D] = dyk * ys[k] * (1.0 - sig[k])
            acts_ref[:, k * SEG:(k + 1) * SEG] = act.astype(BF16)
            dys_ref[:, k * D:(k + 1) * D] = dyk
            dacts.append(_dot_nt(dyk, w_ref[...]))
        da, db, dc = dacts

        def silu_bwd(dact, pre, z, s):
            return dact * (z * s), dact * pre * (s * (1.0 + z * (1.0 - s)))

        d_aout, dza = silu_bwd(da, r["a_out"], r["za"], r["sa"])
        dp_ref[:, O_ZA:O_ZA + SEG] = dza.astype(BF16)
        dp_ref[:, O_U:O_U + SEG] = (d_aout * r["sg"] * _gelu_grad(r["u"], r["tu"])).astype(BF16)
        d_sg = d_aout * r["ug"]
        dvns = []
        for ci in range(R // CHUNK):
            dsc = d_sg[ci * CHUNK:(ci + 1) * CHUNK]
            bsacc[...] += dsc
            dsc_b = dsc.astype(BF16)
            dst = jnp.concatenate([jnp.where(r["lane_group"] == g, dsc_b, r["zero_b"]) for g in range(GROUPS)], axis=0)
            dvns.append(_dot(wcatt_ref[...], dst))
            gwc_ref[...] += _dot_nt(dst, r["vn"][ci * CHUNK:(ci + 1) * CHUNK])
        d_vn = dvns[0] if len(dvns) == 1 else jnp.concatenate(dvns, axis=0)
        vhat = r["vhat"]
        gvec_ref[V_LNG:V_LNG + 1, :] += jnp.sum(d_vn * vhat, axis=0, keepdims=True)
        gvec_ref[V_LNB:V_LNB + 1, :] += jnp.sum(d_vn, axis=0, keepdims=True)
        d_vhat = d_vn * cv[C_LNG:C_LNG + 1, :]
        d_vg = r["rs"] * (d_vhat - jnp.mean(d_vhat, axis=-1, keepdims=True)
                          - vhat * jnp.mean(d_vhat * vhat, axis=-1, keepdims=True))
        dp_ref[:, O_V:O_V + SEG] = (d_vg * _gelu_grad(r["v"], r["tv"])).astype(BF16)

        d_bout, dzb = silu_bwd(db, r["b_out"], r["zb"], r["sb"])
        dp_ref[:, O_ZB:O_ZB + SEG] = dzb.astype(BF16)
        dp_ref[:, O_BG:O_BG + SEG] = (d_bout * r["yconv"]).astype(BF16)
        d_y = d_bout * r["bg"]
        gvec_ref[V_CB:V_CB + 1, :] += jnp.sum(d_y, axis=0, keepdims=True)
        gvec_ref[V_CW0:V_CW0 + 1, :] += jnp.sum(d_y * r["cx2"], axis=0, keepdims=True)
        gvec_ref[V_CW1:V_CW1 + 1, :] += jnp.sum(d_y * r["cx1"], axis=0, keepdims=True)
        gvec_ref[V_CW2:V_CW2 + 1, :] += jnp.sum(d_y * r["cx"], axis=0, keepdims=True)
        extdy[0:R, :] = d_y
        extdy[R:R + HALO, :] = cdy[...]
        d_cx = (cv[C_CW2:C_CW2 + 1, :] * d_y + cv[C_CW1:C_CW1 + 1, :] * extdy[pl.ds(1, R), :]
                + cv[C_CW0:C_CW0 + 1, :] * extdy[pl.ds(2, R), :])
        cdy[...] = d_y[0:HALO]
        dp_ref[:, O_CG:O_CG + SEG] = (d_cx * r["xb"]).astype(BF16)
        dp_ref[:, O_XB:O_XB + SEG] = (d_cx * r["cg"]).astype(BF16)

        d_cout, dzc = silu_bwd(dc, r["c_out"], r["zc"], r["sc"])
        dp_ref[:, O_ZC:O_ZC + SEG] = dzc.astype(BF16)
        gvec_ref[V_PS:V_PS + 1, :] += jnp.sum(d_cout * r["pw"], axis=0, keepdims=True)
        d_pw = (d_cout * cv[C_PS:C_PS + 1, :]).astype(BF16)
        dpool, scaled = [], []
        for gi, w in enumerate(POOL_WINDOWS):
            lo = gi * POOL_GROUP
            dpw_g = d_pw[:, lo:lo + POOL_GROUP]
            gwpool_ref[lo:lo + POOL_GROUP, :] += _dot_tn(r["pooled"][gi], dpw_g)
            dpg = _dot_nt(dpw_g, wpool_ref[gi])
            dpool.append(dpg)
            scaled.append((dpg * r["invs"][gi]).astype(BF16))
        q = jnp.concatenate(scaled, axis=1)
        q_ext = jnp.concatenate([q, cq[...], jnp.zeros((CHUNK - HALO, SEG), BF16)], axis=0)
        for gi, w in enumerate(POOL_WINDOWS):
            lo = gi * POOL_GROUP
            acc = _dot(bandt_ref[gi], q_ext[:, lo:lo + POOL_GROUP])
            dp_ref[:, O_XC + lo:O_XC + lo + POOL_GROUP] = (acc - dpool[gi]).astype(BF16)
        cq[...] = q[0:HALO]

        @pl.when(i == nb - 1)
        def _():
            rr = lax.broadcasted_iota(jnp.int32, gwc_ref.shape, 0) % CHUNK
            cc = lax.broadcasted_iota(jnp.int32, gwc_ref.shape, 1)
            gwc_ref[...] = jnp.where(cc <= rr, gwc_ref[...], 0.0)
            acc = bsacc[...]
            hi = acc.astype(BF16)
            lo_ = (acc - hi.astype(F32)).astype(BF16)
            sel = (lax.broadcasted_iota(jnp.int32, (SEG, LANES), 0) // HEAD
                   == lax.broadcasted_iota(jnp.int32, (SEG, LANES), 1)).astype(BF16)
            gbs_ref[...] = _dot(hi, sel) + _dot(lo_, sel)

    row = lambda w: pl.BlockSpec((R, w), lambda i: (nb - 1 - i, 0))
    consts = [lw["cvec"], lw["bsb"], lw["wcat"], lw["wcatt"], lw["wpool"], _pool_bands(R, False), _pool_bands(R, True),
              lw["wpa"], lw["wpb"], lw["wpc"], lw["wo"]]
    acc_shapes = [(GROUPS * CHUNK, CHUNK), (CHUNK, LANES), (SEG, POOL_GROUP), (8, SEG)]
    row_widths = [N, 3 * SEG, D, 3 * D]
    return pl.pallas_call(
        body, name=name, grid=(nb,),
        in_specs=([row(N)] + _halo_specs(R, nb, True) + [row(D), row(D), row(D), row(D)]
                  + [_const_spec(c.shape) for c in consts]),
        out_specs=[row(w) for w in row_widths] + [_const_spec(s) for s in acc_shapes],
        out_shape=[SDS((T, w), BF16) for w in row_widths] + [SDS(s, F32) for s in acc_shapes],
        scratch_shapes=[pltpu.VMEM((HALO + R, SEG), F32)] * 2
        + [pltpu.VMEM((HALO, SEG), F32), pltpu.VMEM((HALO, SEG), BF16), pltpu.VMEM((CHUNK, SEG), F32)],
        compiler_params=_cparams(("arbitrary",)),
    )(p, p, p, p, dxo, ya, yb, yc, *consts)


def _proj_wgrad(acts, merged, dys, dxo, name, riders=()):
    T, D = dxo.shape
    bk = _pick(T, WGRAD_ROWS // 2, 16)

    def body(a_ref, m_ref, dy_ref, dxo_ref, gwpa_ref, gwpb_ref, gwpc_ref, gwo_ref):
        @pl.when(pl.program_id(0) == 0)
        def _():
            for ref in (gwpa_ref, gwpb_ref, gwpc_ref, gwo_ref):
                _zero(ref)

        gwo_ref[...] += _dot_tn(m_ref[...], dxo_ref[...].astype(BF16))
        for k, ref in enumerate((gwpa_ref, gwpb_ref, gwpc_ref)):
            ref[...] += _dot_tn(a_ref[:, k * SEG:(k + 1) * SEG], dy_ref[:, k * D:(k + 1) * D])

    row = lambda w: pl.BlockSpec((bk, w), lambda i: (i, 0))
    shapes = [(SEG, D), (SEG, D), (SEG, D), (D, D)]
    return _host_call(
        body, name, (T // bk,), riders,
        inputs=[acts, merged, dys, dxo],
        in_specs=[row(3 * SEG), row(D), row(3 * D), row(D)],
        out_specs=[_const_spec(s) for s in shapes], out_shape=[SDS(s, F32) for s in shapes],
        scratch_shapes=[])


def _inproj_token_blocks(T):
    return T // _pick(T, BWD_X_ROWS, 16)


def _inproj_bwd_x(dp, win_t, x, norm_g, dxo, name, riders=(), blocks=None, fill=None):
    T, D = x.shape
    N = dp.shape[1]
    bt = _pick(T, BWD_X_ROWS, 16)
    b0, nblk = blocks if blocks else (0, T // bt)

    def compute(dp_ref, w_ref, x_ref, g_ref, dxo_ref, *rest):
        dx_ref, dg_ref = rest[-2:]

        @pl.when(pl.program_id(0) == 0)
        def _():
            _zero(dg_ref)

        dh = _dot(dp_ref[...], w_ref[...])
        xv = x_ref[...]
        rstd = lax.rsqrt(jnp.mean(xv * xv, axis=-1, keepdims=True) + RMS_EPS)
        xhat = xv * rstd
        dg_ref[0:1, :] += jnp.sum(dh * xhat, axis=0, keepdims=True)
        dxn = dh * g_ref[...]
        dx_ref[...] = dxo_ref[...] + rstd * (dxn - xhat * jnp.mean(dxn * xhat, axis=-1, keepdims=True))

    rows = pl.BlockSpec((bt, D), lambda i: (i + b0, 0))
    return _host_call(
        compute, name, (nblk,), riders,
        inputs=[dp, win_t, x, norm_g.reshape(1, D), dxo] + ([] if fill is None else [fill]),
        in_specs=[pl.BlockSpec((bt, N), lambda i: (i + b0, 0)), _const_spec((N, D)), rows, _const_spec((1, D)), rows]
        + ([] if fill is None else [_ANY]),
        out_specs=[rows, _const_spec((8, D))],
        out_shape=[SDS((T, D), F32), SDS((8, D), F32)],
        scratch_shapes=[],
        aliases={} if fill is None else {5: 0})


def _inproj_bwd_w(dp, h, name, riders=()):
    T, N = dp.shape
    D = h.shape[1]
    bn = _pick(N, 1536, LANES)
    bk = _pick(T, WGRAD_ROWS, 16)
    nk = T // bk

    def compute(dp_ref, h_ref, o_ref):
        @pl.when(pl.program_id(1) == 0)
        def _():
            _zero(o_ref)

        o_ref[...] += _dot_tn(dp_ref[...], h_ref[...])

    return _host_call(
        compute, name, (N // bn, nk), riders,
        inputs=[dp, h],
        in_specs=[pl.BlockSpec((bk, bn), lambda j, k: (k, j)), pl.BlockSpec((bk, D), lambda j, k: (k, 0))],
        out_specs=[pl.BlockSpec((bn, D), lambda j, k: (j, 0))],
        out_shape=[SDS((N, D), F32)],
        scratch_shapes=[])


def _chip_peer(x, y, j):
    px = (1 - x) if (j >> 1) else x
    py = (1 - y) if (j & 1) else y
    return px, py


def _blk(ref, kind, k, n):
    if kind == "rows":
        return ref.at[pl.ds(pl.multiple_of(k * n, 8), n)]
    return ref.at[:, pl.ds(pl.multiple_of(k * n, LANES), n)]


class _Exchange:
    def __init__(self, srcs, out_shapes, n_sems, build, alias=None):
        self.srcs, self.out_shapes, self.n_sems, self.build = list(srcs), list(out_shapes), n_sems, build
        self.alias = dict(alias or {})


def _rider_aliases(riders, first_in, first_out):
    out, i, o = {}, first_in, first_out
    for e in riders:
        out.update({i + s: o + d for s, d in e.alias.items()})
        i, o = i + len(e.srcs), o + len(e.out_shapes)
    return out


def _rider_plan(riders):
    inputs = [s for e in riders for s in e.srcs]
    out_shapes = [o for e in riders for o in e.out_shapes]
    sems = [pltpu.SemaphoreType.DMA((e.n_sems,)) for e in riders for _ in range(2)]

    def copies(in_refs, out_refs, sem_refs):
        cps, i, o = [], 0, 0
        for k, e in enumerate(riders):
            ni, no = len(e.srcs), len(e.out_shapes)
            cps += e.build(in_refs[i:i + ni], out_refs[o:o + no], sem_refs[2 * k], sem_refs[2 * k + 1])
            i, o = i + ni, o + no
        return cps

    return inputs, out_shapes, sems, copies


_ANY = pl.BlockSpec(memory_space=pl.ANY)


def _host_call(compute, name, grid, riders, inputs, in_specs, out_specs, out_shape, scratch_shapes, aliases=None):
    r_in, r_out, r_sems, copies = _rider_plan(riders)
    ni, no, ns = len(inputs), len(out_shape), len(scratch_shapes)

    def body(*refs):
        ins, rins = refs[:ni], refs[ni:ni + len(r_in)]
        outs = refs[ni + len(r_in):ni + len(r_in) + no]
        routs = refs[ni + len(r_in) + no:ni + len(r_in) + no + len(r_out)]
        scr = refs[ni + len(r_in) + no + len(r_out):]
        first = functools.reduce(lambda a, b: a & b, [pl.program_id(d) == 0 for d in range(len(grid))])
        last = functools.reduce(lambda a, b: a & b, [pl.program_id(d) == grid[d] - 1 for d in range(len(grid))])
        if riders:
            @pl.when(first)
            def _():
                for cp in copies(rins, routs, scr[ns:]):
                    cp.start()

        compute(*ins, *outs, *scr[:ns])

        if riders:
            @pl.when(last)
            def _():
                for cp in copies(rins, routs, scr[ns:]):
                    cp.wait()

    res = pl.pallas_call(
        body, name=name, grid=grid,
        in_specs=list(in_specs) + [_ANY] * len(r_in),
        out_specs=list(out_specs) + [_ANY] * len(r_out),
        out_shape=list(out_shape) + r_out,
        scratch_shapes=list(scratch_shapes) + r_sems,
        input_output_aliases={**(aliases or {}), **_rider_aliases(riders, ni, no)},
        compiler_params=_cparams(("arbitrary",) * len(grid)),
    )(*inputs, *r_in)
    return res[:no], _split_riders(riders, res[no:])


def _split_riders(riders, flat):
    out, o = [], 0
    for e in riders:
        out.append(list(flat[o:o + len(e.out_shapes)]))
        o += len(e.out_shapes)
    return out


def _run_exchange(ex, name):
    n_in, n_out = len(ex.srcs), len(ex.out_shapes)

    def body(*refs):
        cps = ex.build(refs[:n_in], refs[n_in:n_in + n_out], refs[n_in + n_out], refs[n_in + n_out + 1])
        for cp in cps:
            cp.start()
        for cp in cps:
            cp.wait()

    return pl.pallas_call(
        body, name=name,
        in_specs=[_ANY] * n_in, out_specs=[_ANY] * n_out, out_shape=ex.out_shapes,
        input_output_aliases=ex.alias,
        scratch_shapes=[pltpu.SemaphoreType.DMA((ex.n_sems,)), pltpu.SemaphoreType.DMA((ex.n_sems,))],
        compiler_params=pltpu.CompilerParams(has_side_effects=True),
    )(*ex.srcs)


def _gather_sizes(shards, kinds):
    sizes = [s.shape[0] if k == "rows" else s.shape[1] for s, k in zip(shards, kinds)]
    fulls = [SDS((s.shape[0] * N_DEV,) + s.shape[1:], s.dtype) if k == "rows"
             else SDS((s.shape[0], s.shape[1] * N_DEV), s.dtype) for s, k in zip(shards, kinds)]
    return sizes, fulls


def _gather_direct(shards, kinds):
    n = len(shards)
    sizes, fulls = _gather_sizes(shards, kinds)

    def build(ins, outs, send_sems, recv_sems):
        x, y, c = _me()
        cps = []
        for a in range(n):
            mine = _blk(outs[a], kinds[a], 4 * x + 2 * y + c, sizes[a])
            cps.append(pltpu.make_async_copy(ins[a], mine, send_sems.at[5 * a + 4]))
            for j in range(N_CHIP):
                to = (x, y, 1 - c) if j == 0 else (*_chip_peer(x, y, j), c)
                cps.append(pltpu.make_async_remote_copy(
                    src_ref=ins[a], dst_ref=mine, send_sem=send_sems.at[5 * a + j], recv_sem=recv_sems.at[5 * a + j],
                    device_id=to, device_id_type=MESH))
        return cps

    return _Exchange(shards, fulls, 5 * n, build)


def _gather_everywhere(shards, kinds):
    n = len(shards)
    sizes, fulls = _gather_sizes(shards, kinds)

    def build(ins, outs, send_sems, recv_sems):
        x, y, c = _me()
        cps = []
        for a in range(n):
            mine = _blk(outs[a], kinds[a], 4 * x + 2 * y + c, sizes[a])
            cps.append(pltpu.make_async_copy(ins[a], mine, send_sems.at[N_DEV * a]))
            for d in range(1, N_DEV):
                to = ((1 - x) if d & 4 else x, (1 - y) if d & 2 else y, (1 - c) if d & 1 else c)
                cps.append(pltpu.make_async_remote_copy(
                    src_ref=ins[a], dst_ref=mine, send_sem=send_sems.at[N_DEV * a + d],
                    recv_sem=recv_sems.at[N_DEV * a + d], device_id=to, device_id_type=MESH))
        return cps

    return _Exchange(shards, fulls, N_DEV * n, build)


def _gather_forward(fulls, kinds, sizes):
    n = len(fulls)

    def build(ins, outs, send_sems, recv_sems):
        x, y, c = _me()
        cps = []
        for a in range(n):
            for j in (1, 2, 3):
                px, py = _chip_peer(x, y, j)
                k = 4 * px + 2 * py + c
                cps.append(pltpu.make_async_remote_copy(
                    src_ref=_blk(ins[a], kinds[a], k, sizes[a]), dst_ref=_blk(outs[a], kinds[a], k, sizes[a]),
                    send_sem=send_sems.at[3 * a + j - 1], recv_sem=recv_sems.at[3 * a + j - 1],
                    device_id=(x, y, 1 - c), device_id_type=MESH))
        return cps

    return _Exchange(fulls, [SDS(f.shape, f.dtype) for f in fulls], 3 * n, build, alias={a: a for a in range(n)})


def _sibling_exchange(grads, kinds, sizes):
    n = len(grads)

    def blk_shape(a):
        g = grads[a]
        return (sizes[a],) + g.shape[1:] if kinds[a] == "rows" else (g.shape[0], sizes[a])

    def build(ins, outs, send_sems, recv_sems):
        x, y, c = _me()
        cps = []
        for a in range(n):
            for q in range(N_CHIP):
                cps.append(pltpu.make_async_remote_copy(
                    src_ref=_blk(ins[a], kinds[a], 2 * q + (1 - c), sizes[a]), dst_ref=outs[a].at[q],
                    send_sem=send_sems.at[N_CHIP * a + q], recv_sem=recv_sems.at[N_CHIP * a + q],
                    device_id=(x, y, 1 - c), device_id_type=MESH))
        return cps

    return _Exchange(grads, [SDS((N_CHIP,) + blk_shape(a), F32) for a in range(n)], N_CHIP * n, build)


def _chip_partial(g, r1, kind, size, core_chip, name):
    if kind == "rows":
        rows, cols = size, g.shape[1]
        g3 = g.reshape(N_DEV, rows, cols)
        rb = _pick(rows, 512, 16)
        g_spec = pl.BlockSpec((1, rb, cols), lambda j, q, s: (2 * q + s[0], j, 0))
        grid = (rows // rb, N_CHIP)
        blk = (1, rb, cols)
        imap = lambda j, q, s: (q, j, 0)
        own_spec = pl.BlockSpec((rb, cols), lambda j, q, s: (j, 0))
    else:
        rows, cols = g.shape[0], size
        g3 = g
        g_spec = pl.BlockSpec((rows, cols), lambda j, q, s: (0, 2 * q + s[0]))
        grid = (1, N_CHIP)
        blk = (1, rows, cols)
        imap = lambda j, q, s: (q, 0, 0)
        own_spec = pl.BlockSpec((rows, cols), lambda j, q, s: (0, 0))

    def body(s_ref, g_ref, r_ref, own_ref, pb_ref):
        t = g_ref[...].reshape(blk) + r_ref[...]
        pb_ref[...] = t.astype(BF16)

        @pl.when(pl.program_id(1) == s_ref[1])
        def _():
            own_ref[...] = t[0]

    return pl.pallas_call(
        body, name=name,
        grid_spec=pltpu.PrefetchScalarGridSpec(
            num_scalar_prefetch=1, grid=grid,
            in_specs=[g_spec, pl.BlockSpec(blk, imap)],
            out_specs=[own_spec, pl.BlockSpec(blk, imap)]),
        out_shape=[SDS((rows, cols), F32), SDS((N_CHIP, rows, cols), BF16)],
        compiler_params=_cparams(("arbitrary", "arbitrary")),
    )(core_chip, g3, r1)


def _chip_exchange(parts):
    n = len(parts)
    m = N_CHIP - 1

    def build(ins, outs, send_sems, recv_sems):
        x, y, c = _me()
        cps = []
        for a in range(n):
            for j in (1, 2, 3):
                px, py = _chip_peer(x, y, j)
                cps.append(pltpu.make_async_remote_copy(
                    src_ref=ins[a].at[2 * px + py], dst_ref=outs[a].at[j - 1], send_sem=send_sems.at[m * a + j - 1],
                    recv_sem=recv_sems.at[m * a + j - 1], device_id=(px, py, c), device_id_type=MESH))
        return cps

    return _Exchange(parts, [SDS((m,) + p.shape[1:], BF16) for p in parts], m * n, build)


def _grad_total(own, r2, name):
    rows, cols = own.shape
    rb = _pick(rows, 512, 16)

    def body(p_ref, r_ref, o_ref):
        s = p_ref[...]
        for j in range(N_CHIP - 1):
            s = s + r_ref[j].astype(F32)
        o_ref[...] = s

    return pl.pallas_call(
        body, name=name, grid=(rows // rb,),
        in_specs=[pl.BlockSpec((rb, cols), lambda i: (i, 0)), pl.BlockSpec((N_CHIP - 1, rb, cols), lambda i: (0, i, 0))],
        out_specs=pl.BlockSpec((rb, cols), lambda i: (i, 0)),
        out_shape=SDS((rows, cols), F32),
        compiler_params=_cparams(("arbitrary",)),
    )(own, r2)


def _all_reduce_small(pack, name):
    rows = pack.shape[0]
    rs = rows // N_DEV
    assert rs * N_DEV == rows and rs % 8 == 0

    def body(x_ref, o_ref, rbuf, red, send1, recv1, send2, recv2):
        x, y, c = _me()
        me = 4 * x + 2 * y + c

        def peer(d):
            px = (1 - x) if (d >> 2) & 1 else x
            py = (1 - y) if (d >> 1) & 1 else y
            pc = (1 - c) if d & 1 else c
            return px, py, pc

        def sl(ref, k):
            return ref.at[pl.ds(pl.multiple_of(k * rs, 8), rs)]

        phase1 = []
        for d in range(1, N_DEV):
            px, py, pc = peer(d)
            phase1.append(pltpu.make_async_remote_copy(
                src_ref=sl(x_ref, 4 * px + 2 * py + pc), dst_ref=rbuf.at[d], send_sem=send1.at[d], recv_sem=recv1.at[d],
                device_id=(px, py, pc), device_id_type=MESH))
        for cp in phase1:
            cp.start()
        acc = sl(x_ref, me)[...]
        for cp in phase1:
            cp.wait()
        for d in range(1, N_DEV):
            acc = acc + rbuf[d]
        red[...] = acc
        sl(o_ref, me)[...] = acc
        phase2 = []
        for d in range(1, N_DEV):
            px, py, pc = peer(d)
            phase2.append(pltpu.make_async_remote_copy(
                src_ref=red, dst_ref=sl(o_ref, me), send_sem=send2.at[d], recv_sem=recv2.at[d],
                device_id=(px, py, pc), device_id_type=MESH))
        for cp in phase2:
            cp.start()
        for cp in phase2:
            cp.wait()

    vm = pl.BlockSpec(memory_space=pltpu.VMEM)
    return pl.pallas_call(
        body, name=name, in_specs=[vm], out_specs=vm, out_shape=SDS(pack.shape, F32),
        scratch_shapes=[pltpu.VMEM((N_DEV, rs, LANES), F32), pltpu.VMEM((rs, LANES), F32),
                        pltpu.SemaphoreType.DMA((N_DEV,)), pltpu.SemaphoreType.DMA((N_DEV,)),
                        pltpu.SemaphoreType.DMA((N_DEV,)), pltpu.SemaphoreType.DMA((N_DEV,))],
        compiler_params=_cparams(None, has_side_effects=True),
    )(pack)


def _adamw(w, g, m, v, name):
    rows, cols = w.shape
    rb = rows if rows * cols * 4 <= ADAMW_WHOLE_BYTES else _pick(rows, 256, 8)
    c1 = 1.0 / (1.0 - ADAM_B1 ** ADAM_STEP)
    c2 = 1.0 / (1.0 - ADAM_B2 ** ADAM_STEP)

    def body(w_ref, g_ref, m_ref, v_ref, d_ref, mo_ref, vo_ref):
        gv = g_ref[...]
        mn = ADAM_B1 * m_ref[...] + (1.0 - ADAM_B1) * gv
        vn = ADAM_B2 * v_ref[...] + (1.0 - ADAM_B2) * (gv * gv)
        mo_ref[...] = mn
        vo_ref[...] = vn
        d_ref[...] = -ADAM_LR * ((mn * c1) / (jnp.sqrt(vn * c2) + ADAM_EPS) + ADAM_WD * w_ref[...])

    spec = pl.BlockSpec((rb, cols), lambda i: (i, 0))
    return pl.pallas_call(
        body, name=name, grid=(rows // rb,),
        in_specs=[spec] * 4, out_specs=[spec] * 3, out_shape=[SDS((rows, cols), F32)] * 3,
        compiler_params=_cparams(("arbitrary",)),
    )(w, g, m, v)


def _pad_rows(a, mult=8):
    r = (-a.shape[0]) % mult
    return a if r == 0 else jnp.pad(a, ((0, r), (0, 0)))


def _as_lanes(a):
    flat = a.reshape(-1)
    pad = (-flat.shape[0]) % (8 * LANES)
    if pad:
        flat = jnp.pad(flat, (0, pad))
    return flat.reshape(-1, LANES)


def kernel(x, norm_g, w_in, ln_g, ln_b, w_s, b_s, conv_w, conv_b, w_pool, pool_scale, w_pa, w_pb, w_pc, w_o, final_g, loss_target, m_norm_g, m_w_in, m_ln_g, m_ln_b, m_w_s, m_b_s, m_conv_w, m_conv_b, m_w_pool, m_pool_scale, m_w_pa, m_w_pb, m_w_pc, m_w_o, m_final_g, v_norm_g, v_w_in, v_ln_g, v_ln_b, v_w_s, v_b_s, v_conv_w, v_conv_b, v_w_pool, v_pool_scale, v_w_pa, v_w_pb, v_w_pc, v_w_o, v_final_g):
    L = w_in.shape[0]
    D = x.shape[-1]
    n_loc = w_in.shape[2]
    pc_loc = w_pa.shape[2]
    x0 = x[0]
    target = loss_target[0]
    xi, yi, ci = _me()
    core_chip = jnp.stack([ci, 2 * xi + yi]).astype(jnp.int32)

    kinds5 = ["rows", "cols", "cols", "cols", "rows"]

    def layer_shards(l):
        return [w_in[l].T.astype(BF16), w_pa[l].astype(BF16), w_pb[l].astype(BF16), w_pc[l].astype(BF16),
                w_o[l].astype(BF16)]

    def gathered(direct, shards, kinds, l):
        sizes, _ = _gather_sizes(shards, kinds)
        return _run_exchange(_gather_forward(direct, kinds, sizes), f"weights_forward_{l}")

    cw_loc = _pad_rows(conv_w.reshape(L * CONV_TAPS, -1))
    cw_loc = jnp.pad(cw_loc, ((0, 0), (0, LANES - cw_loc.shape[1])))
    causal = jnp.tril(jnp.ones((CHUNK, CHUNK), dtype=bool))

    chip_order = jnp.stack([2 * xi + yi] + [2 * px + py for px, py in (_chip_peer(xi, yi, j) for j in (1, 2, 3))])
    sh0 = layer_shards(0)
    rest0, krest0 = sh0[1:] + [cw_loc], kinds5[1:] + ["rows"]
    (p0, h0, win_t0), delivered = _inproj_gathering(x0, norm_g[0], sh0[0], chip_order.astype(jnp.int32), "inproj_fwd_0",
                                                    [_gather_everywhere(rest0, krest0)])
    rest0_full = delivered[0]
    cw_all = rest0_full[-1].reshape(N_DEV, -1, LANES)[:, :L * CONV_TAPS, :conv_w.shape[2]]
    conv_w_full = jnp.transpose(cw_all, (1, 0, 2)).reshape(L, CONV_TAPS, -1)

    def make_layer(l, full5):
        win_t, wpa, wpb, wpc, wo = full5
        wm = jnp.where(causal, w_s[l], 0.0)
        cvec = jnp.concatenate([ln_g[l][None], ln_b[l][None], conv_w_full[l], conv_b[l][None], pool_scale[l][None],
                                jnp.zeros((C_ROWS - 7, SEG), F32)], axis=0)
        return dict(
            win_t=win_t, wpa=wpa, wpb=wpb, wpc=wpc, wo=wo, cvec=cvec,
            bsb=jnp.repeat(b_s[l].T, HEAD, axis=1),
            wcat=jnp.transpose(wm, (1, 0, 2)).reshape(CHUNK, GROUPS * CHUNK).astype(BF16),
            wcatt=jnp.transpose(wm, (2, 0, 1)).reshape(CHUNK, GROUPS * CHUNK).astype(BF16),
            wpool=w_pool[l].astype(BF16))

    layers, xs, saved, win_next = [], [x0], [], None
    for l in range(L):
        if l == 0:
            p, h = p0, h0
            lw = make_layer(0, [win_t0] + list(rest0_full[:4]))
        else:
            (p, h), delivered = _inproj(xs[-1], norm_g[l], win_next, f"inproj_fwd_{l}",
                                        [_gather_everywhere(layer_shards(l)[1:], kinds5[1:])])
            lw = make_layer(l, [win_next] + delivered[0])
        layers.append(lw)
        nxt = layer_shards(l + 1)[:1] if l + 1 < L else None
        (xn, ya, yb, yc), delivered = _mix_fwd(p, xs[-1], lw, f"mix_fwd_{l}", [_gather_direct(nxt, kinds5[:1])] if nxt else [])
        if nxt:
            win_next = gathered(delivered[0], nxt, kinds5[:1], l + 1)[0]
        saved.append((p, h, ya, yb, yc))
        xs.append(xn)
    dx, loss_acc, dfg_acc = _loss_head(xs[-1], final_g, target, "loss_head")

    rs_sizes = [n_loc, pc_loc, pc_loc, pc_loc, w_o.shape[1]]
    await_sibling, await_chips = [], []
    partial_of, from_chips = {}, {}
    serial = [0]

    def riders_now():
        riders, plan = [], []
        for grp in await_chips:
            riders.append(_chip_exchange([partial_of[t][1] for t, _, _, _ in grp]))
            plan.append(("chips", grp))
        for grp in await_sibling:
            riders.append(_sibling_exchange([g for _, g, _, _ in grp], [k for _, _, k, _ in grp], [s for _, _, _, s in grp]))
            plan.append(("sibling", grp))
        del await_chips[:], await_sibling[:]
        return riders, plan

    def absorb(plan, delivered):
        for (what, grp), res in zip(plan, delivered):
            for (t, g, k, s), r in zip(grp, res):
                if what == "chips":
                    from_chips[t] = r
                else:
                    partial_of[t] = _chip_partial(g, r, k, s, core_chip, f"grad_chip_partial_{t[0]}_{t[1]}")
            if what == "sibling":
                await_chips.append(grp)

    small = [None] * L
    for l in reversed(range(L)):
        lw = layers[l]
        p, h, ya, yb, yc = saved[l]
        dp, acts, merged, dys, gwc, gbs, gwpool, gvec = _mix_bwd(p, dx, ya, yb, yc, lw, f"mix_bwd_{l}")
        riders, plan = riders_now()
        (gwpa, gwpb, gwpc, gwo), delivered = _proj_wgrad(acts, merged, dys, dx, f"proj_wgrad_{l}", riders)
        absorb(plan, delivered)
        await_sibling.append([((l, a), g, kinds5[a], rs_sizes[a]) for a, g in ((1, gwpa), (2, gwpb), (3, gwpc), (4, gwo))])

        def bwd_x(dxo, pieces):
            nt = _inproj_token_blocks(dxo.shape[0])
            pieces = min(pieces, nt)
            done, dng, b0 = None, None, 0
            for k in range(pieces):
                cnt = (nt - b0) if k == pieces - 1 else max(1, (nt * LAST_PIECE_SHARE[0]) // (LAST_PIECE_SHARE[1] * (pieces - 1)))
                riders, plan = riders_now()
                (done, dng_k), delivered = _inproj_bwd_x(dp, lw["win_t"], xs[l], norm_g[l], dxo, f"inproj_bwd_x_{l}_{k}",
                                                         riders, blocks=(b0, cnt), fill=done)
                absorb(plan, delivered)
                dng = dng_k if dng is None else dng + dng_k
                b0 += cnt
            return done, dng

        def bwd_w():
            riders, plan = riders_now()
            (gwin_t,), delivered = _inproj_bwd_w(dp, h, f"inproj_bwd_w_{l}", riders)
            absorb(plan, delivered)
            await_sibling.append([((l, 0), gwin_t, kinds5[0], rs_sizes[0])])

        if l == L - 1:
            dx, dng = bwd_x(dx, 1)
            bwd_w()
        else:
            bwd_w()
            dx, dng = bwd_x(dx, 2 if l == 0 else 1)
        small[l] = dict(norm_g=dng[0], ln_g=gvec[V_LNG], ln_b=gvec[V_LNB], w_s=gwc, b_s=gbs, conv_w=gvec[V_CW0:V_CW0 + 3],
                        conv_b=gvec[V_CB], w_pool=gwpool, pool_scale=gvec[V_PS])
    while await_sibling or await_chips:
        riders, plan = riders_now()
        delivered = []
        for ex in riders:
            delivered.append(_run_exchange(ex, f"grad_exchange_tail_{serial[0]}"))
            serial[0] += 1
        absorb(plan, delivered)
    grad_x = dx[None]
    big_grads = []
    for l in range(L):
        tot = [_grad_total(partial_of[(l, a)][0], from_chips[(l, a)], f"grad_total_{l}_{a}") for a in range(5)]
        big_grads.append([tot[0],
                          tot[1].reshape(SEG, pc_loc), tot[2].reshape(SEG, pc_loc), tot[3].reshape(SEG, pc_loc),
                          tot[4]])

    names = ["norm_g", "ln_g", "ln_b", "w_s", "b_s", "conv_w", "conv_b", "w_pool", "pool_scale"]
    pieces = [jnp.stack([small[l][nm] for l in range(L)]).reshape(-1) for nm in names]
    pieces += [dfg_acc[0], loss_acc.reshape(-1)]
    offs = [0]
    for pc in pieces:
        offs.append(offs[-1] + pc.shape[0])
    pack = _pad_rows(_as_lanes(jnp.concatenate(pieces)), 8 * N_DEV)
    red = _all_reduce_small(pack, "small_grads_all_reduce").reshape(-1)

    def unpack(i, shape):
        return red[offs[i]:offs[i] + math.prod(shape)].reshape(shape)

    g_norm_g = unpack(0, (L, D))
    g_ln_g = unpack(1, (L, SEG))
    g_ln_b = unpack(2, (L, SEG))
    g_w_s = unpack(3, (L, GROUPS, CHUNK, CHUNK))
    g_b_s = jnp.transpose(unpack(4, (L, CHUNK, LANES))[:, :, :GROUPS], (0, 2, 1))
    g_conv_w_full = unpack(5, (L, CONV_TAPS, SEG))
    g_conv_b = unpack(6, (L, SEG))
    g_w_pool = unpack(7, (L, len(POOL_WINDOWS), POOL_GROUP, POOL_GROUP))
    g_pool_scale = unpack(8, (L, SEG))
    g_final_g = unpack(9, (D,))
    loss = red[offs[10]]
    dev = 4 * xi + 2 * yi + ci
    g_conv_w = lax.dynamic_slice_in_dim(g_conv_w_full, dev * conv_w.shape[2], conv_w.shape[2], axis=2)

    g_w_in_t = jnp.stack([big_grads[l][0] for l in range(L)])
    g_w_in = jnp.swapaxes(g_w_in_t, 1, 2)
    g_w_pa = jnp.stack([big_grads[l][1] for l in range(L)])
    g_w_pb = jnp.stack([big_grads[l][2] for l in range(L)])
    g_w_pc = jnp.stack([big_grads[l][3] for l in range(L)])
    g_w_o = jnp.stack([big_grads[l][4] for l in range(L)])

    grads = dict(norm_g=g_norm_g, w_in=g_w_in, ln_g=g_ln_g, ln_b=g_ln_b, w_s=g_w_s, b_s=g_b_s, conv_w=g_conv_w,
                 conv_b=g_conv_b, w_pool=g_w_pool, pool_scale=g_pool_scale, w_pa=g_w_pa, w_pb=g_w_pb, w_pc=g_w_pc,
                 w_o=g_w_o, final_g=g_final_g)
    weights = dict(norm_g=norm_g, w_in=w_in, ln_g=ln_g, ln_b=ln_b, w_s=w_s, b_s=b_s, conv_w=conv_w, conv_b=conv_b,
                   w_pool=w_pool, pool_scale=pool_scale, w_pa=w_pa, w_pb=w_pb, w_pc=w_pc, w_o=w_o, final_g=final_g)
    ms = dict(norm_g=m_norm_g, w_in=m_w_in, ln_g=m_ln_g, ln_b=m_ln_b, w_s=m_w_s, b_s=m_b_s, conv_w=m_conv_w,
              conv_b=m_conv_b, w_pool=m_w_pool, pool_scale=m_pool_scale, w_pa=m_w_pa, w_pb=m_w_pb, w_pc=m_w_pc,
              w_o=m_w_o, final_g=m_final_g)
    vs = dict(norm_g=v_norm_g, w_in=v_w_in, ln_g=v_ln_g, ln_b=v_ln_b, w_s=v_w_s, b_s=v_b_s, conv_w=v_conv_w,
              conv_b=v_conv_b, w_pool=v_w_pool, pool_scale=v_pool_scale, w_pa=v_w_pa, w_pb=v_w_pb, w_pc=v_w_pc,
              w_o=v_w_o, final_g=v_final_g)
    order = ["norm_g", "w_in", "ln_g", "ln_b", "w_s", "b_s", "conv_w", "conv_b", "w_pool", "pool_scale", "w_pa", "w_pb",
             "w_pc", "w_o", "final_g"]

    delta, new_m, new_v = {}, {}, {}
    big = ["w_in", "w_pa", "w_pb", "w_pc", "w_o"]
    tr = lambda a: jnp.swapaxes(a, 1, 2)
    for nm in big:
        into = tr if nm == "w_in" else (lambda a: a)
        shp = into(weights[nm]).shape
        two = lambda a: a.reshape(-1, shp[-1])
        g2 = two(g_w_in_t) if nm == "w_in" else two(grads[nm])
        d, mn, vn = _adamw(two(into(weights[nm])), g2, two(into(ms[nm])), two(into(vs[nm])), f"adamw_{nm}")
        delta[nm], new_m[nm], new_v[nm] = (into(a.reshape(shp)) for a in (d, mn, vn))
    rest = [nm for nm in order if nm not in big and nm != "conv_w"] + ["conv_w"]
    cat = lambda src: _as_lanes(jnp.concatenate([src[nm].reshape(-1) for nm in rest]))
    flat = [a.reshape(-1) for a in _adamw(cat(weights), cat(grads), cat(ms), cat(vs), "adamw_small")]
    off = 0
    for nm in rest:
        shp = weights[nm].shape
        n = math.prod(shp)
        delta[nm], new_m[nm], new_v[nm] = (a[off:off + n].reshape(shp) for a in flat)
        off += n

    return (loss, grad_x, *[grads[nm] for nm in order], *[delta[nm] for nm in order],
            *[new_m[nm] for nm in order], *[new_v[nm] for nm in order])
```

```python
import functools
import math

import numpy as np
import jax
import jax.numpy as jnp
from jax import lax
from jax.experimental import pallas as pl
from jax.experimental.pallas import tpu as pltpu

F32 = jnp.float32
BF16 = jnp.bfloat16
SDS = jax.ShapeDtypeStruct
MESH = pl.DeviceIdType.MESH

SEG = 512
CHUNK = 128
GROUPS = 8
HEAD = SEG // GROUPS
POOL_WINDOWS = (2, 4, 8, 16)
POOL_GROUP = SEG // len(POOL_WINDOWS)
CONV_TAPS = 3
HALO = 16
RMS_EPS = 1e-6
LN_EPS = 1e-5
ADAM_LR, ADAM_B1, ADAM_B2, ADAM_EPS, ADAM_WD, ADAM_STEP = 0.001, 0.9, 0.999, 1e-08, 0.01, 10

O_U, O_V, O_ZA, O_XB, O_BG, O_CG, O_ZB, O_XC, O_ZC, O_G = (SEG * i for i in range(10))

N_DEV = 8
N_CHIP = 4
LANES = 128
VMEM_LIMIT = 48 * 1024 * 1024
ADAMW_WHOLE_BYTES = 2 * 1024 * 1024
INPROJ_ROWS = 1024
BWD_X_ROWS = 512
WGRAD_ROWS = 2048
RIDERS_FROM_CHUNK = 2
LAST_PIECE_SHARE = (5, 16)


def _cparams(sem=None, **kw):
    return pltpu.CompilerParams(dimension_semantics=sem, vmem_limit_bytes=VMEM_LIMIT, **kw)


def _pick(total, target, mult):
    best = None
    for d in range(mult, min(total, target) + 1, mult):
        if total % d == 0:
            best = d
    assert best is not None, (total, target, mult)
    return best


def _dot(a, b):
    return jnp.dot(a, b, preferred_element_type=F32)


def _dot_nt(a, b):
    return lax.dot_general(a, b, (((1,), (1,)), ((), ())), preferred_element_type=F32)


def _dot_tn(a, b):
    return lax.dot_general(a, b, (((0,), (0,)), ((), ())), preferred_element_type=F32)


def _zero(ref):
    ref[...] = jnp.zeros(ref.shape, ref.dtype)


def _sigmoid(x):
    return 1.0 / (1.0 + jnp.exp(-x))


_GELU_C = math.sqrt(2.0 / math.pi)


def _gelu(x):
    t = jnp.tanh(_GELU_C * (x + 0.044715 * x * x * x))
    return 0.5 * x * (1.0 + t), t


def _gelu_grad(x, t):
    return 0.5 * (1.0 + t) + 0.5 * x * (1.0 - t * t) * _GELU_C * (1.0 + 3.0 * 0.044715 * x * x)


def _me():
    return lax.axis_index("x"), lax.axis_index("y"), lax.axis_index("c")


def _inproj(x, norm_g, win_t, name, riders=()):
    T, D = x.shape
    N = win_t.shape[0]
    bt = _pick(T, INPROJ_ROWS, 16)
    bn = _pick(N, 1536, LANES)
    grid = (T // bt, N // bn)

    def compute(x_ref, g_ref, w_ref, p_ref, h_ref, hs_ref):
        @pl.when(pl.program_id(1) == 0)
        def _():
            xv = x_ref[...]
            rstd = lax.rsqrt(jnp.mean(xv * xv, axis=-1, keepdims=True) + RMS_EPS)
            hb = (xv * rstd * g_ref[...]).astype(BF16)
            hs_ref[...] = hb
            h_ref[...] = hb

        p_ref[...] = _dot_nt(hs_ref[...], w_ref[...]).astype(BF16)

    return _host_call(
        compute, name, grid, riders,
        inputs=[x, norm_g.reshape(1, D), win_t],
        in_specs=[pl.BlockSpec((bt, D), lambda i, j: (i, 0)),
                  pl.BlockSpec((1, D), lambda i, j: (0, 0)),
                  pl.BlockSpec((bn, D), lambda i, j: (j, 0))],
        out_specs=[pl.BlockSpec((bt, bn), lambda i, j: (i, j)),
                   pl.BlockSpec((bt, D), lambda i, j: (i, 0))],
        out_shape=[SDS((T, N), BF16), SDS((T, D), BF16)],
        scratch_shapes=[pltpu.VMEM((bt, D), BF16)])


def _inproj_gathering(x, norm_g, w_loc, chip_order, name, riders=()):
    T, D = x.shape
    n = w_loc.shape[0]
    N = n * N_DEV
    cw = 2 * n
    bt = _pick(T, INPROJ_ROWS, 16)
    nt = T // bt
    r_in, r_out, r_sems, copies = _rider_plan(riders)
    n_rin, n_rout = len(r_in), len(r_out)

    def body(q_ref, x_ref, g_ref, wloc_ref, *rest):
        rins = rest[:n_rin]
        p_ref, h_ref, wfull_ref = rest[n_rin:n_rin + 3]
        routs = rest[n_rin + 3:n_rin + 3 + n_rout]
        hs_ref, wbuf, send_sems, recv_sems, loc_sems = rest[n_rin + 3 + n_rout:n_rin + 8 + n_rout]
        rsems = rest[n_rin + 8 + n_rout:]
        j, i = pl.program_id(0), pl.program_id(1)
        cx, cy, cc = _me()
        sibling = (cx, cy, 1 - cc)

        def rows(k):
            return wfull_ref.at[pl.ds(pl.multiple_of(k * n, 8), n)]

        def shard_copy(slot, src, k, to):
            return pltpu.make_async_remote_copy(src_ref=src, dst_ref=rows(k), send_sem=send_sems.at[slot],
                                                recv_sem=recv_sems.at[slot], device_id=to, device_id_type=MESH)

        me = 4 * cx + 2 * cy + cc
        place_mine = pltpu.make_async_copy(wloc_ref, rows(me), loc_sems.at[0])
        sends = [shard_copy(0, wloc_ref, me, sibling)]
        for jj in (1, 2, 3):
            sends.append(shard_copy(jj, wloc_ref, me, (*_chip_peer(cx, cy, jj), cc)))

        def forward(jj):
            px, py = _chip_peer(cx, cy, jj)
            k = 4 * px + 2 * py + cc
            return shard_copy(3 + jj, rows(k), k, sibling)

        def load_chunk(q):
            cp = pltpu.make_async_copy(wfull_ref.at[pl.ds(pl.multiple_of(q * cw, 8), cw)], wbuf, loc_sems.at[1])
            cp.start()
            cp.wait()

        keep_h = pltpu.make_async_copy(hs_ref, h_ref, loc_sems.at[2])

        @pl.when((j == 0) & (i == 0))
        def _():
            place_mine.start()
            for cp in sends:
                cp.start()
            place_mine.wait()
            sends[0].wait_recv()
            load_chunk(q_ref[0])

        for jj in (1, 2, 3):
            @pl.when((j == jj) & (i == 0))
            def _(jj=jj):
                sends[jj].wait_recv()
                fwd = forward(jj)
                fwd.start()
                fwd.wait_recv()
                load_chunk(q_ref[jj])
                if jj == RIDERS_FROM_CHUNK:
                    for cp in copies(rins, routs, rsems):
                        cp.start()

        tok = pl.ds(pl.multiple_of(i * bt, bt), bt)

        @pl.when(j == 0)
        def _():
            xv = x_ref[...]
            rstd = lax.rsqrt(jnp.mean(xv * xv, axis=-1, keepdims=True) + RMS_EPS)
            hs_ref[tok, :] = (xv * rstd * g_ref[...]).astype(BF16)

        @pl.when((j == 0) & (i == nt - 1))
        def _():
            keep_h.start()

        p_ref[...] = _dot_nt(hs_ref[tok, :], wbuf[...]).astype(BF16)

        @pl.when((j == N_CHIP - 1) & (i == nt - 1))
        def _():
            keep_h.wait()
            for cp in sends:
                cp.wait_send()
            for jj in (1, 2, 3):
                forward(jj).wait_send()
            for cp in copies(rins, routs, rsems):
                cp.wait()

    res = pl.pallas_call(
        body, name=name,
        grid_spec=pltpu.PrefetchScalarGridSpec(
            num_scalar_prefetch=1, grid=(N_CHIP, nt),
            in_specs=[pl.BlockSpec((bt, D), lambda j, i, q: (jnp.where(j == 0, i, nt - 1), 0)),
                      pl.BlockSpec((1, D), lambda j, i, q: (0, 0)), _ANY] + [_ANY] * n_rin,
            out_specs=[pl.BlockSpec((bt, cw), lambda j, i, q: (i, q[j])), _ANY, _ANY] + [_ANY] * n_rout,
            scratch_shapes=[pltpu.VMEM((T, D), BF16), pltpu.VMEM((cw, D), BF16), pltpu.SemaphoreType.DMA((7,)),
                            pltpu.SemaphoreType.DMA((7,)), pltpu.SemaphoreType.DMA((3,))] + r_sems),
        out_shape=[SDS((T, N), BF16), SDS((T, D), BF16), SDS((N, D), BF16)] + r_out,
        input_output_aliases=_rider_aliases(riders, 4, 3),
        compiler_params=_cparams(("arbitrary", "arbitrary")),
    )(chip_order, x, norm_g.reshape(1, D), w_loc, *r_in)
    return res[:3], _split_riders(riders, res[3:])


C_LNG, C_LNB, C_CW0, C_CW1, C_CW2, C_CB, C_PS = range(7)
C_ROWS = 8


def _pool_bands(R, anticausal):
    t = np.arange(R)[:, None]
    s = np.arange(R + CHUNK)[None, :]
    bands = [((s >= t) & (s < t + w)) if anticausal else ((s > t + CHUNK - w) & (s <= t + CHUNK)) for w in POOL_WINDOWS]
    return jnp.asarray(np.stack(bands), dtype=BF16)


def _mixers(p_ref, hxb_ref, hcg_ref, hxc_ref, cv, bsb_ref, wcat_ref, wpool_ref, band_ref, extb,
            first, blk, R, need_grad):
    def seg(lo):
        return p_ref[:, lo:lo + SEG].astype(F32)

    u, v, za = seg(O_U), seg(O_V), seg(O_ZA)
    xb, bg, cg, zb = seg(O_XB), seg(O_BG), seg(O_CG), seg(O_ZB)
    xc, zc = seg(O_XC), seg(O_ZC)
    out = {}

    ug, tu = _gelu(u)
    vg, tv = _gelu(v)
    mu = jnp.mean(vg, axis=-1, keepdims=True)
    vcen = vg - mu
    rs = lax.rsqrt(jnp.mean(vcen * vcen, axis=-1, keepdims=True) + LN_EPS)
    vhat = vcen * rs
    vn = (vhat * cv[C_LNG:C_LNG + 1, :] + cv[C_LNB:C_LNB + 1, :]).astype(BF16)
    lane_group = lax.broadcasted_iota(jnp.int32, (CHUNK, SEG), 1) // HEAD
    zero_b = jnp.zeros((CHUNK, SEG), BF16)
    sgs = []
    for ci in range(R // CHUNK):
        vc = vn[ci * CHUNK:(ci + 1) * CHUNK]
        vst = jnp.concatenate([jnp.where(lane_group == g, vc, zero_b) for g in range(GROUPS)], axis=0)
        sgs.append(_dot(wcat_ref[...], vst) + bsb_ref[...])
    sg = sgs[0] if len(sgs) == 1 else jnp.concatenate(sgs, axis=0)
    a_out = ug * sg
    sa = _sigmoid(za)
    out["a"] = a_out * (za * sa)

    cx = cg * xb
    halo_b = hcg_ref[...].astype(F32) * hxb_ref[...].astype(F32)
    extb[0:HALO, :] = jnp.where(first, 0.0, halo_b)
    extb[HALO:HALO + R, :] = cx
    cx1 = extb[pl.ds(HALO - 1, R), :]
    cx2 = extb[pl.ds(HALO - 2, R), :]
    yconv = (cv[C_CW0:C_CW0 + 1, :] * cx2 + cv[C_CW1:C_CW1 + 1, :] * cx1
             + cv[C_CW2:C_CW2 + 1, :] * cx + cv[C_CB:C_CB + 1, :])
    b_out = bg * yconv
    sb = _sigmoid(zb)
    out["b"] = b_out * (zb * sb)

    halo_c = hxc_ref[...]
    xc_ext = jnp.concatenate([jnp.zeros((CHUNK - HALO, SEG), BF16), jnp.where(first, jnp.zeros_like(halo_c), halo_c),
                              p_ref[:, O_XC:O_XC + SEG]], axis=0)
    tpos = blk * R + lax.broadcasted_iota(jnp.int32, (R, POOL_GROUP), 0) + 1
    pooled, invs, pws = [], [], []
    for gi, w in enumerate(POOL_WINDOWS):
        lo = gi * POOL_GROUP
        win = _dot(band_ref[gi], xc_ext[:, lo:lo + POOL_GROUP])
        inv = 1.0 / jnp.minimum(tpos, w).astype(F32)
        pg = (win * inv - xc[:, lo:lo + POOL_GROUP]).astype(BF16)
        pooled.append(pg)
        invs.append(inv)
        pws.append(_dot(pg, wpool_ref[gi]))
    pw = jnp.concatenate(pws, axis=1)
    c_out = pw * cv[C_PS:C_PS + 1, :]
    sc = _sigmoid(zc)
    out["c"] = c_out * (zc * sc)

    if need_grad:
        out.update(u=u, v=v, tu=tu, tv=tv, ug=ug, sg=sg, a_out=a_out, za=za, sa=sa,
                   rs=rs, vhat=vhat, vn=vn, lane_group=lane_group, zero_b=zero_b,
                   xb=xb, bg=bg, cg=cg, cx=cx, cx1=cx1, cx2=cx2, yconv=yconv, b_out=b_out, zb=zb, sb=sb,
                   pooled=pooled, invs=invs, pw=pw, c_out=c_out, zc=zc, sc=sc)
    return out


def _halo_specs(R, nb, rev):
    step = R // HALO

    def mk(col):
        def imap(i):
            b = (nb - 1 - i) if rev else i
            return (jnp.maximum(b * step - 1, 0), col)
        return pl.BlockSpec((HALO, SEG), imap)

    return [mk(O_XB // SEG), mk(O_CG // SEG), mk(O_XC // SEG)]


def _const_spec(shape):
    nd = len(shape)
    return pl.BlockSpec(shape, lambda i: (0,) * nd, pipeline_mode=pl.Buffered(1))


MIX_FWD_ROWS = 512
MIX_BWD_ROWS = 256


def _mix_block_rows(T, target):
    return _pick(T, target, CHUNK)


def _mix_fwd(p, x, lw, name, riders=()):
    T, D = x.shape
    N = p.shape[1]
    R = _mix_block_rows(T, MIX_FWD_ROWS)
    nb = T // R

    def body(p_ref, hxb, hcg, hxc, x_ref, cv_ref, bsb_ref, wcat_ref, wpool_ref, band_ref, wpa_ref, wpb_ref, wpc_ref,
             wo_ref, xo_ref, ya_ref, yb_ref, yc_ref, extb):
        i = pl.program_id(0)
        cv = cv_ref[...]
        r = _mixers(p_ref, hxb, hcg, hxc, cv, bsb_ref, wcat_ref, wpool_ref, band_ref, extb,
                    i == 0, i, R, False)
        merged = None
        for k, (act, w_ref, y_ref) in enumerate(((r["a"], wpa_ref, ya_ref), (r["b"], wpb_ref, yb_ref),
                                                 (r["c"], wpc_ref, yc_ref))):
            y = _dot(act.astype(BF16), w_ref[...]).astype(BF16)
            y_ref[...] = y
            term = _sigmoid(p_ref[:, O_G + k * D:O_G + (k + 1) * D]) * y
            merged = term if merged is None else merged + term
        xo_ref[...] = x_ref[...] + _dot(merged, wo_ref[...])

    row = lambda w: pl.BlockSpec((R, w), lambda i: (i, 0))
    consts = [lw["cvec"], lw["bsb"], lw["wcat"], lw["wpool"], _pool_bands(R, False), lw["wpa"], lw["wpb"], lw["wpc"],
              lw["wo"]]
    return _host_call(
        body, name, (nb,), riders,
        inputs=[p, p, p, p, x, *consts],
        in_specs=[row(N)] + _halo_specs(R, nb, False) + [row(D)] + [_const_spec(c.shape) for c in consts],
        out_specs=[row(D), row(D), row(D), row(D)],
        out_shape=[SDS((T, D), F32), SDS((T, D), BF16), SDS((T, D), BF16), SDS((T, D), BF16)],
        scratch_shapes=[pltpu.VMEM((HALO + R, SEG), F32)])


def _loss_head(x, final_g, target, name):
    T, D = x.shape
    bt = _pick(T, INPROJ_ROWS, 8)

    def body(x_ref, g_ref, t_ref, dx_ref, loss_ref, dg_ref):
        @pl.when(pl.program_id(0) == 0)
        def _():
            _zero(loss_ref)
            _zero(dg_ref)

        xv = x_ref[...]
        g = g_ref[...]
        rstd = lax.rsqrt(jnp.mean(xv * xv, axis=-1, keepdims=True) + RMS_EPS)
        xhat = xv * rstd
        err = xhat * g - t_ref[...]
        part = 0.5 * jnp.sum(jnp.sum(err * err, axis=-1, keepdims=True), axis=0, keepdims=True) / D
        loss_ref[...] += jnp.broadcast_to(part, loss_ref.shape)
        dy = err * (1.0 / D)
        dg_ref[0:1, :] += jnp.sum(dy * xhat, axis=0, keepdims=True)
        dxn = dy * g
        dx_ref[...] = rstd * (dxn - xhat * jnp.mean(dxn * xhat, axis=-1, keepdims=True))

    return pl.pallas_call(
        body, name=name, grid=(T // bt,),
        in_specs=[pl.BlockSpec((bt, D), lambda i: (i, 0)), _const_spec((1, D)), pl.BlockSpec((bt, D), lambda i: (i, 0))],
        out_specs=[pl.BlockSpec((bt, D), lambda i: (i, 0)), _const_spec((8, LANES)), _const_spec((8, D))],
        out_shape=[SDS((T, D), F32), SDS((8, LANES), F32), SDS((8, D), F32)],
        compiler_params=_cparams(("arbitrary",)),
    )(x, final_g.reshape(1, D), target)


V_LNG, V_LNB, V_CB, V_PS, V_CW0, V_CW1, V_CW2 = range(7)


def _mix_bwd(p, dxo, ya, yb, yc, lw, name, riders=()):
    T, D = dxo.shape
    N = p.shape[1]
    R = _mix_block_rows(T, MIX_BWD_ROWS)
    nb = T // R

    def body(p_ref, hxb, hcg, hxc, dxo_ref, ya_ref, yb_ref, yc_ref, cv_ref, bsb_ref, wcat_ref, wcatt_ref,
             wpool_ref, band_ref, bandt_ref, wpa_ref, wpb_ref, wpc_ref, wo_ref,
             dp_ref, acts_ref, mrg_ref, dys_ref, gwc_ref, gbs_ref, gwpool_ref, gvec_ref,
             extb, extdy, cdy, cq, bsacc):
        i = pl.program_id(0)
        blk = nb - 1 - i

        @pl.when(i == 0)
        def _():
            for ref in (gwc_ref, gwpool_ref, gvec_ref, cdy, cq, bsacc):
                _zero(ref)

        cv = cv_ref[...]
        r = _mixers(p_ref, hxb, hcg, hxc, cv, bsb_ref, wcat_ref, wpool_ref, band_ref, extb,
                    blk == 0, blk, R, True)

        dxo_b = dxo_ref[...].astype(BF16)
        dm = _dot_nt(dxo_b, wo_ref[...]).astype(BF16)
        ys = [ya_ref[...], yb_ref[...], yc_ref[...]]
        sig = [_sigmoid(p_ref[:, O_G + k * D:O_G + (k + 1) * D]) for k in range(3)]
        mrg_ref[...] = sig[0] * ys[0] + sig[1] * ys[1] + sig[2] * ys[2]
        dacts = []
        for k, (act, w_ref) in enumerate(((r["a"], wpa_ref), (r["b"], wpb_ref), (r["c"], wpc_ref))):
            dyk = dm * sig[k]
            dp_ref[:, O_G + k * D:O_G + (k + 1) * D] = dyk * ys[k] * (1.0 - sig[k])
            acts_ref[:, k * SEG:(k + 1) * SEG] = act.astype(BF16)
            dys_ref[:, k * D:(k + 1) * D] = dyk
            dacts.append(_dot_nt(dyk, w_ref[...]))
        da, db, dc = dacts

        def silu_bwd(dact, pre, z, s):
            return dact * (z * s), dact * pre * (s * (1.0 + z * (1.0 - s)))

        d_aout, dza = silu_bwd(da, r["a_out"], r["za"], r["sa"])
        dp_ref[:, O_ZA:O_ZA + SEG] = dza.astype(BF16)
        dp_ref[:, O_U:O_U + SEG] = (d_aout * r["sg"] * _gelu_grad(r["u"], r["tu"])).astype(BF16)
        d_sg = d_aout * r["ug"]
        dvns = []
        for ci in range(R // CHUNK):
            dsc = d_sg[ci * CHUNK:(ci + 1) * CHUNK]
            bsacc[...] += dsc
            dsc_b = dsc.astype(BF16)
            dst = jnp.concatenate([jnp.where(r["lane_group"] == g, dsc_b, r["zero_b"]) for g in range(GROUPS)], axis=0)
            dvns.append(_dot(wcatt_ref[...], dst))
            gwc_ref[...] += _dot_nt(dst, r["vn"][ci * CHUNK:(ci + 1) * CHUNK])
        d_vn = dvns[0] if len(dvns) == 1 else jnp.concatenate(dvns, axis=0)
        vhat = r["vhat"]
        gvec_ref[V_LNG:V_LNG + 1, :] += jnp.sum(d_vn * vhat, axis=0, keepdims=True)
        gvec_ref[V_LNB:V_LNB + 1, :] += jnp.sum(d_vn, axis=0, keepdims=True)
        d_vhat = d_vn * cv[C_LNG:C_LNG + 1, :]
        d_vg = r["rs"] * (d_vhat - jnp.mean(d_vhat, axis=-1, keepdims=True)
                          - vhat * jnp.mean(d_vhat * vhat, axis=-1, keepdims=True))
        dp_ref[:, O_V:O_V + SEG] = (d_vg * _gelu_grad(r["v"], r["tv"])).astype(BF16)

        d_bout, dzb = silu_bwd(db, r["b_out"], r["zb"], r["sb"])
        dp_ref[:, O_ZB:O_ZB + SEG] = dzb.astype(BF16)
        dp_ref[:, O_BG:O_BG + SEG] = (d_bout * r["yconv"]).astype(BF16)
        d_y = d_bout * r["bg"]
        gvec_ref[V_CB:V_CB + 1, :] += jnp.sum(d_y, axis=0, keepdims=True)
        gvec_ref[V_CW0:V_CW0 + 1, :] += jnp.sum(d_y * r["cx2"], axis=0, keepdims=True)
        gvec_ref[V_CW1:V_CW1 + 1, :] += jnp.sum(d_y * r["cx1"], axis=0, keepdims=True)
        gvec_ref[V_CW2:V_CW2 + 1, :] += jnp.sum(d_y * r["cx"], axis=0, keepdims=True)
        extdy[0:R, :] = d_y
        extdy[R:R + HALO, :] = cdy[...]
        d_cx = (cv[C_CW2:C_CW2 + 1, :] * d_y + cv[C_CW1:C_CW1 + 1, :] * extdy[pl.ds(1, R), :]
                + cv[C_CW0:C_CW0 + 1, :] * extdy[pl.ds(2, R), :])
        cdy[...] = d_y[0:HALO]
        dp_ref[:, O_CG:O_CG + SEG] = (d_cx * r["xb"]).astype(BF16)
        dp_ref[:, O_XB:O_XB + SEG] = (d_cx * r["cg"]).astype(BF16)

        d_cout, dzc = silu_bwd(dc, r["c_out"], r["zc"], r["sc"])
        dp_ref[:, O_ZC:O_ZC + SEG] = dzc.astype(BF16)
        gvec_ref[V_PS:V_PS + 1, :] += jnp.sum(d_cout * r["pw"], axis=0, keepdims=True)
        d_pw = (d_cout * cv[C_PS:C_PS + 1, :]).astype(BF16)
        dpool, scaled = [], []
        for gi, w in enumerate(POOL_WINDOWS):
            lo = gi * POOL_GROUP
            dpw_g = d_pw[:, lo:lo + POOL_GROUP]
            gwpool_ref[lo:lo + POOL_GROUP, :] += _dot_tn(r["pooled"][gi], dpw_g)
            dpg = _dot_nt(dpw_g, wpool_ref[gi])
            dpool.append(dpg)
            scaled.append((dpg * r["invs"][gi]).astype(BF16))
        q = jnp.concatenate(scaled, axis=1)
        q_ext = jnp.concatenate([q, cq[...], jnp.zeros((CHUNK - HALO, SEG), BF16)], axis=0)
        for gi, w in enumerate(POOL_WINDOWS):
            lo = gi * POOL_GROUP
            acc = _dot(bandt_ref[gi], q_ext[:, lo:lo + POOL_GROUP])
            dp_ref[:, O_XC + lo:O_XC + lo + POOL_GROUP] = (acc - dpool[gi]).astype(BF16)
        cq[...] = q[0:HALO]

        @pl.when(i == nb - 1)
        def _():
            rr = lax.broadcasted_iota(jnp.int32, gwc_ref.shape, 0) % CHUNK
            cc = lax.broadcasted_iota(jnp.int32, gwc_ref.shape, 1)
            gwc_ref[...] = jnp.where(cc <= rr, gwc_ref[...], 0.0)
            acc = bsacc[...]
            hi = acc.astype(BF16)
            lo_ = (acc - hi.astype(F32)).astype(BF16)
            sel = (lax.broadcasted_iota(jnp.int32, (SEG, LANES), 0) // HEAD
                   == lax.broadcasted_iota(jnp.int32, (SEG, LANES), 1)).astype(BF16)
            gbs_ref[...] = _dot(hi, sel) + _dot(lo_, sel)

    row = lambda w: pl.BlockSpec((R, w), lambda i: (nb - 1 - i, 0))
    consts = [lw["cvec"], lw["bsb"], lw["wcat"], lw["wcatt"], lw["wpool"], _pool_bands(R, False), _pool_bands(R, True),
              lw["wpa"], lw["wpb"], lw["wpc"], lw["wo"]]
    acc_shapes = [(GROUPS * CHUNK, CHUNK), (CHUNK, LANES), (SEG, POOL_GROUP), (8, SEG)]
    row_widths = [N, 3 * SEG, D, 3 * D]
    return _host_call(
        body, name, (nb,), riders,
        inputs=[p, p, p, p, dxo, ya, yb, yc, *consts],
        in_specs=([row(N)] + _halo_specs(R, nb, True) + [row(D), row(D), row(D), row(D)]
                  + [_const_spec(c.shape) for c in consts]),
        out_specs=[row(w) for w in row_widths] + [_const_spec(s) for s in acc_shapes],
        out_shape=[SDS((T, w), BF16) for w in row_widths] + [SDS(s, F32) for s in acc_shapes],
        scratch_shapes=[pltpu.VMEM((HALO + R, SEG), F32)] * 2
        + [pltpu.VMEM((HALO, SEG), F32), pltpu.VMEM((HALO, SEG), BF16), pltpu.VMEM((CHUNK, SEG), F32)])


def _proj_wgrad(acts, merged, dys, dxo, name, riders=()):
    T, D = dxo.shape
    bk = _pick(T, WGRAD_ROWS // 2, 16)

    def body(a_ref, m_ref, dy_ref, dxo_ref, gwpa_ref, gwpb_ref, gwpc_ref, gwo_ref):
        @pl.when(pl.program_id(0) == 0)
        def _():
            for ref in (gwpa_ref, gwpb_ref, gwpc_ref, gwo_ref):
                _zero(ref)

        gwo_ref[...] += _dot_tn(m_ref[...], dxo_ref[...].astype(BF16))
        for k, ref in enumerate((gwpa_ref, gwpb_ref, gwpc_ref)):
            ref[...] += _dot_tn(a_ref[:, k * SEG:(k + 1) * SEG], dy_ref[:, k * D:(k + 1) * D])

    row = lambda w: pl.BlockSpec((bk, w), lambda i: (i, 0))
    shapes = [(SEG, D), (SEG, D), (SEG, D), (D, D)]
    return _host_call(
        body, name, (T // bk,), riders,
        inputs=[acts, merged, dys, dxo],
        in_specs=[row(3 * SEG), row(D), row(3 * D), row(D)],
        out_specs=[_const_spec(s) for s in shapes], out_shape=[SDS(s, F32) for s in shapes],
        scratch_shapes=[])


def _inproj_token_blocks(T):
    return T // _pick(T, BWD_X_ROWS, 16)


def _inproj_bwd_x(dp, win_t, x, norm_g, dxo, name, riders=(), blocks=None, fill=None):
    T, D = x.shape
    N = dp.shape[1]
    bt = _pick(T, BWD_X_ROWS, 16)
    b0, nblk = blocks if blocks else (0, T // bt)

    def compute(dp_ref, w_ref, x_ref, g_ref, dxo_ref, *rest):
        dx_ref, dg_ref = rest[-2:]

        @pl.when(pl.program_id(0) == 0)
        def _():
            _zero(dg_ref)

        dh = _dot(dp_ref[...], w_ref[...])
        xv = x_ref[...]
        rstd = lax.rsqrt(jnp.mean(xv * xv, axis=-1, keepdims=True) + RMS_EPS)
        xhat = xv * rstd
        dg_ref[0:1, :] += jnp.sum(dh * xhat, axis=0, keepdims=True)
        dxn = dh * g_ref[...]
        dx_ref[...] = dxo_ref[...] + rstd * (dxn - xhat * jnp.mean(dxn * xhat, axis=-1, keepdims=True))

    rows = pl.BlockSpec((bt, D), lambda i: (i + b0, 0))
    return _host_call(
        compute, name, (nblk,), riders,
        inputs=[dp, win_t, x, norm_g.reshape(1, D), dxo] + ([] if fill is None else [fill]),
        in_specs=[pl.BlockSpec((bt, N), lambda i: (i + b0, 0)), _const_spec((N, D)), rows, _const_spec((1, D)), rows]
        + ([] if fill is None else [_ANY]),
        out_specs=[rows, _const_spec((8, D))],
        out_shape=[SDS((T, D), F32), SDS((8, D), F32)],
        scratch_shapes=[],
        aliases={} if fill is None else {5: 0})


def _inproj_bwd_w(dp, h, name, riders=()):
    T, N = dp.shape
    D = h.shape[1]
    bn = _pick(N, 1536, LANES)
    bk = _pick(T, WGRAD_ROWS, 16)
    nk = T // bk

    def compute(dp_ref, h_ref, o_ref):
        @pl.when(pl.program_id(1) == 0)
        def _():
            _zero(o_ref)

        o_ref[...] += _dot_tn(dp_ref[...], h_ref[...])

    return _host_call(
        compute, name, (N // bn, nk), riders,
        inputs=[dp, h],
        in_specs=[pl.BlockSpec((bk, bn), lambda j, k: (k, j)), pl.BlockSpec((bk, D), lambda j, k: (k, 0))],
        out_specs=[pl.BlockSpec((bn, D), lambda j, k: (j, 0))],
        out_shape=[SDS((N, D), F32)],
        scratch_shapes=[])


def _chip_peer(x, y, j):
    px = (1 - x) if (j >> 1) else x
    py = (1 - y) if (j & 1) else y
    return px, py


def _blk(ref, kind, k, n):
    if kind == "rows":
        return ref.at[pl.ds(pl.multiple_of(k * n, 8), n)]
    return ref.at[:, pl.ds(pl.multiple_of(k * n, LANES), n)]


class _Exchange:
    def __init__(self, srcs, out_shapes, n_sems, build, alias=None):
        self.srcs, self.out_shapes, self.n_sems, self.build = list(srcs), list(out_shapes), n_sems, build
        self.alias = dict(alias or {})


def _rider_aliases(riders, first_in, first_out):
    out, i, o = {}, first_in, first_out
    for e in riders:
        out.update({i + s: o + d for s, d in e.alias.items()})
        i, o = i + len(e.srcs), o + len(e.out_shapes)
    return out


def _rider_plan(riders):
    inputs = [s for e in riders for s in e.srcs]
    out_shapes = [o for e in riders for o in e.out_shapes]
    sems = [pltpu.SemaphoreType.DMA((e.n_sems,)) for e in riders for _ in range(2)]

    def copies(in_refs, out_refs, sem_refs):
        cps, i, o = [], 0, 0
        for k, e in enumerate(riders):
            ni, no = len(e.srcs), len(e.out_shapes)
            cps += e.build(in_refs[i:i + ni], out_refs[o:o + no], sem_refs[2 * k], sem_refs[2 * k + 1])
            i, o = i + ni, o + no
        return cps

    return inputs, out_shapes, sems, copies


_ANY = pl.BlockSpec(memory_space=pl.ANY)


def _host_call(compute, name, grid, riders, inputs, in_specs, out_specs, out_shape, scratch_shapes, aliases=None):
    r_in, r_out, r_sems, copies = _rider_plan(riders)
    ni, no, ns = len(inputs), len(out_shape), len(scratch_shapes)

    def body(*refs):
        ins, rins = refs[:ni], refs[ni:ni + len(r_in)]
        outs = refs[ni + len(r_in):ni + len(r_in) + no]
        routs = refs[ni + len(r_in) + no:ni + len(r_in) + no + len(r_out)]
        scr = refs[ni + len(r_in) + no + len(r_out):]
        first = functools.reduce(lambda a, b: a & b, [pl.program_id(d) == 0 for d in range(len(grid))])
        last = functools.reduce(lambda a, b: a & b, [pl.program_id(d) == grid[d] - 1 for d in range(len(grid))])
        if riders:
            @pl.when(first)
            def _():
                for cp in copies(rins, routs, scr[ns:]):
                    cp.start()

        compute(*ins, *outs, *scr[:ns])

        if riders:
            @pl.when(last)
            def _():
                for cp in copies(rins, routs, scr[ns:]):
                    cp.wait()

    res = pl.pallas_call(
        body, name=name, grid=grid,
        in_specs=list(in_specs) + [_ANY] * len(r_in),
        out_specs=list(out_specs) + [_ANY] * len(r_out),
        out_shape=list(out_shape) + r_out,
        scratch_shapes=list(scratch_shapes) + r_sems,
        input_output_aliases={**(aliases or {}), **_rider_aliases(riders, ni, no)},
        compiler_params=_cparams(("arbitrary",) * len(grid)),
    )(*inputs, *r_in)
    return res[:no], _split_riders(riders, res[no:])


def _split_riders(riders, flat):
    out, o = [], 0
    for e in riders:
        out.append(list(flat[o:o + len(e.out_shapes)]))
        o += len(e.out_shapes)
    return out


def _run_exchange(ex, name):
    n_in, n_out = len(ex.srcs), len(ex.out_shapes)

    def body(*refs):
        cps = ex.build(refs[:n_in], refs[n_in:n_in + n_out], refs[n_in + n_out], refs[n_in + n_out + 1])
        for cp in cps:
            cp.start()
        for cp in cps:
            cp.wait()

    return pl.pallas_call(
        body, name=name,
        in_specs=[_ANY] * n_in, out_specs=[_ANY] * n_out, out_shape=ex.out_shapes,
        input_output_aliases=ex.alias,
        scratch_shapes=[pltpu.SemaphoreType.DMA((ex.n_sems,)), pltpu.SemaphoreType.DMA((ex.n_sems,))],
        compiler_params=pltpu.CompilerParams(has_side_effects=True),
    )(*ex.srcs)


def _gather_sizes(shards, kinds):
    sizes = [s.shape[0] if k == "rows" else s.shape[1] for s, k in zip(shards, kinds)]
    fulls = [SDS((s.shape[0] * N_DEV,) + s.shape[1:], s.dtype) if k == "rows"
             else SDS((s.shape[0], s.shape[1] * N_DEV), s.dtype) for s, k in zip(shards, kinds)]
    return sizes, fulls


def _gather_direct(shards, kinds):
    n = len(shards)
    sizes, fulls = _gather_sizes(shards, kinds)

    def build(ins, outs, send_sems, recv_sems):
        x, y, c = _me()
        cps = []
        for a in range(n):
            mine = _blk(outs[a], kinds[a], 4 * x + 2 * y + c, sizes[a])
            cps.append(pltpu.make_async_copy(ins[a], mine, send_sems.at[5 * a + 4]))
            for j in range(N_CHIP):
                to = (x, y, 1 - c) if j == 0 else (*_chip_peer(x, y, j), c)
                cps.append(pltpu.make_async_remote_copy(
                    src_ref=ins[a], dst_ref=mine, send_sem=send_sems.at[5 * a + j], recv_sem=recv_sems.at[5 * a + j],
                    device_id=to, device_id_type=MESH))
        return cps

    return _Exchange(shards, fulls, 5 * n, build)


def _gather_everywhere(shards, kinds):
    n = len(shards)
    sizes, fulls = _gather_sizes(shards, kinds)

    def build(ins, outs, send_sems, recv_sems):
        x, y, c = _me()
        cps = []
        for a in range(n):
            mine = _blk(outs[a], kinds[a], 4 * x + 2 * y + c, sizes[a])
            cps.append(pltpu.make_async_copy(ins[a], mine, send_sems.at[N_DEV * a]))
            for d in range(1, N_DEV):
                to = ((1 - x) if d & 4 else x, (1 - y) if d & 2 else y, (1 - c) if d & 1 else c)
                cps.append(pltpu.make_async_remote_copy(
                    src_ref=ins[a], dst_ref=mine, send_sem=send_sems.at[N_DEV * a + d],
                    recv_sem=recv_sems.at[N_DEV * a + d], device_id=to, device_id_type=MESH))
        return cps

    return _Exchange(shards, fulls, N_DEV * n, build)


def _gather_forward(fulls, kinds, sizes):
    n = len(fulls)

    def build(ins, outs, send_sems, recv_sems):
        x, y, c = _me()
        cps = []
        for a in range(n):
            for j in (1, 2, 3):
                px, py = _chip_peer(x, y, j)
                k = 4 * px + 2 * py + c
                cps.append(pltpu.make_async_remote_copy(
                    src_ref=_blk(ins[a], kinds[a], k, sizes[a]), dst_ref=_blk(outs[a], kinds[a], k, sizes[a]),
                    send_sem=send_sems.at[3 * a + j - 1], recv_sem=recv_sems.at[3 * a + j - 1],
                    device_id=(x, y, 1 - c), device_id_type=MESH))
        return cps

    return _Exchange(fulls, [SDS(f.shape, f.dtype) for f in fulls], 3 * n, build, alias={a: a for a in range(n)})


def _sibling_exchange(grads, kinds, sizes):
    n = len(grads)

    def blk_shape(a):
        g = grads[a]
        return (sizes[a],) + g.shape[1:] if kinds[a] == "rows" else (g.shape[0], sizes[a])

    def build(ins, outs, send_sems, recv_sems):
        x, y, c = _me()
        cps = []
        for a in range(n):
            for q in range(N_CHIP):
                cps.append(pltpu.make_async_remote_copy(
                    src_ref=_blk(ins[a], kinds[a], 2 * q + (1 - c), sizes[a]), dst_ref=outs[a].at[q],
                    send_sem=send_sems.at[N_CHIP * a + q], recv_sem=recv_sems.at[N_CHIP * a + q],
                    device_id=(x, y, 1 - c), device_id_type=MESH))
        return cps

    return _Exchange(grads, [SDS((N_CHIP,) + blk_shape(a), F32) for a in range(n)], N_CHIP * n, build)


def _chip_partial(g, r1, kind, size, core_chip, name):
    if kind == "rows":
        rows, cols = size, g.shape[1]
        g3 = g.reshape(N_DEV, rows, cols)
        rb = _pick(rows, 512, 16)
        g_spec = pl.BlockSpec((1, rb, cols), lambda j, q, s: (2 * q + s[0], j, 0))
        grid = (rows // rb, N_CHIP)
        blk = (1, rb, cols)
        imap = lambda j, q, s: (q, j, 0)
        own_spec = pl.BlockSpec((rb, cols), lambda j, q, s: (j, 0))
    else:
        rows, cols = g.shape[0], size
        g3 = g
        g_spec = pl.BlockSpec((rows, cols), lambda j, q, s: (0, 2 * q + s[0]))
        grid = (1, N_CHIP)
        blk = (1, rows, cols)
        imap = lambda j, q, s: (q, 0, 0)
        own_spec = pl.BlockSpec((rows, cols), lambda j, q, s: (0, 0))

    def body(s_ref, g_ref, r_ref, own_ref, pb_ref):
        t = g_ref[...].reshape(blk) + r_ref[...]
        pb_ref[...] = t.astype(BF16)

        @pl.when(pl.program_id(1) == s_ref[1])
        def _():
            own_ref[...] = t[0]

    return pl.pallas_call(
        body, name=name,
        grid_spec=pltpu.PrefetchScalarGridSpec(
            num_scalar_prefetch=1, grid=grid,
            in_specs=[g_spec, pl.BlockSpec(blk, imap)],
            out_specs=[own_spec, pl.BlockSpec(blk, imap)]),
        out_shape=[SDS((rows, cols), F32), SDS((N_CHIP, rows, cols), BF16)],
        compiler_params=_cparams(("arbitrary", "arbitrary")),
    )(core_chip, g3, r1)


def _chip_exchange(parts):
    n = len(parts)
    m = N_CHIP - 1

    def build(ins, outs, send_sems, recv_sems):
        x, y, c = _me()
        cps = []
        for a in range(n):
            for j in (1, 2, 3):
                px, py = _chip_peer(x, y, j)
                cps.append(pltpu.make_async_remote_copy(
                    src_ref=ins[a].at[2 * px + py], dst_ref=outs[a].at[j - 1], send_sem=send_sems.at[m * a + j - 1],
                    recv_sem=recv_sems.at[m * a + j - 1], device_id=(px, py, c), device_id_type=MESH))
        return cps

    return _Exchange(parts, [SDS((m,) + p.shape[1:], BF16) for p in parts], m * n, build)


def _grad_total(own, r2, name):
    rows, cols = own.shape
    rb = _pick(rows, 512, 16)

    def body(p_ref, r_ref, o_ref):
        s = p_ref[...]
        for j in range(N_CHIP - 1):
            s = s + r_ref[j].astype(F32)
        o_ref[...] = s

    return pl.pallas_call(
        body, name=name, grid=(rows // rb,),
        in_specs=[pl.BlockSpec((rb, cols), lambda i: (i, 0)), pl.BlockSpec((N_CHIP - 1, rb, cols), lambda i: (0, i, 0))],
        out_specs=pl.BlockSpec((rb, cols), lambda i: (i, 0)),
        out_shape=SDS((rows, cols), F32),
        compiler_params=_cparams(("arbitrary",)),
    )(own, r2)


def _all_reduce_small(pack, name):
    rows = pack.shape[0]
    rs = rows // N_DEV
    assert rs * N_DEV == rows and rs % 8 == 0

    def body(x_ref, o_ref, rbuf, red, send1, recv1, send2, recv2):
        x, y, c = _me()
        me = 4 * x + 2 * y + c

        def peer(d):
            px = (1 - x) if (d >> 2) & 1 else x
            py = (1 - y) if (d >> 1) & 1 else y
            pc = (1 - c) if d & 1 else c
            return px, py, pc

        def sl(ref, k):
            return ref.at[pl.ds(pl.multiple_of(k * rs, 8), rs)]

        phase1 = []
        for d in range(1, N_DEV):
            px, py, pc = peer(d)
            phase1.append(pltpu.make_async_remote_copy(
                src_ref=sl(x_ref, 4 * px + 2 * py + pc), dst_ref=rbuf.at[d], send_sem=send1.at[d], recv_sem=recv1.at[d],
                device_id=(px, py, pc), device_id_type=MESH))
        for cp in phase1:
            cp.start()
        acc = sl(x_ref, me)[...]
        for cp in phase1:
            cp.wait()
        for d in range(1, N_DEV):
            acc = acc + rbuf[d]
        red[...] = acc
        sl(o_ref, me)[...] = acc
        phase2 = []
        for d in range(1, N_DEV):
            px, py, pc = peer(d)
            phase2.append(pltpu.make_async_remote_copy(
                src_ref=red, dst_ref=sl(o_ref, me), send_sem=send2.at[d], recv_sem=recv2.at[d],
                device_id=(px, py, pc), device_id_type=MESH))
        for cp in phase2:
            cp.start()
        for cp in phase2:
            cp.wait()

    vm = pl.BlockSpec(memory_space=pltpu.VMEM)
    return pl.pallas_call(
        body, name=name, in_specs=[vm], out_specs=vm, out_shape=SDS(pack.shape, F32),
        scratch_shapes=[pltpu.VMEM((N_DEV, rs, LANES), F32), pltpu.VMEM((rs, LANES), F32),
                        pltpu.SemaphoreType.DMA((N_DEV,)), pltpu.SemaphoreType.DMA((N_DEV,)),
                        pltpu.SemaphoreType.DMA((N_DEV,)), pltpu.SemaphoreType.DMA((N_DEV,))],
        compiler_params=_cparams(None, has_side_effects=True),
    )(pack)


def _adamw(w, g, m, v, name):
    rows, cols = w.shape
    rb = rows if rows * cols * 4 <= ADAMW_WHOLE_BYTES else _pick(rows, 256, 8)
    c1 = 1.0 / (1.0 - ADAM_B1 ** ADAM_STEP)
    c2 = 1.0 / (1.0 - ADAM_B2 ** ADAM_STEP)

    def body(w_ref, g_ref, m_ref, v_ref, d_ref, mo_ref, vo_ref):
        gv = g_ref[...]
        mn = ADAM_B1 * m_ref[...] + (1.0 - ADAM_B1) * gv
        vn = ADAM_B2 * v_ref[...] + (1.0 - ADAM_B2) * (gv * gv)
        mo_ref[...] = mn
        vo_ref[...] = vn
        d_ref[...] = -ADAM_LR * ((mn * c1) / (jnp.sqrt(vn * c2) + ADAM_EPS) + ADAM_WD * w_ref[...])

    spec = pl.BlockSpec((rb, cols), lambda i: (i, 0))
    return pl.pallas_call(
        body, name=name, grid=(rows // rb,),
        in_specs=[spec] * 4, out_specs=[spec] * 3, out_shape=[SDS((rows, cols), F32)] * 3,
        compiler_params=_cparams(("arbitrary",)),
    )(w, g, m, v)


def _pad_rows(a, mult=8):
    r = (-a.shape[0]) % mult
    return a if r == 0 else jnp.pad(a, ((0, r), (0, 0)))


def _as_lanes(a):
    flat = a.reshape(-1)
    pad = (-flat.shape[0]) % (8 * LANES)
    if pad:
        flat = jnp.pad(flat, (0, pad))
    return flat.reshape(-1, LANES)


def kernel(x, norm_g, w_in, ln_g, ln_b, w_s, b_s, conv_w, conv_b, w_pool, pool_scale, w_pa, w_pb, w_pc, w_o, final_g, loss_target, m_norm_g, m_w_in, m_ln_g, m_ln_b, m_w_s, m_b_s, m_conv_w, m_conv_b, m_w_pool, m_pool_scale, m_w_pa, m_w_pb, m_w_pc, m_w_o, m_final_g, v_norm_g, v_w_in, v_ln_g, v_ln_b, v_w_s, v_b_s, v_conv_w, v_conv_b, v_w_pool, v_pool_scale, v_w_pa, v_w_pb, v_w_pc, v_w_o, v_final_g):
    L = w_in.shape[0]
    D = x.shape[-1]
    n_loc = w_in.shape[2]
    pc_loc = w_pa.shape[2]
    x0 = x[0]
    target = loss_target[0]
    xi, yi, ci = _me()
    core_chip = jnp.stack([ci, 2 * xi + yi]).astype(jnp.int32)

    kinds5 = ["rows", "cols", "cols", "cols", "rows"]

    def layer_shards(l):
        return [w_in[l].T.astype(BF16), w_pa[l].astype(BF16), w_pb[l].astype(BF16), w_pc[l].astype(BF16),
                w_o[l].astype(BF16)]

    def gathered(direct, shards, kinds, l):
        sizes, _ = _gather_sizes(shards, kinds)
        return _run_exchange(_gather_forward(direct, kinds, sizes), f"weights_forward_{l}")

    cw_loc = _pad_rows(conv_w.reshape(L * CONV_TAPS, -1))
    cw_loc = jnp.pad(cw_loc, ((0, 0), (0, LANES - cw_loc.shape[1])))
    causal = jnp.tril(jnp.ones((CHUNK, CHUNK), dtype=bool))

    chip_order = jnp.stack([2 * xi + yi] + [2 * px + py for px, py in (_chip_peer(xi, yi, j) for j in (1, 2, 3))])
    sh0 = layer_shards(0)
    rest0, krest0 = sh0[1:] + [cw_loc], kinds5[1:] + ["rows"]
    (p0, h0, win_t0), delivered = _inproj_gathering(x0, norm_g[0], sh0[0], chip_order.astype(jnp.int32), "inproj_fwd_0",
                                                    [_gather_everywhere(rest0, krest0)])
    rest0_full = delivered[0]
    cw_all = rest0_full[-1].reshape(N_DEV, -1, LANES)[:, :L * CONV_TAPS, :conv_w.shape[2]]
    conv_w_full = jnp.transpose(cw_all, (1, 0, 2)).reshape(L, CONV_TAPS, -1)

    def make_layer(l, full5):
        win_t, wpa, wpb, wpc, wo = full5
        wm = jnp.where(causal, w_s[l], 0.0)
        cvec = jnp.concatenate([ln_g[l][None], ln_b[l][None], conv_w_full[l], conv_b[l][None], pool_scale[l][None],
                                jnp.zeros((C_ROWS - 7, SEG), F32)], axis=0)
        return dict(
            win_t=win_t, wpa=wpa, wpb=wpb, wpc=wpc, wo=wo, cvec=cvec,
            bsb=jnp.repeat(b_s[l].T, HEAD, axis=1),
            wcat=jnp.transpose(wm, (1, 0, 2)).reshape(CHUNK, GROUPS * CHUNK).astype(BF16),
            wcatt=jnp.transpose(wm, (2, 0, 1)).reshape(CHUNK, GROUPS * CHUNK).astype(BF16),
            wpool=w_pool[l].astype(BF16))

    layers, xs, saved, win_next = [], [x0], [], None
    for l in range(L):
        if l == 0:
            p, h = p0, h0
            lw = make_layer(0, [win_t0] + list(rest0_full[:4]))
        else:
            (p, h), delivered = _inproj(xs[-1], norm_g[l], win_next, f"inproj_fwd_{l}",
                                        [_gather_everywhere(layer_shards(l)[1:], kinds5[1:])])
            lw = make_layer(l, [win_next] + delivered[0])
        layers.append(lw)
        nxt = layer_shards(l + 1)[:1] if l + 1 < L else None
        (xn, ya, yb, yc), delivered = _mix_fwd(p, xs[-1], lw, f"mix_fwd_{l}", [_gather_direct(nxt, kinds5[:1])] if nxt else [])
        if nxt:
            win_next = gathered(delivered[0], nxt, kinds5[:1], l + 1)[0]
        saved.append((p, h, ya, yb, yc))
        xs.append(xn)
    dx, loss_acc, dfg_acc = _loss_head(xs[-1], final_g, target, "loss_head")

    rs_sizes = [n_loc, pc_loc, pc_loc, pc_loc, w_o.shape[1]]
    await_sibling, await_chips = [], []
    partial_of, from_chips = {}, {}
    serial = [0]

    def riders_now():
        riders, plan = [], []
        for grp in await_chips:
            riders.append(_chip_exchange([partial_of[t][1] for t, _, _, _ in grp]))
            plan.append(("chips", grp))
        for grp in await_sibling:
            riders.append(_sibling_exchange([g for _, g, _, _ in grp], [k for _, _, k, _ in grp], [s for _, _, _, s in grp]))
            plan.append(("sibling", grp))
        del await_chips[:], await_sibling[:]
        return riders, plan

    def absorb(plan, delivered):
        for (what, grp), res in zip(plan, delivered):
            for (t, g, k, s), r in zip(grp, res):
                if what == "chips":
                    from_chips[t] = r
                else:
                    partial_of[t] = _chip_partial(g, r, k, s, core_chip, f"grad_chip_partial_{t[0]}_{t[1]}")
            if what == "sibling":
                await_chips.append(grp)

    small = [None] * L
    for l in reversed(range(L)):
        lw = layers[l]
        p, h, ya, yb, yc = saved[l]
        riders, plan = riders_now()
        (dp, acts, merged, dys, gwc, gbs, gwpool, gvec), delivered = _mix_bwd(p, dx, ya, yb, yc, lw, f"mix_bwd_{l}", riders)
        absorb(plan, delivered)
        (gwpa, gwpb, gwpc, gwo), _ = _proj_wgrad(acts, merged, dys, dx, f"proj_wgrad_{l}")
        await_sibling.append([((l, a), g, kinds5[a], rs_sizes[a]) for a, g in ((1, gwpa), (2, gwpb), (3, gwpc), (4, gwo))])

        def bwd_x(dxo, pieces):
            nt = _inproj_token_blocks(dxo.shape[0])
            pieces = min(pieces, nt)
            done, dng, b0 = None, None, 0
            for k in range(pieces):
                cnt = (nt - b0) if k == pieces - 1 else max(1, (nt * LAST_PIECE_SHARE[0]) // (LAST_PIECE_SHARE[1] * (pieces - 1)))
                riders, plan = riders_now()
                (done, dng_k), delivered = _inproj_bwd_x(dp, lw["win_t"], xs[l], norm_g[l], dxo, f"inproj_bwd_x_{l}_{k}",
                                                         riders, blocks=(b0, cnt), fill=done)
                absorb(plan, delivered)
                dng = dng_k if dng is None else dng + dng_k
                b0 += cnt
            return done, dng

        def bwd_w():
            riders, plan = riders_now()
            (gwin_t,), delivered = _inproj_bwd_w(dp, h, f"inproj_bwd_w_{l}", riders)
            absorb(plan, delivered)
            await_sibling.append([((l, 0), gwin_t, kinds5[0], rs_sizes[0])])

        if l == L - 1:
            dx, dng = bwd_x(dx, 1)
            bwd_w()
        else:
            bwd_w()
            dx, dng = bwd_x(dx, 2 if l == 0 else 1)
        small[l] = dict(norm_g=dng[0], ln_g=gvec[V_LNG], ln_b=gvec[V_LNB], w_s=gwc, b_s=gbs, conv_w=gvec[V_CW0:V_CW0 + 3],
                        conv_b=gvec[V_CB], w_pool=gwpool, pool_scale=gvec[V_PS])
    while await_sibling or await_chips:
        riders, plan = riders_now()
        delivered = []
        for ex in riders:
            delivered.append(_run_exchange(ex, f"grad_exchange_tail_{serial[0]}"))
            serial[0] += 1
        absorb(plan, delivered)
    grad_x = dx[None]
    big_grads = []
    for l in range(L):
        tot = [_grad_total(partial_of[(l, a)][0], from_chips[(l, a)], f"grad_total_{l}_{a}") for a in range(5)]
        big_grads.append([tot[0],
                          tot[1].reshape(SEG, pc_loc), tot[2].reshape(SEG, pc_loc), tot[3].reshape(SEG, pc_loc),
                          tot[4]])

    names = ["norm_g", "ln_g", "ln_b", "w_s", "b_s", "conv_w", "conv_b", "w_pool", "pool_scale"]
    pieces = [jnp.stack([small[l][nm] for l in range(L)]).reshape(-1) for nm in names]
    pieces += [dfg_acc[0], loss_acc.reshape(-1)]
    offs = [0]
    for pc in pieces:
        offs.append(offs[-1] + pc.shape[0])
    pack = _pad_rows(_as_lanes(jnp.concatenate(pieces)), 8 * N_DEV)
    red = _all_reduce_small(pack, "small_grads_all_reduce").reshape(-1)

    def unpack(i, shape):
        return red[offs[i]:offs[i] + math.prod(shape)].reshape(shape)

    g_norm_g = unpack(0, (L, D))
    g_ln_g = unpack(1, (L, SEG))
    g_ln_b = unpack(2, (L, SEG))
    g_w_s = unpack(3, (L, GROUPS, CHUNK, CHUNK))
    g_b_s = jnp.transpose(unpack(4, (L, CHUNK, LANES))[:, :, :GROUPS], (0, 2, 1))
    g_conv_w_full = unpack(5, (L, CONV_TAPS, SEG))
    g_conv_b = unpack(6, (L, SEG))
    g_w_pool = unpack(7, (L, len(POOL_WINDOWS), POOL_GROUP, POOL_GROUP))
    g_pool_scale = unpack(8, (L, SEG))
    g_final_g = unpack(9, (D,))
    loss = red[offs[10]]
    dev = 4 * xi + 2 * yi + ci
    g_conv_w = lax.dynamic_slice_in_dim(g_conv_w_full, dev * conv_w.shape[2], conv_w.shape[2], axis=2)

    g_w_in_t = jnp.stack([big_grads[l][0] for l in range(L)])
    g_w_in = jnp.swapaxes(g_w_in_t, 1, 2)
    g_w_pa = jnp.stack([big_grads[l][1] for l in range(L)])
    g_w_pb = jnp.stack([big_grads[l][2] for l in range(L)])
    g_w_pc = jnp.stack([big_grads[l][3] for l in range(L)])
    g_w_o = jnp.stack([big_grads[l][4] for l in range(L)])

    grads = dict(norm_g=g_norm_g, w_in=g_w_in, ln_g=g_ln_g, ln_b=g_ln_b, w_s=g_w_s, b_s=g_b_s, conv_w=g_conv_w,
                 conv_b=g_conv_b, w_pool=g_w_pool, pool_scale=g_pool_scale, w_pa=g_w_pa, w_pb=g_w_pb, w_pc=g_w_pc,
                 w_o=g_w_o, final_g=g_final_g)
    weights = dict(norm_g=norm_g, w_in=w_in, ln_g=ln_g, ln_b=ln_b, w_s=w_s, b_s=b_s, conv_w=conv_w, conv_b=conv_b,
                   w_pool=w_pool, pool_scale=pool_scale, w_pa=w_pa, w_pb=w_pb, w_pc=w_pc, w_o=w_o, final_g=final_g)
    ms = dict(norm_g=m_norm_g, w_in=m_w_in, ln_g=m_ln_g, ln_b=m_ln_b, w_s=m_w_s, b_s=m_b_s, conv_w=m_conv_w,
              conv_b=m_conv_b, w_pool=m_w_pool, pool_scale=m_pool_scale, w_pa=m_w_pa, w_pb=m_w_pb, w_pc=m_w_pc,
              w_o=m_w_o, final_g=m_final_g)
    vs = dict(norm_g=v_norm_g, w_in=v_w_in, ln_g=v_ln_g, ln_b=v_ln_b, w_s=v_w_s, b_s=v_b_s, conv_w=v_conv_w,
              conv_b=v_conv_b, w_pool=v_w_pool, pool_scale=v_pool_scale, w_pa=v_w_pa, w_pb=v_w_pb, w_pc=v_w_pc,
              w_o=v_w_o, final_g=v_final_g)
    order = ["norm_g", "w_in", "ln_g", "ln_b", "w_s", "b_s", "conv_w", "conv_b", "w_pool", "pool_scale", "w_pa", "w_pb",
             "w_pc", "w_o", "final_g"]

    delta, new_m, new_v = {}, {}, {}
    big = ["w_in", "w_pa", "w_pb", "w_pc", "w_o"]
    tr = lambda a: jnp.swapaxes(a, 1, 2)
    for nm in big:
        into = tr if nm == "w_in" else (lambda a: a)
        shp = into(weights[nm]).shape
        two = lambda a: a.reshape(-1, shp[-1])
        g2 = two(g_w_in_t) if nm == "w_in" else two(grads[nm])
        d, mn, vn = _adamw(two(into(weights[nm])), g2, two(into(ms[nm])), two(into(vs[nm])), f"adamw_{nm}")
        delta[nm], new_m[nm], new_v[nm] = (into(a.reshape(shp)) for a in (d, mn, vn))
    rest = [nm for nm in order if nm not in big and nm != "conv_w"] + ["conv_w"]
    cat = lambda src: _as_lanes(jnp.concatenate([src[nm].reshape(-1) for nm in rest]))
    flat = [a.reshape(-1) for a in _adamw(cat(weights), cat(grads), cat(ms), cat(vs), "adamw_small")]
    off = 0
    for nm in rest:
        shp = weights[nm].shape
        n = math.prod(shp)
        delta[nm], new_m[nm], new_v[nm] = (a[off:off + n].reshape(shp) for a in flat)
        off += n

    return (loss, grad_x, *[grads[nm] for nm in order], *[delta[nm] for nm in order],
            *[new_m[nm] for nm in order], *[new_v[nm] for nm in order])
```
